```python
import jax, jax.numpy as jnp
from jax import lax
import numpy as np

D_MODEL = 2048
BATCH = 8
SEQ = 8192
DEPTH = 1

EPS = 1e-6
MEM_LEN = 256

CHUNK = 128
A_GROUP_DIM = 128
A_GROUPS = D_MODEL // A_GROUP_DIM
A_WIDTH = A_GROUPS * A_GROUP_DIM

QK_NOPE = 128
QK_ROPE = 64
V_DIM = 128
MLA_HEADS = D_MODEL // V_DIM
Q_LORA = 512
KV_LORA = 512
MLA_WIDTH = MLA_HEADS * V_DIM
QK_DIM = QK_NOPE + QK_ROPE
ROPE_THETA = 10000.0
Q_BLOCK = 128

MEM_HEADS = 4
MEM_HEAD_DIM = D_MODEL // MEM_HEADS
MEM_WIDTH = MEM_HEADS * MEM_HEAD_DIM

N_BRANCH = 3
BRANCH_WIDTH = D_MODEL

IN_SIZES = (A_WIDTH, A_WIDTH, A_WIDTH,
            Q_LORA, KV_LORA, QK_ROPE, MLA_WIDTH,
            MEM_WIDTH, MEM_WIDTH)
IN_TOTAL = int(sum(IN_SIZES))
IN_SPLITS = [int(o) for o in np.cumsum(IN_SIZES)[:-1]]

kernel_name = "hybrid_gmlp_mla_memory_gated"


def rmsnorm(x, g):
    xf = x.astype(jnp.float32)
    xf = xf * lax.rsqrt(jnp.mean(xf * xf, axis=-1, keepdims=True) + EPS)
    return xf.astype(x.dtype) * g


def layernorm(x, g, b):
    xf = x.astype(jnp.float32)
    mu = jnp.mean(xf, axis=-1, keepdims=True)
    var = jnp.mean(jnp.square(xf - mu), axis=-1, keepdims=True)
    return ((xf - mu) * lax.rsqrt(var + EPS)).astype(x.dtype) * g + b


def rope_tables(positions):
    inv_freq = 1.0 / (ROPE_THETA ** (jnp.arange(0, QK_ROPE, 2, dtype=jnp.float32) / QK_ROPE))
    ang = positions.astype(jnp.float32)[..., None] * inv_freq
    return jnp.cos(ang), jnp.sin(ang)


def apply_rope(t, cos, sin):
    t1, t2 = jnp.split(t, 2, axis=-1)
    cos = cos.astype(t.dtype)
    sin = sin.astype(t.dtype)
    return jnp.concatenate([t1 * cos - t2 * sin, t2 * cos + t1 * sin], axis=-1)


def chunked_spatial_gating(u_raw, v_raw, ln_g, ln_b, w_s, b_s):
    B, S, _ = u_raw.shape
    u = jax.nn.gelu(u_raw)
    v = layernorm(jax.nn.gelu(v_raw), ln_g, ln_b)
    vc = v.reshape(B, S // CHUNK, CHUNK, A_GROUPS, A_GROUP_DIM)
    causal = jnp.tril(jnp.ones((CHUNK, CHUNK), dtype=w_s.dtype))
    ws = w_s * causal[None]
    sv = jnp.einsum('gts,bcsgd->bctgd', ws, vc) + b_s.T[None, None, :, :, None]
    return u * sv.reshape(B, S, A_WIDTH)


def latent_attention(c_q, c_kv, k_rope, cos, sin, q_norm_g, w_uq, kv_norm_g, w_ukv):
    B, S, _ = c_q.shape
    q = (rmsnorm(c_q, q_norm_g) @ w_uq).reshape(B, S, MLA_HEADS, QK_DIM)
    q_nope, q_pe = jnp.split(q, [QK_NOPE], axis=-1)
    q_pe = apply_rope(q_pe, cos[:, :, None, :], sin[:, :, None, :])
    q = jnp.concatenate([q_nope, q_pe], axis=-1)

    kv = (rmsnorm(c_kv, kv_norm_g) @ w_ukv).reshape(B, S, MLA_HEADS, QK_NOPE + V_DIM)
    k_nope, v = jnp.split(kv, [QK_NOPE], axis=-1)
    k_pe = apply_rope(k_rope, cos, sin)
    k_pe = jnp.broadcast_to(k_pe[:, :, None, :], (B, S, MLA_HEADS, QK_ROPE))
    k = jnp.concatenate([k_nope, k_pe], axis=-1)

    qh = q.transpose(0, 2, 1, 3)
    kh = k.transpose(0, 2, 1, 3)
    vh = v.transpose(0, 2, 1, 3)
    n_blk = S // Q_BLOCK
    q_blocks = qh.reshape(B, MLA_HEADS, n_blk, Q_BLOCK, QK_DIM).transpose(2, 0, 1, 3, 4)
    scale = QK_DIM ** -0.5
    key_pos = jnp.arange(S)

    def one_block(args):
        qi, bi = args
        s = jnp.einsum('bhqd,bhkd->bhqk', qi, kh).astype(jnp.float32) * scale
        q_pos = bi * Q_BLOCK + jnp.arange(Q_BLOCK)
        mask = key_pos[None, :] <= q_pos[:, None]
        s = jnp.where(mask[None, None], s, -1e30)
        p = jax.nn.softmax(s, axis=-1).astype(vh.dtype)
        return jnp.einsum('bhqk,bhkd->bhqd', p, vh)

    o = lax.map(one_block, (q_blocks, jnp.arange(n_blk)))
    return o.transpose(1, 0, 3, 2, 4).reshape(B, S, MLA_WIDTH)


def memory_attention(q_m, mem, mem_norm_g, w_mem_kv):
    B, S, _ = q_m.shape
    kv = rmsnorm(mem, mem_norm_g) @ w_mem_kv
    k_m, v_m = jnp.split(kv.reshape(B, MEM_LEN, 2, MEM_HEADS, MEM_HEAD_DIM), 2, axis=2)
    k_m, v_m = k_m[:, :, 0], v_m[:, :, 0]
    q = q_m.reshape(B, S, MEM_HEADS, MEM_HEAD_DIM)
    s = jnp.einsum('bshd,bmhd->bhsm', q, k_m).astype(jnp.float32) * (MEM_HEAD_DIM ** -0.5)
    p = jax.nn.softmax(s, axis=-1).astype(v_m.dtype)
    return jnp.einsum('bhsm,bmhd->bshd', p, v_m).reshape(B, S, MEM_WIDTH)


def hybrid_layer(x, mem, cos, sin, g_pre, w_in, a_ln_g, a_ln_b, a_w_s, a_b_s,
                 q_norm_g, w_uq, kv_norm_g, w_ukv, mem_norm_g, w_mem_kv,
                 w_gate, b_gate, w_branch, w_out, g_post):
    B, S, D = x.shape
    h = rmsnorm(x, g_pre)
    proj = h @ w_in
    u, v, z_a, c_q, c_kv, k_rope, z_b, q_m, z_m = jnp.split(proj, IN_SPLITS, axis=-1)

    y_a = chunked_spatial_gating(u, v, a_ln_g, a_ln_b, a_w_s, a_b_s) * jax.nn.silu(z_a)
    y_b = latent_attention(c_q, c_kv, k_rope, cos, sin, q_norm_g, w_uq, kv_norm_g, w_ukv) * jax.nn.silu(z_b)
    y_m = memory_attention(q_m, mem, mem_norm_g, w_mem_kv) * jax.nn.silu(z_m)

    y = jnp.stack([y_a, y_b, y_m], axis=2)
    p = jnp.einsum('bsnc,ncd->bsnd', y, w_branch)
    gates = jax.nn.sigmoid(h @ w_gate + b_gate).reshape(B, S, N_BRANCH, D)
    merged = jnp.sum(gates * p, axis=2)
    out = merged @ w_out
    return x + rmsnorm(out, g_post)


def _fwd_setup_inputs(seed: int = 0) -> dict:
    key = jax.random.key(seed)
    ks = jax.random.split(key, 24)
    f32 = jnp.float32
    L, D = DEPTH, D_MODEL

    def w(k, shape, fan_in):
        return jax.random.normal(k, shape, f32) * (fan_in ** -0.5)

    def gain(k, shape):
        return 1.0 + 0.02 * jax.random.normal(k, shape, f32)

    def bias(k, shape):
        return 0.01 * jax.random.normal(k, shape, f32)

    return {
        "x": jax.random.normal(ks[0], (BATCH, SEQ, D), f32),
        "mem": jax.random.normal(ks[1], (BATCH, MEM_LEN, D), f32),
        "positions": jnp.broadcast_to(jnp.arange(SEQ, dtype=jnp.int32)[None], (BATCH, SEQ)),
        "g_pre": gain(ks[2], (L, D)),
        "w_in": w(ks[3], (L, D, IN_TOTAL), D),
        "a_ln_g": gain(ks[4], (L, A_WIDTH)),
        "a_ln_b": bias(ks[5], (L, A_WIDTH)),
        "a_w_s": w(ks[6], (L, A_GROUPS, CHUNK, CHUNK), CHUNK),
        "a_b_s": gain(ks[7], (L, A_GROUPS, CHUNK)),
        "q_norm_g": gain(ks[8], (L, Q_LORA)),
        "w_uq": w(ks[9], (L, Q_LORA, MLA_HEADS * QK_DIM), Q_LORA),
        "kv_norm_g": gain(ks[10], (L, KV_LORA)),
        "w_ukv": w(ks[11], (L, KV_LORA, MLA_HEADS * (QK_NOPE + V_DIM)), KV_LORA),
        "mem_norm_g": gain(ks[12], (L, D)),
        "w_mem_kv": w(ks[13], (L, D, 2 * MEM_WIDTH), D),
        "w_gate": w(ks[14], (L, D, N_BRANCH * D), D),
        "b_gate": bias(ks[15], (L, N_BRANCH * D)),
        "w_branch": w(ks[16], (L, N_BRANCH, BRANCH_WIDTH, D), BRANCH_WIDTH),
        "w_out": w(ks[17], (L, D, D), D),
        "g_post": gain(ks[18], (L, D)),
    }


def _fwd_reference(x, mem, positions, g_pre, w_in, a_ln_g, a_ln_b, a_w_s, a_b_s,
              q_norm_g, w_uq, kv_norm_g, w_ukv, mem_norm_g, w_mem_kv,
              w_gate, b_gate, w_branch, w_out, g_post):
    cos, sin = rope_tables(positions)
    for l in range(DEPTH):
        x = hybrid_layer(x, mem, cos, sin, g_pre[l], w_in[l], a_ln_g[l], a_ln_b[l],
                         a_w_s[l], a_b_s[l], q_norm_g[l], w_uq[l], kv_norm_g[l], w_ukv[l],
                         mem_norm_g[l], w_mem_kv[l], w_gate[l], b_gate[l], w_branch[l],
                         w_out[l], g_post[l])
    return x


import jax as _jax
import jax.numpy as _jnp

TWIN_FORMAT = 'train_step'
FWD_PARAMS = ['x', 'mem', 'positions', 'g_pre', 'w_in', 'a_ln_g', 'a_ln_b', 'a_w_s', 'a_b_s', 'q_norm_g', 'w_uq', 'kv_norm_g', 'w_ukv', 'mem_norm_g', 'w_mem_kv', 'w_gate', 'b_gate', 'w_branch', 'w_out', 'g_post']
TWIN_WEIGHTS = ['g_pre', 'w_in', 'a_ln_g', 'a_ln_b', 'a_w_s', 'a_b_s', 'q_norm_g', 'w_uq', 'kv_norm_g', 'w_ukv', 'mem_norm_g', 'w_mem_kv', 'w_gate', 'b_gate', 'w_branch', 'w_out', 'g_post']
TWIN_DIFF_INPUT = 'x'
TWIN_INPUTS = ['x', 'mem', 'positions', 'g_pre', 'w_in', 'a_ln_g', 'a_ln_b', 'a_w_s', 'a_b_s', 'q_norm_g', 'w_uq', 'kv_norm_g', 'w_ukv', 'mem_norm_g', 'w_mem_kv', 'w_gate', 'b_gate', 'w_branch', 'w_out', 'g_post', 'loss_target', 'm_g_pre', 'm_w_in', 'm_a_ln_g', 'm_a_ln_b', 'm_a_w_s', 'm_a_b_s', 'm_q_norm_g', 'm_w_uq', 'm_kv_norm_g', 'm_w_ukv', 'm_mem_norm_g', 'm_w_mem_kv', 'm_w_gate', 'm_b_gate', 'm_w_branch', 'm_w_out', 'm_g_post', 'v_g_pre', 'v_w_in', 'v_a_ln_g', 'v_a_ln_b', 'v_a_w_s', 'v_a_b_s', 'v_q_norm_g', 'v_w_uq', 'v_kv_norm_g', 'v_w_ukv', 'v_mem_norm_g', 'v_w_mem_kv', 'v_w_gate', 'v_b_gate', 'v_w_branch', 'v_w_out', 'v_g_post']
TWIN_OUTPUTS = ['loss', 'grad_x', 'grad_g_pre', 'grad_w_in', 'grad_a_ln_g', 'grad_a_ln_b', 'grad_a_w_s', 'grad_a_b_s', 'grad_q_norm_g', 'grad_w_uq', 'grad_kv_norm_g', 'grad_w_ukv', 'grad_mem_norm_g', 'grad_w_mem_kv', 'grad_w_gate', 'grad_b_gate', 'grad_w_branch', 'grad_w_out', 'grad_g_post', 'delta_g_pre', 'delta_w_in', 'delta_a_ln_g', 'delta_a_ln_b', 'delta_a_w_s', 'delta_a_b_s', 'delta_q_norm_g', 'delta_w_uq', 'delta_kv_norm_g', 'delta_w_ukv', 'delta_mem_norm_g', 'delta_w_mem_kv', 'delta_w_gate', 'delta_b_gate', 'delta_w_branch', 'delta_w_out', 'delta_g_post', 'new_m_g_pre', 'new_m_w_in', 'new_m_a_ln_g', 'new_m_a_ln_b', 'new_m_a_w_s', 'new_m_a_b_s', 'new_m_q_norm_g', 'new_m_w_uq', 'new_m_kv_norm_g', 'new_m_w_ukv', 'new_m_mem_norm_g', 'new_m_w_mem_kv', 'new_m_w_gate', 'new_m_b_gate', 'new_m_w_branch', 'new_m_w_out', 'new_m_g_post', 'new_v_g_pre', 'new_v_w_in', 'new_v_a_ln_g', 'new_v_a_ln_b', 'new_v_a_w_s', 'new_v_a_b_s', 'new_v_q_norm_g', 'new_v_w_uq', 'new_v_kv_norm_g', 'new_v_w_ukv', 'new_v_mem_norm_g', 'new_v_w_mem_kv', 'new_v_w_gate', 'new_v_b_gate', 'new_v_w_branch', 'new_v_w_out', 'new_v_g_post']
TWIN_LEAF_KINDS = {'loss': 'loss', 'grad_x': 'grad_x', 'grad_g_pre': 'grad_w', 'grad_w_in': 'grad_w', 'grad_a_ln_g': 'grad_w', 'grad_a_ln_b': 'grad_w', 'grad_a_w_s': 'grad_w', 'grad_a_b_s': 'grad_w', 'grad_q_norm_g': 'grad_w', 'grad_w_uq': 'grad_w', 'grad_kv_norm_g': 'grad_w', 'grad_w_ukv': 'grad_w', 'grad_mem_norm_g': 'grad_w', 'grad_w_mem_kv': 'grad_w', 'grad_w_gate': 'grad_w', 'grad_b_gate': 'grad_w', 'grad_w_branch': 'grad_w', 'grad_w_out': 'grad_w', 'grad_g_post': 'grad_w', 'delta_g_pre': 'delta_w', 'delta_w_in': 'delta_w', 'delta_a_ln_g': 'delta_w', 'delta_a_ln_b': 'delta_w', 'delta_a_w_s': 'delta_w', 'delta_a_b_s': 'delta_w', 'delta_q_norm_g': 'delta_w', 'delta_w_uq': 'delta_w', 'delta_kv_norm_g': 'delta_w', 'delta_w_ukv': 'delta_w', 'delta_mem_norm_g': 'delta_w', 'delta_w_mem_kv': 'delta_w', 'delta_w_gate': 'delta_w', 'delta_b_gate': 'delta_w', 'delta_w_branch': 'delta_w', 'delta_w_out': 'delta_w', 'delta_g_post': 'delta_w', 'new_m_g_pre': 'new_m', 'new_m_w_in': 'new_m', 'new_m_a_ln_g': 'new_m', 'new_m_a_ln_b': 'new_m', 'new_m_a_w_s': 'new_m', 'new_m_a_b_s': 'new_m', 'new_m_q_norm_g': 'new_m', 'new_m_w_uq': 'new_m', 'new_m_kv_norm_g': 'new_m', 'new_m_w_ukv': 'new_m', 'new_m_mem_norm_g': 'new_m', 'new_m_w_mem_kv': 'new_m', 'new_m_w_gate': 'new_m', 'new_m_b_gate': 'new_m', 'new_m_w_branch': 'new_m', 'new_m_w_out': 'new_m', 'new_m_g_post': 'new_m', 'new_v_g_pre': 'new_v', 'new_v_w_in': 'new_v', 'new_v_a_ln_g': 'new_v', 'new_v_a_ln_b': 'new_v', 'new_v_a_w_s': 'new_v', 'new_v_a_b_s': 'new_v', 'new_v_q_norm_g': 'new_v', 'new_v_w_uq': 'new_v', 'new_v_kv_norm_g': 'new_v', 'new_v_w_ukv': 'new_v', 'new_v_mem_norm_g': 'new_v', 'new_v_w_mem_kv': 'new_v', 'new_v_w_gate': 'new_v', 'new_v_b_gate': 'new_v', 'new_v_w_branch': 'new_v', 'new_v_w_out': 'new_v', 'new_v_g_post': 'new_v'}


def _forward(args):
    return _fwd_reference(*[args[k] for k in FWD_PARAMS])


def _output_shape():
    def fwd():
        inp = _fwd_setup_inputs(0)
        return _fwd_reference(*[inp[k] for k in FWD_PARAMS])
    out = _jax.eval_shape(fwd)
    return out.shape, out.dtype

N_MICROBATCH = 1
ADAM_LR = 0.001
ADAM_B1 = 0.9
ADAM_B2 = 0.999
ADAM_EPS = 1e-08
ADAM_WD = 0.01
ADAM_STEP = 10
PER_EXAMPLE_BATCH_AXIS = {'x': 0, 'mem': 0, 'positions': 0, 'loss_target': 0}
SHARED_INPUTS = []
_WEIGHT_DTYPES = {'g_pre': _jnp.float32, 'w_in': _jnp.float32, 'a_ln_g': _jnp.float32, 'a_ln_b': _jnp.float32, 'a_w_s': _jnp.float32, 'a_b_s': _jnp.float32, 'q_norm_g': _jnp.float32, 'w_uq': _jnp.float32, 'kv_norm_g': _jnp.float32, 'w_ukv': _jnp.float32, 'mem_norm_g': _jnp.float32, 'w_mem_kv': _jnp.float32, 'w_gate': _jnp.float32, 'b_gate': _jnp.float32, 'w_branch': _jnp.float32, 'w_out': _jnp.float32, 'g_post': _jnp.float32}
MOMENT_SCALE = {'g_pre': 3.115882e-01, 'w_in': 1.208544e-01, 'a_ln_g': 9.808793e-02, 'a_ln_b': 9.644392e-02, 'a_w_s': 9.938458e-02, 'a_b_s': 1.442057e-01, 'q_norm_g': 7.682139e-02, 'w_uq': 3.202050e-02, 'kv_norm_g': 1.200300e-01, 'w_ukv': 3.996234e-02, 'mem_norm_g': 3.584531e-02, 'w_mem_kv': 2.576807e-02, 'w_gate': 4.125095e-02, 'b_gate': 5.284964e-02, 'w_branch': 1.277058e-01, 'w_out': 2.221843e-01, 'g_post': 3.205064e+01}


def _to_microbatches(a, axis):
    t = _jnp.moveaxis(a, axis, 0)
    t = t.reshape((N_MICROBATCH, t.shape[0] // N_MICROBATCH) + t.shape[1:])
    return _jnp.moveaxis(t, 1, axis + 1)


def setup_inputs(seed: int = 0) -> dict:
    inp = _fwd_setup_inputs(seed)
    key = _jax.random.fold_in(_jax.random.key(seed), 7919)
    shape, _ = _output_shape()
    out = dict(inp)
    out["loss_target"] = _jax.random.normal(_jax.random.fold_in(key, 0), shape, _jnp.float32)
    for i, name in enumerate(TWIN_WEIGHTS):
        w = inp[name].astype(_jnp.float32)
        if MOMENT_SCALE is None:
            s = _jnp.sqrt(_jnp.mean(_jnp.square(w)) + 1e-30)
        else:
            s = MOMENT_SCALE[name]
        km, kv = _jax.random.split(_jax.random.fold_in(key, i + 1))
        out[name] = w
        out["m_" + name] = s * _jax.random.normal(km, w.shape, _jnp.float32)
        out["v_" + name] = (s * s) * _jax.random.uniform(kv, w.shape, _jnp.float32, 0.5, 1.5)
    if N_MICROBATCH > 1:
        for name, axis in PER_EXAMPLE_BATCH_AXIS.items():
            out[name] = _to_microbatches(out[name], axis)
    return {'x': out['x'], 'mem': out['mem'], 'positions': out['positions'], 'g_pre': out['g_pre'], 'w_in': out['w_in'], 'a_ln_g': out['a_ln_g'], 'a_ln_b': out['a_ln_b'], 'a_w_s': out['a_w_s'], 'a_b_s': out['a_b_s'], 'q_norm_g': out['q_norm_g'], 'w_uq': out['w_uq'], 'kv_norm_g': out['kv_norm_g'], 'w_ukv': out['w_ukv'], 'mem_norm_g': out['mem_norm_g'], 'w_mem_kv': out['w_mem_kv'], 'w_gate': out['w_gate'], 'b_gate': out['b_gate'], 'w_branch': out['w_branch'], 'w_out': out['w_out'], 'g_post': out['g_post'], 'loss_target': out['loss_target'], 'm_g_pre': out['m_g_pre'], 'm_w_in': out['m_w_in'], 'm_a_ln_g': out['m_a_ln_g'], 'm_a_ln_b': out['m_a_ln_b'], 'm_a_w_s': out['m_a_w_s'], 'm_a_b_s': out['m_a_b_s'], 'm_q_norm_g': out['m_q_norm_g'], 'm_w_uq': out['m_w_uq'], 'm_kv_norm_g': out['m_kv_norm_g'], 'm_w_ukv': out['m_w_ukv'], 'm_mem_norm_g': out['m_mem_norm_g'], 'm_w_mem_kv': out['m_w_mem_kv'], 'm_w_gate': out['m_w_gate'], 'm_b_gate': out['m_b_gate'], 'm_w_branch': out['m_w_branch'], 'm_w_out': out['m_w_out'], 'm_g_post': out['m_g_post'], 'v_g_pre': out['v_g_pre'], 'v_w_in': out['v_w_in'], 'v_a_ln_g': out['v_a_ln_g'], 'v_a_ln_b': out['v_a_ln_b'], 'v_a_w_s': out['v_a_w_s'], 'v_a_b_s': out['v_a_b_s'], 'v_q_norm_g': out['v_q_norm_g'], 'v_w_uq': out['v_w_uq'], 'v_kv_norm_g': out['v_kv_norm_g'], 'v_w_ukv': out['v_w_ukv'], 'v_mem_norm_g': out['v_mem_norm_g'], 'v_w_mem_kv': out['v_w_mem_kv'], 'v_w_gate': out['v_w_gate'], 'v_b_gate': out['v_b_gate'], 'v_w_branch': out['v_w_branch'], 'v_w_out': out['v_w_out'], 'v_g_post': out['v_g_post']}


def _loss(weights, diff, rest, loss_target):
    with _jax.named_scope("forward"):
        args = {**rest, TWIN_DIFF_INPUT: diff, **{k: w.astype(_WEIGHT_DTYPES[k]) for k, w in weights.items()}}
        y = _forward(args)
    with _jax.named_scope("loss_head"):
        err = _jnp.square(y.astype(_jnp.float32) - loss_target)
        return 0.5 * _jnp.sum(_jnp.mean(err, axis=-1)) if err.ndim else 0.5 * err


def _adamw(w, g, m, v):
    m = ADAM_B1 * m + (1.0 - ADAM_B1) * g
    v = ADAM_B2 * v + (1.0 - ADAM_B2) * _jnp.square(g)
    m_hat = m / (1.0 - ADAM_B1 ** ADAM_STEP)
    v_hat = v / (1.0 - ADAM_B2 ** ADAM_STEP)
    delta = -ADAM_LR * (m_hat / (_jnp.sqrt(v_hat) + ADAM_EPS) + ADAM_WD * w)
    return delta, m, v


def reference(x, mem, positions, g_pre, w_in, a_ln_g, a_ln_b, a_w_s, a_b_s, q_norm_g, w_uq, kv_norm_g, w_ukv, mem_norm_g, w_mem_kv, w_gate, b_gate, w_branch, w_out, g_post, loss_target, m_g_pre, m_w_in, m_a_ln_g, m_a_ln_b, m_a_w_s, m_a_b_s, m_q_norm_g, m_w_uq, m_kv_norm_g, m_w_ukv, m_mem_norm_g, m_w_mem_kv, m_w_gate, m_b_gate, m_w_branch, m_w_out, m_g_post, v_g_pre, v_w_in, v_a_ln_g, v_a_ln_b, v_a_w_s, v_a_b_s, v_q_norm_g, v_w_uq, v_kv_norm_g, v_w_ukv, v_mem_norm_g, v_w_mem_kv, v_w_gate, v_b_gate, v_w_branch, v_w_out, v_g_post):
    given = dict(x=x, mem=mem, positions=positions, g_pre=g_pre, w_in=w_in, a_ln_g=a_ln_g, a_ln_b=a_ln_b, a_w_s=a_w_s, a_b_s=a_b_s, q_norm_g=q_norm_g, w_uq=w_uq, kv_norm_g=kv_norm_g, w_ukv=w_ukv, mem_norm_g=mem_norm_g, w_mem_kv=w_mem_kv, w_gate=w_gate, b_gate=b_gate, w_branch=w_branch, w_out=w_out, g_post=g_post, loss_target=loss_target, m_g_pre=m_g_pre, m_w_in=m_w_in, m_a_ln_g=m_a_ln_g, m_a_ln_b=m_a_ln_b, m_a_w_s=m_a_w_s, m_a_b_s=m_a_b_s, m_q_norm_g=m_q_norm_g, m_w_uq=m_w_uq, m_kv_norm_g=m_kv_norm_g, m_w_ukv=m_w_ukv, m_mem_norm_g=m_mem_norm_g, m_w_mem_kv=m_w_mem_kv, m_w_gate=m_w_gate, m_b_gate=m_b_gate, m_w_branch=m_w_branch, m_w_out=m_w_out, m_g_post=m_g_post, v_g_pre=v_g_pre, v_w_in=v_w_in, v_a_ln_g=v_a_ln_g, v_a_ln_b=v_a_ln_b, v_a_w_s=v_a_w_s, v_a_b_s=v_a_b_s, v_q_norm_g=v_q_norm_g, v_w_uq=v_w_uq, v_kv_norm_g=v_kv_norm_g, v_w_ukv=v_w_ukv, v_mem_norm_g=v_mem_norm_g, v_w_mem_kv=v_w_mem_kv, v_w_gate=v_w_gate, v_b_gate=v_b_gate, v_w_branch=v_w_branch, v_w_out=v_w_out, v_g_post=v_g_post)
    weights = {n: given[n] for n in TWIN_WEIGHTS}
    shared = {n: given[n] for n in SHARED_INPUTS}
    per_example = {n: given[n] for n in ['x', 'mem', 'positions']}
    grad_fn = _jax.value_and_grad(_loss, argnums=(0, 1))

    def one_microbatch(ex, loss_target):
        ex = dict(ex)
        diff = ex.pop(TWIN_DIFF_INPUT)
        return grad_fn(weights, diff, {**shared, **ex}, loss_target)

    if N_MICROBATCH == 1:
        loss, (grad_w, grad_x) = one_microbatch(per_example, given["loss_target"])
    else:
        def body(carry, xs):
            loss_sum, grad_sum = carry
            l_k, (gw_k, gx_k) = one_microbatch(xs[0], xs[1])
            with _jax.named_scope("update"):
                return (loss_sum + l_k, _jax.tree.map(_jnp.add, grad_sum, gw_k)), gx_k

        init = (_jnp.zeros((), _jnp.float32), _jax.tree.map(_jnp.zeros_like, weights))
        (loss, grad_w), grad_x = _jax.lax.scan(body, init, (per_example, given["loss_target"]))
    with _jax.named_scope("update"):
        delta_w, new_m, new_v = {}, {}, {}
        for n in TWIN_WEIGHTS:
            delta_w[n], new_m[n], new_v[n] = _adamw(weights[n], grad_w[n], given["m_" + n], given["v_" + n])
    return (loss, grad_x, *[grad_w[n] for n in TWIN_WEIGHTS], *[delta_w[n] for n in TWIN_WEIGHTS],
            *[new_m[n] for n in TWIN_WEIGHTS], *[new_v[n] for n in TWIN_WEIGHTS])
```

```python
import functools
import math

import jax
import jax.numpy as jnp
from jax import lax
from jax.experimental import pallas as pl
from jax.experimental.pallas import tpu as pltpu

F32 = jnp.float32
BF16 = jnp.bfloat16

D = 2048
EPS = 1e-6
CHUNK = 128
A_GROUPS = 16
HEADS = 16
QK_NOPE = 128
QK_ROPE = 64
QK_DIM = QK_NOPE + QK_ROPE
V_DIM = 128
LORA = 512
MEM_HEADS = 4
MEM_HEAD_DIM = 512
ROPE_THETA = 10000.0
QK_PAD = 256
IN_REF = 13376
IN_PAD = 13440
COL_U, COL_V, COL_ZA, COL_ZB, COL_QM, COL_ZM = 0, 1, 2, 3, 4, 5
COL_CQ, COL_CKV = 24, 25
COL_KR = 104

ADAM_LR = 0.001
ADAM_B1 = 0.9
ADAM_B2 = 0.999
ADAM_EPS = 1e-08
ADAM_WD = 0.01
ADAM_STEP = 10

VMEM_LIMIT = 56 * 1024 * 1024
LANES = 128

BIG = ("w_in", "w_uq", "w_ukv", "w_mem_kv", "w_gate", "w_branch", "w_out")
SMALL = ("g_pre", "a_ln_g", "a_ln_b", "a_w_s", "a_b_s", "q_norm_g", "kv_norm_g", "mem_norm_g", "b_gate", "g_post")
WEIGHTS = ("g_pre", "w_in", "a_ln_g", "a_ln_b", "a_w_s", "a_b_s", "q_norm_g", "w_uq", "kv_norm_g", "w_ukv",
           "mem_norm_g", "w_mem_kv", "w_gate", "b_gate", "w_branch", "w_out", "g_post")
FLAT_W = 512
N_CHIPS = 4
N_DEV = 8


def _params(sem=None):
    return pltpu.CompilerParams(dimension_semantics=sem, vmem_limit_bytes=VMEM_LIMIT)


def _sigmoid(z):
    return 1.0 / (1.0 + jnp.exp(-z))


def _gelu_parts(x):
    c = math.sqrt(2.0 / math.pi)
    x2 = x * x
    t = jnp.tanh(c * (x + 0.044715 * x * x2))
    g = 0.5 * x * (1.0 + t)
    dg = 0.5 * (1.0 + t) + 0.5 * x * (1.0 - t * t) * (c * (1.0 + 3.0 * 0.044715 * x2))
    return g, dg


def _silu_parts(z):
    s = _sigmoid(z)
    return z * s, s * (1.0 + z * (1.0 - s))


def _dot(a, b, dims):
    return lax.dot_general(a, b, (dims, ((), ())), preferred_element_type=F32)


NN = ((1,), (0,))
NT = ((1,), (1,))
TN = ((0,), (0,))


def _mm(a, b, mode, *, tm, tn, tk, out_dtype, name, add=None):
    if mode == "nn":
        (M, K), (_, N) = a.shape, b.shape
    elif mode == "nt":
        (M, K), (N, _) = a.shape, b.shape
    else:
        (K, M), (_, N) = a.shape, b.shape
    tm, tn, tk = min(tm, M), min(tn, N), min(tk, K)
    assert M % tm == 0 and N % tn == 0 and K % tk == 0, (name, M, N, K, tm, tn, tk)
    ni, nj, nk = M // tm, N // tn, K // tk
    dims = {"nn": NN, "nt": NT, "tn": TN}[mode]
    has_add = add is not None

    def body(*refs):
        a_ref, b_ref = refs[0], refs[1]
        add_ref = refs[2] if has_add else None
        o_ref = refs[3] if has_add else refs[2]
        part = _dot(a_ref[...].astype(BF16), b_ref[...].astype(BF16), dims)

        def finish(r):
            if has_add:
                r = r + add_ref[...]
            o_ref[...] = r.astype(out_dtype)

        if nk == 1:
            finish(part)
        else:
            acc = refs[-1]
            k = pl.program_id(2)

            @pl.when(k == 0)
            def _():
                acc[...] = part

            @pl.when(k > 0)
            def _():
                acc[...] += part

            @pl.when(k == nk - 1)
            def _():
                finish(acc[...])

    if mode == "nn":
        a_spec = pl.BlockSpec((tm, tk), lambda j, i, k: (i, k))
        b_spec = pl.BlockSpec((tk, tn), lambda j, i, k: (k, j))
    elif mode == "nt":
        a_spec = pl.BlockSpec((tm, tk), lambda j, i, k: (i, k))
        b_spec = pl.BlockSpec((tn, tk), lambda j, i, k: (j, k))
    else:
        a_spec = pl.BlockSpec((tk, tm), lambda j, i, k: (k, i))
        b_spec = pl.BlockSpec((tk, tn), lambda j, i, k: (k, j))
    o_spec = pl.BlockSpec((tm, tn), lambda j, i, k: (i, j))
    in_specs = [a_spec, b_spec] + ([o_spec] if has_add else [])
    args = (a, b) + ((add,) if has_add else ())
    return pl.pallas_call(
        body, name=name, grid=(nj, ni, nk), in_specs=in_specs, out_specs=o_spec,
        out_shape=jax.ShapeDtypeStruct((M, N), out_dtype),
        scratch_shapes=[pltpu.VMEM((tm, tn), F32)] if nk > 1 else [],
        compiler_params=_params(("parallel", "parallel", "arbitrary")),
    )(*args)


def _rms_fwd(x, g, *, width, col, tm, name):
    rows = x.shape[0]
    tm = min(tm, rows)

    def body(x_ref, g_ref, y_ref, r_ref):
        xv = x_ref[...]
        r = lax.rsqrt(jnp.mean(xv * xv, axis=-1, keepdims=True) + EPS)
        y_ref[...] = ((xv * r) * g_ref[...]).astype(BF16)
        r_ref[...] = r

    return pl.pallas_call(
        body, name=name, grid=(rows // tm,),
        in_specs=[pl.BlockSpec((tm, width), lambda i: (i, col)), pl.BlockSpec((1, width), lambda i: (0, 0))],
        out_specs=[pl.BlockSpec((tm, width), lambda i: (i, 0)), pl.BlockSpec((tm, 1), lambda i: (i, 0))],
        out_shape=[jax.ShapeDtypeStruct((rows, width), BF16), jax.ShapeDtypeStruct((rows, 1), F32)],
        compiler_params=_params(("parallel",)),
    )(x, g)


def _rms_bwd(d, x, rstd, g, *, width, col, tm, out_dtype, name, residual=None):
    rows = d.shape[0]
    tm = min(tm, rows)
    has_res = residual is not None

    def body(*refs):
        d_ref, x_ref, r_ref, g_ref = refs[:4]
        res_ref = refs[4] if has_res else None
        dx_ref, gg_ref = refs[-2], refs[-1]
        dv = d_ref[...]
        n = x_ref[...] * r_ref[...]

        @pl.when(pl.program_id(0) == 0)
        def _():
            gg_ref[...] = jnp.zeros_like(gg_ref)

        gg_ref[...] += jnp.sum(dv * n, axis=0, keepdims=True)
        gd = dv * g_ref[...]
        dx = r_ref[...] * (gd - n * jnp.mean(gd * n, axis=-1, keepdims=True))
        if has_res:
            dx = dx + res_ref[...]
        dx_ref[...] = dx.astype(out_dtype)

    blk = pl.BlockSpec((tm, width), lambda i: (i, 0))
    in_specs = [blk, pl.BlockSpec((tm, width), lambda i: (i, col)), pl.BlockSpec((tm, 1), lambda i: (i, 0)),
                pl.BlockSpec((1, width), lambda i: (0, 0))] + ([blk] if has_res else [])
    args = (d, x, rstd, g) + ((residual,) if has_res else ())
    return pl.pallas_call(
        body, name=name, grid=(rows // tm,), in_specs=in_specs,
        out_specs=[blk, pl.BlockSpec((1, width), lambda i: (0, 0))],
        out_shape=[jax.ShapeDtypeStruct((rows, width), out_dtype), jax.ShapeDtypeStruct((1, width), F32)],
        compiler_params=_params(("arbitrary",)),
    )(*args)


def _rope_tables(pos_col, inv_lane, *, tm):
    rows = pos_col.shape[0]
    tm = min(tm, rows)

    def body(p_ref, f_ref, c_ref, s1_ref, s2_ref):
        ang = p_ref[...].astype(F32) * f_ref[...]
        lane = lax.broadcasted_iota(jnp.int32, ang.shape, 1)
        c, s = jnp.cos(ang), jnp.sin(ang)
        half = QK_ROPE // 2
        c_ref[...] = jnp.where(lane < QK_ROPE, c, 0.0)
        s1_ref[...] = jnp.where(lane < half, -s, 0.0)
        s2_ref[...] = jnp.where((lane >= half) & (lane < QK_ROPE), s, 0.0)

    blk = pl.BlockSpec((tm, LANES), lambda i: (i, 0))
    return pl.pallas_call(
        body, name="rope_tables", grid=(rows // tm,),
        in_specs=[pl.BlockSpec((tm, 1), lambda i: (i, 0)), pl.BlockSpec((1, LANES), lambda i: (0, 0))],
        out_specs=[blk, blk, blk], out_shape=[jax.ShapeDtypeStruct((rows, LANES), F32)] * 3,
        compiler_params=_params(("parallel",)),
    )(pos_col, inv_lane)


def _rot(t, c, s1, s2, sign):
    r1 = pltpu.roll(t, LANES - QK_ROPE // 2, 1) * s1
    r2 = pltpu.roll(t, QK_ROPE // 2, 1) * s2
    return t * c + (r1 + r2) if sign > 0 else t * c - (r1 + r2)


def _mla_proj(cqn, ckvn, proj, tabs, wq, wk, wv, *, tm):
    rows = cqn.shape[0]
    tm = min(tm, rows)

    def body(cq_ref, ckv_ref, kr_ref, c_ref, s1_ref, s2_ref, wq_ref, wk_ref, wv_ref, q_ref, k_ref, v_ref):
        c, s1, s2 = c_ref[...], s1_ref[...], s2_ref[...]
        q = _dot(cq_ref[...], wq_ref[...], NN)
        k = _dot(ckv_ref[...], wk_ref[...], NN)
        kpe = _rot(kr_ref[...], c, s1, s2, 1).astype(BF16)
        for h in range(HEADS):
            lo = h * QK_PAD
            q_ref[:, lo:lo + QK_NOPE] = q[:, lo:lo + QK_NOPE].astype(BF16)
            q_ref[:, lo + QK_NOPE:lo + QK_PAD] = _rot(q[:, lo + QK_NOPE:lo + QK_PAD], c, s1, s2, 1).astype(BF16)
            k_ref[:, lo:lo + QK_NOPE] = k[:, lo:lo + QK_NOPE].astype(BF16)
            k_ref[:, lo + QK_NOPE:lo + QK_PAD] = kpe
        v_ref[...] = _dot(ckv_ref[...], wv_ref[...], NN).astype(BF16)

    def row(w):
        return pl.BlockSpec((tm, w), lambda i: (i, 0))

    def whole(w):
        return pl.BlockSpec(w.shape, lambda i: (0, 0))

    return pl.pallas_call(
        body, name="mla_proj", grid=(rows // tm,),
        in_specs=[row(LORA), row(LORA), pl.BlockSpec((tm, LANES), lambda i: (i, COL_KR)), row(LANES), row(LANES), row(LANES),
                  whole(wq), whole(wk), whole(wv)],
        out_specs=[row(HEADS * QK_PAD), row(HEADS * QK_PAD), row(HEADS * V_DIM)],
        out_shape=[jax.ShapeDtypeStruct((rows, HEADS * QK_PAD), BF16), jax.ShapeDtypeStruct((rows, HEADS * QK_PAD), BF16),
                   jax.ShapeDtypeStruct((rows, HEADS * V_DIM), BF16)],
        compiler_params=_params(("parallel",)),
    )(cqn, ckvn, proj, *tabs, wq, wk, wv)


def _mla_fwd(q, k, v, proj, *, t):
    S = q.shape[0]
    t = min(t, S)
    n = S // t
    scale = QK_DIM ** -0.5

    def body(q_ref, k_ref, v_ref, z_ref, o_ref, y_ref, lse_ref, m_s, l_s, acc_s):
        qi, ki = pl.program_id(1), pl.program_id(2)

        @pl.when(ki == 0)
        def _():
            m_s[...] = jnp.full_like(m_s, -jnp.inf)
            l_s[...] = jnp.zeros_like(l_s)
            acc_s[...] = jnp.zeros_like(acc_s)

        def step(diag):
            s = _dot(q_ref[...], k_ref[...], NT) * scale
            if diag:
                r = lax.broadcasted_iota(jnp.int32, s.shape, 0)
                c = lax.broadcasted_iota(jnp.int32, s.shape, 1)
                s = jnp.where(c <= r, s, -1e30)
            m_old = m_s[...]
            m_new = jnp.maximum(m_old, jnp.max(s, axis=-1, keepdims=True))
            alpha = jnp.exp(m_old - m_new)
            p = jnp.exp(s - m_new)
            l_s[...] = alpha * l_s[...] + jnp.sum(p, axis=-1, keepdims=True)
            acc_s[...] = alpha * acc_s[...] + _dot(p.astype(BF16), v_ref[...], NN)
            m_s[...] = m_new

        @pl.when(ki < qi)
        def _():
            step(False)

        @pl.when(ki == qi)
        def _():
            step(True)
            o = acc_s[...] / l_s[...]
            o_ref[...] = o
            sz, _ = _silu_parts(z_ref[...])
            y_ref[...] = (o * sz).astype(BF16)
            lse_ref[0] = m_s[...] + jnp.log(l_s[...])

    zcol = COL_ZB * (D // V_DIM)
    return pl.pallas_call(
        body, name="mla_fwd", grid=(HEADS, n, n),
        in_specs=[pl.BlockSpec((t, QK_PAD), lambda h, i, j: (i, h)),
                  pl.BlockSpec((t, QK_PAD), lambda h, i, j: (jnp.minimum(i, j), h)),
                  pl.BlockSpec((t, V_DIM), lambda h, i, j: (jnp.minimum(i, j), h)),
                  pl.BlockSpec((t, V_DIM), lambda h, i, j: (i, zcol + h))],
        out_specs=[pl.BlockSpec((t, V_DIM), lambda h, i, j: (i, h)), pl.BlockSpec((t, V_DIM), lambda h, i, j: (i, h)),
                   pl.BlockSpec((1, t, 1), lambda h, i, j: (h, i, 0))],
        out_shape=[jax.ShapeDtypeStruct((S, HEADS * V_DIM), F32), jax.ShapeDtypeStruct((S, HEADS * V_DIM), BF16),
                   jax.ShapeDtypeStruct((HEADS, S, 1), F32)],
        scratch_shapes=[pltpu.VMEM((t, 1), F32), pltpu.VMEM((t, 1), F32), pltpu.VMEM((t, V_DIM), F32)],
        compiler_params=_params(("parallel", "parallel", "arbitrary")),
    )(q, k, v, proj)


def _mla_gate_bwd(dy, o, proj, *, tm):
    S = dy.shape[0]
    tm = min(tm, S)

    def body(dy_ref, o_ref, z_ref, do_ref, dz_ref, dl_ref):
        sz, dsz = _silu_parts(z_ref[...])
        dyv, ov = dy_ref[...], o_ref[...]
        do = dyv * sz
        do_ref[...] = do.astype(BF16)
        dz_ref[...] = (dyv * ov * dsz).astype(BF16)
        prod = do * ov
        for h in range(HEADS):
            dl_ref[h] = jnp.sum(prod[:, h * V_DIM:(h + 1) * V_DIM], axis=-1, keepdims=True)

    blk = pl.BlockSpec((tm, D), lambda i: (i, 0))
    return pl.pallas_call(
        body, name="mla_gate_bwd", grid=(S // tm,),
        in_specs=[blk, blk, pl.BlockSpec((tm, D), lambda i: (i, COL_ZB))],
        out_specs=[blk, blk, pl.BlockSpec((HEADS, tm, 1), lambda i: (0, i, 0))],
        out_shape=[jax.ShapeDtypeStruct((S, D), BF16), jax.ShapeDtypeStruct((S, D), BF16),
                   jax.ShapeDtypeStruct((HEADS, S, 1), F32)],
        compiler_params=_params(("parallel",)),
    )(dy, o, proj)


def _mla_bwd(q, k, v, do, lse, delta, *, t):
    S = q.shape[0]
    t = min(t, S)
    n = S // t
    scale = QK_DIM ** -0.5

    def body(q_ref, k_ref, v_ref, do_ref, lse_ref, dl_ref, dq_ref, dk_ref, dv_ref, dk_s, dv_s):
        ki, qi = pl.program_id(1), pl.program_id(2)

        @pl.when((ki == 0) & (qi == 0))
        def _():
            dq_ref[...] = jnp.zeros_like(dq_ref)

        @pl.when(qi == 0)
        def _():
            dk_s[...] = jnp.zeros_like(dk_s)
            dv_s[...] = jnp.zeros_like(dv_s)

        def step(diag):
            qv, kv, dov = q_ref[...], k_ref[...], do_ref[...]
            s = _dot(qv, kv, NT) * scale
            if diag:
                r = lax.broadcasted_iota(jnp.int32, s.shape, 0)
                c = lax.broadcasted_iota(jnp.int32, s.shape, 1)
                s = jnp.where(c <= r, s, -1e30)
            p = jnp.exp(s - lse_ref[0])
            p16 = p.astype(BF16)
            dv_s[...] += _dot(p16, dov, TN)
            dp = _dot(dov, v_ref[...], NT)
            ds = (p * (dp - dl_ref[0]) * scale).astype(BF16)
            dk_s[...] += _dot(ds, qv, TN)
            rows = pl.ds(pl.multiple_of(qi * t, t), t)
            dq_ref[rows, :] += _dot(ds, kv, NN)

        @pl.when(qi > ki)
        def _():
            step(False)

        @pl.when(qi == ki)
        def _():
            step(True)

        @pl.when(qi == n - 1)
        def _():
            dk_ref[...] = dk_s[...]
            dv_ref[...] = dv_s[...]

    def qmap(h, j, i):
        return (jnp.maximum(i, j), h)

    return pl.pallas_call(
        body, name="mla_bwd", grid=(HEADS, n, n),
        in_specs=[pl.BlockSpec((t, QK_PAD), qmap),
                  pl.BlockSpec((t, QK_PAD), lambda h, j, i: (j, h)),
                  pl.BlockSpec((t, V_DIM), lambda h, j, i: (j, h)),
                  pl.BlockSpec((t, V_DIM), qmap),
                  pl.BlockSpec((1, t, 1), lambda h, j, i: (h, jnp.maximum(i, j), 0)),
                  pl.BlockSpec((1, t, 1), lambda h, j, i: (h, jnp.maximum(i, j), 0))],
        out_specs=[pl.BlockSpec((S, QK_PAD), lambda h, j, i: (0, h)),
                   pl.BlockSpec((t, QK_PAD), lambda h, j, i: (j, h)),
                   pl.BlockSpec((t, V_DIM), lambda h, j, i: (j, h))],
        out_shape=[jax.ShapeDtypeStruct((S, HEADS * QK_PAD), F32), jax.ShapeDtypeStruct((S, HEADS * QK_PAD), F32),
                   jax.ShapeDtypeStruct((S, HEADS * V_DIM), F32)],
        scratch_shapes=[pltpu.VMEM((t, QK_PAD), F32), pltpu.VMEM((t, V_DIM), F32)],
        compiler_params=_params(("arbitrary", "arbitrary", "arbitrary")),
    )(q, k, v, do, lse, delta)


def _mla_qk_post(dq, dk, tabs, *, tm):
    S = dq.shape[0]
    tm = min(tm, S)

    def body(dq_ref, dk_ref, c_ref, s1_ref, s2_ref, q16_ref, k16_ref, kr_ref):
        c, s1, s2 = c_ref[...], s1_ref[...], s2_ref[...]
        kpe = jnp.zeros((tm, LANES), F32)
        for h in range(HEADS):
            lo = h * QK_PAD
            q16_ref[:, lo:lo + QK_NOPE] = dq_ref[:, lo:lo + QK_NOPE].astype(BF16)
            q16_ref[:, lo + QK_NOPE:lo + QK_PAD] = _rot(dq_ref[:, lo + QK_NOPE:lo + QK_PAD], c, s1, s2, -1).astype(BF16)
            kpe = kpe + dk_ref[:, lo + QK_NOPE:lo + QK_PAD]
        k16_ref[...] = dk_ref[...].astype(BF16)
        kr_ref[...] = _rot(kpe, c, s1, s2, -1).astype(BF16)

    wide = pl.BlockSpec((tm, HEADS * QK_PAD), lambda i: (i, 0))
    lane = pl.BlockSpec((tm, LANES), lambda i: (i, 0))
    return pl.pallas_call(
        body, name="mla_qk_post", grid=(S // tm,),
        in_specs=[wide, wide, lane, lane, lane], out_specs=[wide, wide, lane],
        out_shape=[jax.ShapeDtypeStruct((S, HEADS * QK_PAD), BF16), jax.ShapeDtypeStruct((S, HEADS * QK_PAD), BF16),
                   jax.ShapeDtypeStruct((S, LANES), BF16)],
        compiler_params=_params(("parallel",)),
    )(dq, dk, *tabs)


def _mem_scores(q16, km_ref, h):
    lo = h * MEM_HEAD_DIM
    s = _dot(q16, km_ref[:, lo:lo + MEM_HEAD_DIM], NT) * (MEM_HEAD_DIM ** -0.5)
    e = jnp.exp(s - jnp.max(s, axis=-1, keepdims=True))
    return e / jnp.sum(e, axis=-1, keepdims=True)


def _mem_fwd(proj, kvm, *, tm):
    S = proj.shape[0]
    tm = min(tm, S)
    M = kvm.shape[0]

    def body(q_ref, z_ref, km_ref, vm_ref, y_ref):
        sz, _ = _silu_parts(z_ref[...])
        for h in range(MEM_HEADS):
            lo = h * MEM_HEAD_DIM
            p = _mem_scores(q_ref[:, lo:lo + MEM_HEAD_DIM].astype(BF16), km_ref, h)
            o = _dot(p.astype(BF16), vm_ref[:, lo:lo + MEM_HEAD_DIM], NN)
            y_ref[:, lo:lo + MEM_HEAD_DIM] = (o * sz[:, lo:lo + MEM_HEAD_DIM]).astype(BF16)

    return pl.pallas_call(
        body, name="mem_fwd", grid=(S // tm,),
        in_specs=[pl.BlockSpec((tm, D), lambda i: (i, COL_QM)), pl.BlockSpec((tm, D), lambda i: (i, COL_ZM)),
                  pl.BlockSpec((M, D), lambda i: (0, 0)), pl.BlockSpec((M, D), lambda i: (0, 1))],
        out_specs=pl.BlockSpec((tm, D), lambda i: (i, 0)),
        out_shape=jax.ShapeDtypeStruct((S, D), BF16),
        compiler_params=_params(("parallel",)),
    )(proj, proj, kvm, kvm)


def _mem_bwd(proj, kvm, dy, *, tm):
    S = proj.shape[0]
    tm = min(tm, S)
    M = kvm.shape[0]
    scale = MEM_HEAD_DIM ** -0.5

    def body(q_ref, z_ref, km_ref, vm_ref, dy_ref, dq_ref, dz_ref, dkv_ref):
        @pl.when(pl.program_id(0) == 0)
        def _():
            dkv_ref[...] = jnp.zeros_like(dkv_ref)

        sz, dsz = _silu_parts(z_ref[...])
        dyv = dy_ref[...]
        for h in range(MEM_HEADS):
            lo = h * MEM_HEAD_DIM
            sl = slice(lo, lo + MEM_HEAD_DIM)
            q16 = q_ref[:, sl].astype(BF16)
            p = _mem_scores(q16, km_ref, h)
            p16 = p.astype(BF16)
            o = _dot(p16, vm_ref[:, sl], NN)
            dy_h = dyv[:, sl]
            dz_ref[:, sl] = (dy_h * o * dsz[:, sl]).astype(BF16)
            do16 = (dy_h * sz[:, sl]).astype(BF16)
            dp = _dot(do16, vm_ref[:, sl], NT)
            ds = (p * (dp - jnp.sum(dp * p, axis=-1, keepdims=True)) * scale).astype(BF16)
            dq_ref[:, sl] = _dot(ds, km_ref[:, sl], NN).astype(BF16)
            dkv_ref[:, sl] += _dot(ds, q16, TN)
            dkv_ref[:, D + lo:D + lo + MEM_HEAD_DIM] += _dot(p16, do16, TN)

    blk = pl.BlockSpec((tm, D), lambda i: (i, 0))
    return pl.pallas_call(
        body, name="mem_bwd", grid=(S // tm,),
        in_specs=[pl.BlockSpec((tm, D), lambda i: (i, COL_QM)), pl.BlockSpec((tm, D), lambda i: (i, COL_ZM)),
                  pl.BlockSpec((M, D), lambda i: (0, 0)), pl.BlockSpec((M, D), lambda i: (0, 1)), blk],
        out_specs=[blk, blk, pl.BlockSpec((M, 2 * D), lambda i: (0, 0))],
        out_shape=[jax.ShapeDtypeStruct((S, D), BF16), jax.ShapeDtypeStruct((S, D), BF16),
                   jax.ShapeDtypeStruct((M, 2 * D), F32)],
        compiler_params=_params(("arbitrary",)),
    )(proj, proj, kvm, kvm, dy)


def _gmlp_common(u_ref, v_ref, lng_ref, lnb_ref):
    u, du = _gelu_parts(u_ref[...])
    vg, dvg = _gelu_parts(v_ref[...])
    mu = jnp.mean(vg, axis=-1, keepdims=True)
    vc = vg - mu
    r = lax.rsqrt(jnp.mean(vc * vc, axis=-1, keepdims=True) + EPS)
    vhat = vc * r
    vn = vhat * lng_ref[...] + lnb_ref[...]
    return u, du, dvg, r, vhat, vn.astype(BF16)


def _gmlp_fwd(proj, ln_g, ln_b, wm, bs_t):
    S = proj.shape[0]

    def body(u_ref, v_ref, z_ref, lng_ref, lnb_ref, wm_ref, bs_ref, y_ref):
        u, _, _, _, _, v16 = _gmlp_common(u_ref, v_ref, lng_ref, lnb_ref)
        sz, _ = _silu_parts(z_ref[...])
        for g in range(A_GROUPS):
            sl = slice(g * CHUNK, (g + 1) * CHUNK)
            sv = _dot(wm_ref[g], v16[:, sl], NN) + bs_ref[:, g:g + 1]
            y_ref[:, sl] = (u[:, sl] * sv * sz[:, sl]).astype(BF16)

    def col(c):
        return pl.BlockSpec((CHUNK, D), lambda i: (i, c))

    vec = pl.BlockSpec((1, D), lambda i: (0, 0))
    return pl.pallas_call(
        body, name="gmlp_fwd", grid=(S // CHUNK,),
        in_specs=[col(COL_U), col(COL_V), col(COL_ZA), vec, vec,
                  pl.BlockSpec((A_GROUPS, CHUNK, CHUNK), lambda i: (0, 0, 0)), pl.BlockSpec((CHUNK, A_GROUPS), lambda i: (0, 0))],
        out_specs=col(0), out_shape=jax.ShapeDtypeStruct((S, D), BF16),
        compiler_params=_params(("parallel",)),
    )(proj, proj, proj, ln_g, ln_b, wm, bs_t)


def _gmlp_bwd(proj, dy, ln_g, ln_b, wm, bs_t):
    S = proj.shape[0]

    def body(u_ref, v_ref, z_ref, dy_ref, lng_ref, lnb_ref, wm_ref, bs_ref,
             du_ref, dv_ref, dz_ref, gws_ref, dsv_ref, glg_ref, glb_ref, dvn_s):
        @pl.when(pl.program_id(0) == 0)
        def _():
            gws_ref[...] = jnp.zeros_like(gws_ref)
            dsv_ref[...] = jnp.zeros_like(dsv_ref)
            glg_ref[...] = jnp.zeros_like(glg_ref)
            glb_ref[...] = jnp.zeros_like(glb_ref)

        u, du, dvg, r, vhat, v16 = _gmlp_common(u_ref, v_ref, lng_ref, lnb_ref)
        sz, dsz = _silu_parts(z_ref[...])
        dyv = dy_ref[...]
        for g in range(A_GROUPS):
            sl = slice(g * CHUNK, (g + 1) * CHUNK)
            sv = _dot(wm_ref[g], v16[:, sl], NN) + bs_ref[:, g:g + 1]
            dy_g, u_g, sz_g = dyv[:, sl], u[:, sl], sz[:, sl]
            dsv = dy_g * u_g * sz_g
            du_ref[:, sl] = (dy_g * sv * sz_g * du[:, sl]).astype(BF16)
            dz_ref[:, sl] = (dy_g * u_g * sv * dsz[:, sl]).astype(BF16)
            dsv16 = dsv.astype(BF16)
            dvn_s[:, sl] = _dot(wm_ref[g], dsv16, TN)
            gws_ref[g] += _dot(dsv16, v16[:, sl], NT)
            dsv_ref[:, sl] += dsv
        dvn = dvn_s[...]
        glb_ref[...] += jnp.sum(dvn, axis=0, keepdims=True)
        glg_ref[...] += jnp.sum(dvn * vhat, axis=0, keepdims=True)
        dvh = dvn * lng_ref[...]
        dvc = r * (dvh - jnp.mean(dvh, axis=-1, keepdims=True) - vhat * jnp.mean(dvh * vhat, axis=-1, keepdims=True))
        dv_ref[...] = (dvc * dvg).astype(BF16)

    def col(c):
        return pl.BlockSpec((CHUNK, D), lambda i: (i, c))

    vec = pl.BlockSpec((1, D), lambda i: (0, 0))
    wsp = pl.BlockSpec((A_GROUPS, CHUNK, CHUNK), lambda i: (0, 0, 0))
    return pl.pallas_call(
        body, name="gmlp_bwd", grid=(S // CHUNK,),
        in_specs=[col(COL_U), col(COL_V), col(COL_ZA), col(0), vec, vec, wsp, pl.BlockSpec((CHUNK, A_GROUPS), lambda i: (0, 0))],
        out_specs=[col(0), col(0), col(0), wsp, pl.BlockSpec((CHUNK, D), lambda i: (0, 0)), vec, vec],
        out_shape=[jax.ShapeDtypeStruct((S, D), BF16)] * 3 + [
            jax.ShapeDtypeStruct((A_GROUPS, CHUNK, CHUNK), F32), jax.ShapeDtypeStruct((CHUNK, D), F32),
            jax.ShapeDtypeStruct((1, D), F32), jax.ShapeDtypeStruct((1, D), F32)],
        scratch_shapes=[pltpu.VMEM((CHUNK, D), F32)],
        compiler_params=_params(("arbitrary",)),
    )(proj, proj, proj, dy, ln_g, ln_b, wm, bs_t)


def _gate_merge(h16, ys, wg, bg, wbs, *, tm, tn):
    S = h16.shape[0]
    tm = min(tm, S)
    nj = D // tn

    def body(h_ref, ya_ref, yb_ref, ym_ref, wg0, wg1, wg2, bg0, bg1, bg2, wb0, wb1, wb2,
             mg_ref, g0_ref, g1_ref, g2_ref, p0_ref, p1_ref, p2_ref):
        hv = h_ref[...]
        acc = None
        for y_ref, wg_ref, bgr, wb_ref, g_ref, p_ref in ((ya_ref, wg0, bg0, wb0, g0_ref, p0_ref),
                                                         (yb_ref, wg1, bg1, wb1, g1_ref, p1_ref),
                                                         (ym_ref, wg2, bg2, wb2, g2_ref, p2_ref)):
            gate = _sigmoid(_dot(hv, wg_ref[...], NN) + bgr[...])
            p = _dot(y_ref[...], wb_ref[...], NN)
            g_ref[...] = gate.astype(BF16)
            p_ref[...] = p.astype(BF16)
            acc = gate * p if acc is None else acc + gate * p
        mg_ref[...] = acc.astype(BF16)

    a_spec = pl.BlockSpec((tm, D), lambda j, i: (i, 0))
    o_spec = pl.BlockSpec((tm, tn), lambda j, i: (i, j))

    def wgs(n):
        return pl.BlockSpec((D, tn), lambda j, i: (0, n * nj + j))

    def bgs(n):
        return pl.BlockSpec((1, tn), lambda j, i: (0, n * nj + j))

    wbsp = pl.BlockSpec((D, tn), lambda j, i: (0, j))
    return pl.pallas_call(
        body, name="gate_merge", grid=(nj, S // tm),
        in_specs=[a_spec] * 4 + [wgs(0), wgs(1), wgs(2), bgs(0), bgs(1), bgs(2), wbsp, wbsp, wbsp],
        out_specs=[o_spec] * 7, out_shape=[jax.ShapeDtypeStruct((S, D), BF16)] * 7,
        compiler_params=_params(("parallel", "parallel")),
    )(h16, *ys, wg, wg, wg, bg, bg, bg, *wbs)


def _gate_bwd(dmerged, gates, ps, *, tm):
    S = dmerged.shape[0]
    tm = min(tm, S)

    def body(dm_ref, g0, g1, g2, p0, p1, p2, dp0, dp1, dp2, dg_ref, gb_ref):
        @pl.when(pl.program_id(0) == 0)
        def _():
            gb_ref[...] = jnp.zeros_like(gb_ref)

        dm = dm_ref[...]
        for n, (g_ref, p_ref, dp_ref) in enumerate(((g0, p0, dp0), (g1, p1, dp1), (g2, p2, dp2))):
            gate = g_ref[...].astype(F32)
            dp_ref[...] = (dm * gate).astype(BF16)
            dg = dm * p_ref[...].astype(F32) * gate * (1.0 - gate)
            dg_ref[:, n * D:(n + 1) * D] = dg.astype(BF16)
            gb_ref[:, n * D:(n + 1) * D] += jnp.sum(dg, axis=0, keepdims=True)

    blk = pl.BlockSpec((tm, D), lambda i: (i, 0))
    return pl.pallas_call(
        body, name="gate_bwd", grid=(S // tm,),
        in_specs=[blk] * 7,
        out_specs=[blk, blk, blk, pl.BlockSpec((tm, 3 * D), lambda i: (i, 0)), pl.BlockSpec((1, 3 * D), lambda i: (0, 0))],
        out_shape=[jax.ShapeDtypeStruct((S, D), BF16)] * 3 + [jax.ShapeDtypeStruct((S, 3 * D), BF16),
                                                              jax.ShapeDtypeStruct((1, 3 * D), F32)],
        compiler_params=_params(("arbitrary",)),
    )(dmerged, *gates, *ps)


def _post_loss(x, out, target, g_post, *, tm):
    S = x.shape[0]
    tm = min(tm, S)

    def body(x_ref, o_ref, t_ref, g_ref, dy_ref, do_ref, gg_ref, ls_ref):
        @pl.when(pl.program_id(0) == 0)
        def _():
            gg_ref[...] = jnp.zeros_like(gg_ref)
            ls_ref[...] = jnp.zeros_like(ls_ref)

        ov = o_ref[...]
        r = lax.rsqrt(jnp.mean(ov * ov, axis=-1, keepdims=True) + EPS)
        n = ov * r
        err = (x_ref[...] + n * g_ref[...]) - t_ref[...]
        ls_ref[...] += 0.5 * jnp.sum(jnp.mean(err * err, axis=-1, keepdims=True))
        dy = err * (1.0 / D)
        dy_ref[...] = dy
        gg_ref[...] += jnp.sum(dy * n, axis=0, keepdims=True)
        gd = dy * g_ref[...]
        do_ref[...] = (r * (gd - n * jnp.mean(gd * n, axis=-1, keepdims=True))).astype(BF16)

    blk = pl.BlockSpec((tm, D), lambda i: (i, 0))
    vec = pl.BlockSpec((1, D), lambda i: (0, 0))
    return pl.pallas_call(
        body, name="post_loss", grid=(S // tm,),
        in_specs=[blk, blk, blk, vec],
        out_specs=[blk, blk, vec, pl.BlockSpec((1, LANES), lambda i: (0, 0))],
        out_shape=[jax.ShapeDtypeStruct((S, D), F32), jax.ShapeDtypeStruct((S, D), BF16),
                   jax.ShapeDtypeStruct((1, D), F32), jax.ShapeDtypeStruct((1, LANES), F32)],
        compiler_params=_params(("arbitrary",)),
    )(x, out, target, g_post)


def _adamw(w, g, m, v, *, tr, name):
    rows, width = w.shape
    tr = min(tr, rows)
    assert rows % tr == 0
    c1 = 1.0 - ADAM_B1 ** ADAM_STEP
    c2 = 1.0 - ADAM_B2 ** ADAM_STEP

    def body(w_ref, g_ref, m_ref, v_ref, d_ref, nm_ref, nv_ref):
        gv = g_ref[...]
        nm = ADAM_B1 * m_ref[...] + (1.0 - ADAM_B1) * gv
        nv = ADAM_B2 * v_ref[...] + (1.0 - ADAM_B2) * (gv * gv)
        d_ref[...] = -ADAM_LR * ((nm / c1) / (jnp.sqrt(nv / c2) + ADAM_EPS) + ADAM_WD * w_ref[...])
        nm_ref[...] = nm
        nv_ref[...] = nv

    blk = pl.BlockSpec((tr, width), lambda i: (i, 0))
    return pl.pallas_call(
        body, name=name, grid=(rows // tr,), in_specs=[blk] * 4, out_specs=[blk] * 3,
        out_shape=[jax.ShapeDtypeStruct((rows, width), F32)] * 3,
        compiler_params=_params(("parallel",)),
    )(w, g, m, v)


MESH = pl.DeviceIdType.MESH
ANY = pl.BlockSpec(memory_space=pl.ANY)


def _place():
    return lax.axis_index("x"), lax.axis_index("y"), lax.axis_index("c")


def _other_chips(x, y):
    return [(1 - x, y), (x, 1 - y), (1 - x, 1 - y)]


def _remote(src, dst, send_sem, recv_sem, dev):
    return pltpu.make_async_remote_copy(src_ref=src, dst_ref=dst, send_sem=send_sem, recv_sem=recv_sem,
                                        device_id=dev, device_id_type=MESH)


def _allgather_chips(flat):
    R, W = flat.shape
    hh = R // 2

    def body(x_ref, out_ref, send_sems, recv_sems, local_sem):
        x, y, c = _place()
        sibling = (x, y, 1 - c)
        chips = _other_chips(x, y)

        def half(px, py, hc):
            return out_ref.at[2 * px + py, pl.ds(hc * hh, hh), :]

        mine = pltpu.make_async_copy(x_ref, out_ref.at[2 * x + y], local_sem)
        mine.start()
        first = [_remote(x_ref.at[pl.ds(c * hh, hh), :], half(x, y, c), send_sems.at[k], recv_sems.at[k], (px, py, c))
                 for k, (px, py) in enumerate(chips)]
        for cp in first:
            cp.start()
        passed = [_remote(half(px, py, c), half(px, py, c), send_sems.at[3 + k], recv_sems.at[3 + k], sibling)
                  for k, (px, py) in enumerate(chips)]
        for k, (px, py) in enumerate(chips):
            _remote(half(px, py, c), half(px, py, c), send_sems.at[k], recv_sems.at[k], (px, py, c)).wait_recv()
            passed[k].start()
        for k, (px, py) in enumerate(chips):
            _remote(half(px, py, 1 - c), half(px, py, 1 - c), send_sems.at[3 + k], recv_sems.at[3 + k], sibling).wait_recv()
        for cp in first + passed:
            cp.wait_send()
        mine.wait()

    return pl.pallas_call(
        body, name="allgather_weights", in_specs=[ANY], out_specs=ANY,
        out_shape=jax.ShapeDtypeStruct((N_CHIPS, R, W), flat.dtype),
        scratch_shapes=[pltpu.SemaphoreType.DMA((6,)), pltpu.SemaphoreType.DMA((6,)), pltpu.SemaphoreType.DMA],
    )(flat)


def _pair_exchange(g):
    _, R, W = g.shape
    hh = R // 2

    def body(g_ref, out_ref, send_sem, recv_sem):
        x, y, c = _place()
        cp = _remote(g_ref.at[:, pl.ds((1 - c) * hh, hh), :], out_ref, send_sem, recv_sem, (x, y, 1 - c))
        cp.start()
        cp.wait()

    return pl.pallas_call(
        body, name="grad_pair_exchange", in_specs=[ANY], out_specs=ANY,
        out_shape=jax.ShapeDtypeStruct((N_CHIPS, hh, W), g.dtype),
        scratch_shapes=[pltpu.SemaphoreType.DMA, pltpu.SemaphoreType.DMA],
    )(g)


def _pair_add(g, recv, cidx, *, tr):
    _, R, W = g.shape
    hh = R // 2
    nb = hh // tr
    assert hh % tr == 0

    def body(c_ref, g_ref, r_ref, o32_ref, o16_ref):
        s = g_ref[...] + r_ref[...]
        o32_ref[...] = s
        o16_ref[...] = s.astype(BF16)

    blk = pl.BlockSpec((1, tr, W), lambda j, i, c_ref: (j, i, 0))
    return pl.pallas_call(
        body, name="grad_pair_add",
        grid_spec=pltpu.PrefetchScalarGridSpec(
            num_scalar_prefetch=1, grid=(N_CHIPS, nb),
            in_specs=[pl.BlockSpec((1, tr, W), lambda j, i, c_ref: (j, i + c_ref[0] * nb, 0)), blk],
            out_specs=[blk, blk]),
        out_shape=[jax.ShapeDtypeStruct((N_CHIPS, hh, W), F32), jax.ShapeDtypeStruct((N_CHIPS, hh, W), BF16)],
        compiler_params=_params(("parallel", "parallel")),
    )(cidx, g, recv)


def _chip_exchange(p16):
    _, hh, W = p16.shape

    def body(p_ref, out_ref, send_sems, recv_sems):
        x, y, c = _place()
        cps = [_remote(p_ref.at[2 * px + py], out_ref.at[k], send_sems.at[k], recv_sems.at[k], (px, py, c))
               for k, (px, py) in enumerate(_other_chips(x, y))]
        for cp in cps:
            cp.start()
        for cp in cps:
            cp.wait()

    return pl.pallas_call(
        body, name="grad_chip_exchange", in_specs=[ANY], out_specs=ANY,
        out_shape=jax.ShapeDtypeStruct((3, hh, W), p16.dtype),
        scratch_shapes=[pltpu.SemaphoreType.DMA((3,)), pltpu.SemaphoreType.DMA((3,))],
    )(p16)


def _chip_add(p32, recv, jidx, *, tr):
    _, hh, W = p32.shape
    assert hh % tr == 0

    def body(j_ref, p_ref, r_ref, o_ref):
        s = p_ref[0]
        for k in range(3):
            s = s + r_ref[k].astype(F32)
        o_ref[...] = s

    return pl.pallas_call(
        body, name="grad_chip_add",
        grid_spec=pltpu.PrefetchScalarGridSpec(
            num_scalar_prefetch=1, grid=(hh // tr,),
            in_specs=[pl.BlockSpec((1, tr, W), lambda i, j_ref: (j_ref[0], i, 0)),
                      pl.BlockSpec((3, tr, W), lambda i, j_ref: (0, i, 0))],
            out_specs=pl.BlockSpec((tr, W), lambda i, j_ref: (i, 0))),
        out_shape=jax.ShapeDtypeStruct((hh, W), F32),
        compiler_params=_params(("parallel",)),
    )(jidx, p32, recv)


def _halves_exchange(t):
    hh, W = t.shape

    def body(t_ref, out_ref, send_sem, recv_sem, local_sem):
        x, y, c = _place()
        own = out_ref.at[pl.ds(c * hh, hh), :]
        mine = pltpu.make_async_copy(t_ref, own, local_sem)
        mine.start()
        cp = _remote(t_ref, own, send_sem, recv_sem, (x, y, 1 - c))
        cp.start()
        _remote(t_ref, out_ref.at[pl.ds((1 - c) * hh, hh), :], send_sem, recv_sem, (x, y, 1 - c)).wait_recv()
        cp.wait_send()
        mine.wait()

    return pl.pallas_call(
        body, name="grad_halves_exchange", in_specs=[ANY], out_specs=ANY,
        out_shape=jax.ShapeDtypeStruct((2 * hh, W), t.dtype),
        scratch_shapes=[pltpu.SemaphoreType.DMA, pltpu.SemaphoreType.DMA, pltpu.SemaphoreType.DMA],
    )(t)


def _small_allreduce(buf):
    rows, W = buf.shape

    def body(x_ref, sum_ref, gath_ref, send_sems, recv_sems):
        x, y, c = _place()
        me = 4 * x + 2 * y + c
        gath_ref[me] = x_ref[...]
        cps = []
        for k in range(1, N_DEV):
            fx, fy, fc = (k >> 2) & 1, (k >> 1) & 1, k & 1
            peer = (1 - x if fx else x, 1 - y if fy else y, 1 - c if fc else c)
            cp = _remote(x_ref, gath_ref.at[me], send_sems.at[k - 1], recv_sems.at[k - 1], peer)
            cp.start()
            cps.append((cp, 4 * peer[0] + 2 * peer[1] + peer[2]))
        for k, (cp, src) in enumerate(cps):
            _remote(x_ref, gath_ref.at[src], send_sems.at[k], recv_sems.at[k], (x, y, c)).wait_recv()
        for cp, _ in cps:
            cp.wait_send()
        acc = gath_ref[0]
        for d in range(1, N_DEV):
            acc = acc + gath_ref[d]
        sum_ref[...] = acc

    vm = pl.BlockSpec(memory_space=pltpu.VMEM)
    return pl.pallas_call(
        body, name="small_allreduce", in_specs=[vm], out_specs=vm,
        out_shape=jax.ShapeDtypeStruct((rows, W), F32),
        scratch_shapes=[pltpu.VMEM((N_DEV, rows, W), F32), pltpu.SemaphoreType.DMA((N_DEV - 1,)),
                        pltpu.SemaphoreType.DMA((N_DEV - 1,))],
        compiler_params=pltpu.CompilerParams(vmem_limit_bytes=VMEM_LIMIT),
    )(buf)


def _flat_rows(shape):
    n = math.prod(shape)
    assert n % FLAT_W == 0
    return n // FLAT_W


def _pack_flat(parts, dtype):
    return jnp.concatenate([p.astype(dtype).reshape(-1, FLAT_W) for p in parts], axis=0)


def _unpack_flat(flat, shapes):
    out, r = [], 0
    for s in shapes:
        n = _flat_rows(s)
        out.append(flat[r:r + n].reshape(s))
        r += n
    return out


SHARD_SHAPES = {"w_in": (D, IN_REF // N_CHIPS), "w_uq": (LORA, HEADS * QK_DIM // N_CHIPS),
                "w_ukv": (LORA, HEADS * (QK_NOPE + V_DIM) // N_CHIPS), "w_mem_kv": (D, 2 * D // N_CHIPS),
                "w_gate": (D, 3 * D // N_CHIPS), "w_branch": (3, D // N_CHIPS, D), "w_out": (D // N_CHIPS, D)}
SHARD_AXIS = {"w_in": 1, "w_uq": 1, "w_ukv": 1, "w_mem_kv": 1, "w_gate": 1, "w_branch": 1, "w_out": 0}


def _weights_from_gathered(gathered):
    per_chip = [_unpack_flat(gathered[j], [SHARD_SHAPES[n] for n in BIG]) for j in range(N_CHIPS)]
    full = {n: jnp.concatenate([per_chip[j][i] for j in range(N_CHIPS)], axis=SHARD_AXIS[n]) for i, n in enumerate(BIG)}
    w = full["w_in"]
    w_in = jnp.concatenate([w[:, :3 * D], w[:, 3 * D + 2 * LORA + QK_ROPE:], w[:, 3 * D:3 * D + 2 * LORA + QK_ROPE],
                            jnp.zeros((D, IN_PAD - IN_REF), w.dtype)], axis=1)
    wq = jnp.pad(full["w_uq"].reshape(LORA, HEADS, QK_DIM), ((0, 0), (0, 0), (0, QK_PAD - QK_DIM))).reshape(LORA, HEADS * QK_PAD)
    kv3 = full["w_ukv"].reshape(LORA, HEADS, QK_NOPE + V_DIM)
    wk = jnp.pad(kv3[:, :, :QK_NOPE], ((0, 0), (0, 0), (0, QK_PAD - QK_NOPE))).reshape(LORA, HEADS * QK_PAD)
    wv = kv3[:, :, QK_NOPE:].reshape(LORA, HEADS * V_DIM)
    return {"w_in": w_in, "wq": wq, "wk": wk, "wv": wv, "w_mem_kv": full["w_mem_kv"], "w_gate": full["w_gate"],
            "w_branch": full["w_branch"], "w_out": full["w_out"]}


def _grads_to_blocks(g):
    gi = g["w_in"]
    w_in = jnp.concatenate([gi[:, :3 * D], gi[:, 6 * D:6 * D + 2 * LORA + QK_ROPE], gi[:, 3 * D:6 * D]], axis=1)
    w_uq = g["wq"].reshape(LORA, HEADS, QK_PAD)[:, :, :QK_DIM].reshape(LORA, HEADS * QK_DIM)
    w_ukv = jnp.concatenate([g["wk"].reshape(LORA, HEADS, QK_PAD)[:, :, :QK_NOPE], g["wv"].reshape(LORA, HEADS, V_DIM)],
                            axis=2).reshape(LORA, HEADS * (QK_NOPE + V_DIM))
    full = {"w_in": w_in, "w_uq": w_uq, "w_ukv": w_ukv, "w_mem_kv": g["w_mem_kv"], "w_gate": g["w_gate"],
            "w_branch": g["w_branch"], "w_out": g["w_out"]}
    blocks = []
    for j in range(N_CHIPS):
        parts = []
        for n in BIG:
            ax, size = SHARD_AXIS[n], SHARD_SHAPES[n][SHARD_AXIS[n]]
            parts.append(lax.slice_in_dim(full[n], j * size, (j + 1) * size, axis=ax))
        blocks.append(_pack_flat(parts, F32))
    return jnp.stack(blocks, axis=0)


def _local_step(x, mem, pos_col, target, W, P):
    S = x.shape[0]
    h16, rstd_x = _rms_fwd(x, P["g_pre"], width=D, col=0, tm=256, name="pre_norm")
    memn16, rstd_m = _rms_fwd(mem, P["mem_norm_g"], width=D, col=0, tm=256, name="mem_norm")
    proj = _mm(h16, W["w_in"], "nn", tm=512, tn=1920, tk=D, out_dtype=F32, name="in_proj")

    causal = jnp.tril(jnp.ones((CHUNK, CHUNK), F32))
    wm = (P["a_w_s"] * causal[None]).astype(BF16)
    bs_t = P["a_b_s"].T
    ya = _gmlp_fwd(proj, P["a_ln_g"], P["a_ln_b"], wm, bs_t)

    inv = 1.0 / (ROPE_THETA ** (jnp.arange(0, QK_ROPE, 2, dtype=F32) / QK_ROPE))
    inv_lane = jnp.concatenate([inv, inv, jnp.zeros((LANES - QK_ROPE,), F32)])[None]
    tabs = _rope_tables(pos_col, inv_lane, tm=1024)
    cqn, rstd_q = _rms_fwd(proj, P["q_norm_g"], width=LORA, col=COL_CQ, tm=512, name="q_norm")
    ckvn, rstd_kv = _rms_fwd(proj, P["kv_norm_g"], width=LORA, col=COL_CKV, tm=512, name="kv_norm")
    q16, k16, v16 = _mla_proj(cqn, ckvn, proj, tabs, W["wq"], W["wk"], W["wv"], tm=256)
    o_b, yb, lse = _mla_fwd(q16, k16, v16, proj, t=512)

    kvm = _mm(memn16, W["w_mem_kv"], "nn", tm=256, tn=1024, tk=D, out_dtype=BF16, name="mem_kv")
    ym = _mem_fwd(proj, kvm, tm=512)

    wbs = [W["w_branch"][n] for n in range(3)]
    merged, g0, g1, g2, p0, p1, p2 = _gate_merge(h16, (ya, yb, ym), W["w_gate"], P["b_gate"], wbs, tm=512, tn=512)
    out = _mm(merged, W["w_out"], "nn", tm=512, tn=1024, tk=D, out_dtype=F32, name="out_proj")
    dy, dout, g_g_post, loss = _post_loss(x, out, target, P["g_post"], tm=256)

    g_w_out = _mm(merged, dout, "tn", tm=1024, tn=1024, tk=512, out_dtype=F32, name="gw_out")
    dmerged = _mm(dout, W["w_out"], "nt", tm=512, tn=1024, tk=D, out_dtype=F32, name="d_merged")
    dp0, dp1, dp2, dgpre, g_b_gate = _gate_bwd(dmerged, (g0, g1, g2), (p0, p1, p2), tm=256)
    g_w_gate = _mm(h16, dgpre, "tn", tm=1024, tn=1024, tk=512, out_dtype=F32, name="gw_gate")
    dh_gate = _mm(dgpre, W["w_gate"], "nt", tm=512, tn=D, tk=D, out_dtype=F32, name="dh_gate")
    g_w_branch = jnp.stack([_mm(y, dp, "tn", tm=1024, tn=1024, tk=512, out_dtype=F32, name=f"gw_branch{n}")
                            for n, (y, dp) in enumerate(((ya, dp0), (yb, dp1), (ym, dp2)))], axis=0)
    dya, dyb, dym = [_mm(dp, wbs[n], "nt", tm=512, tn=1024, tk=D, out_dtype=F32, name=f"dy_branch{n}")
                     for n, dp in enumerate((dp0, dp1, dp2))]

    dqm, dzm, dkvm = _mem_bwd(proj, kvm, dym, tm=512)
    dkvm16 = dkvm.astype(BF16)
    g_w_mem_kv = _mm(memn16, dkvm16, "tn", tm=1024, tn=1024, tk=256, out_dtype=F32, name="gw_mem_kv")
    dmemn = _mm(dkvm16, W["w_mem_kv"], "nt", tm=256, tn=1024, tk=2 * D, out_dtype=F32, name="d_memn")
    _, g_mem_norm = _rms_bwd(dmemn, mem, rstd_m, P["mem_norm_g"], width=D, col=0, tm=256, out_dtype=BF16, name="mem_norm_bwd")

    do16, dzb, delta = _mla_gate_bwd(dyb, o_b, proj, tm=256)
    dq, dk, dv = _mla_bwd(q16, k16, v16, do16, lse, delta, t=512)
    dq16, dk16, dkr = _mla_qk_post(dq, dk, tabs, tm=256)
    dv16 = dv.astype(BF16)
    g_wq = _mm(cqn, dq16, "tn", tm=512, tn=1024, tk=512, out_dtype=F32, name="gw_uq")
    g_wk = _mm(ckvn, dk16, "tn", tm=512, tn=1024, tk=512, out_dtype=F32, name="gw_uk")
    g_wv = _mm(ckvn, dv16, "tn", tm=512, tn=1024, tk=512, out_dtype=F32, name="gw_uv")
    dcqn = _mm(dq16, W["wq"], "nt", tm=512, tn=LORA, tk=HEADS * QK_PAD, out_dtype=F32, name="d_cqn")
    dckvn_k = _mm(dk16, W["wk"], "nt", tm=512, tn=LORA, tk=HEADS * QK_PAD, out_dtype=F32, name="d_ckvn_k")
    dckvn = _mm(dv16, W["wv"], "nt", tm=512, tn=LORA, tk=HEADS * V_DIM, out_dtype=F32, name="d_ckvn", add=dckvn_k)
    dcq, g_q_norm = _rms_bwd(dcqn, proj, rstd_q, P["q_norm_g"], width=LORA, col=COL_CQ, tm=512, out_dtype=BF16, name="q_norm_bwd")
    dckv, g_kv_norm = _rms_bwd(dckvn, proj, rstd_kv, P["kv_norm_g"], width=LORA, col=COL_CKV, tm=512, out_dtype=BF16, name="kv_norm_bwd")

    du, dvr, dza, gws, dsv_sum, g_ln_g, g_ln_b = _gmlp_bwd(proj, dya, P["a_ln_g"], P["a_ln_b"], wm, bs_t)
    g_a_w_s = gws * causal[None]
    g_a_b_s = dsv_sum.reshape(CHUNK, A_GROUPS, CHUNK).sum(axis=-1).T

    dproj = jnp.concatenate([du, dvr, dza, dzb, dqm, dzm, dcq, dckv, dkr], axis=1)
    g_w_in = _mm(h16, dproj, "tn", tm=1024, tn=1920, tk=512, out_dtype=F32, name="gw_in")
    dh = _mm(dproj, W["w_in"], "nt", tm=512, tn=D, tk=1920, out_dtype=F32, name="d_h", add=dh_gate)
    grad_x, g_g_pre = _rms_bwd(dh, x, rstd_x, P["g_pre"], width=D, col=0, tm=256, out_dtype=F32, name="pre_norm_bwd", residual=dy)

    big = {"w_in": g_w_in, "wq": g_wq, "wk": g_wk, "wv": g_wv, "w_mem_kv": g_w_mem_kv, "w_gate": g_w_gate,
           "w_branch": g_w_branch, "w_out": g_w_out}
    small = {"g_pre": g_g_pre, "a_ln_g": g_ln_g, "a_ln_b": g_ln_b, "a_w_s": g_a_w_s, "a_b_s": g_a_b_s,
             "q_norm_g": g_q_norm, "kv_norm_g": g_kv_norm, "mem_norm_g": g_mem_norm, "b_gate": g_b_gate, "g_post": g_g_post}
    return loss, grad_x, big, small


SMALL_ROWS = 2208


def _pack_small(parts, loss_row):
    flat = jnp.concatenate([p.reshape(-1) for p in parts] + ([loss_row.reshape(-1)] if loss_row is not None else []))
    return jnp.pad(flat, (0, SMALL_ROWS * LANES - flat.shape[0])).reshape(SMALL_ROWS, LANES)


def kernel(x, mem, positions, g_pre, w_in, a_ln_g, a_ln_b, a_w_s, a_b_s, q_norm_g, w_uq, kv_norm_g, w_ukv, mem_norm_g, w_mem_kv, w_gate, b_gate, w_branch, w_out, g_post, loss_target, m_g_pre, m_w_in, m_a_ln_g, m_a_ln_b, m_a_w_s, m_a_b_s, m_q_norm_g, m_w_uq, m_kv_norm_g, m_w_ukv, m_mem_norm_g, m_w_mem_kv, m_w_gate, m_b_gate, m_w_branch, m_w_out, m_g_post, v_g_pre, v_w_in, v_a_ln_g, v_a_ln_b, v_a_w_s, v_a_b_s, v_q_norm_g, v_w_uq, v_kv_norm_g, v_w_ukv, v_mem_norm_g, v_w_mem_kv, v_w_gate, v_b_gate, v_w_branch, v_w_out, v_g_post):
    w = dict(g_pre=g_pre, w_in=w_in, a_ln_g=a_ln_g, a_ln_b=a_ln_b, a_w_s=a_w_s, a_b_s=a_b_s, q_norm_g=q_norm_g, w_uq=w_uq,
             kv_norm_g=kv_norm_g, w_ukv=w_ukv, mem_norm_g=mem_norm_g, w_mem_kv=w_mem_kv, w_gate=w_gate, b_gate=b_gate,
             w_branch=w_branch, w_out=w_out, g_post=g_post)
    m = dict(g_pre=m_g_pre, w_in=m_w_in, a_ln_g=m_a_ln_g, a_ln_b=m_a_ln_b, a_w_s=m_a_w_s, a_b_s=m_a_b_s, q_norm_g=m_q_norm_g,
             w_uq=m_w_uq, kv_norm_g=m_kv_norm_g, w_ukv=m_w_ukv, mem_norm_g=m_mem_norm_g, w_mem_kv=m_w_mem_kv, w_gate=m_w_gate,
             b_gate=m_b_gate, w_branch=m_w_branch, w_out=m_w_out, g_post=m_g_post)
    v = dict(g_pre=v_g_pre, w_in=v_w_in, a_ln_g=v_a_ln_g, a_ln_b=v_a_ln_b, a_w_s=v_a_w_s, a_b_s=v_a_b_s, q_norm_g=v_q_norm_g,
             w_uq=v_w_uq, kv_norm_g=v_kv_norm_g, w_ukv=v_w_ukv, mem_norm_g=v_mem_norm_g, w_mem_kv=v_w_mem_kv, w_gate=v_w_gate,
             b_gate=v_b_gate, w_branch=v_w_branch, w_out=v_w_out, g_post=v_g_post)

    gathered = _allgather_chips(_pack_flat([w[n][0] for n in BIG], BF16))
    W = _weights_from_gathered(gathered)
    P = {n: w[n][0] for n in SMALL}
    for n in ("g_pre", "a_ln_g", "a_ln_b", "q_norm_g", "kv_norm_g", "mem_norm_g", "b_gate", "g_post"):
        P[n] = P[n].reshape(1, -1)

    S = x.shape[1]
    loss_row, grad_x, big, small = _local_step(x[0], mem[0], positions.reshape(S, 1), loss_target[0], W, P)

    cidx = lax.axis_index("c").astype(jnp.int32).reshape(1)
    jidx = (2 * lax.axis_index("x") + lax.axis_index("y")).astype(jnp.int32).reshape(1)
    blocks = _grads_to_blocks(big)
    from_sibling = _pair_exchange(blocks)
    p32, p16 = _pair_add(blocks, from_sibling, cidx, tr=1120)
    from_chips = _chip_exchange(p16)
    half_total = _chip_add(p32, from_chips, jidx, tr=1120)
    g_flat = _halves_exchange(half_total)
    w_flat, m_flat, v_flat = (_pack_flat([t[n][0] for n in BIG], F32) for t in (w, m, v))
    d_flat, nm_flat, nv_flat = _adamw(w_flat, g_flat, m_flat, v_flat, tr=1120, name="adamw_big")
    shard_shapes = [SHARD_SHAPES[n] for n in BIG]
    res = {}
    for key, flat in (("grad", g_flat), ("delta", d_flat), ("new_m", nm_flat), ("new_v", nv_flat)):
        for n, t in zip(BIG, _unpack_flat(flat, shard_shapes)):
            res[key, n] = t[None]

    g_small = _small_allreduce(_pack_small([small[n] for n in SMALL], loss_row))
    ws, ms, vs = (_pack_small([t[n] for n in SMALL], None) for t in (w, m, v))
    d_small, nm_small, nv_small = _adamw(ws, g_small, ms, vs, tr=SMALL_ROWS, name="adamw_small")
    for key, flat in (("grad", g_small), ("delta", d_small), ("new_m", nm_small), ("new_v", nv_small)):
        off = 0
        fl = flat.reshape(-1)
        for n in SMALL:
            size = math.prod(w[n].shape)
            res[key, n] = fl[off:off + size].reshape(w[n].shape)
            off += size
    n_small = sum(math.prod(w[n].shape) for n in SMALL)
    loss = g_small.reshape(-1)[n_small]

    outs = [loss, grad_x[None]]
    for key in ("grad", "delta", "new_m", "new_v"):
        outs += [res[key, n] for n in WEIGHTS]
    return tuple(outs)
```

```python
import functools
import math

import jax
import jax.numpy as jnp
from jax import lax
from jax.experimental import pallas as pl
from jax.experimental.pallas import tpu as pltpu

F32 = jnp.float32
BF16 = jnp.bfloat16

D = 2048
EPS = 1e-6
CHUNK = 128
A_GROUPS = 16
HEADS = 16
QK_NOPE = 128
QK_ROPE = 64
QK_DIM = QK_NOPE + QK_ROPE
V_DIM = 128
LORA = 512
MEM_HEADS = 4
MEM_HEAD_DIM = 512
ROPE_THETA = 10000.0
QK_PAD = 256
IN_REF = 13376
IN_PAD = 13440
COL_U, COL_V, COL_ZA, COL_ZB, COL_QM, COL_ZM = 0, 1, 2, 3, 4, 5
COL_CQ, COL_CKV = 24, 25
COL_KR = 104

ADAM_LR = 0.001
ADAM_B1 = 0.9
ADAM_B2 = 0.999
ADAM_EPS = 1e-08
ADAM_WD = 0.01
ADAM_STEP = 10

VMEM_LIMIT = 56 * 1024 * 1024
LANES = 128

BIG = ("w_in", "w_uq", "w_ukv", "w_mem_kv", "w_gate", "w_branch", "w_out")
SMALL = ("g_pre", "a_ln_g", "a_ln_b", "a_w_s", "a_b_s", "q_norm_g", "kv_norm_g", "mem_norm_g", "b_gate", "g_post")
WEIGHTS = ("g_pre", "w_in", "a_ln_g", "a_ln_b", "a_w_s", "a_b_s", "q_norm_g", "w_uq", "kv_norm_g", "w_ukv",
           "mem_norm_g", "w_mem_kv", "w_gate", "b_gate", "w_branch", "w_out", "g_post")
FLAT_W = 512
N_CHIPS = 4
N_DEV = 8


def _params(sem=None):
    return pltpu.CompilerParams(dimension_semantics=sem, vmem_limit_bytes=VMEM_LIMIT)


def _sigmoid(z):
    return 1.0 / (1.0 + jnp.exp(-z))


def _gelu_parts(x):
    c = math.sqrt(2.0 / math.pi)
    x2 = x * x
    t = jnp.tanh(c * (x + 0.044715 * x * x2))
    g = 0.5 * x * (1.0 + t)
    dg = 0.5 * (1.0 + t) + 0.5 * x * (1.0 - t * t) * (c * (1.0 + 3.0 * 0.044715 * x2))
    return g, dg


def _silu_parts(z):
    s = _sigmoid(z)
    return z * s, s * (1.0 + z * (1.0 - s))


def _dot(a, b, dims):
    return lax.dot_general(a, b, (dims, ((), ())), preferred_element_type=F32)


NN = ((1,), (0,))
NT = ((1,), (1,))
TN = ((0,), (0,))


def _mm(a, b, mode, *, tm, tn, tk, out_dtype, name, add=None):
    if mode == "nn":
        (M, K), (_, N) = a.shape, b.shape
    elif mode == "nt":
        (M, K), (N, _) = a.shape, b.shape
    else:
        (K, M), (_, N) = a.shape, b.shape
    tm, tn, tk = min(tm, M), min(tn, N), min(tk, K)
    assert M % tm == 0 and N % tn == 0 and K % tk == 0, (name, M, N, K, tm, tn, tk)
    ni, nj, nk = M // tm, N // tn, K // tk
    dims = {"nn": NN, "nt": NT, "tn": TN}[mode]
    has_add = add is not None

    def body(*refs):
        a_ref, b_ref = refs[0], refs[1]
        add_ref = refs[2] if has_add else None
        o_ref = refs[3] if has_add else refs[2]
        part = _dot(a_ref[...].astype(BF16), b_ref[...].astype(BF16), dims)

        def finish(r):
            if has_add:
                r = r + add_ref[...]
            o_ref[...] = r.astype(out_dtype)

        if nk == 1:
            finish(part)
        else:
            acc = refs[-1]
            k = pl.program_id(2)

            @pl.when(k == 0)
            def _():
                acc[...] = part

            @pl.when(k > 0)
            def _():
                acc[...] += part

            @pl.when(k == nk - 1)
            def _():
                finish(acc[...])

    if mode == "nn":
        a_spec = pl.BlockSpec((tm, tk), lambda j, i, k: (i, k))
        b_spec = pl.BlockSpec((tk, tn), lambda j, i, k: (k, j))
    elif mode == "nt":
        a_spec = pl.BlockSpec((tm, tk), lambda j, i, k: (i, k))
        b_spec = pl.BlockSpec((tn, tk), lambda j, i, k: (j, k))
    else:
        a_spec = pl.BlockSpec((tk, tm), lambda j, i, k: (k, i))
        b_spec = pl.BlockSpec((tk, tn), lambda j, i, k: (k, j))
    o_spec = pl.BlockSpec((tm, tn), lambda j, i, k: (i, j))
    in_specs = [a_spec, b_spec] + ([o_spec] if has_add else [])
    args = (a, b) + ((add,) if has_add else ())
    return pl.pallas_call(
        body, name=name, grid=(nj, ni, nk), in_specs=in_specs, out_specs=o_spec,
        out_shape=jax.ShapeDtypeStruct((M, N), out_dtype),
        scratch_shapes=[pltpu.VMEM((tm, tn), F32)] if nk > 1 else [],
        compiler_params=_params(("parallel", "parallel", "arbitrary")),
    )(*args)


def _rms_fwd(x, g, *, width, col, tm, name):
    rows = x.shape[0]
    tm = min(tm, rows)

    def body(x_ref, g_ref, y_ref, r_ref):
        xv = x_ref[...]
        r = lax.rsqrt(jnp.mean(xv * xv, axis=-1, keepdims=True) + EPS)
        y_ref[...] = ((xv * r) * g_ref[...]).astype(BF16)
        r_ref[...] = r

    return pl.pallas_call(
        body, name=name, grid=(rows // tm,),
        in_specs=[pl.BlockSpec((tm, width), lambda i: (i, col)), pl.BlockSpec((1, width), lambda i: (0, 0))],
        out_specs=[pl.BlockSpec((tm, width), lambda i: (i, 0)), pl.BlockSpec((tm, 1), lambda i: (i, 0))],
        out_shape=[jax.ShapeDtypeStruct((rows, width), BF16), jax.ShapeDtypeStruct((rows, 1), F32)],
        compiler_params=_params(("parallel",)),
    )(x, g)


def _rms_bwd(d, x, rstd, g, *, width, col, tm, out_dtype, name, residual=None):
    rows = d.shape[0]
    tm = min(tm, rows)
    has_res = residual is not None

    def body(*refs):
        d_ref, x_ref, r_ref, g_ref = refs[:4]
        res_ref = refs[4] if has_res else None
        dx_ref, gg_ref = refs[-2], refs[-1]
        dv = d_ref[...]
        n = x_ref[...] * r_ref[...]

        @pl.when(pl.program_id(0) == 0)
        def _():
            gg_ref[...] = jnp.zeros_like(gg_ref)

        gg_ref[...] += jnp.sum(dv * n, axis=0, keepdims=True)
        gd = dv * g_ref[...]
        dx = r_ref[...] * (gd - n * jnp.mean(gd * n, axis=-1, keepdims=True))
        if has_res:
            dx = dx + res_ref[...]
        dx_ref[...] = dx.astype(out_dtype)

    blk = pl.BlockSpec((tm, width), lambda i: (i, 0))
    in_specs = [blk, pl.BlockSpec((tm, width), lambda i: (i, col)), pl.BlockSpec((tm, 1), lambda i: (i, 0)),
                pl.BlockSpec((1, width), lambda i: (0, 0))] + ([blk] if has_res else [])
    args = (d, x, rstd, g) + ((residual,) if has_res else ())
    return pl.pallas_call(
        body, name=name, grid=(rows // tm,), in_specs=in_specs,
        out_specs=[blk, pl.BlockSpec((1, width), lambda i: (0, 0))],
        out_shape=[jax.ShapeDtypeStruct((rows, width), out_dtype), jax.ShapeDtypeStruct((1, width), F32)],
        compiler_params=_params(("arbitrary",)),
    )(*args)


def _rope_tables(pos_col, inv_lane, *, tm):
    rows = pos_col.shape[0]
    tm = min(tm, rows)

    def body(p_ref, f_ref, c_ref, s1_ref, s2_ref):
        ang = p_ref[...].astype(F32) * f_ref[...]
        lane = lax.broadcasted_iota(jnp.int32, ang.shape, 1)
        c, s = jnp.cos(ang), jnp.sin(ang)
        half = QK_ROPE // 2
        c_ref[...] = jnp.where(lane < QK_ROPE, c, 0.0)
        s1_ref[...] = jnp.where(lane < half, -s, 0.0)
        s2_ref[...] = jnp.where((lane >= half) & (lane < QK_ROPE), s, 0.0)

    blk = pl.BlockSpec((tm, LANES), lambda i: (i, 0))
    return pl.pallas_call(
        body, name="rope_tables", grid=(rows // tm,),
        in_specs=[pl.BlockSpec((tm, 1), lambda i: (i, 0)), pl.BlockSpec((1, LANES), lambda i: (0, 0))],
        out_specs=[blk, blk, blk], out_shape=[jax.ShapeDtypeStruct((rows, LANES), F32)] * 3,
        compiler_params=_params(("parallel",)),
    )(pos_col, inv_lane)


def _rot(t, c, s1, s2, sign):
    r1 = pltpu.roll(t, LANES - QK_ROPE // 2, 1) * s1
    r2 = pltpu.roll(t, QK_ROPE // 2, 1) * s2
    return t * c + (r1 + r2) if sign > 0 else t * c - (r1 + r2)


def _mla_proj(cqn, ckvn, proj, tabs, wq, wk, wv, *, tm):
    rows = cqn.shape[0]
    tm = min(tm, rows)

    def body(cq_ref, ckv_ref, kr_ref, c_ref, s1_ref, s2_ref, wq_ref, wk_ref, wv_ref, q_ref, k_ref, v_ref):
        c, s1, s2 = c_ref[...], s1_ref[...], s2_ref[...]
        q = _dot(cq_ref[...], wq_ref[...], NN)
        k = _dot(ckv_ref[...], wk_ref[...], NN)
        kpe = _rot(kr_ref[...], c, s1, s2, 1).astype(BF16)
        for h in range(HEADS):
            lo = h * QK_PAD
            q_ref[:, lo:lo + QK_NOPE] = q[:, lo:lo + QK_NOPE].astype(BF16)
            q_ref[:, lo + QK_NOPE:lo + QK_PAD] = _rot(q[:, lo + QK_NOPE:lo + QK_PAD], c, s1, s2, 1).astype(BF16)
            k_ref[:, lo:lo + QK_NOPE] = k[:, lo:lo + QK_NOPE].astype(BF16)
            k_ref[:, lo + QK_NOPE:lo + QK_PAD] = kpe
        v_ref[...] = _dot(ckv_ref[...], wv_ref[...], NN).astype(BF16)

    def row(w):
        return pl.BlockSpec((tm, w), lambda i: (i, 0))

    def whole(w):
        return pl.BlockSpec(w.shape, lambda i: (0, 0))

    return pl.pallas_call(
        body, name="mla_proj", grid=(rows // tm,),
        in_specs=[row(LORA), row(LORA), pl.BlockSpec((tm, LANES), lambda i: (i, COL_KR)), row(LANES), row(LANES), row(LANES),
                  whole(wq), whole(wk), whole(wv)],
        out_specs=[row(HEADS * QK_PAD), row(HEADS * QK_PAD), row(HEADS * V_DIM)],
        out_shape=[jax.ShapeDtypeStruct((rows, HEADS * QK_PAD), BF16), jax.ShapeDtypeStruct((rows, HEADS * QK_PAD), BF16),
                   jax.ShapeDtypeStruct((rows, HEADS * V_DIM), BF16)],
        compiler_params=_params(("parallel",)),
    )(cqn, ckvn, proj, *tabs, wq, wk, wv)


def _mla_fwd(q, k, v, proj, *, t):
    S = q.shape[0]
    t = min(t, S)
    n = S // t
    scale = QK_DIM ** -0.5

    def body(q_ref, k_ref, v_ref, z_ref, o_ref, y_ref, lse_ref):
        qi = pl.program_id(1)
        qv = q_ref[...]

        def block(j, carry, diag):
            m_old, l_old, acc = carry
            ks = pl.ds(pl.multiple_of(j * t, t), t)
            s = _dot(qv, k_ref[ks, :], NT) * scale
            if diag:
                r = lax.broadcasted_iota(jnp.int32, s.shape, 0)
                c = lax.broadcasted_iota(jnp.int32, s.shape, 1)
                s = jnp.where(c <= r, s, -1e30)
            m_new = jnp.maximum(m_old, jnp.max(s, axis=-1, keepdims=True))
            alpha = jnp.exp(m_old - m_new)
            p = jnp.exp(s - m_new)
            l_new = alpha * l_old + jnp.sum(p, axis=-1, keepdims=True)
            acc = alpha * acc + _dot(p.astype(BF16), v_ref[ks, :], NN)
            return m_new, l_new, acc

        init = (jnp.full((t, 1), -jnp.inf, F32), jnp.zeros((t, 1), F32), jnp.zeros((t, V_DIM), F32))
        carry = lax.fori_loop(0, qi, lambda j, cr: block(j, cr, False), init)
        m_f, l_f, acc = block(qi, carry, True)
        o = acc / l_f
        o_ref[...] = o
        sz, _ = _silu_parts(z_ref[...])
        y_ref[...] = (o * sz).astype(BF16)
        lse_ref[0] = m_f + jnp.log(l_f)

    zcol = COL_ZB * (D // V_DIM)
    return pl.pallas_call(
        body, name="mla_fwd", grid=(HEADS, n),
        in_specs=[pl.BlockSpec((t, QK_PAD), lambda h, i: (i, h)),
                  pl.BlockSpec((S, QK_PAD), lambda h, i: (0, h)),
                  pl.BlockSpec((S, V_DIM), lambda h, i: (0, h)),
                  pl.BlockSpec((t, V_DIM), lambda h, i: (i, zcol + h))],
        out_specs=[pl.BlockSpec((t, V_DIM), lambda h, i: (i, h)), pl.BlockSpec((t, V_DIM), lambda h, i: (i, h)),
                   pl.BlockSpec((1, t, 1), lambda h, i: (h, i, 0))],
        out_shape=[jax.ShapeDtypeStruct((S, HEADS * V_DIM), F32), jax.ShapeDtypeStruct((S, HEADS * V_DIM), BF16),
                   jax.ShapeDtypeStruct((HEADS, S, 1), F32)],
        compiler_params=_params(("parallel", "parallel")),
    )(q, k, v, proj)


def _mla_gate_bwd(dy, o, proj, lse, *, tm):
    S = dy.shape[0]
    tm = min(tm, S)

    def body(dy_ref, o_ref, z_ref, lse_ref, do_ref, dz_ref, st_ref):
        sz, dsz = _silu_parts(z_ref[...])
        dyv, ov = dy_ref[...], o_ref[...]
        do = dyv * sz
        do_ref[...] = do.astype(BF16)
        dz_ref[...] = (dyv * ov * dsz).astype(BF16)
        prod = do * ov
        lane = lax.broadcasted_iota(jnp.int32, (tm, LANES), 1)
        for h in range(HEADS):
            delta = jnp.sum(prod[:, h * V_DIM:(h + 1) * V_DIM], axis=-1, keepdims=True)
            st_ref[h] = jnp.where(lane == 0, lse_ref[h], jnp.where(lane == 1, delta, 0.0))

    blk = pl.BlockSpec((tm, D), lambda i: (i, 0))
    return pl.pallas_call(
        body, name="mla_gate_bwd", grid=(S // tm,),
        in_specs=[blk, blk, pl.BlockSpec((tm, D), lambda i: (i, COL_ZB)), pl.BlockSpec((HEADS, tm, 1), lambda i: (0, i, 0))],
        out_specs=[blk, blk, pl.BlockSpec((HEADS, tm, LANES), lambda i: (0, i, 0))],
        out_shape=[jax.ShapeDtypeStruct((S, D), BF16), jax.ShapeDtypeStruct((S, D), BF16),
                   jax.ShapeDtypeStruct((HEADS, S, LANES), F32)],
        compiler_params=_params(("parallel",)),
    )(dy, o, proj, lse)


def _mla_bwd(q, k, v, do, stats, *, t):
    S = q.shape[0]
    t = min(t, S)
    n = S // t
    scale = QK_DIM ** -0.5

    def body(q_ref, k_ref, v_ref, do_ref, st_ref, dq_ref, dk_ref, dv_ref):
        ki = pl.program_id(1)

        @pl.when(ki == 0)
        def _():
            dq_ref[...] = jnp.zeros_like(dq_ref)

        kv, vv = k_ref[...], v_ref[...]

        def block(i, carry, diag):
            dk, dv = carry
            rows = pl.ds(pl.multiple_of(i * t, t), t)
            qv, dov, st = q_ref[rows, :], do_ref[rows, :], st_ref[0, rows, :]
            s = _dot(qv, kv, NT) * scale
            if diag:
                r = lax.broadcasted_iota(jnp.int32, s.shape, 0)
                c = lax.broadcasted_iota(jnp.int32, s.shape, 1)
                s = jnp.where(c <= r, s, -1e30)
            p = jnp.exp(s - st[:, 0:1])
            p16 = p.astype(BF16)
            dv = dv + _dot(p16, dov, TN)
            dp = _dot(dov, vv, NT)
            ds = (p * (dp - st[:, 1:2]) * scale).astype(BF16)
            dk = dk + _dot(ds, qv, TN)
            dq_ref[rows, :] += _dot(ds, kv, NN)
            return dk, dv

        carry = block(ki, (jnp.zeros((t, QK_PAD), F32), jnp.zeros((t, V_DIM), F32)), True)
        dk, dv = lax.fori_loop(ki + 1, n, lambda i, cr: block(i, cr, False), carry)
        dk_ref[...] = dk
        dv_ref[...] = dv

    return pl.pallas_call(
        body, name="mla_bwd", grid=(HEADS, n),
        in_specs=[pl.BlockSpec((S, QK_PAD), lambda h, j: (0, h)),
                  pl.BlockSpec((t, QK_PAD), lambda h, j: (j, h)),
                  pl.BlockSpec((t, V_DIM), lambda h, j: (j, h)),
                  pl.BlockSpec((S, V_DIM), lambda h, j: (0, h)),
                  pl.BlockSpec((1, S, LANES), lambda h, j: (h, 0, 0))],
        out_specs=[pl.BlockSpec((S, QK_PAD), lambda h, j: (0, h)),
                   pl.BlockSpec((t, QK_PAD), lambda h, j: (j, h)),
                   pl.BlockSpec((t, V_DIM), lambda h, j: (j, h))],
        out_shape=[jax.ShapeDtypeStruct((S, HEADS * QK_PAD), F32), jax.ShapeDtypeStruct((S, HEADS * QK_PAD), F32),
                   jax.ShapeDtypeStruct((S, HEADS * V_DIM), F32)],
        compiler_params=_params(("arbitrary", "arbitrary")),
    )(q, k, v, do, stats)


def _mla_qk_post(dq, dk, tabs, *, tm):
    S = dq.shape[0]
    tm = min(tm, S)

    def body(dq_ref, dk_ref, c_ref, s1_ref, s2_ref, q16_ref, k16_ref, kr_ref):
        c, s1, s2 = c_ref[...], s1_ref[...], s2_ref[...]
        kpe = jnp.zeros((tm, LANES), F32)
        for h in range(HEADS):
            lo = h * QK_PAD
            q16_ref[:, lo:lo + QK_NOPE] = dq_ref[:, lo:lo + QK_NOPE].astype(BF16)
            q16_ref[:, lo + QK_NOPE:lo + QK_PAD] = _rot(dq_ref[:, lo + QK_NOPE:lo + QK_PAD], c, s1, s2, -1).astype(BF16)
            kpe = kpe + dk_ref[:, lo + QK_NOPE:lo + QK_PAD]
        k16_ref[...] = dk_ref[...].astype(BF16)
        kr_ref[...] = _rot(kpe, c, s1, s2, -1).astype(BF16)

    wide = pl.BlockSpec((tm, HEADS * QK_PAD), lambda i: (i, 0))
    lane = pl.BlockSpec((tm, LANES), lambda i: (i, 0))
    return pl.pallas_call(
        body, name="mla_qk_post", grid=(S // tm,),
        in_specs=[wide, wide, lane, lane, lane], out_specs=[wide, wide, lane],
        out_shape=[jax.ShapeDtypeStruct((S, HEADS * QK_PAD), BF16), jax.ShapeDtypeStruct((S, HEADS * QK_PAD), BF16),
                   jax.ShapeDtypeStruct((S, LANES), BF16)],
        compiler_params=_params(("parallel",)),
    )(dq, dk, *tabs)


def _mem_scores(q16, km_ref, h):
    lo = h * MEM_HEAD_DIM
    s = _dot(q16, km_ref[:, lo:lo + MEM_HEAD_DIM], NT) * (MEM_HEAD_DIM ** -0.5)
    e = jnp.exp(s - jnp.max(s, axis=-1, keepdims=True))
    return e / jnp.sum(e, axis=-1, keepdims=True)


def _mem_fwd(proj, kvm, *, tm):
    S = proj.shape[0]
    tm = min(tm, S)
    M = kvm.shape[0]

    def body(q_ref, z_ref, km_ref, vm_ref, y_ref):
        sz, _ = _silu_parts(z_ref[...])
        for h in range(MEM_HEADS):
            lo = h * MEM_HEAD_DIM
            p = _mem_scores(q_ref[:, lo:lo + MEM_HEAD_DIM].astype(BF16), km_ref, h)
            o = _dot(p.astype(BF16), vm_ref[:, lo:lo + MEM_HEAD_DIM], NN)
            y_ref[:, lo:lo + MEM_HEAD_DIM] = (o * sz[:, lo:lo + MEM_HEAD_DIM]).astype(BF16)

    return pl.pallas_call(
        body, name="mem_fwd", grid=(S // tm,),
        in_specs=[pl.BlockSpec((tm, D), lambda i: (i, COL_QM)), pl.BlockSpec((tm, D), lambda i: (i, COL_ZM)),
                  pl.BlockSpec((M, D), lambda i: (0, 0)), pl.BlockSpec((M, D), lambda i: (0, 1))],
        out_specs=pl.BlockSpec((tm, D), lambda i: (i, 0)),
        out_shape=jax.ShapeDtypeStruct((S, D), BF16),
        compiler_params=_params(("parallel",)),
    )(proj, proj, kvm, kvm)


def _mem_bwd(proj, kvm, dy, *, tm):
    S = proj.shape[0]
    tm = min(tm, S)
    M = kvm.shape[0]
    scale = MEM_HEAD_DIM ** -0.5

    def body(q_ref, z_ref, km_ref, vm_ref, dy_ref, dq_ref, dz_ref, dkv_ref):
        @pl.when(pl.program_id(0) == 0)
        def _():
            dkv_ref[...] = jnp.zeros_like(dkv_ref)

        sz, dsz = _silu_parts(z_ref[...])
        dyv = dy_ref[...]
        for h in range(MEM_HEADS):
            lo = h * MEM_HEAD_DIM
            sl = slice(lo, lo + MEM_HEAD_DIM)
            q16 = q_ref[:, sl].astype(BF16)
            p = _mem_scores(q16, km_ref, h)
            p16 = p.astype(BF16)
            o = _dot(p16, vm_ref[:, sl], NN)
            dy_h = dyv[:, sl]
            dz_ref[:, sl] = (dy_h * o * dsz[:, sl]).astype(BF16)
            do16 = (dy_h * sz[:, sl]).astype(BF16)
            dp = _dot(do16, vm_ref[:, sl], NT)
            ds = (p * (dp - jnp.sum(dp * p, axis=-1, keepdims=True)) * scale).astype(BF16)
            dq_ref[:, sl] = _dot(ds, km_ref[:, sl], NN).astype(BF16)
            dkv_ref[:, sl] += _dot(ds, q16, TN)
            dkv_ref[:, D + lo:D + lo + MEM_HEAD_DIM] += _dot(p16, do16, TN)

    blk = pl.BlockSpec((tm, D), lambda i: (i, 0))
    return pl.pallas_call(
        body, name="mem_bwd", grid=(S // tm,),
        in_specs=[pl.BlockSpec((tm, D), lambda i: (i, COL_QM)), pl.BlockSpec((tm, D), lambda i: (i, COL_ZM)),
                  pl.BlockSpec((M, D), lambda i: (0, 0)), pl.BlockSpec((M, D), lambda i: (0, 1)), blk],
        out_specs=[blk, blk, pl.BlockSpec((M, 2 * D), lambda i: (0, 0))],
        out_shape=[jax.ShapeDtypeStruct((S, D), BF16), jax.ShapeDtypeStruct((S, D), BF16),
                   jax.ShapeDtypeStruct((M, 2 * D), F32)],
        compiler_params=_params(("arbitrary",)),
    )(proj, proj, kvm, kvm, dy)


def _gmlp_common(u_ref, v_ref, lng_ref, lnb_ref):
    u, du = _gelu_parts(u_ref[...])
    vg, dvg = _gelu_parts(v_ref[...])
    mu = jnp.mean(vg, axis=-1, keepdims=True)
    vc = vg - mu
    r = lax.rsqrt(jnp.mean(vc * vc, axis=-1, keepdims=True) + EPS)
    vhat = vc * r
    vn = vhat * lng_ref[...] + lnb_ref[...]
    return u, du, dvg, r, vhat, vn.astype(BF16)


def _gmlp_fwd(proj, ln_g, ln_b, wm, bs_t):
    S = proj.shape[0]

    def body(u_ref, v_ref, z_ref, lng_ref, lnb_ref, wm_ref, bs_ref, y_ref):
        u, _, _, _, _, v16 = _gmlp_common(u_ref, v_ref, lng_ref, lnb_ref)
        sz, _ = _silu_parts(z_ref[...])
        for g in range(A_GROUPS):
            sl = slice(g * CHUNK, (g + 1) * CHUNK)
            sv = _dot(wm_ref[g], v16[:, sl], NN) + bs_ref[:, g:g + 1]
            y_ref[:, sl] = (u[:, sl] * sv * sz[:, sl]).astype(BF16)

    def col(c):
        return pl.BlockSpec((CHUNK, D), lambda i: (i, c))

    vec = pl.BlockSpec((1, D), lambda i: (0, 0))
    return pl.pallas_call(
        body, name="gmlp_fwd", grid=(S // CHUNK,),
        in_specs=[col(COL_U), col(COL_V), col(COL_ZA), vec, vec,
                  pl.BlockSpec((A_GROUPS, CHUNK, CHUNK), lambda i: (0, 0, 0)), pl.BlockSpec((CHUNK, A_GROUPS), lambda i: (0, 0))],
        out_specs=col(0), out_shape=jax.ShapeDtypeStruct((S, D), BF16),
        compiler_params=_params(("parallel",)),
    )(proj, proj, proj, ln_g, ln_b, wm, bs_t)


def _gmlp_bwd(proj, dy, ln_g, ln_b, wm, bs_t):
    S = proj.shape[0]

    def body(u_ref, v_ref, z_ref, dy_ref, lng_ref, lnb_ref, wm_ref, bs_ref,
             du_ref, dv_ref, dz_ref, gws_ref, dsv_ref, glg_ref, glb_ref, dvn_s):
        @pl.when(pl.program_id(0) == 0)
        def _():
            gws_ref[...] = jnp.zeros_like(gws_ref)
            dsv_ref[...] = jnp.zeros_like(dsv_ref)
            glg_ref[...] = jnp.zeros_like(glg_ref)
            glb_ref[...] = jnp.zeros_like(glb_ref)

        u, du, dvg, r, vhat, v16 = _gmlp_common(u_ref, v_ref, lng_ref, lnb_ref)
        sz, dsz = _silu_parts(z_ref[...])
        dyv = dy_ref[...]
        for g in range(A_GROUPS):
            sl = slice(g * CHUNK, (g + 1) * CHUNK)
            sv = _dot(wm_ref[g], v16[:, sl], NN) + bs_ref[:, g:g + 1]
            dy_g, u_g, sz_g = dyv[:, sl], u[:, sl], sz[:, sl]
            dsv = dy_g * u_g * sz_g
            du_ref[:, sl] = (dy_g * sv * sz_g * du[:, sl]).astype(BF16)
            dz_ref[:, sl] = (dy_g * u_g * sv * dsz[:, sl]).astype(BF16)
            dsv16 = dsv.astype(BF16)
            dvn_s[:, sl] = _dot(wm_ref[g], dsv16, TN)
            gws_ref[g] += _dot(dsv16, v16[:, sl], NT)
            dsv_ref[:, sl] += dsv
        dvn = dvn_s[...]
        glb_ref[...] += jnp.sum(dvn, axis=0, keepdims=True)
        glg_ref[...] += jnp.sum(dvn * vhat, axis=0, keepdims=True)
        dvh = dvn * lng_ref[...]
        dvc = r * (dvh - jnp.mean(dvh, axis=-1, keepdims=True) - vhat * jnp.mean(dvh * vhat, axis=-1, keepdims=True))
        dv_ref[...] = (dvc * dvg).astype(BF16)

    def col(c):
        return pl.BlockSpec((CHUNK, D), lambda i: (i, c))

    vec = pl.BlockSpec((1, D), lambda i: (0, 0))
    wsp = pl.BlockSpec((A_GROUPS, CHUNK, CHUNK), lambda i: (0, 0, 0))
    return pl.pallas_call(
        body, name="gmlp_bwd", grid=(S // CHUNK,),
        in_specs=[col(COL_U), col(COL_V), col(COL_ZA), col(0), vec, vec, wsp, pl.BlockSpec((CHUNK, A_GROUPS), lambda i: (0, 0))],
        out_specs=[col(0), col(0), col(0), wsp, pl.BlockSpec((CHUNK, D), lambda i: (0, 0)), vec, vec],
        out_shape=[jax.ShapeDtypeStruct((S, D), BF16)] * 3 + [
            jax.ShapeDtypeStruct((A_GROUPS, CHUNK, CHUNK), F32), jax.ShapeDtypeStruct((CHUNK, D), F32),
            jax.ShapeDtypeStruct((1, D), F32), jax.ShapeDtypeStruct((1, D), F32)],
        scratch_shapes=[pltpu.VMEM((CHUNK, D), F32)],
        compiler_params=_params(("arbitrary",)),
    )(proj, proj, proj, dy, ln_g, ln_b, wm, bs_t)


def _gate_merge(h16, ys, wg, bg, wbs, *, tm, tn):
    S = h16.shape[0]
    tm = min(tm, S)
    nj = D // tn

    def body(h_ref, ya_ref, yb_ref, ym_ref, wg0, wg1, wg2, bg0, bg1, bg2, wb0, wb1, wb2,
             mg_ref, g0_ref, g1_ref, g2_ref, p0_ref, p1_ref, p2_ref):
        hv = h_ref[...]
        acc = None
        for y_ref, wg_ref, bgr, wb_ref, g_ref, p_ref in ((ya_ref, wg0, bg0, wb0, g0_ref, p0_ref),
                                                         (yb_ref, wg1, bg1, wb1, g1_ref, p1_ref),
                                                         (ym_ref, wg2, bg2, wb2, g2_ref, p2_ref)):
            gate = _sigmoid(_dot(hv, wg_ref[...], NN) + bgr[...])
            p = _dot(y_ref[...], wb_ref[...], NN)
            g_ref[...] = gate.astype(BF16)
            p_ref[...] = p.astype(BF16)
            acc = gate * p if acc is None else acc + gate * p
        mg_ref[...] = acc.astype(BF16)

    a_spec = pl.BlockSpec((tm, D), lambda j, i: (i, 0))
    o_spec = pl.BlockSpec((tm, tn), lambda j, i: (i, j))

    def wgs(n):
        return pl.BlockSpec((D, tn), lambda j, i: (0, n * nj + j))

    def bgs(n):
        return pl.BlockSpec((1, tn), lambda j, i: (0, n * nj + j))

    wbsp = pl.BlockSpec((D, tn), lambda j, i: (0, j))
    return pl.pallas_call(
        body, name="gate_merge", grid=(nj, S // tm),
        in_specs=[a_spec] * 4 + [wgs(0), wgs(1), wgs(2), bgs(0), bgs(1), bgs(2), wbsp, wbsp, wbsp],
        out_specs=[o_spec] * 7, out_shape=[jax.ShapeDtypeStruct((S, D), BF16)] * 7,
        compiler_params=_params(("parallel", "parallel")),
    )(h16, *ys, wg, wg, wg, bg, bg, bg, *wbs)


def _gate_bwd(dmerged, gates, ps, *, tm):
    S = dmerged.shape[0]
    tm = min(tm, S)

    def body(dm_ref, g0, g1, g2, p0, p1, p2, dp0, dp1, dp2, dg_ref, gb_ref):
        @pl.when(pl.program_id(0) == 0)
        def _():
            gb_ref[...] = jnp.zeros_like(gb_ref)

        dm = dm_ref[...]
        for n, (g_ref, p_ref, dp_ref) in enumerate(((g0, p0, dp0), (g1, p1, dp1), (g2, p2, dp2))):
            gate = g_ref[...].astype(F32)
            dp_ref[...] = (dm * gate).astype(BF16)
            dg = dm * p_ref[...].astype(F32) * gate * (1.0 - gate)
            dg_ref[:, n * D:(n + 1) * D] = dg.astype(BF16)
            gb_ref[:, n * D:(n + 1) * D] += jnp.sum(dg, axis=0, keepdims=True)

    blk = pl.BlockSpec((tm, D), lambda i: (i, 0))
    return pl.pallas_call(
        body, name="gate_bwd", grid=(S // tm,),
        in_specs=[blk] * 7,
        out_specs=[blk, blk, blk, pl.BlockSpec((tm, 3 * D), lambda i: (i, 0)), pl.BlockSpec((1, 3 * D), lambda i: (0, 0))],
        out_shape=[jax.ShapeDtypeStruct((S, D), BF16)] * 3 + [jax.ShapeDtypeStruct((S, 3 * D), BF16),
                                                              jax.ShapeDtypeStruct((1, 3 * D), F32)],
        compiler_params=_params(("arbitrary",)),
    )(dmerged, *gates, *ps)


def _post_loss(x, out, target, g_post, *, tm):
    S = x.shape[0]
    tm = min(tm, S)

    def body(x_ref, o_ref, t_ref, g_ref, dy_ref, do_ref, gg_ref, ls_ref):
        @pl.when(pl.program_id(0) == 0)
        def _():
            gg_ref[...] = jnp.zeros_like(gg_ref)
            ls_ref[...] = jnp.zeros_like(ls_ref)

        ov = o_ref[...]
        r = lax.rsqrt(jnp.mean(ov * ov, axis=-1, keepdims=True) + EPS)
        n = ov * r
        err = (x_ref[...] + n * g_ref[...]) - t_ref[...]
        ls_ref[...] += 0.5 * jnp.sum(jnp.mean(err * err, axis=-1, keepdims=True))
        dy = err * (1.0 / D)
        dy_ref[...] = dy
        gg_ref[...] += jnp.sum(dy * n, axis=0, keepdims=True)
        gd = dy * g_ref[...]
        do_ref[...] = (r * (gd - n * jnp.mean(gd * n, axis=-1, keepdims=True))).astype(BF16)

    blk = pl.BlockSpec((tm, D), lambda i: (i, 0))
    vec = pl.BlockSpec((1, D), lambda i: (0, 0))
    return pl.pallas_call(
        body, name="post_loss", grid=(S // tm,),
        in_specs=[blk, blk, blk, vec],
        out_specs=[blk, blk, vec, pl.BlockSpec((1, LANES), lambda i: (0, 0))],
        out_shape=[jax.ShapeDtypeStruct((S, D), F32), jax.ShapeDtypeStruct((S, D), BF16),
                   jax.ShapeDtypeStruct((1, D), F32), jax.ShapeDtypeStruct((1, LANES), F32)],
        compiler_params=_params(("arbitrary",)),
    )(x, out, target, g_post)


def _adamw(w, g, m, v, *, tr, name):
    rows, width = w.shape
    tr = min(tr, rows)
    assert rows % tr == 0
    c1 = 1.0 - ADAM_B1 ** ADAM_STEP
    c2 = 1.0 - ADAM_B2 ** ADAM_STEP

    def body(w_ref, g_ref, m_ref, v_ref, d_ref, nm_ref, nv_ref):
        gv = g_ref[...]
        nm = ADAM_B1 * m_ref[...] + (1.0 - ADAM_B1) * gv
        nv = ADAM_B2 * v_ref[...] + (1.0 - ADAM_B2) * (gv * gv)
        d_ref[...] = -ADAM_LR * ((nm / c1) / (jnp.sqrt(nv / c2) + ADAM_EPS) + ADAM_WD * w_ref[...])
        nm_ref[...] = nm
        nv_ref[...] = nv

    blk = pl.BlockSpec((tr, width), lambda i: (i, 0))
    return pl.pallas_call(
        body, name=name, grid=(rows // tr,), in_specs=[blk] * 4, out_specs=[blk] * 3,
        out_shape=[jax.ShapeDtypeStruct((rows, width), F32)] * 3,
        compiler_params=_params(("parallel",)),
    )(w, g, m, v)


MESH = pl.DeviceIdType.MESH
ANY = pl.BlockSpec(memory_space=pl.ANY)


def _place():
    return lax.axis_index("x"), lax.axis_index("y"), lax.axis_index("c")


def _other_chips(x, y):
    return [(1 - x, y), (x, 1 - y), (1 - x, 1 - y)]


def _remote(src, dst, send_sem, recv_sem, dev):
    return pltpu.make_async_remote_copy(src_ref=src, dst_ref=dst, send_sem=send_sem, recv_sem=recv_sem,
                                        device_id=dev, device_id_type=MESH)


def _allgather_chips(flat):
    R, W = flat.shape
    hh = R // 2

    def body(x_ref, out_ref, send_sems, recv_sems):
        x, y, c = _place()
        sibling = (x, y, 1 - c)
        chips = _other_chips(x, y)

        def half(px, py, hc):
            return out_ref.at[2 * px + py, pl.ds(hc * hh, hh), :]

        first =[_remote(x_ref.at[pl.ds(c * hh, hh), :], half(x, y, c), send_sems.at[k], recv_sems.at[k], (px, py, c))
                 for k, (px, py) in enumerate(chips)]
        for cp in first:
            cp.start()
        passed = [_remote(half(px, py, c), half(px, py, c), send_sems.at[3 + k], recv_sems.at[3 + k], sibling)
                  for k, (px, py) in enumerate(chips)]
        for k, (px, py) in enumerate(chips):
            _remote(half(px, py, c), half(px, py, c), send_sems.at[k], recv_sems.at[k], (px, py, c)).wait_recv()
            passed[k].start()
        for k, (px, py) in enumerate(chips):
            _remote(half(px, py, 1 - c), half(px, py, 1 - c), send_sems.at[3 + k], recv_sems.at[3 + k], sibling).wait_recv()
        for cp in first + passed:
            cp.wait_send()

    out = pl.pallas_call(
        body, name="allgather_weights", in_specs=[ANY], out_specs=ANY,
        out_shape=jax.ShapeDtypeStruct((N_CHIPS, R, W), flat.dtype),
        scratch_shapes=[pltpu.SemaphoreType.DMA((6,)), pltpu.SemaphoreType.DMA((6,))],
    )(flat)
    own = 2 * lax.axis_index("x") + lax.axis_index("y")
    return lax.dynamic_update_slice(out, flat[None], (own, 0, 0))


def _pair_exchange(g):
    _, R, W = g.shape
    hh = R // 2

    def body(g_ref, out_ref, send_sem, recv_sem):
        x, y, c = _place()
        cp = _remote(g_ref.at[:, pl.ds((1 - c) * hh, hh), :], out_ref, send_sem, recv_sem, (x, y, 1 - c))
        cp.start()
        cp.wait()

    return pl.pallas_call(
        body, name="grad_pair_exchange", in_specs=[ANY], out_specs=ANY,
        out_shape=jax.ShapeDtypeStruct((N_CHIPS, hh, W), g.dtype),
        scratch_shapes=[pltpu.SemaphoreType.DMA, pltpu.SemaphoreType.DMA],
    )(g)


def _pair_add(g, recv, cidx, *, tr):
    _, R, W = g.shape
    hh = R // 2
    nb = hh // tr
    assert hh % tr == 0

    def body(c_ref, g_ref, r_ref, o32_ref, o16_ref):
        s = g_ref[...] + r_ref[...]
        o32_ref[...] = s
        o16_ref[...] = s.astype(BF16)

    blk = pl.BlockSpec((1, tr, W), lambda j, i, c_ref: (j, i, 0))
    return pl.pallas_call(
        body, name="grad_pair_add",
        grid_spec=pltpu.PrefetchScalarGridSpec(
            num_scalar_prefetch=1, grid=(N_CHIPS, nb),
            in_specs=[pl.BlockSpec((1, tr, W), lambda j, i, c_ref: (j, i + c_ref[0] * nb, 0)), blk],
            out_specs=[blk, blk]),
        out_shape=[jax.ShapeDtypeStruct((N_CHIPS, hh, W), F32), jax.ShapeDtypeStruct((N_CHIPS, hh, W), BF16)],
        compiler_params=_params(("parallel", "parallel")),
    )(cidx, g, recv)


def _chip_exchange(p16):
    _, hh, W = p16.shape

    def body(p_ref, out_ref, send_sems, recv_sems):
        x, y, c = _place()
        cps = [_remote(p_ref.at[2 * px + py], out_ref.at[k], send_sems.at[k], recv_sems.at[k], (px, py, c))
               for k, (px, py) in enumerate(_other_chips(x, y))]
        for cp in cps:
            cp.start()
        for cp in cps:
            cp.wait()

    return pl.pallas_call(
        body, name="grad_chip_exchange", in_specs=[ANY], out_specs=ANY,
        out_shape=jax.ShapeDtypeStruct((3, hh, W), p16.dtype),
        scratch_shapes=[pltpu.SemaphoreType.DMA((3,)), pltpu.SemaphoreType.DMA((3,))],
    )(p16)


def _chip_add(p32, recv, jidx, *, tr):
    _, hh, W = p32.shape
    assert hh % tr == 0

    def body(j_ref, p_ref, r_ref, o_ref):
        s = p_ref[0]
        for k in range(3):
            s = s + r_ref[k].astype(F32)
        o_ref[...] = s

    return pl.pallas_call(
        body, name="grad_chip_add",
        grid_spec=pltpu.PrefetchScalarGridSpec(
            num_scalar_prefetch=1, grid=(hh // tr,),
            in_specs=[pl.BlockSpec((1, tr, W), lambda i, j_ref: (j_ref[0], i, 0)),
                      pl.BlockSpec((3, tr, W), lambda i, j_ref: (0, i, 0))],
            out_specs=pl.BlockSpec((tr, W), lambda i, j_ref: (i, 0))),
        out_shape=jax.ShapeDtypeStruct((hh, W), F32),
        compiler_params=_params(("parallel",)),
    )(jidx, p32, recv)


def _halves_exchange(t):
    hh, W = t.shape

    def body(t_ref, out_ref, send_sem, recv_sem):
        x, y, c = _place()
        cp = _remote(t_ref, out_ref.at[pl.ds(c * hh, hh), :], send_sem, recv_sem, (x, y, 1 - c))
        cp.start()
        _remote(t_ref, out_ref.at[pl.ds((1 - c) * hh, hh), :], send_sem, recv_sem, (x, y, 1 - c)).wait_recv()
        cp.wait_send()

    out = pl.pallas_call(
        body, name="grad_halves_exchange", in_specs=[ANY], out_specs=ANY,
        out_shape=jax.ShapeDtypeStruct((2 * hh, W), t.dtype),
        scratch_shapes=[pltpu.SemaphoreType.DMA, pltpu.SemaphoreType.DMA],
    )(t)
    return lax.dynamic_update_slice(out, t, (lax.axis_index("c") * hh, 0))


def _small_allreduce(buf):
    rows, W = buf.shape

    def body(x_ref, sum_ref, gath_ref, send_sems, recv_sems):
        x, y, c = _place()
        me = 4 * x + 2 * y + c
        gath_ref[me] = x_ref[...]
        cps = []
        for k in range(1, N_DEV):
            fx, fy, fc = (k >> 2) & 1, (k >> 1) & 1, k & 1
            peer = (1 - x if fx else x, 1 - y if fy else y, 1 - c if fc else c)
            cp = _remote(x_ref, gath_ref.at[me], send_sems.at[k - 1], recv_sems.at[k - 1], peer)
            cp.start()
            cps.append((cp, 4 * peer[0] + 2 * peer[1] + peer[2]))
        for k, (cp, src) in enumerate(cps):
            _remote(x_ref, gath_ref.at[src], send_sems.at[k], recv_sems.at[k], (x, y, c)).wait_recv()
        for cp, _ in cps:
            cp.wait_send()
        acc = gath_ref[0]
        for d in range(1, N_DEV):
            acc = acc + gath_ref[d]
        sum_ref[...] = acc

    vm = pl.BlockSpec(memory_space=pltpu.VMEM)
    return pl.pallas_call(
        body, name="small_allreduce", in_specs=[vm], out_specs=vm,
        out_shape=jax.ShapeDtypeStruct((rows, W), F32),
        scratch_shapes=[pltpu.VMEM((N_DEV, rows, W), F32), pltpu.SemaphoreType.DMA((N_DEV - 1,)),
                        pltpu.SemaphoreType.DMA((N_DEV - 1,))],
        compiler_params=pltpu.CompilerParams(vmem_limit_bytes=VMEM_LIMIT),
    )(buf)


def _flat_rows(shape):
    n = math.prod(shape)
    assert n % FLAT_W == 0
    return n // FLAT_W


def _pack_flat(parts, dtype):
    return jnp.concatenate([p.astype(dtype).reshape(-1, FLAT_W) for p in parts], axis=0)


def _unpack_flat(flat, shapes):
    out, r = [], 0
    for s in shapes:
        n = _flat_rows(s)
        out.append(flat[r:r + n].reshape(s))
        r += n
    return out


SHARD_SHAPES = {"w_in": (D, IN_REF // N_CHIPS), "w_uq": (LORA, HEADS * QK_DIM // N_CHIPS),
                "w_ukv": (LORA, HEADS * (QK_NOPE + V_DIM) // N_CHIPS), "w_mem_kv": (D, 2 * D // N_CHIPS),
                "w_gate": (D, 3 * D // N_CHIPS), "w_branch": (3, D // N_CHIPS, D), "w_out": (D // N_CHIPS, D)}
SHARD_AXIS = {"w_in": 1, "w_uq": 1, "w_ukv": 1, "w_mem_kv": 1, "w_gate": 1, "w_branch": 1, "w_out": 0}


def _weights_from_gathered(gathered):
    per_chip = [_unpack_flat(gathered[j], [SHARD_SHAPES[n] for n in BIG]) for j in range(N_CHIPS)]
    full = {n: jnp.concatenate([per_chip[j][i] for j in range(N_CHIPS)], axis=SHARD_AXIS[n]) for i, n in enumerate(BIG)}
    w = full["w_in"]
    w_in = jnp.concatenate([w[:, :3 * D], w[:, 3 * D + 2 * LORA + QK_ROPE:], w[:, 3 * D:3 * D + 2 * LORA + QK_ROPE],
                            jnp.zeros((D, IN_PAD - IN_REF), w.dtype)], axis=1)
    wq = jnp.pad(full["w_uq"].reshape(LORA, HEADS, QK_DIM), ((0, 0), (0, 0), (0, QK_PAD - QK_DIM))).reshape(LORA, HEADS * QK_PAD)
    kv3 = full["w_ukv"].reshape(LORA, HEADS, QK_NOPE + V_DIM)
    wk = jnp.pad(kv3[:, :, :QK_NOPE], ((0, 0), (0, 0), (0, QK_PAD - QK_NOPE))).reshape(LORA, HEADS * QK_PAD)
    wv = kv3[:, :, QK_NOPE:].reshape(LORA, HEADS * V_DIM)
    return {"w_in": w_in, "wq": wq, "wk": wk, "wv": wv, "w_mem_kv": full["w_mem_kv"], "w_gate": full["w_gate"],
            "w_branch": full["w_branch"], "w_out": full["w_out"]}


def _grads_to_blocks(g):
    gi = g["w_in"]
    w_in = jnp.concatenate([gi[:, :3 * D], gi[:, 6 * D:6 * D + 2 * LORA + QK_ROPE], gi[:, 3 * D:6 * D]], axis=1)
    w_uq = g["wq"].reshape(LORA, HEADS, QK_PAD)[:, :, :QK_DIM].reshape(LORA, HEADS * QK_DIM)
    w_ukv = jnp.concatenate([g["wk"].reshape(LORA, HEADS, QK_PAD)[:, :, :QK_NOPE], g["wv"].reshape(LORA, HEADS, V_DIM)],
                            axis=2).reshape(LORA, HEADS * (QK_NOPE + V_DIM))
    full = {"w_in": w_in, "w_uq": w_uq, "w_ukv": w_ukv, "w_mem_kv": g["w_mem_kv"], "w_gate": g["w_gate"],
            "w_branch": g["w_branch"], "w_out": g["w_out"]}
    blocks = []
    for j in range(N_CHIPS):
        parts = []
        for n in BIG:
            ax, size = SHARD_AXIS[n], SHARD_SHAPES[n][SHARD_AXIS[n]]
            parts.append(lax.slice_in_dim(full[n], j * size, (j + 1) * size, axis=ax))
        blocks.append(_pack_flat(parts, F32))
    return jnp.stack(blocks, axis=0)


def _local_step(x, mem, pos_col, target, W, P):
    S = x.shape[0]
    h16, rstd_x = _rms_fwd(x, P["g_pre"], width=D, col=0, tm=256, name="pre_norm")
    memn16, rstd_m = _rms_fwd(mem, P["mem_norm_g"], width=D, col=0, tm=256, name="mem_norm")
    proj = _mm(h16, W["w_in"], "nn", tm=512, tn=1920, tk=D, out_dtype=F32, name="in_proj")

    causal = jnp.tril(jnp.ones((CHUNK, CHUNK), F32))
    wm = (P["a_w_s"] * causal[None]).astype(BF16)
    bs_t = P["a_b_s"].T
    ya = _gmlp_fwd(proj, P["a_ln_g"], P["a_ln_b"], wm, bs_t)

    inv = 1.0 / (ROPE_THETA ** (jnp.arange(0, QK_ROPE, 2, dtype=F32) / QK_ROPE))
    inv_lane = jnp.concatenate([inv, inv, jnp.zeros((LANES - QK_ROPE,), F32)])[None]
    tabs = _rope_tables(pos_col, inv_lane, tm=1024)
    cqn, rstd_q = _rms_fwd(proj, P["q_norm_g"], width=LORA, col=COL_CQ, tm=512, name="q_norm")
    ckvn, rstd_kv = _rms_fwd(proj, P["kv_norm_g"], width=LORA, col=COL_CKV, tm=512, name="kv_norm")
    q16, k16, v16 = _mla_proj(cqn, ckvn, proj, tabs, W["wq"], W["wk"], W["wv"], tm=256)
    o_b, yb, lse = _mla_fwd(q16, k16, v16, proj, t=512)

    kvm = _mm(memn16, W["w_mem_kv"], "nn", tm=256, tn=1024, tk=D, out_dtype=BF16, name="mem_kv")
    ym = _mem_fwd(proj, kvm, tm=512)

    wbs = [W["w_branch"][n] for n in range(3)]
    merged, g0, g1, g2, p0, p1, p2 = _gate_merge(h16, (ya, yb, ym), W["w_gate"], P["b_gate"], wbs, tm=512, tn=512)
    out = _mm(merged, W["w_out"], "nn", tm=512, tn=1024, tk=D, out_dtype=F32, name="out_proj")
    dy, dout, g_g_post, loss = _post_loss(x, out, target, P["g_post"], tm=256)

    g_w_out = _mm(merged, dout, "tn", tm=1024, tn=1024, tk=512, out_dtype=F32, name="gw_out")
    dmerged = _mm(dout, W["w_out"], "nt", tm=512, tn=1024, tk=D, out_dtype=F32, name="d_merged")
    dp0, dp1, dp2, dgpre, g_b_gate = _gate_bwd(dmerged, (g0, g1, g2), (p0, p1, p2), tm=256)
    g_w_gate = _mm(h16, dgpre, "tn", tm=1024, tn=1024, tk=512, out_dtype=F32, name="gw_gate")
    dh_gate = _mm(dgpre, W["w_gate"], "nt", tm=512, tn=D, tk=D, out_dtype=F32, name="dh_gate")
    g_w_branch = jnp.stack([_mm(y, dp, "tn", tm=1024, tn=1024, tk=512, out_dtype=F32, name=f"gw_branch{n}")
                            for n, (y, dp) in enumerate(((ya, dp0), (yb, dp1), (ym, dp2)))], axis=0)
    dya, dyb, dym = [_mm(dp, wbs[n], "nt", tm=512, tn=1024, tk=D, out_dtype=F32, name=f"dy_branch{n}")
                     for n, dp in enumerate((dp0, dp1, dp2))]

    dqm, dzm, dkvm = _mem_bwd(proj, kvm, dym, tm=512)
    dkvm16 = dkvm.astype(BF16)
    g_w_mem_kv = _mm(memn16, dkvm16, "tn", tm=1024, tn=1024, tk=256, out_dtype=F32, name="gw_mem_kv")
    dmemn = _mm(dkvm16, W["w_mem_kv"], "nt", tm=256, tn=1024, tk=2 * D, out_dtype=F32, name="d_memn")
    _, g_mem_norm = _rms_bwd(dmemn, mem, rstd_m, P["mem_norm_g"], width=D, col=0, tm=256, out_dtype=BF16, name="mem_norm_bwd")

    do16, dzb, stats = _mla_gate_bwd(dyb, o_b, proj, lse, tm=256)
    dq, dk, dv = _mla_bwd(q16, k16, v16, do16, stats, t=512)
    dq16, dk16, dkr = _mla_qk_post(dq, dk, tabs, tm=256)
    dv16 = dv.astype(BF16)
    g_wq = _mm(cqn, dq16, "tn", tm=512, tn=1024, tk=512, out_dtype=F32, name="gw_uq")
    g_wk = _mm(ckvn, dk16, "tn", tm=512, tn=1024, tk=512, out_dtype=F32, name="gw_uk")
    g_wv = _mm(ckvn, dv16, "tn", tm=512, tn=1024, tk=512, out_dtype=F32, name="gw_uv")
    dcqn = _mm(dq16, W["wq"], "nt", tm=512, tn=LORA, tk=HEADS * QK_PAD, out_dtype=F32, name="d_cqn")
    dckvn_k = _mm(dk16, W["wk"], "nt", tm=512, tn=LORA, tk=HEADS * QK_PAD, out_dtype=F32, name="d_ckvn_k")
    dckvn = _mm(dv16, W["wv"], "nt", tm=512, tn=LORA, tk=HEADS * V_DIM, out_dtype=F32, name="d_ckvn", add=dckvn_k)
    dcq, g_q_norm = _rms_bwd(dcqn, proj, rstd_q, P["q_norm_g"], width=LORA, col=COL_CQ, tm=512, out_dtype=BF16, name="q_norm_bwd")
    dckv, g_kv_norm = _rms_bwd(dckvn, proj, rstd_kv, P["kv_norm_g"], width=LORA, col=COL_CKV, tm=512, out_dtype=BF16, name="kv_norm_bwd")

    du, dvr, dza, gws, dsv_sum, g_ln_g, g_ln_b = _gmlp_bwd(proj, dya, P["a_ln_g"], P["a_ln_b"], wm, bs_t)
    g_a_w_s = gws * causal[None]
    g_a_b_s = dsv_sum.reshape(CHUNK, A_GROUPS, CHUNK).sum(axis=-1).T

    dproj = jnp.concatenate([du, dvr, dza, dzb, dqm, dzm, dcq, dckv, dkr], axis=1)
    g_w_in = _mm(h16, dproj, "tn", tm=1024, tn=1920, tk=512, out_dtype=F32, name="gw_in")
    dh = _mm(dproj, W["w_in"], "nt", tm=512, tn=D, tk=1920, out_dtype=F32, name="d_h", add=dh_gate)
    grad_x, g_g_pre = _rms_bwd(dh, x, rstd_x, P["g_pre"], width=D, col=0, tm=256, out_dtype=F32, name="pre_norm_bwd", residual=dy)

    big = {"w_in": g_w_in, "wq": g_wq, "wk": g_wk, "wv": g_wv, "w_mem_kv": g_w_mem_kv, "w_gate": g_w_gate,
           "w_branch": g_w_branch, "w_out": g_w_out}
    small = {"g_pre": g_g_pre, "a_ln_g": g_ln_g, "a_ln_b": g_ln_b, "a_w_s": g_a_w_s, "a_b_s": g_a_b_s,
             "q_norm_g": g_q_norm, "kv_norm_g": g_kv_norm, "mem_norm_g": g_mem_norm, "b_gate": g_b_gate, "g_post": g_g_post}
    return loss, grad_x, big, small


SMALL_ROWS = 2208


def _pack_small(parts, loss_row):
    flat = jnp.concatenate([p.reshape(-1) for p in parts] + ([loss_row.reshape(-1)] if loss_row is not None else []))
    return jnp.pad(flat, (0, SMALL_ROWS * LANES - flat.shape[0])).reshape(SMALL_ROWS, LANES)


def kernel(x, mem, positions, g_pre, w_in, a_ln_g, a_ln_b, a_w_s, a_b_s, q_norm_g, w_uq, kv_norm_g, w_ukv, mem_norm_g, w_mem_kv, w_gate, b_gate, w_branch, w_out, g_post, loss_target, m_g_pre, m_w_in, m_a_ln_g, m_a_ln_b, m_a_w_s, m_a_b_s, m_q_norm_g, m_w_uq, m_kv_norm_g, m_w_ukv, m_mem_norm_g, m_w_mem_kv, m_w_gate, m_b_gate, m_w_branch, m_w_out, m_g_post, v_g_pre, v_w_in, v_a_ln_g, v_a_ln_b, v_a_w_s, v_a_b_s, v_q_norm_g, v_w_uq, v_kv_norm_g, v_w_ukv, v_mem_norm_g, v_w_mem_kv, v_w_gate, v_b_gate, v_w_branch, v_w_out, v_g_post):
    w = dict(g_pre=g_pre, w_in=w_in, a_ln_g=a_ln_g, a_ln_b=a_ln_b, a_w_s=a_w_s, a_b_s=a_b_s, q_norm_g=q_norm_g, w_uq=w_uq,
             kv_norm_g=kv_norm_g, w_ukv=w_ukv, mem_norm_g=mem_norm_g, w_mem_kv=w_mem_kv, w_gate=w_gate, b_gate=b_gate,
             w_branch=w_branch, w_out=w_out, g_post=g_post)
    m = dict(g_pre=m_g_pre, w_in=m_w_in, a_ln_g=m_a_ln_g, a_ln_b=m_a_ln_b, a_w_s=m_a_w_s, a_b_s=m_a_b_s, q_norm_g=m_q_norm_g,
             w_uq=m_w_uq, kv_norm_g=m_kv_norm_g, w_ukv=m_w_ukv, mem_norm_g=m_mem_norm_g, w_mem_kv=m_w_mem_kv, w_gate=m_w_gate,
             b_gate=m_b_gate, w_branch=m_w_branch, w_out=m_w_out, g_post=m_g_post)
    v = dict(g_pre=v_g_pre, w_in=v_w_in, a_ln_g=v_a_ln_g, a_ln_b=v_a_ln_b, a_w_s=v_a_w_s, a_b_s=v_a_b_s, q_norm_g=v_q_norm_g,
             w_uq=v_w_uq, kv_norm_g=v_kv_norm_g, w_ukv=v_w_ukv, mem_norm_g=v_mem_norm_g, w_mem_kv=v_w_mem_kv, w_gate=v_w_gate,
             b_gate=v_b_gate, w_branch=v_w_branch, w_out=v_w_out, g_post=v_g_post)

    gathered = _allgather_chips(_pack_flat([w[n][0] for n in BIG], BF16))
    W = _weights_from_gathered(gathered)
    P = {n: w[n][0] for n in SMALL}
    for n in ("g_pre", "a_ln_g", "a_ln_b", "q_norm_g", "kv_norm_g", "mem_norm_g", "b_gate", "g_post"):
        P[n] = P[n].reshape(1, -1)

    S = x.shape[1]
    loss_row, grad_x, big, small = _local_step(x[0], mem[0], positions.reshape(S, 1), loss_target[0], W, P)

    cidx = lax.axis_index("c").astype(jnp.int32).reshape(1)
    jidx = (2 * lax.axis_index("x") + lax.axis_index("y")).astype(jnp.int32).reshape(1)
    blocks = _grads_to_blocks(big)
    from_sibling = _pair_exchange(blocks)
    p32, p16 = _pair_add(blocks, from_sibling, cidx, tr=1120)
    from_chips = _chip_exchange(p16)
    half_total = _chip_add(p32, from_chips, jidx, tr=1120)
    g_flat = _halves_exchange(half_total)
    w_flat, m_flat, v_flat = (_pack_flat([t[n][0] for n in BIG], F32) for t in (w, m, v))
    d_flat, nm_flat, nv_flat = _adamw(w_flat, g_flat, m_flat, v_flat, tr=1120, name="adamw_big")
    shard_shapes = [SHARD_SHAPES[n] for n in BIG]
    res = {}
    for key, flat in (("grad", g_flat), ("delta", d_flat), ("new_m", nm_flat), ("new_v", nv_flat)):
        for n, t in zip(BIG, _unpack_flat(flat, shard_shapes)):
            res[key, n] = t[None]

    g_small = _small_allreduce(_pack_small([small[n] for n in SMALL], loss_row))
    ws, ms, vs = (_pack_small([t[n] for n in SMALL], None) for t in (w, m, v))
    d_small, nm_small, nv_small = _adamw(ws, g_small, ms, vs, tr=SMALL_ROWS, name="adamw_small")
    for key, flat in (("grad", g_small), ("delta", d_small), ("new_m", nm_small), ("new_v", nv_small)):
        off = 0
        fl = flat.reshape(-1)
        for n in SMALL:
            size = math.prod(w[n].shape)
            res[key, n] = fl[off:off + size].reshape(w[n].shape)
            off += size
    n_small = sum(math.prod(w[n].shape) for n in SMALL)
    loss = g_small.reshape(-1)[n_small]

    outs = [loss, grad_x[None]]
    for key in ("grad", "delta", "new_m", "new_v"):
        outs += [res[key, n] for n in WEIGHTS]
    return tuple(outs)
```

```python
import functools
import math

import jax
import jax.numpy as jnp
from jax import lax
from jax.experimental import pallas as pl
from jax.experimental.pallas import tpu as pltpu

F32 = jnp.float32
BF16 = jnp.bfloat16

D = 2048
EPS = 1e-6
CHUNK = 128
A_GROUPS = 16
HEADS = 16
QK_NOPE = 128
QK_ROPE = 64
QK_DIM = QK_NOPE + QK_ROPE
V_DIM = 128
LORA = 512
MEM_HEADS = 4
MEM_HEAD_DIM = 512
ROPE_THETA = 10000.0
QK_PAD = 256
IN_REF = 13376
IN_PAD = 13440
COL_U, COL_V, COL_ZA, COL_ZB, COL_QM, COL_ZM = 0, 1, 2, 3, 4, 5
COL_CQ, COL_CKV = 24, 25
COL_KR = 104

ADAM_LR = 0.001
ADAM_B1 = 0.9
ADAM_B2 = 0.999
ADAM_EPS = 1e-08
ADAM_WD = 0.01
ADAM_STEP = 10

VMEM_LIMIT = 56 * 1024 * 1024
LANES = 128

BIG = ("w_in", "w_uq", "w_ukv", "w_mem_kv", "w_gate", "w_branch", "w_out")
SMALL = ("g_pre", "a_ln_g", "a_ln_b", "a_w_s", "a_b_s", "q_norm_g", "kv_norm_g", "mem_norm_g", "b_gate", "g_post")
WEIGHTS = ("g_pre", "w_in", "a_ln_g", "a_ln_b", "a_w_s", "a_b_s", "q_norm_g", "w_uq", "kv_norm_g", "w_ukv",
           "mem_norm_g", "w_mem_kv", "w_gate", "b_gate", "w_branch", "w_out", "g_post")
N_CHIPS = 4
N_DEV = 8


def _params(sem=None):
    return pltpu.CompilerParams(dimension_semantics=sem, vmem_limit_bytes=VMEM_LIMIT)


def _sigmoid(z):
    return 1.0 / (1.0 + jnp.exp(-z))


def _gelu_parts(x):
    c = math.sqrt(2.0 / math.pi)
    x2 = x * x
    t = jnp.tanh(c * (x + 0.044715 * x * x2))
    g = 0.5 * x * (1.0 + t)
    dg = 0.5 * (1.0 + t) + 0.5 * x * (1.0 - t * t) * (c * (1.0 + 3.0 * 0.044715 * x2))
    return g, dg


def _silu_parts(z):
    s = _sigmoid(z)
    return z * s, s * (1.0 + z * (1.0 - s))


def _dot(a, b, dims):
    return lax.dot_general(a, b, (dims, ((), ())), preferred_element_type=F32)


NN = ((1,), (0,))
NT = ((1,), (1,))
TN = ((0,), (0,))


def _mm(a, b, mode, *, tm, tn, tk, out_dtype, name, add=None):
    if mode == "nn":
        (M, K), (_, N) = a.shape, b.shape
    elif mode == "nt":
        (M, K), (N, _) = a.shape, b.shape
    else:
        (K, M), (_, N) = a.shape, b.shape
    tm, tn, tk = min(tm, M), min(tn, N), min(tk, K)
    assert M % tm == 0 and N % tn == 0 and K % tk == 0, (name, M, N, K, tm, tn, tk)
    ni, nj, nk = M // tm, N // tn, K // tk
    dims = {"nn": NN, "nt": NT, "tn": TN}[mode]
    has_add = add is not None

    def body(*refs):
        a_ref, b_ref = refs[0], refs[1]
        add_ref = refs[2] if has_add else None
        o_ref = refs[3] if has_add else refs[2]
        part = _dot(a_ref[...].astype(BF16), b_ref[...].astype(BF16), dims)

        def finish(r):
            if has_add:
                r = r + add_ref[...]
            o_ref[...] = r.astype(out_dtype)

        if nk == 1:
            finish(part)
        else:
            acc = refs[-1]
            k = pl.program_id(2)

            @pl.when(k == 0)
            def _():
                acc[...] = part

            @pl.when(k > 0)
            def _():
                acc[...] += part

            @pl.when(k == nk - 1)
            def _():
                finish(acc[...])

    if mode == "nn":
        a_spec = pl.BlockSpec((tm, tk), lambda j, i, k: (i, k))
        b_spec = pl.BlockSpec((tk, tn), lambda j, i, k: (k, j))
    elif mode == "nt":
        a_spec = pl.BlockSpec((tm, tk), lambda j, i, k: (i, k))
        b_spec = pl.BlockSpec((tn, tk), lambda j, i, k: (j, k))
    else:
        a_spec = pl.BlockSpec((tk, tm), lambda j, i, k: (k, i))
        b_spec = pl.BlockSpec((tk, tn), lambda j, i, k: (k, j))
    o_spec = pl.BlockSpec((tm, tn), lambda j, i, k: (i, j))
    in_specs = [a_spec, b_spec] + ([o_spec] if has_add else [])
    args = (a, b) + ((add,) if has_add else ())
    return pl.pallas_call(
        body, name=name, grid=(nj, ni, nk), in_specs=in_specs, out_specs=o_spec,
        out_shape=jax.ShapeDtypeStruct((M, N), out_dtype),
        scratch_shapes=[pltpu.VMEM((tm, tn), F32)] if nk > 1 else [],
        compiler_params=_params(("parallel", "parallel", "arbitrary")),
    )(*args)


def _rms_fwd(x, g, *, width, col, tm, name):
    rows = x.shape[0]
    tm = min(tm, rows)

    def body(x_ref, g_ref, y_ref, r_ref):
        xv = x_ref[...]
        r = lax.rsqrt(jnp.mean(xv * xv, axis=-1, keepdims=True) + EPS)
        y_ref[...] = ((xv * r) * g_ref[...]).astype(BF16)
        r_ref[...] = r

    return pl.pallas_call(
        body, name=name, grid=(rows // tm,),
        in_specs=[pl.BlockSpec((tm, width), lambda i: (i, col)), pl.BlockSpec((1, width), lambda i: (0, 0))],
        out_specs=[pl.BlockSpec((tm, width), lambda i: (i, 0)), pl.BlockSpec((tm, 1), lambda i: (i, 0))],
        out_shape=[jax.ShapeDtypeStruct((rows, width), BF16), jax.ShapeDtypeStruct((rows, 1), F32)],
        compiler_params=_params(("parallel",)),
    )(x, g)


def _rms_bwd(d, x, rstd, g, *, width, col, tm, out_dtype, name, residual=None):
    rows = d.shape[0]
    tm = min(tm, rows)
    has_res = residual is not None

    def body(*refs):
        d_ref, x_ref, r_ref, g_ref = refs[:4]
        res_ref = refs[4] if has_res else None
        dx_ref, gg_ref = refs[-2], refs[-1]
        dv = d_ref[...]
        n = x_ref[...] * r_ref[...]

        @pl.when(pl.program_id(0) == 0)
        def _():
            gg_ref[...] = jnp.zeros_like(gg_ref)

        gg_ref[...] += jnp.sum(dv * n, axis=0, keepdims=True)
        gd = dv * g_ref[...]
        dx = r_ref[...] * (gd - n * jnp.mean(gd * n, axis=-1, keepdims=True))
        if has_res:
            dx = dx + res_ref[...]
        dx_ref[...] = dx.astype(out_dtype)

    blk = pl.BlockSpec((tm, width), lambda i: (i, 0))
    in_specs = [blk, pl.BlockSpec((tm, width), lambda i: (i, col)), pl.BlockSpec((tm, 1), lambda i: (i, 0)),
                pl.BlockSpec((1, width), lambda i: (0, 0))] + ([blk] if has_res else [])
    args = (d, x, rstd, g) + ((residual,) if has_res else ())
    return pl.pallas_call(
        body, name=name, grid=(rows // tm,), in_specs=in_specs,
        out_specs=[blk, pl.BlockSpec((1, width), lambda i: (0, 0))],
        out_shape=[jax.ShapeDtypeStruct((rows, width), out_dtype), jax.ShapeDtypeStruct((1, width), F32)],
        compiler_params=_params(("arbitrary",)),
    )(*args)


def _rope_tables(pos_col, inv_lane, *, tm):
    rows = pos_col.shape[0]
    tm = min(tm, rows)

    def body(p_ref, f_ref, c_ref, s1_ref, s2_ref):
        ang = p_ref[...].astype(F32) * f_ref[...]
        lane = lax.broadcasted_iota(jnp.int32, ang.shape, 1)
        c, s = jnp.cos(ang), jnp.sin(ang)
        half = QK_ROPE // 2
        c_ref[...] = jnp.where(lane < QK_ROPE, c, 0.0)
        s1_ref[...] = jnp.where(lane < half, -s, 0.0)
        s2_ref[...] = jnp.where((lane >= half) & (lane < QK_ROPE), s, 0.0)

    blk = pl.BlockSpec((tm, LANES), lambda i: (i, 0))
    return pl.pallas_call(
        body, name="rope_tables", grid=(rows // tm,),
        in_specs=[pl.BlockSpec((tm, 1), lambda i: (i, 0)), pl.BlockSpec((1, LANES), lambda i: (0, 0))],
        out_specs=[blk, blk, blk], out_shape=[jax.ShapeDtypeStruct((rows, LANES), F32)] * 3,
        compiler_params=_params(("parallel",)),
    )(pos_col, inv_lane)


def _rot(t, c, s1, s2, sign):
    r1 = pltpu.roll(t, LANES - QK_ROPE // 2, 1) * s1
    r2 = pltpu.roll(t, QK_ROPE // 2, 1) * s2
    return t * c + (r1 + r2) if sign > 0 else t * c - (r1 + r2)


def _mla_proj(cqn, ckvn, proj, tabs, wq, wk, wv, *, tm):
    rows = cqn.shape[0]
    tm = min(tm, rows)

    def body(cq_ref, ckv_ref, kr_ref, c_ref, s1_ref, s2_ref, wq_ref, wk_ref, wv_ref, q_ref, k_ref, v_ref):
        c, s1, s2 = c_ref[...], s1_ref[...], s2_ref[...]
        q = _dot(cq_ref[...], wq_ref[...], NN)
        k = _dot(ckv_ref[...], wk_ref[...], NN)
        kpe = _rot(kr_ref[...], c, s1, s2, 1).astype(BF16)
        for h in range(HEADS):
            lo = h * QK_PAD
            q_ref[:, lo:lo + QK_NOPE] = q[:, lo:lo + QK_NOPE].astype(BF16)
            q_ref[:, lo + QK_NOPE:lo + QK_PAD] = _rot(q[:, lo + QK_NOPE:lo + QK_PAD], c, s1, s2, 1).astype(BF16)
            k_ref[:, lo:lo + QK_NOPE] = k[:, lo:lo + QK_NOPE].astype(BF16)
            k_ref[:, lo + QK_NOPE:lo + QK_PAD] = kpe
        v_ref[...] = _dot(ckv_ref[...], wv_ref[...], NN).astype(BF16)

    def row(w):
        return pl.BlockSpec((tm, w), lambda i: (i, 0))

    def whole(w):
        return pl.BlockSpec(w.shape, lambda i: (0, 0))

    return pl.pallas_call(
        body, name="mla_proj", grid=(rows // tm,),
        in_specs=[row(LORA), row(LORA), pl.BlockSpec((tm, LANES), lambda i: (i, COL_KR)), row(LANES), row(LANES), row(LANES),
                  whole(wq), whole(wk), whole(wv)],
        out_specs=[row(HEADS * QK_PAD), row(HEADS * QK_PAD), row(HEADS * V_DIM)],
        out_shape=[jax.ShapeDtypeStruct((rows, HEADS * QK_PAD), BF16), jax.ShapeDtypeStruct((rows, HEADS * QK_PAD), BF16),
                   jax.ShapeDtypeStruct((rows, HEADS * V_DIM), BF16)],
        compiler_params=_params(("parallel",)),
    )(cqn, ckvn, proj, *tabs, wq, wk, wv)


def _mla_fwd(q, k, v, proj, *, t):
    S = q.shape[0]
    t = min(t, S)
    n = S // t
    scale = QK_DIM ** -0.5

    def body(q_ref, k_ref, v_ref, z_ref, o_ref, y_ref, lse_ref):
        qi = pl.program_id(1)
        qv = q_ref[...]

        def block(j, carry, diag):
            m_old, l_old, acc = carry
            ks = pl.ds(pl.multiple_of(j * t, t), t)
            s = _dot(qv, k_ref[ks, :], NT) * scale
            if diag:
                r = lax.broadcasted_iota(jnp.int32, s.shape, 0)
                c = lax.broadcasted_iota(jnp.int32, s.shape, 1)
                s = jnp.where(c <= r, s, -1e30)
            m_new = jnp.maximum(m_old, jnp.max(s, axis=-1, keepdims=True))
            alpha = jnp.exp(m_old - m_new)
            p = jnp.exp(s - m_new)
            l_new = alpha * l_old + jnp.sum(p, axis=-1, keepdims=True)
            acc = alpha * acc + _dot(p.astype(BF16), v_ref[ks, :], NN)
            return m_new, l_new, acc

        init = (jnp.full((t, 1), -jnp.inf, F32), jnp.zeros((t, 1), F32), jnp.zeros((t, V_DIM), F32))
        carry = lax.fori_loop(0, qi, lambda j, cr: block(j, cr, False), init)
        m_f, l_f, acc = block(qi, carry, True)
        o = acc / l_f
        o_ref[...] = o
        sz, _ = _silu_parts(z_ref[...])
        y_ref[...] = (o * sz).astype(BF16)
        lse_ref[0] = m_f + jnp.log(l_f)

    zcol = COL_ZB * (D // V_DIM)
    return pl.pallas_call(
        body, name="mla_fwd", grid=(HEADS, n),
        in_specs=[pl.BlockSpec((t, QK_PAD), lambda h, i: (i, h)),
                  pl.BlockSpec((S, QK_PAD), lambda h, i: (0, h)),
                  pl.BlockSpec((S, V_DIM), lambda h, i: (0, h)),
                  pl.BlockSpec((t, V_DIM), lambda h, i: (i, zcol + h))],
        out_specs=[pl.BlockSpec((t, V_DIM), lambda h, i: (i, h)), pl.BlockSpec((t, V_DIM), lambda h, i: (i, h)),
                   pl.BlockSpec((1, t, 1), lambda h, i: (h, i, 0))],
        out_shape=[jax.ShapeDtypeStruct((S, HEADS * V_DIM), F32), jax.ShapeDtypeStruct((S, HEADS * V_DIM), BF16),
                   jax.ShapeDtypeStruct((HEADS, S, 1), F32)],
        compiler_params=_params(("parallel", "parallel")),
    )(q, k, v, proj)


def _mla_gate_bwd(dy, o, proj, lse, *, tm):
    S = dy.shape[0]
    tm = min(tm, S)

    def body(dy_ref, o_ref, z_ref, lse_ref, do_ref, dz_ref, st_ref):
        sz, dsz = _silu_parts(z_ref[...])
        dyv, ov = dy_ref[...], o_ref[...]
        do = dyv * sz
        do_ref[...] = do.astype(BF16)
        dz_ref[...] = (dyv * ov * dsz).astype(BF16)
        prod = do * ov
        lane = lax.broadcasted_iota(jnp.int32, (tm, LANES), 1)
        for h in range(HEADS):
            delta = jnp.sum(prod[:, h * V_DIM:(h + 1) * V_DIM], axis=-1, keepdims=True)
            st_ref[h] = jnp.where(lane == 0, lse_ref[h], jnp.where(lane == 1, delta, 0.0))

    blk = pl.BlockSpec((tm, D), lambda i: (i, 0))
    return pl.pallas_call(
        body, name="mla_gate_bwd", grid=(S // tm,),
        in_specs=[blk, blk, pl.BlockSpec((tm, D), lambda i: (i, COL_ZB)), pl.BlockSpec((HEADS, tm, 1), lambda i: (0, i, 0))],
        out_specs=[blk, blk, pl.BlockSpec((HEADS, tm, LANES), lambda i: (0, i, 0))],
        out_shape=[jax.ShapeDtypeStruct((S, D), BF16), jax.ShapeDtypeStruct((S, D), BF16),
                   jax.ShapeDtypeStruct((HEADS, S, LANES), F32)],
        compiler_params=_params(("parallel",)),
    )(dy, o, proj, lse)


def _mla_bwd(q, k, v, do, stats, *, t):
    S = q.shape[0]
    t = min(t, S)
    n = S // t
    scale = QK_DIM ** -0.5

    def body(q_ref, k_ref, v_ref, do_ref, st_ref, dq_ref, dk_ref, dv_ref):
        ki = pl.program_id(1)

        @pl.when(ki == 0)
        def _():
            dq_ref[...] = jnp.zeros_like(dq_ref)

        kv, vv = k_ref[...], v_ref[...]

        def block(i, carry, diag):
            dk, dv = carry
            rows = pl.ds(pl.multiple_of(i * t, t), t)
            qv, dov, st = q_ref[rows, :], do_ref[rows, :], st_ref[0, rows, :]
            s = _dot(qv, kv, NT) * scale
            if diag:
                r = lax.broadcasted_iota(jnp.int32, s.shape, 0)
                c = lax.broadcasted_iota(jnp.int32, s.shape, 1)
                s = jnp.where(c <= r, s, -1e30)
            p = jnp.exp(s - st[:, 0:1])
            p16 = p.astype(BF16)
            dv = dv + _dot(p16, dov, TN)
            dp = _dot(dov, vv, NT)
            ds = (p * (dp - st[:, 1:2]) * scale).astype(BF16)
            dk = dk + _dot(ds, qv, TN)
            dq_ref[rows, :] += _dot(ds, kv, NN)
            return dk, dv

        carry = block(ki, (jnp.zeros((t, QK_PAD), F32), jnp.zeros((t, V_DIM), F32)), True)
        dk, dv = lax.fori_loop(ki + 1, n, lambda i, cr: block(i, cr, False), carry)
        dk_ref[...] = dk
        dv_ref[...] = dv

    return pl.pallas_call(
        body, name="mla_bwd", grid=(HEADS, n),
        in_specs=[pl.BlockSpec((S, QK_PAD), lambda h, j: (0, h)),
                  pl.BlockSpec((t, QK_PAD), lambda h, j: (j, h)),
                  pl.BlockSpec((t, V_DIM), lambda h, j: (j, h)),
                  pl.BlockSpec((S, V_DIM), lambda h, j: (0, h)),
                  pl.BlockSpec((1, S, LANES), lambda h, j: (h, 0, 0))],
        out_specs=[pl.BlockSpec((S, QK_PAD), lambda h, j: (0, h)),
                   pl.BlockSpec((t, QK_PAD), lambda h, j: (j, h)),
                   pl.BlockSpec((t, V_DIM), lambda h, j: (j, h))],
        out_shape=[jax.ShapeDtypeStruct((S, HEADS * QK_PAD), F32), jax.ShapeDtypeStruct((S, HEADS * QK_PAD), F32),
                   jax.ShapeDtypeStruct((S, HEADS * V_DIM), F32)],
        compiler_params=_params(("arbitrary", "arbitrary")),
    )(q, k, v, do, stats)


def _mla_qk_post(dq, dk, tabs, *, tm):
    S = dq.shape[0]
    tm = min(tm, S)

    def body(dq_ref, dk_ref, c_ref, s1_ref, s2_ref, q16_ref, k16_ref, kr_ref):
        c, s1, s2 = c_ref[...], s1_ref[...], s2_ref[...]
        kpe = jnp.zeros((tm, LANES), F32)
        for h in range(HEADS):
            lo = h * QK_PAD
            q16_ref[:, lo:lo + QK_NOPE] = dq_ref[:, lo:lo + QK_NOPE].astype(BF16)
            q16_ref[:, lo + QK_NOPE:lo + QK_PAD] = _rot(dq_ref[:, lo + QK_NOPE:lo + QK_PAD], c, s1, s2, -1).astype(BF16)
            kpe = kpe + dk_ref[:, lo + QK_NOPE:lo + QK_PAD]
        k16_ref[...] = dk_ref[...].astype(BF16)
        kr_ref[...] = _rot(kpe, c, s1, s2, -1).astype(BF16)

    wide = pl.BlockSpec((tm, HEADS * QK_PAD), lambda i: (i, 0))
    lane = pl.BlockSpec((tm, LANES), lambda i: (i, 0))
    return pl.pallas_call(
        body, name="mla_qk_post", grid=(S // tm,),
        in_specs=[wide, wide, lane, lane, lane], out_specs=[wide, wide, lane],
        out_shape=[jax.ShapeDtypeStruct((S, HEADS * QK_PAD), BF16), jax.ShapeDtypeStruct((S, HEADS * QK_PAD), BF16),
                   jax.ShapeDtypeStruct((S, LANES), BF16)],
        compiler_params=_params(("parallel",)),
    )(dq, dk, *tabs)


def _mem_scores(q16, km_ref, h):
    lo = h * MEM_HEAD_DIM
    s = _dot(q16, km_ref[:, lo:lo + MEM_HEAD_DIM], NT) * (MEM_HEAD_DIM ** -0.5)
    e = jnp.exp(s - jnp.max(s, axis=-1, keepdims=True))
    return e / jnp.sum(e, axis=-1, keepdims=True)


def _mem_fwd(proj, kvm, *, tm):
    S = proj.shape[0]
    tm = min(tm, S)
    M = kvm.shape[0]

    def body(q_ref, z_ref, km_ref, vm_ref, y_ref):
        sz, _ = _silu_parts(z_ref[...])
        for h in range(MEM_HEADS):
            lo = h * MEM_HEAD_DIM
            p = _mem_scores(q_ref[:, lo:lo + MEM_HEAD_DIM].astype(BF16), km_ref, h)
            o = _dot(p.astype(BF16), vm_ref[:, lo:lo + MEM_HEAD_DIM], NN)
            y_ref[:, lo:lo + MEM_HEAD_DIM] = (o * sz[:, lo:lo + MEM_HEAD_DIM]).astype(BF16)

    return pl.pallas_call(
        body, name="mem_fwd", grid=(S // tm,),
        in_specs=[pl.BlockSpec((tm, D), lambda i: (i, COL_QM)), pl.BlockSpec((tm, D), lambda i: (i, COL_ZM)),
                  pl.BlockSpec((M, D), lambda i: (0, 0)), pl.BlockSpec((M, D), lambda i: (0, 1))],
        out_specs=pl.BlockSpec((tm, D), lambda i: (i, 0)),
        out_shape=jax.ShapeDtypeStruct((S, D), BF16),
        compiler_params=_params(("parallel",)),
    )(proj, proj, kvm, kvm)


def _mem_bwd(proj, kvm, dy, *, tm):
    S = proj.shape[0]
    tm = min(tm, S)
    M = kvm.shape[0]
    scale = MEM_HEAD_DIM ** -0.5

    def body(q_ref, z_ref, km_ref, vm_ref, dy_ref, dq_ref, dz_ref, dkv_ref):
        @pl.when(pl.program_id(0) == 0)
        def _():
            dkv_ref[...] = jnp.zeros_like(dkv_ref)

        sz, dsz = _silu_parts(z_ref[...])
        dyv = dy_ref[...]
        for h in range(MEM_HEADS):
            lo = h * MEM_HEAD_DIM
            sl = slice(lo, lo + MEM_HEAD_DIM)
            q16 = q_ref[:, sl].astype(BF16)
            p = _mem_scores(q16, km_ref, h)
            p16 = p.astype(BF16)
            o = _dot(p16, vm_ref[:, sl], NN)
            dy_h = dyv[:, sl]
            dz_ref[:, sl] = (dy_h * o * dsz[:, sl]).astype(BF16)
            do16 = (dy_h * sz[:, sl]).astype(BF16)
            dp = _dot(do16, vm_ref[:, sl], NT)
            ds = (p * (dp - jnp.sum(dp * p, axis=-1, keepdims=True)) * scale).astype(BF16)
            dq_ref[:, sl] = _dot(ds, km_ref[:, sl], NN).astype(BF16)
            dkv_ref[:, sl] += _dot(ds, q16, TN)
            dkv_ref[:, D + lo:D + lo + MEM_HEAD_DIM] += _dot(p16, do16, TN)

    blk = pl.BlockSpec((tm, D), lambda i: (i, 0))
    return pl.pallas_call(
        body, name="mem_bwd", grid=(S // tm,),
        in_specs=[pl.BlockSpec((tm, D), lambda i: (i, COL_QM)), pl.BlockSpec((tm, D), lambda i: (i, COL_ZM)),
                  pl.BlockSpec((M, D), lambda i: (0, 0)), pl.BlockSpec((M, D), lambda i: (0, 1)), blk],
        out_specs=[blk, blk, pl.BlockSpec((M, 2 * D), lambda i: (0, 0))],
        out_shape=[jax.ShapeDtypeStruct((S, D), BF16), jax.ShapeDtypeStruct((S, D), BF16),
                   jax.ShapeDtypeStruct((M, 2 * D), F32)],
        compiler_params=_params(("arbitrary",)),
    )(proj, proj, kvm, kvm, dy)


def _gmlp_common(u_ref, v_ref, lng_ref, lnb_ref):
    u, du = _gelu_parts(u_ref[...])
    vg, dvg = _gelu_parts(v_ref[...])
    mu = jnp.mean(vg, axis=-1, keepdims=True)
    vc = vg - mu
    r = lax.rsqrt(jnp.mean(vc * vc, axis=-1, keepdims=True) + EPS)
    vhat = vc * r
    vn = vhat * lng_ref[...] + lnb_ref[...]
    return u, du, dvg, r, vhat, vn.astype(BF16)


def _gmlp_fwd(proj, ln_g, ln_b, wm, bs_t):
    S = proj.shape[0]

    def body(u_ref, v_ref, z_ref, lng_ref, lnb_ref, wm_ref, bs_ref, y_ref):
        u, _, _, _, _, v16 = _gmlp_common(u_ref, v_ref, lng_ref, lnb_ref)
        sz, _ = _silu_parts(z_ref[...])
        for g in range(A_GROUPS):
            sl = slice(g * CHUNK, (g + 1) * CHUNK)
            sv = _dot(wm_ref[g], v16[:, sl], NN) + bs_ref[:, g:g + 1]
            y_ref[:, sl] = (u[:, sl] * sv * sz[:, sl]).astype(BF16)

    def col(c):
        return pl.BlockSpec((CHUNK, D), lambda i: (i, c))

    vec = pl.BlockSpec((1, D), lambda i: (0, 0))
    return pl.pallas_call(
        body, name="gmlp_fwd", grid=(S // CHUNK,),
        in_specs=[col(COL_U), col(COL_V), col(COL_ZA), vec, vec,
                  pl.BlockSpec((A_GROUPS, CHUNK, CHUNK), lambda i: (0, 0, 0)), pl.BlockSpec((CHUNK, A_GROUPS), lambda i: (0, 0))],
        out_specs=col(0), out_shape=jax.ShapeDtypeStruct((S, D), BF16),
        compiler_params=_params(("parallel",)),
    )(proj, proj, proj, ln_g, ln_b, wm, bs_t)


def _gmlp_bwd(proj, dy, ln_g, ln_b, wm, bs_t):
    S = proj.shape[0]

    def body(u_ref, v_ref, z_ref, dy_ref, lng_ref, lnb_ref, wm_ref, bs_ref,
             du_ref, dv_ref, dz_ref, gws_ref, dsv_ref, glg_ref, glb_ref, dvn_s):
        @pl.when(pl.program_id(0) == 0)
        def _():
            gws_ref[...] = jnp.zeros_like(gws_ref)
            dsv_ref[...] = jnp.zeros_like(dsv_ref)
            glg_ref[...] = jnp.zeros_like(glg_ref)
            glb_ref[...] = jnp.zeros_like(glb_ref)

        u, du, dvg, r, vhat, v16 = _gmlp_common(u_ref, v_ref, lng_ref, lnb_ref)
        sz, dsz = _silu_parts(z_ref[...])
        dyv = dy_ref[...]
        for g in range(A_GROUPS):
            sl = slice(g * CHUNK, (g + 1) * CHUNK)
            sv = _dot(wm_ref[g], v16[:, sl], NN) + bs_ref[:, g:g + 1]
            dy_g, u_g, sz_g = dyv[:, sl], u[:, sl], sz[:, sl]
            dsv = dy_g * u_g * sz_g
            du_ref[:, sl] = (dy_g * sv * sz_g * du[:, sl]).astype(BF16)
            dz_ref[:, sl] = (dy_g * u_g * sv * dsz[:, sl]).astype(BF16)
            dsv16 = dsv.astype(BF16)
            dvn_s[:, sl] = _dot(wm_ref[g], dsv16, TN)
            gws_ref[g] += _dot(dsv16, v16[:, sl], NT)
            dsv_ref[:, sl] += dsv
        dvn = dvn_s[...]
        glb_ref[...] += jnp.sum(dvn, axis=0, keepdims=True)
        glg_ref[...] += jnp.sum(dvn * vhat, axis=0, keepdims=True)
        dvh = dvn * lng_ref[...]
        dvc = r * (dvh - jnp.mean(dvh, axis=-1, keepdims=True) - vhat * jnp.mean(dvh * vhat, axis=-1, keepdims=True))
        dv_ref[...] = (dvc * dvg).astype(BF16)

    def col(c):
        return pl.BlockSpec((CHUNK, D), lambda i: (i, c))

    vec = pl.BlockSpec((1, D), lambda i: (0, 0))
    wsp = pl.BlockSpec((A_GROUPS, CHUNK, CHUNK), lambda i: (0, 0, 0))
    return pl.pallas_call(
        body, name="gmlp_bwd", grid=(S // CHUNK,),
        in_specs=[col(COL_U), col(COL_V), col(COL_ZA), col(0), vec, vec, wsp, pl.BlockSpec((CHUNK, A_GROUPS), lambda i: (0, 0))],
        out_specs=[col(0), col(0), col(0), wsp, pl.BlockSpec((CHUNK, D), lambda i: (0, 0)), vec, vec],
        out_shape=[jax.ShapeDtypeStruct((S, D), BF16)] * 3 + [
            jax.ShapeDtypeStruct((A_GROUPS, CHUNK, CHUNK), F32), jax.ShapeDtypeStruct((CHUNK, D), F32),
            jax.ShapeDtypeStruct((1, D), F32), jax.ShapeDtypeStruct((1, D), F32)],
        scratch_shapes=[pltpu.VMEM((CHUNK, D), F32)],
        compiler_params=_params(("arbitrary",)),
    )(proj, proj, proj, dy, ln_g, ln_b, wm, bs_t)


def _gate_merge(h16, ys, wg, bg, wbs, *, tm, tn):
    S = h16.shape[0]
    tm = min(tm, S)
    nj = D // tn

    def body(h_ref, ya_ref, yb_ref, ym_ref, wg0, wg1, wg2, bg0, bg1, bg2, wb0, wb1, wb2,
             mg_ref, g0_ref, g1_ref, g2_ref, p0_ref, p1_ref, p2_ref):
        hv = h_ref[...]
        acc = None
        for y_ref, wg_ref, bgr, wb_ref, g_ref, p_ref in ((ya_ref, wg0, bg0, wb0, g0_ref, p0_ref),
                                                         (yb_ref, wg1, bg1, wb1, g1_ref, p1_ref),
                                                         (ym_ref, wg2, bg2, wb2, g2_ref, p2_ref)):
            gate = _sigmoid(_dot(hv, wg_ref[...], NN) + bgr[...])
            p = _dot(y_ref[...], wb_ref[...], NN)
            g_ref[...] = gate.astype(BF16)
            p_ref[...] = p.astype(BF16)
            acc = gate * p if acc is None else acc + gate * p
        mg_ref[...] = acc.astype(BF16)

    a_spec = pl.BlockSpec((tm, D), lambda j, i: (i, 0))
    o_spec = pl.BlockSpec((tm, tn), lambda j, i: (i, j))

    def wgs(n):
        return pl.BlockSpec((D, tn), lambda j, i: (0, n * nj + j))

    def bgs(n):
        return pl.BlockSpec((1, tn), lambda j, i: (0, n * nj + j))

    wbsp = pl.BlockSpec((D, tn), lambda j, i: (0, j))
    return pl.pallas_call(
        body, name="gate_merge", grid=(nj, S // tm),
        in_specs=[a_spec] * 4 + [wgs(0), wgs(1), wgs(2), bgs(0), bgs(1), bgs(2), wbsp, wbsp, wbsp],
        out_specs=[o_spec] * 7, out_shape=[jax.ShapeDtypeStruct((S, D), BF16)] * 7,
        compiler_params=_params(("parallel", "parallel")),
    )(h16, *ys, wg, wg, wg, bg, bg, bg, *wbs)


def _gate_bwd(dmerged, gates, ps, *, tm):
    S = dmerged.shape[0]
    tm = min(tm, S)

    def body(dm_ref, g0, g1, g2, p0, p1, p2, dp0, dp1, dp2, dg_ref, gb_ref):
        @pl.when(pl.program_id(0) == 0)
        def _():
            gb_ref[...] = jnp.zeros_like(gb_ref)

        dm = dm_ref[...]
        for n, (g_ref, p_ref, dp_ref) in enumerate(((g0, p0, dp0), (g1, p1, dp1), (g2, p2, dp2))):
            gate = g_ref[...].astype(F32)
            dp_ref[...] = (dm * gate).astype(BF16)
            dg = dm * p_ref[...].astype(F32) * gate * (1.0 - gate)
            dg_ref[:, n * D:(n + 1) * D] = dg.astype(BF16)
            gb_ref[:, n * D:(n + 1) * D] += jnp.sum(dg, axis=0, keepdims=True)

    blk = pl.BlockSpec((tm, D), lambda i: (i, 0))
    return pl.pallas_call(
        body, name="gate_bwd", grid=(S // tm,),
        in_specs=[blk] * 7,
        out_specs=[blk, blk, blk, pl.BlockSpec((tm, 3 * D), lambda i: (i, 0)), pl.BlockSpec((1, 3 * D), lambda i: (0, 0))],
        out_shape=[jax.ShapeDtypeStruct((S, D), BF16)] * 3 + [jax.ShapeDtypeStruct((S, 3 * D), BF16),
                                                              jax.ShapeDtypeStruct((1, 3 * D), F32)],
        compiler_params=_params(("arbitrary",)),
    )(dmerged, *gates, *ps)


def _post_loss(x, out, target, g_post, *, tm):
    S = x.shape[0]
    tm = min(tm, S)

    def body(x_ref, o_ref, t_ref, g_ref, dy_ref, do_ref, gg_ref, ls_ref):
        @pl.when(pl.program_id(0) == 0)
        def _():
            gg_ref[...] = jnp.zeros_like(gg_ref)
            ls_ref[...] = jnp.zeros_like(ls_ref)

        ov = o_ref[...]
        r = lax.rsqrt(jnp.mean(ov * ov, axis=-1, keepdims=True) + EPS)
        n = ov * r
        err = (x_ref[...] + n * g_ref[...]) - t_ref[...]
        ls_ref[...] += 0.5 * jnp.sum(jnp.mean(err * err, axis=-1, keepdims=True))
        dy = err * (1.0 / D)
        dy_ref[...] = dy
        gg_ref[...] += jnp.sum(dy * n, axis=0, keepdims=True)
        gd = dy * g_ref[...]
        do_ref[...] = (r * (gd - n * jnp.mean(gd * n, axis=-1, keepdims=True))).astype(BF16)

    blk = pl.BlockSpec((tm, D), lambda i: (i, 0))
    vec = pl.BlockSpec((1, D), lambda i: (0, 0))
    return pl.pallas_call(
        body, name="post_loss", grid=(S // tm,),
        in_specs=[blk, blk, blk, vec],
        out_specs=[blk, blk, vec, pl.BlockSpec((1, LANES), lambda i: (0, 0))],
        out_shape=[jax.ShapeDtypeStruct((S, D), F32), jax.ShapeDtypeStruct((S, D), BF16),
                   jax.ShapeDtypeStruct((1, D), F32), jax.ShapeDtypeStruct((1, LANES), F32)],
        compiler_params=_params(("arbitrary",)),
    )(x, out, target, g_post)


def _adamw(w, g, m, v, *, tr, name):
    rows, width = w.shape
    tr = min(tr, rows)
    assert rows % tr == 0
    c1 = 1.0 - ADAM_B1 ** ADAM_STEP
    c2 = 1.0 - ADAM_B2 ** ADAM_STEP

    def body(w_ref, g_ref, m_ref, v_ref, d_ref, nm_ref, nv_ref):
        gv = g_ref[...]
        nm = ADAM_B1 * m_ref[...] + (1.0 - ADAM_B1) * gv
        nv = ADAM_B2 * v_ref[...] + (1.0 - ADAM_B2) * (gv * gv)
        d_ref[...] = -ADAM_LR * ((nm / c1) / (jnp.sqrt(nv / c2) + ADAM_EPS) + ADAM_WD * w_ref[...])
        nm_ref[...] = nm
        nv_ref[...] = nv

    blk = pl.BlockSpec((tr, width), lambda i: (i, 0))
    return pl.pallas_call(
        body, name=name, grid=(rows // tr,), in_specs=[blk] * 4, out_specs=[blk] * 3,
        out_shape=[jax.ShapeDtypeStruct((rows, width), F32)] * 3,
        compiler_params=_params(("parallel",)),
    )(w, g, m, v)


MESH = pl.DeviceIdType.MESH
ANY = pl.BlockSpec(memory_space=pl.ANY)


def _place():
    return lax.axis_index("x"), lax.axis_index("y"), lax.axis_index("c")


def _other_chips(x, y):
    return [(1 - x, y), (x, 1 - y), (1 - x, 1 - y)]


def _remote(src, dst, send_sem, recv_sem, dev):
    return pltpu.make_async_remote_copy(src_ref=src, dst_ref=dst, send_sem=send_sem, recv_sem=recv_sem,
                                        device_id=dev, device_id_type=MESH)


def _allgather_chips(shards):
    nw = len(shards)

    def body(*refs):
        x_refs, out_refs = refs[:nw], refs[nw:2 * nw]
        send_sems, recv_sems = refs[2 * nw:]
        x, y, c = _place()
        sibling = (x, y, 1 - c)
        chips = _other_chips(x, y)

        def half(w, px, py, hc):
            hh = shards[w].shape[0] // 2
            return out_refs[w].at[2 * px + py, pl.ds(hc * hh, hh), :]

        sent = []
        for w in range(nw):
            hh = shards[w].shape[0] // 2
            for k, (px, py) in enumerate(chips):
                cp = _remote(x_refs[w].at[pl.ds(c * hh, hh), :], half(w, x, y, c), send_sems.at[6 * w + k],
                             recv_sems.at[6 * w + k], (px, py, c))
                cp.start()
                sent.append(cp)
        for w in range(nw):
            for k, (px, py) in enumerate(chips):
                landed = half(w, px, py, c)
                _remote(landed, landed, send_sems.at[6 * w + k], recv_sems.at[6 * w + k], (px, py, c)).wait_recv()
                cp = _remote(landed, landed, send_sems.at[6 * w + 3 + k], recv_sems.at[6 * w + 3 + k], sibling)
                cp.start()
                sent.append(cp)
        for w in range(nw):
            for k, (px, py) in enumerate(chips):
                other = half(w, px, py, 1 - c)
                _remote(other, other, send_sems.at[6 * w + 3 + k], recv_sems.at[6 * w + 3 + k], sibling).wait_recv()
        for cp in sent:
            cp.wait_send()

    outs = pl.pallas_call(
        body, name="allgather_weights", in_specs=[ANY] * nw, out_specs=[ANY] * nw,
        out_shape=[jax.ShapeDtypeStruct((N_CHIPS,) + s.shape, s.dtype) for s in shards],
        scratch_shapes=[pltpu.SemaphoreType.DMA((6 * nw,)), pltpu.SemaphoreType.DMA((6 * nw,))],
    )(*shards)
    own = 2 * lax.axis_index("x") + lax.axis_index("y")
    return [lax.dynamic_update_slice(o, s[None], (own, 0, 0)) for o, s in zip(outs, shards)]


def _pair_exchange(gs):
    nw = len(gs)

    def body(*refs):
        g_refs, out_refs = refs[:nw], refs[nw:2 * nw]
        send_sems, recv_sems = refs[2 * nw:]
        x, y, c = _place()
        cps = []
        for w in range(nw):
            hh = gs[w].shape[1] // 2
            cp = _remote(g_refs[w].at[:, pl.ds((1 - c) * hh, hh), :], out_refs[w], send_sems.at[w], recv_sems.at[w], (x, y, 1 - c))
            cp.start()
            cps.append(cp)
        for cp in cps:
            cp.wait()

    return pl.pallas_call(
        body, name="grad_pair_exchange", in_specs=[ANY] * nw, out_specs=[ANY] * nw,
        out_shape=[jax.ShapeDtypeStruct((N_CHIPS, g.shape[1] // 2, g.shape[2]), g.dtype) for g in gs],
        scratch_shapes=[pltpu.SemaphoreType.DMA((nw,)), pltpu.SemaphoreType.DMA((nw,))],
    )(*gs)


def _row_tile(rows, cols, unit=16, budget=2 * 1024 * 1024):
    best = unit
    for t in range(unit, rows + 1, unit):
        if rows % t == 0 and t * cols * 4 <= budget:
            best = t
    assert rows % best == 0, (rows, cols)
    return best


def _pair_add(g, recv, cidx, *, name):
    _, R, W = g.shape
    hh = R // 2
    tr = _row_tile(hh, W)
    nb = hh // tr

    def body(c_ref, g_ref, r_ref, o32_ref, o16_ref):
        s = g_ref[...] + r_ref[...]
        o32_ref[...] = s
        o16_ref[...] = s.astype(BF16)

    blk = pl.BlockSpec((1, tr, W), lambda j, i, c_ref: (j, i, 0))
    return pl.pallas_call(
        body, name=name,
        grid_spec=pltpu.PrefetchScalarGridSpec(
            num_scalar_prefetch=1, grid=(N_CHIPS, nb),
            in_specs=[pl.BlockSpec((1, tr, W), lambda j, i, c_ref: (j, i + c_ref[0] * nb, 0)), blk],
            out_specs=[blk, blk]),
        out_shape=[jax.ShapeDtypeStruct((N_CHIPS, hh, W), F32), jax.ShapeDtypeStruct((N_CHIPS, hh, W), BF16)],
        compiler_params=_params(("parallel", "parallel")),
    )(cidx, g, recv)


def _chip_exchange(ps):
    nw = len(ps)

    def body(*refs):
        p_refs, out_refs = refs[:nw], refs[nw:2 * nw]
        send_sems, recv_sems = refs[2 * nw:]
        x, y, c = _place()
        cps = []
        for w in range(nw):
            for k, (px, py) in enumerate(_other_chips(x, y)):
                cp = _remote(p_refs[w].at[2 * px + py], out_refs[w].at[k], send_sems.at[3 * w + k], recv_sems.at[3 * w + k],
                             (px, py, c))
                cp.start()
                cps.append(cp)
        for cp in cps:
            cp.wait()

    return pl.pallas_call(
        body, name="grad_chip_exchange", in_specs=[ANY] * nw, out_specs=[ANY] * nw,
        out_shape=[jax.ShapeDtypeStruct((3,) + p.shape[1:], p.dtype) for p in ps],
        scratch_shapes=[pltpu.SemaphoreType.DMA((3 * nw,)), pltpu.SemaphoreType.DMA((3 * nw,))],
    )(*ps)


def _chip_add(p32, recv, jidx, *, name):
    _, hh, W = p32.shape
    tr = _row_tile(hh, W, budget=1024 * 1024)

    def body(j_ref, p_ref, r_ref, o_ref):
        s = p_ref[0]
        for k in range(3):
            s = s + r_ref[k].astype(F32)
        o_ref[...] = s

    return pl.pallas_call(
        body, name=name,
        grid_spec=pltpu.PrefetchScalarGridSpec(
            num_scalar_prefetch=1, grid=(hh // tr,),
            in_specs=[pl.BlockSpec((1, tr, W), lambda i, j_ref: (j_ref[0], i, 0)),
                      pl.BlockSpec((3, tr, W), lambda i, j_ref: (0, i, 0))],
            out_specs=pl.BlockSpec((tr, W), lambda i, j_ref: (i, 0))),
        out_shape=jax.ShapeDtypeStruct((hh, W), F32),
        compiler_params=_params(("parallel",)),
    )(jidx, p32, recv)


def _halves_exchange(ts):
    nw = len(ts)

    def body(*refs):
        t_refs, out_refs = refs[:nw], refs[nw:2 * nw]
        send_sems, recv_sems = refs[2 * nw:]
        x, y, c = _place()
        cps = []
        for w in range(nw):
            hh = ts[w].shape[0]
            cp = _remote(t_refs[w], out_refs[w].at[pl.ds(c * hh, hh), :], send_sems.at[w], recv_sems.at[w], (x, y, 1 - c))
            cp.start()
            cps.append(cp)
        for w in range(nw):
            hh = ts[w].shape[0]
            _remote(t_refs[w], out_refs[w].at[pl.ds((1 - c) * hh, hh), :], send_sems.at[w], recv_sems.at[w], (x, y, 1 - c)).wait_recv()
        for cp in cps:
            cp.wait_send()

    outs = pl.pallas_call(
        body, name="grad_halves_exchange", in_specs=[ANY] * nw, out_specs=[ANY] * nw,
        out_shape=[jax.ShapeDtypeStruct((2 * t.shape[0], t.shape[1]), t.dtype) for t in ts],
        scratch_shapes=[pltpu.SemaphoreType.DMA((nw,)), pltpu.SemaphoreType.DMA((nw,))],
    )(*ts)
    c = lax.axis_index("c")
    return [lax.dynamic_update_slice(o, t, (c * t.shape[0], 0)) for o, t in zip(outs, ts)]


def _adam_math(w, g, m, v):
    nm = ADAM_B1 * m + (1.0 - ADAM_B1) * g
    nv = ADAM_B2 * v + (1.0 - ADAM_B2) * (g * g)
    c1 = 1.0 - ADAM_B1 ** ADAM_STEP
    c2 = 1.0 - ADAM_B2 ** ADAM_STEP
    return -ADAM_LR * ((nm / c1) / (jnp.sqrt(nv / c2) + ADAM_EPS) + ADAM_WD * w), nm, nv


STAGE_ROWS = 32
STAGE_VEC = {"g_pre": 0, "a_ln_g": 1, "a_ln_b": 2, "mem_norm_g": 3, "g_post": 4}
STAGE_BGATE = 5
STAGE_MIX = 8
STAGE_ABS = 16


def _small_step(g, loss_row, w, m, v):
    n = len(SMALL)

    def body(*refs):
        g_r = dict(zip(SMALL, refs[:n]))
        loss_r = refs[n]
        w_r = dict(zip(SMALL, refs[n + 1:2 * n + 1]))
        m_r = dict(zip(SMALL, refs[2 * n + 1:3 * n + 1]))
        v_r = dict(zip(SMALL, refs[3 * n + 1:4 * n + 1]))
        outs = refs[4 * n + 1:8 * n + 1]
        o_r = {name: outs[4 * i:4 * i + 4] for i, name in enumerate(SMALL)}
        loss_o = refs[8 * n + 1]
        stage, ga, gw, send_sems, recv_sems = refs[8 * n + 2:]

        stage[...] = jnp.zeros_like(stage)
        for name, row in STAGE_VEC.items():
            stage[row:row + 1, :] = g_r[name][...]
        for t in range(3):
            stage[STAGE_BGATE + t:STAGE_BGATE + t + 1, :] = g_r["b_gate"][:, t * D:(t + 1) * D]
        stage[STAGE_MIX:STAGE_MIX + 1, 0:LORA] = g_r["q_norm_g"][...]
        stage[STAGE_MIX:STAGE_MIX + 1, LORA:2 * LORA] = g_r["kv_norm_g"][...]
        stage[STAGE_MIX:STAGE_MIX + 1, 2 * LORA:2 * LORA + LANES] = loss_r[...]
        stage[STAGE_ABS:STAGE_ABS + A_GROUPS, 0:CHUNK] = g_r["a_b_s"][...]

        x, y, c = _place()
        me = 4 * x + 2 * y + c
        ga[me] = stage[...]
        gw[me] = g_r["a_w_s"][...]
        cps, srcs = [], []
        for k in range(1, N_DEV):
            fx, fy, fc = (k >> 2) & 1, (k >> 1) & 1, k & 1
            peer = (1 - x if fx else x, 1 - y if fy else y, 1 - c if fc else c)
            for j, (src, dst) in enumerate(((stage, ga), (g_r["a_w_s"], gw))):
                cp = _remote(src, dst.at[me], send_sems.at[2 * (k - 1) + j], recv_sems.at[2 * (k - 1) + j], peer)
                cp.start()
                cps.append(cp)
            srcs.append(4 * peer[0] + 2 * peer[1] + peer[2])
        for k, src in enumerate(srcs):
            _remote(stage, ga.at[src], send_sems.at[2 * k], recv_sems.at[2 * k], (x, y, c)).wait_recv()
            _remote(g_r["a_w_s"], gw.at[src], send_sems.at[2 * k + 1], recv_sems.at[2 * k + 1], (x, y, c)).wait_recv()
        for cp in cps:
            cp.wait_send()
        sa, sw = ga[0], gw[0]
        for d in range(1, N_DEV):
            sa = sa + ga[d]
            sw = sw + gw[d]

        def update(name, gsum, cols=None):
            sel = (slice(None), cols) if cols is not None else Ellipsis
            delta, nm, nv = _adam_math(w_r[name][sel], gsum, m_r[name][sel], v_r[name][sel])
            for ref, val in zip(o_r[name], (gsum, delta, nm, nv)):
                ref[sel] = val

        for name, row in STAGE_VEC.items():
            update(name, sa[row:row + 1, :])
        for t in range(3):
            update("b_gate", sa[STAGE_BGATE + t:STAGE_BGATE + t + 1, :], slice(t * D, (t + 1) * D))
        update("q_norm_g", sa[STAGE_MIX:STAGE_MIX + 1, 0:LORA])
        update("kv_norm_g", sa[STAGE_MIX:STAGE_MIX + 1, LORA:2 * LORA])
        update("a_b_s", sa[STAGE_ABS:STAGE_ABS + A_GROUPS, 0:CHUNK])
        update("a_w_s", sw)
        loss_o[...] = sa[STAGE_MIX:STAGE_MIX + 1, 2 * LORA:2 * LORA + LANES]

    vm = pl.BlockSpec(memory_space=pltpu.VMEM)
    ins = [g[k] for k in SMALL] + [loss_row] + [w[k] for k in SMALL] + [m[k] for k in SMALL] + [v[k] for k in SMALL]
    out_shape = [jax.ShapeDtypeStruct(w[k].shape, F32) for k in SMALL for _ in range(4)] + [jax.ShapeDtypeStruct((1, LANES), F32)]
    res = pl.pallas_call(
        body, name="small_allreduce_adamw", in_specs=[vm] * len(ins), out_specs=[vm] * len(out_shape), out_shape=out_shape,
        scratch_shapes=[pltpu.VMEM((STAGE_ROWS, D), F32), pltpu.VMEM((N_DEV, STAGE_ROWS, D), F32),
                        pltpu.VMEM((N_DEV, A_GROUPS, CHUNK, CHUNK), F32),
                        pltpu.SemaphoreType.DMA((2 * (N_DEV - 1),)), pltpu.SemaphoreType.DMA((2 * (N_DEV - 1),))],
        compiler_params=pltpu.CompilerParams(vmem_limit_bytes=VMEM_LIMIT),
    )(*ins)
    return {k: tuple(res[4 * i:4 * i + 4]) for i, k in enumerate(SMALL)}, res[-1]


SHARD_2D = {"w_in": (D, IN_REF // N_CHIPS), "w_uq": (LORA, HEADS * QK_DIM // N_CHIPS),
            "w_ukv": (LORA, HEADS * (QK_NOPE + V_DIM) // N_CHIPS), "w_mem_kv": (D, 2 * D // N_CHIPS),
            "w_gate": (D, 3 * D // N_CHIPS), "w_branch": (3 * D // N_CHIPS, D), "w_out": (D // N_CHIPS, D)}


def _cols(blocks):
    return jnp.concatenate([blocks[j] for j in range(N_CHIPS)], axis=1)


def _weights_from_gathered(gathered):
    full = {n: _cols(gathered[n]) for n in ("w_in", "w_uq", "w_ukv", "w_mem_kv", "w_gate")}
    full["w_branch"] = gathered["w_branch"].reshape(N_CHIPS, 3, D // N_CHIPS, D).transpose(1, 0, 2, 3).reshape(3, D, D)
    full["w_out"] = gathered["w_out"].reshape(D, D)
    w = full["w_in"]
    w_in = jnp.concatenate([w[:, :3 * D], w[:, 3 * D + 2 * LORA + QK_ROPE:], w[:, 3 * D:3 * D + 2 * LORA + QK_ROPE],
                            jnp.zeros((D, IN_PAD - IN_REF), w.dtype)], axis=1)
    wq = jnp.pad(full["w_uq"].reshape(LORA, HEADS, QK_DIM), ((0, 0), (0, 0), (0, QK_PAD - QK_DIM))).reshape(LORA, HEADS * QK_PAD)
    kv3 = full["w_ukv"].reshape(LORA, HEADS, QK_NOPE + V_DIM)
    wk = jnp.pad(kv3[:, :, :QK_NOPE], ((0, 0), (0, 0), (0, QK_PAD - QK_NOPE))).reshape(LORA, HEADS * QK_PAD)
    wv = kv3[:, :, QK_NOPE:].reshape(LORA, HEADS * V_DIM)
    return {"w_in": w_in, "wq": wq, "wk": wk, "wv": wv, "w_mem_kv": full["w_mem_kv"], "w_gate": full["w_gate"],
            "w_branch": full["w_branch"], "w_out": full["w_out"]}


def _grads_to_blocks(g):
    gi = g["w_in"]
    w_in = jnp.concatenate([gi[:, :3 * D], gi[:, 6 * D:6 * D + 2 * LORA + QK_ROPE], gi[:, 3 * D:6 * D]], axis=1)
    w_uq = g["wq"].reshape(LORA, HEADS, QK_PAD)[:, :, :QK_DIM].reshape(LORA, HEADS * QK_DIM)
    w_ukv = jnp.concatenate([g["wk"].reshape(LORA, HEADS, QK_PAD)[:, :, :QK_NOPE], g["wv"].reshape(LORA, HEADS, V_DIM)],
                            axis=2).reshape(LORA, HEADS * (QK_NOPE + V_DIM))

    def col_blocks(a):
        return a.reshape(a.shape[0], N_CHIPS, a.shape[1] // N_CHIPS).transpose(1, 0, 2)

    return {"w_in": col_blocks(w_in), "w_uq": col_blocks(w_uq), "w_ukv": col_blocks(w_ukv),
            "w_mem_kv": col_blocks(g["w_mem_kv"]), "w_gate": col_blocks(g["w_gate"]),
            "w_branch": g["w_branch"].reshape(3, N_CHIPS, D // N_CHIPS, D).transpose(1, 0, 2, 3).reshape(N_CHIPS, 3 * D // N_CHIPS, D),
            "w_out": g["w_out"].reshape(N_CHIPS, D // N_CHIPS, D)}


def _local_step(x, mem, pos_col, target, W, P):
    S = x.shape[0]
    h16, rstd_x = _rms_fwd(x, P["g_pre"], width=D, col=0, tm=256, name="pre_norm")
    memn16, rstd_m = _rms_fwd(mem, P["mem_norm_g"], width=D, col=0, tm=256, name="mem_norm")
    proj = _mm(h16, W["w_in"], "nn", tm=512, tn=1920, tk=D, out_dtype=F32, name="in_proj")

    causal = jnp.tril(jnp.ones((CHUNK, CHUNK), F32))
    wm = (P["a_w_s"] * causal[None]).astype(BF16)
    bs_t = P["a_b_s"].T
    ya = _gmlp_fwd(proj, P["a_ln_g"], P["a_ln_b"], wm, bs_t)

    inv = 1.0 / (ROPE_THETA ** (jnp.arange(0, QK_ROPE, 2, dtype=F32) / QK_ROPE))
    inv_lane = jnp.concatenate([inv, inv, jnp.zeros((LANES - QK_ROPE,), F32)])[None]
    tabs = _rope_tables(pos_col, inv_lane, tm=1024)
    cqn, rstd_q = _rms_fwd(proj, P["q_norm_g"], width=LORA, col=COL_CQ, tm=512, name="q_norm")
    ckvn, rstd_kv = _rms_fwd(proj, P["kv_norm_g"], width=LORA, col=COL_CKV, tm=512, name="kv_norm")
    q16, k16, v16 = _mla_proj(cqn, ckvn, proj, tabs, W["wq"], W["wk"], W["wv"], tm=256)
    o_b, yb, lse = _mla_fwd(q16, k16, v16, proj, t=512)

    kvm = _mm(memn16, W["w_mem_kv"], "nn", tm=256, tn=1024, tk=D, out_dtype=BF16, name="mem_kv")
    ym = _mem_fwd(proj, kvm, tm=512)

    wbs = [W["w_branch"][n] for n in range(3)]
    merged, g0, g1, g2, p0, p1, p2 = _gate_merge(h16, (ya, yb, ym), W["w_gate"], P["b_gate"], wbs, tm=512, tn=512)
    out = _mm(merged, W["w_out"], "nn", tm=512, tn=1024, tk=D, out_dtype=F32, name="out_proj")
    dy, dout, g_g_post, loss = _post_loss(x, out, target, P["g_post"], tm=256)

    g_w_out = _mm(merged, dout, "tn", tm=1024, tn=1024, tk=512, out_dtype=F32, name="gw_out")
    dmerged = _mm(dout, W["w_out"], "nt", tm=512, tn=1024, tk=D, out_dtype=F32, name="d_merged")
    dp0, dp1, dp2, dgpre, g_b_gate = _gate_bwd(dmerged, (g0, g1, g2), (p0, p1, p2), tm=256)
    g_w_gate = _mm(h16, dgpre, "tn", tm=1024, tn=1024, tk=512, out_dtype=F32, name="gw_gate")
    dh_gate = _mm(dgpre, W["w_gate"], "nt", tm=512, tn=D, tk=D, out_dtype=F32, name="dh_gate")
    g_w_branch = jnp.stack([_mm(y, dp, "tn", tm=1024, tn=1024, tk=512, out_dtype=F32, name=f"gw_branch{n}")
                            for n, (y, dp) in enumerate(((ya, dp0), (yb, dp1), (ym, dp2)))], axis=0)
    dya, dyb, dym = [_mm(dp, wbs[n], "nt", tm=512, tn=1024, tk=D, out_dtype=F32, name=f"dy_branch{n}")
                     for n, dp in enumerate((dp0, dp1, dp2))]

    dqm, dzm, dkvm = _mem_bwd(proj, kvm, dym, tm=512)
    dkvm16 = dkvm.astype(BF16)
    g_w_mem_kv = _mm(memn16, dkvm16, "tn", tm=1024, tn=1024, tk=256, out_dtype=F32, name="gw_mem_kv")
    dmemn = _mm(dkvm16, W["w_mem_kv"], "nt", tm=256, tn=1024, tk=2 * D, out_dtype=F32, name="d_memn")
    _, g_mem_norm = _rms_bwd(dmemn, mem, rstd_m, P["mem_norm_g"], width=D, col=0, tm=256, out_dtype=BF16, name="mem_norm_bwd")

    do16, dzb, stats = _mla_gate_bwd(dyb, o_b, proj, lse, tm=256)
    dq, dk, dv = _mla_bwd(q16, k16, v16, do16, stats, t=512)
    dq16, dk16, dkr = _mla_qk_post(dq, dk, tabs, tm=256)
    dv16 = dv.astype(BF16)
    g_wq = _mm(cqn, dq16, "tn", tm=512, tn=1024, tk=512, out_dtype=F32, name="gw_uq")
    g_wk = _mm(ckvn, dk16, "tn", tm=512, tn=1024, tk=512, out_dtype=F32, name="gw_uk")
    g_wv = _mm(ckvn, dv16, "tn", tm=512, tn=1024, tk=512, out_dtype=F32, name="gw_uv")
    dcqn = _mm(dq16, W["wq"], "nt", tm=512, tn=LORA, tk=HEADS * QK_PAD, out_dtype=F32, name="d_cqn")
    dckvn_k = _mm(dk16, W["wk"], "nt", tm=512, tn=LORA, tk=HEADS * QK_PAD, out_dtype=F32, name="d_ckvn_k")
    dckvn = _mm(dv16, W["wv"], "nt", tm=512, tn=LORA, tk=HEADS * V_DIM, out_dtype=F32, name="d_ckvn", add=dckvn_k)
    dcq, g_q_norm = _rms_bwd(dcqn, proj, rstd_q, P["q_norm_g"], width=LORA, col=COL_CQ, tm=512, out_dtype=BF16, name="q_norm_bwd")
    dckv, g_kv_norm = _rms_bwd(dckvn, proj, rstd_kv, P["kv_norm_g"], width=LORA, col=COL_CKV, tm=512, out_dtype=BF16, name="kv_norm_bwd")

    du, dvr, dza, gws, dsv_sum, g_ln_g, g_ln_b = _gmlp_bwd(proj, dya, P["a_ln_g"], P["a_ln_b"], wm, bs_t)
    g_a_w_s = gws * causal[None]
    g_a_b_s = dsv_sum.reshape(CHUNK, A_GROUPS, CHUNK).sum(axis=-1).T

    dproj = jnp.concatenate([du, dvr, dza, dzb, dqm, dzm, dcq, dckv, dkr], axis=1)
    g_w_in = _mm(h16, dproj, "tn", tm=1024, tn=1920, tk=512, out_dtype=F32, name="gw_in")
    dh = _mm(dproj, W["w_in"], "nt", tm=512, tn=D, tk=1920, out_dtype=F32, name="d_h", add=dh_gate)
    grad_x, g_g_pre = _rms_bwd(dh, x, rstd_x, P["g_pre"], width=D, col=0, tm=256, out_dtype=F32, name="pre_norm_bwd", residual=dy)

    big = {"w_in": g_w_in, "wq": g_wq, "wk": g_wk, "wv": g_wv, "w_mem_kv": g_w_mem_kv, "w_gate": g_w_gate,
           "w_branch": g_w_branch, "w_out": g_w_out}
    small = {"g_pre": g_g_pre, "a_ln_g": g_ln_g, "a_ln_b": g_ln_b, "a_w_s": g_a_w_s, "a_b_s": g_a_b_s,
             "q_norm_g": g_q_norm, "kv_norm_g": g_kv_norm, "mem_norm_g": g_mem_norm, "b_gate": g_b_gate, "g_post": g_g_post}
    return loss, grad_x, big, small


def kernel(x, mem, positions, g_pre, w_in, a_ln_g, a_ln_b, a_w_s, a_b_s, q_norm_g, w_uq, kv_norm_g, w_ukv, mem_norm_g, w_mem_kv, w_gate, b_gate, w_branch, w_out, g_post, loss_target, m_g_pre, m_w_in, m_a_ln_g, m_a_ln_b, m_a_w_s, m_a_b_s, m_q_norm_g, m_w_uq, m_kv_norm_g, m_w_ukv, m_mem_norm_g, m_w_mem_kv, m_w_gate, m_b_gate, m_w_branch, m_w_out, m_g_post, v_g_pre, v_w_in, v_a_ln_g, v_a_ln_b, v_a_w_s, v_a_b_s, v_q_norm_g, v_w_uq, v_kv_norm_g, v_w_ukv, v_mem_norm_g, v_w_mem_kv, v_w_gate, v_b_gate, v_w_branch, v_w_out, v_g_post):
    w = dict(g_pre=g_pre, w_in=w_in, a_ln_g=a_ln_g, a_ln_b=a_ln_b, a_w_s=a_w_s, a_b_s=a_b_s, q_norm_g=q_norm_g, w_uq=w_uq,
             kv_norm_g=kv_norm_g, w_ukv=w_ukv, mem_norm_g=mem_norm_g, w_mem_kv=w_mem_kv, w_gate=w_gate, b_gate=b_gate,
             w_branch=w_branch, w_out=w_out, g_post=g_post)
    m = dict(g_pre=m_g_pre, w_in=m_w_in, a_ln_g=m_a_ln_g, a_ln_b=m_a_ln_b, a_w_s=m_a_w_s, a_b_s=m_a_b_s, q_norm_g=m_q_norm_g,
             w_uq=m_w_uq, kv_norm_g=m_kv_norm_g, w_ukv=m_w_ukv, mem_norm_g=m_mem_norm_g, w_mem_kv=m_w_mem_kv, w_gate=m_w_gate,
             b_gate=m_b_gate, w_branch=m_w_branch, w_out=m_w_out, g_post=m_g_post)
    v = dict(g_pre=v_g_pre, w_in=v_w_in, a_ln_g=v_a_ln_g, a_ln_b=v_a_ln_b, a_w_s=v_a_w_s, a_b_s=v_a_b_s, q_norm_g=v_q_norm_g,
             w_uq=v_w_uq, kv_norm_g=v_kv_norm_g, w_ukv=v_w_ukv, mem_norm_g=v_mem_norm_g, w_mem_kv=v_w_mem_kv, w_gate=v_w_gate,
             b_gate=v_b_gate, w_branch=v_w_branch, w_out=v_w_out, g_post=v_g_post)

    def two_d(t, n):
        return t[n].reshape(SHARD_2D[n]) if n in SHARD_2D else t[n].reshape(t[n].shape[1:] if t[n].ndim > 2 else t[n].shape)

    gathered = _allgather_chips([two_d(w, n).astype(BF16) for n in BIG])
    W = _weights_from_gathered(dict(zip(BIG, gathered)))
    P = {n: two_d(w, n) for n in SMALL}

    S = x.shape[1]
    loss_row, grad_x, big, small = _local_step(x[0], mem[0], positions.reshape(S, 1), loss_target[0], W, P)

    cidx = lax.axis_index("c").astype(jnp.int32).reshape(1)
    jidx = (2 * lax.axis_index("x") + lax.axis_index("y")).astype(jnp.int32).reshape(1)
    blocks = _grads_to_blocks(big)
    blocks = [blocks[n] for n in BIG]
    from_sibling = _pair_exchange(blocks)
    sums = [_pair_add(b, r, cidx, name=f"grad_pair_add_{n}") for n, b, r in zip(BIG, blocks, from_sibling)]
    from_chips = _chip_exchange([p16 for _, p16 in sums])
    totals = [_chip_add(p32, r, jidx, name=f"grad_chip_add_{n}") for n, (p32, _), r in zip(BIG, sums, from_chips)]
    reduced = _halves_exchange(totals)
    res = {}
    for n, g_n in zip(BIG, reduced):
        upd = _adamw(two_d(w, n), g_n, two_d(m, n), two_d(v, n), tr=_row_tile(g_n.shape[0], g_n.shape[1], unit=8), name=f"adamw_{n}")
        for key, t in zip(("grad", "delta", "new_m", "new_v"), (g_n,) + tuple(upd)):
            res[key, n] = t.reshape(w[n].shape)

    small_out, loss_sum = _small_step(small, loss_row, P, {n: two_d(m, n) for n in SMALL}, {n: two_d(v, n) for n in SMALL})
    for n in SMALL:
        for key, t in zip(("grad", "delta", "new_m", "new_v"), small_out[n]):
            res[key, n] = t.reshape(w[n].shape)
    loss = loss_sum[0, 0]

    outs = [loss, grad_x[None]]
    for key in ("grad", "delta", "new_m", "new_v"):
        outs += [res[key, n] for n in WEIGHTS]
    return tuple(outs)
```

```python
import functools
import math

import jax
import jax.numpy as jnp
from jax import lax
from jax.experimental import pallas as pl
from jax.experimental.pallas import tpu as pltpu

F32 = jnp.float32
BF16 = jnp.bfloat16

D = 2048
EPS = 1e-6
CHUNK = 128
A_GROUPS = 16
HEADS = 16
QK_NOPE = 128
QK_ROPE = 64
QK_DIM = QK_NOPE + QK_ROPE
V_DIM = 128
LORA = 512
MEM_HEADS = 4
MEM_HEAD_DIM = 512
ROPE_THETA = 10000.0
QK_PAD = 256
IN_REF = 13376
IN_PAD = 13440
COL_U, COL_V, COL_ZA, COL_ZB, COL_QM, COL_ZM = 0, 1, 2, 3, 4, 5
COL_CQ, COL_CKV = 24, 25
COL_KR = 104

ADAM_LR = 0.001
ADAM_B1 = 0.9
ADAM_B2 = 0.999
ADAM_EPS = 1e-08
ADAM_WD = 0.01
ADAM_STEP = 10

VMEM_LIMIT = 56 * 1024 * 1024
LANES = 128
LOG2E = math.log2(math.e)

BIG = ("w_in", "w_uq", "w_ukv", "w_mem_kv", "w_gate", "w_branch", "w_out")
SMALL = ("g_pre", "a_ln_g", "a_ln_b", "a_w_s", "a_b_s", "q_norm_g", "kv_norm_g", "mem_norm_g", "b_gate", "g_post")
WEIGHTS = ("g_pre", "w_in", "a_ln_g", "a_ln_b", "a_w_s", "a_b_s", "q_norm_g", "w_uq", "kv_norm_g", "w_ukv",
           "mem_norm_g", "w_mem_kv", "w_gate", "b_gate", "w_branch", "w_out", "g_post")
N_CHIPS = 4
N_DEV = 8


def _params(sem=None):
    return pltpu.CompilerParams(dimension_semantics=sem, vmem_limit_bytes=VMEM_LIMIT)


def _sigmoid(z):
    return 1.0 / (1.0 + jnp.exp(-z))


def _gelu_parts(x):
    c = math.sqrt(2.0 / math.pi)
    x2 = x * x
    t = jnp.tanh(c * (x + 0.044715 * x * x2))
    g = 0.5 * x * (1.0 + t)
    dg = 0.5 * (1.0 + t) + 0.5 * x * (1.0 - t * t) * (c * (1.0 + 3.0 * 0.044715 * x2))
    return g, dg


def _silu_parts(z):
    s = _sigmoid(z)
    return z * s, s * (1.0 + z * (1.0 - s))


def _dot(a, b, dims):
    return lax.dot_general(a, b, (dims, ((), ())), preferred_element_type=F32)


NN = ((1,), (0,))
NT = ((1,), (1,))
TN = ((0,), (0,))


def _mm(a, b, mode, *, tm, tn, tk, out_dtype, name, add=None):
    if mode == "nn":
        (M, K), (_, N) = a.shape, b.shape
    elif mode == "nt":
        (M, K), (N, _) = a.shape, b.shape
    else:
        (K, M), (_, N) = a.shape, b.shape
    tm, tn, tk = min(tm, M), min(tn, N), min(tk, K)
    assert M % tm == 0 and N % tn == 0 and K % tk == 0, (name, M, N, K, tm, tn, tk)
    ni, nj, nk = M // tm, N // tn, K // tk
    dims = {"nn": NN, "nt": NT, "tn": TN}[mode]
    has_add = add is not None

    def body(*refs):
        a_ref, b_ref = refs[0], refs[1]
        add_ref = refs[2] if has_add else None
        o_ref = refs[3] if has_add else refs[2]
        part = _dot(a_ref[...].astype(BF16), b_ref[...].astype(BF16), dims)

        def finish(r):
            if has_add:
                r = r + add_ref[...]
            o_ref[...] = r.astype(out_dtype)

        if nk == 1:
            finish(part)
        else:
            acc = refs[-1]
            k = pl.program_id(2)

            @pl.when(k == 0)
            def _():
                acc[...] = part

            @pl.when(k > 0)
            def _():
                acc[...] += part

            @pl.when(k == nk - 1)
            def _():
                finish(acc[...])

    if mode == "nn":
        a_spec = pl.BlockSpec((tm, tk), lambda j, i, k: (i, k))
        b_spec = pl.BlockSpec((tk, tn), lambda j, i, k: (k, j))
    elif mode == "nt":
        a_spec = pl.BlockSpec((tm, tk), lambda j, i, k: (i, k))
        b_spec = pl.BlockSpec((tn, tk), lambda j, i, k: (j, k))
    else:
        a_spec = pl.BlockSpec((tk, tm), lambda j, i, k: (k, i))
        b_spec = pl.BlockSpec((tk, tn), lambda j, i, k: (k, j))
    o_spec = pl.BlockSpec((tm, tn), lambda j, i, k: (i, j))
    in_specs = [a_spec, b_spec] + ([o_spec] if has_add else [])
    args = (a, b) + ((add,) if has_add else ())
    return pl.pallas_call(
        body, name=name, grid=(nj, ni, nk), in_specs=in_specs, out_specs=o_spec,
        out_shape=jax.ShapeDtypeStruct((M, N), out_dtype),
        scratch_shapes=[pltpu.VMEM((tm, tn), F32)] if nk > 1 else [],
        compiler_params=_params(("parallel", "parallel", "arbitrary")),
    )(*args)


def _rms_fwd(x, g, *, width, col, tm, name):
    rows = x.shape[0]
    tm = min(tm, rows)

    def body(x_ref, g_ref, y_ref, r_ref):
        xv = x_ref[...]
        r = lax.rsqrt(jnp.mean(xv * xv, axis=-1, keepdims=True) + EPS)
        y_ref[...] = ((xv * r) * g_ref[...]).astype(BF16)
        r_ref[...] = r

    return pl.pallas_call(
        body, name=name, grid=(rows // tm,),
        in_specs=[pl.BlockSpec((tm, width), lambda i: (i, col)), pl.BlockSpec((1, width), lambda i: (0, 0))],
        out_specs=[pl.BlockSpec((tm, width), lambda i: (i, 0)), pl.BlockSpec((tm, 1), lambda i: (i, 0))],
        out_shape=[jax.ShapeDtypeStruct((rows, width), BF16), jax.ShapeDtypeStruct((rows, 1), F32)],
        compiler_params=_params(("parallel",)),
    )(x, g)


def _rms_bwd(d, x, rstd, g, *, width, col, tm, out_dtype, name, residual=None):
    rows = d.shape[0]
    tm = min(tm, rows)
    has_res = residual is not None

    def body(*refs):
        d_ref, x_ref, r_ref, g_ref = refs[:4]
        res_ref = refs[4] if has_res else None
        dx_ref, gg_ref = refs[-2], refs[-1]
        dv = d_ref[...]
        n = x_ref[...] * r_ref[...]

        @pl.when(pl.program_id(0) == 0)
        def _():
            gg_ref[...] = jnp.zeros_like(gg_ref)

        gg_ref[...] += jnp.sum(dv * n, axis=0, keepdims=True)
        gd = dv * g_ref[...]
        dx = r_ref[...] * (gd - n * jnp.mean(gd * n, axis=-1, keepdims=True))
        if has_res:
            dx = dx + res_ref[...]
        dx_ref[...] = dx.astype(out_dtype)

    blk = pl.BlockSpec((tm, width), lambda i: (i, 0))
    in_specs = [blk, pl.BlockSpec((tm, width), lambda i: (i, col)), pl.BlockSpec((tm, 1), lambda i: (i, 0)),
                pl.BlockSpec((1, width), lambda i: (0, 0))] + ([blk] if has_res else [])
    args = (d, x, rstd, g) + ((residual,) if has_res else ())
    return pl.pallas_call(
        body, name=name, grid=(rows // tm,), in_specs=in_specs,
        out_specs=[blk, pl.BlockSpec((1, width), lambda i: (0, 0))],
        out_shape=[jax.ShapeDtypeStruct((rows, width), out_dtype), jax.ShapeDtypeStruct((1, width), F32)],
        compiler_params=_params(("arbitrary",)),
    )(*args)


def _rope_tables(pos_col, inv_lane, *, tm):
    rows = pos_col.shape[0]
    tm = min(tm, rows)

    def body(p_ref, f_ref, c_ref, s1_ref, s2_ref):
        ang = p_ref[...].astype(F32) * f_ref[...]
        lane = lax.broadcasted_iota(jnp.int32, ang.shape, 1)
        c, s = jnp.cos(ang), jnp.sin(ang)
        half = QK_ROPE // 2
        c_ref[...] = jnp.where(lane < QK_ROPE, c, 0.0)
        s1_ref[...] = jnp.where(lane < half, -s, 0.0)
        s2_ref[...] = jnp.where((lane >= half) & (lane < QK_ROPE), s, 0.0)

    blk = pl.BlockSpec((tm, LANES), lambda i: (i, 0))
    return pl.pallas_call(
        body, name="rope_tables", grid=(rows // tm,),
        in_specs=[pl.BlockSpec((tm, 1), lambda i: (i, 0)), pl.BlockSpec((1, LANES), lambda i: (0, 0))],
        out_specs=[blk, blk, blk], out_shape=[jax.ShapeDtypeStruct((rows, LANES), F32)] * 3,
        compiler_params=_params(("parallel",)),
    )(pos_col, inv_lane)


def _rot(t, c, s1, s2, sign):
    r1 = pltpu.roll(t, LANES - QK_ROPE // 2, 1) * s1
    r2 = pltpu.roll(t, QK_ROPE // 2, 1) * s2
    return t * c + (r1 + r2) if sign > 0 else t * c - (r1 + r2)


def _mla_proj(cqn, ckvn, proj, tabs, wq, wk, wv, *, tm):
    rows = cqn.shape[0]
    tm = min(tm, rows)

    def body(cq_ref, ckv_ref, kr_ref, c_ref, s1_ref, s2_ref, wq_ref, wk_ref, wv_ref, q_ref, k_ref, v_ref):
        c, s1, s2 = c_ref[...], s1_ref[...], s2_ref[...]
        q = _dot(cq_ref[...], wq_ref[...], NN)
        k = _dot(ckv_ref[...], wk_ref[...], NN)
        kpe = _rot(kr_ref[...], c, s1, s2, 1).astype(BF16)
        for h in range(HEADS):
            lo = h * QK_PAD
            q_ref[:, lo:lo + QK_NOPE] = q[:, lo:lo + QK_NOPE].astype(BF16)
            q_ref[:, lo + QK_NOPE:lo + QK_PAD] = _rot(q[:, lo + QK_NOPE:lo + QK_PAD], c, s1, s2, 1).astype(BF16)
            k_ref[:, lo:lo + QK_NOPE] = k[:, lo:lo + QK_NOPE].astype(BF16)
            k_ref[:, lo + QK_NOPE:lo + QK_PAD] = kpe
        v_ref[...] = _dot(ckv_ref[...], wv_ref[...], NN).astype(BF16)

    def row(w):
        return pl.BlockSpec((tm, w), lambda i: (i, 0))

    def whole(w):
        return pl.BlockSpec(w.shape, lambda i: (0, 0))

    return pl.pallas_call(
        body, name="mla_proj", grid=(rows // tm,),
        in_specs=[row(LORA), row(LORA), pl.BlockSpec((tm, LANES), lambda i: (i, COL_KR)), row(LANES), row(LANES), row(LANES),
                  whole(wq), whole(wk), whole(wv)],
        out_specs=[row(HEADS * QK_PAD), row(HEADS * QK_PAD), row(HEADS * V_DIM)],
        out_shape=[jax.ShapeDtypeStruct((rows, HEADS * QK_PAD), BF16), jax.ShapeDtypeStruct((rows, HEADS * QK_PAD), BF16),
                   jax.ShapeDtypeStruct((rows, HEADS * V_DIM), BF16)],
        compiler_params=_params(("parallel",)),
    )(cqn, ckvn, proj, *tabs, wq, wk, wv)


def _mla_fwd(q, k, v, proj, *, t):
    S = q.shape[0]
    t = min(t, S)
    n = S // t
    scale = QK_DIM ** -0.5

    def body(q_ref, k_ref, v_ref, z_ref, o_ref, y_ref, lse_ref):
        qi = pl.program_id(1)
        qv = q_ref[...]
        c2 = scale * LOG2E

        def block(k0, width, carry, row0):
            m_old, l_old, acc = carry
            ks = pl.ds(pl.multiple_of(k0, t), width)
            s = _dot(qv, k_ref[ks, :], NT)
            if row0 is not None:
                r = lax.broadcasted_iota(jnp.int32, s.shape, 0)
                c = lax.broadcasted_iota(jnp.int32, s.shape, 1)
                s = jnp.where(c <= r + row0, s, -1e30)
            m_new = jnp.maximum(m_old, jnp.max(s, axis=-1, keepdims=True))
            alpha = jnp.exp2((m_old - m_new) * c2)
            p = jnp.exp2((s - m_new) * c2)
            l_new = alpha * l_old + jnp.sum(p, axis=-1, keepdims=True)
            acc = alpha * acc + _dot(p.astype(BF16), v_ref[ks, :], NN)
            return m_new, l_new, acc

        init = (jnp.full((t, 1), -1e30, F32), jnp.zeros((t, 1), F32), jnp.zeros((t, V_DIM), F32))
        carry = lax.fori_loop(0, qi // 2, lambda j, cr: block(j * (2 * t), 2 * t, cr, None), init)
        m_f, l_f, acc = lax.cond(qi % 2 == 1,
                                 lambda cr: block((qi - 1) * t, 2 * t, cr, t),
                                 lambda cr: block(qi * t, t, cr, 0), carry)
        o = acc / l_f
        o_ref[...] = o
        sz, _ = _silu_parts(z_ref[...])
        y_ref[...] = (o * sz).astype(BF16)
        lse_ref[0] = m_f * scale + jnp.log(l_f)

    zcol = COL_ZB * (D // V_DIM)
    return pl.pallas_call(
        body, name="mla_fwd", grid=(HEADS, n),
        in_specs=[pl.BlockSpec((t, QK_PAD), lambda h, i: (i, h)),
                  pl.BlockSpec((S, QK_PAD), lambda h, i: (0, h)),
                  pl.BlockSpec((S, V_DIM), lambda h, i: (0, h)),
                  pl.BlockSpec((t, V_DIM), lambda h, i: (i, zcol + h))],
        out_specs=[pl.BlockSpec((t, V_DIM), lambda h, i: (i, h)), pl.BlockSpec((t, V_DIM), lambda h, i: (i, h)),
                   pl.BlockSpec((1, t, 1), lambda h, i: (h, i, 0))],
        out_shape=[jax.ShapeDtypeStruct((S, HEADS * V_DIM), F32), jax.ShapeDtypeStruct((S, HEADS * V_DIM), BF16),
                   jax.ShapeDtypeStruct((HEADS, S, 1), F32)],
        compiler_params=_params(("parallel", "parallel")),
    )(q, k, v, proj)


def _mla_gate_bwd(dy, o, proj, lse, *, tm):
    S = dy.shape[0]
    tm = min(tm, S)

    def body(dy_ref, o_ref, z_ref, lse_ref, do_ref, dz_ref, st_ref):
        sz, dsz = _silu_parts(z_ref[...])
        dyv, ov = dy_ref[...], o_ref[...]
        do = dyv * sz
        do_ref[...] = do.astype(BF16)
        dz_ref[...] = (dyv * ov * dsz).astype(BF16)
        prod = do * ov
        lane = lax.broadcasted_iota(jnp.int32, (tm, LANES), 1)
        for h in range(HEADS):
            delta = jnp.sum(prod[:, h * V_DIM:(h + 1) * V_DIM], axis=-1, keepdims=True)
            cols = jnp.where(lane == 0, lse_ref[h] * LOG2E, jnp.where(lane == 1, delta, 0.0))
            st_ref[h, 0] = cols.T[0:8, :]

    blk = pl.BlockSpec((tm, D), lambda i: (i, 0))
    return pl.pallas_call(
        body, name="mla_gate_bwd", grid=(S // tm,),
        in_specs=[blk, blk, pl.BlockSpec((tm, D), lambda i: (i, COL_ZB)), pl.BlockSpec((HEADS, tm, 1), lambda i: (0, i, 0))],
        out_specs=[blk, blk, pl.BlockSpec((HEADS, 1, 8, tm), lambda i: (0, i, 0, 0))],
        out_shape=[jax.ShapeDtypeStruct((S, D), BF16), jax.ShapeDtypeStruct((S, D), BF16),
                   jax.ShapeDtypeStruct((HEADS, S // tm, 8, tm), F32)],
        compiler_params=_params(("parallel",)),
    )(dy, o, proj, lse)


def _mla_bwd(q, k, v, do, stats, *, t):
    S = q.shape[0]
    t = min(t, S)
    n = S // t
    c2 = (QK_DIM ** -0.5) * LOG2E

    def body(q_ref, k_ref, v_ref, do_ref, st_ref, dq_ref, dk_ref, dv_ref):
        ki = pl.program_id(1)

        @pl.when(ki == 0)
        def _():
            dq_ref[...] = jnp.zeros_like(dq_ref)

        kv, vv = k_ref[...], v_ref[...]

        def block(i, carry, diag):
            dk, dv = carry
            rows = pl.ds(pl.multiple_of(i * t, t), t)
            qv, dov, st = q_ref[rows, :], do_ref[rows, :], st_ref[0, i]
            s = _dot(kv, qv, NT)
            if diag:
                key = lax.broadcasted_iota(jnp.int32, s.shape, 0)
                qry = lax.broadcasted_iota(jnp.int32, s.shape, 1)
                s = jnp.where(key <= qry, s, -1e30)
            p = jnp.exp2(s * c2 - st[0:1, :])
            p16 = p.astype(BF16)
            dv = dv + _dot(p16, dov, NN)
            dp = _dot(vv, dov, NT)
            ds = (p * (dp - st[1:2, :])).astype(BF16)
            dk = dk + _dot(ds, qv, NN)
            dq_ref[rows, :] += _dot(ds, kv, TN)
            return dk, dv

        carry = block(ki, (jnp.zeros((t, QK_PAD), F32), jnp.zeros((t, V_DIM), F32)), True)
        rest = n - 1 - ki
        carry = lax.cond(rest % 2 == 1, lambda cr: block(ki + 1, cr, False), lambda cr: cr, carry)
        first = ki + 1 + rest % 2

        def pair(i, cr):
            return block(first + 2 * i + 1, block(first + 2 * i, cr, False), False)

        dk, dv = lax.fori_loop(0, rest // 2, pair, carry)
        dk_ref[...] = dk
        dv_ref[...] = dv

    return pl.pallas_call(
        body, name="mla_bwd", grid=(HEADS, n),
        in_specs=[pl.BlockSpec((S, QK_PAD), lambda h, j: (0, h)),
                  pl.BlockSpec((t, QK_PAD), lambda h, j: (j, h)),
                  pl.BlockSpec((t, V_DIM), lambda h, j: (j, h)),
                  pl.BlockSpec((S, V_DIM), lambda h, j: (0, h)),
                  pl.BlockSpec((1, n, 8, t), lambda h, j: (h, 0, 0, 0))],
        out_specs=[pl.BlockSpec((S, QK_PAD), lambda h, j: (0, h)),
                   pl.BlockSpec((t, QK_PAD), lambda h, j: (j, h)),
                   pl.BlockSpec((t, V_DIM), lambda h, j: (j, h))],
        out_shape=[jax.ShapeDtypeStruct((S, HEADS * QK_PAD), F32), jax.ShapeDtypeStruct((S, HEADS * QK_PAD), F32),
                   jax.ShapeDtypeStruct((S, HEADS * V_DIM), F32)],
        compiler_params=_params(("arbitrary", "arbitrary")),
    )(q, k, v, do, stats)


def _mla_qk_post(dq, dk, tabs, *, tm):
    S = dq.shape[0]
    tm = min(tm, S)
    scale = QK_DIM ** -0.5

    def body(dq_ref, dk_ref, c_ref, s1_ref, s2_ref, q16_ref, k16_ref, kr_ref):
        c, s1, s2 = c_ref[...] * scale, s1_ref[...] * scale, s2_ref[...] * scale
        kpe = jnp.zeros((tm, LANES), F32)
        for h in range(HEADS):
            lo = h * QK_PAD
            q16_ref[:, lo:lo + QK_NOPE] = (dq_ref[:, lo:lo + QK_NOPE] * scale).astype(BF16)
            q16_ref[:, lo + QK_NOPE:lo + QK_PAD] = _rot(dq_ref[:, lo + QK_NOPE:lo + QK_PAD], c, s1, s2, -1).astype(BF16)
            kpe = kpe + dk_ref[:, lo + QK_NOPE:lo + QK_PAD]
        k16_ref[...] = (dk_ref[...] * scale).astype(BF16)
        kr_ref[...] = _rot(kpe, c, s1, s2, -1).astype(BF16)

    wide = pl.BlockSpec((tm, HEADS * QK_PAD), lambda i: (i, 0))
    lane = pl.BlockSpec((tm, LANES), lambda i: (i, 0))
    return pl.pallas_call(
        body, name="mla_qk_post", grid=(S // tm,),
        in_specs=[wide, wide, lane, lane, lane], out_specs=[wide, wide, lane],
        out_shape=[jax.ShapeDtypeStruct((S, HEADS * QK_PAD), BF16), jax.ShapeDtypeStruct((S, HEADS * QK_PAD), BF16),
                   jax.ShapeDtypeStruct((S, LANES), BF16)],
        compiler_params=_params(("parallel",)),
    )(dq, dk, *tabs)


def _mem_scores(q16, km_ref, h):
    lo = h * MEM_HEAD_DIM
    s = _dot(q16, km_ref[:, lo:lo + MEM_HEAD_DIM], NT) * (MEM_HEAD_DIM ** -0.5)
    e = jnp.exp(s - jnp.max(s, axis=-1, keepdims=True))
    return e / jnp.sum(e, axis=-1, keepdims=True)


def _mem_fwd(proj, kvm, *, tm):
    S = proj.shape[0]
    tm = min(tm, S)
    M = kvm.shape[0]

    def body(q_ref, z_ref, km_ref, vm_ref, y_ref):
        sz, _ = _silu_parts(z_ref[...])
        for h in range(MEM_HEADS):
            lo = h * MEM_HEAD_DIM
            p = _mem_scores(q_ref[:, lo:lo + MEM_HEAD_DIM].astype(BF16), km_ref, h)
            o = _dot(p.astype(BF16), vm_ref[:, lo:lo + MEM_HEAD_DIM], NN)
            y_ref[:, lo:lo + MEM_HEAD_DIM] = (o * sz[:, lo:lo + MEM_HEAD_DIM]).astype(BF16)

    return pl.pallas_call(
        body, name="mem_fwd", grid=(S // tm,),
        in_specs=[pl.BlockSpec((tm, D), lambda i: (i, COL_QM)), pl.BlockSpec((tm, D), lambda i: (i, COL_ZM)),
                  pl.BlockSpec((M, D), lambda i: (0, 0)), pl.BlockSpec((M, D), lambda i: (0, 1))],
        out_specs=pl.BlockSpec((tm, D), lambda i: (i, 0)),
        out_shape=jax.ShapeDtypeStruct((S, D), BF16),
        compiler_params=_params(("parallel",)),
    )(proj, proj, kvm, kvm)


def _mem_bwd(proj, kvm, dy, *, tm):
    S = proj.shape[0]
    tm = min(tm, S)
    M = kvm.shape[0]
    scale = MEM_HEAD_DIM ** -0.5

    def body(q_ref, z_ref, km_ref, vm_ref, dy_ref, dq_ref, dz_ref, dkv_ref):
        @pl.when(pl.program_id(0) == 0)
        def _():
            dkv_ref[...] = jnp.zeros_like(dkv_ref)

        sz, dsz = _silu_parts(z_ref[...])
        dyv = dy_ref[...]
        for h in range(MEM_HEADS):
            lo = h * MEM_HEAD_DIM
            sl = slice(lo, lo + MEM_HEAD_DIM)
            q16 = q_ref[:, sl].astype(BF16)
            p = _mem_scores(q16, km_ref, h)
            p16 = p.astype(BF16)
            o = _dot(p16, vm_ref[:, sl], NN)
            dy_h = dyv[:, sl]
            dz_ref[:, sl] = (dy_h * o * dsz[:, sl]).astype(BF16)
            do16 = (dy_h * sz[:, sl]).astype(BF16)
            dp = _dot(do16, vm_ref[:, sl], NT)
            ds = (p * (dp - jnp.sum(dp * p, axis=-1, keepdims=True)) * scale).astype(BF16)
            dq_ref[:, sl] = _dot(ds, km_ref[:, sl], NN).astype(BF16)
            dkv_ref[:, sl] += _dot(ds, q16, TN)
            dkv_ref[:, D + lo:D + lo + MEM_HEAD_DIM] += _dot(p16, do16, TN)

    blk = pl.BlockSpec((tm, D), lambda i: (i, 0))
    return pl.pallas_call(
        body, name="mem_bwd", grid=(S // tm,),
        in_specs=[pl.BlockSpec((tm, D), lambda i: (i, COL_QM)), pl.BlockSpec((tm, D), lambda i: (i, COL_ZM)),
                  pl.BlockSpec((M, D), lambda i: (0, 0)), pl.BlockSpec((M, D), lambda i: (0, 1)), blk],
        out_specs=[blk, blk, pl.BlockSpec((M, 2 * D), lambda i: (0, 0))],
        out_shape=[jax.ShapeDtypeStruct((S, D), BF16), jax.ShapeDtypeStruct((S, D), BF16),
                   jax.ShapeDtypeStruct((M, 2 * D), F32)],
        compiler_params=_params(("arbitrary",)),
    )(proj, proj, kvm, kvm, dy)


def _gmlp_common(u_ref, v_ref, lng_ref, lnb_ref):
    u, du = _gelu_parts(u_ref[...])
    vg, dvg = _gelu_parts(v_ref[...])
    mu = jnp.mean(vg, axis=-1, keepdims=True)
    vc = vg - mu
    r = lax.rsqrt(jnp.mean(vc * vc, axis=-1, keepdims=True) + EPS)
    vhat = vc * r
    vn = vhat * lng_ref[...] + lnb_ref[...]
    return u, du, dvg, r, vhat, vn.astype(BF16)


def _gmlp_fwd(proj, ln_g, ln_b, wm, bs_t):
    S = proj.shape[0]

    def body(u_ref, v_ref, z_ref, lng_ref, lnb_ref, wm_ref, bs_ref, y_ref):
        u, _, _, _, _, v16 = _gmlp_common(u_ref, v_ref, lng_ref, lnb_ref)
        sz, _ = _silu_parts(z_ref[...])
        for g in range(A_GROUPS):
            sl = slice(g * CHUNK, (g + 1) * CHUNK)
            sv = _dot(wm_ref[g], v16[:, sl], NN) + bs_ref[:, g:g + 1]
            y_ref[:, sl] = (u[:, sl] * sv * sz[:, sl]).astype(BF16)

    def col(c):
        return pl.BlockSpec((CHUNK, D), lambda i: (i, c))

    vec = pl.BlockSpec((1, D), lambda i: (0, 0))
    return pl.pallas_call(
        body, name="gmlp_fwd", grid=(S // CHUNK,),
        in_specs=[col(COL_U), col(COL_V), col(COL_ZA), vec, vec,
                  pl.BlockSpec((A_GROUPS, CHUNK, CHUNK), lambda i: (0, 0, 0)), pl.BlockSpec((CHUNK, A_GROUPS), lambda i: (0, 0))],
        out_specs=col(0), out_shape=jax.ShapeDtypeStruct((S, D), BF16),
        compiler_params=_params(("parallel",)),
    )(proj, proj, proj, ln_g, ln_b, wm, bs_t)


def _gmlp_bwd(proj, dy, ln_g, ln_b, wm, bs_t):
    S = proj.shape[0]

    def body(u_ref, v_ref, z_ref, dy_ref, lng_ref, lnb_ref, wm_ref, bs_ref,
             du_ref, dv_ref, dz_ref, gws_ref, dsv_ref, glg_ref, glb_ref, dvn_s):
        @pl.when(pl.program_id(0) == 0)
        def _():
            gws_ref[...] = jnp.zeros_like(gws_ref)
            dsv_ref[...] = jnp.zeros_like(dsv_ref)
            glg_ref[...] = jnp.zeros_like(glg_ref)
            glb_ref[...] = jnp.zeros_like(glb_ref)

        u, du, dvg, r, vhat, v16 = _gmlp_common(u_ref, v_ref, lng_ref, lnb_ref)
        sz, dsz = _silu_parts(z_ref[...])
        dyv = dy_ref[...]
        for g in range(A_GROUPS):
            sl = slice(g * CHUNK, (g + 1) * CHUNK)
            sv = _dot(wm_ref[g], v16[:, sl], NN) + bs_ref[:, g:g + 1]
            dy_g, u_g, sz_g = dyv[:, sl], u[:, sl], sz[:, sl]
            dsv = dy_g * u_g * sz_g
            du_ref[:, sl] = (dy_g * sv * sz_g * du[:, sl]).astype(BF16)
            dz_ref[:, sl] = (dy_g * u_g * sv * dsz[:, sl]).astype(BF16)
            dsv16 = dsv.astype(BF16)
            dvn_s[:, sl] = _dot(wm_ref[g], dsv16, TN)
            gws_ref[g] += _dot(dsv16, v16[:, sl], NT)
            dsv_ref[:, sl] += dsv
        dvn = dvn_s[...]
        glb_ref[...] += jnp.sum(dvn, axis=0, keepdims=True)
        glg_ref[...] += jnp.sum(dvn * vhat, axis=0, keepdims=True)
        dvh = dvn * lng_ref[...]
        dvc = r * (dvh - jnp.mean(dvh, axis=-1, keepdims=True) - vhat * jnp.mean(dvh * vhat, axis=-1, keepdims=True))
        dv_ref[...] = (dvc * dvg).astype(BF16)

    def col(c):
        return pl.BlockSpec((CHUNK, D), lambda i: (i, c))

    vec = pl.BlockSpec((1, D), lambda i: (0, 0))
    wsp = pl.BlockSpec((A_GROUPS, CHUNK, CHUNK), lambda i: (0, 0, 0))
    return pl.pallas_call(
        body, name="gmlp_bwd", grid=(S // CHUNK,),
        in_specs=[col(COL_U), col(COL_V), col(COL_ZA), col(0), vec, vec, wsp, pl.BlockSpec((CHUNK, A_GROUPS), lambda i: (0, 0))],
        out_specs=[col(0), col(0), col(0), wsp, pl.BlockSpec((CHUNK, D), lambda i: (0, 0)), vec, vec],
        out_shape=[jax.ShapeDtypeStruct((S, D), BF16)] * 3 + [
            jax.ShapeDtypeStruct((A_GROUPS, CHUNK, CHUNK), F32), jax.ShapeDtypeStruct((CHUNK, D), F32),
            jax.ShapeDtypeStruct((1, D), F32), jax.ShapeDtypeStruct((1, D), F32)],
        scratch_shapes=[pltpu.VMEM((CHUNK, D), F32)],
        compiler_params=_params(("arbitrary",)),
    )(proj, proj, proj, dy, ln_g, ln_b, wm, bs_t)


def _gate_merge(h16, ys, wg, bg, wbs, *, tm, tn):
    S = h16.shape[0]
    tm = min(tm, S)
    nj = D // tn

    def body(h_ref, ya_ref, yb_ref, ym_ref, wg0, wg1, wg2, bg0, bg1, bg2, wb0, wb1, wb2,
             mg_ref, g0_ref, g1_ref, g2_ref, p0_ref, p1_ref, p2_ref):
        hv = h_ref[...]
        acc = None
        for y_ref, wg_ref, bgr, wb_ref, g_ref, p_ref in ((ya_ref, wg0, bg0, wb0, g0_ref, p0_ref),
                                                         (yb_ref, wg1, bg1, wb1, g1_ref, p1_ref),
                                                         (ym_ref, wg2, bg2, wb2, g2_ref, p2_ref)):
            gate = _sigmoid(_dot(hv, wg_ref[...], NN) + bgr[...])
            p = _dot(y_ref[...], wb_ref[...], NN)
            g_ref[...] = gate.astype(BF16)
            p_ref[...] = p.astype(BF16)
            acc = gate * p if acc is None else acc + gate * p
        mg_ref[...] = acc.astype(BF16)

    a_spec = pl.BlockSpec((tm, D), lambda j, i: (i, 0))
    o_spec = pl.BlockSpec((tm, tn), lambda j, i: (i, j))

    def wgs(n):
        return pl.BlockSpec((D, tn), lambda j, i: (0, n * nj + j))

    def bgs(n):
        return pl.BlockSpec((1, tn), lambda j, i: (0, n * nj + j))

    wbsp = pl.BlockSpec((D, tn), lambda j, i: (0, j))
    return pl.pallas_call(
        body, name="gate_merge", grid=(nj, S // tm),
        in_specs=[a_spec] * 4 + [wgs(0), wgs(1), wgs(2), bgs(0), bgs(1), bgs(2), wbsp, wbsp, wbsp],
        out_specs=[o_spec] * 7, out_shape=[jax.ShapeDtypeStruct((S, D), BF16)] * 7,
        compiler_params=_params(("parallel", "parallel")),
    )(h16, *ys, wg, wg, wg, bg, bg, bg, *wbs)


def _gate_bwd(dmerged, gates, ps, *, tm):
    S = dmerged.shape[0]
    tm = min(tm, S)

    def body(dm_ref, g0, g1, g2, p0, p1, p2, dp0, dp1, dp2, dg_ref, gb_ref):
        @pl.when(pl.program_id(0) == 0)
        def _():
            gb_ref[...] = jnp.zeros_like(gb_ref)

        dm = dm_ref[...]
        for n, (g_ref, p_ref, dp_ref) in enumerate(((g0, p0, dp0), (g1, p1, dp1), (g2, p2, dp2))):
            gate = g_ref[...].astype(F32)
            dp_ref[...] = (dm * gate).astype(BF16)
            dg = dm * p_ref[...].astype(F32) * gate * (1.0 - gate)
            dg_ref[:, n * D:(n + 1) * D] = dg.astype(BF16)
            gb_ref[:, n * D:(n + 1) * D] += jnp.sum(dg, axis=0, keepdims=True)

    blk = pl.BlockSpec((tm, D), lambda i: (i, 0))
    return pl.pallas_call(
        body, name="gate_bwd", grid=(S // tm,),
        in_specs=[blk] * 7,
        out_specs=[blk, blk, blk, pl.BlockSpec((tm, 3 * D), lambda i: (i, 0)), pl.BlockSpec((1, 3 * D), lambda i: (0, 0))],
        out_shape=[jax.ShapeDtypeStruct((S, D), BF16)] * 3 + [jax.ShapeDtypeStruct((S, 3 * D), BF16),
                                                              jax.ShapeDtypeStruct((1, 3 * D), F32)],
        compiler_params=_params(("arbitrary",)),
    )(dmerged, *gates, *ps)


def _post_loss(x, out, target, g_post, *, tm):
    S = x.shape[0]
    tm = min(tm, S)

    def body(x_ref, o_ref, t_ref, g_ref, dy_ref, do_ref, gg_ref, ls_ref):
        @pl.when(pl.program_id(0) == 0)
        def _():
            gg_ref[...] = jnp.zeros_like(gg_ref)
            ls_ref[...] = jnp.zeros_like(ls_ref)

        ov = o_ref[...]
        r = lax.rsqrt(jnp.mean(ov * ov, axis=-1, keepdims=True) + EPS)
        n = ov * r
        err = (x_ref[...] + n * g_ref[...]) - t_ref[...]
        ls_ref[...] += 0.5 * jnp.sum(jnp.mean(err * err, axis=-1, keepdims=True))
        dy = err * (1.0 / D)
        dy_ref[...] = dy
        gg_ref[...] += jnp.sum(dy * n, axis=0, keepdims=True)
        gd = dy * g_ref[...]
        do_ref[...] = (r * (gd - n * jnp.mean(gd * n, axis=-1, keepdims=True))).astype(BF16)

    blk = pl.BlockSpec((tm, D), lambda i: (i, 0))
    vec = pl.BlockSpec((1, D), lambda i: (0, 0))
    return pl.pallas_call(
        body, name="post_loss", grid=(S // tm,),
        in_specs=[blk, blk, blk, vec],
        out_specs=[blk, blk, vec, pl.BlockSpec((1, LANES), lambda i: (0, 0))],
        out_shape=[jax.ShapeDtypeStruct((S, D), F32), jax.ShapeDtypeStruct((S, D), BF16),
                   jax.ShapeDtypeStruct((1, D), F32), jax.ShapeDtypeStruct((1, LANES), F32)],
        compiler_params=_params(("arbitrary",)),
    )(x, out, target, g_post)


def _adamw(w, g, m, v, *, tr, name):
    rows, width = w.shape
    tr = min(tr, rows)
    assert rows % tr == 0
    c1 = 1.0 - ADAM_B1 ** ADAM_STEP
    c2 = 1.0 - ADAM_B2 ** ADAM_STEP

    def body(w_ref, g_ref, m_ref, v_ref, d_ref, nm_ref, nv_ref):
        gv = g_ref[...]
        nm = ADAM_B1 * m_ref[...] + (1.0 - ADAM_B1) * gv
        nv = ADAM_B2 * v_ref[...] + (1.0 - ADAM_B2) * (gv * gv)
        d_ref[...] = -ADAM_LR * ((nm / c1) / (jnp.sqrt(nv / c2) + ADAM_EPS) + ADAM_WD * w_ref[...])
        nm_ref[...] = nm
        nv_ref[...] = nv

    blk = pl.BlockSpec((tr, width), lambda i: (i, 0))
    return pl.pallas_call(
        body, name=name, grid=(rows // tr,), in_specs=[blk] * 4, out_specs=[blk] * 3,
        out_shape=[jax.ShapeDtypeStruct((rows, width), F32)] * 3,
        compiler_params=_params(("parallel",)),
    )(w, g, m, v)


MESH = pl.DeviceIdType.MESH
ANY = pl.BlockSpec(memory_space=pl.ANY)


def _place():
    return lax.axis_index("x"), lax.axis_index("y"), lax.axis_index("c")


def _other_chips(x, y):
    return [(1 - x, y), (x, 1 - y), (1 - x, 1 - y)]


def _remote(src, dst, send_sem, recv_sem, dev):
    return pltpu.make_async_remote_copy(src_ref=src, dst_ref=dst, send_sem=send_sem, recv_sem=recv_sem,
                                        device_id=dev, device_id_type=MESH)


def _allgather_chips(shards):
    nw = len(shards)

    def body(*refs):
        x_refs, out_refs = refs[:nw], refs[nw:2 * nw]
        send_sems, recv_sems = refs[2 * nw:]
        x, y, c = _place()
        sibling = (x, y, 1 - c)
        chips = _other_chips(x, y)

        def half(w, px, py, hc):
            hh = shards[w].shape[0] // 2
            return out_refs[w].at[2 * px + py, pl.ds(hc * hh, hh), :]

        sent = []
        for w in range(nw):
            hh = shards[w].shape[0] // 2
            for k, (px, py) in enumerate(chips):
                cp = _remote(x_refs[w].at[pl.ds(c * hh, hh), :], half(w, x, y, c), send_sems.at[6 * w + k],
                             recv_sems.at[6 * w + k], (px, py, c))
                cp.start()
                sent.append(cp)
        for w in range(nw):
            for k, (px, py) in enumerate(chips):
                landed = half(w, px, py, c)
                _remote(landed, landed, send_sems.at[6 * w + k], recv_sems.at[6 * w + k], (px, py, c)).wait_recv()
                cp = _remote(landed, landed, send_sems.at[6 * w + 3 + k], recv_sems.at[6 * w + 3 + k], sibling)
                cp.start()
                sent.append(cp)
        for w in range(nw):
            for k, (px, py) in enumerate(chips):
                other = half(w, px, py, 1 - c)
                _remote(other, other, send_sems.at[6 * w + 3 + k], recv_sems.at[6 * w + 3 + k], sibling).wait_recv()
        for cp in sent:
            cp.wait_send()

    outs = pl.pallas_call(
        body, name="allgather_weights", in_specs=[ANY] * nw, out_specs=[ANY] * nw,
        out_shape=[jax.ShapeDtypeStruct((N_CHIPS,) + s.shape, s.dtype) for s in shards],
        scratch_shapes=[pltpu.SemaphoreType.DMA((6 * nw,)), pltpu.SemaphoreType.DMA((6 * nw,))],
    )(*shards)
    own = 2 * lax.axis_index("x") + lax.axis_index("y")
    return [lax.dynamic_update_slice(o, s[None], (own, 0, 0)) for o, s in zip(outs, shards)]


def _pair_exchange(gs):
    nw = len(gs)

    def body(*refs):
        g_refs, out_refs = refs[:nw], refs[nw:2 * nw]
        send_sems, recv_sems = refs[2 * nw:]
        x, y, c = _place()
        cps = []
        for w in range(nw):
            hh = gs[w].shape[1] // 2
            cp = _remote(g_refs[w].at[:, pl.ds((1 - c) * hh, hh), :], out_refs[w], send_sems.at[w], recv_sems.at[w], (x, y, 1 - c))
            cp.start()
            cps.append(cp)
        for cp in cps:
            cp.wait()

    return pl.pallas_call(
        body, name="grad_pair_exchange", in_specs=[ANY] * nw, out_specs=[ANY] * nw,
        out_shape=[jax.ShapeDtypeStruct((N_CHIPS, g.shape[1] // 2, g.shape[2]), g.dtype) for g in gs],
        scratch_shapes=[pltpu.SemaphoreType.DMA((nw,)), pltpu.SemaphoreType.DMA((nw,))],
    )(*gs)


def _row_tile(rows, cols, unit=16, budget=2 * 1024 * 1024):
    best = unit
    for t in range(unit, rows + 1, unit):
        if rows % t == 0 and t * cols * 4 <= budget:
            best = t
    assert rows % best == 0, (rows, cols)
    return best


def _pair_add(g, recv, cidx, *, name):
    _, R, W = g.shape
    hh = R // 2
    tr = _row_tile(hh, W)
    nb = hh // tr

    def body(c_ref, g_ref, r_ref, o32_ref, o16_ref):
        s = g_ref[...] + r_ref[...]
        o32_ref[...] = s
        o16_ref[...] = s.astype(BF16)

    blk = pl.BlockSpec((1, tr, W), lambda j, i, c_ref: (j, i, 0))
    return pl.pallas_call(
        body, name=name,
        grid_spec=pltpu.PrefetchScalarGridSpec(
            num_scalar_prefetch=1, grid=(N_CHIPS, nb),
            in_specs=[pl.BlockSpec((1, tr, W), lambda j, i, c_ref: (j, i + c_ref[0] * nb, 0)), blk],
            out_specs=[blk, blk]),
        out_shape=[jax.ShapeDtypeStruct((N_CHIPS, hh, W), F32), jax.ShapeDtypeStruct((N_CHIPS, hh, W), BF16)],
        compiler_params=_params(("parallel", "parallel")),
    )(cidx, g, recv)


def _chip_exchange(ps):
    nw = len(ps)

    def body(*refs):
        p_refs, out_refs = refs[:nw], refs[nw:2 * nw]
        send_sems, recv_sems = refs[2 * nw:]
        x, y, c = _place()
        cps = []
        for w in range(nw):
            for k, (px, py) in enumerate(_other_chips(x, y)):
                cp = _remote(p_refs[w].at[2 * px + py], out_refs[w].at[k], send_sems.at[3 * w + k], recv_sems.at[3 * w + k],
                             (px, py, c))
                cp.start()
                cps.append(cp)
        for cp in cps:
            cp.wait()

    return pl.pallas_call(
        body, name="grad_chip_exchange", in_specs=[ANY] * nw, out_specs=[ANY] * nw,
        out_shape=[jax.ShapeDtypeStruct((3,) + p.shape[1:], p.dtype) for p in ps],
        scratch_shapes=[pltpu.SemaphoreType.DMA((3 * nw,)), pltpu.SemaphoreType.DMA((3 * nw,))],
    )(*ps)


def _chip_add(p32, recv, jidx, *, name):
    _, hh, W = p32.shape
    tr = _row_tile(hh, W, budget=1024 * 1024)

    def body(j_ref, p_ref, r_ref, o_ref):
        s = p_ref[0]
        for k in range(3):
            s = s + r_ref[k].astype(F32)
        o_ref[...] = s

    return pl.pallas_call(
        body, name=name,
        grid_spec=pltpu.PrefetchScalarGridSpec(
            num_scalar_prefetch=1, grid=(hh // tr,),
            in_specs=[pl.BlockSpec((1, tr, W), lambda i, j_ref: (j_ref[0], i, 0)),
                      pl.BlockSpec((3, tr, W), lambda i, j_ref: (0, i, 0))],
            out_specs=pl.BlockSpec((tr, W), lambda i, j_ref: (i, 0))),
        out_shape=jax.ShapeDtypeStruct((hh, W), F32),
        compiler_params=_params(("parallel",)),
    )(jidx, p32, recv)


def _halves_exchange(ts):
    nw = len(ts)

    def body(*refs):
        t_refs, out_refs = refs[:nw], refs[nw:2 * nw]
        send_sems, recv_sems = refs[2 * nw:]
        x, y, c = _place()
        cps = []
        for w in range(nw):
            hh = ts[w].shape[0]
            cp = _remote(t_refs[w], out_refs[w].at[pl.ds(c * hh, hh), :], send_sems.at[w], recv_sems.at[w], (x, y, 1 - c))
            cp.start()
            cps.append(cp)
        for w in range(nw):
            hh = ts[w].shape[0]
            _remote(t_refs[w], out_refs[w].at[pl.ds((1 - c) * hh, hh), :], send_sems.at[w], recv_sems.at[w], (x, y, 1 - c)).wait_recv()
        for cp in cps:
            cp.wait_send()

    outs = pl.pallas_call(
        body, name="grad_halves_exchange", in_specs=[ANY] * nw, out_specs=[ANY] * nw,
        out_shape=[jax.ShapeDtypeStruct((2 * t.shape[0], t.shape[1]), t.dtype) for t in ts],
        scratch_shapes=[pltpu.SemaphoreType.DMA((nw,)), pltpu.SemaphoreType.DMA((nw,))],
    )(*ts)
    c = lax.axis_index("c")
    return [lax.dynamic_update_slice(o, t, (c * t.shape[0], 0)) for o, t in zip(outs, ts)]


def _adam_math(w, g, m, v):
    nm = ADAM_B1 * m + (1.0 - ADAM_B1) * g
    nv = ADAM_B2 * v + (1.0 - ADAM_B2) * (g * g)
    c1 = 1.0 - ADAM_B1 ** ADAM_STEP
    c2 = 1.0 - ADAM_B2 ** ADAM_STEP
    return -ADAM_LR * ((nm / c1) / (jnp.sqrt(nv / c2) + ADAM_EPS) + ADAM_WD * w), nm, nv


STAGE_ROWS = 32
STAGE_VEC = {"g_pre": 0, "a_ln_g": 1, "a_ln_b": 2, "mem_norm_g": 3, "g_post": 4}
STAGE_BGATE = 5
STAGE_MIX = 8
STAGE_ABS = 16


def _small_step(g, loss_row, w, m, v):
    n = len(SMALL)

    def body(*refs):
        g_r = dict(zip(SMALL, refs[:n]))
        loss_r = refs[n]
        w_r = dict(zip(SMALL, refs[n + 1:2 * n + 1]))
        m_r = dict(zip(SMALL, refs[2 * n + 1:3 * n + 1]))
        v_r = dict(zip(SMALL, refs[3 * n + 1:4 * n + 1]))
        outs = refs[4 * n + 1:8 * n + 1]
        o_r = {name: outs[4 * i:4 * i + 4] for i, name in enumerate(SMALL)}
        loss_o = refs[8 * n + 1]
        stage, ga, gw, send_sems, recv_sems = refs[8 * n + 2:]

        stage[...] = jnp.zeros_like(stage)
        for name, row in STAGE_VEC.items():
            stage[row:row + 1, :] = g_r[name][...]
        for t in range(3):
            stage[STAGE_BGATE + t:STAGE_BGATE + t + 1, :] = g_r["b_gate"][:, t * D:(t + 1) * D]
        stage[STAGE_MIX:STAGE_MIX + 1, 0:LORA] = g_r["q_norm_g"][...]
        stage[STAGE_MIX:STAGE_MIX + 1, LORA:2 * LORA] = g_r["kv_norm_g"][...]
        stage[STAGE_MIX:STAGE_MIX + 1, 2 * LORA:2 * LORA + LANES] = loss_r[...]
        stage[STAGE_ABS:STAGE_ABS + A_GROUPS, 0:CHUNK] = g_r["a_b_s"][...]

        x, y, c = _place()
        me = 4 * x + 2 * y + c
        ga[me] = stage[...]
        gw[me] = g_r["a_w_s"][...]
        cps, srcs = [], []
        for k in range(1, N_DEV):
            fx, fy, fc = (k >> 2) & 1, (k >> 1) & 1, k & 1
            peer = (1 - x if fx else x, 1 - y if fy else y, 1 - c if fc else c)
            for j, (src, dst) in enumerate(((stage, ga), (g_r["a_w_s"], gw))):
                cp = _remote(src, dst.at[me], send_sems.at[2 * (k - 1) + j], recv_sems.at[2 * (k - 1) + j], peer)
                cp.start()
                cps.append(cp)
            srcs.append(4 * peer[0] + 2 * peer[1] + peer[2])
        for k, src in enumerate(srcs):
            _remote(stage, ga.at[src], send_sems.at[2 * k], recv_sems.at[2 * k], (x, y, c)).wait_recv()
            _remote(g_r["a_w_s"], gw.at[src], send_sems.at[2 * k + 1], recv_sems.at[2 * k + 1], (x, y, c)).wait_recv()
        for cp in cps:
            cp.wait_send()
        sa, sw = ga[0], gw[0]
        for d in range(1, N_DEV):
            sa = sa + ga[d]
            sw = sw + gw[d]

        def update(name, gsum, cols=None):
            sel = (slice(None), cols) if cols is not None else Ellipsis
            delta, nm, nv = _adam_math(w_r[name][sel], gsum, m_r[name][sel], v_r[name][sel])
            for ref, val in zip(o_r[name], (gsum, delta, nm, nv)):
                ref[sel] = val

        for name, row in STAGE_VEC.items():
            update(name, sa[row:row + 1, :])
        for t in range(3):
            update("b_gate", sa[STAGE_BGATE + t:STAGE_BGATE + t + 1, :], slice(t * D, (t + 1) * D))
        update("q_norm_g", sa[STAGE_MIX:STAGE_MIX + 1, 0:LORA])
        update("kv_norm_g", sa[STAGE_MIX:STAGE_MIX + 1, LORA:2 * LORA])
        update("a_b_s", sa[STAGE_ABS:STAGE_ABS + A_GROUPS, 0:CHUNK])
        update("a_w_s", sw)
        loss_o[...] = sa[STAGE_MIX:STAGE_MIX + 1, 2 * LORA:2 * LORA + LANES]

    vm = pl.BlockSpec(memory_space=pltpu.VMEM)
    ins = [g[k] for k in SMALL] + [loss_row] + [w[k] for k in SMALL] + [m[k] for k in SMALL] + [v[k] for k in SMALL]
    out_shape = [jax.ShapeDtypeStruct(w[k].shape, F32) for k in SMALL for _ in range(4)] + [jax.ShapeDtypeStruct((1, LANES), F32)]
    res = pl.pallas_call(
        body, name="small_allreduce_adamw", in_specs=[vm] * len(ins), out_specs=[vm] * len(out_shape), out_shape=out_shape,
        scratch_shapes=[pltpu.VMEM((STAGE_ROWS, D), F32), pltpu.VMEM((N_DEV, STAGE_ROWS, D), F32),
                        pltpu.VMEM((N_DEV, A_GROUPS, CHUNK, CHUNK), F32),
                        pltpu.SemaphoreType.DMA((2 * (N_DEV - 1),)), pltpu.SemaphoreType.DMA((2 * (N_DEV - 1),))],
        compiler_params=pltpu.CompilerParams(vmem_limit_bytes=VMEM_LIMIT),
    )(*ins)
    return {k: tuple(res[4 * i:4 * i + 4]) for i, k in enumerate(SMALL)}, res[-1]


SHARD_2D = {"w_in": (D, IN_REF // N_CHIPS), "w_uq": (LORA, HEADS * QK_DIM // N_CHIPS),
            "w_ukv": (LORA, HEADS * (QK_NOPE + V_DIM) // N_CHIPS), "w_mem_kv": (D, 2 * D // N_CHIPS),
            "w_gate": (D, 3 * D // N_CHIPS), "w_branch": (3 * D // N_CHIPS, D), "w_out": (D // N_CHIPS, D)}


def _cols(blocks):
    return jnp.concatenate([blocks[j] for j in range(N_CHIPS)], axis=1)


def _weights_from_gathered(gathered):
    full = {n: _cols(gathered[n]) for n in ("w_in", "w_uq", "w_ukv", "w_mem_kv", "w_gate")}
    full["w_branch"] = gathered["w_branch"].reshape(N_CHIPS, 3, D // N_CHIPS, D).transpose(1, 0, 2, 3).reshape(3, D, D)
    full["w_out"] = gathered["w_out"].reshape(D, D)
    w = full["w_in"]
    w_in = jnp.concatenate([w[:, :3 * D], w[:, 3 * D + 2 * LORA + QK_ROPE:], w[:, 3 * D:3 * D + 2 * LORA + QK_ROPE],
                            jnp.zeros((D, IN_PAD - IN_REF), w.dtype)], axis=1)
    wq = jnp.pad(full["w_uq"].reshape(LORA, HEADS, QK_DIM), ((0, 0), (0, 0), (0, QK_PAD - QK_DIM))).reshape(LORA, HEADS * QK_PAD)
    kv3 = full["w_ukv"].reshape(LORA, HEADS, QK_NOPE + V_DIM)
    wk = jnp.pad(kv3[:, :, :QK_NOPE], ((0, 0), (0, 0), (0, QK_PAD - QK_NOPE))).reshape(LORA, HEADS * QK_PAD)
    wv = kv3[:, :, QK_NOPE:].reshape(LORA, HEADS * V_DIM)
    return {"w_in": w_in, "wq": wq, "wk": wk, "wv": wv, "w_mem_kv": full["w_mem_kv"], "w_gate": full["w_gate"],
            "w_branch": full["w_branch"], "w_out": full["w_out"]}


def _grads_to_blocks(g):
    gi = g["w_in"]
    w_in = jnp.concatenate([gi[:, :3 * D], gi[:, 6 * D:6 * D + 2 * LORA + QK_ROPE], gi[:, 3 * D:6 * D]], axis=1)
    w_uq = g["wq"].reshape(LORA, HEADS, QK_PAD)[:, :, :QK_DIM].reshape(LORA, HEADS * QK_DIM)
    w_ukv = jnp.concatenate([g["wk"].reshape(LORA, HEADS, QK_PAD)[:, :, :QK_NOPE], g["wv"].reshape(LORA, HEADS, V_DIM)],
                            axis=2).reshape(LORA, HEADS * (QK_NOPE + V_DIM))

    def col_blocks(a):
        return a.reshape(a.shape[0], N_CHIPS, a.shape[1] // N_CHIPS).transpose(1, 0, 2)

    return {"w_in": col_blocks(w_in), "w_uq": col_blocks(w_uq), "w_ukv": col_blocks(w_ukv),
            "w_mem_kv": col_blocks(g["w_mem_kv"]), "w_gate": col_blocks(g["w_gate"]),
            "w_branch": g["w_branch"].reshape(3, N_CHIPS, D // N_CHIPS, D).transpose(1, 0, 2, 3).reshape(N_CHIPS, 3 * D // N_CHIPS, D),
            "w_out": g["w_out"].reshape(N_CHIPS, D // N_CHIPS, D)}


def _local_step(x, mem, pos_col, target, W, P):
    S = x.shape[0]
    h16, rstd_x = _rms_fwd(x, P["g_pre"], width=D, col=0, tm=256, name="pre_norm")
    memn16, rstd_m = _rms_fwd(mem, P["mem_norm_g"], width=D, col=0, tm=256, name="mem_norm")
    proj = _mm(h16, W["w_in"], "nn", tm=512, tn=1920, tk=D, out_dtype=F32, name="in_proj")

    causal = jnp.tril(jnp.ones((CHUNK, CHUNK), F32))
    wm = (P["a_w_s"] * causal[None]).astype(BF16)
    bs_t = P["a_b_s"].T
    ya = _gmlp_fwd(proj, P["a_ln_g"], P["a_ln_b"], wm, bs_t)

    inv = 1.0 / (ROPE_THETA ** (jnp.arange(0, QK_ROPE, 2, dtype=F32) / QK_ROPE))
    inv_lane = jnp.concatenate([inv, inv, jnp.zeros((LANES - QK_ROPE,), F32)])[None]
    tabs = _rope_tables(pos_col, inv_lane, tm=1024)
    cqn, rstd_q = _rms_fwd(proj, P["q_norm_g"], width=LORA, col=COL_CQ, tm=512, name="q_norm")
    ckvn, rstd_kv = _rms_fwd(proj, P["kv_norm_g"], width=LORA, col=COL_CKV, tm=512, name="kv_norm")
    q16, k16, v16 = _mla_proj(cqn, ckvn, proj, tabs, W["wq"], W["wk"], W["wv"], tm=256)
    o_b, yb, lse = _mla_fwd(q16, k16, v16, proj, t=512)

    kvm = _mm(memn16, W["w_mem_kv"], "nn", tm=256, tn=1024, tk=D, out_dtype=BF16, name="mem_kv")
    ym = _mem_fwd(proj, kvm, tm=512)

    wbs = [W["w_branch"][n] for n in range(3)]
    merged, g0, g1, g2, p0, p1, p2 = _gate_merge(h16, (ya, yb, ym), W["w_gate"], P["b_gate"], wbs, tm=512, tn=512)
    out = _mm(merged, W["w_out"], "nn", tm=512, tn=1024, tk=D, out_dtype=F32, name="out_proj")
    dy, dout, g_g_post, loss = _post_loss(x, out, target, P["g_post"], tm=256)

    g_w_out = _mm(merged, dout, "tn", tm=1024, tn=1024, tk=512, out_dtype=F32, name="gw_out")
    dmerged = _mm(dout, W["w_out"], "nt", tm=512, tn=1024, tk=D, out_dtype=F32, name="d_merged")
    dp0, dp1, dp2, dgpre, g_b_gate = _gate_bwd(dmerged, (g0, g1, g2), (p0, p1, p2), tm=256)
    g_w_gate = _mm(h16, dgpre, "tn", tm=1024, tn=1024, tk=512, out_dtype=F32, name="gw_gate")
    dh_gate = _mm(dgpre, W["w_gate"], "nt", tm=512, tn=D, tk=D, out_dtype=F32, name="dh_gate")
    g_w_branch = jnp.stack([_mm(y, dp, "tn", tm=1024, tn=1024, tk=512, out_dtype=F32, name=f"gw_branch{n}")
                            for n, (y, dp) in enumerate(((ya, dp0), (yb, dp1), (ym, dp2)))], axis=0)
    dya, dyb, dym = [_mm(dp, wbs[n], "nt", tm=512, tn=1024, tk=D, out_dtype=F32, name=f"dy_branch{n}")
                     for n, dp in enumerate((dp0, dp1, dp2))]

    dqm, dzm, dkvm = _mem_bwd(proj, kvm, dym, tm=512)
    dkvm16 = dkvm.astype(BF16)
    g_w_mem_kv = _mm(memn16, dkvm16, "tn", tm=1024, tn=1024, tk=256, out_dtype=F32, name="gw_mem_kv")
    dmemn = _mm(dkvm16, W["w_mem_kv"], "nt", tm=256, tn=1024, tk=2 * D, out_dtype=F32, name="d_memn")
    _, g_mem_norm = _rms_bwd(dmemn, mem, rstd_m, P["mem_norm_g"], width=D, col=0, tm=256, out_dtype=BF16, name="mem_norm_bwd")

    do16, dzb, stats = _mla_gate_bwd(dyb, o_b, proj, lse, tm=512)
    dq, dk, dv = _mla_bwd(q16, k16, v16, do16, stats, t=512)
    dq16, dk16, dkr = _mla_qk_post(dq, dk, tabs, tm=256)
    dv16 = dv.astype(BF16)
    g_wq = _mm(cqn, dq16, "tn", tm=512, tn=1024, tk=512, out_dtype=F32, name="gw_uq")
    g_wk = _mm(ckvn, dk16, "tn", tm=512, tn=1024, tk=512, out_dtype=F32, name="gw_uk")
    g_wv = _mm(ckvn, dv16, "tn", tm=512, tn=1024, tk=512, out_dtype=F32, name="gw_uv")
    dcqn = _mm(dq16, W["wq"], "nt", tm=512, tn=LORA, tk=HEADS * QK_PAD, out_dtype=F32, name="d_cqn")
    dckvn_k = _mm(dk16, W["wk"], "nt", tm=512, tn=LORA, tk=HEADS * QK_PAD, out_dtype=F32, name="d_ckvn_k")
    dckvn = _mm(dv16, W["wv"], "nt", tm=512, tn=LORA, tk=HEADS * V_DIM, out_dtype=F32, name="d_ckvn", add=dckvn_k)
    dcq, g_q_norm = _rms_bwd(dcqn, proj, rstd_q, P["q_norm_g"], width=LORA, col=COL_CQ, tm=512, out_dtype=BF16, name="q_norm_bwd")
    dckv, g_kv_norm = _rms_bwd(dckvn, proj, rstd_kv, P["kv_norm_g"], width=LORA, col=COL_CKV, tm=512, out_dtype=BF16, name="kv_norm_bwd")

    du, dvr, dza, gws, dsv_sum, g_ln_g, g_ln_b = _gmlp_bwd(proj, dya, P["a_ln_g"], P["a_ln_b"], wm, bs_t)
    g_a_w_s = gws * causal[None]
    g_a_b_s = dsv_sum.reshape(CHUNK, A_GROUPS, CHUNK).sum(axis=-1).T

    dproj = jnp.concatenate([du, dvr, dza, dzb, dqm, dzm, dcq, dckv, dkr], axis=1)
    g_w_in = _mm(h16, dproj, "tn", tm=1024, tn=1920, tk=512, out_dtype=F32, name="gw_in")
    dh = _mm(dproj, W["w_in"], "nt", tm=512, tn=D, tk=1920, out_dtype=F32, name="d_h", add=dh_gate)
    grad_x, g_g_pre = _rms_bwd(dh, x, rstd_x, P["g_pre"], width=D, col=0, tm=256, out_dtype=F32, name="pre_norm_bwd", residual=dy)

    big = {"w_in": g_w_in, "wq": g_wq, "wk": g_wk, "wv": g_wv, "w_mem_kv": g_w_mem_kv, "w_gate": g_w_gate,
           "w_branch": g_w_branch, "w_out": g_w_out}
    small = {"g_pre": g_g_pre, "a_ln_g": g_ln_g, "a_ln_b": g_ln_b, "a_w_s": g_a_w_s, "a_b_s": g_a_b_s,
             "q_norm_g": g_q_norm, "kv_norm_g": g_kv_norm, "mem_norm_g": g_mem_norm, "b_gate": g_b_gate, "g_post": g_g_post}
    return loss, grad_x, big, small


def kernel(x, mem, positions, g_pre, w_in, a_ln_g, a_ln_b, a_w_s, a_b_s, q_norm_g, w_uq, kv_norm_g, w_ukv, mem_norm_g, w_mem_kv, w_gate, b_gate, w_branch, w_out, g_post, loss_target, m_g_pre, m_w_in, m_a_ln_g, m_a_ln_b, m_a_w_s, m_a_b_s, m_q_norm_g, m_w_uq, m_kv_norm_g, m_w_ukv, m_mem_norm_g, m_w_mem_kv, m_w_gate, m_b_gate, m_w_branch, m_w_out, m_g_post, v_g_pre, v_w_in, v_a_ln_g, v_a_ln_b, v_a_w_s, v_a_b_s, v_q_norm_g, v_w_uq, v_kv_norm_g, v_w_ukv, v_mem_norm_g, v_w_mem_kv, v_w_gate, v_b_gate, v_w_branch, v_w_out, v_g_post):
    w = dict(g_pre=g_pre, w_in=w_in, a_ln_g=a_ln_g, a_ln_b=a_ln_b, a_w_s=a_w_s, a_b_s=a_b_s, q_norm_g=q_norm_g, w_uq=w_uq,
             kv_norm_g=kv_norm_g, w_ukv=w_ukv, mem_norm_g=mem_norm_g, w_mem_kv=w_mem_kv, w_gate=w_gate, b_gate=b_gate,
             w_branch=w_branch, w_out=w_out, g_post=g_post)
    m = dict(g_pre=m_g_pre, w_in=m_w_in, a_ln_g=m_a_ln_g, a_ln_b=m_a_ln_b, a_w_s=m_a_w_s, a_b_s=m_a_b_s, q_norm_g=m_q_norm_g,
             w_uq=m_w_uq, kv_norm_g=m_kv_norm_g, w_ukv=m_w_ukv, mem_norm_g=m_mem_norm_g, w_mem_kv=m_w_mem_kv, w_gate=m_w_gate,
             b_gate=m_b_gate, w_branch=m_w_branch, w_out=m_w_out, g_post=m_g_post)
    v = dict(g_pre=v_g_pre, w_in=v_w_in, a_ln_g=v_a_ln_g, a_ln_b=v_a_ln_b, a_w_s=v_a_w_s, a_b_s=v_a_b_s, q_norm_g=v_q_norm_g,
             w_uq=v_w_uq, kv_norm_g=v_kv_norm_g, w_ukv=v_w_ukv, mem_norm_g=v_mem_norm_g, w_mem_kv=v_w_mem_kv, w_gate=v_w_gate,
             b_gate=v_b_gate, w_branch=v_w_branch, w_out=v_w_out, g_post=v_g_post)

    def two_d(t, n):
        return t[n].reshape(SHARD_2D[n]) if n in SHARD_2D else t[n].reshape(t[n].shape[1:] if t[n].ndim > 2 else t[n].shape)

    gathered = _allgather_chips([two_d(w, n).astype(BF16) for n in BIG])
    W = _weights_from_gathered(dict(zip(BIG, gathered)))
    P = {n: two_d(w, n) for n in SMALL}

    S = x.shape[1]
    loss_row, grad_x, big, small = _local_step(x[0], mem[0], positions.reshape(S, 1), loss_target[0], W, P)

    cidx = lax.axis_index("c").astype(jnp.int32).reshape(1)
    jidx = (2 * lax.axis_index("x") + lax.axis_index("y")).astype(jnp.int32).reshape(1)
    blocks = _grads_to_blocks(big)
    blocks = [blocks[n] for n in BIG]
    from_sibling = _pair_exchange(blocks)
    sums = [_pair_add(b, r, cidx, name=f"grad_pair_add_{n}") for n, b, r in zip(BIG, blocks, from_sibling)]
    from_chips = _chip_exchange([p16 for _, p16 in sums])
    totals = [_chip_add(p32, r, jidx, name=f"grad_chip_add_{n}") for n, (p32, _), r in zip(BIG, sums, from_chips)]
    reduced = _halves_exchange(totals)
    res = {}
    for n, g_n in zip(BIG, reduced):
        upd = _adamw(two_d(w, n), g_n, two_d(m, n), two_d(v, n), tr=_row_tile(g_n.shape[0], g_n.shape[1], unit=8), name=f"adamw_{n}")
        for key, t in zip(("grad", "delta", "new_m", "new_v"), (g_n,) + tuple(upd)):
            res[key, n] = t.reshape(w[n].shape)

    small_out, loss_sum = _small_step(small, loss_row, P, {n: two_d(m, n) for n in SMALL}, {n: two_d(v, n) for n in SMALL})
    for n in SMALL:
        for key, t in zip(("grad", "delta", "new_m", "new_v"), small_out[n]):
            res[key, n] = t.reshape(w[n].shape)
    loss = loss_sum[0, 0]

    outs = [loss, grad_x[None]]
    for key in ("grad", "delta", "new_m", "new_v"):
        outs += [res[key, n] for n in WEIGHTS]
    return tuple(outs)
```

```python
import math
from typing import NamedTuple

import jax
import jax.numpy as jnp
from jax import lax
from jax.experimental import pallas as pl
from jax.experimental.pallas import tpu as pltpu

F32 = jnp.float32
BF16 = jnp.bfloat16

D = 2048
EPS = 1e-6
CHUNK = 128
A_GROUPS = 16
HEADS = 16
QK_NOPE = 128
QK_ROPE = 64
QK_DIM = QK_NOPE + QK_ROPE
V_DIM = 128
LORA = 512
MEM_HEADS = 4
MEM_HEAD_DIM = 512
ROPE_THETA = 10000.0
QK_PAD = 256
IN_REF = 13376
IN_PAD = 13440
COL_U, COL_V, COL_ZA, COL_ZB, COL_QM, COL_ZM = 0, 1, 2, 3, 4, 5
COL_CQ, COL_CKV = 24, 25
COL_KR = 104

ADAM_LR = 0.001
ADAM_B1 = 0.9
ADAM_B2 = 0.999
ADAM_EPS = 1e-08
ADAM_WD = 0.01
ADAM_STEP = 10

VMEM_LIMIT = 56 * 1024 * 1024
LANES = 128
LOG2E = math.log2(math.e)

BIG = ("w_in", "w_uq", "w_ukv", "w_mem_kv", "w_gate", "w_branch", "w_out")
SMALL = ("g_pre", "a_ln_g", "a_ln_b", "a_w_s", "a_b_s", "q_norm_g", "kv_norm_g", "mem_norm_g", "b_gate", "g_post")
WEIGHTS = ("g_pre", "w_in", "a_ln_g", "a_ln_b", "a_w_s", "a_b_s", "q_norm_g", "w_uq", "kv_norm_g", "w_ukv",
           "mem_norm_g", "w_mem_kv", "w_gate", "b_gate", "w_branch", "w_out", "g_post")
N_CHIPS = 4
N_DEV = 8


def _params(sem=None):
    return pltpu.CompilerParams(dimension_semantics=sem, vmem_limit_bytes=VMEM_LIMIT)


def _sigmoid(z):
    return 1.0 / (1.0 + jnp.exp(-z))


def _gelu_parts(x):
    c = math.sqrt(2.0 / math.pi)
    x2 = x * x
    t = jnp.tanh(c * (x + 0.044715 * x * x2))
    g = 0.5 * x * (1.0 + t)
    dg = 0.5 * (1.0 + t) + 0.5 * x * (1.0 - t * t) * (c * (1.0 + 3.0 * 0.044715 * x2))
    return g, dg


def _silu_parts(z):
    s = _sigmoid(z)
    return z * s, s * (1.0 + z * (1.0 - s))


def _dot(a, b, dims):
    return lax.dot_general(a, b, (dims, ((), ())), preferred_element_type=F32)


NN = ((1,), (0,))
NT = ((1,), (1,))
TN = ((0,), (0,))


class _Rider(NamedTuple):
    ins: tuple
    out_shapes: tuple
    n_sems: int
    phases: tuple


def _ride(rider, refs_in, refs_out, sems, step, total):
    for frac, fn in rider.phases:
        @pl.when(step == int(frac * (total - 1)))
        def _():
            fn(refs_in, refs_out, sems[0], sems[1])


def _mm(a, b, mode, *, tm, tn, tk, out_dtype, name, add=None, rider=None):
    if mode == "nn":
        (M, K), (_, N) = a.shape, b.shape
    elif mode == "nt":
        (M, K), (N, _) = a.shape, b.shape
    else:
        (K, M), (_, N) = a.shape, b.shape
    tm, tn, tk = min(tm, M), min(tn, N), min(tk, K)
    assert M % tm == 0 and N % tn == 0 and K % tk == 0, (name, M, N, K, tm, tn, tk)
    ni, nj, nk = M // tm, N // tn, K // tk
    dims = {"nn": NN, "nt": NT, "tn": TN}[mode]
    has_add = add is not None
    n_rin = len(rider.ins) if rider else 0
    n_rout = len(rider.out_shapes) if rider else 0

    def body(*refs):
        a_ref, b_ref = refs[0], refs[1]
        pos = 2
        add_ref = refs[pos] if has_add else None
        pos += int(has_add)
        rin = refs[pos:pos + n_rin]
        pos += n_rin
        o_ref = refs[pos]
        rout = refs[pos + 1:pos + 1 + n_rout]
        pos += 1 + n_rout
        acc = refs[pos] if nk > 1 else None
        sems = refs[-2:] if rider else None
        if rider:
            step = (pl.program_id(0) * ni + pl.program_id(1)) * nk + pl.program_id(2)
            _ride(rider._replace(phases=rider.phases[:1]), rin, rout, sems, step, nj * ni * nk)
        part = _dot(a_ref[...].astype(BF16), b_ref[...].astype(BF16), dims)

        def finish(r):
            if has_add:
                r = r + add_ref[...]
            o_ref[...] = r.astype(out_dtype)

        if nk == 1:
            finish(part)
        else:
            k = pl.program_id(2)

            @pl.when(k == 0)
            def _():
                acc[...] = part

            @pl.when(k > 0)
            def _():
                acc[...] += part

            @pl.when(k == nk - 1)
            def _():
                finish(acc[...])

        if rider:
            _ride(rider._replace(phases=rider.phases[1:]), rin, rout, sems, step, nj * ni * nk)

    if mode == "nn":
        a_spec = pl.BlockSpec((tm, tk), lambda j, i, k: (i, k))
        b_spec = pl.BlockSpec((tk, tn), lambda j, i, k: (k, j))
    elif mode == "nt":
        a_spec = pl.BlockSpec((tm, tk), lambda j, i, k: (i, k))
        b_spec = pl.BlockSpec((tn, tk), lambda j, i, k: (j, k))
    else:
        a_spec = pl.BlockSpec((tk, tm), lambda j, i, k: (k, i))
        b_spec = pl.BlockSpec((tk, tn), lambda j, i, k: (k, j))
    o_spec = pl.BlockSpec((tm, tn), lambda j, i, k: (i, j))
    hbm = pl.BlockSpec(memory_space=pl.ANY)
    in_specs = [a_spec, b_spec] + ([o_spec] if has_add else []) + [hbm] * n_rin
    args = (a, b) + ((add,) if has_add else ()) + (tuple(rider.ins) if rider else ())
    scratch = [pltpu.VMEM((tm, tn), F32)] if nk > 1 else []
    if rider:
        scratch += [pltpu.SemaphoreType.DMA((rider.n_sems,)), pltpu.SemaphoreType.DMA((rider.n_sems,))]
    res = pl.pallas_call(
        body, name=name, grid=(nj, ni, nk), in_specs=in_specs, out_specs=[o_spec] + [hbm] * n_rout,
        out_shape=[jax.ShapeDtypeStruct((M, N), out_dtype)] + (list(rider.out_shapes) if rider else []),
        scratch_shapes=scratch,
        compiler_params=_params(("arbitrary",) * 3 if rider else ("parallel", "parallel", "arbitrary")),
    )(*args)
    return (res[0], list(res[1:])) if rider else res[0]


def _rms_fwd(x, g, *, width, col, tm, name):
    rows = x.shape[0]
    tm = min(tm, rows)

    def body(x_ref, g_ref, y_ref, r_ref):
        xv = x_ref[...]
        r = lax.rsqrt(jnp.mean(xv * xv, axis=-1, keepdims=True) + EPS)
        y_ref[...] = ((xv * r) * g_ref[...]).astype(BF16)
        r_ref[...] = r

    return pl.pallas_call(
        body, name=name, grid=(rows // tm,),
        in_specs=[pl.BlockSpec((tm, width), lambda i: (i, col)), pl.BlockSpec((1, width), lambda i: (0, 0))],
        out_specs=[pl.BlockSpec((tm, width), lambda i: (i, 0)), pl.BlockSpec((tm, 1), lambda i: (i, 0))],
        out_shape=[jax.ShapeDtypeStruct((rows, width), BF16), jax.ShapeDtypeStruct((rows, 1), F32)],
        compiler_params=_params(("parallel",)),
    )(x, g)


def _rms_bwd(d, x, rstd, g, *, width, col, tm, out_dtype, name, residual=None):
    rows = d.shape[0]
    tm = min(tm, rows)
    has_res = residual is not None

    def body(*refs):
        d_ref, x_ref, r_ref, g_ref = refs[:4]
        res_ref = refs[4] if has_res else None
        dx_ref, gg_ref = refs[-2], refs[-1]
        dv = d_ref[...]
        n = x_ref[...] * r_ref[...]

        @pl.when(pl.program_id(0) == 0)
        def _():
            gg_ref[...] = jnp.zeros_like(gg_ref)

        gg_ref[...] += jnp.sum(dv * n, axis=0, keepdims=True)
        gd = dv * g_ref[...]
        dx = r_ref[...] * (gd - n * jnp.mean(gd * n, axis=-1, keepdims=True))
        if has_res:
            dx = dx + res_ref[...]
        dx_ref[...] = dx.astype(out_dtype)

    blk = pl.BlockSpec((tm, width), lambda i: (i, 0))
    in_specs = [blk, pl.BlockSpec((tm, width), lambda i: (i, col)), pl.BlockSpec((tm, 1), lambda i: (i, 0)),
                pl.BlockSpec((1, width), lambda i: (0, 0))] + ([blk] if has_res else [])
    args = (d, x, rstd, g) + ((residual,) if has_res else ())
    return pl.pallas_call(
        body, name=name, grid=(rows // tm,), in_specs=in_specs,
        out_specs=[blk, pl.BlockSpec((1, width), lambda i: (0, 0))],
        out_shape=[jax.ShapeDtypeStruct((rows, width), out_dtype), jax.ShapeDtypeStruct((1, width), F32)],
        compiler_params=_params(("arbitrary",)),
    )(*args)


def _rope_tables(pos_col, inv_lane, *, tm):
    rows = pos_col.shape[0]
    tm = min(tm, rows)

    def body(p_ref, f_ref, c_ref, s1_ref, s2_ref):
        ang = p_ref[...].astype(F32) * f_ref[...]
        lane = lax.broadcasted_iota(jnp.int32, ang.shape, 1)
        c, s = jnp.cos(ang), jnp.sin(ang)
        half = QK_ROPE // 2
        c_ref[...] = jnp.where(lane < QK_ROPE, c, 0.0)
        s1_ref[...] = jnp.where(lane < half, -s, 0.0)
        s2_ref[...] = jnp.where((lane >= half) & (lane < QK_ROPE), s, 0.0)

    blk = pl.BlockSpec((tm, LANES), lambda i: (i, 0))
    return pl.pallas_call(
        body, name="rope_tables", grid=(rows // tm,),
        in_specs=[pl.BlockSpec((tm, 1), lambda i: (i, 0)), pl.BlockSpec((1, LANES), lambda i: (0, 0))],
        out_specs=[blk, blk, blk], out_shape=[jax.ShapeDtypeStruct((rows, LANES), F32)] * 3,
        compiler_params=_params(("parallel",)),
    )(pos_col, inv_lane)


def _rot(t, c, s1, s2, sign):
    r1 = pltpu.roll(t, LANES - QK_ROPE // 2, 1) * s1
    r2 = pltpu.roll(t, QK_ROPE // 2, 1) * s2
    return t * c + (r1 + r2) if sign > 0 else t * c - (r1 + r2)


def _mla_proj(cqn, ckvn, proj, tabs, wq, wk, wv, *, tm):
    rows = cqn.shape[0]
    tm = min(tm, rows)

    def body(cq_ref, ckv_ref, kr_ref, c_ref, s1_ref, s2_ref, wq_ref, wk_ref, wv_ref, q_ref, k_ref, v_ref):
        c, s1, s2 = c_ref[...], s1_ref[...], s2_ref[...]
        q = _dot(cq_ref[...], wq_ref[...], NN)
        k = _dot(ckv_ref[...], wk_ref[...], NN)
        kpe = _rot(kr_ref[...], c, s1, s2, 1).astype(BF16)
        for h in range(HEADS):
            lo = h * QK_PAD
            q_ref[:, lo:lo + QK_NOPE] = q[:, lo:lo + QK_NOPE].astype(BF16)
            q_ref[:, lo + QK_NOPE:lo + QK_PAD] = _rot(q[:, lo + QK_NOPE:lo + QK_PAD], c, s1, s2, 1).astype(BF16)
            k_ref[:, lo:lo + QK_NOPE] = k[:, lo:lo + QK_NOPE].astype(BF16)
            k_ref[:, lo + QK_NOPE:lo + QK_PAD] = kpe
        v_ref[...] = _dot(ckv_ref[...], wv_ref[...], NN).astype(BF16)

    def row(w):
        return pl.BlockSpec((tm, w), lambda i: (i, 0))

    def whole(w):
        return pl.BlockSpec(w.shape, lambda i: (0, 0))

    return pl.pallas_call(
        body, name="mla_proj", grid=(rows // tm,),
        in_specs=[row(LORA), row(LORA), pl.BlockSpec((tm, LANES), lambda i: (i, COL_KR)), row(LANES), row(LANES), row(LANES),
                  whole(wq), whole(wk), whole(wv)],
        out_specs=[row(HEADS * QK_PAD), row(HEADS * QK_PAD), row(HEADS * V_DIM)],
        out_shape=[jax.ShapeDtypeStruct((rows, HEADS * QK_PAD), BF16), jax.ShapeDtypeStruct((rows, HEADS * QK_PAD), BF16),
                   jax.ShapeDtypeStruct((rows, HEADS * V_DIM), BF16)],
        compiler_params=_params(("parallel",)),
    )(cqn, ckvn, proj, *tabs, wq, wk, wv)


def _mla_fwd(q, k, v, proj, *, t):
    S = q.shape[0]
    t = min(t, S)
    n = S // t
    scale = QK_DIM ** -0.5

    def body(q_ref, k_ref, v_ref, z_ref, o_ref, y_ref, lse_ref):
        qi = pl.program_id(1)
        qv = q_ref[...]
        c2 = scale * LOG2E

        def block(k0, width, carry, row0):
            m_old, l_old, acc = carry
            ks = pl.ds(pl.multiple_of(k0, t), width)
            s = _dot(qv, k_ref[ks, :], NT)
            if row0 is not None:
                r = lax.broadcasted_iota(jnp.int32, s.shape, 0)
                c = lax.broadcasted_iota(jnp.int32, s.shape, 1)
                s = jnp.where(c <= r + row0, s, -1e30)
            m_new = jnp.maximum(m_old, jnp.max(s, axis=-1, keepdims=True))
            alpha = jnp.exp2((m_old - m_new) * c2)
            p = jnp.exp2((s - m_new) * c2)
            l_new = alpha * l_old + jnp.sum(p, axis=-1, keepdims=True)
            acc = alpha * acc + _dot(p.astype(BF16), v_ref[ks, :], NN)
            return m_new, l_new, acc

        init = (jnp.full((t, 1), -1e30, F32), jnp.zeros((t, 1), F32), jnp.zeros((t, V_DIM), F32))
        carry = lax.fori_loop(0, qi // 2, lambda j, cr: block(j * (2 * t), 2 * t, cr, None), init)
        m_f, l_f, acc = lax.cond(qi % 2 == 1,
                                 lambda cr: block((qi - 1) * t, 2 * t, cr, t),
                                 lambda cr: block(qi * t, t, cr, 0), carry)
        o = acc / l_f
        o_ref[...] = o
        sz, _ = _silu_parts(z_ref[...])
        y_ref[...] = (o * sz).astype(BF16)
        lse_ref[0] = m_f * scale + jnp.log(l_f)

    zcol = COL_ZB * (D // V_DIM)
    return pl.pallas_call(
        body, name="mla_fwd", grid=(HEADS, n),
        in_specs=[pl.BlockSpec((t, QK_PAD), lambda h, i: (i, h)),
                  pl.BlockSpec((S, QK_PAD), lambda h, i: (0, h)),
                  pl.BlockSpec((S, V_DIM), lambda h, i: (0, h)),
                  pl.BlockSpec((t, V_DIM), lambda h, i: (i, zcol + h))],
        out_specs=[pl.BlockSpec((t, V_DIM), lambda h, i: (i, h)), pl.BlockSpec((t, V_DIM), lambda h, i: (i, h)),
                   pl.BlockSpec((1, t, 1), lambda h, i: (h, i, 0))],
        out_shape=[jax.ShapeDtypeStruct((S, HEADS * V_DIM), F32), jax.ShapeDtypeStruct((S, HEADS * V_DIM), BF16),
                   jax.ShapeDtypeStruct((HEADS, S, 1), F32)],
        compiler_params=_params(("parallel", "parallel")),
    )(q, k, v, proj)


def _mla_gate_bwd(dy, o, proj, lse, *, tm):
    S = dy.shape[0]
    tm = min(tm, S)

    def body(dy_ref, o_ref, z_ref, lse_ref, do_ref, dz_ref, st_ref):
        sz, dsz = _silu_parts(z_ref[...])
        dyv, ov = dy_ref[...], o_ref[...]
        do = dyv * sz
        do_ref[...] = do.astype(BF16)
        dz_ref[...] = (dyv * ov * dsz).astype(BF16)
        prod = do * ov
        lane = lax.broadcasted_iota(jnp.int32, (tm, LANES), 1)
        for h in range(HEADS):
            delta = jnp.sum(prod[:, h * V_DIM:(h + 1) * V_DIM], axis=-1, keepdims=True)
            cols = jnp.where(lane == 0, lse_ref[h] * LOG2E, jnp.where(lane == 1, delta, 0.0))
            st_ref[h, 0] = cols.T[0:8, :]

    blk = pl.BlockSpec((tm, D), lambda i: (i, 0))
    return pl.pallas_call(
        body, name="mla_gate_bwd", grid=(S // tm,),
        in_specs=[blk, blk, pl.BlockSpec((tm, D), lambda i: (i, COL_ZB)), pl.BlockSpec((HEADS, tm, 1), lambda i: (0, i, 0))],
        out_specs=[blk, blk, pl.BlockSpec((HEADS, 1, 8, tm), lambda i: (0, i, 0, 0))],
        out_shape=[jax.ShapeDtypeStruct((S, D), BF16), jax.ShapeDtypeStruct((S, D), BF16),
                   jax.ShapeDtypeStruct((HEADS, S // tm, 8, tm), F32)],
        compiler_params=_params(("parallel",)),
    )(dy, o, proj, lse)


def _mla_bwd(q, k, v, do, stats, *, t, rider):
    S = q.shape[0]
    t = min(t, S)
    n = S // t
    c2 = (QK_DIM ** -0.5) * LOG2E

    n_rin, n_rout = len(rider.ins), len(rider.out_shapes)

    def body(*refs):
        q_ref, k_ref, v_ref, do_ref, st_ref = refs[:5]
        rin = refs[5:5 + n_rin]
        dq_ref, dk_ref, dv_ref = refs[5 + n_rin:8 + n_rin]
        rout = refs[8 + n_rin:8 + n_rin + n_rout]
        ki = pl.program_id(1)
        step = pl.program_id(0) * n + ki
        _ride(rider._replace(phases=rider.phases[:1]), rin, rout, refs[-2:], step, HEADS * n)

        @pl.when(ki == 0)
        def _():
            dq_ref[...] = jnp.zeros_like(dq_ref)

        kv, vv = k_ref[...], v_ref[...]

        def block(i, carry, diag):
            dk, dv = carry
            rows = pl.ds(pl.multiple_of(i * t, t), t)
            qv, dov, st = q_ref[rows, :], do_ref[rows, :], st_ref[0, i]
            s = _dot(kv, qv, NT)
            if diag:
                key = lax.broadcasted_iota(jnp.int32, s.shape, 0)
                qry = lax.broadcasted_iota(jnp.int32, s.shape, 1)
                s = jnp.where(key <= qry, s, -1e30)
            p = jnp.exp2(s * c2 - st[0:1, :])
            p16 = p.astype(BF16)
            dv = dv + _dot(p16, dov, NN)
            dp = _dot(vv, dov, NT)
            ds = (p * (dp - st[1:2, :])).astype(BF16)
            dk = dk + _dot(ds, qv, NN)
            dq_ref[rows, :] += _dot(ds, kv, TN)
            return dk, dv

        carry = block(ki, (jnp.zeros((t, QK_PAD), F32), jnp.zeros((t, V_DIM), F32)), True)
        rest = n - 1 - ki
        carry = lax.cond(rest % 2 == 1, lambda cr: block(ki + 1, cr, False), lambda cr: cr, carry)
        first = ki + 1 + rest % 2

        def pair(i, cr):
            return block(first + 2 * i + 1, block(first + 2 * i, cr, False), False)

        dk, dv = lax.fori_loop(0, rest // 2, pair, carry)
        dk_ref[...] = dk
        dv_ref[...] = dv
        _ride(rider._replace(phases=rider.phases[1:]), rin, rout, refs[-2:], step, HEADS * n)

    hbm = pl.BlockSpec(memory_space=pl.ANY)
    res = pl.pallas_call(
        body, name="mla_bwd", grid=(HEADS, n),
        in_specs=[pl.BlockSpec((S, QK_PAD), lambda h, j: (0, h)),
                  pl.BlockSpec((t, QK_PAD), lambda h, j: (j, h)),
                  pl.BlockSpec((t, V_DIM), lambda h, j: (j, h)),
                  pl.BlockSpec((S, V_DIM), lambda h, j: (0, h)),
                  pl.BlockSpec((1, n, 8, t), lambda h, j: (h, 0, 0, 0))] + [hbm] * n_rin,
        out_specs=[pl.BlockSpec((S, QK_PAD), lambda h, j: (0, h)),
                   pl.BlockSpec((t, QK_PAD), lambda h, j: (j, h)),
                   pl.BlockSpec((t, V_DIM), lambda h, j: (j, h))] + [hbm] * n_rout,
        out_shape=[jax.ShapeDtypeStruct((S, HEADS * QK_PAD), F32), jax.ShapeDtypeStruct((S, HEADS * QK_PAD), F32),
                   jax.ShapeDtypeStruct((S, HEADS * V_DIM), F32)] + list(rider.out_shapes),
        scratch_shapes=[pltpu.SemaphoreType.DMA((rider.n_sems,)), pltpu.SemaphoreType.DMA((rider.n_sems,))],
        compiler_params=_params(("arbitrary", "arbitrary")),
    )(q, k, v, do, stats, *rider.ins)
    return res[0], res[1], res[2], list(res[3:])


def _mla_qk_post(dq, dk, tabs, *, tm):
    S = dq.shape[0]
    tm = min(tm, S)
    scale = QK_DIM ** -0.5

    def body(dq_ref, dk_ref, c_ref, s1_ref, s2_ref, q16_ref, k16_ref, kr_ref):
        c, s1, s2 = c_ref[...] * scale, s1_ref[...] * scale, s2_ref[...] * scale
        kpe = jnp.zeros((tm, LANES), F32)
        for h in range(HEADS):
            lo = h * QK_PAD
            q16_ref[:, lo:lo + QK_NOPE] = (dq_ref[:, lo:lo + QK_NOPE] * scale).astype(BF16)
            q16_ref[:, lo + QK_NOPE:lo + QK_PAD] = _rot(dq_ref[:, lo + QK_NOPE:lo + QK_PAD], c, s1, s2, -1).astype(BF16)
            kpe = kpe + dk_ref[:, lo + QK_NOPE:lo + QK_PAD]
        k16_ref[...] = (dk_ref[...] * scale).astype(BF16)
        kr_ref[...] = _rot(kpe, c, s1, s2, -1).astype(BF16)

    wide = pl.BlockSpec((tm, HEADS * QK_PAD), lambda i: (i, 0))
    lane = pl.BlockSpec((tm, LANES), lambda i: (i, 0))
    return pl.pallas_call(
        body, name="mla_qk_post", grid=(S // tm,),
        in_specs=[wide, wide, lane, lane, lane], out_specs=[wide, wide, lane],
        out_shape=[jax.ShapeDtypeStruct((S, HEADS * QK_PAD), BF16), jax.ShapeDtypeStruct((S, HEADS * QK_PAD), BF16),
                   jax.ShapeDtypeStruct((S, LANES), BF16)],
        compiler_params=_params(("parallel",)),
    )(dq, dk, *tabs)


def _mem_scores(q16, km_ref, h):
    lo = h * MEM_HEAD_DIM
    s = _dot(q16, km_ref[:, lo:lo + MEM_HEAD_DIM], NT) * (MEM_HEAD_DIM ** -0.5)
    e = jnp.exp(s - jnp.max(s, axis=-1, keepdims=True))
    return e / jnp.sum(e, axis=-1, keepdims=True)


def _mem_fwd(proj, kvm, *, tm):
    S = proj.shape[0]
    tm = min(tm, S)
    M = kvm.shape[0]

    def body(q_ref, z_ref, km_ref, vm_ref, y_ref):
        sz, _ = _silu_parts(z_ref[...])
        for h in range(MEM_HEADS):
            lo = h * MEM_HEAD_DIM
            p = _mem_scores(q_ref[:, lo:lo + MEM_HEAD_DIM].astype(BF16), km_ref, h)
            o = _dot(p.astype(BF16), vm_ref[:, lo:lo + MEM_HEAD_DIM], NN)
            y_ref[:, lo:lo + MEM_HEAD_DIM] = (o * sz[:, lo:lo + MEM_HEAD_DIM]).astype(BF16)

    return pl.pallas_call(
        body, name="mem_fwd", grid=(S // tm,),
        in_specs=[pl.BlockSpec((tm, D), lambda i: (i, COL_QM)), pl.BlockSpec((tm, D), lambda i: (i, COL_ZM)),
                  pl.BlockSpec((M, D), lambda i: (0, 0)), pl.BlockSpec((M, D), lambda i: (0, 1))],
        out_specs=pl.BlockSpec((tm, D), lambda i: (i, 0)),
        out_shape=jax.ShapeDtypeStruct((S, D), BF16),
        compiler_params=_params(("parallel",)),
    )(proj, proj, kvm, kvm)


def _mem_bwd(proj, kvm, dy, *, tm):
    S = proj.shape[0]
    tm = min(tm, S)
    M = kvm.shape[0]
    scale = MEM_HEAD_DIM ** -0.5

    def body(q_ref, z_ref, km_ref, vm_ref, dy_ref, dq_ref, dz_ref, dkv_ref):
        @pl.when(pl.program_id(0) == 0)
        def _():
            dkv_ref[...] = jnp.zeros_like(dkv_ref)

        sz, dsz = _silu_parts(z_ref[...])
        dyv = dy_ref[...]
        for h in range(MEM_HEADS):
            lo = h * MEM_HEAD_DIM
            sl = slice(lo, lo + MEM_HEAD_DIM)
            q16 = q_ref[:, sl].astype(BF16)
            p = _mem_scores(q16, km_ref, h)
            p16 = p.astype(BF16)
            o = _dot(p16, vm_ref[:, sl], NN)
            dy_h = dyv[:, sl]
            dz_ref[:, sl] = (dy_h * o * dsz[:, sl]).astype(BF16)
            do16 = (dy_h * sz[:, sl]).astype(BF16)
            dp = _dot(do16, vm_ref[:, sl], NT)
            ds = (p * (dp - jnp.sum(dp * p, axis=-1, keepdims=True)) * scale).astype(BF16)
            dq_ref[:, sl] = _dot(ds, km_ref[:, sl], NN).astype(BF16)
            dkv_ref[:, sl] += _dot(ds, q16, TN)
            dkv_ref[:, D + lo:D + lo + MEM_HEAD_DIM] += _dot(p16, do16, TN)

    blk = pl.BlockSpec((tm, D), lambda i: (i, 0))
    return pl.pallas_call(
        body, name="mem_bwd", grid=(S // tm,),
        in_specs=[pl.BlockSpec((tm, D), lambda i: (i, COL_QM)), pl.BlockSpec((tm, D), lambda i: (i, COL_ZM)),
                  pl.BlockSpec((M, D), lambda i: (0, 0)), pl.BlockSpec((M, D), lambda i: (0, 1)), blk],
        out_specs=[blk, blk, pl.BlockSpec((M, 2 * D), lambda i: (0, 0))],
        out_shape=[jax.ShapeDtypeStruct((S, D), BF16), jax.ShapeDtypeStruct((S, D), BF16),
                   jax.ShapeDtypeStruct((M, 2 * D), F32)],
        compiler_params=_params(("arbitrary",)),
    )(proj, proj, kvm, kvm, dy)


def _gmlp_common(u_ref, v_ref, lng_ref, lnb_ref):
    u, du = _gelu_parts(u_ref[...])
    vg, dvg = _gelu_parts(v_ref[...])
    mu = jnp.mean(vg, axis=-1, keepdims=True)
    vc = vg - mu
    r = lax.rsqrt(jnp.mean(vc * vc, axis=-1, keepdims=True) + EPS)
    vhat = vc * r
    vn = vhat * lng_ref[...] + lnb_ref[...]
    return u, du, dvg, r, vhat, vn.astype(BF16)


def _gmlp_fwd(proj, ln_g, ln_b, wm, bs_t):
    S = proj.shape[0]

    def body(u_ref, v_ref, z_ref, lng_ref, lnb_ref, wm_ref, bs_ref, y_ref):
        u, _, _, _, _, v16 = _gmlp_common(u_ref, v_ref, lng_ref, lnb_ref)
        sz, _ = _silu_parts(z_ref[...])
        for g in range(A_GROUPS):
            sl = slice(g * CHUNK, (g + 1) * CHUNK)
            sv = _dot(wm_ref[g], v16[:, sl], NN) + bs_ref[:, g:g + 1]
            y_ref[:, sl] = (u[:, sl] * sv * sz[:, sl]).astype(BF16)

    def col(c):
        return pl.BlockSpec((CHUNK, D), lambda i: (i, c))

    vec = pl.BlockSpec((1, D), lambda i: (0, 0))
    return pl.pallas_call(
        body, name="gmlp_fwd", grid=(S // CHUNK,),
        in_specs=[col(COL_U), col(COL_V), col(COL_ZA), vec, vec,
                  pl.BlockSpec((A_GROUPS, CHUNK, CHUNK), lambda i: (0, 0, 0)), pl.BlockSpec((CHUNK, A_GROUPS), lambda i: (0, 0))],
        out_specs=col(0), out_shape=jax.ShapeDtypeStruct((S, D), BF16),
        compiler_params=_params(("parallel",)),
    )(proj, proj, proj, ln_g, ln_b, wm, bs_t)


def _gmlp_bwd(proj, dy, ln_g, ln_b, wm, bs_t):
    S = proj.shape[0]

    def body(u_ref, v_ref, z_ref, dy_ref, lng_ref, lnb_ref, wm_ref, bs_ref,
             du_ref, dv_ref, dz_ref, gws_ref, dsv_ref, glg_ref, glb_ref, dvn_s):
        @pl.when(pl.program_id(0) == 0)
        def _():
            gws_ref[...] = jnp.zeros_like(gws_ref)
            dsv_ref[...] = jnp.zeros_like(dsv_ref)
            glg_ref[...] = jnp.zeros_like(glg_ref)
            glb_ref[...] = jnp.zeros_like(glb_ref)

        u, du, dvg, r, vhat, v16 = _gmlp_common(u_ref, v_ref, lng_ref, lnb_ref)
        sz, dsz = _silu_parts(z_ref[...])
        dyv = dy_ref[...]
        for g in range(A_GROUPS):
            sl = slice(g * CHUNK, (g + 1) * CHUNK)
            sv = _dot(wm_ref[g], v16[:, sl], NN) + bs_ref[:, g:g + 1]
            dy_g, u_g, sz_g = dyv[:, sl], u[:, sl], sz[:, sl]
            dsv = dy_g * u_g * sz_g
            du_ref[:, sl] = (dy_g * sv * sz_g * du[:, sl]).astype(BF16)
            dz_ref[:, sl] = (dy_g * u_g * sv * dsz[:, sl]).astype(BF16)
            dsv16 = dsv.astype(BF16)
            dvn_s[:, sl] = _dot(wm_ref[g], dsv16, TN)
            gws_ref[g] += _dot(dsv16, v16[:, sl], NT)
            dsv_ref[:, sl] += dsv
        dvn = dvn_s[...]
        glb_ref[...] += jnp.sum(dvn, axis=0, keepdims=True)
        glg_ref[...] += jnp.sum(dvn * vhat, axis=0, keepdims=True)
        dvh = dvn * lng_ref[...]
        dvc = r * (dvh - jnp.mean(dvh, axis=-1, keepdims=True) - vhat * jnp.mean(dvh * vhat, axis=-1, keepdims=True))
        dv_ref[...] = (dvc * dvg).astype(BF16)

    def col(c):
        return pl.BlockSpec((CHUNK, D), lambda i: (i, c))

    vec = pl.BlockSpec((1, D), lambda i: (0, 0))
    wsp = pl.BlockSpec((A_GROUPS, CHUNK, CHUNK), lambda i: (0, 0, 0))
    return pl.pallas_call(
        body, name="gmlp_bwd", grid=(S // CHUNK,),
        in_specs=[col(COL_U), col(COL_V), col(COL_ZA), col(0), vec, vec, wsp, pl.BlockSpec((CHUNK, A_GROUPS), lambda i: (0, 0))],
        out_specs=[col(0), col(0), col(0), wsp, pl.BlockSpec((CHUNK, D), lambda i: (0, 0)), vec, vec],
        out_shape=[jax.ShapeDtypeStruct((S, D), BF16)] * 3 + [
            jax.ShapeDtypeStruct((A_GROUPS, CHUNK, CHUNK), F32), jax.ShapeDtypeStruct((CHUNK, D), F32),
            jax.ShapeDtypeStruct((1, D), F32), jax.ShapeDtypeStruct((1, D), F32)],
        scratch_shapes=[pltpu.VMEM((CHUNK, D), F32)],
        compiler_params=_params(("arbitrary",)),
    )(proj, proj, proj, dy, ln_g, ln_b, wm, bs_t)


def _gate_merge(h16, ys, wg, bg, wbs, *, tm, tn):
    S = h16.shape[0]
    tm = min(tm, S)
    nj = D // tn

    def body(h_ref, ya_ref, yb_ref, ym_ref, wg0, wg1, wg2, bg0, bg1, bg2, wb0, wb1, wb2,
             mg_ref, g0_ref, g1_ref, g2_ref, p0_ref, p1_ref, p2_ref):
        hv = h_ref[...]
        acc = None
        for y_ref, wg_ref, bgr, wb_ref, g_ref, p_ref in ((ya_ref, wg0, bg0, wb0, g0_ref, p0_ref),
                                                         (yb_ref, wg1, bg1, wb1, g1_ref, p1_ref),
                                                         (ym_ref, wg2, bg2, wb2, g2_ref, p2_ref)):
            gate = _sigmoid(_dot(hv, wg_ref[...], NN) + bgr[...])
            p = _dot(y_ref[...], wb_ref[...], NN)
            g_ref[...] = gate.astype(BF16)
            p_ref[...] = p.astype(BF16)
            acc = gate * p if acc is None else acc + gate * p
        mg_ref[...] = acc.astype(BF16)

    a_spec = pl.BlockSpec((tm, D), lambda j, i: (i, 0))
    o_spec = pl.BlockSpec((tm, tn), lambda j, i: (i, j))

    def wgs(n):
        return pl.BlockSpec((D, tn), lambda j, i: (0, n * nj + j))

    def bgs(n):
        return pl.BlockSpec((1, tn), lambda j, i: (0, n * nj + j))

    wbsp = pl.BlockSpec((D, tn), lambda j, i: (0, j))
    return pl.pallas_call(
        body, name="gate_merge", grid=(nj, S // tm),
        in_specs=[a_spec] * 4 + [wgs(0), wgs(1), wgs(2), bgs(0), bgs(1), bgs(2), wbsp, wbsp, wbsp],
        out_specs=[o_spec] * 7, out_shape=[jax.ShapeDtypeStruct((S, D), BF16)] * 7,
        compiler_params=_params(("parallel", "parallel")),
    )(h16, *ys, wg, wg, wg, bg, bg, bg, *wbs)


def _gate_bwd(dmerged, gates, ps, *, tm):
    S = dmerged.shape[0]
    tm = min(tm, S)

    def body(dm_ref, g0, g1, g2, p0, p1, p2, dp0, dp1, dp2, dg_ref, gb_ref):
        @pl.when(pl.program_id(0) == 0)
        def _():
            gb_ref[...] = jnp.zeros_like(gb_ref)

        dm = dm_ref[...]
        for n, (g_ref, p_ref, dp_ref) in enumerate(((g0, p0, dp0), (g1, p1, dp1), (g2, p2, dp2))):
            gate = g_ref[...].astype(F32)
            dp_ref[...] = (dm * gate).astype(BF16)
            dg = dm * p_ref[...].astype(F32) * gate * (1.0 - gate)
            dg_ref[:, n * D:(n + 1) * D] = dg.astype(BF16)
            gb_ref[:, n * D:(n + 1) * D] += jnp.sum(dg, axis=0, keepdims=True)

    blk = pl.BlockSpec((tm, D), lambda i: (i, 0))
    return pl.pallas_call(
        body, name="gate_bwd", grid=(S // tm,),
        in_specs=[blk] * 7,
        out_specs=[blk, blk, blk, pl.BlockSpec((tm, 3 * D), lambda i: (i, 0)), pl.BlockSpec((1, 3 * D), lambda i: (0, 0))],
        out_shape=[jax.ShapeDtypeStruct((S, D), BF16)] * 3 + [jax.ShapeDtypeStruct((S, 3 * D), BF16),
                                                              jax.ShapeDtypeStruct((1, 3 * D), F32)],
        compiler_params=_params(("arbitrary",)),
    )(dmerged, *gates, *ps)


def _post_loss(x, out, target, g_post, *, tm):
    S = x.shape[0]
    tm = min(tm, S)

    def body(x_ref, o_ref, t_ref, g_ref, dy_ref, do_ref, gg_ref, ls_ref):
        @pl.when(pl.program_id(0) == 0)
        def _():
            gg_ref[...] = jnp.zeros_like(gg_ref)
            ls_ref[...] = jnp.zeros_like(ls_ref)

        ov = o_ref[...]
        r = lax.rsqrt(jnp.mean(ov * ov, axis=-1, keepdims=True) + EPS)
        n = ov * r
        err = (x_ref[...] + n * g_ref[...]) - t_ref[...]
        ls_ref[...] += 0.5 * jnp.sum(jnp.mean(err * err, axis=-1, keepdims=True))
        dy = err * (1.0 / D)
        dy_ref[...] = dy
        gg_ref[...] += jnp.sum(dy * n, axis=0, keepdims=True)
        gd = dy * g_ref[...]
        do_ref[...] = (r * (gd - n * jnp.mean(gd * n, axis=-1, keepdims=True))).astype(BF16)

    blk = pl.BlockSpec((tm, D), lambda i: (i, 0))
    vec = pl.BlockSpec((1, D), lambda i: (0, 0))
    return pl.pallas_call(
        body, name="post_loss", grid=(S // tm,),
        in_specs=[blk, blk, blk, vec],
        out_specs=[blk, blk, vec, pl.BlockSpec((1, LANES), lambda i: (0, 0))],
        out_shape=[jax.ShapeDtypeStruct((S, D), F32), jax.ShapeDtypeStruct((S, D), BF16),
                   jax.ShapeDtypeStruct((1, D), F32), jax.ShapeDtypeStruct((1, LANES), F32)],
        compiler_params=_params(("arbitrary",)),
    )(x, out, target, g_post)


def _adamw(w, g, m, v, *, tr, name):
    rows, width = w.shape
    tr = min(tr, rows)
    assert rows % tr == 0
    c1 = 1.0 - ADAM_B1 ** ADAM_STEP
    c2 = 1.0 - ADAM_B2 ** ADAM_STEP

    def body(w_ref, g_ref, m_ref, v_ref, d_ref, nm_ref, nv_ref):
        gv = g_ref[...]
        nm = ADAM_B1 * m_ref[...] + (1.0 - ADAM_B1) * gv
        nv = ADAM_B2 * v_ref[...] + (1.0 - ADAM_B2) * (gv * gv)
        d_ref[...] = -ADAM_LR * ((nm / c1) / (jnp.sqrt(nv / c2) + ADAM_EPS) + ADAM_WD * w_ref[...])
        nm_ref[...] = nm
        nv_ref[...] = nv

    blk = pl.BlockSpec((tr, width), lambda i: (i, 0))
    return pl.pallas_call(
        body, name=name, grid=(rows // tr,), in_specs=[blk] * 4, out_specs=[blk] * 3,
        out_shape=[jax.ShapeDtypeStruct((rows, width), F32)] * 3,
        compiler_params=_params(("parallel",)),
    )(w, g, m, v)


MESH = pl.DeviceIdType.MESH
ANY = pl.BlockSpec(memory_space=pl.ANY)


def _place():
    return lax.axis_index("x"), lax.axis_index("y"), lax.axis_index("c")


def _other_chips(x, y):
    return [(1 - x, y), (x, 1 - y), (1 - x, 1 - y)]


def _remote(src, dst, send_sem, recv_sem, dev):
    return pltpu.make_async_remote_copy(src_ref=src, dst_ref=dst, send_sem=send_sem, recv_sem=recv_sem,
                                        device_id=dev, device_id_type=MESH)


def _allgather_chips(shards):
    nw = len(shards)

    def body(*refs):
        x_refs, out_refs = refs[:nw], refs[nw:2 * nw]
        send_sems, recv_sems = refs[2 * nw:]
        x, y, c = _place()
        sibling = (x, y, 1 - c)
        chips = _other_chips(x, y)

        def half(w, px, py, hc):
            hh = shards[w].shape[0] // 2
            return out_refs[w].at[2 * px + py, pl.ds(hc * hh, hh), :]

        sent = []
        for w in range(nw):
            hh = shards[w].shape[0] // 2
            for k, (px, py) in enumerate(chips):
                cp = _remote(x_refs[w].at[pl.ds(c * hh, hh), :], half(w, x, y, c), send_sems.at[6 * w + k],
                             recv_sems.at[6 * w + k], (px, py, c))
                cp.start()
                sent.append(cp)
        for w in range(nw):
            for k, (px, py) in enumerate(chips):
                landed = half(w, px, py, c)
                _remote(landed, landed, send_sems.at[6 * w + k], recv_sems.at[6 * w + k], (px, py, c)).wait_recv()
                cp = _remote(landed, landed, send_sems.at[6 * w + 3 + k], recv_sems.at[6 * w + 3 + k], sibling)
                cp.start()
                sent.append(cp)
        for w in range(nw):
            for k, (px, py) in enumerate(chips):
                other = half(w, px, py, 1 - c)
                _remote(other, other, send_sems.at[6 * w + 3 + k], recv_sems.at[6 * w + 3 + k], sibling).wait_recv()
        for cp in sent:
            cp.wait_send()

    outs = pl.pallas_call(
        body, name="allgather_weights", in_specs=[ANY] * nw, out_specs=[ANY] * nw,
        out_shape=[jax.ShapeDtypeStruct((N_CHIPS,) + s.shape, s.dtype) for s in shards],
        scratch_shapes=[pltpu.SemaphoreType.DMA((6 * nw,)), pltpu.SemaphoreType.DMA((6 * nw,))],
    )(*shards)
    own = 2 * lax.axis_index("x") + lax.axis_index("y")
    return [lax.dynamic_update_slice(o, s[None], (own, 0, 0)) for o, s in zip(outs, shards)]


def _row_tile(rows, cols, unit=16, budget=2 * 1024 * 1024):
    best = unit
    for t in range(unit, rows + 1, unit):
        if rows % t == 0 and t * cols * 4 <= budget:
            best = t
    assert rows % best == 0, (rows, cols)
    return best


def _peers(x, y, c):
    out = []
    for k in range(1, N_DEV):
        out.append((k, (1 - x if (k >> 2) & 1 else x, 1 - y if (k >> 1) & 1 else y, 1 - c if k & 1 else c)))
    return out


def _gather_rider(shards):
    nw = len(shards)

    def half(outs, w, px, py, hc):
        hh = shards[w].shape[0] // 2
        return outs[w].at[2 * px + py, pl.ds(hc * hh, hh), :]

    def ici(ins, outs, ss, rs, w, k, px, py, c, x, y):
        hh = shards[w].shape[0] // 2
        return _remote(ins[w].at[pl.ds(c * hh, hh), :], half(outs, w, x, y, c), ss.at[6 * w + k], rs.at[6 * w + k], (px, py, c))

    def passing(outs, ss, rs, w, k, px, py, hc, sibling):
        landed = half(outs, w, px, py, hc)
        return _remote(landed, landed, ss.at[6 * w + 3 + k], rs.at[6 * w + 3 + k], sibling)

    def start(ins, outs, ss, rs):
        x, y, c = _place()
        for w in range(nw):
            for k, (px, py) in enumerate(_other_chips(x, y)):
                ici(ins, outs, ss, rs, w, k, px, py, c, x, y).start()

    def forward(ins, outs, ss, rs):
        x, y, c = _place()
        for w in range(nw):
            for k, (px, py) in enumerate(_other_chips(x, y)):
                landed = half(outs, w, px, py, c)
                _remote(landed, landed, ss.at[6 * w + k], rs.at[6 * w + k], (px, py, c)).wait_recv()
                passing(outs, ss, rs, w, k, px, py, c, (x, y, 1 - c)).start()

    def finish(ins, outs, ss, rs):
        x, y, c = _place()
        for w in range(nw):
            for k, (px, py) in enumerate(_other_chips(x, y)):
                passing(outs, ss, rs, w, k, px, py, 1 - c, (x, y, 1 - c)).wait_recv()
        for w in range(nw):
            for k, (px, py) in enumerate(_other_chips(x, y)):
                ici(ins, outs, ss, rs, w, k, px, py, c, x, y).wait_send()
                passing(outs, ss, rs, w, k, px, py, c, (x, y, 1 - c)).wait_send()

    return _Rider(ins=tuple(shards), out_shapes=tuple(jax.ShapeDtypeStruct((N_CHIPS,) + s.shape, s.dtype) for s in shards),
                  n_sems=6 * nw, phases=((0.0, start), (0.8, forward), (1.0, finish)))


def _own_blocks_in_place(gathered, shards):
    own = 2 * lax.axis_index("x") + lax.axis_index("y")
    return [lax.dynamic_update_slice(o, s[None], (own, 0, 0)) for o, s in zip(gathered, shards)]


def _exchange_rider(blocks):
    nw = len(blocks)

    def copy(ins, outs, ss, rs, w, k, peer):
        hh = blocks[w].shape[1] // 2
        px, py, pc = peer
        return _remote(ins[w].at[2 * px + py, pl.ds(pc * hh, hh), :], outs[w].at[k - 1], ss.at[7 * w + k - 1], rs.at[7 * w + k - 1], peer)

    def start(ins, outs, ss, rs):
        for w in range(nw):
            for k, peer in _peers(*_place()):
                copy(ins, outs, ss, rs, w, k, peer).start()

    def finish(ins, outs, ss, rs):
        for w in range(nw):
            for k, peer in _peers(*_place()):
                copy(ins, outs, ss, rs, w, k, peer).wait()

    return _Rider(ins=tuple(blocks),
                  out_shapes=tuple(jax.ShapeDtypeStruct((N_DEV - 1, b.shape[1] // 2, b.shape[2]), b.dtype) for b in blocks),
                  n_sems=7 * nw, phases=((0.0, start), (1.0, finish)))


def _reduce_add(own, recv, cidx, *, name):
    R, W = own.shape
    hh = R // 2
    tr = _row_tile(hh, W, budget=1024 * 1024)
    nb = hh // tr

    def body(c_ref, o_ref, r_ref, t_ref):
        s = o_ref[...]
        for k in range(N_DEV - 1):
            s = s + r_ref[k].astype(F32)
        t_ref[...] = s

    return pl.pallas_call(
        body, name=name,
        grid_spec=pltpu.PrefetchScalarGridSpec(
            num_scalar_prefetch=1, grid=(nb,),
            in_specs=[pl.BlockSpec((tr, W), lambda i, c_ref: (i + c_ref[0] * nb, 0)),
                      pl.BlockSpec((N_DEV - 1, tr, W), lambda i, c_ref: (0, i, 0))],
            out_specs=pl.BlockSpec((tr, W), lambda i, c_ref: (i, 0))),
        out_shape=jax.ShapeDtypeStruct((hh, W), F32),
        compiler_params=_params(("parallel",)),
    )(cidx, own, recv)


def _halves_exchange(ts):
    nw = len(ts)

    def body(*refs):
        t_refs, out_refs = refs[:nw], refs[nw:2 * nw]
        send_sems, recv_sems = refs[2 * nw:]
        x, y, c = _place()
        cps = []
        for w in range(nw):
            hh = ts[w].shape[0]
            cp = _remote(t_refs[w], out_refs[w].at[pl.ds(c * hh, hh), :], send_sems.at[w], recv_sems.at[w], (x, y, 1 - c))
            cp.start()
            cps.append(cp)
        for w in range(nw):
            hh = ts[w].shape[0]
            _remote(t_refs[w], out_refs[w].at[pl.ds((1 - c) * hh, hh), :], send_sems.at[w], recv_sems.at[w], (x, y, 1 - c)).wait_recv()
        for cp in cps:
            cp.wait_send()

    outs = pl.pallas_call(
        body, name="grad_halves_exchange", in_specs=[ANY] * nw, out_specs=[ANY] * nw,
        out_shape=[jax.ShapeDtypeStruct((2 * t.shape[0], t.shape[1]), t.dtype) for t in ts],
        scratch_shapes=[pltpu.SemaphoreType.DMA((nw,)), pltpu.SemaphoreType.DMA((nw,))],
    )(*ts)
    c = lax.axis_index("c")
    return [lax.dynamic_update_slice(o, t, (c * t.shape[0], 0)) for o, t in zip(outs, ts)]


def _adam_math(w, g, m, v):
    nm = ADAM_B1 * m + (1.0 - ADAM_B1) * g
    nv = ADAM_B2 * v + (1.0 - ADAM_B2) * (g * g)
    c1 = 1.0 - ADAM_B1 ** ADAM_STEP
    c2 = 1.0 - ADAM_B2 ** ADAM_STEP
    return -ADAM_LR * ((nm / c1) / (jnp.sqrt(nv / c2) + ADAM_EPS) + ADAM_WD * w), nm, nv


STAGE_ROWS = 32
STAGE_VEC = {"g_pre": 0, "a_ln_g": 1, "a_ln_b": 2, "mem_norm_g": 3, "g_post": 4}
STAGE_BGATE = 5
STAGE_MIX = 8
STAGE_ABS = 16


def _small_step(g, loss_row, w, m, v):
    n = len(SMALL)

    def body(*refs):
        g_r = dict(zip(SMALL, refs[:n]))
        loss_r = refs[n]
        w_r = dict(zip(SMALL, refs[n + 1:2 * n + 1]))
        m_r = dict(zip(SMALL, refs[2 * n + 1:3 * n + 1]))
        v_r = dict(zip(SMALL, refs[3 * n + 1:4 * n + 1]))
        outs = refs[4 * n + 1:8 * n + 1]
        o_r = {name: outs[4 * i:4 * i + 4] for i, name in enumerate(SMALL)}
        loss_o = refs[8 * n + 1]
        stage, ga, gw, send_sems, recv_sems = refs[8 * n + 2:]

        stage[...] = jnp.zeros_like(stage)
        for name, row in STAGE_VEC.items():
            stage[row:row + 1, :] = g_r[name][...]
        for t in range(3):
            stage[STAGE_BGATE + t:STAGE_BGATE + t + 1, :] = g_r["b_gate"][:, t * D:(t + 1) * D]
        stage[STAGE_MIX:STAGE_MIX + 1, 0:LORA] = g_r["q_norm_g"][...]
        stage[STAGE_MIX:STAGE_MIX + 1, LORA:2 * LORA] = g_r["kv_norm_g"][...]
        stage[STAGE_MIX:STAGE_MIX + 1, 2 * LORA:2 * LORA + LANES] = loss_r[...]
        stage[STAGE_ABS:STAGE_ABS + A_GROUPS, 0:CHUNK] = g_r["a_b_s"][...]

        x, y, c = _place()
        me = 4 * x + 2 * y + c
        ga[me] = stage[...]
        gw[me] = g_r["a_w_s"][...]
        cps, srcs = [], []
        for k in range(1, N_DEV):
            fx, fy, fc = (k >> 2) & 1, (k >> 1) & 1, k & 1
            peer = (1 - x if fx else x, 1 - y if fy else y, 1 - c if fc else c)
            for j, (src, dst) in enumerate(((stage, ga), (g_r["a_w_s"], gw))):
                cp = _remote(src, dst.at[me], send_sems.at[2 * (k - 1) + j], recv_sems.at[2 * (k - 1) + j], peer)
                cp.start()
                cps.append(cp)
            srcs.append(4 * peer[0] + 2 * peer[1] + peer[2])
        for k, src in enumerate(srcs):
            _remote(stage, ga.at[src], send_sems.at[2 * k], recv_sems.at[2 * k], (x, y, c)).wait_recv()
            _remote(g_r["a_w_s"], gw.at[src], send_sems.at[2 * k + 1], recv_sems.at[2 * k + 1], (x, y, c)).wait_recv()
        for cp in cps:
            cp.wait_send()
        sa, sw = ga[0], gw[0]
        for d in range(1, N_DEV):
            sa = sa + ga[d]
            sw = sw + gw[d]

        def update(name, gsum, cols=None):
            sel = (slice(None), cols) if cols is not None else Ellipsis
            delta, nm, nv = _adam_math(w_r[name][sel], gsum, m_r[name][sel], v_r[name][sel])
            for ref, val in zip(o_r[name], (gsum, delta, nm, nv)):
                ref[sel] = val

        for name, row in STAGE_VEC.items():
            update(name, sa[row:row + 1, :])
        for t in range(3):
            update("b_gate", sa[STAGE_BGATE + t:STAGE_BGATE + t + 1, :], slice(t * D, (t + 1) * D))
        update("q_norm_g", sa[STAGE_MIX:STAGE_MIX + 1, 0:LORA])
        update("kv_norm_g", sa[STAGE_MIX:STAGE_MIX + 1, LORA:2 * LORA])
        update("a_b_s", sa[STAGE_ABS:STAGE_ABS + A_GROUPS, 0:CHUNK])
        update("a_w_s", sw)
        loss_o[...] = sa[STAGE_MIX:STAGE_MIX + 1, 2 * LORA:2 * LORA + LANES]

    vm = pl.BlockSpec(memory_space=pltpu.VMEM)
    ins = [g[k] for k in SMALL] + [loss_row] + [w[k] for k in SMALL] + [m[k] for k in SMALL] + [v[k] for k in SMALL]
    out_shape = [jax.ShapeDtypeStruct(w[k].shape, F32) for k in SMALL for _ in range(4)] + [jax.ShapeDtypeStruct((1, LANES), F32)]
    res = pl.pallas_call(
        body, name="small_allreduce_adamw", in_specs=[vm] * len(ins), out_specs=[vm] * len(out_shape), out_shape=out_shape,
        scratch_shapes=[pltpu.VMEM((STAGE_ROWS, D), F32), pltpu.VMEM((N_DEV, STAGE_ROWS, D), F32),
                        pltpu.VMEM((N_DEV, A_GROUPS, CHUNK, CHUNK), F32),
                        pltpu.SemaphoreType.DMA((2 * (N_DEV - 1),)), pltpu.SemaphoreType.DMA((2 * (N_DEV - 1),))],
        compiler_params=pltpu.CompilerParams(vmem_limit_bytes=VMEM_LIMIT),
    )(*ins)
    return {k: tuple(res[4 * i:4 * i + 4]) for i, k in enumerate(SMALL)}, res[-1]


SHARD_2D = {"w_in": (D, IN_REF // N_CHIPS), "w_uq": (LORA, HEADS * QK_DIM // N_CHIPS),
            "w_ukv": (LORA, HEADS * (QK_NOPE + V_DIM) // N_CHIPS), "w_mem_kv": (D, 2 * D // N_CHIPS),
            "w_gate": (D, 3 * D // N_CHIPS), "w_branch": (3 * D // N_CHIPS, D), "w_out": (D // N_CHIPS, D)}


def _cols(blocks):
    return jnp.concatenate([blocks[j] for j in range(N_CHIPS)], axis=1)


def _w_in_layout(gathered):
    w = _cols(gathered)
    return jnp.concatenate([w[:, :3 * D], w[:, 3 * D + 2 * LORA + QK_ROPE:], w[:, 3 * D:3 * D + 2 * LORA + QK_ROPE],
                            jnp.zeros((D, IN_PAD - IN_REF), w.dtype)], axis=1)


REST = BIG[1:]


def _rest_layouts(gathered):
    wq = jnp.pad(_cols(gathered["w_uq"]).reshape(LORA, HEADS, QK_DIM), ((0, 0), (0, 0), (0, QK_PAD - QK_DIM))).reshape(LORA, HEADS * QK_PAD)
    kv3 = _cols(gathered["w_ukv"]).reshape(LORA, HEADS, QK_NOPE + V_DIM)
    wk = jnp.pad(kv3[:, :, :QK_NOPE], ((0, 0), (0, 0), (0, QK_PAD - QK_NOPE))).reshape(LORA, HEADS * QK_PAD)
    wv = kv3[:, :, QK_NOPE:].reshape(LORA, HEADS * V_DIM)
    w_branch = gathered["w_branch"].reshape(N_CHIPS, 3, D // N_CHIPS, D).transpose(1, 0, 2, 3).reshape(3, D, D)
    return {"wq": wq, "wk": wk, "wv": wv, "w_mem_kv": _cols(gathered["w_mem_kv"]), "w_gate": _cols(gathered["w_gate"]),
            "w_branch": w_branch, "w_out": gathered["w_out"].reshape(D, D)}


def _grad_reference_layout(name, g):
    if name == "w_in":
        return jnp.concatenate([g[:, :3 * D], g[:, 6 * D:6 * D + 2 * LORA + QK_ROPE], g[:, 3 * D:6 * D]], axis=1)
    if name == "w_uq":
        return g.reshape(LORA, HEADS, QK_PAD)[:, :, :QK_DIM].reshape(LORA, HEADS * QK_DIM)
    if name == "w_ukv":
        gk, gv = g
        return jnp.concatenate([gk.reshape(LORA, HEADS, QK_PAD)[:, :, :QK_NOPE], gv.reshape(LORA, HEADS, V_DIM)],
                               axis=2).reshape(LORA, HEADS * (QK_NOPE + V_DIM))
    return g


def _grad_blocks(name, full):
    own = 2 * lax.axis_index("x") + lax.axis_index("y")
    R, C = SHARD_2D[name]
    if name == "w_branch":
        blocks = full.reshape(3, N_CHIPS, D // N_CHIPS, D).transpose(1, 0, 2, 3).reshape(N_CHIPS, R, C)
        mine = lax.dynamic_slice_in_dim(full, own * (D // N_CHIPS), D // N_CHIPS, axis=1).reshape(R, C)
    elif name == "w_out":
        blocks = full.reshape(N_CHIPS, R, C)
        mine = lax.dynamic_slice_in_dim(full, own * R, R, axis=0)
    else:
        blocks = full.reshape(R, N_CHIPS, C).transpose(1, 0, 2)
        mine = lax.dynamic_slice_in_dim(full, own * C, C, axis=1)
    return blocks.astype(BF16), mine


def _local_step(x, mem, pos_col, target, w_in, rest_shards, P):
    cidx = lax.axis_index("c").astype(jnp.int32).reshape(1)
    h16, rstd_x = _rms_fwd(x, P["g_pre"], width=D, col=0, tm=256, name="pre_norm")
    memn16, rstd_m = _rms_fwd(mem, P["mem_norm_g"], width=D, col=0, tm=256, name="mem_norm")
    proj, rest = _mm(h16, w_in, "nn", tm=512, tn=1920, tk=D, out_dtype=F32, name="in_proj", rider=_gather_rider(rest_shards))
    W = _rest_layouts(dict(zip(REST, _own_blocks_in_place(rest, rest_shards))))

    causal = jnp.tril(jnp.ones((CHUNK, CHUNK), F32))
    wm = (P["a_w_s"] * causal[None]).astype(BF16)
    bs_t = P["a_b_s"].T
    ya = _gmlp_fwd(proj, P["a_ln_g"], P["a_ln_b"], wm, bs_t)

    inv = 1.0 / (ROPE_THETA ** (jnp.arange(0, QK_ROPE, 2, dtype=F32) / QK_ROPE))
    inv_lane = jnp.concatenate([inv, inv, jnp.zeros((LANES - QK_ROPE,), F32)])[None]
    tabs = _rope_tables(pos_col, inv_lane, tm=1024)
    cqn, rstd_q = _rms_fwd(proj, P["q_norm_g"], width=LORA, col=COL_CQ, tm=512, name="q_norm")
    ckvn, rstd_kv = _rms_fwd(proj, P["kv_norm_g"], width=LORA, col=COL_CKV, tm=512, name="kv_norm")
    q16, k16, v16 = _mla_proj(cqn, ckvn, proj, tabs, W["wq"], W["wk"], W["wv"], tm=256)
    o_b, yb, lse = _mla_fwd(q16, k16, v16, proj, t=512)

    kvm = _mm(memn16, W["w_mem_kv"], "nn", tm=256, tn=1024, tk=D, out_dtype=BF16, name="mem_kv")
    ym = _mem_fwd(proj, kvm, tm=512)

    wbs = [W["w_branch"][n] for n in range(3)]
    merged, g0, g1, g2, p0, p1, p2 = _gate_merge(h16, (ya, yb, ym), W["w_gate"], P["b_gate"], wbs, tm=512, tn=512)
    out = _mm(merged, W["w_out"], "nn", tm=512, tn=1024, tk=D, out_dtype=F32, name="out_proj")
    dy, dout, g_g_post, loss = _post_loss(x, out, target, P["g_post"], tm=256)

    full = {}
    full["w_out"] = _mm(merged, dout, "tn", tm=1024, tn=1024, tk=512, out_dtype=F32, name="gw_out")
    dmerged = _mm(dout, W["w_out"], "nt", tm=512, tn=1024, tk=D, out_dtype=F32, name="d_merged")
    dp0, dp1, dp2, dgpre, g_b_gate = _gate_bwd(dmerged, (g0, g1, g2), (p0, p1, p2), tm=256)
    full["w_gate"] = _mm(h16, dgpre, "tn", tm=1024, tn=1024, tk=512, out_dtype=F32, name="gw_gate")
    dh_gate = _mm(dgpre, W["w_gate"], "nt", tm=512, tn=D, tk=D, out_dtype=F32, name="dh_gate")
    full["w_branch"] = jnp.stack([_mm(y, dp, "tn", tm=1024, tn=1024, tk=512, out_dtype=F32, name=f"gw_branch{n}")
                                  for n, (y, dp) in enumerate(((ya, dp0), (yb, dp1), (ym, dp2)))], axis=0)
    dya, dyb, dym = [_mm(dp, wbs[n], "nt", tm=512, tn=1024, tk=D, out_dtype=F32, name=f"dy_branch{n}")
                     for n, dp in enumerate((dp0, dp1, dp2))]

    dqm, dzm, dkvm = _mem_bwd(proj, kvm, dym, tm=512)
    dkvm16 = dkvm.astype(BF16)
    full["w_mem_kv"] = _mm(memn16, dkvm16, "tn", tm=1024, tn=1024, tk=256, out_dtype=F32, name="gw_mem_kv")
    dmemn = _mm(dkvm16, W["w_mem_kv"], "nt", tm=256, tn=1024, tk=2 * D, out_dtype=F32, name="d_memn")
    _, g_mem_norm = _rms_bwd(dmemn, mem, rstd_m, P["mem_norm_g"], width=D, col=0, tm=256, out_dtype=BF16, name="mem_norm_bwd")

    own, recv = {}, {}
    early = ("w_out", "w_gate", "w_branch", "w_mem_kv")
    early_blocks = []
    for n in early:
        blocks, own[n] = _grad_blocks(n, full[n])
        early_blocks.append(blocks)
    do16, dzb, stats = _mla_gate_bwd(dyb, o_b, proj, lse, tm=512)
    dq, dk, dv, landed = _mla_bwd(q16, k16, v16, do16, stats, t=512, rider=_exchange_rider(early_blocks))
    recv.update(zip(early, landed))
    dq16, dk16, dkr = _mla_qk_post(dq, dk, tabs, tm=256)
    dv16 = dv.astype(BF16)
    g_wq = _mm(cqn, dq16, "tn", tm=512, tn=1024, tk=512, out_dtype=F32, name="gw_uq")
    g_wk = _mm(ckvn, dk16, "tn", tm=512, tn=1024, tk=512, out_dtype=F32, name="gw_uk")
    g_wv = _mm(ckvn, dv16, "tn", tm=512, tn=1024, tk=512, out_dtype=F32, name="gw_uv")
    dcqn = _mm(dq16, W["wq"], "nt", tm=512, tn=LORA, tk=HEADS * QK_PAD, out_dtype=F32, name="d_cqn")
    dckvn_k = _mm(dk16, W["wk"], "nt", tm=512, tn=LORA, tk=HEADS * QK_PAD, out_dtype=F32, name="d_ckvn_k")
    dckvn = _mm(dv16, W["wv"], "nt", tm=512, tn=LORA, tk=HEADS * V_DIM, out_dtype=F32, name="d_ckvn", add=dckvn_k)
    dcq, g_q_norm = _rms_bwd(dcqn, proj, rstd_q, P["q_norm_g"], width=LORA, col=COL_CQ, tm=512, out_dtype=BF16, name="q_norm_bwd")
    dckv, g_kv_norm = _rms_bwd(dckvn, proj, rstd_kv, P["kv_norm_g"], width=LORA, col=COL_CKV, tm=512, out_dtype=BF16, name="kv_norm_bwd")

    du, dvr, dza, gws, dsv_sum, g_ln_g, g_ln_b = _gmlp_bwd(proj, dya, P["a_ln_g"], P["a_ln_b"], wm, bs_t)
    g_a_w_s = gws * causal[None]
    g_a_b_s = dsv_sum.reshape(CHUNK, A_GROUPS, CHUNK).sum(axis=-1).T

    mid = ("w_uq", "w_ukv")
    mid_blocks = []
    for n, g in (("w_uq", g_wq), ("w_ukv", (g_wk, g_wv))):
        blocks, own[n] = _grad_blocks(n, _grad_reference_layout(n, g))
        mid_blocks.append(blocks)
    dproj = jnp.concatenate([du, dvr, dza, dzb, dqm, dzm, dcq, dckv, dkr], axis=1)
    g_w_in, landed = _mm(h16, dproj, "tn", tm=1024, tn=1920, tk=512, out_dtype=F32, name="gw_in", rider=_exchange_rider(mid_blocks))
    recv.update(zip(mid, landed))
    in_blocks, own["w_in"] = _grad_blocks("w_in", _grad_reference_layout("w_in", g_w_in))
    dh, landed = _mm(dproj, w_in, "nt", tm=512, tn=D, tk=1920, out_dtype=F32, name="d_h", add=dh_gate, rider=_exchange_rider([in_blocks]))
    recv["w_in"] = landed[0]
    grad_x, g_g_pre = _rms_bwd(dh, x, rstd_x, P["g_pre"], width=D, col=0, tm=256, out_dtype=F32, name="pre_norm_bwd", residual=dy)

    totals = [_reduce_add(own[n], recv[n], cidx, name=f"grad_reduce_{n}") for n in BIG]
    small = {"g_pre": g_g_pre, "a_ln_g": g_ln_g, "a_ln_b": g_ln_b, "a_w_s": g_a_w_s, "a_b_s": g_a_b_s,
             "q_norm_g": g_q_norm, "kv_norm_g": g_kv_norm, "mem_norm_g": g_mem_norm, "b_gate": g_b_gate, "g_post": g_g_post}
    return loss, grad_x, totals, small


def kernel(x, mem, positions, g_pre, w_in, a_ln_g, a_ln_b, a_w_s, a_b_s, q_norm_g, w_uq, kv_norm_g, w_ukv, mem_norm_g, w_mem_kv, w_gate, b_gate, w_branch, w_out, g_post, loss_target, m_g_pre, m_w_in, m_a_ln_g, m_a_ln_b, m_a_w_s, m_a_b_s, m_q_norm_g, m_w_uq, m_kv_norm_g, m_w_ukv, m_mem_norm_g, m_w_mem_kv, m_w_gate, m_b_gate, m_w_branch, m_w_out, m_g_post, v_g_pre, v_w_in, v_a_ln_g, v_a_ln_b, v_a_w_s, v_a_b_s, v_q_norm_g, v_w_uq, v_kv_norm_g, v_w_ukv, v_mem_norm_g, v_w_mem_kv, v_w_gate, v_b_gate, v_w_branch, v_w_out, v_g_post):
    w = dict(g_pre=g_pre, w_in=w_in, a_ln_g=a_ln_g, a_ln_b=a_ln_b, a_w_s=a_w_s, a_b_s=a_b_s, q_norm_g=q_norm_g, w_uq=w_uq,
             kv_norm_g=kv_norm_g, w_ukv=w_ukv, mem_norm_g=mem_norm_g, w_mem_kv=w_mem_kv, w_gate=w_gate, b_gate=b_gate,
             w_branch=w_branch, w_out=w_out, g_post=g_post)
    m = dict(g_pre=m_g_pre, w_in=m_w_in, a_ln_g=m_a_ln_g, a_ln_b=m_a_ln_b, a_w_s=m_a_w_s, a_b_s=m_a_b_s, q_norm_g=m_q_norm_g,
             w_uq=m_w_uq, kv_norm_g=m_kv_norm_g, w_ukv=m_w_ukv, mem_norm_g=m_mem_norm_g, w_mem_kv=m_w_mem_kv, w_gate=m_w_gate,
             b_gate=m_b_gate, w_branch=m_w_branch, w_out=m_w_out, g_post=m_g_post)
    v = dict(g_pre=v_g_pre, w_in=v_w_in, a_ln_g=v_a_ln_g, a_ln_b=v_a_ln_b, a_w_s=v_a_w_s, a_b_s=v_a_b_s, q_norm_g=v_q_norm_g,
             w_uq=v_w_uq, kv_norm_g=v_kv_norm_g, w_ukv=v_w_ukv, mem_norm_g=v_mem_norm_g, w_mem_kv=v_w_mem_kv, w_gate=v_w_gate,
             b_gate=v_b_gate, w_branch=v_w_branch, w_out=v_w_out, g_post=v_g_post)

    def two_d(t, n):
        return t[n].reshape(SHARD_2D[n]) if n in SHARD_2D else t[n].reshape(t[n].shape[1:] if t[n].ndim > 2 else t[n].shape)

    shards = [two_d(w, n).astype(BF16) for n in BIG]
    w_in_full = _w_in_layout(_allgather_chips(shards[:1])[0])
    P = {n: two_d(w, n) for n in SMALL}

    S = x.shape[1]
    loss_row, grad_x, totals, small = _local_step(x[0], mem[0], positions.reshape(S, 1), loss_target[0], w_in_full, shards[1:], P)

    reduced = _halves_exchange(totals)
    res = {}
    for n, g_n in zip(BIG, reduced):
        upd = _adamw(two_d(w, n), g_n, two_d(m, n), two_d(v, n), tr=_row_tile(g_n.shape[0], g_n.shape[1], unit=8), name=f"adamw_{n}")
        for key, t in zip(("grad", "delta", "new_m", "new_v"), (g_n,) + tuple(upd)):
            res[key, n] = t.reshape(w[n].shape)

    small_out, loss_sum = _small_step(small, loss_row, P, {n: two_d(m, n) for n in SMALL}, {n: two_d(v, n) for n in SMALL})
    for n in SMALL:
        for key, t in zip(("grad", "delta", "new_m", "new_v"), small_out[n]):
            res[key, n] = t.reshape(w[n].shape)
    loss = loss_sum[0, 0]

    outs = [loss, grad_x[None]]
    for key in ("grad", "delta", "new_m", "new_v"):
        outs += [res[key, n] for n in WEIGHTS]
    return tuple(outs)
```

```python
import math
from typing import NamedTuple

import jax
import jax.numpy as jnp
from jax import lax
from jax.experimental import pallas as pl
from jax.experimental.pallas import tpu as pltpu

F32 = jnp.float32
BF16 = jnp.bfloat16

D = 2048
EPS = 1e-6
CHUNK = 128
A_GROUPS = 16
HEADS = 16
QK_NOPE = 128
QK_ROPE = 64
QK_DIM = QK_NOPE + QK_ROPE
V_DIM = 128
LORA = 512
MEM_HEADS = 4
MEM_HEAD_DIM = 512
ROPE_THETA = 10000.0
QK_PAD = 256
IN_REF = 13376
IN_PAD = 13440
COL_U, COL_V, COL_ZA, COL_ZB, COL_QM, COL_ZM = 0, 1, 2, 3, 4, 5
COL_CQ, COL_CKV = 24, 25
COL_KR = 104

ADAM_LR = 0.001
ADAM_B1 = 0.9
ADAM_B2 = 0.999
ADAM_EPS = 1e-08
ADAM_WD = 0.01
ADAM_STEP = 10

VMEM_LIMIT = 56 * 1024 * 1024
LANES = 128
LOG2E = math.log2(math.e)

BIG = ("w_in", "w_uq", "w_ukv", "w_mem_kv", "w_gate", "w_branch", "w_out")
SMALL = ("g_pre", "a_ln_g", "a_ln_b", "a_w_s", "a_b_s", "q_norm_g", "kv_norm_g", "mem_norm_g", "b_gate", "g_post")
WEIGHTS = ("g_pre", "w_in", "a_ln_g", "a_ln_b", "a_w_s", "a_b_s", "q_norm_g", "w_uq", "kv_norm_g", "w_ukv",
           "mem_norm_g", "w_mem_kv", "w_gate", "b_gate", "w_branch", "w_out", "g_post")
N_CHIPS = 4
N_DEV = 8


def _params(sem=None):
    return pltpu.CompilerParams(dimension_semantics=sem, vmem_limit_bytes=VMEM_LIMIT)


def _sigmoid(z):
    return 1.0 / (1.0 + jnp.exp(-z))


def _gelu_parts(x):
    c = math.sqrt(2.0 / math.pi)
    x2 = x * x
    t = jnp.tanh(c * (x + 0.044715 * x * x2))
    g = 0.5 * x * (1.0 + t)
    dg = 0.5 * (1.0 + t) + 0.5 * x * (1.0 - t * t) * (c * (1.0 + 3.0 * 0.044715 * x2))
    return g, dg


def _silu_parts(z):
    s = _sigmoid(z)
    return z * s, s * (1.0 + z * (1.0 - s))


def _dot(a, b, dims):
    return lax.dot_general(a, b, (dims, ((), ())), preferred_element_type=F32)


NN = ((1,), (0,))
NT = ((1,), (1,))
TN = ((0,), (0,))
TN_TK = 2048


class _Rider(NamedTuple):
    ins: tuple
    out_shapes: tuple
    n_sems: int
    phases: tuple


def _ride(rider, refs_in, refs_out, sems, step, total):
    for frac, fn in rider.phases:
        @pl.when(step == int(frac * (total - 1)))
        def _():
            fn(refs_in, refs_out, sems[0], sems[1])


def _mm(a, b, mode, *, tm, tn, tk, out_dtype, name, add=None, rider=None):
    if mode == "nn":
        (M, K), (_, N) = a.shape, b.shape
    elif mode == "nt":
        (M, K), (N, _) = a.shape, b.shape
    else:
        (K, M), (_, N) = a.shape, b.shape
    tm, tn, tk = min(tm, M), min(tn, N), min(tk, K)
    assert M % tm == 0 and N % tn == 0 and K % tk == 0, (name, M, N, K, tm, tn, tk)
    ni, nj, nk = M // tm, N // tn, K // tk
    dims = {"nn": NN, "nt": NT, "tn": TN}[mode]
    has_add = add is not None
    n_rin = len(rider.ins) if rider else 0
    n_rout = len(rider.out_shapes) if rider else 0

    def body(*refs):
        a_ref, b_ref = refs[0], refs[1]
        pos = 2
        add_ref = refs[pos] if has_add else None
        pos += int(has_add)
        rin = refs[pos:pos + n_rin]
        pos += n_rin
        o_ref = refs[pos]
        rout = refs[pos + 1:pos + 1 + n_rout]
        pos += 1 + n_rout
        acc = refs[pos] if nk > 1 else None
        sems = refs[-2:] if rider else None
        if rider:
            step = (pl.program_id(0) * ni + pl.program_id(1)) * nk + pl.program_id(2)
            _ride(rider._replace(phases=rider.phases[:1]), rin, rout, sems, step, nj * ni * nk)
        part = _dot(a_ref[...].astype(BF16), b_ref[...].astype(BF16), dims)

        def finish(r):
            if has_add:
                r = r + add_ref[...]
            o_ref[...] = r.astype(out_dtype)

        if nk == 1:
            finish(part)
        else:
            k = pl.program_id(2)

            @pl.when(k == 0)
            def _():
                acc[...] = part

            @pl.when(k > 0)
            def _():
                acc[...] += part

            @pl.when(k == nk - 1)
            def _():
                finish(acc[...])

        if rider:
            _ride(rider._replace(phases=rider.phases[1:]), rin, rout, sems, step, nj * ni * nk)

    if mode == "nn":
        a_spec = pl.BlockSpec((tm, tk), lambda j, i, k: (i, k))
        b_spec = pl.BlockSpec((tk, tn), lambda j, i, k: (k, j))
    elif mode == "nt":
        a_spec = pl.BlockSpec((tm, tk), lambda j, i, k: (i, k))
        b_spec = pl.BlockSpec((tn, tk), lambda j, i, k: (j, k))
    else:
        a_spec = pl.BlockSpec((tk, tm), lambda j, i, k: (k, i))
        b_spec = pl.BlockSpec((tk, tn), lambda j, i, k: (k, j))
    o_spec = pl.BlockSpec((tm, tn), lambda j, i, k: (i, j))
    hbm = pl.BlockSpec(memory_space=pl.ANY)
    in_specs = [a_spec, b_spec] + ([o_spec] if has_add else []) + [hbm] * n_rin
    args = (a, b) + ((add,) if has_add else ()) + (tuple(rider.ins) if rider else ())
    scratch = [pltpu.VMEM((tm, tn), F32)] if nk > 1 else []
    if rider:
        scratch += [pltpu.SemaphoreType.DMA((rider.n_sems,)), pltpu.SemaphoreType.DMA((rider.n_sems,))]
    res = pl.pallas_call(
        body, name=name, grid=(nj, ni, nk), in_specs=in_specs, out_specs=[o_spec] + [hbm] * n_rout,
        out_shape=[jax.ShapeDtypeStruct((M, N), out_dtype)] + (list(rider.out_shapes) if rider else []),
        scratch_shapes=scratch,
        compiler_params=_params(("arbitrary",) * 3 if rider else ("parallel", "parallel", "arbitrary")),
    )(*args)
    return (res[0], list(res[1:])) if rider else res[0]


def _rms_fwd(x, g, *, width, col, tm, name):
    rows = x.shape[0]
    tm = min(tm, rows)

    def body(x_ref, g_ref, y_ref, r_ref):
        xv = x_ref[...]
        r = lax.rsqrt(jnp.mean(xv * xv, axis=-1, keepdims=True) + EPS)
        y_ref[...] = ((xv * r) * g_ref[...]).astype(BF16)
        r_ref[...] = r

    return pl.pallas_call(
        body, name=name, grid=(rows // tm,),
        in_specs=[pl.BlockSpec((tm, width), lambda i: (i, col)), pl.BlockSpec((1, width), lambda i: (0, 0))],
        out_specs=[pl.BlockSpec((tm, width), lambda i: (i, 0)), pl.BlockSpec((tm, 1), lambda i: (i, 0))],
        out_shape=[jax.ShapeDtypeStruct((rows, width), BF16), jax.ShapeDtypeStruct((rows, 1), F32)],
        compiler_params=_params(("parallel",)),
    )(x, g)


def _rms_bwd(d, x, rstd, g, *, width, col, tm, out_dtype, name, residual=None):
    rows = d.shape[0]
    tm = min(tm, rows)
    has_res = residual is not None

    def body(*refs):
        d_ref, x_ref, r_ref, g_ref = refs[:4]
        res_ref = refs[4] if has_res else None
        dx_ref, gg_ref = refs[-2], refs[-1]
        dv = d_ref[...]
        n = x_ref[...] * r_ref[...]

        @pl.when(pl.program_id(0) == 0)
        def _():
            gg_ref[...] = jnp.zeros_like(gg_ref)

        gg_ref[...] += jnp.sum(dv * n, axis=0, keepdims=True)
        gd = dv * g_ref[...]
        dx = r_ref[...] * (gd - n * jnp.mean(gd * n, axis=-1, keepdims=True))
        if has_res:
            dx = dx + res_ref[...]
        dx_ref[...] = dx.astype(out_dtype)

    blk = pl.BlockSpec((tm, width), lambda i: (i, 0))
    in_specs = [blk, pl.BlockSpec((tm, width), lambda i: (i, col)), pl.BlockSpec((tm, 1), lambda i: (i, 0)),
                pl.BlockSpec((1, width), lambda i: (0, 0))] + ([blk] if has_res else [])
    args = (d, x, rstd, g) + ((residual,) if has_res else ())
    return pl.pallas_call(
        body, name=name, grid=(rows // tm,), in_specs=in_specs,
        out_specs=[blk, pl.BlockSpec((1, width), lambda i: (0, 0))],
        out_shape=[jax.ShapeDtypeStruct((rows, width), out_dtype), jax.ShapeDtypeStruct((1, width), F32)],
        compiler_params=_params(("arbitrary",)),
    )(*args)


def _rope_tables(pos_col, inv_lane, *, tm):
    rows = pos_col.shape[0]
    tm = min(tm, rows)

    def body(p_ref, f_ref, c_ref, s1_ref, s2_ref):
        ang = p_ref[...].astype(F32) * f_ref[...]
        lane = lax.broadcasted_iota(jnp.int32, ang.shape, 1)
        c, s = jnp.cos(ang), jnp.sin(ang)
        half = QK_ROPE // 2
        c_ref[...] = jnp.where(lane < QK_ROPE, c, 0.0)
        s1_ref[...] = jnp.where(lane < half, -s, 0.0)
        s2_ref[...] = jnp.where((lane >= half) & (lane < QK_ROPE), s, 0.0)

    blk = pl.BlockSpec((tm, LANES), lambda i: (i, 0))
    return pl.pallas_call(
        body, name="rope_tables", grid=(rows // tm,),
        in_specs=[pl.BlockSpec((tm, 1), lambda i: (i, 0)), pl.BlockSpec((1, LANES), lambda i: (0, 0))],
        out_specs=[blk, blk, blk], out_shape=[jax.ShapeDtypeStruct((rows, LANES), F32)] * 3,
        compiler_params=_params(("parallel",)),
    )(pos_col, inv_lane)


def _rot(t, c, s1, s2, sign):
    r1 = pltpu.roll(t, LANES - QK_ROPE // 2, 1) * s1
    r2 = pltpu.roll(t, QK_ROPE // 2, 1) * s2
    return t * c + (r1 + r2) if sign > 0 else t * c - (r1 + r2)


def _mla_proj(cqn, ckvn, proj, tabs, wq, wk, wv, *, tm):
    rows = cqn.shape[0]
    tm = min(tm, rows)

    def body(cq_ref, ckv_ref, kr_ref, c_ref, s1_ref, s2_ref, wq_ref, wk_ref, wv_ref, q_ref, k_ref, v_ref):
        c, s1, s2 = c_ref[...], s1_ref[...], s2_ref[...]
        q = _dot(cq_ref[...], wq_ref[...], NN)
        k = _dot(ckv_ref[...], wk_ref[...], NN)
        kpe = _rot(kr_ref[...], c, s1, s2, 1).astype(BF16)
        for h in range(HEADS):
            lo = h * QK_PAD
            q_ref[:, lo:lo + QK_NOPE] = q[:, lo:lo + QK_NOPE].astype(BF16)
            q_ref[:, lo + QK_NOPE:lo + QK_PAD] = _rot(q[:, lo + QK_NOPE:lo + QK_PAD], c, s1, s2, 1).astype(BF16)
            k_ref[:, lo:lo + QK_NOPE] = k[:, lo:lo + QK_NOPE].astype(BF16)
            k_ref[:, lo + QK_NOPE:lo + QK_PAD] = kpe
        v_ref[...] = _dot(ckv_ref[...], wv_ref[...], NN).astype(BF16)

    def row(w):
        return pl.BlockSpec((tm, w), lambda i: (i, 0))

    def whole(w):
        return pl.BlockSpec(w.shape, lambda i: (0, 0))

    return pl.pallas_call(
        body, name="mla_proj", grid=(rows // tm,),
        in_specs=[row(LORA), row(LORA), pl.BlockSpec((tm, LANES), lambda i: (i, COL_KR)), row(LANES), row(LANES), row(LANES),
                  whole(wq), whole(wk), whole(wv)],
        out_specs=[row(HEADS * QK_PAD), row(HEADS * QK_PAD), row(HEADS * V_DIM)],
        out_shape=[jax.ShapeDtypeStruct((rows, HEADS * QK_PAD), BF16), jax.ShapeDtypeStruct((rows, HEADS * QK_PAD), BF16),
                   jax.ShapeDtypeStruct((rows, HEADS * V_DIM), BF16)],
        compiler_params=_params(("parallel",)),
    )(cqn, ckvn, proj, *tabs, wq, wk, wv)


def _mla_fwd(q, k, v, proj, *, t):
    S = q.shape[0]
    t = min(t, S // 2)
    n = S // t
    assert S % (2 * t) == 0
    scale = QK_DIM ** -0.5

    def body(q_ref, k_ref, v_ref, z_ref, o_ref, y_ref, lse_ref):
        qi = pl.program_id(1)
        qv = q_ref[...]
        c2 = scale * LOG2E

        def block(k0, width, carry, row0):
            m_old, l_old, acc = carry
            ks = pl.ds(pl.multiple_of(k0, t), width)
            s = _dot(qv, k_ref[ks, :], NT)
            if row0 is not None:
                r = lax.broadcasted_iota(jnp.int32, s.shape, 0)
                c = lax.broadcasted_iota(jnp.int32, s.shape, 1)
                s = jnp.where(c <= r + row0, s, -1e30)
            m_new = jnp.maximum(m_old, jnp.max(s, axis=-1, keepdims=True))
            alpha = jnp.exp2((m_old - m_new) * c2)
            p = jnp.exp2((s - m_new) * c2)
            l_new = alpha * l_old + jnp.sum(p, axis=-1, keepdims=True)
            acc = alpha * acc + _dot(p.astype(BF16), v_ref[ks, :], NN)
            return m_new, l_new, acc

        init = (jnp.full((t, 1), -1e30, F32), jnp.zeros((t, 1), F32), jnp.zeros((t, V_DIM), F32))
        carry = lax.fori_loop(0, qi // 2, lambda j, cr: block(j * (2 * t), 2 * t, cr, None), init)
        m_f, l_f, acc = lax.cond(qi % 2 == 1,
                                 lambda cr: block((qi - 1) * t, 2 * t, cr, t),
                                 lambda cr: block(qi * t, t, cr, 0), carry)
        o = acc / l_f
        o_ref[...] = o
        sz, _ = _silu_parts(z_ref[...])
        y_ref[...] = (o * sz).astype(BF16)
        lse_ref[0] = m_f * scale + jnp.log(l_f)

    zcol = COL_ZB * (D // V_DIM)
    return pl.pallas_call(
        body, name="mla_fwd", grid=(HEADS, n),
        in_specs=[pl.BlockSpec((t, QK_PAD), lambda h, i: (i, h)),
                  pl.BlockSpec((S, QK_PAD), lambda h, i: (0, h)),
                  pl.BlockSpec((S, V_DIM), lambda h, i: (0, h)),
                  pl.BlockSpec((t, V_DIM), lambda h, i: (i, zcol + h))],
        out_specs=[pl.BlockSpec((t, V_DIM), lambda h, i: (i, h)), pl.BlockSpec((t, V_DIM), lambda h, i: (i, h)),
                   pl.BlockSpec((1, t, 1), lambda h, i: (h, i, 0))],
        out_shape=[jax.ShapeDtypeStruct((S, HEADS * V_DIM), F32), jax.ShapeDtypeStruct((S, HEADS * V_DIM), BF16),
                   jax.ShapeDtypeStruct((HEADS, S, 1), F32)],
        compiler_params=_params(("parallel", "parallel")),
    )(q, k, v, proj)


def _mla_gate_bwd(dy, o, proj, lse, *, tm):
    S = dy.shape[0]
    tm = min(tm, S)

    def body(dy_ref, o_ref, z_ref, lse_ref, do_ref, dz_ref, st_ref):
        sz, dsz = _silu_parts(z_ref[...])
        dyv, ov = dy_ref[...], o_ref[...]
        do = dyv * sz
        do_ref[...] = do.astype(BF16)
        dz_ref[...] = (dyv * ov * dsz).astype(BF16)
        prod = do * ov
        lane = lax.broadcasted_iota(jnp.int32, (tm, LANES), 1)
        for h in range(HEADS):
            delta = jnp.sum(prod[:, h * V_DIM:(h + 1) * V_DIM], axis=-1, keepdims=True)
            cols = jnp.where(lane == 0, lse_ref[h] * LOG2E, jnp.where(lane == 1, delta, 0.0))
            st_ref[h, 0] = cols.T[0:8, :]

    blk = pl.BlockSpec((tm, D), lambda i: (i, 0))
    return pl.pallas_call(
        body, name="mla_gate_bwd", grid=(S // tm,),
        in_specs=[blk, blk, pl.BlockSpec((tm, D), lambda i: (i, COL_ZB)), pl.BlockSpec((HEADS, tm, 1), lambda i: (0, i, 0))],
        out_specs=[blk, blk, pl.BlockSpec((HEADS, 1, 8, tm), lambda i: (0, i, 0, 0))],
        out_shape=[jax.ShapeDtypeStruct((S, D), BF16), jax.ShapeDtypeStruct((S, D), BF16),
                   jax.ShapeDtypeStruct((HEADS, S // tm, 8, tm), F32)],
        compiler_params=_params(("parallel",)),
    )(dy, o, proj, lse)


def _mla_bwd(q, k, v, do, stats, *, t, rider):
    S = q.shape[0]
    t = min(t, S)
    n = S // t
    c2 = (QK_DIM ** -0.5) * LOG2E

    n_rin, n_rout = len(rider.ins), len(rider.out_shapes)

    def body(*refs):
        q_ref, k_ref, v_ref, do_ref, st_ref = refs[:5]
        rin = refs[5:5 + n_rin]
        dq_ref, dk_ref, dv_ref = refs[5 + n_rin:8 + n_rin]
        rout = refs[8 + n_rin:8 + n_rin + n_rout]
        ki = pl.program_id(1)
        step = pl.program_id(0) * n + ki
        _ride(rider._replace(phases=rider.phases[:1]), rin, rout, refs[-2:], step, HEADS * n)

        @pl.when(ki == 0)
        def _():
            dq_ref[...] = jnp.zeros_like(dq_ref)

        kv, vv = k_ref[...], v_ref[...]

        def block(i, carry, diag):
            dk, dv = carry
            rows = pl.ds(pl.multiple_of(i * t, t), t)
            qv, dov, st = q_ref[rows, :], do_ref[rows, :], st_ref[0, i]
            s = _dot(kv, qv, NT)
            if diag:
                key = lax.broadcasted_iota(jnp.int32, s.shape, 0)
                qry = lax.broadcasted_iota(jnp.int32, s.shape, 1)
                s = jnp.where(key <= qry, s, -1e30)
            p = jnp.exp2(s * c2 - st[0:1, :])
            p16 = p.astype(BF16)
            dv = dv + _dot(p16, dov, NN)
            dp = _dot(vv, dov, NT)
            ds = (p * (dp - st[1:2, :])).astype(BF16)
            dk = dk + _dot(ds, qv, NN)
            dq_ref[rows, :] += _dot(ds, kv, TN)
            return dk, dv

        carry = block(ki, (jnp.zeros((t, QK_PAD), F32), jnp.zeros((t, V_DIM), F32)), True)
        rest = n - 1 - ki
        carry = lax.cond(rest % 2 == 1, lambda cr: block(ki + 1, cr, False), lambda cr: cr, carry)
        first = ki + 1 + rest % 2

        def pair(i, cr):
            return block(first + 2 * i + 1, block(first + 2 * i, cr, False), False)

        dk, dv = lax.fori_loop(0, rest // 2, pair, carry)
        dk_ref[...] = dk
        dv_ref[...] = dv.astype(BF16)
        _ride(rider._replace(phases=rider.phases[1:]), rin, rout, refs[-2:], step, HEADS * n)

    hbm = pl.BlockSpec(memory_space=pl.ANY)
    res = pl.pallas_call(
        body, name="mla_bwd", grid=(HEADS, n),
        in_specs=[pl.BlockSpec((S, QK_PAD), lambda h, j: (0, h)),
                  pl.BlockSpec((t, QK_PAD), lambda h, j: (j, h)),
                  pl.BlockSpec((t, V_DIM), lambda h, j: (j, h)),
                  pl.BlockSpec((S, V_DIM), lambda h, j: (0, h)),
                  pl.BlockSpec((1, n, 8, t), lambda h, j: (h, 0, 0, 0))] + [hbm] * n_rin,
        out_specs=[pl.BlockSpec((S, QK_PAD), lambda h, j: (0, h)),
                   pl.BlockSpec((t, QK_PAD), lambda h, j: (j, h)),
                   pl.BlockSpec((t, V_DIM), lambda h, j: (j, h))] + [hbm] * n_rout,
        out_shape=[jax.ShapeDtypeStruct((S, HEADS * QK_PAD), F32), jax.ShapeDtypeStruct((S, HEADS * QK_PAD), F32),
                   jax.ShapeDtypeStruct((S, HEADS * V_DIM), BF16)] + list(rider.out_shapes),
        scratch_shapes=[pltpu.SemaphoreType.DMA((rider.n_sems,)), pltpu.SemaphoreType.DMA((rider.n_sems,))],
        compiler_params=_params(("arbitrary", "arbitrary")),
    )(q, k, v, do, stats, *rider.ins)
    return res[0], res[1], res[2], list(res[3:])


def _mla_qk_post(dq, dk, tabs, *, tm):
    S = dq.shape[0]
    tm = min(tm, S)
    scale = QK_DIM ** -0.5

    def body(dq_ref, dk_ref, c_ref, s1_ref, s2_ref, q16_ref, k16_ref, kr_ref):
        c, s1, s2 = c_ref[...] * scale, s1_ref[...] * scale, s2_ref[...] * scale
        kpe = jnp.zeros((tm, LANES), F32)
        for h in range(HEADS):
            lo = h * QK_PAD
            q16_ref[:, lo:lo + QK_NOPE] = (dq_ref[:, lo:lo + QK_NOPE] * scale).astype(BF16)
            q16_ref[:, lo + QK_NOPE:lo + QK_PAD] = _rot(dq_ref[:, lo + QK_NOPE:lo + QK_PAD], c, s1, s2, -1).astype(BF16)
            kpe = kpe + dk_ref[:, lo + QK_NOPE:lo + QK_PAD]
        k16_ref[...] = (dk_ref[...] * scale).astype(BF16)
        kr_ref[...] = _rot(kpe, c, s1, s2, -1).astype(BF16)

    wide = pl.BlockSpec((tm, HEADS * QK_PAD), lambda i: (i, 0))
    lane = pl.BlockSpec((tm, LANES), lambda i: (i, 0))
    return pl.pallas_call(
        body, name="mla_qk_post", grid=(S // tm,),
        in_specs=[wide, wide, lane, lane, lane], out_specs=[wide, wide, lane],
        out_shape=[jax.ShapeDtypeStruct((S, HEADS * QK_PAD), BF16), jax.ShapeDtypeStruct((S, HEADS * QK_PAD), BF16),
                   jax.ShapeDtypeStruct((S, LANES), BF16)],
        compiler_params=_params(("parallel",)),
    )(dq, dk, *tabs)


def _mem_scores(q16, km_ref, h):
    lo = h * MEM_HEAD_DIM
    s = _dot(q16, km_ref[:, lo:lo + MEM_HEAD_DIM], NT) * (MEM_HEAD_DIM ** -0.5)
    e = jnp.exp(s - jnp.max(s, axis=-1, keepdims=True))
    return e / jnp.sum(e, axis=-1, keepdims=True)


def _mem_fwd(proj, kvm, *, tm):
    S = proj.shape[0]
    tm = min(tm, S)
    M = kvm.shape[0]

    def body(q_ref, z_ref, km_ref, vm_ref, y_ref):
        sz, _ = _silu_parts(z_ref[...])
        for h in range(MEM_HEADS):
            lo = h * MEM_HEAD_DIM
            p = _mem_scores(q_ref[:, lo:lo + MEM_HEAD_DIM].astype(BF16), km_ref, h)
            o = _dot(p.astype(BF16), vm_ref[:, lo:lo + MEM_HEAD_DIM], NN)
            y_ref[:, lo:lo + MEM_HEAD_DIM] = (o * sz[:, lo:lo + MEM_HEAD_DIM]).astype(BF16)

    return pl.pallas_call(
        body, name="mem_fwd", grid=(S // tm,),
        in_specs=[pl.BlockSpec((tm, D), lambda i: (i, COL_QM)), pl.BlockSpec((tm, D), lambda i: (i, COL_ZM)),
                  pl.BlockSpec((M, D), lambda i: (0, 0)), pl.BlockSpec((M, D), lambda i: (0, 1))],
        out_specs=pl.BlockSpec((tm, D), lambda i: (i, 0)),
        out_shape=jax.ShapeDtypeStruct((S, D), BF16),
        compiler_params=_params(("parallel",)),
    )(proj, proj, kvm, kvm)


def _mem_bwd(proj, kvm, dy, *, tm):
    S = proj.shape[0]
    tm = min(tm, S)
    M = kvm.shape[0]
    scale = MEM_HEAD_DIM ** -0.5

    def body(q_ref, z_ref, km_ref, vm_ref, dy_ref, dq_ref, dz_ref, dkv_ref):
        @pl.when(pl.program_id(0) == 0)
        def _():
            dkv_ref[...] = jnp.zeros_like(dkv_ref)

        sz, dsz = _silu_parts(z_ref[...])
        dyv = dy_ref[...]
        for h in range(MEM_HEADS):
            lo = h * MEM_HEAD_DIM
            sl = slice(lo, lo + MEM_HEAD_DIM)
            q16 = q_ref[:, sl].astype(BF16)
            p = _mem_scores(q16, km_ref, h)
            p16 = p.astype(BF16)
            o = _dot(p16, vm_ref[:, sl], NN)
            dy_h = dyv[:, sl]
            dz_ref[:, sl] = (dy_h * o * dsz[:, sl]).astype(BF16)
            do16 = (dy_h * sz[:, sl]).astype(BF16)
            dp = _dot(do16, vm_ref[:, sl], NT)
            ds = (p * (dp - jnp.sum(dp * p, axis=-1, keepdims=True)) * scale).astype(BF16)
            dq_ref[:, sl] = _dot(ds, km_ref[:, sl], NN).astype(BF16)
            dkv_ref[:, sl] += _dot(ds, q16, TN)
            dkv_ref[:, D + lo:D + lo + MEM_HEAD_DIM] += _dot(p16, do16, TN)

    blk = pl.BlockSpec((tm, D), lambda i: (i, 0))
    return pl.pallas_call(
        body, name="mem_bwd", grid=(S // tm,),
        in_specs=[pl.BlockSpec((tm, D), lambda i: (i, COL_QM)), pl.BlockSpec((tm, D), lambda i: (i, COL_ZM)),
                  pl.BlockSpec((M, D), lambda i: (0, 0)), pl.BlockSpec((M, D), lambda i: (0, 1)), blk],
        out_specs=[blk, blk, pl.BlockSpec((M, 2 * D), lambda i: (0, 0))],
        out_shape=[jax.ShapeDtypeStruct((S, D), BF16), jax.ShapeDtypeStruct((S, D), BF16),
                   jax.ShapeDtypeStruct((M, 2 * D), F32)],
        compiler_params=_params(("arbitrary",)),
    )(proj, proj, kvm, kvm, dy)


def _gmlp_common(u_ref, v_ref, lng_ref, lnb_ref):
    u, du = _gelu_parts(u_ref[...])
    vg, dvg = _gelu_parts(v_ref[...])
    mu = jnp.mean(vg, axis=-1, keepdims=True)
    vc = vg - mu
    r = lax.rsqrt(jnp.mean(vc * vc, axis=-1, keepdims=True) + EPS)
    vhat = vc * r
    vn = vhat * lng_ref[...] + lnb_ref[...]
    return u, du, dvg, r, vhat, vn.astype(BF16)


def _gmlp_fwd(proj, ln_g, ln_b, wm, bs_t):
    S = proj.shape[0]

    def body(u_ref, v_ref, z_ref, lng_ref, lnb_ref, wm_ref, bs_ref, y_ref):
        u, _, _, _, _, v16 = _gmlp_common(u_ref, v_ref, lng_ref, lnb_ref)
        sz, _ = _silu_parts(z_ref[...])
        for g in range(A_GROUPS):
            sl = slice(g * CHUNK, (g + 1) * CHUNK)
            sv = _dot(wm_ref[g], v16[:, sl], NN) + bs_ref[:, g:g + 1]
            y_ref[:, sl] = (u[:, sl] * sv * sz[:, sl]).astype(BF16)

    def col(c):
        return pl.BlockSpec((CHUNK, D), lambda i: (i, c))

    vec = pl.BlockSpec((1, D), lambda i: (0, 0))
    return pl.pallas_call(
        body, name="gmlp_fwd", grid=(S // CHUNK,),
        in_specs=[col(COL_U), col(COL_V), col(COL_ZA), vec, vec,
                  pl.BlockSpec((A_GROUPS, CHUNK, CHUNK), lambda i: (0, 0, 0)), pl.BlockSpec((CHUNK, A_GROUPS), lambda i: (0, 0))],
        out_specs=col(0), out_shape=jax.ShapeDtypeStruct((S, D), BF16),
        compiler_params=_params(("parallel",)),
    )(proj, proj, proj, ln_g, ln_b, wm, bs_t)


def _gmlp_bwd(proj, dy, ln_g, ln_b, wm, bs_t):
    S = proj.shape[0]

    def body(u_ref, v_ref, z_ref, dy_ref, lng_ref, lnb_ref, wm_ref, bs_ref,
             du_ref, dv_ref, dz_ref, gws_ref, dsv_ref, glg_ref, glb_ref, dvn_s):
        @pl.when(pl.program_id(0) == 0)
        def _():
            gws_ref[...] = jnp.zeros_like(gws_ref)
            dsv_ref[...] = jnp.zeros_like(dsv_ref)
            glg_ref[...] = jnp.zeros_like(glg_ref)
            glb_ref[...] = jnp.zeros_like(glb_ref)

        u, du, dvg, r, vhat, v16 = _gmlp_common(u_ref, v_ref, lng_ref, lnb_ref)
        sz, dsz = _silu_parts(z_ref[...])
        dyv = dy_ref[...]
        for g in range(A_GROUPS):
            sl = slice(g * CHUNK, (g + 1) * CHUNK)
            sv = _dot(wm_ref[g], v16[:, sl], NN) + bs_ref[:, g:g + 1]
            dy_g, u_g, sz_g = dyv[:, sl], u[:, sl], sz[:, sl]
            dsv = dy_g * u_g * sz_g
            du_ref[:, sl] = (dy_g * sv * sz_g * du[:, sl]).astype(BF16)
            dz_ref[:, sl] = (dy_g * u_g * sv * dsz[:, sl]).astype(BF16)
            dsv16 = dsv.astype(BF16)
            dvn_s[:, sl] = _dot(wm_ref[g], dsv16, TN)
            gws_ref[g] += _dot(dsv16, v16[:, sl], NT)
            dsv_ref[:, sl] += dsv
        dvn = dvn_s[...]
        glb_ref[...] += jnp.sum(dvn, axis=0, keepdims=True)
        glg_ref[...] += jnp.sum(dvn * vhat, axis=0, keepdims=True)
        dvh = dvn * lng_ref[...]
        dvc = r * (dvh - jnp.mean(dvh, axis=-1, keepdims=True) - vhat * jnp.mean(dvh * vhat, axis=-1, keepdims=True))
        dv_ref[...] = (dvc * dvg).astype(BF16)

    def col(c):
        return pl.BlockSpec((CHUNK, D), lambda i: (i, c))

    vec = pl.BlockSpec((1, D), lambda i: (0, 0))
    wsp = pl.BlockSpec((A_GROUPS, CHUNK, CHUNK), lambda i: (0, 0, 0))
    return pl.pallas_call(
        body, name="gmlp_bwd", grid=(S // CHUNK,),
        in_specs=[col(COL_U), col(COL_V), col(COL_ZA), col(0), vec, vec, wsp, pl.BlockSpec((CHUNK, A_GROUPS), lambda i: (0, 0))],
        out_specs=[col(0), col(0), col(0), wsp, pl.BlockSpec((CHUNK, D), lambda i: (0, 0)), vec, vec],
        out_shape=[jax.ShapeDtypeStruct((S, D), BF16)] * 3 + [
            jax.ShapeDtypeStruct((A_GROUPS, CHUNK, CHUNK), F32), jax.ShapeDtypeStruct((CHUNK, D), F32),
            jax.ShapeDtypeStruct((1, D), F32), jax.ShapeDtypeStruct((1, D), F32)],
        scratch_shapes=[pltpu.VMEM((CHUNK, D), F32)],
        compiler_params=_params(("arbitrary",)),
    )(proj, proj, proj, dy, ln_g, ln_b, wm, bs_t)


def _gate_merge(h16, ys, wg, bg, wbs, *, tm, tn):
    S = h16.shape[0]
    tm = min(tm, S)
    nj = D // tn

    def body(h_ref, ya_ref, yb_ref, ym_ref, wg0, wg1, wg2, bg0, bg1, bg2, wb0, wb1, wb2,
             mg_ref, g0_ref, g1_ref, g2_ref, p0_ref, p1_ref, p2_ref):
        hv = h_ref[...]
        acc = None
        for y_ref, wg_ref, bgr, wb_ref, g_ref, p_ref in ((ya_ref, wg0, bg0, wb0, g0_ref, p0_ref),
                                                         (yb_ref, wg1, bg1, wb1, g1_ref, p1_ref),
                                                         (ym_ref, wg2, bg2, wb2, g2_ref, p2_ref)):
            gate = _sigmoid(_dot(hv, wg_ref[...], NN) + bgr[...])
            p = _dot(y_ref[...], wb_ref[...], NN)
            g_ref[...] = gate.astype(BF16)
            p_ref[...] = p.astype(BF16)
            acc = gate * p if acc is None else acc + gate * p
        mg_ref[...] = acc.astype(BF16)

    a_spec = pl.BlockSpec((tm, D), lambda j, i: (i, 0))
    o_spec = pl.BlockSpec((tm, tn), lambda j, i: (i, j))

    def wgs(n):
        return pl.BlockSpec((D, tn), lambda j, i: (0, n * nj + j))

    def bgs(n):
        return pl.BlockSpec((1, tn), lambda j, i: (0, n * nj + j))

    wbsp = pl.BlockSpec((D, tn), lambda j, i: (0, j))
    return pl.pallas_call(
        body, name="gate_merge", grid=(nj, S // tm),
        in_specs=[a_spec] * 4 + [wgs(0), wgs(1), wgs(2), bgs(0), bgs(1), bgs(2), wbsp, wbsp, wbsp],
        out_specs=[o_spec] * 7, out_shape=[jax.ShapeDtypeStruct((S, D), BF16)] * 7,
        compiler_params=_params(("parallel", "parallel")),
    )(h16, *ys, wg, wg, wg, bg, bg, bg, *wbs)


def _gate_bwd(dmerged, gates, ps, *, tm):
    S = dmerged.shape[0]
    tm = min(tm, S)

    def body(dm_ref, g0, g1, g2, p0, p1, p2, dp0, dp1, dp2, dg_ref, gb_ref):
        @pl.when(pl.program_id(0) == 0)
        def _():
            gb_ref[...] = jnp.zeros_like(gb_ref)

        dm = dm_ref[...]
        for n, (g_ref, p_ref, dp_ref) in enumerate(((g0, p0, dp0), (g1, p1, dp1), (g2, p2, dp2))):
            gate = g_ref[...].astype(F32)
            dp_ref[...] = (dm * gate).astype(BF16)
            dg = dm * p_ref[...].astype(F32) * gate * (1.0 - gate)
            dg_ref[:, n * D:(n + 1) * D] = dg.astype(BF16)
            gb_ref[:, n * D:(n + 1) * D] += jnp.sum(dg, axis=0, keepdims=True)

    blk = pl.BlockSpec((tm, D), lambda i: (i, 0))
    return pl.pallas_call(
        body, name="gate_bwd", grid=(S // tm,),
        in_specs=[blk] * 7,
        out_specs=[blk, blk, blk, pl.BlockSpec((tm, 3 * D), lambda i: (i, 0)), pl.BlockSpec((1, 3 * D), lambda i: (0, 0))],
        out_shape=[jax.ShapeDtypeStruct((S, D), BF16)] * 3 + [jax.ShapeDtypeStruct((S, 3 * D), BF16),
                                                              jax.ShapeDtypeStruct((1, 3 * D), F32)],
        compiler_params=_params(("arbitrary",)),
    )(dmerged, *gates, *ps)


def _post_loss(x, out, target, g_post, *, tm):
    S = x.shape[0]
    tm = min(tm, S)

    def body(x_ref, o_ref, t_ref, g_ref, dy_ref, do_ref, gg_ref, ls_ref):
        @pl.when(pl.program_id(0) == 0)
        def _():
            gg_ref[...] = jnp.zeros_like(gg_ref)
            ls_ref[...] = jnp.zeros_like(ls_ref)

        ov = o_ref[...]
        r = lax.rsqrt(jnp.mean(ov * ov, axis=-1, keepdims=True) + EPS)
        n = ov * r
        err = (x_ref[...] + n * g_ref[...]) - t_ref[...]
        ls_ref[...] += 0.5 * jnp.sum(jnp.mean(err * err, axis=-1, keepdims=True))
        dy = err * (1.0 / D)
        dy_ref[...] = dy
        gg_ref[...] += jnp.sum(dy * n, axis=0, keepdims=True)
        gd = dy * g_ref[...]
        do_ref[...] = (r * (gd - n * jnp.mean(gd * n, axis=-1, keepdims=True))).astype(BF16)

    blk = pl.BlockSpec((tm, D), lambda i: (i, 0))
    vec = pl.BlockSpec((1, D), lambda i: (0, 0))
    return pl.pallas_call(
        body, name="post_loss", grid=(S // tm,),
        in_specs=[blk, blk, blk, vec],
        out_specs=[blk, blk, vec, pl.BlockSpec((1, LANES), lambda i: (0, 0))],
        out_shape=[jax.ShapeDtypeStruct((S, D), F32), jax.ShapeDtypeStruct((S, D), BF16),
                   jax.ShapeDtypeStruct((1, D), F32), jax.ShapeDtypeStruct((1, LANES), F32)],
        compiler_params=_params(("arbitrary",)),
    )(x, out, target, g_post)


def _adamw(w, g, m, v, *, tr, name):
    rows, width = w.shape
    tr = min(tr, rows)
    assert rows % tr == 0
    c1 = 1.0 - ADAM_B1 ** ADAM_STEP
    c2 = 1.0 - ADAM_B2 ** ADAM_STEP

    def body(w_ref, g_ref, m_ref, v_ref, d_ref, nm_ref, nv_ref):
        gv = g_ref[...]
        nm = ADAM_B1 * m_ref[...] + (1.0 - ADAM_B1) * gv
        nv = ADAM_B2 * v_ref[...] + (1.0 - ADAM_B2) * (gv * gv)
        d_ref[...] = -ADAM_LR * ((nm / c1) / (jnp.sqrt(nv / c2) + ADAM_EPS) + ADAM_WD * w_ref[...])
        nm_ref[...] = nm
        nv_ref[...] = nv

    blk = pl.BlockSpec((tr, width), lambda i: (i, 0))
    return pl.pallas_call(
        body, name=name, grid=(rows // tr,), in_specs=[blk] * 4, out_specs=[blk] * 3,
        out_shape=[jax.ShapeDtypeStruct((rows, width), F32)] * 3,
        compiler_params=_params(("parallel",)),
    )(w, g, m, v)


MESH = pl.DeviceIdType.MESH
ANY = pl.BlockSpec(memory_space=pl.ANY)


def _place():
    return lax.axis_index("x"), lax.axis_index("y"), lax.axis_index("c")


def _other_chips(x, y):
    return [(1 - x, y), (x, 1 - y), (1 - x, 1 - y)]


def _remote(src, dst, send_sem, recv_sem, dev):
    return pltpu.make_async_remote_copy(src_ref=src, dst_ref=dst, send_sem=send_sem, recv_sem=recv_sem,
                                        device_id=dev, device_id_type=MESH)


def _allgather_chips(shards):
    nw = len(shards)

    def body(*refs):
        x_refs, out_refs = refs[:nw], refs[nw:2 * nw]
        send_sems, recv_sems = refs[2 * nw:]
        x, y, c = _place()
        sibling = (x, y, 1 - c)
        chips = _other_chips(x, y)

        def half(w, px, py, hc):
            hh = shards[w].shape[0] // 2
            return out_refs[w].at[2 * px + py, pl.ds(hc * hh, hh), :]

        sent = []
        for w in range(nw):
            hh = shards[w].shape[0] // 2
            for k, (px, py) in enumerate(chips):
                cp = _remote(x_refs[w].at[pl.ds(c * hh, hh), :], half(w, x, y, c), send_sems.at[6 * w + k],
                             recv_sems.at[6 * w + k], (px, py, c))
                cp.start()
                sent.append(cp)
        for w in range(nw):
            for k, (px, py) in enumerate(chips):
                landed = half(w, px, py, c)
                _remote(landed, landed, send_sems.at[6 * w + k], recv_sems.at[6 * w + k], (px, py, c)).wait_recv()
                cp = _remote(landed, landed, send_sems.at[6 * w + 3 + k], recv_sems.at[6 * w + 3 + k], sibling)
                cp.start()
                sent.append(cp)
        for w in range(nw):
            for k, (px, py) in enumerate(chips):
                other = half(w, px, py, 1 - c)
                _remote(other, other, send_sems.at[6 * w + 3 + k], recv_sems.at[6 * w + 3 + k], sibling).wait_recv()
        for cp in sent:
            cp.wait_send()

    outs = pl.pallas_call(
        body, name="allgather_weights", in_specs=[ANY] * nw, out_specs=[ANY] * nw,
        out_shape=[jax.ShapeDtypeStruct((N_CHIPS,) + s.shape, s.dtype) for s in shards],
        scratch_shapes=[pltpu.SemaphoreType.DMA((6 * nw,)), pltpu.SemaphoreType.DMA((6 * nw,))],
    )(*shards)
    own = 2 * lax.axis_index("x") + lax.axis_index("y")
    return [lax.dynamic_update_slice(o, s[None], (own, 0, 0)) for o, s in zip(outs, shards)]


def _row_tile(rows, cols, unit=16, budget=2 * 1024 * 1024):
    best = unit
    for t in range(unit, rows + 1, unit):
        if rows % t == 0 and t * cols * 4 <= budget:
            best = t
    assert rows % best == 0, (rows, cols)
    return best


def _peers(x, y, c):
    out = []
    for k in range(1, N_DEV):
        out.append((k, (1 - x if (k >> 2) & 1 else x, 1 - y if (k >> 1) & 1 else y, 1 - c if k & 1 else c)))
    return out


def _gather_rider(shards):
    nw = len(shards)

    def half(outs, w, px, py, hc):
        hh = shards[w].shape[0] // 2
        return outs[w].at[2 * px + py, pl.ds(hc * hh, hh), :]

    def ici(ins, outs, ss, rs, w, k, px, py, c, x, y):
        hh = shards[w].shape[0] // 2
        return _remote(ins[w].at[pl.ds(c * hh, hh), :], half(outs, w, x, y, c), ss.at[6 * w + k], rs.at[6 * w + k], (px, py, c))

    def passing(outs, ss, rs, w, k, px, py, hc, sibling):
        landed = half(outs, w, px, py, hc)
        return _remote(landed, landed, ss.at[6 * w + 3 + k], rs.at[6 * w + 3 + k], sibling)

    def start(ins, outs, ss, rs):
        x, y, c = _place()
        for w in range(nw):
            for k, (px, py) in enumerate(_other_chips(x, y)):
                ici(ins, outs, ss, rs, w, k, px, py, c, x, y).start()

    def forward(ins, outs, ss, rs):
        x, y, c = _place()
        for w in range(nw):
            for k, (px, py) in enumerate(_other_chips(x, y)):
                landed = half(outs, w, px, py, c)
                _remote(landed, landed, ss.at[6 * w + k], rs.at[6 * w + k], (px, py, c)).wait_recv()
                passing(outs, ss, rs, w, k, px, py, c, (x, y, 1 - c)).start()

    def finish(ins, outs, ss, rs):
        x, y, c = _place()
        for w in range(nw):
            for k, (px, py) in enumerate(_other_chips(x, y)):
                passing(outs, ss, rs, w, k, px, py, 1 - c, (x, y, 1 - c)).wait_recv()
        for w in range(nw):
            for k, (px, py) in enumerate(_other_chips(x, y)):
                ici(ins, outs, ss, rs, w, k, px, py, c, x, y).wait_send()
                passing(outs, ss, rs, w, k, px, py, c, (x, y, 1 - c)).wait_send()

    return _Rider(ins=tuple(shards), out_shapes=tuple(jax.ShapeDtypeStruct((N_CHIPS,) + s.shape, s.dtype) for s in shards),
                  n_sems=6 * nw, phases=((0.0, start), (0.8, forward), (1.0, finish)))


def _own_blocks_in_place(gathered, shards):
    own = 2 * lax.axis_index("x") + lax.axis_index("y")
    return [lax.dynamic_update_slice(o, s[None], (own, 0, 0)) for o, s in zip(gathered, shards)]


def _exchange_rider(blocks):
    nw = len(blocks)

    def copy(ins, outs, ss, rs, w, k, peer):
        hh = blocks[w].shape[1] // 2
        px, py, pc = peer
        return _remote(ins[w].at[2 * px + py, pl.ds(pc * hh, hh), :], outs[w].at[k - 1], ss.at[7 * w + k - 1], rs.at[7 * w + k - 1], peer)

    def start(ins, outs, ss, rs):
        for w in range(nw):
            for k, peer in _peers(*_place()):
                copy(ins, outs, ss, rs, w, k, peer).start()

    def finish(ins, outs, ss, rs):
        for w in range(nw):
            for k, peer in _peers(*_place()):
                copy(ins, outs, ss, rs, w, k, peer).wait()

    return _Rider(ins=tuple(blocks),
                  out_shapes=tuple(jax.ShapeDtypeStruct((N_DEV - 1, b.shape[1] // 2, b.shape[2]), b.dtype) for b in blocks),
                  n_sems=7 * nw, phases=((0.0, start), (1.0, finish)))


def _reduce_add(own, recv, cidx, *, name):
    R, W = own.shape
    hh = R // 2
    tr = _row_tile(hh, W, budget=1024 * 1024)
    nb = hh // tr

    def body(c_ref, o_ref, r_ref, t_ref):
        s = o_ref[...]
        for k in range(N_DEV - 1):
            s = s + r_ref[k].astype(F32)
        t_ref[...] = s

    return pl.pallas_call(
        body, name=name,
        grid_spec=pltpu.PrefetchScalarGridSpec(
            num_scalar_prefetch=1, grid=(nb,),
            in_specs=[pl.BlockSpec((tr, W), lambda i, c_ref: (i + c_ref[0] * nb, 0)),
                      pl.BlockSpec((N_DEV - 1, tr, W), lambda i, c_ref: (0, i, 0))],
            out_specs=pl.BlockSpec((tr, W), lambda i, c_ref: (i, 0))),
        out_shape=jax.ShapeDtypeStruct((hh, W), F32),
        compiler_params=_params(("parallel",)),
    )(cidx, own, recv)


def _halves_exchange(ts):
    nw = len(ts)

    def body(*refs):
        t_refs, out_refs = refs[:nw], refs[nw:2 * nw]
        send_sems, recv_sems = refs[2 * nw:]
        x, y, c = _place()
        cps = []
        for w in range(nw):
            hh = ts[w].shape[0]
            cp = _remote(t_refs[w], out_refs[w].at[pl.ds(c * hh, hh), :], send_sems.at[w], recv_sems.at[w], (x, y, 1 - c))
            cp.start()
            cps.append(cp)
        for w in range(nw):
            hh = ts[w].shape[0]
            _remote(t_refs[w], out_refs[w].at[pl.ds((1 - c) * hh, hh), :], send_sems.at[w], recv_sems.at[w], (x, y, 1 - c)).wait_recv()
        for cp in cps:
            cp.wait_send()

    outs = pl.pallas_call(
        body, name="grad_halves_exchange", in_specs=[ANY] * nw, out_specs=[ANY] * nw,
        out_shape=[jax.ShapeDtypeStruct((2 * t.shape[0], t.shape[1]), t.dtype) for t in ts],
        scratch_shapes=[pltpu.SemaphoreType.DMA((nw,)), pltpu.SemaphoreType.DMA((nw,))],
    )(*ts)
    c = lax.axis_index("c")
    return [lax.dynamic_update_slice(o, t, (c * t.shape[0], 0)) for o, t in zip(outs, ts)]


def _adam_math(w, g, m, v):
    nm = ADAM_B1 * m + (1.0 - ADAM_B1) * g
    nv = ADAM_B2 * v + (1.0 - ADAM_B2) * (g * g)
    c1 = 1.0 - ADAM_B1 ** ADAM_STEP
    c2 = 1.0 - ADAM_B2 ** ADAM_STEP
    return -ADAM_LR * ((nm / c1) / (jnp.sqrt(nv / c2) + ADAM_EPS) + ADAM_WD * w), nm, nv


STAGE_ROWS = 32
STAGE_VEC = {"g_pre": 0, "a_ln_g": 1, "a_ln_b": 2, "mem_norm_g": 3, "g_post": 4}
STAGE_BGATE = 5
STAGE_MIX = 8
STAGE_ABS = 16


def _small_step(g, loss_row, w, m, v):
    n = len(SMALL)

    def body(*refs):
        g_r = dict(zip(SMALL, refs[:n]))
        loss_r = refs[n]
        w_r = dict(zip(SMALL, refs[n + 1:2 * n + 1]))
        m_r = dict(zip(SMALL, refs[2 * n + 1:3 * n + 1]))
        v_r = dict(zip(SMALL, refs[3 * n + 1:4 * n + 1]))
        outs = refs[4 * n + 1:8 * n + 1]
        o_r = {name: outs[4 * i:4 * i + 4] for i, name in enumerate(SMALL)}
        loss_o = refs[8 * n + 1]
        stage, ga, gw, send_sems, recv_sems = refs[8 * n + 2:]

        stage[...] = jnp.zeros_like(stage)
        for name, row in STAGE_VEC.items():
            stage[row:row + 1, :] = g_r[name][...]
        for t in range(3):
            stage[STAGE_BGATE + t:STAGE_BGATE + t + 1, :] = g_r["b_gate"][:, t * D:(t + 1) * D]
        stage[STAGE_MIX:STAGE_MIX + 1, 0:LORA] = g_r["q_norm_g"][...]
        stage[STAGE_MIX:STAGE_MIX + 1, LORA:2 * LORA] = g_r["kv_norm_g"][...]
        stage[STAGE_MIX:STAGE_MIX + 1, 2 * LORA:2 * LORA + LANES] = loss_r[...]
        stage[STAGE_ABS:STAGE_ABS + A_GROUPS, 0:CHUNK] = g_r["a_b_s"][...]

        x, y, c = _place()
        me = 4 * x + 2 * y + c
        ga[me] = stage[...]
        gw[me] = g_r["a_w_s"][...]
        cps, srcs = [], []
        for k in range(1, N_DEV):
            fx, fy, fc = (k >> 2) & 1, (k >> 1) & 1, k & 1
            peer = (1 - x if fx else x, 1 - y if fy else y, 1 - c if fc else c)
            for j, (src, dst) in enumerate(((stage, ga), (g_r["a_w_s"], gw))):
                cp = _remote(src, dst.at[me], send_sems.at[2 * (k - 1) + j], recv_sems.at[2 * (k - 1) + j], peer)
                cp.start()
                cps.append(cp)
            srcs.append(4 * peer[0] + 2 * peer[1] + peer[2])
        for k, src in enumerate(srcs):
            _remote(stage, ga.at[src], send_sems.at[2 * k], recv_sems.at[2 * k], (x, y, c)).wait_recv()
            _remote(g_r["a_w_s"], gw.at[src], send_sems.at[2 * k + 1], recv_sems.at[2 * k + 1], (x, y, c)).wait_recv()
        for cp in cps:
            cp.wait_send()
        sa, sw = ga[0], gw[0]
        for d in range(1, N_DEV):
            sa = sa + ga[d]
            sw = sw + gw[d]

        def update(name, gsum, cols=None):
            sel = (slice(None), cols) if cols is not None else Ellipsis
            delta, nm, nv = _adam_math(w_r[name][sel], gsum, m_r[name][sel], v_r[name][sel])
            for ref, val in zip(o_r[name], (gsum, delta, nm, nv)):
                ref[sel] = val

        for name, row in STAGE_VEC.items():
            update(name, sa[row:row + 1, :])
        for t in range(3):
            update("b_gate", sa[STAGE_BGATE + t:STAGE_BGATE + t + 1, :], slice(t * D, (t + 1) * D))
        update("q_norm_g", sa[STAGE_MIX:STAGE_MIX + 1, 0:LORA])
        update("kv_norm_g", sa[STAGE_MIX:STAGE_MIX + 1, LORA:2 * LORA])
        update("a_b_s", sa[STAGE_ABS:STAGE_ABS + A_GROUPS, 0:CHUNK])
        update("a_w_s", sw)
        loss_o[...] = sa[STAGE_MIX:STAGE_MIX + 1, 2 * LORA:2 * LORA + LANES]

    vm = pl.BlockSpec(memory_space=pltpu.VMEM)
    ins = [g[k] for k in SMALL] + [loss_row] + [w[k] for k in SMALL] + [m[k] for k in SMALL] + [v[k] for k in SMALL]
    out_shape = [jax.ShapeDtypeStruct(w[k].shape, F32) for k in SMALL for _ in range(4)] + [jax.ShapeDtypeStruct((1, LANES), F32)]
    res = pl.pallas_call(
        body, name="small_allreduce_adamw", in_specs=[vm] * len(ins), out_specs=[vm] * len(out_shape), out_shape=out_shape,
        scratch_shapes=[pltpu.VMEM((STAGE_ROWS, D), F32), pltpu.VMEM((N_DEV, STAGE_ROWS, D), F32),
                        pltpu.VMEM((N_DEV, A_GROUPS, CHUNK, CHUNK), F32),
                        pltpu.SemaphoreType.DMA((2 * (N_DEV - 1),)), pltpu.SemaphoreType.DMA((2 * (N_DEV - 1),))],
        compiler_params=pltpu.CompilerParams(vmem_limit_bytes=VMEM_LIMIT),
    )(*ins)
    return {k: tuple(res[4 * i:4 * i + 4]) for i, k in enumerate(SMALL)}, res[-1]


SHARD_2D = {"w_in": (D, IN_REF // N_CHIPS), "w_uq": (LORA, HEADS * QK_DIM // N_CHIPS),
            "w_ukv": (LORA, HEADS * (QK_NOPE + V_DIM) // N_CHIPS), "w_mem_kv": (D, 2 * D // N_CHIPS),
            "w_gate": (D, 3 * D // N_CHIPS), "w_branch": (3 * D // N_CHIPS, D), "w_out": (D // N_CHIPS, D)}


def _cols(blocks):
    return jnp.concatenate([blocks[j] for j in range(N_CHIPS)], axis=1)


def _w_in_layout(gathered):
    w = _cols(gathered)
    return jnp.concatenate([w[:, :3 * D], w[:, 3 * D + 2 * LORA + QK_ROPE:], w[:, 3 * D:3 * D + 2 * LORA + QK_ROPE],
                            jnp.zeros((D, IN_PAD - IN_REF), w.dtype)], axis=1)


REST = BIG[1:]


def _rest_layouts(gathered):
    wq = jnp.pad(_cols(gathered["w_uq"]).reshape(LORA, HEADS, QK_DIM), ((0, 0), (0, 0), (0, QK_PAD - QK_DIM))).reshape(LORA, HEADS * QK_PAD)
    kv3 = _cols(gathered["w_ukv"]).reshape(LORA, HEADS, QK_NOPE + V_DIM)
    wk = jnp.pad(kv3[:, :, :QK_NOPE], ((0, 0), (0, 0), (0, QK_PAD - QK_NOPE))).reshape(LORA, HEADS * QK_PAD)
    wv = kv3[:, :, QK_NOPE:].reshape(LORA, HEADS * V_DIM)
    w_branch = gathered["w_branch"].reshape(N_CHIPS, 3, D // N_CHIPS, D).transpose(1, 0, 2, 3).reshape(3, D, D)
    return {"wq": wq, "wk": wk, "wv": wv, "w_mem_kv": _cols(gathered["w_mem_kv"]), "w_gate": _cols(gathered["w_gate"]),
            "w_branch": w_branch, "w_out": gathered["w_out"].reshape(D, D)}


def _grad_reference_layout(name, g):
    if name == "w_in":
        return jnp.concatenate([g[:, :3 * D], g[:, 6 * D:6 * D + 2 * LORA + QK_ROPE], g[:, 3 * D:6 * D]], axis=1)
    if name == "w_uq":
        return g.reshape(LORA, HEADS, QK_PAD)[:, :, :QK_DIM].reshape(LORA, HEADS * QK_DIM)
    if name == "w_ukv":
        gk, gv = g
        return jnp.concatenate([gk.reshape(LORA, HEADS, QK_PAD)[:, :, :QK_NOPE], gv.reshape(LORA, HEADS, V_DIM)],
                               axis=2).reshape(LORA, HEADS * (QK_NOPE + V_DIM))
    return g


def _grad_blocks(name, full):
    own = 2 * lax.axis_index("x") + lax.axis_index("y")
    R, C = SHARD_2D[name]
    if name == "w_branch":
        blocks = full.reshape(3, N_CHIPS, D // N_CHIPS, D).transpose(1, 0, 2, 3).reshape(N_CHIPS, R, C)
        mine = lax.dynamic_slice_in_dim(full, own * (D // N_CHIPS), D // N_CHIPS, axis=1).reshape(R, C)
    elif name == "w_out":
        blocks = full.reshape(N_CHIPS, R, C)
        mine = lax.dynamic_slice_in_dim(full, own * R, R, axis=0)
    else:
        blocks = full.reshape(R, N_CHIPS, C).transpose(1, 0, 2)
        mine = lax.dynamic_slice_in_dim(full, own * C, C, axis=1)
    return blocks.astype(BF16), mine


def _local_step(x, mem, pos_col, target, w_in, rest_shards, P):
    cidx = lax.axis_index("c").astype(jnp.int32).reshape(1)
    h16, rstd_x = _rms_fwd(x, P["g_pre"], width=D, col=0, tm=256, name="pre_norm")
    memn16, rstd_m = _rms_fwd(mem, P["mem_norm_g"], width=D, col=0, tm=256, name="mem_norm")
    proj, rest = _mm(h16, w_in, "nn", tm=512, tn=1920, tk=D, out_dtype=F32, name="in_proj", rider=_gather_rider(rest_shards))
    W = _rest_layouts(dict(zip(REST, _own_blocks_in_place(rest, rest_shards))))

    causal = jnp.tril(jnp.ones((CHUNK, CHUNK), F32))
    wm = (P["a_w_s"] * causal[None]).astype(BF16)
    bs_t = P["a_b_s"].T
    ya = _gmlp_fwd(proj, P["a_ln_g"], P["a_ln_b"], wm, bs_t)

    inv = 1.0 / (ROPE_THETA ** (jnp.arange(0, QK_ROPE, 2, dtype=F32) / QK_ROPE))
    inv_lane = jnp.concatenate([inv, inv, jnp.zeros((LANES - QK_ROPE,), F32)])[None]
    tabs = _rope_tables(pos_col, inv_lane, tm=1024)
    cqn, rstd_q = _rms_fwd(proj, P["q_norm_g"], width=LORA, col=COL_CQ, tm=512, name="q_norm")
    ckvn, rstd_kv = _rms_fwd(proj, P["kv_norm_g"], width=LORA, col=COL_CKV, tm=512, name="kv_norm")
    q16, k16, v16 = _mla_proj(cqn, ckvn, proj, tabs, W["wq"], W["wk"], W["wv"], tm=256)
    o_b, yb, lse = _mla_fwd(q16, k16, v16, proj, t=512)

    kvm = _mm(memn16, W["w_mem_kv"], "nn", tm=256, tn=1024, tk=D, out_dtype=BF16, name="mem_kv")
    ym = _mem_fwd(proj, kvm, tm=512)

    wbs = [W["w_branch"][n] for n in range(3)]
    merged, g0, g1, g2, p0, p1, p2 = _gate_merge(h16, (ya, yb, ym), W["w_gate"], P["b_gate"], wbs, tm=512, tn=512)
    out = _mm(merged, W["w_out"], "nn", tm=512, tn=1024, tk=D, out_dtype=F32, name="out_proj")
    dy, dout, g_g_post, loss = _post_loss(x, out, target, P["g_post"], tm=256)

    full = {}
    full["w_out"] = _mm(merged, dout, "tn", tm=1024, tn=1024, tk=TN_TK, out_dtype=F32, name="gw_out")
    dmerged = _mm(dout, W["w_out"], "nt", tm=512, tn=1024, tk=D, out_dtype=F32, name="d_merged")
    dp0, dp1, dp2, dgpre, g_b_gate = _gate_bwd(dmerged, (g0, g1, g2), (p0, p1, p2), tm=256)
    full["w_gate"] = _mm(h16, dgpre, "tn", tm=1024, tn=1024, tk=TN_TK, out_dtype=F32, name="gw_gate")
    dh_gate = _mm(dgpre, W["w_gate"], "nt", tm=512, tn=D, tk=D, out_dtype=F32, name="dh_gate")
    full["w_branch"] = jnp.stack([_mm(y, dp, "tn", tm=1024, tn=1024, tk=TN_TK, out_dtype=F32, name=f"gw_branch{n}")
                                  for n, (y, dp) in enumerate(((ya, dp0), (yb, dp1), (ym, dp2)))], axis=0)
    dya, dyb, dym = [_mm(dp, wbs[n], "nt", tm=512, tn=1024, tk=D, out_dtype=F32, name=f"dy_branch{n}")
                     for n, dp in enumerate((dp0, dp1, dp2))]

    dqm, dzm, dkvm = _mem_bwd(proj, kvm, dym, tm=512)
    dkvm16 = dkvm.astype(BF16)
    full["w_mem_kv"] = _mm(memn16, dkvm16, "tn", tm=1024, tn=1024, tk=256, out_dtype=F32, name="gw_mem_kv")
    dmemn = _mm(dkvm16, W["w_mem_kv"], "nt", tm=256, tn=1024, tk=2 * D, out_dtype=F32, name="d_memn")
    _, g_mem_norm = _rms_bwd(dmemn, mem, rstd_m, P["mem_norm_g"], width=D, col=0, tm=256, out_dtype=BF16, name="mem_norm_bwd")

    own, recv = {}, {}
    early = ("w_out", "w_gate", "w_branch", "w_mem_kv")
    early_blocks = []
    for n in early:
        blocks, own[n] = _grad_blocks(n, full[n])
        early_blocks.append(blocks)
    do16, dzb, stats = _mla_gate_bwd(dyb, o_b, proj, lse, tm=512)
    dq, dk, dv16, landed = _mla_bwd(q16, k16, v16, do16, stats, t=512, rider=_exchange_rider(early_blocks))
    recv.update(zip(early, landed))
    dq16, dk16, dkr = _mla_qk_post(dq, dk, tabs, tm=256)
    g_wq = _mm(cqn, dq16, "tn", tm=512, tn=1024, tk=TN_TK, out_dtype=F32, name="gw_uq")
    g_wk = _mm(ckvn, dk16, "tn", tm=512, tn=1024, tk=TN_TK, out_dtype=F32, name="gw_uk")
    g_wv = _mm(ckvn, dv16, "tn", tm=512, tn=1024, tk=TN_TK, out_dtype=F32, name="gw_uv")
    dcqn = _mm(dq16, W["wq"], "nt", tm=512, tn=LORA, tk=HEADS * QK_PAD, out_dtype=F32, name="d_cqn")
    dckvn_k = _mm(dk16, W["wk"], "nt", tm=512, tn=LORA, tk=HEADS * QK_PAD, out_dtype=F32, name="d_ckvn_k")
    dckvn = _mm(dv16, W["wv"], "nt", tm=512, tn=LORA, tk=HEADS * V_DIM, out_dtype=F32, name="d_ckvn", add=dckvn_k)
    dcq, g_q_norm = _rms_bwd(dcqn, proj, rstd_q, P["q_norm_g"], width=LORA, col=COL_CQ, tm=512, out_dtype=BF16, name="q_norm_bwd")
    dckv, g_kv_norm = _rms_bwd(dckvn, proj, rstd_kv, P["kv_norm_g"], width=LORA, col=COL_CKV, tm=512, out_dtype=BF16, name="kv_norm_bwd")

    du, dvr, dza, gws, dsv_sum, g_ln_g, g_ln_b = _gmlp_bwd(proj, dya, P["a_ln_g"], P["a_ln_b"], wm, bs_t)
    g_a_w_s = gws * causal[None]
    g_a_b_s = dsv_sum.reshape(CHUNK, A_GROUPS, CHUNK).sum(axis=-1).T

    mid = ("w_uq", "w_ukv")
    mid_blocks = []
    for n, g in (("w_uq", g_wq), ("w_ukv", (g_wk, g_wv))):
        blocks, own[n] = _grad_blocks(n, _grad_reference_layout(n, g))
        mid_blocks.append(blocks)
    dproj = jnp.concatenate([du, dvr, dza, dzb, dqm, dzm, dcq, dckv, dkr], axis=1)
    g_w_in, landed = _mm(h16, dproj, "tn", tm=1024, tn=1920, tk=TN_TK // 2, out_dtype=F32, name="gw_in", rider=_exchange_rider(mid_blocks))
    recv.update(zip(mid, landed))
    in_blocks, own["w_in"] = _grad_blocks("w_in", _grad_reference_layout("w_in", g_w_in))
    dh, landed = _mm(dproj, w_in, "nt", tm=512, tn=D, tk=1920, out_dtype=F32, name="d_h", add=dh_gate, rider=_exchange_rider([in_blocks]))
    recv["w_in"] = landed[0]
    grad_x, g_g_pre = _rms_bwd(dh, x, rstd_x, P["g_pre"], width=D, col=0, tm=256, out_dtype=F32, name="pre_norm_bwd", residual=dy)

    totals = [_reduce_add(own[n], recv[n], cidx, name=f"grad_reduce_{n}") for n in BIG]
    small = {"g_pre": g_g_pre, "a_ln_g": g_ln_g, "a_ln_b": g_ln_b, "a_w_s": g_a_w_s, "a_b_s": g_a_b_s,
             "q_norm_g": g_q_norm, "kv_norm_g": g_kv_norm, "mem_norm_g": g_mem_norm, "b_gate": g_b_gate, "g_post": g_g_post}
    return loss, grad_x, totals, small


def kernel(x, mem, positions, g_pre, w_in, a_ln_g, a_ln_b, a_w_s, a_b_s, q_norm_g, w_uq, kv_norm_g, w_ukv, mem_norm_g, w_mem_kv, w_gate, b_gate, w_branch, w_out, g_post, loss_target, m_g_pre, m_w_in, m_a_ln_g, m_a_ln_b, m_a_w_s, m_a_b_s, m_q_norm_g, m_w_uq, m_kv_norm_g, m_w_ukv, m_mem_norm_g, m_w_mem_kv, m_w_gate, m_b_gate, m_w_branch, m_w_out, m_g_post, v_g_pre, v_w_in, v_a_ln_g, v_a_ln_b, v_a_w_s, v_a_b_s, v_q_norm_g, v_w_uq, v_kv_norm_g, v_w_ukv, v_mem_norm_g, v_w_mem_kv, v_w_gate, v_b_gate, v_w_branch, v_w_out, v_g_post):
    w = dict(g_pre=g_pre, w_in=w_in, a_ln_g=a_ln_g, a_ln_b=a_ln_b, a_w_s=a_w_s, a_b_s=a_b_s, q_norm_g=q_norm_g, w_uq=w_uq,
             kv_norm_g=kv_norm_g, w_ukv=w_ukv, mem_norm_g=mem_norm_g, w_mem_kv=w_mem_kv, w_gate=w_gate, b_gate=b_gate,
             w_branch=w_branch, w_out=w_out, g_post=g_post)
    m = dict(g_pre=m_g_pre, w_in=m_w_in, a_ln_g=m_a_ln_g, a_ln_b=m_a_ln_b, a_w_s=m_a_w_s, a_b_s=m_a_b_s, q_norm_g=m_q_norm_g,
             w_uq=m_w_uq, kv_norm_g=m_kv_norm_g, w_ukv=m_w_ukv, mem_norm_g=m_mem_norm_g, w_mem_kv=m_w_mem_kv, w_gate=m_w_gate,
             b_gate=m_b_gate, w_branch=m_w_branch, w_out=m_w_out, g_post=m_g_post)
    v = dict(g_pre=v_g_pre, w_in=v_w_in, a_ln_g=v_a_ln_g, a_ln_b=v_a_ln_b, a_w_s=v_a_w_s, a_b_s=v_a_b_s, q_norm_g=v_q_norm_g,
             w_uq=v_w_uq, kv_norm_g=v_kv_norm_g, w_ukv=v_w_ukv, mem_norm_g=v_mem_norm_g, w_mem_kv=v_w_mem_kv, w_gate=v_w_gate,
             b_gate=v_b_gate, w_branch=v_w_branch, w_out=v_w_out, g_post=v_g_post)

    def two_d(t, n):
        return t[n].reshape(SHARD_2D[n]) if n in SHARD_2D else t[n].reshape(t[n].shape[1:] if t[n].ndim > 2 else t[n].shape)

    shards = [two_d(w, n).astype(BF16) for n in BIG]
    w_in_full = _w_in_layout(_allgather_chips(shards[:1])[0])
    P = {n: two_d(w, n) for n in SMALL}

    S = x.shape[1]
    loss_row, grad_x, totals, small = _local_step(x[0], mem[0], positions.reshape(S, 1), loss_target[0], w_in_full, shards[1:], P)

    reduced = _halves_exchange(totals)
    res = {}
    for n, g_n in zip(BIG, reduced):
        upd = _adamw(two_d(w, n), g_n, two_d(m, n), two_d(v, n), tr=_row_tile(g_n.shape[0], g_n.shape[1], unit=8), name=f"adamw_{n}")
        for key, t in zip(("grad", "delta", "new_m", "new_v"), (g_n,) + tuple(upd)):
            res[key, n] = t.reshape(w[n].shape)

    small_out, loss_sum = _small_step(small, loss_row, P, {n: two_d(m, n) for n in SMALL}, {n: two_d(v, n) for n in SMALL})
    for n in SMALL:
        for key, t in zip(("grad", "delta", "new_m", "new_v"), small_out[n]):
            res[key, n] = t.reshape(w[n].shape)
    loss = loss_sum[0, 0]

    outs = [loss, grad_x[None]]
    for key in ("grad", "delta", "new_m", "new_v"):
        outs += [res[key, n] for n in WEIGHTS]
    return tuple(outs)
```

```python
import math
from typing import NamedTuple

import jax
import jax.numpy as jnp
from jax import lax
from jax.experimental import pallas as pl
from jax.experimental.pallas import tpu as pltpu

F32 = jnp.float32
BF16 = jnp.bfloat16

D = 2048
EPS = 1e-6
CHUNK = 128
A_GROUPS = 16
HEADS = 16
QK_NOPE = 128
QK_ROPE = 64
QK_DIM = QK_NOPE + QK_ROPE
V_DIM = 128
LORA = 512
MEM_HEADS = 4
MEM_HEAD_DIM = 512
ROPE_THETA = 10000.0
QK_PAD = 256
IN_REF = 13376
IN_PAD = 13440
COL_U, COL_V, COL_ZA, COL_ZB, COL_QM, COL_ZM = 0, 1, 2, 3, 4, 5
COL_CQ, COL_CKV = 24, 25
COL_KR = 104

ADAM_LR = 0.001
ADAM_B1 = 0.9
ADAM_B2 = 0.999
ADAM_EPS = 1e-08
ADAM_WD = 0.01
ADAM_STEP = 10

VMEM_LIMIT = 56 * 1024 * 1024
LANES = 128
LOG2E = math.log2(math.e)

BIG = ("w_in", "w_uq", "w_ukv", "w_mem_kv", "w_gate", "w_branch", "w_out")
SMALL = ("g_pre", "a_ln_g", "a_ln_b", "a_w_s", "a_b_s", "q_norm_g", "kv_norm_g", "mem_norm_g", "b_gate", "g_post")
WEIGHTS = ("g_pre", "w_in", "a_ln_g", "a_ln_b", "a_w_s", "a_b_s", "q_norm_g", "w_uq", "kv_norm_g", "w_ukv",
           "mem_norm_g", "w_mem_kv", "w_gate", "b_gate", "w_branch", "w_out", "g_post")
N_CHIPS = 4
N_DEV = 8


def _params(sem=None):
    return pltpu.CompilerParams(dimension_semantics=sem, vmem_limit_bytes=VMEM_LIMIT)


def _sigmoid(z):
    return 1.0 / (1.0 + jnp.exp(-z))


def _gelu_parts(x):
    c = math.sqrt(2.0 / math.pi)
    x2 = x * x
    t = jnp.tanh(c * (x + 0.044715 * x * x2))
    g = 0.5 * x * (1.0 + t)
    dg = 0.5 * (1.0 + t) + 0.5 * x * (1.0 - t * t) * (c * (1.0 + 3.0 * 0.044715 * x2))
    return g, dg


def _silu_parts(z):
    s = _sigmoid(z)
    return z * s, s * (1.0 + z * (1.0 - s))


def _dot(a, b, dims):
    return lax.dot_general(a, b, (dims, ((), ())), preferred_element_type=F32)


NN = ((1,), (0,))
NT = ((1,), (1,))
TN = ((0,), (0,))
TN_TK = 2048


class _Rider(NamedTuple):
    ins: tuple
    out_shapes: tuple
    n_sems: int
    phases: tuple


def _ride(rider, refs_in, refs_out, sems, step, total):
    for frac, fn in rider.phases:
        @pl.when(step == int(frac * (total - 1)))
        def _():
            fn(refs_in, refs_out, sems[0], sems[1])


def _mm(a, b, mode, *, tm, tn, tk, out_dtype, name, add=None, rider=None):
    if mode == "nn":
        (M, K), (_, N) = a.shape, b.shape
    elif mode == "nt":
        (M, K), (N, _) = a.shape, b.shape
    else:
        (K, M), (_, N) = a.shape, b.shape
    tm, tn, tk = min(tm, M), min(tn, N), min(tk, K)
    assert M % tm == 0 and N % tn == 0 and K % tk == 0, (name, M, N, K, tm, tn, tk)
    ni, nj, nk = M // tm, N // tn, K // tk
    dims = {"nn": NN, "nt": NT, "tn": TN}[mode]
    has_add = add is not None
    n_rin = len(rider.ins) if rider else 0
    n_rout = len(rider.out_shapes) if rider else 0

    def body(*refs):
        a_ref, b_ref = refs[0], refs[1]
        pos = 2
        add_ref = refs[pos] if has_add else None
        pos += int(has_add)
        rin = refs[pos:pos + n_rin]
        pos += n_rin
        o_ref = refs[pos]
        rout = refs[pos + 1:pos + 1 + n_rout]
        pos += 1 + n_rout
        acc = refs[pos] if nk > 1 else None
        sems = refs[-2:] if rider else None
        if rider:
            step = (pl.program_id(0) * ni + pl.program_id(1)) * nk + pl.program_id(2)
            _ride(rider._replace(phases=rider.phases[:1]), rin, rout, sems, step, nj * ni * nk)
        part = _dot(a_ref[...].astype(BF16), b_ref[...].astype(BF16), dims)

        def finish(r):
            if has_add:
                r = r + add_ref[...]
            o_ref[...] = r.astype(out_dtype)

        if nk == 1:
            finish(part)
        else:
            k = pl.program_id(2)

            @pl.when(k == 0)
            def _():
                acc[...] = part

            @pl.when(k > 0)
            def _():
                acc[...] += part

            @pl.when(k == nk - 1)
            def _():
                finish(acc[...])

        if rider:
            _ride(rider._replace(phases=rider.phases[1:]), rin, rout, sems, step, nj * ni * nk)

    if mode == "nn":
        a_spec = pl.BlockSpec((tm, tk), lambda j, i, k: (i, k))
        b_spec = pl.BlockSpec((tk, tn), lambda j, i, k: (k, j))
    elif mode == "nt":
        a_spec = pl.BlockSpec((tm, tk), lambda j, i, k: (i, k))
        b_spec = pl.BlockSpec((tn, tk), lambda j, i, k: (j, k))
    else:
        a_spec = pl.BlockSpec((tk, tm), lambda j, i, k: (k, i))
        b_spec = pl.BlockSpec((tk, tn), lambda j, i, k: (k, j))
    o_spec = pl.BlockSpec((tm, tn), lambda j, i, k: (i, j))
    hbm = pl.BlockSpec(memory_space=pl.ANY)
    in_specs = [a_spec, b_spec] + ([o_spec] if has_add else []) + [hbm] * n_rin
    args = (a, b) + ((add,) if has_add else ()) + (tuple(rider.ins) if rider else ())
    scratch = [pltpu.VMEM((tm, tn), F32)] if nk > 1 else []
    if rider:
        scratch += [pltpu.SemaphoreType.DMA((rider.n_sems,)), pltpu.SemaphoreType.DMA((rider.n_sems,))]
    res = pl.pallas_call(
        body, name=name, grid=(nj, ni, nk), in_specs=in_specs, out_specs=[o_spec] + [hbm] * n_rout,
        out_shape=[jax.ShapeDtypeStruct((M, N), out_dtype)] + (list(rider.out_shapes) if rider else []),
        scratch_shapes=scratch,
        compiler_params=_params(("arbitrary",) * 3 if rider else ("parallel", "parallel", "arbitrary")),
    )(*args)
    return (res[0], list(res[1:])) if rider else res[0]


def _rms_fwd(x, g, *, width, col, tm, name):
    rows = x.shape[0]
    tm = min(tm, rows)

    def body(x_ref, g_ref, y_ref, r_ref):
        xv = x_ref[...].astype(F32)
        r = lax.rsqrt(jnp.mean(xv * xv, axis=-1, keepdims=True) + EPS)
        y_ref[...] = ((xv * r) * g_ref[...]).astype(BF16)
        r_ref[...] = r

    return pl.pallas_call(
        body, name=name, grid=(rows // tm,),
        in_specs=[pl.BlockSpec((tm, width), lambda i: (i, col)), pl.BlockSpec((1, width), lambda i: (0, 0))],
        out_specs=[pl.BlockSpec((tm, width), lambda i: (i, 0)), pl.BlockSpec((tm, 1), lambda i: (i, 0))],
        out_shape=[jax.ShapeDtypeStruct((rows, width), BF16), jax.ShapeDtypeStruct((rows, 1), F32)],
        compiler_params=_params(("parallel",)),
    )(x, g)


def _rms_bwd(d, x, rstd, g, *, width, col, tm, out_dtype, name, residual=None):
    rows = d.shape[0]
    tm = min(tm, rows)
    has_res = residual is not None

    def body(*refs):
        d_ref, x_ref, r_ref, g_ref = refs[:4]
        res_ref = refs[4] if has_res else None
        dx_ref, gg_ref = refs[-2], refs[-1]
        dv = d_ref[...]
        n = x_ref[...].astype(F32) * r_ref[...]

        @pl.when(pl.program_id(0) == 0)
        def _():
            gg_ref[...] = jnp.zeros_like(gg_ref)

        gg_ref[...] += jnp.sum(dv * n, axis=0, keepdims=True)
        gd = dv * g_ref[...]
        dx = r_ref[...] * (gd - n * jnp.mean(gd * n, axis=-1, keepdims=True))
        if has_res:
            dx = dx + res_ref[...]
        dx_ref[...] = dx.astype(out_dtype)

    blk = pl.BlockSpec((tm, width), lambda i: (i, 0))
    in_specs = [blk, pl.BlockSpec((tm, width), lambda i: (i, col)), pl.BlockSpec((tm, 1), lambda i: (i, 0)),
                pl.BlockSpec((1, width), lambda i: (0, 0))] + ([blk] if has_res else [])
    args = (d, x, rstd, g) + ((residual,) if has_res else ())
    return pl.pallas_call(
        body, name=name, grid=(rows // tm,), in_specs=in_specs,
        out_specs=[blk, pl.BlockSpec((1, width), lambda i: (0, 0))],
        out_shape=[jax.ShapeDtypeStruct((rows, width), out_dtype), jax.ShapeDtypeStruct((1, width), F32)],
        compiler_params=_params(("arbitrary",)),
    )(*args)


def _rope_tables(pos_col, inv_lane, *, tm):
    rows = pos_col.shape[0]
    tm = min(tm, rows)

    def body(p_ref, f_ref, c_ref, s1_ref, s2_ref):
        ang = p_ref[...].astype(F32) * f_ref[...]
        lane = lax.broadcasted_iota(jnp.int32, ang.shape, 1)
        c, s = jnp.cos(ang), jnp.sin(ang)
        half = QK_ROPE // 2
        c_ref[...] = jnp.where(lane < QK_ROPE, c, 0.0)
        s1_ref[...] = jnp.where(lane < half, -s, 0.0)
        s2_ref[...] = jnp.where((lane >= half) & (lane < QK_ROPE), s, 0.0)

    blk = pl.BlockSpec((tm, LANES), lambda i: (i, 0))
    return pl.pallas_call(
        body, name="rope_tables", grid=(rows // tm,),
        in_specs=[pl.BlockSpec((tm, 1), lambda i: (i, 0)), pl.BlockSpec((1, LANES), lambda i: (0, 0))],
        out_specs=[blk, blk, blk], out_shape=[jax.ShapeDtypeStruct((rows, LANES), F32)] * 3,
        compiler_params=_params(("parallel",)),
    )(pos_col, inv_lane)


def _rot(t, c, s1, s2, sign):
    r1 = pltpu.roll(t, LANES - QK_ROPE // 2, 1) * s1
    r2 = pltpu.roll(t, QK_ROPE // 2, 1) * s2
    return t * c + (r1 + r2) if sign > 0 else t * c - (r1 + r2)


def _mla_proj(cqn, ckvn, proj, tabs, wq, wk, wv, *, tm):
    rows = cqn.shape[0]
    tm = min(tm, rows)

    def body(cq_ref, ckv_ref, kr_ref, c_ref, s1_ref, s2_ref, wq_ref, wk_ref, wv_ref, q_ref, k_ref, v_ref):
        c, s1, s2 = c_ref[...], s1_ref[...], s2_ref[...]
        q = _dot(cq_ref[...], wq_ref[...], NN)
        k = _dot(ckv_ref[...], wk_ref[...], NN)
        kpe = _rot(kr_ref[...].astype(F32), c, s1, s2, 1).astype(BF16)
        for h in range(HEADS):
            lo = h * QK_PAD
            q_ref[:, lo:lo + QK_NOPE] = q[:, lo:lo + QK_NOPE].astype(BF16)
            q_ref[:, lo + QK_NOPE:lo + QK_PAD] = _rot(q[:, lo + QK_NOPE:lo + QK_PAD], c, s1, s2, 1).astype(BF16)
            k_ref[:, lo:lo + QK_NOPE] = k[:, lo:lo + QK_NOPE].astype(BF16)
            k_ref[:, lo + QK_NOPE:lo + QK_PAD] = kpe
        v_ref[...] = _dot(ckv_ref[...], wv_ref[...], NN).astype(BF16)

    def row(w):
        return pl.BlockSpec((tm, w), lambda i: (i, 0))

    def whole(w):
        return pl.BlockSpec(w.shape, lambda i: (0, 0))

    return pl.pallas_call(
        body, name="mla_proj", grid=(rows // tm,),
        in_specs=[row(LORA), row(LORA), pl.BlockSpec((tm, LANES), lambda i: (i, COL_KR)), row(LANES), row(LANES), row(LANES),
                  whole(wq), whole(wk), whole(wv)],
        out_specs=[row(HEADS * QK_PAD), row(HEADS * QK_PAD), row(HEADS * V_DIM)],
        out_shape=[jax.ShapeDtypeStruct((rows, HEADS * QK_PAD), BF16), jax.ShapeDtypeStruct((rows, HEADS * QK_PAD), BF16),
                   jax.ShapeDtypeStruct((rows, HEADS * V_DIM), BF16)],
        compiler_params=_params(("parallel",)),
    )(cqn, ckvn, proj, *tabs, wq, wk, wv)


def _mla_fwd(q, k, v, proj, *, t):
    S = q.shape[0]
    t = min(t, S // 2)
    n = S // t
    assert S % (2 * t) == 0
    scale = QK_DIM ** -0.5

    def body(q_ref, k_ref, v_ref, z_ref, o_ref, y_ref, lse_ref):
        qi = pl.program_id(1)
        qv = q_ref[...]
        c2 = scale * LOG2E

        def block(k0, width, carry, row0):
            m_old, l_old, acc = carry
            ks = pl.ds(pl.multiple_of(k0, t), width)
            s = _dot(qv, k_ref[ks, :], NT)
            if row0 is not None:
                r = lax.broadcasted_iota(jnp.int32, s.shape, 0)
                c = lax.broadcasted_iota(jnp.int32, s.shape, 1)
                s = jnp.where(c <= r + row0, s, -1e30)
            m_new = jnp.maximum(m_old, jnp.max(s, axis=-1, keepdims=True))
            alpha = jnp.exp2((m_old - m_new) * c2)
            p = jnp.exp2((s - m_new) * c2)
            l_new = alpha * l_old + jnp.sum(p, axis=-1, keepdims=True)
            acc = alpha * acc + _dot(p.astype(BF16), v_ref[ks, :], NN)
            return m_new, l_new, acc

        init = (jnp.full((t, 1), -1e30, F32), jnp.zeros((t, 1), F32), jnp.zeros((t, V_DIM), F32))
        carry = lax.fori_loop(0, qi // 2, lambda j, cr: block(j * (2 * t), 2 * t, cr, None), init)
        m_f, l_f, acc = lax.cond(qi % 2 == 1,
                                 lambda cr: block((qi - 1) * t, 2 * t, cr, t),
                                 lambda cr: block(qi * t, t, cr, 0), carry)
        o = acc / l_f
        o_ref[...] = o
        sz, _ = _silu_parts(z_ref[...].astype(F32))
        y_ref[...] = (o * sz).astype(BF16)
        lse_ref[0] = m_f * scale + jnp.log(l_f)

    zcol = COL_ZB * (D // V_DIM)
    return pl.pallas_call(
        body, name="mla_fwd", grid=(HEADS, n),
        in_specs=[pl.BlockSpec((t, QK_PAD), lambda h, i: (i, h)),
                  pl.BlockSpec((S, QK_PAD), lambda h, i: (0, h)),
                  pl.BlockSpec((S, V_DIM), lambda h, i: (0, h)),
                  pl.BlockSpec((t, V_DIM), lambda h, i: (i, zcol + h))],
        out_specs=[pl.BlockSpec((t, V_DIM), lambda h, i: (i, h)), pl.BlockSpec((t, V_DIM), lambda h, i: (i, h)),
                   pl.BlockSpec((1, t, 1), lambda h, i: (h, i, 0))],
        out_shape=[jax.ShapeDtypeStruct((S, HEADS * V_DIM), F32), jax.ShapeDtypeStruct((S, HEADS * V_DIM), BF16),
                   jax.ShapeDtypeStruct((HEADS, S, 1), F32)],
        compiler_params=_params(("parallel", "parallel")),
    )(q, k, v, proj)


def _mla_gate_bwd(dy, o, proj, lse, *, tm):
    S = dy.shape[0]
    tm = min(tm, S)

    def body(dy_ref, o_ref, z_ref, lse_ref, do_ref, dz_ref, st_ref):
        sz, dsz = _silu_parts(z_ref[...].astype(F32))
        dyv, ov = dy_ref[...], o_ref[...]
        do = dyv * sz
        do_ref[...] = do.astype(BF16)
        dz_ref[...] = (dyv * ov * dsz).astype(BF16)
        prod = do * ov
        lane = lax.broadcasted_iota(jnp.int32, (tm, LANES), 1)
        for h in range(HEADS):
            delta = jnp.sum(prod[:, h * V_DIM:(h + 1) * V_DIM], axis=-1, keepdims=True)
            cols = jnp.where(lane == 0, lse_ref[h] * LOG2E, jnp.where(lane == 1, delta, 0.0))
            st_ref[h, 0] = cols.T[0:8, :]

    blk = pl.BlockSpec((tm, D), lambda i: (i, 0))
    return pl.pallas_call(
        body, name="mla_gate_bwd", grid=(S // tm,),
        in_specs=[blk, blk, pl.BlockSpec((tm, D), lambda i: (i, COL_ZB)), pl.BlockSpec((HEADS, tm, 1), lambda i: (0, i, 0))],
        out_specs=[blk, blk, pl.BlockSpec((HEADS, 1, 8, tm), lambda i: (0, i, 0, 0))],
        out_shape=[jax.ShapeDtypeStruct((S, D), BF16), jax.ShapeDtypeStruct((S, D), BF16),
                   jax.ShapeDtypeStruct((HEADS, S // tm, 8, tm), F32)],
        compiler_params=_params(("parallel",)),
    )(dy, o, proj, lse)


def _mla_bwd(q, k, v, do, stats, *, t, rider):
    S = q.shape[0]
    t = min(t, S)
    n = S // t
    c2 = (QK_DIM ** -0.5) * LOG2E

    n_rin, n_rout = len(rider.ins), len(rider.out_shapes)

    def body(*refs):
        q_ref, k_ref, v_ref, do_ref, st_ref = refs[:5]
        rin = refs[5:5 + n_rin]
        dq_ref, dk_ref, dv_ref = refs[5 + n_rin:8 + n_rin]
        rout = refs[8 + n_rin:8 + n_rin + n_rout]
        ki = pl.program_id(1)
        step = pl.program_id(0) * n + ki
        _ride(rider._replace(phases=rider.phases[:1]), rin, rout, refs[-2:], step, HEADS * n)

        @pl.when(ki == 0)
        def _():
            dq_ref[...] = jnp.zeros_like(dq_ref)

        kv, vv = k_ref[...], v_ref[...]

        def block(i, carry, diag):
            dk, dv = carry
            rows = pl.ds(pl.multiple_of(i * t, t), t)
            qv, dov, st = q_ref[rows, :], do_ref[rows, :], st_ref[0, i]
            s = _dot(kv, qv, NT)
            if diag:
                key = lax.broadcasted_iota(jnp.int32, s.shape, 0)
                qry = lax.broadcasted_iota(jnp.int32, s.shape, 1)
                s = jnp.where(key <= qry, s, -1e30)
            p = jnp.exp2(s * c2 - st[0:1, :])
            p16 = p.astype(BF16)
            dv = dv + _dot(p16, dov, NN)
            dp = _dot(vv, dov, NT)
            ds = (p * (dp - st[1:2, :])).astype(BF16)
            dk = dk + _dot(ds, qv, NN)
            dq_ref[rows, :] += _dot(ds, kv, TN)
            return dk, dv

        carry = block(ki, (jnp.zeros((t, QK_PAD), F32), jnp.zeros((t, V_DIM), F32)), True)
        rest = n - 1 - ki
        carry = lax.cond(rest % 2 == 1, lambda cr: block(ki + 1, cr, False), lambda cr: cr, carry)
        first = ki + 1 + rest % 2

        def pair(i, cr):
            return block(first + 2 * i + 1, block(first + 2 * i, cr, False), False)

        dk, dv = lax.fori_loop(0, rest // 2, pair, carry)
        dk_ref[...] = dk
        dv_ref[...] = dv.astype(BF16)
        _ride(rider._replace(phases=rider.phases[1:]), rin, rout, refs[-2:], step, HEADS * n)

    hbm = pl.BlockSpec(memory_space=pl.ANY)
    res = pl.pallas_call(
        body, name="mla_bwd", grid=(HEADS, n),
        in_specs=[pl.BlockSpec((S, QK_PAD), lambda h, j: (0, h)),
                  pl.BlockSpec((t, QK_PAD), lambda h, j: (j, h)),
                  pl.BlockSpec((t, V_DIM), lambda h, j: (j, h)),
                  pl.BlockSpec((S, V_DIM), lambda h, j: (0, h)),
                  pl.BlockSpec((1, n, 8, t), lambda h, j: (h, 0, 0, 0))] + [hbm] * n_rin,
        out_specs=[pl.BlockSpec((S, QK_PAD), lambda h, j: (0, h)),
                   pl.BlockSpec((t, QK_PAD), lambda h, j: (j, h)),
                   pl.BlockSpec((t, V_DIM), lambda h, j: (j, h))] + [hbm] * n_rout,
        out_shape=[jax.ShapeDtypeStruct((S, HEADS * QK_PAD), F32), jax.ShapeDtypeStruct((S, HEADS * QK_PAD), F32),
                   jax.ShapeDtypeStruct((S, HEADS * V_DIM), BF16)] + list(rider.out_shapes),
        scratch_shapes=[pltpu.SemaphoreType.DMA((rider.n_sems,)), pltpu.SemaphoreType.DMA((rider.n_sems,))],
        compiler_params=_params(("arbitrary", "arbitrary")),
    )(q, k, v, do, stats, *rider.ins)
    return res[0], res[1], res[2], list(res[3:])


def _mla_qk_post(dq, dk, tabs, *, tm):
    S = dq.shape[0]
    tm = min(tm, S)
    scale = QK_DIM ** -0.5

    def body(dq_ref, dk_ref, c_ref, s1_ref, s2_ref, q16_ref, k16_ref, kr_ref):
        c, s1, s2 = c_ref[...] * scale, s1_ref[...] * scale, s2_ref[...] * scale
        kpe = jnp.zeros((tm, LANES), F32)
        for h in range(HEADS):
            lo = h * QK_PAD
            q16_ref[:, lo:lo + QK_NOPE] = (dq_ref[:, lo:lo + QK_NOPE] * scale).astype(BF16)
            q16_ref[:, lo + QK_NOPE:lo + QK_PAD] = _rot(dq_ref[:, lo + QK_NOPE:lo + QK_PAD], c, s1, s2, -1).astype(BF16)
            kpe = kpe + dk_ref[:, lo + QK_NOPE:lo + QK_PAD]
        k16_ref[...] = (dk_ref[...] * scale).astype(BF16)
        kr_ref[...] = _rot(kpe, c, s1, s2, -1).astype(BF16)

    wide = pl.BlockSpec((tm, HEADS * QK_PAD), lambda i: (i, 0))
    lane = pl.BlockSpec((tm, LANES), lambda i: (i, 0))
    return pl.pallas_call(
        body, name="mla_qk_post", grid=(S // tm,),
        in_specs=[wide, wide, lane, lane, lane], out_specs=[wide, wide, lane],
        out_shape=[jax.ShapeDtypeStruct((S, HEADS * QK_PAD), BF16), jax.ShapeDtypeStruct((S, HEADS * QK_PAD), BF16),
                   jax.ShapeDtypeStruct((S, LANES), BF16)],
        compiler_params=_params(("parallel",)),
    )(dq, dk, *tabs)


def _mem_scores(q16, km_ref, h):
    lo = h * MEM_HEAD_DIM
    s = _dot(q16, km_ref[:, lo:lo + MEM_HEAD_DIM], NT) * (MEM_HEAD_DIM ** -0.5)
    e = jnp.exp(s - jnp.max(s, axis=-1, keepdims=True))
    return e / jnp.sum(e, axis=-1, keepdims=True)


def _mem_fwd(proj, kvm, *, tm):
    S = proj.shape[0]
    tm = min(tm, S)
    M = kvm.shape[0]

    def body(q_ref, z_ref, km_ref, vm_ref, y_ref):
        sz, _ = _silu_parts(z_ref[...].astype(F32))
        for h in range(MEM_HEADS):
            lo = h * MEM_HEAD_DIM
            p = _mem_scores(q_ref[:, lo:lo + MEM_HEAD_DIM].astype(BF16), km_ref, h)
            o = _dot(p.astype(BF16), vm_ref[:, lo:lo + MEM_HEAD_DIM], NN)
            y_ref[:, lo:lo + MEM_HEAD_DIM] = (o * sz[:, lo:lo + MEM_HEAD_DIM]).astype(BF16)

    return pl.pallas_call(
        body, name="mem_fwd", grid=(S // tm,),
        in_specs=[pl.BlockSpec((tm, D), lambda i: (i, COL_QM)), pl.BlockSpec((tm, D), lambda i: (i, COL_ZM)),
                  pl.BlockSpec((M, D), lambda i: (0, 0)), pl.BlockSpec((M, D), lambda i: (0, 1))],
        out_specs=pl.BlockSpec((tm, D), lambda i: (i, 0)),
        out_shape=jax.ShapeDtypeStruct((S, D), BF16),
        compiler_params=_params(("parallel",)),
    )(proj, proj, kvm, kvm)


def _mem_bwd(proj, kvm, dy, *, tm):
    S = proj.shape[0]
    tm = min(tm, S)
    M = kvm.shape[0]
    scale = MEM_HEAD_DIM ** -0.5

    def body(q_ref, z_ref, km_ref, vm_ref, dy_ref, dq_ref, dz_ref, dkv_ref):
        @pl.when(pl.program_id(0) == 0)
        def _():
            dkv_ref[...] = jnp.zeros_like(dkv_ref)

        sz, dsz = _silu_parts(z_ref[...].astype(F32))
        dyv = dy_ref[...]
        for h in range(MEM_HEADS):
            lo = h * MEM_HEAD_DIM
            sl = slice(lo, lo + MEM_HEAD_DIM)
            q16 = q_ref[:, sl].astype(BF16)
            p = _mem_scores(q16, km_ref, h)
            p16 = p.astype(BF16)
            o = _dot(p16, vm_ref[:, sl], NN)
            dy_h = dyv[:, sl]
            dz_ref[:, sl] = (dy_h * o * dsz[:, sl]).astype(BF16)
            do16 = (dy_h * sz[:, sl]).astype(BF16)
            dp = _dot(do16, vm_ref[:, sl], NT)
            ds = (p * (dp - jnp.sum(dp * p, axis=-1, keepdims=True)) * scale).astype(BF16)
            dq_ref[:, sl] = _dot(ds, km_ref[:, sl], NN).astype(BF16)
            dkv_ref[:, sl] += _dot(ds, q16, TN)
            dkv_ref[:, D + lo:D + lo + MEM_HEAD_DIM] += _dot(p16, do16, TN)

    blk = pl.BlockSpec((tm, D), lambda i: (i, 0))
    return pl.pallas_call(
        body, name="mem_bwd", grid=(S // tm,),
        in_specs=[pl.BlockSpec((tm, D), lambda i: (i, COL_QM)), pl.BlockSpec((tm, D), lambda i: (i, COL_ZM)),
                  pl.BlockSpec((M, D), lambda i: (0, 0)), pl.BlockSpec((M, D), lambda i: (0, 1)), blk],
        out_specs=[blk, blk, pl.BlockSpec((M, 2 * D), lambda i: (0, 0))],
        out_shape=[jax.ShapeDtypeStruct((S, D), BF16), jax.ShapeDtypeStruct((S, D), BF16),
                   jax.ShapeDtypeStruct((M, 2 * D), F32)],
        compiler_params=_params(("arbitrary",)),
    )(proj, proj, kvm, kvm, dy)


def _gmlp_common(u_ref, v_ref, lng_ref, lnb_ref):
    u, du = _gelu_parts(u_ref[...].astype(F32))
    vg, dvg = _gelu_parts(v_ref[...].astype(F32))
    mu = jnp.mean(vg, axis=-1, keepdims=True)
    vc = vg - mu
    r = lax.rsqrt(jnp.mean(vc * vc, axis=-1, keepdims=True) + EPS)
    vhat = vc * r
    vn = vhat * lng_ref[...] + lnb_ref[...]
    return u, du, dvg, r, vhat, vn.astype(BF16)


def _gmlp_fwd(proj, ln_g, ln_b, wm, bs_t):
    S = proj.shape[0]

    def body(u_ref, v_ref, z_ref, lng_ref, lnb_ref, wm_ref, bs_ref, y_ref):
        u, _, _, _, _, v16 = _gmlp_common(u_ref, v_ref, lng_ref, lnb_ref)
        sz, _ = _silu_parts(z_ref[...].astype(F32))
        for g in range(A_GROUPS):
            sl = slice(g * CHUNK, (g + 1) * CHUNK)
            sv = _dot(wm_ref[g], v16[:, sl], NN) + bs_ref[:, g:g + 1]
            y_ref[:, sl] = (u[:, sl] * sv * sz[:, sl]).astype(BF16)

    def col(c):
        return pl.BlockSpec((CHUNK, D), lambda i: (i, c))

    vec = pl.BlockSpec((1, D), lambda i: (0, 0))
    return pl.pallas_call(
        body, name="gmlp_fwd", grid=(S // CHUNK,),
        in_specs=[col(COL_U), col(COL_V), col(COL_ZA), vec, vec,
                  pl.BlockSpec((A_GROUPS, CHUNK, CHUNK), lambda i: (0, 0, 0)), pl.BlockSpec((CHUNK, A_GROUPS), lambda i: (0, 0))],
        out_specs=col(0), out_shape=jax.ShapeDtypeStruct((S, D), BF16),
        compiler_params=_params(("parallel",)),
    )(proj, proj, proj, ln_g, ln_b, wm, bs_t)


def _gmlp_bwd(proj, dy, ln_g, ln_b, wm, bs_t):
    S = proj.shape[0]

    def body(u_ref, v_ref, z_ref, dy_ref, lng_ref, lnb_ref, wm_ref, bs_ref,
             du_ref, dv_ref, dz_ref, gws_ref, dsv_ref, glg_ref, glb_ref, dvn_s):
        @pl.when(pl.program_id(0) == 0)
        def _():
            gws_ref[...] = jnp.zeros_like(gws_ref)
            dsv_ref[...] = jnp.zeros_like(dsv_ref)
            glg_ref[...] = jnp.zeros_like(glg_ref)
            glb_ref[...] = jnp.zeros_like(glb_ref)

        u, du, dvg, r, vhat, v16 = _gmlp_common(u_ref, v_ref, lng_ref, lnb_ref)
        sz, dsz = _silu_parts(z_ref[...].astype(F32))
        dyv = dy_ref[...]
        for g in range(A_GROUPS):
            sl = slice(g * CHUNK, (g + 1) * CHUNK)
            sv = _dot(wm_ref[g], v16[:, sl], NN) + bs_ref[:, g:g + 1]
            dy_g, u_g, sz_g = dyv[:, sl], u[:, sl], sz[:, sl]
            dsv = dy_g * u_g * sz_g
            du_ref[:, sl] = (dy_g * sv * sz_g * du[:, sl]).astype(BF16)
            dz_ref[:, sl] = (dy_g * u_g * sv * dsz[:, sl]).astype(BF16)
            dsv16 = dsv.astype(BF16)
            dvn_s[:, sl] = _dot(wm_ref[g], dsv16, TN)
            gws_ref[g] += _dot(dsv16, v16[:, sl], NT)
            dsv_ref[:, sl] += dsv
        dvn = dvn_s[...]
        glb_ref[...] += jnp.sum(dvn, axis=0, keepdims=True)
        glg_ref[...] += jnp.sum(dvn * vhat, axis=0, keepdims=True)
        dvh = dvn * lng_ref[...]
        dvc = r * (dvh - jnp.mean(dvh, axis=-1, keepdims=True) - vhat * jnp.mean(dvh * vhat, axis=-1, keepdims=True))
        dv_ref[...] = (dvc * dvg).astype(BF16)

    def col(c):
        return pl.BlockSpec((CHUNK, D), lambda i: (i, c))

    vec = pl.BlockSpec((1, D), lambda i: (0, 0))
    wsp = pl.BlockSpec((A_GROUPS, CHUNK, CHUNK), lambda i: (0, 0, 0))
    return pl.pallas_call(
        body, name="gmlp_bwd", grid=(S // CHUNK,),
        in_specs=[col(COL_U), col(COL_V), col(COL_ZA), col(0), vec, vec, wsp, pl.BlockSpec((CHUNK, A_GROUPS), lambda i: (0, 0))],
        out_specs=[col(0), col(0), col(0), wsp, pl.BlockSpec((CHUNK, D), lambda i: (0, 0)), vec, vec],
        out_shape=[jax.ShapeDtypeStruct((S, D), BF16)] * 3 + [
            jax.ShapeDtypeStruct((A_GROUPS, CHUNK, CHUNK), F32), jax.ShapeDtypeStruct((CHUNK, D), F32),
            jax.ShapeDtypeStruct((1, D), F32), jax.ShapeDtypeStruct((1, D), F32)],
        scratch_shapes=[pltpu.VMEM((CHUNK, D), F32)],
        compiler_params=_params(("arbitrary",)),
    )(proj, proj, proj, dy, ln_g, ln_b, wm, bs_t)


def _gate_merge(h16, ys, wg, bg, wbs, *, tm, tn):
    S = h16.shape[0]
    tm = min(tm, S)
    nj = D // tn

    def body(h_ref, ya_ref, yb_ref, ym_ref, wg0, wg1, wg2, bg0, bg1, bg2, wb0, wb1, wb2,
             mg_ref, g0_ref, g1_ref, g2_ref, p0_ref, p1_ref, p2_ref):
        hv = h_ref[...]
        acc = None
        for y_ref, wg_ref, bgr, wb_ref, g_ref, p_ref in ((ya_ref, wg0, bg0, wb0, g0_ref, p0_ref),
                                                         (yb_ref, wg1, bg1, wb1, g1_ref, p1_ref),
                                                         (ym_ref, wg2, bg2, wb2, g2_ref, p2_ref)):
            gate = _sigmoid(_dot(hv, wg_ref[...], NN) + bgr[...])
            p = _dot(y_ref[...], wb_ref[...], NN)
            g_ref[...] = gate.astype(BF16)
            p_ref[...] = p.astype(BF16)
            acc = gate * p if acc is None else acc + gate * p
        mg_ref[...] = acc.astype(BF16)

    a_spec = pl.BlockSpec((tm, D), lambda j, i: (i, 0))
    o_spec = pl.BlockSpec((tm, tn), lambda j, i: (i, j))

    def wgs(n):
        return pl.BlockSpec((D, tn), lambda j, i: (0, n * nj + j))

    def bgs(n):
        return pl.BlockSpec((1, tn), lambda j, i: (0, n * nj + j))

    wbsp = pl.BlockSpec((D, tn), lambda j, i: (0, j))
    return pl.pallas_call(
        body, name="gate_merge", grid=(nj, S // tm),
        in_specs=[a_spec] * 4 + [wgs(0), wgs(1), wgs(2), bgs(0), bgs(1), bgs(2), wbsp, wbsp, wbsp],
        out_specs=[o_spec] * 7, out_shape=[jax.ShapeDtypeStruct((S, D), BF16)] * 7,
        compiler_params=_params(("parallel", "parallel")),
    )(h16, *ys, wg, wg, wg, bg, bg, bg, *wbs)


def _gate_bwd(dmerged, gates, ps, *, tm):
    S = dmerged.shape[0]
    tm = min(tm, S)

    def body(dm_ref, g0, g1, g2, p0, p1, p2, dp0, dp1, dp2, dg_ref, gb_ref):
        @pl.when(pl.program_id(0) == 0)
        def _():
            gb_ref[...] = jnp.zeros_like(gb_ref)

        dm = dm_ref[...]
        for n, (g_ref, p_ref, dp_ref) in enumerate(((g0, p0, dp0), (g1, p1, dp1), (g2, p2, dp2))):
            gate = g_ref[...].astype(F32)
            dp_ref[...] = (dm * gate).astype(BF16)
            dg = dm * p_ref[...].astype(F32) * gate * (1.0 - gate)
            dg_ref[:, n * D:(n + 1) * D] = dg.astype(BF16)
            gb_ref[:, n * D:(n + 1) * D] += jnp.sum(dg, axis=0, keepdims=True)

    blk = pl.BlockSpec((tm, D), lambda i: (i, 0))
    return pl.pallas_call(
        body, name="gate_bwd", grid=(S // tm,),
        in_specs=[blk] * 7,
        out_specs=[blk, blk, blk, pl.BlockSpec((tm, 3 * D), lambda i: (i, 0)), pl.BlockSpec((1, 3 * D), lambda i: (0, 0))],
        out_shape=[jax.ShapeDtypeStruct((S, D), BF16)] * 3 + [jax.ShapeDtypeStruct((S, 3 * D), BF16),
                                                              jax.ShapeDtypeStruct((1, 3 * D), F32)],
        compiler_params=_params(("arbitrary",)),
    )(dmerged, *gates, *ps)


def _post_loss(x, out, target, g_post, *, tm):
    S = x.shape[0]
    tm = min(tm, S)

    def body(x_ref, o_ref, t_ref, g_ref, dy_ref, do_ref, gg_ref, ls_ref):
        @pl.when(pl.program_id(0) == 0)
        def _():
            gg_ref[...] = jnp.zeros_like(gg_ref)
            ls_ref[...] = jnp.zeros_like(ls_ref)

        ov = o_ref[...]
        r = lax.rsqrt(jnp.mean(ov * ov, axis=-1, keepdims=True) + EPS)
        n = ov * r
        err = (x_ref[...] + n * g_ref[...]) - t_ref[...]
        ls_ref[...] += 0.5 * jnp.sum(jnp.mean(err * err, axis=-1, keepdims=True))
        dy = err * (1.0 / D)
        dy_ref[...] = dy
        gg_ref[...] += jnp.sum(dy * n, axis=0, keepdims=True)
        gd = dy * g_ref[...]
        do_ref[...] = (r * (gd - n * jnp.mean(gd * n, axis=-1, keepdims=True))).astype(BF16)

    blk = pl.BlockSpec((tm, D), lambda i: (i, 0))
    vec = pl.BlockSpec((1, D), lambda i: (0, 0))
    return pl.pallas_call(
        body, name="post_loss", grid=(S // tm,),
        in_specs=[blk, blk, blk, vec],
        out_specs=[blk, blk, vec, pl.BlockSpec((1, LANES), lambda i: (0, 0))],
        out_shape=[jax.ShapeDtypeStruct((S, D), F32), jax.ShapeDtypeStruct((S, D), BF16),
                   jax.ShapeDtypeStruct((1, D), F32), jax.ShapeDtypeStruct((1, LANES), F32)],
        compiler_params=_params(("arbitrary",)),
    )(x, out, target, g_post)


def _adamw(w, g_own, g_other, m, v, cidx, *, name):
    rows, width = w.shape
    hh = rows // 2
    tr = _row_tile(hh, width, unit=8)
    nb = hh // tr

    def body(c_ref, w_ref, own_ref, oth_ref, m_ref, v_ref, g_ref, d_ref, nm_ref, nv_ref):
        mine = (pl.program_id(0) // nb) == c_ref[0]
        gv = jnp.where(mine, own_ref[...], oth_ref[...])
        delta, nm, nv = _adam_math(w_ref[...], gv, m_ref[...], v_ref[...])
        g_ref[...] = gv
        d_ref[...] = delta
        nm_ref[...] = nm
        nv_ref[...] = nv

    blk = pl.BlockSpec((tr, width), lambda i, c_ref: (i, 0))
    half = pl.BlockSpec((tr, width), lambda i, c_ref: (i % nb, 0))
    return pl.pallas_call(
        body, name=name,
        grid_spec=pltpu.PrefetchScalarGridSpec(num_scalar_prefetch=1, grid=(rows // tr,),
                                               in_specs=[blk, half, half, blk, blk], out_specs=[blk] * 4),
        out_shape=[jax.ShapeDtypeStruct((rows, width), F32)] * 4,
        compiler_params=_params(("parallel",)),
    )(cidx, w, g_own, g_other, m, v)


MESH = pl.DeviceIdType.MESH
ANY = pl.BlockSpec(memory_space=pl.ANY)


def _place():
    return lax.axis_index("x"), lax.axis_index("y"), lax.axis_index("c")


def _other_chips(x, y):
    return [(1 - x, y), (x, 1 - y), (1 - x, 1 - y)]


def _remote(src, dst, send_sem, recv_sem, dev):
    return pltpu.make_async_remote_copy(src_ref=src, dst_ref=dst, send_sem=send_sem, recv_sem=recv_sem,
                                        device_id=dev, device_id_type=MESH)


def _allgather_chips(shards):
    nw = len(shards)

    def body(*refs):
        x_refs, out_refs = refs[:nw], refs[nw:2 * nw]
        send_sems, recv_sems = refs[2 * nw:]
        x, y, c = _place()
        sibling = (x, y, 1 - c)
        chips = _other_chips(x, y)

        def half(w, px, py, hc):
            hh = shards[w].shape[0] // 2
            return out_refs[w].at[2 * px + py, pl.ds(hc * hh, hh), :]

        sent = []
        for w in range(nw):
            hh = shards[w].shape[0] // 2
            for k, (px, py) in enumerate(chips):
                cp = _remote(x_refs[w].at[pl.ds(c * hh, hh), :], half(w, x, y, c), send_sems.at[6 * w + k],
                             recv_sems.at[6 * w + k], (px, py, c))
                cp.start()
                sent.append(cp)
        for w in range(nw):
            for k, (px, py) in enumerate(chips):
                landed = half(w, px, py, c)
                _remote(landed, landed, send_sems.at[6 * w + k], recv_sems.at[6 * w + k], (px, py, c)).wait_recv()
                cp = _remote(landed, landed, send_sems.at[6 * w + 3 + k], recv_sems.at[6 * w + 3 + k], sibling)
                cp.start()
                sent.append(cp)
        for w in range(nw):
            for k, (px, py) in enumerate(chips):
                other = half(w, px, py, 1 - c)
                _remote(other, other, send_sems.at[6 * w + 3 + k], recv_sems.at[6 * w + 3 + k], sibling).wait_recv()
        for cp in sent:
            cp.wait_send()

    outs = pl.pallas_call(
        body, name="allgather_weights", in_specs=[ANY] * nw, out_specs=[ANY] * nw,
        out_shape=[jax.ShapeDtypeStruct((N_CHIPS,) + s.shape, s.dtype) for s in shards],
        scratch_shapes=[pltpu.SemaphoreType.DMA((6 * nw,)), pltpu.SemaphoreType.DMA((6 * nw,))],
    )(*shards)
    own = 2 * lax.axis_index("x") + lax.axis_index("y")
    return [lax.dynamic_update_slice(o, s[None], (own, 0, 0)) for o, s in zip(outs, shards)]


def _row_tile(rows, cols, unit=16, budget=2 * 1024 * 1024):
    best = unit
    for t in range(unit, rows + 1, unit):
        if rows % t == 0 and t * cols * 4 <= budget:
            best = t
    assert rows % best == 0, (rows, cols)
    return best


def _peers(x, y, c):
    out = []
    for k in range(1, N_DEV):
        out.append((k, (1 - x if (k >> 2) & 1 else x, 1 - y if (k >> 1) & 1 else y, 1 - c if k & 1 else c)))
    return out


def _gather_rider(shards):
    nw = len(shards)

    def half(outs, w, px, py, hc):
        hh = shards[w].shape[0] // 2
        return outs[w].at[2 * px + py, pl.ds(hc * hh, hh), :]

    def ici(ins, outs, ss, rs, w, k, px, py, c, x, y):
        hh = shards[w].shape[0] // 2
        return _remote(ins[w].at[pl.ds(c * hh, hh), :], half(outs, w, x, y, c), ss.at[6 * w + k], rs.at[6 * w + k], (px, py, c))

    def passing(outs, ss, rs, w, k, px, py, hc, sibling):
        landed = half(outs, w, px, py, hc)
        return _remote(landed, landed, ss.at[6 * w + 3 + k], rs.at[6 * w + 3 + k], sibling)

    def start(ins, outs, ss, rs):
        x, y, c = _place()
        for w in range(nw):
            for k, (px, py) in enumerate(_other_chips(x, y)):
                ici(ins, outs, ss, rs, w, k, px, py, c, x, y).start()

    def forward(ins, outs, ss, rs):
        x, y, c = _place()
        for w in range(nw):
            for k, (px, py) in enumerate(_other_chips(x, y)):
                landed = half(outs, w, px, py, c)
                _remote(landed, landed, ss.at[6 * w + k], rs.at[6 * w + k], (px, py, c)).wait_recv()
                passing(outs, ss, rs, w, k, px, py, c, (x, y, 1 - c)).start()

    def finish(ins, outs, ss, rs):
        x, y, c = _place()
        for w in range(nw):
            for k, (px, py) in enumerate(_other_chips(x, y)):
                passing(outs, ss, rs, w, k, px, py, 1 - c, (x, y, 1 - c)).wait_recv()
        for w in range(nw):
            for k, (px, py) in enumerate(_other_chips(x, y)):
                ici(ins, outs, ss, rs, w, k, px, py, c, x, y).wait_send()
                passing(outs, ss, rs, w, k, px, py, c, (x, y, 1 - c)).wait_send()

    return _Rider(ins=tuple(shards), out_shapes=tuple(jax.ShapeDtypeStruct((N_CHIPS,) + s.shape, s.dtype) for s in shards),
                  n_sems=6 * nw, phases=((0.0, start), (0.8, forward), (1.0, finish)))


def _own_blocks_in_place(gathered, shards):
    own = 2 * lax.axis_index("x") + lax.axis_index("y")
    return [lax.dynamic_update_slice(o, s[None], (own, 0, 0)) for o, s in zip(gathered, shards)]


def _exchange_rider(blocks):
    nw = len(blocks)

    def copy(ins, outs, ss, rs, w, k, peer):
        hh = blocks[w].shape[1] // 2
        px, py, pc = peer
        return _remote(ins[w].at[2 * px + py, pl.ds(pc * hh, hh), :], outs[w].at[k - 1], ss.at[7 * w + k - 1], rs.at[7 * w + k - 1], peer)

    def start(ins, outs, ss, rs):
        for w in range(nw):
            for k, peer in _peers(*_place()):
                copy(ins, outs, ss, rs, w, k, peer).start()

    def finish(ins, outs, ss, rs):
        for w in range(nw):
            for k, peer in _peers(*_place()):
                copy(ins, outs, ss, rs, w, k, peer).wait()

    return _Rider(ins=tuple(blocks),
                  out_shapes=tuple(jax.ShapeDtypeStruct((N_DEV - 1, b.shape[1] // 2, b.shape[2]), b.dtype) for b in blocks),
                  n_sems=7 * nw, phases=((0.0, start), (1.0, finish)))


def _reduce_add(own, recv, cidx, *, name):
    R, W = own.shape
    hh = R // 2
    tr = _row_tile(hh, W, budget=1024 * 1024)
    nb = hh // tr

    def body(c_ref, o_ref, r_ref, t_ref):
        s = o_ref[...]
        for k in range(N_DEV - 1):
            s = s + r_ref[k].astype(F32)
        t_ref[...] = s

    return pl.pallas_call(
        body, name=name,
        grid_spec=pltpu.PrefetchScalarGridSpec(
            num_scalar_prefetch=1, grid=(nb,),
            in_specs=[pl.BlockSpec((tr, W), lambda i, c_ref: (i + c_ref[0] * nb, 0)),
                      pl.BlockSpec((N_DEV - 1, tr, W), lambda i, c_ref: (0, i, 0))],
            out_specs=pl.BlockSpec((tr, W), lambda i, c_ref: (i, 0))),
        out_shape=jax.ShapeDtypeStruct((hh, W), F32),
        compiler_params=_params(("parallel",)),
    )(cidx, own, recv)


def _halves_exchange(ts):
    nw = len(ts)

    def body(*refs):
        t_refs, out_refs = refs[:nw], refs[nw:2 * nw]
        send_sems, recv_sems = refs[2 * nw:]
        x, y, c = _place()
        cps = []
        for w in range(nw):
            cp = _remote(t_refs[w], out_refs[w], send_sems.at[w], recv_sems.at[w], (x, y, 1 - c))
            cp.start()
            cps.append(cp)
        for cp in cps:
            cp.wait()

    return pl.pallas_call(
        body, name="grad_halves_exchange", in_specs=[ANY] * nw, out_specs=[ANY] * nw,
        out_shape=[jax.ShapeDtypeStruct(t.shape, t.dtype) for t in ts],
        scratch_shapes=[pltpu.SemaphoreType.DMA((nw,)), pltpu.SemaphoreType.DMA((nw,))],
    )(*ts)


def _adam_math(w, g, m, v):
    nm = ADAM_B1 * m + (1.0 - ADAM_B1) * g
    nv = ADAM_B2 * v + (1.0 - ADAM_B2) * (g * g)
    c1 = 1.0 - ADAM_B1 ** ADAM_STEP
    c2 = 1.0 - ADAM_B2 ** ADAM_STEP
    return -ADAM_LR * ((nm / c1) / (jnp.sqrt(nv / c2) + ADAM_EPS) + ADAM_WD * w), nm, nv


STAGE_ROWS = 32
STAGE_VEC = {"g_pre": 0, "a_ln_g": 1, "a_ln_b": 2, "mem_norm_g": 3, "g_post": 4}
STAGE_BGATE = 5
STAGE_MIX = 8
STAGE_ABS = 16


def _small_step(g, loss_row, w, m, v):
    n = len(SMALL)

    def reduce_body(*refs):
        g_r = dict(zip(SMALL, refs[:n]))
        loss_r = refs[n]
        sa_o, sw_o = refs[n + 1], refs[n + 2]
        stage, ga, gw, send_sems, recv_sems = refs[n + 3:]

        stage[...] = jnp.zeros_like(stage)
        for name, row in STAGE_VEC.items():
            stage[row:row + 1, :] = g_r[name][...]
        for t in range(3):
            stage[STAGE_BGATE + t:STAGE_BGATE + t + 1, :] = g_r["b_gate"][:, t * D:(t + 1) * D]
        stage[STAGE_MIX:STAGE_MIX + 1, 0:LORA] = g_r["q_norm_g"][...]
        stage[STAGE_MIX:STAGE_MIX + 1, LORA:2 * LORA] = g_r["kv_norm_g"][...]
        stage[STAGE_MIX:STAGE_MIX + 1, 2 * LORA:2 * LORA + LANES] = loss_r[...]
        stage[STAGE_ABS:STAGE_ABS + A_GROUPS, 0:CHUNK] = g_r["a_b_s"][...]

        x, y, c = _place()
        me = 4 * x + 2 * y + c
        ga[me] = stage[...]
        gw[me] = g_r["a_w_s"][...]
        cps, srcs = [], []
        for k in range(1, N_DEV):
            fx, fy, fc = (k >> 2) & 1, (k >> 1) & 1, k & 1
            peer = (1 - x if fx else x, 1 - y if fy else y, 1 - c if fc else c)
            for j, (src, dst) in enumerate(((stage, ga), (g_r["a_w_s"], gw))):
                cp = _remote(src, dst.at[me], send_sems.at[2 * (k - 1) + j], recv_sems.at[2 * (k - 1) + j], peer)
                cp.start()
                cps.append(cp)
            srcs.append(4 * peer[0] + 2 * peer[1] + peer[2])
        for k, src in enumerate(srcs):
            _remote(stage, ga.at[src], send_sems.at[2 * k], recv_sems.at[2 * k], (x, y, c)).wait_recv()
            _remote(g_r["a_w_s"], gw.at[src], send_sems.at[2 * k + 1], recv_sems.at[2 * k + 1], (x, y, c)).wait_recv()
        for cp in cps:
            cp.wait_send()
        sa, sw = ga[0], gw[0]
        for d in range(1, N_DEV):
            sa = sa + ga[d]
            sw = sw + gw[d]
        sa_o[...] = sa
        sw_o[...] = sw

    def update_body(*refs):
        sa, sw = refs[0][...], refs[1][...]
        w_r = dict(zip(SMALL, refs[2:n + 2]))
        m_r = dict(zip(SMALL, refs[n + 2:2 * n + 2]))
        v_r = dict(zip(SMALL, refs[2 * n + 2:3 * n + 2]))
        outs = refs[3 * n + 2:7 * n + 2]
        o_r = {name: outs[4 * i:4 * i + 4] for i, name in enumerate(SMALL)}
        loss_o = refs[7 * n + 2]

        def update(name, gsum, cols=None):
            sel = (slice(None), cols) if cols is not None else Ellipsis
            delta, nm, nv = _adam_math(w_r[name][sel], gsum, m_r[name][sel], v_r[name][sel])
            for ref, val in zip(o_r[name], (gsum, delta, nm, nv)):
                ref[sel] = val

        for name, row in STAGE_VEC.items():
            update(name, sa[row:row + 1, :])
        for t in range(3):
            update("b_gate", sa[STAGE_BGATE + t:STAGE_BGATE + t + 1, :], slice(t * D, (t + 1) * D))
        update("q_norm_g", sa[STAGE_MIX:STAGE_MIX + 1, 0:LORA])
        update("kv_norm_g", sa[STAGE_MIX:STAGE_MIX + 1, LORA:2 * LORA])
        update("a_b_s", sa[STAGE_ABS:STAGE_ABS + A_GROUPS, 0:CHUNK])
        update("a_w_s", sw)
        loss_o[...] = sa[STAGE_MIX:STAGE_MIX + 1, 2 * LORA:2 * LORA + LANES]

    vm = pl.BlockSpec(memory_space=pltpu.VMEM)
    sa, sw = pl.pallas_call(
        reduce_body, name="small_allreduce", in_specs=[vm] * (n + 1), out_specs=[vm, vm],
        out_shape=[jax.ShapeDtypeStruct((STAGE_ROWS, D), F32), jax.ShapeDtypeStruct((A_GROUPS, CHUNK, CHUNK), F32)],
        scratch_shapes=[pltpu.VMEM((STAGE_ROWS, D), F32), pltpu.VMEM((N_DEV, STAGE_ROWS, D), F32),
                        pltpu.VMEM((N_DEV, A_GROUPS, CHUNK, CHUNK), F32),
                        pltpu.SemaphoreType.DMA((2 * (N_DEV - 1),)), pltpu.SemaphoreType.DMA((2 * (N_DEV - 1),))],
        compiler_params=pltpu.CompilerParams(vmem_limit_bytes=VMEM_LIMIT),
    )(*[g[k] for k in SMALL], loss_row)
    ins = [sa, sw] + [w[k] for k in SMALL] + [m[k] for k in SMALL] + [v[k] for k in SMALL]
    out_shape = [jax.ShapeDtypeStruct(w[k].shape, F32) for k in SMALL for _ in range(4)] + [jax.ShapeDtypeStruct((1, LANES), F32)]
    res = pl.pallas_call(
        update_body, name="small_adamw", in_specs=[vm] * len(ins), out_specs=[vm] * len(out_shape), out_shape=out_shape,
        compiler_params=pltpu.CompilerParams(vmem_limit_bytes=VMEM_LIMIT),
    )(*ins)
    return {k: tuple(res[4 * i:4 * i + 4]) for i, k in enumerate(SMALL)}, res[-1]


SHARD_2D = {"w_in": (D, IN_REF // N_CHIPS), "w_uq": (LORA, HEADS * QK_DIM // N_CHIPS),
            "w_ukv": (LORA, HEADS * (QK_NOPE + V_DIM) // N_CHIPS), "w_mem_kv": (D, 2 * D // N_CHIPS),
            "w_gate": (D, 3 * D // N_CHIPS), "w_branch": (3 * D // N_CHIPS, D), "w_out": (D // N_CHIPS, D)}


def _cols(blocks):
    return jnp.concatenate([blocks[j] for j in range(N_CHIPS)], axis=1)


def _w_in_layout(gathered):
    w = _cols(gathered)
    return jnp.concatenate([w[:, :3 * D], w[:, 3 * D + 2 * LORA + QK_ROPE:], w[:, 3 * D:3 * D + 2 * LORA + QK_ROPE],
                            jnp.zeros((D, IN_PAD - IN_REF), w.dtype)], axis=1)


REST = BIG[1:]


def _rest_layouts(gathered):
    wq = jnp.pad(_cols(gathered["w_uq"]).reshape(LORA, HEADS, QK_DIM), ((0, 0), (0, 0), (0, QK_PAD - QK_DIM))).reshape(LORA, HEADS * QK_PAD)
    kv3 = _cols(gathered["w_ukv"]).reshape(LORA, HEADS, QK_NOPE + V_DIM)
    wk = jnp.pad(kv3[:, :, :QK_NOPE], ((0, 0), (0, 0), (0, QK_PAD - QK_NOPE))).reshape(LORA, HEADS * QK_PAD)
    wv = kv3[:, :, QK_NOPE:].reshape(LORA, HEADS * V_DIM)
    w_branch = gathered["w_branch"].reshape(N_CHIPS, 3, D // N_CHIPS, D).transpose(1, 0, 2, 3).reshape(3, D, D)
    return {"wq": wq, "wk": wk, "wv": wv, "w_mem_kv": _cols(gathered["w_mem_kv"]), "w_gate": _cols(gathered["w_gate"]),
            "w_branch": w_branch, "w_out": gathered["w_out"].reshape(D, D)}


def _grad_reference_layout(name, g):
    if name == "w_in":
        return jnp.concatenate([g[:, :3 * D], g[:, 6 * D:6 * D + 2 * LORA + QK_ROPE], g[:, 3 * D:6 * D]], axis=1)
    if name == "w_uq":
        return g.reshape(LORA, HEADS, QK_PAD)[:, :, :QK_DIM].reshape(LORA, HEADS * QK_DIM)
    if name == "w_ukv":
        gk, gv = g
        return jnp.concatenate([gk.reshape(LORA, HEADS, QK_PAD)[:, :, :QK_NOPE], gv.reshape(LORA, HEADS, V_DIM)],
                               axis=2).reshape(LORA, HEADS * (QK_NOPE + V_DIM))
    return g


def _grad_blocks(name, full):
    own = 2 * lax.axis_index("x") + lax.axis_index("y")
    R, C = SHARD_2D[name]
    if name == "w_branch":
        blocks = full.reshape(3, N_CHIPS, D // N_CHIPS, D).transpose(1, 0, 2, 3).reshape(N_CHIPS, R, C)
        mine = lax.dynamic_slice_in_dim(full, own * (D // N_CHIPS), D // N_CHIPS, axis=1).reshape(R, C)
    elif name == "w_out":
        blocks = full.reshape(N_CHIPS, R, C)
        mine = lax.dynamic_slice_in_dim(full, own * R, R, axis=0)
    else:
        blocks = full.reshape(R, N_CHIPS, C).transpose(1, 0, 2)
        mine = lax.dynamic_slice_in_dim(full, own * C, C, axis=1)
    return blocks.astype(BF16), mine


def _local_step(x, mem, pos_col, target, w_in, rest_shards, P):
    cidx = lax.axis_index("c").astype(jnp.int32).reshape(1)
    h16, rstd_x = _rms_fwd(x, P["g_pre"], width=D, col=0, tm=256, name="pre_norm")
    memn16, rstd_m = _rms_fwd(mem, P["mem_norm_g"], width=D, col=0, tm=256, name="mem_norm")
    proj, rest = _mm(h16, w_in, "nn", tm=512, tn=1920, tk=D, out_dtype=BF16, name="in_proj", rider=_gather_rider(rest_shards))
    W = _rest_layouts(dict(zip(REST, _own_blocks_in_place(rest, rest_shards))))

    causal = jnp.tril(jnp.ones((CHUNK, CHUNK), F32))
    wm = (P["a_w_s"] * causal[None]).astype(BF16)
    bs_t = P["a_b_s"].T
    ya = _gmlp_fwd(proj, P["a_ln_g"], P["a_ln_b"], wm, bs_t)

    inv = 1.0 / (ROPE_THETA ** (jnp.arange(0, QK_ROPE, 2, dtype=F32) / QK_ROPE))
    inv_lane = jnp.concatenate([inv, inv, jnp.zeros((LANES - QK_ROPE,), F32)])[None]
    tabs = _rope_tables(pos_col, inv_lane, tm=1024)
    cqn, rstd_q = _rms_fwd(proj, P["q_norm_g"], width=LORA, col=COL_CQ, tm=512, name="q_norm")
    ckvn, rstd_kv = _rms_fwd(proj, P["kv_norm_g"], width=LORA, col=COL_CKV, tm=512, name="kv_norm")
    q16, k16, v16 = _mla_proj(cqn, ckvn, proj, tabs, W["wq"], W["wk"], W["wv"], tm=256)
    o_b, yb, lse = _mla_fwd(q16, k16, v16, proj, t=512)

    kvm = _mm(memn16, W["w_mem_kv"], "nn", tm=256, tn=1024, tk=D, out_dtype=BF16, name="mem_kv")
    ym = _mem_fwd(proj, kvm, tm=512)

    wbs = [W["w_branch"][n] for n in range(3)]
    merged, g0, g1, g2, p0, p1, p2 = _gate_merge(h16, (ya, yb, ym), W["w_gate"], P["b_gate"], wbs, tm=512, tn=512)
    out = _mm(merged, W["w_out"], "nn", tm=512, tn=1024, tk=D, out_dtype=F32, name="out_proj")
    dy, dout, g_g_post, loss = _post_loss(x, out, target, P["g_post"], tm=256)

    full = {}
    full["w_out"] = _mm(merged, dout, "tn", tm=1024, tn=1024, tk=TN_TK, out_dtype=F32, name="gw_out")
    dmerged = _mm(dout, W["w_out"], "nt", tm=512, tn=1024, tk=D, out_dtype=F32, name="d_merged")
    dp0, dp1, dp2, dgpre, g_b_gate = _gate_bwd(dmerged, (g0, g1, g2), (p0, p1, p2), tm=256)
    full["w_gate"] = _mm(h16, dgpre, "tn", tm=1024, tn=1024, tk=TN_TK, out_dtype=F32, name="gw_gate")
    dh_gate = _mm(dgpre, W["w_gate"], "nt", tm=512, tn=D, tk=D, out_dtype=F32, name="dh_gate")
    full["w_branch"] = jnp.stack([_mm(y, dp, "tn", tm=1024, tn=1024, tk=TN_TK, out_dtype=F32, name=f"gw_branch{n}")
                                  for n, (y, dp) in enumerate(((ya, dp0), (yb, dp1), (ym, dp2)))], axis=0)
    dya, dyb, dym = [_mm(dp, wbs[n], "nt", tm=512, tn=1024, tk=D, out_dtype=F32, name=f"dy_branch{n}")
                     for n, dp in enumerate((dp0, dp1, dp2))]

    dqm, dzm, dkvm = _mem_bwd(proj, kvm, dym, tm=512)
    dkvm16 = dkvm.astype(BF16)
    full["w_mem_kv"] = _mm(memn16, dkvm16, "tn", tm=1024, tn=1024, tk=256, out_dtype=F32, name="gw_mem_kv")
    dmemn = _mm(dkvm16, W["w_mem_kv"], "nt", tm=256, tn=1024, tk=2 * D, out_dtype=F32, name="d_memn")
    _, g_mem_norm = _rms_bwd(dmemn, mem, rstd_m, P["mem_norm_g"], width=D, col=0, tm=256, out_dtype=BF16, name="mem_norm_bwd")

    own, recv = {}, {}
    early = ("w_out", "w_gate", "w_branch", "w_mem_kv")
    early_blocks = []
    for n in early:
        blocks, own[n] = _grad_blocks(n, full[n])
        early_blocks.append(blocks)
    do16, dzb, stats = _mla_gate_bwd(dyb, o_b, proj, lse, tm=512)
    dq, dk, dv16, landed = _mla_bwd(q16, k16, v16, do16, stats, t=512, rider=_exchange_rider(early_blocks))
    recv.update(zip(early, landed))
    dq16, dk16, dkr = _mla_qk_post(dq, dk, tabs, tm=256)
    g_wq = _mm(cqn, dq16, "tn", tm=512, tn=1024, tk=TN_TK, out_dtype=F32, name="gw_uq")
    g_wk = _mm(ckvn, dk16, "tn", tm=512, tn=1024, tk=TN_TK, out_dtype=F32, name="gw_uk")
    g_wv = _mm(ckvn, dv16, "tn", tm=512, tn=1024, tk=TN_TK, out_dtype=F32, name="gw_uv")
    dcqn = _mm(dq16, W["wq"], "nt", tm=512, tn=LORA, tk=HEADS * QK_PAD, out_dtype=F32, name="d_cqn")
    dckvn_k = _mm(dk16, W["wk"], "nt", tm=512, tn=LORA, tk=HEADS * QK_PAD, out_dtype=F32, name="d_ckvn_k")
    dckvn = _mm(dv16, W["wv"], "nt", tm=512, tn=LORA, tk=HEADS * V_DIM, out_dtype=F32, name="d_ckvn", add=dckvn_k)
    dcq, g_q_norm = _rms_bwd(dcqn, proj, rstd_q, P["q_norm_g"], width=LORA, col=COL_CQ, tm=512, out_dtype=BF16, name="q_norm_bwd")
    dckv, g_kv_norm = _rms_bwd(dckvn, proj, rstd_kv, P["kv_norm_g"], width=LORA, col=COL_CKV, tm=512, out_dtype=BF16, name="kv_norm_bwd")

    du, dvr, dza, gws, dsv_sum, g_ln_g, g_ln_b = _gmlp_bwd(proj, dya, P["a_ln_g"], P["a_ln_b"], wm, bs_t)
    g_a_w_s = gws * causal[None]
    g_a_b_s = dsv_sum.reshape(CHUNK, A_GROUPS, CHUNK).sum(axis=-1).T

    mid = ("w_uq", "w_ukv")
    mid_blocks = []
    for n, g in (("w_uq", g_wq), ("w_ukv", (g_wk, g_wv))):
        blocks, own[n] = _grad_blocks(n, _grad_reference_layout(n, g))
        mid_blocks.append(blocks)
    dproj = jnp.concatenate([du, dvr, dza, dzb, dqm, dzm, dcq, dckv, dkr], axis=1)
    g_w_in, landed = _mm(h16, dproj, "tn", tm=1024, tn=896, tk=TN_TK, out_dtype=F32, name="gw_in", rider=_exchange_rider(mid_blocks))
    recv.update(zip(mid, landed))
    in_blocks, own["w_in"] = _grad_blocks("w_in", _grad_reference_layout("w_in", g_w_in))
    dh, landed = _mm(dproj, w_in, "nt", tm=512, tn=D, tk=1920, out_dtype=F32, name="d_h", add=dh_gate, rider=_exchange_rider([in_blocks]))
    recv["w_in"] = landed[0]
    grad_x, g_g_pre = _rms_bwd(dh, x, rstd_x, P["g_pre"], width=D, col=0, tm=256, out_dtype=F32, name="pre_norm_bwd", residual=dy)

    totals = [_reduce_add(own[n], recv[n], cidx, name=f"grad_reduce_{n}") for n in BIG]
    small = {"g_pre": g_g_pre, "a_ln_g": g_ln_g, "a_ln_b": g_ln_b, "a_w_s": g_a_w_s, "a_b_s": g_a_b_s,
             "q_norm_g": g_q_norm, "kv_norm_g": g_kv_norm, "mem_norm_g": g_mem_norm, "b_gate": g_b_gate, "g_post": g_g_post}
    return loss, grad_x, totals, small


def kernel(x, mem, positions, g_pre, w_in, a_ln_g, a_ln_b, a_w_s, a_b_s, q_norm_g, w_uq, kv_norm_g, w_ukv, mem_norm_g, w_mem_kv, w_gate, b_gate, w_branch, w_out, g_post, loss_target, m_g_pre, m_w_in, m_a_ln_g, m_a_ln_b, m_a_w_s, m_a_b_s, m_q_norm_g, m_w_uq, m_kv_norm_g, m_w_ukv, m_mem_norm_g, m_w_mem_kv, m_w_gate, m_b_gate, m_w_branch, m_w_out, m_g_post, v_g_pre, v_w_in, v_a_ln_g, v_a_ln_b, v_a_w_s, v_a_b_s, v_q_norm_g, v_w_uq, v_kv_norm_g, v_w_ukv, v_mem_norm_g, v_w_mem_kv, v_w_gate, v_b_gate, v_w_branch, v_w_out, v_g_post):
    w = dict(g_pre=g_pre, w_in=w_in, a_ln_g=a_ln_g, a_ln_b=a_ln_b, a_w_s=a_w_s, a_b_s=a_b_s, q_norm_g=q_norm_g, w_uq=w_uq,
             kv_norm_g=kv_norm_g, w_ukv=w_ukv, mem_norm_g=mem_norm_g, w_mem_kv=w_mem_kv, w_gate=w_gate, b_gate=b_gate,
             w_branch=w_branch, w_out=w_out, g_post=g_post)
    m = dict(g_pre=m_g_pre, w_in=m_w_in, a_ln_g=m_a_ln_g, a_ln_b=m_a_ln_b, a_w_s=m_a_w_s, a_b_s=m_a_b_s, q_norm_g=m_q_norm_g,
             w_uq=m_w_uq, kv_norm_g=m_kv_norm_g, w_ukv=m_w_ukv, mem_norm_g=m_mem_norm_g, w_mem_kv=m_w_mem_kv, w_gate=m_w_gate,
             b_gate=m_b_gate, w_branch=m_w_branch, w_out=m_w_out, g_post=m_g_post)
    v = dict(g_pre=v_g_pre, w_in=v_w_in, a_ln_g=v_a_ln_g, a_ln_b=v_a_ln_b, a_w_s=v_a_w_s, a_b_s=v_a_b_s, q_norm_g=v_q_norm_g,
             w_uq=v_w_uq, kv_norm_g=v_kv_norm_g, w_ukv=v_w_ukv, mem_norm_g=v_mem_norm_g, w_mem_kv=v_w_mem_kv, w_gate=v_w_gate,
             b_gate=v_b_gate, w_branch=v_w_branch, w_out=v_w_out, g_post=v_g_post)

    def two_d(t, n):
        return t[n].reshape(SHARD_2D[n]) if n in SHARD_2D else t[n].reshape(t[n].shape[1:] if t[n].ndim > 2 else t[n].shape)

    shards = [two_d(w, n).astype(BF16) for n in BIG]
    w_in_full = _w_in_layout(_allgather_chips(shards[:1])[0])
    P = {n: two_d(w, n) for n in SMALL}

    S = x.shape[1]
    loss_row, grad_x, totals, small = _local_step(x[0], mem[0], positions.reshape(S, 1), loss_target[0], w_in_full, shards[1:], P)

    from_sibling = _halves_exchange(totals)
    cidx = lax.axis_index("c").astype(jnp.int32).reshape(1)
    res = {}
    for n, own, other in zip(BIG, totals, from_sibling):
        upd = _adamw(two_d(w, n), own, other, two_d(m, n), two_d(v, n), cidx, name=f"adamw_{n}")
        for key, t in zip(("grad", "delta", "new_m", "new_v"), upd):
            res[key, n] = t.reshape(w[n].shape)

    small_out, loss_sum = _small_step(small, loss_row, P, {n: two_d(m, n) for n in SMALL}, {n: two_d(v, n) for n in SMALL})
    for n in SMALL:
        for key, t in zip(("grad", "delta", "new_m", "new_v"), small_out[n]):
            res[key, n] = t.reshape(w[n].shape)
    loss = loss_sum[0, 0]

    outs = [loss, grad_x[None]]
    for key in ("grad", "delta", "new_m", "new_v"):
        outs += [res[key, n] for n in WEIGHTS]
    return tuple(outs)
```

```python
import math
from typing import NamedTuple

import jax
import jax.numpy as jnp
from jax import lax
from jax.experimental import pallas as pl
from jax.experimental.pallas import tpu as pltpu

F32 = jnp.float32
BF16 = jnp.bfloat16

D = 2048
EPS = 1e-6
CHUNK = 128
A_GROUPS = 16
HEADS = 16
QK_NOPE = 128
QK_ROPE = 64
QK_DIM = QK_NOPE + QK_ROPE
V_DIM = 128
LORA = 512
MEM_HEADS = 4
MEM_HEAD_DIM = 512
ROPE_THETA = 10000.0
QK_PAD = 256
IN_REF = 13376
IN_PAD = 13440
COL_U, COL_V, COL_ZA, COL_ZB, COL_QM, COL_ZM = 0, 1, 2, 3, 4, 5
COL_CQ, COL_CKV = 24, 25
COL_KR = 104

ADAM_LR = 0.001
ADAM_B1 = 0.9
ADAM_B2 = 0.999
ADAM_EPS = 1e-08
ADAM_WD = 0.01
ADAM_STEP = 10

VMEM_LIMIT = 56 * 1024 * 1024
LANES = 128
LOG2E = math.log2(math.e)

BIG = ("w_in", "w_uq", "w_ukv", "w_mem_kv", "w_gate", "w_branch", "w_out")
SMALL = ("g_pre", "a_ln_g", "a_ln_b", "a_w_s", "a_b_s", "q_norm_g", "kv_norm_g", "mem_norm_g", "b_gate", "g_post")
WEIGHTS = ("g_pre", "w_in", "a_ln_g", "a_ln_b", "a_w_s", "a_b_s", "q_norm_g", "w_uq", "kv_norm_g", "w_ukv",
           "mem_norm_g", "w_mem_kv", "w_gate", "b_gate", "w_branch", "w_out", "g_post")
N_CHIPS = 4
N_DEV = 8


def _params(sem=None):
    return pltpu.CompilerParams(dimension_semantics=sem, vmem_limit_bytes=VMEM_LIMIT)


def _sigmoid(z):
    return 1.0 / (1.0 + jnp.exp(-z))


def _gelu_parts(x):
    c = math.sqrt(2.0 / math.pi)
    x2 = x * x
    t = jnp.tanh(c * (x + 0.044715 * x * x2))
    g = 0.5 * x * (1.0 + t)
    dg = 0.5 * (1.0 + t) + 0.5 * x * (1.0 - t * t) * (c * (1.0 + 3.0 * 0.044715 * x2))
    return g, dg


def _silu_parts(z):
    s = _sigmoid(z)
    return z * s, s * (1.0 + z * (1.0 - s))


def _dot(a, b, dims):
    return lax.dot_general(a, b, (dims, ((), ())), preferred_element_type=F32)


NN = ((1,), (0,))
NT = ((1,), (1,))
TN = ((0,), (0,))
TN_TK = 2048


class _Rider(NamedTuple):
    ins: tuple
    out_shapes: tuple
    n_sems: int
    phases: tuple


def _ride(rider, refs_in, refs_out, sems, step, total):
    for frac, fn in rider.phases:
        @pl.when(step == int(frac * (total - 1)))
        def _():
            fn(refs_in, refs_out, sems[0], sems[1])


def _mm(a, b, mode, *, tm, tn, tk, out_dtype, name, add=None, rider=None):
    if mode == "nn":
        (M, K), (_, N) = a.shape, b.shape
    elif mode == "nt":
        (M, K), (N, _) = a.shape, b.shape
    else:
        (K, M), (_, N) = a.shape, b.shape
    tm, tn, tk = min(tm, M), min(tn, N), min(tk, K)
    assert M % tm == 0 and N % tn == 0 and K % tk == 0, (name, M, N, K, tm, tn, tk)
    ni, nj, nk = M // tm, N // tn, K // tk
    dims = {"nn": NN, "nt": NT, "tn": TN}[mode]
    has_add = add is not None
    n_rin = len(rider.ins) if rider else 0
    n_rout = len(rider.out_shapes) if rider else 0

    def body(*refs):
        a_ref, b_ref = refs[0], refs[1]
        pos = 2
        add_ref = refs[pos] if has_add else None
        pos += int(has_add)
        rin = refs[pos:pos + n_rin]
        pos += n_rin
        o_ref = refs[pos]
        rout = refs[pos + 1:pos + 1 + n_rout]
        pos += 1 + n_rout
        acc = refs[pos] if nk > 1 else None
        sems = refs[-2:] if rider else None
        if rider:
            step = (pl.program_id(0) * ni + pl.program_id(1)) * nk + pl.program_id(2)
            _ride(rider._replace(phases=rider.phases[:1]), rin, rout, sems, step, nj * ni * nk)
        part = _dot(a_ref[...].astype(BF16), b_ref[...].astype(BF16), dims)

        def finish(r):
            if has_add:
                r = r + add_ref[...]
            o_ref[...] = r.astype(out_dtype)

        if nk == 1:
            finish(part)
        else:
            k = pl.program_id(2)

            @pl.when(k == 0)
            def _():
                acc[...] = part

            @pl.when(k > 0)
            def _():
                acc[...] += part

            @pl.when(k == nk - 1)
            def _():
                finish(acc[...])

        if rider:
            _ride(rider._replace(phases=rider.phases[1:]), rin, rout, sems, step, nj * ni * nk)

    if mode == "nn":
        a_spec = pl.BlockSpec((tm, tk), lambda j, i, k: (i, k))
        b_spec = pl.BlockSpec((tk, tn), lambda j, i, k: (k, j))
    elif mode == "nt":
        a_spec = pl.BlockSpec((tm, tk), lambda j, i, k: (i, k))
        b_spec = pl.BlockSpec((tn, tk), lambda j, i, k: (j, k))
    else:
        a_spec = pl.BlockSpec((tk, tm), lambda j, i, k: (k, i))
        b_spec = pl.BlockSpec((tk, tn), lambda j, i, k: (k, j))
    o_spec = pl.BlockSpec((tm, tn), lambda j, i, k: (i, j))
    hbm = pl.BlockSpec(memory_space=pl.ANY)
    in_specs = [a_spec, b_spec] + ([o_spec] if has_add else []) + [hbm] * n_rin
    args = (a, b) + ((add,) if has_add else ()) + (tuple(rider.ins) if rider else ())
    scratch = [pltpu.VMEM((tm, tn), F32)] if nk > 1 else []
    if rider:
        scratch += [pltpu.SemaphoreType.DMA((rider.n_sems,)), pltpu.SemaphoreType.DMA((rider.n_sems,))]
    res = pl.pallas_call(
        body, name=name, grid=(nj, ni, nk), in_specs=in_specs, out_specs=[o_spec] + [hbm] * n_rout,
        out_shape=[jax.ShapeDtypeStruct((M, N), out_dtype)] + (list(rider.out_shapes) if rider else []),
        scratch_shapes=scratch,
        compiler_params=_params(("arbitrary",) * 3 if rider else ("parallel", "parallel", "arbitrary")),
    )(*args)
    return (res[0], list(res[1:])) if rider else res[0]


def _rms_fwd(x, g, *, width, col, tm, name):
    rows = x.shape[0]
    tm = min(tm, rows)

    def body(x_ref, g_ref, y_ref):
        xv = x_ref[...].astype(F32)
        r = lax.rsqrt(jnp.mean(xv * xv, axis=-1, keepdims=True) + EPS)
        y_ref[...] = ((xv * r) * g_ref[...]).astype(BF16)

    return pl.pallas_call(
        body, name=name, grid=(rows // tm,),
        in_specs=[pl.BlockSpec((tm, width), lambda i: (i, col)), pl.BlockSpec((1, width), lambda i: (0, 0))],
        out_specs=pl.BlockSpec((tm, width), lambda i: (i, 0)),
        out_shape=jax.ShapeDtypeStruct((rows, width), BF16),
        compiler_params=_params(("parallel",)),
    )(x, g)


def _rms_bwd(d, x, g, *, width, col, tm, out_dtype, name, residual=None):
    rows = d.shape[0]
    tm = min(tm, rows)
    has_res = residual is not None

    def body(*refs):
        d_ref, x_ref, g_ref = refs[:3]
        res_ref = refs[3] if has_res else None
        dx_ref, gg_ref = refs[-2], refs[-1]
        dv = d_ref[...]
        xv = x_ref[...].astype(F32)
        r = lax.rsqrt(jnp.mean(xv * xv, axis=-1, keepdims=True) + EPS)
        n = xv * r

        @pl.when(pl.program_id(0) == 0)
        def _():
            gg_ref[...] = jnp.zeros_like(gg_ref)

        gg_ref[...] += jnp.sum(dv * n, axis=0, keepdims=True)
        gd = dv * g_ref[...]
        dx = r * (gd - n * jnp.mean(gd * n, axis=-1, keepdims=True))
        if has_res:
            dx = dx + res_ref[...]
        dx_ref[...] = dx.astype(out_dtype)

    blk = pl.BlockSpec((tm, width), lambda i: (i, 0))
    in_specs = [blk, pl.BlockSpec((tm, width), lambda i: (i, col)),
                pl.BlockSpec((1, width), lambda i: (0, 0))] + ([blk] if has_res else [])
    args = (d, x, g) + ((residual,) if has_res else ())
    return pl.pallas_call(
        body, name=name, grid=(rows // tm,), in_specs=in_specs,
        out_specs=[blk, pl.BlockSpec((1, width), lambda i: (0, 0))],
        out_shape=[jax.ShapeDtypeStruct((rows, width), out_dtype), jax.ShapeDtypeStruct((1, width), F32)],
        compiler_params=_params(("arbitrary",)),
    )(*args)


def _rope_tables(pos_col, inv_lane, *, tm):
    rows = pos_col.shape[0]
    tm = min(tm, rows)

    def body(p_ref, f_ref, c_ref, s1_ref, s2_ref):
        ang = p_ref[...].astype(F32) * f_ref[...]
        lane = lax.broadcasted_iota(jnp.int32, ang.shape, 1)
        c, s = jnp.cos(ang), jnp.sin(ang)
        half = QK_ROPE // 2
        c_ref[...] = jnp.where(lane < QK_ROPE, c, 0.0)
        s1_ref[...] = jnp.where(lane < half, -s, 0.0)
        s2_ref[...] = jnp.where((lane >= half) & (lane < QK_ROPE), s, 0.0)

    blk = pl.BlockSpec((tm, LANES), lambda i: (i, 0))
    return pl.pallas_call(
        body, name="rope_tables", grid=(rows // tm,),
        in_specs=[pl.BlockSpec((tm, 1), lambda i: (i, 0)), pl.BlockSpec((1, LANES), lambda i: (0, 0))],
        out_specs=[blk, blk, blk], out_shape=[jax.ShapeDtypeStruct((rows, LANES), F32)] * 3,
        compiler_params=_params(("parallel",)),
    )(pos_col, inv_lane)


def _rot(t, c, s1, s2, sign):
    r1 = pltpu.roll(t, LANES - QK_ROPE // 2, 1) * s1
    r2 = pltpu.roll(t, QK_ROPE // 2, 1) * s2
    return t * c + (r1 + r2) if sign > 0 else t * c - (r1 + r2)


def _mla_proj(cqn, ckvn, proj, tabs, wq, wk, wv, *, tm):
    rows = cqn.shape[0]
    tm = min(tm, rows)

    def body(cq_ref, ckv_ref, kr_ref, c_ref, s1_ref, s2_ref, wq_ref, wk_ref, wv_ref, q_ref, k_ref, v_ref):
        c, s1, s2 = c_ref[...], s1_ref[...], s2_ref[...]
        q = _dot(cq_ref[...], wq_ref[...], NN)
        k = _dot(ckv_ref[...], wk_ref[...], NN)
        kpe = _rot(kr_ref[...].astype(F32), c, s1, s2, 1).astype(BF16)
        for h in range(HEADS):
            lo = h * QK_PAD
            q_ref[:, lo:lo + QK_NOPE] = q[:, lo:lo + QK_NOPE].astype(BF16)
            q_ref[:, lo + QK_NOPE:lo + QK_PAD] = _rot(q[:, lo + QK_NOPE:lo + QK_PAD], c, s1, s2, 1).astype(BF16)
            k_ref[:, lo:lo + QK_NOPE] = k[:, lo:lo + QK_NOPE].astype(BF16)
            k_ref[:, lo + QK_NOPE:lo + QK_PAD] = kpe
        v_ref[...] = _dot(ckv_ref[...], wv_ref[...], NN).astype(BF16)

    def row(w):
        return pl.BlockSpec((tm, w), lambda i: (i, 0))

    def whole(w):
        return pl.BlockSpec(w.shape, lambda i: (0, 0))

    return pl.pallas_call(
        body, name="mla_proj", grid=(rows // tm,),
        in_specs=[row(LORA), row(LORA), pl.BlockSpec((tm, LANES), lambda i: (i, COL_KR)), row(LANES), row(LANES), row(LANES),
                  whole(wq), whole(wk), whole(wv)],
        out_specs=[row(HEADS * QK_PAD), row(HEADS * QK_PAD), row(HEADS * V_DIM)],
        out_shape=[jax.ShapeDtypeStruct((rows, HEADS * QK_PAD), BF16), jax.ShapeDtypeStruct((rows, HEADS * QK_PAD), BF16),
                   jax.ShapeDtypeStruct((rows, HEADS * V_DIM), BF16)],
        compiler_params=_params(("parallel",)),
    )(cqn, ckvn, proj, *tabs, wq, wk, wv)


def _mla_fwd(q, k, v, proj, *, t):
    S = q.shape[0]
    t = min(t, S // 2)
    n = S // t
    assert S % (2 * t) == 0
    scale = QK_DIM ** -0.5

    def body(q_ref, k_ref, v_ref, z_ref, o_ref, y_ref, lse_ref):
        qi = pl.program_id(1)
        qv = q_ref[...]
        c2 = scale * LOG2E

        def block(k0, width, carry, row0):
            m_old, l_old, acc = carry
            ks = pl.ds(pl.multiple_of(k0, t), width)
            s = _dot(qv, k_ref[ks, :], NT)
            if row0 is not None:
                r = lax.broadcasted_iota(jnp.int32, s.shape, 0)
                c = lax.broadcasted_iota(jnp.int32, s.shape, 1)
                s = jnp.where(c <= r + row0, s, -1e30)
            m_new = jnp.maximum(m_old, jnp.max(s, axis=-1, keepdims=True))
            alpha = jnp.exp2((m_old - m_new) * c2)
            p = jnp.exp2((s - m_new) * c2)
            l_new = alpha * l_old + jnp.sum(p, axis=-1, keepdims=True)
            acc = alpha * acc + _dot(p.astype(BF16), v_ref[ks, :], NN)
            return m_new, l_new, acc

        init = (jnp.full((t, 1), -1e30, F32), jnp.zeros((t, 1), F32), jnp.zeros((t, V_DIM), F32))
        carry = lax.fori_loop(0, qi // 2, lambda j, cr: block(j * (2 * t), 2 * t, cr, None), init)
        m_f, l_f, acc = lax.cond(qi % 2 == 1,
                                 lambda cr: block((qi - 1) * t, 2 * t, cr, t),
                                 lambda cr: block(qi * t, t, cr, 0), carry)
        o = acc / l_f
        o_ref[...] = o
        sz, _ = _silu_parts(z_ref[...].astype(F32))
        y_ref[...] = (o * sz).astype(BF16)
        lse2 = (m_f * scale + jnp.log(l_f)) * LOG2E
        lane = lax.broadcasted_iota(jnp.int32, (t, LANES), 1)
        lse_ref[0, 0] = jnp.where(lane == 0, lse2, 0.0).T[0:8, :]

    zcol = COL_ZB * (D // V_DIM)
    return pl.pallas_call(
        body, name="mla_fwd", grid=(HEADS, n),
        in_specs=[pl.BlockSpec((t, QK_PAD), lambda h, i: (i, h)),
                  pl.BlockSpec((S, QK_PAD), lambda h, i: (0, h)),
                  pl.BlockSpec((S, V_DIM), lambda h, i: (0, h)),
                  pl.BlockSpec((t, V_DIM), lambda h, i: (i, zcol + h))],
        out_specs=[pl.BlockSpec((t, V_DIM), lambda h, i: (i, h)), pl.BlockSpec((t, V_DIM), lambda h, i: (i, h)),
                   pl.BlockSpec((1, 1, 8, t), lambda h, i: (h, i, 0, 0))],
        out_shape=[jax.ShapeDtypeStruct((S, HEADS * V_DIM), F32), jax.ShapeDtypeStruct((S, HEADS * V_DIM), BF16),
                   jax.ShapeDtypeStruct((HEADS, n, 8, t), F32)],
        compiler_params=_params(("parallel", "parallel")),
    )(q, k, v, proj)


def _mla_gate_bwd(dy, o, proj, lse, *, tm):
    S = dy.shape[0]
    tm = min(tm, S)

    def body(dy_ref, o_ref, z_ref, lse_ref, do_ref, dz_ref, st_ref):
        sz, dsz = _silu_parts(z_ref[...].astype(F32))
        dyv, ov = dy_ref[...], o_ref[...]
        do = dyv * sz
        do_ref[...] = do.astype(BF16)
        dz_ref[...] = (dyv * ov * dsz).astype(BF16)
        prod = do * ov
        lane = lax.broadcasted_iota(jnp.int32, (tm, LANES), 1)
        for h in range(HEADS):
            delta = jnp.sum(prod[:, h * V_DIM:(h + 1) * V_DIM], axis=-1, keepdims=True)
            st_ref[h, 0] = lse_ref[h, 0] + jnp.where(lane == 1, delta, 0.0).T[0:8, :]

    blk = pl.BlockSpec((tm, D), lambda i: (i, 0))
    return pl.pallas_call(
        body, name="mla_gate_bwd", grid=(S // tm,),
        in_specs=[blk, blk, pl.BlockSpec((tm, D), lambda i: (i, COL_ZB)), pl.BlockSpec((HEADS, 1, 8, tm), lambda i: (0, i, 0, 0))],
        out_specs=[blk, blk, pl.BlockSpec((HEADS, 1, 8, tm), lambda i: (0, i, 0, 0))],
        out_shape=[jax.ShapeDtypeStruct((S, D), BF16), jax.ShapeDtypeStruct((S, D), BF16),
                   jax.ShapeDtypeStruct((HEADS, S // tm, 8, tm), F32)],
        compiler_params=_params(("parallel",)),
    )(dy, o, proj, lse)


def _mla_bwd(q, k, v, do, stats, *, t, rider):
    S = q.shape[0]
    t = min(t, S)
    n = S // t
    c2 = (QK_DIM ** -0.5) * LOG2E

    n_rin, n_rout = len(rider.ins), len(rider.out_shapes)

    def body(*refs):
        q_ref, k_ref, v_ref, do_ref, st_ref = refs[:5]
        rin = refs[5:5 + n_rin]
        dq_ref, dk_ref, dv_ref = refs[5 + n_rin:8 + n_rin]
        rout = refs[8 + n_rin:8 + n_rin + n_rout]
        ki = pl.program_id(1)
        step = pl.program_id(0) * n + ki
        _ride(rider._replace(phases=rider.phases[:1]), rin, rout, refs[-2:], step, HEADS * n)

        @pl.when(ki == 0)
        def _():
            dq_ref[...] = jnp.zeros_like(dq_ref)

        kv, vv = k_ref[...], v_ref[...]

        def block(i, carry, diag):
            dk, dv = carry
            rows = pl.ds(pl.multiple_of(i * t, t), t)
            qv, dov, st = q_ref[rows, :], do_ref[rows, :], st_ref[0, i]
            s = _dot(kv, qv, NT)
            if diag:
                key = lax.broadcasted_iota(jnp.int32, s.shape, 0)
                qry = lax.broadcasted_iota(jnp.int32, s.shape, 1)
                s = jnp.where(key <= qry, s, -1e30)
            p = jnp.exp2(s * c2 - st[0:1, :])
            p16 = p.astype(BF16)
            dv = dv + _dot(p16, dov, NN)
            dp = _dot(vv, dov, NT)
            ds = (p * (dp - st[1:2, :])).astype(BF16)
            dk = dk + _dot(ds, qv, NN)
            dq_ref[rows, :] += _dot(ds, kv, TN)
            return dk, dv

        carry = block(ki, (jnp.zeros((t, QK_PAD), F32), jnp.zeros((t, V_DIM), F32)), True)
        rest = n - 1 - ki
        carry = lax.cond(rest % 2 == 1, lambda cr: block(ki + 1, cr, False), lambda cr: cr, carry)
        first = ki + 1 + rest % 2

        def pair(i, cr):
            return block(first + 2 * i + 1, block(first + 2 * i, cr, False), False)

        dk, dv = lax.fori_loop(0, rest // 2, pair, carry)
        dk_ref[...] = (dk * (QK_DIM ** -0.5)).astype(BF16)
        dv_ref[...] = dv.astype(BF16)
        _ride(rider._replace(phases=rider.phases[1:]), rin, rout, refs[-2:], step, HEADS * n)

    hbm = pl.BlockSpec(memory_space=pl.ANY)
    res = pl.pallas_call(
        body, name="mla_bwd", grid=(HEADS, n),
        in_specs=[pl.BlockSpec((S, QK_PAD), lambda h, j: (0, h)),
                  pl.BlockSpec((t, QK_PAD), lambda h, j: (j, h)),
                  pl.BlockSpec((t, V_DIM), lambda h, j: (j, h)),
                  pl.BlockSpec((S, V_DIM), lambda h, j: (0, h)),
                  pl.BlockSpec((1, n, 8, t), lambda h, j: (h, 0, 0, 0))] + [hbm] * n_rin,
        out_specs=[pl.BlockSpec((S, QK_PAD), lambda h, j: (0, h)),
                   pl.BlockSpec((t, QK_PAD), lambda h, j: (j, h)),
                   pl.BlockSpec((t, V_DIM), lambda h, j: (j, h))] + [hbm] * n_rout,
        out_shape=[jax.ShapeDtypeStruct((S, HEADS * QK_PAD), F32), jax.ShapeDtypeStruct((S, HEADS * QK_PAD), BF16),
                   jax.ShapeDtypeStruct((S, HEADS * V_DIM), BF16)] + list(rider.out_shapes),
        scratch_shapes=[pltpu.SemaphoreType.DMA((rider.n_sems,)), pltpu.SemaphoreType.DMA((rider.n_sems,))],
        compiler_params=_params(("arbitrary", "arbitrary")),
    )(q, k, v, do, stats, *rider.ins)
    return res[0], res[1], res[2], list(res[3:])


def _mla_qk_post(dq, dk, tabs, *, tm):
    S = dq.shape[0]
    tm = min(tm, S)
    scale = QK_DIM ** -0.5

    def body(dq_ref, dk_ref, c_ref, s1_ref, s2_ref, q16_ref, kr_ref):
        c, s1, s2 = c_ref[...], s1_ref[...], s2_ref[...]
        kpe = jnp.zeros((tm, LANES), F32)
        for h in range(HEADS):
            lo = h * QK_PAD
            q16_ref[:, lo:lo + QK_NOPE] = (dq_ref[:, lo:lo + QK_NOPE] * scale).astype(BF16)
            q16_ref[:, lo + QK_NOPE:lo + QK_PAD] = _rot(dq_ref[:, lo + QK_NOPE:lo + QK_PAD] * scale, c, s1, s2, -1).astype(BF16)
            kpe = kpe + dk_ref[:, lo + QK_NOPE:lo + QK_PAD].astype(F32)
        kr_ref[...] = _rot(kpe, c, s1, s2, -1).astype(BF16)

    wide = pl.BlockSpec((tm, HEADS * QK_PAD), lambda i: (i, 0))
    lane = pl.BlockSpec((tm, LANES), lambda i: (i, 0))
    return pl.pallas_call(
        body, name="mla_qk_post", grid=(S // tm,),
        in_specs=[wide, wide, lane, lane, lane], out_specs=[wide, lane],
        out_shape=[jax.ShapeDtypeStruct((S, HEADS * QK_PAD), BF16), jax.ShapeDtypeStruct((S, LANES), BF16)],
        compiler_params=_params(("parallel",)),
    )(dq, dk, *tabs)


def _mem_scores(q16, km_ref, h):
    lo = h * MEM_HEAD_DIM
    s = _dot(q16, km_ref[:, lo:lo + MEM_HEAD_DIM], NT) * (MEM_HEAD_DIM ** -0.5)
    e = jnp.exp(s - jnp.max(s, axis=-1, keepdims=True))
    return e / jnp.sum(e, axis=-1, keepdims=True)


def _mem_fwd(proj, kvm, *, tm):
    S = proj.shape[0]
    tm = min(tm, S)
    M = kvm.shape[0]

    def body(q_ref, z_ref, km_ref, vm_ref, y_ref):
        sz, _ = _silu_parts(z_ref[...].astype(F32))
        for h in range(MEM_HEADS):
            lo = h * MEM_HEAD_DIM
            p = _mem_scores(q_ref[:, lo:lo + MEM_HEAD_DIM].astype(BF16), km_ref, h)
            o = _dot(p.astype(BF16), vm_ref[:, lo:lo + MEM_HEAD_DIM], NN)
            y_ref[:, lo:lo + MEM_HEAD_DIM] = (o * sz[:, lo:lo + MEM_HEAD_DIM]).astype(BF16)

    return pl.pallas_call(
        body, name="mem_fwd", grid=(S // tm,),
        in_specs=[pl.BlockSpec((tm, D), lambda i: (i, COL_QM)), pl.BlockSpec((tm, D), lambda i: (i, COL_ZM)),
                  pl.BlockSpec((M, D), lambda i: (0, 0)), pl.BlockSpec((M, D), lambda i: (0, 1))],
        out_specs=pl.BlockSpec((tm, D), lambda i: (i, 0)),
        out_shape=jax.ShapeDtypeStruct((S, D), BF16),
        compiler_params=_params(("parallel",)),
    )(proj, proj, kvm, kvm)


def _mem_bwd(proj, kvm, dy, *, tm):
    S = proj.shape[0]
    tm = min(tm, S)
    M = kvm.shape[0]
    scale = MEM_HEAD_DIM ** -0.5

    def body(q_ref, z_ref, km_ref, vm_ref, dy_ref, dq_ref, dz_ref, dkv_ref):
        @pl.when(pl.program_id(0) == 0)
        def _():
            dkv_ref[...] = jnp.zeros_like(dkv_ref)

        sz, dsz = _silu_parts(z_ref[...].astype(F32))
        dyv = dy_ref[...]
        for h in range(MEM_HEADS):
            lo = h * MEM_HEAD_DIM
            sl = slice(lo, lo + MEM_HEAD_DIM)
            q16 = q_ref[:, sl].astype(BF16)
            p = _mem_scores(q16, km_ref, h)
            p16 = p.astype(BF16)
            o = _dot(p16, vm_ref[:, sl], NN)
            dy_h = dyv[:, sl]
            dz_ref[:, sl] = (dy_h * o * dsz[:, sl]).astype(BF16)
            do16 = (dy_h * sz[:, sl]).astype(BF16)
            dp = _dot(do16, vm_ref[:, sl], NT)
            ds = (p * (dp - jnp.sum(dp * p, axis=-1, keepdims=True)) * scale).astype(BF16)
            dq_ref[:, sl] = _dot(ds, km_ref[:, sl], NN).astype(BF16)
            dkv_ref[:, sl] += _dot(ds, q16, TN)
            dkv_ref[:, D + lo:D + lo + MEM_HEAD_DIM] += _dot(p16, do16, TN)

    blk = pl.BlockSpec((tm, D), lambda i: (i, 0))
    return pl.pallas_call(
        body, name="mem_bwd", grid=(S // tm,),
        in_specs=[pl.BlockSpec((tm, D), lambda i: (i, COL_QM)), pl.BlockSpec((tm, D), lambda i: (i, COL_ZM)),
                  pl.BlockSpec((M, D), lambda i: (0, 0)), pl.BlockSpec((M, D), lambda i: (0, 1)), blk],
        out_specs=[blk, blk, pl.BlockSpec((M, 2 * D), lambda i: (0, 0))],
        out_shape=[jax.ShapeDtypeStruct((S, D), BF16), jax.ShapeDtypeStruct((S, D), BF16),
                   jax.ShapeDtypeStruct((M, 2 * D), F32)],
        compiler_params=_params(("arbitrary",)),
    )(proj, proj, kvm, kvm, dy)


def _gmlp_common(u_ref, v_ref, lng_ref, lnb_ref):
    u, du = _gelu_parts(u_ref[...].astype(F32))
    vg, dvg = _gelu_parts(v_ref[...].astype(F32))
    mu = jnp.mean(vg, axis=-1, keepdims=True)
    vc = vg - mu
    r = lax.rsqrt(jnp.mean(vc * vc, axis=-1, keepdims=True) + EPS)
    vhat = vc * r
    vn = vhat * lng_ref[...] + lnb_ref[...]
    return u, du, dvg, r, vhat, vn.astype(BF16)


def _gmlp_fwd(proj, ln_g, ln_b, wm, bs_t):
    S = proj.shape[0]

    def body(u_ref, v_ref, z_ref, lng_ref, lnb_ref, wm_ref, bs_ref, y_ref):
        u, _, _, _, _, v16 = _gmlp_common(u_ref, v_ref, lng_ref, lnb_ref)
        sz, _ = _silu_parts(z_ref[...].astype(F32))
        for g in range(A_GROUPS):
            sl = slice(g * CHUNK, (g + 1) * CHUNK)
            sv = _dot(wm_ref[g], v16[:, sl], NN) + bs_ref[:, g:g + 1]
            y_ref[:, sl] = (u[:, sl] * sv * sz[:, sl]).astype(BF16)

    def col(c):
        return pl.BlockSpec((CHUNK, D), lambda i: (i, c))

    vec = pl.BlockSpec((1, D), lambda i: (0, 0))
    return pl.pallas_call(
        body, name="gmlp_fwd", grid=(S // CHUNK,),
        in_specs=[col(COL_U), col(COL_V), col(COL_ZA), vec, vec,
                  pl.BlockSpec((A_GROUPS, CHUNK, CHUNK), lambda i: (0, 0, 0)), pl.BlockSpec((CHUNK, A_GROUPS), lambda i: (0, 0))],
        out_specs=col(0), out_shape=jax.ShapeDtypeStruct((S, D), BF16),
        compiler_params=_params(("parallel",)),
    )(proj, proj, proj, ln_g, ln_b, wm, bs_t)


def _gmlp_bwd(proj, dy, ln_g, ln_b, wm, bs_t):
    S = proj.shape[0]

    def body(u_ref, v_ref, z_ref, dy_ref, lng_ref, lnb_ref, wm_ref, bs_ref,
             du_ref, dv_ref, dz_ref, gws_ref, dsv_ref, glg_ref, glb_ref, dvn_s):
        @pl.when(pl.program_id(0) == 0)
        def _():
            gws_ref[...] = jnp.zeros_like(gws_ref)
            dsv_ref[...] = jnp.zeros_like(dsv_ref)
            glg_ref[...] = jnp.zeros_like(glg_ref)
            glb_ref[...] = jnp.zeros_like(glb_ref)

        u, du, dvg, r, vhat, v16 = _gmlp_common(u_ref, v_ref, lng_ref, lnb_ref)
        sz, dsz = _silu_parts(z_ref[...].astype(F32))
        dyv = dy_ref[...]
        for g in range(A_GROUPS):
            sl = slice(g * CHUNK, (g + 1) * CHUNK)
            sv = _dot(wm_ref[g], v16[:, sl], NN) + bs_ref[:, g:g + 1]
            dy_g, u_g, sz_g = dyv[:, sl], u[:, sl], sz[:, sl]
            dsv = dy_g * u_g * sz_g
            du_ref[:, sl] = (dy_g * sv * sz_g * du[:, sl]).astype(BF16)
            dz_ref[:, sl] = (dy_g * u_g * sv * dsz[:, sl]).astype(BF16)
            dsv16 = dsv.astype(BF16)
            dvn_s[:, sl] = _dot(wm_ref[g], dsv16, TN)
            gws_ref[g] += _dot(dsv16, v16[:, sl], NT)
            dsv_ref[:, sl] += dsv
        dvn = dvn_s[...]
        glb_ref[...] += jnp.sum(dvn, axis=0, keepdims=True)
        glg_ref[...] += jnp.sum(dvn * vhat, axis=0, keepdims=True)
        dvh = dvn * lng_ref[...]
        dvc = r * (dvh - jnp.mean(dvh, axis=-1, keepdims=True) - vhat * jnp.mean(dvh * vhat, axis=-1, keepdims=True))
        dv_ref[...] = (dvc * dvg).astype(BF16)

    def col(c):
        return pl.BlockSpec((CHUNK, D), lambda i: (i, c))

    vec = pl.BlockSpec((1, D), lambda i: (0, 0))
    wsp = pl.BlockSpec((A_GROUPS, CHUNK, CHUNK), lambda i: (0, 0, 0))
    return pl.pallas_call(
        body, name="gmlp_bwd", grid=(S // CHUNK,),
        in_specs=[col(COL_U), col(COL_V), col(COL_ZA), col(0), vec, vec, wsp, pl.BlockSpec((CHUNK, A_GROUPS), lambda i: (0, 0))],
        out_specs=[col(0), col(0), col(0), wsp, pl.BlockSpec((CHUNK, D), lambda i: (0, 0)), vec, vec],
        out_shape=[jax.ShapeDtypeStruct((S, D), BF16)] * 3 + [
            jax.ShapeDtypeStruct((A_GROUPS, CHUNK, CHUNK), F32), jax.ShapeDtypeStruct((CHUNK, D), F32),
            jax.ShapeDtypeStruct((1, D), F32), jax.ShapeDtypeStruct((1, D), F32)],
        scratch_shapes=[pltpu.VMEM((CHUNK, D), F32)],
        compiler_params=_params(("arbitrary",)),
    )(proj, proj, proj, dy, ln_g, ln_b, wm, bs_t)


def _gate_merge(h16, ys, wg, bg, wbs, *, tm, tn):
    S = h16.shape[0]
    tm = min(tm, S)
    nj = D // tn

    def body(h_ref, ya_ref, yb_ref, ym_ref, wg0, wg1, wg2, bg0, bg1, bg2, wb0, wb1, wb2,
             mg_ref, g0_ref, g1_ref, g2_ref, p0_ref, p1_ref, p2_ref):
        hv = h_ref[...]
        acc = None
        for y_ref, wg_ref, bgr, wb_ref, g_ref, p_ref in ((ya_ref, wg0, bg0, wb0, g0_ref, p0_ref),
                                                         (yb_ref, wg1, bg1, wb1, g1_ref, p1_ref),
                                                         (ym_ref, wg2, bg2, wb2, g2_ref, p2_ref)):
            gate = _sigmoid(_dot(hv, wg_ref[...], NN) + bgr[...])
            p = _dot(y_ref[...], wb_ref[...], NN)
            g_ref[...] = gate.astype(BF16)
            p_ref[...] = p.astype(BF16)
            acc = gate * p if acc is None else acc + gate * p
        mg_ref[...] = acc.astype(BF16)

    a_spec = pl.BlockSpec((tm, D), lambda j, i: (i, 0))
    o_spec = pl.BlockSpec((tm, tn), lambda j, i: (i, j))

    def wgs(n):
        return pl.BlockSpec((D, tn), lambda j, i: (0, n * nj + j))

    def bgs(n):
        return pl.BlockSpec((1, tn), lambda j, i: (0, n * nj + j))

    wbsp = pl.BlockSpec((D, tn), lambda j, i: (0, j))
    return pl.pallas_call(
        body, name="gate_merge", grid=(nj, S // tm),
        in_specs=[a_spec] * 4 + [wgs(0), wgs(1), wgs(2), bgs(0), bgs(1), bgs(2), wbsp, wbsp, wbsp],
        out_specs=[o_spec] * 7, out_shape=[jax.ShapeDtypeStruct((S, D), BF16)] * 7,
        compiler_params=_params(("parallel", "parallel")),
    )(h16, *ys, wg, wg, wg, bg, bg, bg, *wbs)


def _gate_bwd(dmerged, gates, ps, *, tm):
    S = dmerged.shape[0]
    tm = min(tm, S)

    def body(dm_ref, g0, g1, g2, p0, p1, p2, dp0, dp1, dp2, dg_ref, gb_ref):
        @pl.when(pl.program_id(0) == 0)
        def _():
            gb_ref[...] = jnp.zeros_like(gb_ref)

        dm = dm_ref[...]
        for n, (g_ref, p_ref, dp_ref) in enumerate(((g0, p0, dp0), (g1, p1, dp1), (g2, p2, dp2))):
            gate = g_ref[...].astype(F32)
            dp_ref[...] = (dm * gate).astype(BF16)
            dg = dm * p_ref[...].astype(F32) * gate * (1.0 - gate)
            dg_ref[:, n * D:(n + 1) * D] = dg.astype(BF16)
            gb_ref[:, n * D:(n + 1) * D] += jnp.sum(dg, axis=0, keepdims=True)

    blk = pl.BlockSpec((tm, D), lambda i: (i, 0))
    return pl.pallas_call(
        body, name="gate_bwd", grid=(S // tm,),
        in_specs=[blk] * 7,
        out_specs=[blk, blk, blk, pl.BlockSpec((tm, 3 * D), lambda i: (i, 0)), pl.BlockSpec((1, 3 * D), lambda i: (0, 0))],
        out_shape=[jax.ShapeDtypeStruct((S, D), BF16)] * 3 + [jax.ShapeDtypeStruct((S, 3 * D), BF16),
                                                              jax.ShapeDtypeStruct((1, 3 * D), F32)],
        compiler_params=_params(("arbitrary",)),
    )(dmerged, *gates, *ps)


def _post_loss(x, out, target, g_post, *, tm):
    S = x.shape[0]
    tm = min(tm, S)

    def body(x_ref, o_ref, t_ref, g_ref, dy_ref, do_ref, gg_ref, ls_ref):
        @pl.when(pl.program_id(0) == 0)
        def _():
            gg_ref[...] = jnp.zeros_like(gg_ref)
            ls_ref[...] = jnp.zeros_like(ls_ref)

        ov = o_ref[...]
        r = lax.rsqrt(jnp.mean(ov * ov, axis=-1, keepdims=True) + EPS)
        n = ov * r
        err = (x_ref[...] + n * g_ref[...]) - t_ref[...]
        ls_ref[...] += 0.5 * jnp.sum(jnp.mean(err * err, axis=-1, keepdims=True))
        dy = err * (1.0 / D)
        dy_ref[...] = dy
        gg_ref[...] += jnp.sum(dy * n, axis=0, keepdims=True)
        gd = dy * g_ref[...]
        do_ref[...] = (r * (gd - n * jnp.mean(gd * n, axis=-1, keepdims=True))).astype(BF16)

    blk = pl.BlockSpec((tm, D), lambda i: (i, 0))
    vec = pl.BlockSpec((1, D), lambda i: (0, 0))
    return pl.pallas_call(
        body, name="post_loss", grid=(S // tm,),
        in_specs=[blk, blk, blk, vec],
        out_specs=[blk, blk, vec, pl.BlockSpec((1, LANES), lambda i: (0, 0))],
        out_shape=[jax.ShapeDtypeStruct((S, D), F32), jax.ShapeDtypeStruct((S, D), BF16),
                   jax.ShapeDtypeStruct((1, D), F32), jax.ShapeDtypeStruct((1, LANES), F32)],
        compiler_params=_params(("arbitrary",)),
    )(x, out, target, g_post)


def _adamw(w, g_own, g_other, m, v, cidx, *, name):
    rows, width = w.shape
    hh = rows // 2
    tr = _row_tile(hh, width, unit=8)
    nb = hh // tr

    def body(c_ref, w_ref, own_ref, oth_ref, m_ref, v_ref, g_ref, d_ref, nm_ref, nv_ref):
        mine = (pl.program_id(0) // nb) == c_ref[0]
        gv = jnp.where(mine, own_ref[...], oth_ref[...])
        delta, nm, nv = _adam_math(w_ref[...], gv, m_ref[...], v_ref[...])
        g_ref[...] = gv
        d_ref[...] = delta
        nm_ref[...] = nm
        nv_ref[...] = nv

    blk = pl.BlockSpec((tr, width), lambda i, c_ref: (i, 0))
    half = pl.BlockSpec((tr, width), lambda i, c_ref: (i % nb, 0))
    return pl.pallas_call(
        body, name=name,
        grid_spec=pltpu.PrefetchScalarGridSpec(num_scalar_prefetch=1, grid=(rows // tr,),
                                               in_specs=[blk, half, half, blk, blk], out_specs=[blk] * 4),
        out_shape=[jax.ShapeDtypeStruct((rows, width), F32)] * 4,
        compiler_params=_params(("parallel",)),
    )(cidx, w, g_own, g_other, m, v)


MESH = pl.DeviceIdType.MESH
ANY = pl.BlockSpec(memory_space=pl.ANY)


def _place():
    return lax.axis_index("x"), lax.axis_index("y"), lax.axis_index("c")


def _other_chips(x, y):
    return [(1 - x, y), (x, 1 - y), (1 - x, 1 - y)]


def _remote(src, dst, send_sem, recv_sem, dev):
    return pltpu.make_async_remote_copy(src_ref=src, dst_ref=dst, send_sem=send_sem, recv_sem=recv_sem,
                                        device_id=dev, device_id_type=MESH)


def _allgather_chips(shards):
    nw = len(shards)

    def body(*refs):
        x_refs, out_refs = refs[:nw], refs[nw:2 * nw]
        send_sems, recv_sems = refs[2 * nw:]
        x, y, c = _place()
        sibling = (x, y, 1 - c)
        chips = _other_chips(x, y)

        def half(w, px, py, hc):
            hh = shards[w].shape[0] // 2
            return out_refs[w].at[2 * px + py, pl.ds(hc * hh, hh), :]

        sent = []
        for w in range(nw):
            hh = shards[w].shape[0] // 2
            for k, (px, py) in enumerate(chips):
                cp = _remote(x_refs[w].at[pl.ds(c * hh, hh), :], half(w, x, y, c), send_sems.at[6 * w + k],
                             recv_sems.at[6 * w + k], (px, py, c))
                cp.start()
                sent.append(cp)
        for w in range(nw):
            for k, (px, py) in enumerate(chips):
                landed = half(w, px, py, c)
                _remote(landed, landed, send_sems.at[6 * w + k], recv_sems.at[6 * w + k], (px, py, c)).wait_recv()
                cp = _remote(landed, landed, send_sems.at[6 * w + 3 + k], recv_sems.at[6 * w + 3 + k], sibling)
                cp.start()
                sent.append(cp)
        for w in range(nw):
            for k, (px, py) in enumerate(chips):
                other = half(w, px, py, 1 - c)
                _remote(other, other, send_sems.at[6 * w + 3 + k], recv_sems.at[6 * w + 3 + k], sibling).wait_recv()
        for cp in sent:
            cp.wait_send()

    outs = pl.pallas_call(
        body, name="allgather_weights", in_specs=[ANY] * nw, out_specs=[ANY] * nw,
        out_shape=[jax.ShapeDtypeStruct((N_CHIPS,) + s.shape, s.dtype) for s in shards],
        scratch_shapes=[pltpu.SemaphoreType.DMA((6 * nw,)), pltpu.SemaphoreType.DMA((6 * nw,))],
    )(*shards)
    own = 2 * lax.axis_index("x") + lax.axis_index("y")
    return [lax.dynamic_update_slice(o, s[None], (own, 0, 0)) for o, s in zip(outs, shards)]


def _row_tile(rows, cols, unit=16, budget=2 * 1024 * 1024):
    best = unit
    for t in range(unit, rows + 1, unit):
        if rows % t == 0 and t * cols * 4 <= budget:
            best = t
    assert rows % best == 0, (rows, cols)
    return best


def _peers(x, y, c):
    out = []
    for k in range(1, N_DEV):
        out.append((k, (1 - x if (k >> 2) & 1 else x, 1 - y if (k >> 1) & 1 else y, 1 - c if k & 1 else c)))
    return out


def _gather_rider(shards):
    nw = len(shards)

    def half(outs, w, px, py, hc):
        hh = shards[w].shape[0] // 2
        return outs[w].at[2 * px + py, pl.ds(hc * hh, hh), :]

    def ici(ins, outs, ss, rs, w, k, px, py, c, x, y):
        hh = shards[w].shape[0] // 2
        return _remote(ins[w].at[pl.ds(c * hh, hh), :], half(outs, w, x, y, c), ss.at[6 * w + k], rs.at[6 * w + k], (px, py, c))

    def passing(outs, ss, rs, w, k, px, py, hc, sibling):
        landed = half(outs, w, px, py, hc)
        return _remote(landed, landed, ss.at[6 * w + 3 + k], rs.at[6 * w + 3 + k], sibling)

    def start(ins, outs, ss, rs):
        x, y, c = _place()
        for w in range(nw):
            for k, (px, py) in enumerate(_other_chips(x, y)):
                ici(ins, outs, ss, rs, w, k, px, py, c, x, y).start()

    def forward(ins, outs, ss, rs):
        x, y, c = _place()
        for w in range(nw):
            for k, (px, py) in enumerate(_other_chips(x, y)):
                landed = half(outs, w, px, py, c)
                _remote(landed, landed, ss.at[6 * w + k], rs.at[6 * w + k], (px, py, c)).wait_recv()
                passing(outs, ss, rs, w, k, px, py, c, (x, y, 1 - c)).start()

    def finish(ins, outs, ss, rs):
        x, y, c = _place()
        for w in range(nw):
            for k, (px, py) in enumerate(_other_chips(x, y)):
                passing(outs, ss, rs, w, k, px, py, 1 - c, (x, y, 1 - c)).wait_recv()
        for w in range(nw):
            for k, (px, py) in enumerate(_other_chips(x, y)):
                ici(ins, outs, ss, rs, w, k, px, py, c, x, y).wait_send()
                passing(outs, ss, rs, w, k, px, py, c, (x, y, 1 - c)).wait_send()

    return _Rider(ins=tuple(shards), out_shapes=tuple(jax.ShapeDtypeStruct((N_CHIPS,) + s.shape, s.dtype) for s in shards),
                  n_sems=6 * nw, phases=((0.0, start), (0.8, forward), (1.0, finish)))


def _own_blocks_in_place(gathered, shards):
    own = 2 * lax.axis_index("x") + lax.axis_index("y")
    return [lax.dynamic_update_slice(o, s[None], (own, 0, 0)) for o, s in zip(gathered, shards)]


def _exchange_rider(blocks):
    nw = len(blocks)

    def copy(ins, outs, ss, rs, w, k, peer):
        hh = blocks[w].shape[1] // 2
        px, py, pc = peer
        return _remote(ins[w].at[2 * px + py, pl.ds(pc * hh, hh), :], outs[w].at[k - 1], ss.at[7 * w + k - 1], rs.at[7 * w + k - 1], peer)

    def start(ins, outs, ss, rs):
        for w in range(nw):
            for k, peer in _peers(*_place()):
                copy(ins, outs, ss, rs, w, k, peer).start()

    def finish(ins, outs, ss, rs):
        for w in range(nw):
            for k, peer in _peers(*_place()):
                copy(ins, outs, ss, rs, w, k, peer).wait()

    return _Rider(ins=tuple(blocks),
                  out_shapes=tuple(jax.ShapeDtypeStruct((N_DEV - 1, b.shape[1] // 2, b.shape[2]), b.dtype) for b in blocks),
                  n_sems=7 * nw, phases=((0.0, start), (1.0, finish)))


def _reduce_add(own, recv, cidx, *, name):
    R, W = own.shape
    hh = R // 2
    tr = _row_tile(hh, W, budget=1024 * 1024)
    nb = hh // tr

    def body(c_ref, o_ref, r_ref, t_ref):
        s = o_ref[...]
        for k in range(N_DEV - 1):
            s = s + r_ref[k].astype(F32)
        t_ref[...] = s

    return pl.pallas_call(
        body, name=name,
        grid_spec=pltpu.PrefetchScalarGridSpec(
            num_scalar_prefetch=1, grid=(nb,),
            in_specs=[pl.BlockSpec((tr, W), lambda i, c_ref: (i + c_ref[0] * nb, 0)),
                      pl.BlockSpec((N_DEV - 1, tr, W), lambda i, c_ref: (0, i, 0))],
            out_specs=pl.BlockSpec((tr, W), lambda i, c_ref: (i, 0))),
        out_shape=jax.ShapeDtypeStruct((hh, W), F32),
        compiler_params=_params(("parallel",)),
    )(cidx, own, recv)


def _halves_exchange(ts):
    nw = len(ts)

    def body(*refs):
        t_refs, out_refs = refs[:nw], refs[nw:2 * nw]
        send_sems, recv_sems = refs[2 * nw:]
        x, y, c = _place()
        cps = []
        for w in range(nw):
            cp = _remote(t_refs[w], out_refs[w], send_sems.at[w], recv_sems.at[w], (x, y, 1 - c))
            cp.start()
            cps.append(cp)
        for cp in cps:
            cp.wait()

    return pl.pallas_call(
        body, name="grad_halves_exchange", in_specs=[ANY] * nw, out_specs=[ANY] * nw,
        out_shape=[jax.ShapeDtypeStruct(t.shape, t.dtype) for t in ts],
        scratch_shapes=[pltpu.SemaphoreType.DMA((nw,)), pltpu.SemaphoreType.DMA((nw,))],
    )(*ts)


def _adam_math(w, g, m, v):
    nm = ADAM_B1 * m + (1.0 - ADAM_B1) * g
    nv = ADAM_B2 * v + (1.0 - ADAM_B2) * (g * g)
    c1 = 1.0 - ADAM_B1 ** ADAM_STEP
    c2 = 1.0 - ADAM_B2 ** ADAM_STEP
    return -ADAM_LR * ((nm / c1) / (jnp.sqrt(nv / c2) + ADAM_EPS) + ADAM_WD * w), nm, nv


STAGE_ROWS = 32
STAGE_VEC = {"g_pre": 0, "a_ln_g": 1, "a_ln_b": 2, "mem_norm_g": 3, "g_post": 4}
STAGE_BGATE = 5
STAGE_MIX = 8
STAGE_ABS = 16


def _small_step(g, loss_row, w, m, v):
    n = len(SMALL)

    def reduce_body(*refs):
        g_r = dict(zip(SMALL, refs[:n]))
        loss_r = refs[n]
        sa_o, sw_o = refs[n + 1], refs[n + 2]
        stage, ga, gw, send_sems, recv_sems = refs[n + 3:]

        stage[...] = jnp.zeros_like(stage)
        for name, row in STAGE_VEC.items():
            stage[row:row + 1, :] = g_r[name][...]
        for t in range(3):
            stage[STAGE_BGATE + t:STAGE_BGATE + t + 1, :] = g_r["b_gate"][:, t * D:(t + 1) * D]
        stage[STAGE_MIX:STAGE_MIX + 1, 0:LORA] = g_r["q_norm_g"][...]
        stage[STAGE_MIX:STAGE_MIX + 1, LORA:2 * LORA] = g_r["kv_norm_g"][...]
        stage[STAGE_MIX:STAGE_MIX + 1, 2 * LORA:2 * LORA + LANES] = loss_r[...]
        stage[STAGE_ABS:STAGE_ABS + A_GROUPS, 0:CHUNK] = g_r["a_b_s"][...]

        x, y, c = _place()
        me = 4 * x + 2 * y + c
        ga[me] = stage[...]
        gw[me] = g_r["a_w_s"][...]
        cps, srcs = [], []
        for k in range(1, N_DEV):
            fx, fy, fc = (k >> 2) & 1, (k >> 1) & 1, k & 1
            peer = (1 - x if fx else x, 1 - y if fy else y, 1 - c if fc else c)
            for j, (src, dst) in enumerate(((stage, ga), (g_r["a_w_s"], gw))):
                cp = _remote(src, dst.at[me], send_sems.at[2 * (k - 1) + j], recv_sems.at[2 * (k - 1) + j], peer)
                cp.start()
                cps.append(cp)
            srcs.append(4 * peer[0] + 2 * peer[1] + peer[2])
        for k, src in enumerate(srcs):
            _remote(stage, ga.at[src], send_sems.at[2 * k], recv_sems.at[2 * k], (x, y, c)).wait_recv()
            _remote(g_r["a_w_s"], gw.at[src], send_sems.at[2 * k + 1], recv_sems.at[2 * k + 1], (x, y, c)).wait_recv()
        for cp in cps:
            cp.wait_send()
        sa, sw = ga[0], gw[0]
        for d in range(1, N_DEV):
            sa = sa + ga[d]
            sw = sw + gw[d]
        sa_o[...] = sa
        sw_o[...] = sw

    def update_body(*refs):
        sa, sw = refs[0][...], refs[1][...]
        w_r = dict(zip(SMALL, refs[2:n + 2]))
        m_r = dict(zip(SMALL, refs[n + 2:2 * n + 2]))
        v_r = dict(zip(SMALL, refs[2 * n + 2:3 * n + 2]))
        outs = refs[3 * n + 2:7 * n + 2]
        o_r = {name: outs[4 * i:4 * i + 4] for i, name in enumerate(SMALL)}
        loss_o = refs[7 * n + 2]

        def update(name, gsum, cols=None):
            sel = (slice(None), cols) if cols is not None else Ellipsis
            delta, nm, nv = _adam_math(w_r[name][sel], gsum, m_r[name][sel], v_r[name][sel])
            for ref, val in zip(o_r[name], (gsum, delta, nm, nv)):
                ref[sel] = val

        for name, row in STAGE_VEC.items():
            update(name, sa[row:row + 1, :])
        for t in range(3):
            update("b_gate", sa[STAGE_BGATE + t:STAGE_BGATE + t + 1, :], slice(t * D, (t + 1) * D))
        update("q_norm_g", sa[STAGE_MIX:STAGE_MIX + 1, 0:LORA])
        update("kv_norm_g", sa[STAGE_MIX:STAGE_MIX + 1, LORA:2 * LORA])
        update("a_b_s", sa[STAGE_ABS:STAGE_ABS + A_GROUPS, 0:CHUNK])
        update("a_w_s", sw)
        loss_o[...] = sa[STAGE_MIX:STAGE_MIX + 1, 2 * LORA:2 * LORA + LANES]

    vm = pl.BlockSpec(memory_space=pltpu.VMEM)
    sa, sw = pl.pallas_call(
        reduce_body, name="small_allreduce", in_specs=[vm] * (n + 1), out_specs=[vm, vm],
        out_shape=[jax.ShapeDtypeStruct((STAGE_ROWS, D), F32), jax.ShapeDtypeStruct((A_GROUPS, CHUNK, CHUNK), F32)],
        scratch_shapes=[pltpu.VMEM((STAGE_ROWS, D), F32), pltpu.VMEM((N_DEV, STAGE_ROWS, D), F32),
                        pltpu.VMEM((N_DEV, A_GROUPS, CHUNK, CHUNK), F32),
                        pltpu.SemaphoreType.DMA((2 * (N_DEV - 1),)), pltpu.SemaphoreType.DMA((2 * (N_DEV - 1),))],
        compiler_params=pltpu.CompilerParams(vmem_limit_bytes=VMEM_LIMIT),
    )(*[g[k] for k in SMALL], loss_row)
    ins = [sa, sw] + [w[k] for k in SMALL] + [m[k] for k in SMALL] + [v[k] for k in SMALL]
    out_shape = [jax.ShapeDtypeStruct(w[k].shape, F32) for k in SMALL for _ in range(4)] + [jax.ShapeDtypeStruct((1, LANES), F32)]
    res = pl.pallas_call(
        update_body, name="small_adamw", in_specs=[vm] * len(ins), out_specs=[vm] * len(out_shape), out_shape=out_shape,
        compiler_params=pltpu.CompilerParams(vmem_limit_bytes=VMEM_LIMIT),
    )(*ins)
    return {k: tuple(res[4 * i:4 * i + 4]) for i, k in enumerate(SMALL)}, res[-1]


SHARD_2D = {"w_in": (D, IN_REF // N_CHIPS), "w_uq": (LORA, HEADS * QK_DIM // N_CHIPS),
            "w_ukv": (LORA, HEADS * (QK_NOPE + V_DIM) // N_CHIPS), "w_mem_kv": (D, 2 * D // N_CHIPS),
            "w_gate": (D, 3 * D // N_CHIPS), "w_branch": (3 * D // N_CHIPS, D), "w_out": (D // N_CHIPS, D)}


def _cols(blocks):
    return jnp.concatenate([blocks[j] for j in range(N_CHIPS)], axis=1)


def _w_in_layout(gathered):
    w = _cols(gathered)
    return jnp.concatenate([w[:, :3 * D], w[:, 3 * D + 2 * LORA + QK_ROPE:], w[:, 3 * D:3 * D + 2 * LORA + QK_ROPE],
                            jnp.zeros((D, IN_PAD - IN_REF), w.dtype)], axis=1)


REST = BIG[1:]


def _rest_layouts(gathered):
    wq = jnp.pad(_cols(gathered["w_uq"]).reshape(LORA, HEADS, QK_DIM), ((0, 0), (0, 0), (0, QK_PAD - QK_DIM))).reshape(LORA, HEADS * QK_PAD)
    kv3 = _cols(gathered["w_ukv"]).reshape(LORA, HEADS, QK_NOPE + V_DIM)
    wk = jnp.pad(kv3[:, :, :QK_NOPE], ((0, 0), (0, 0), (0, QK_PAD - QK_NOPE))).reshape(LORA, HEADS * QK_PAD)
    wv = kv3[:, :, QK_NOPE:].reshape(LORA, HEADS * V_DIM)
    w_branch = gathered["w_branch"].reshape(N_CHIPS, 3, D // N_CHIPS, D).transpose(1, 0, 2, 3).reshape(3, D, D)
    return {"wq": wq, "wk": wk, "wv": wv, "w_mem_kv": _cols(gathered["w_mem_kv"]), "w_gate": _cols(gathered["w_gate"]),
            "w_branch": w_branch, "w_out": gathered["w_out"].reshape(D, D)}


def _grad_reference_layout(name, g):
    if name == "w_in":
        return jnp.concatenate([g[:, :3 * D], g[:, 6 * D:6 * D + 2 * LORA + QK_ROPE], g[:, 3 * D:6 * D]], axis=1)
    if name == "w_uq":
        return g.reshape(LORA, HEADS, QK_PAD)[:, :, :QK_DIM].reshape(LORA, HEADS * QK_DIM)
    if name == "w_ukv":
        gk, gv = g
        return jnp.concatenate([gk.reshape(LORA, HEADS, QK_PAD)[:, :, :QK_NOPE], gv.reshape(LORA, HEADS, V_DIM)],
                               axis=2).reshape(LORA, HEADS * (QK_NOPE + V_DIM))
    return g


def _grad_blocks(name, full):
    own = 2 * lax.axis_index("x") + lax.axis_index("y")
    R, C = SHARD_2D[name]
    if name == "w_branch":
        blocks = full.reshape(3, N_CHIPS, D // N_CHIPS, D).transpose(1, 0, 2, 3).reshape(N_CHIPS, R, C)
        mine = lax.dynamic_slice_in_dim(full, own * (D // N_CHIPS), D // N_CHIPS, axis=1).reshape(R, C)
    elif name == "w_out":
        blocks = full.reshape(N_CHIPS, R, C)
        mine = lax.dynamic_slice_in_dim(full, own * R, R, axis=0)
    else:
        blocks = full.reshape(R, N_CHIPS, C).transpose(1, 0, 2)
        mine = lax.dynamic_slice_in_dim(full, own * C, C, axis=1)
    return blocks.astype(BF16), mine


def _local_step(x, mem, pos_col, target, w_in, rest_shards, P):
    cidx = lax.axis_index("c").astype(jnp.int32).reshape(1)
    h16 = _rms_fwd(x, P["g_pre"], width=D, col=0, tm=256, name="pre_norm")
    memn16 = _rms_fwd(mem, P["mem_norm_g"], width=D, col=0, tm=256, name="mem_norm")
    proj, rest = _mm(h16, w_in, "nn", tm=512, tn=1920, tk=D, out_dtype=BF16, name="in_proj", rider=_gather_rider(rest_shards))
    W = _rest_layouts(dict(zip(REST, _own_blocks_in_place(rest, rest_shards))))

    causal = jnp.tril(jnp.ones((CHUNK, CHUNK), F32))
    wm = (P["a_w_s"] * causal[None]).astype(BF16)
    bs_t = P["a_b_s"].T
    ya = _gmlp_fwd(proj, P["a_ln_g"], P["a_ln_b"], wm, bs_t)

    inv = 1.0 / (ROPE_THETA ** (jnp.arange(0, QK_ROPE, 2, dtype=F32) / QK_ROPE))
    inv_lane = jnp.concatenate([inv, inv, jnp.zeros((LANES - QK_ROPE,), F32)])[None]
    tabs = _rope_tables(pos_col, inv_lane, tm=1024)
    cqn = _rms_fwd(proj, P["q_norm_g"], width=LORA, col=COL_CQ, tm=512, name="q_norm")
    ckvn = _rms_fwd(proj, P["kv_norm_g"], width=LORA, col=COL_CKV, tm=512, name="kv_norm")
    q16, k16, v16 = _mla_proj(cqn, ckvn, proj, tabs, W["wq"], W["wk"], W["wv"], tm=256)
    o_b, yb, lse = _mla_fwd(q16, k16, v16, proj, t=512)

    kvm = _mm(memn16, W["w_mem_kv"], "nn", tm=256, tn=1024, tk=D, out_dtype=BF16, name="mem_kv")
    ym = _mem_fwd(proj, kvm, tm=512)

    wbs = [W["w_branch"][n] for n in range(3)]
    merged, g0, g1, g2, p0, p1, p2 = _gate_merge(h16, (ya, yb, ym), W["w_gate"], P["b_gate"], wbs, tm=512, tn=512)
    out = _mm(merged, W["w_out"], "nn", tm=512, tn=1024, tk=D, out_dtype=F32, name="out_proj")
    dy, dout, g_g_post, loss = _post_loss(x, out, target, P["g_post"], tm=256)

    full = {}
    full["w_out"] = _mm(merged, dout, "tn", tm=1024, tn=1024, tk=TN_TK, out_dtype=F32, name="gw_out")
    dmerged = _mm(dout, W["w_out"], "nt", tm=512, tn=1024, tk=D, out_dtype=F32, name="d_merged")
    dp0, dp1, dp2, dgpre, g_b_gate = _gate_bwd(dmerged, (g0, g1, g2), (p0, p1, p2), tm=256)
    full["w_gate"] = _mm(h16, dgpre, "tn", tm=1024, tn=1024, tk=TN_TK, out_dtype=F32, name="gw_gate")
    dh_gate = _mm(dgpre, W["w_gate"], "nt", tm=1024, tn=1024, tk=3 * D // 2, out_dtype=F32, name="dh_gate")
    full["w_branch"] = jnp.stack([_mm(y, dp, "tn", tm=1024, tn=1024, tk=TN_TK, out_dtype=F32, name=f"gw_branch{n}")
                                  for n, (y, dp) in enumerate(((ya, dp0), (yb, dp1), (ym, dp2)))], axis=0)
    dya, dyb, dym = [_mm(dp, wbs[n], "nt", tm=512, tn=1024, tk=D, out_dtype=F32, name=f"dy_branch{n}")
                     for n, dp in enumerate((dp0, dp1, dp2))]

    dqm, dzm, dkvm = _mem_bwd(proj, kvm, dym, tm=512)
    dkvm16 = dkvm.astype(BF16)
    full["w_mem_kv"] = _mm(memn16, dkvm16, "tn", tm=1024, tn=1024, tk=256, out_dtype=F32, name="gw_mem_kv")
    dmemn = _mm(dkvm16, W["w_mem_kv"], "nt", tm=256, tn=1024, tk=2 * D, out_dtype=F32, name="d_memn")
    _, g_mem_norm = _rms_bwd(dmemn, mem, P["mem_norm_g"], width=D, col=0, tm=256, out_dtype=BF16, name="mem_norm_bwd")

    own, recv = {}, {}
    early = ("w_out", "w_gate", "w_branch", "w_mem_kv")
    early_blocks = []
    for n in early:
        blocks, own[n] = _grad_blocks(n, full[n])
        early_blocks.append(blocks)
    do16, dzb, stats = _mla_gate_bwd(dyb, o_b, proj, lse, tm=512)
    dq, dk16, dv16, landed = _mla_bwd(q16, k16, v16, do16, stats, t=512, rider=_exchange_rider(early_blocks))
    recv.update(zip(early, landed))
    dq16, dkr = _mla_qk_post(dq, dk16, tabs, tm=256)
    g_wq = _mm(cqn, dq16, "tn", tm=512, tn=1024, tk=TN_TK, out_dtype=F32, name="gw_uq")
    g_wk = _mm(ckvn, dk16, "tn", tm=512, tn=1024, tk=TN_TK, out_dtype=F32, name="gw_uk")
    g_wv = _mm(ckvn, dv16, "tn", tm=512, tn=1024, tk=TN_TK, out_dtype=F32, name="gw_uv")
    dcqn = _mm(dq16, W["wq"], "nt", tm=512, tn=LORA, tk=HEADS * QK_PAD, out_dtype=F32, name="d_cqn")
    dckvn_k = _mm(dk16, W["wk"], "nt", tm=512, tn=LORA, tk=HEADS * QK_PAD, out_dtype=F32, name="d_ckvn_k")
    dckvn = _mm(dv16, W["wv"], "nt", tm=512, tn=LORA, tk=HEADS * V_DIM, out_dtype=F32, name="d_ckvn", add=dckvn_k)
    dcq, g_q_norm = _rms_bwd(dcqn, proj, P["q_norm_g"], width=LORA, col=COL_CQ, tm=512, out_dtype=BF16, name="q_norm_bwd")
    dckv, g_kv_norm = _rms_bwd(dckvn, proj, P["kv_norm_g"], width=LORA, col=COL_CKV, tm=512, out_dtype=BF16, name="kv_norm_bwd")

    du, dvr, dza, gws, dsv_sum, g_ln_g, g_ln_b = _gmlp_bwd(proj, dya, P["a_ln_g"], P["a_ln_b"], wm, bs_t)
    g_a_w_s = gws * causal[None]
    g_a_b_s = dsv_sum.reshape(CHUNK, A_GROUPS, CHUNK).sum(axis=-1).T

    mid = ("w_uq", "w_ukv")
    mid_blocks = []
    for n, g in (("w_uq", g_wq), ("w_ukv", (g_wk, g_wv))):
        blocks, own[n] = _grad_blocks(n, _grad_reference_layout(n, g))
        mid_blocks.append(blocks)
    dproj = jnp.concatenate([du, dvr, dza, dzb, dqm, dzm, dcq, dckv, dkr], axis=1)
    g_w_in, landed = _mm(h16, dproj, "tn", tm=1024, tn=896, tk=TN_TK, out_dtype=F32, name="gw_in", rider=_exchange_rider(mid_blocks))
    recv.update(zip(mid, landed))
    in_blocks, own["w_in"] = _grad_blocks("w_in", _grad_reference_layout("w_in", g_w_in))
    dh, landed = _mm(dproj, w_in, "nt", tm=1024, tn=1024, tk=2688, out_dtype=F32, name="d_h", add=dh_gate, rider=_exchange_rider([in_blocks]))
    recv["w_in"] = landed[0]
    grad_x, g_g_pre = _rms_bwd(dh, x, P["g_pre"], width=D, col=0, tm=256, out_dtype=F32, name="pre_norm_bwd", residual=dy)

    totals = [_reduce_add(own[n], recv[n], cidx, name=f"grad_reduce_{n}") for n in BIG]
    small = {"g_pre": g_g_pre, "a_ln_g": g_ln_g, "a_ln_b": g_ln_b, "a_w_s": g_a_w_s, "a_b_s": g_a_b_s,
             "q_norm_g": g_q_norm, "kv_norm_g": g_kv_norm, "mem_norm_g": g_mem_norm, "b_gate": g_b_gate, "g_post": g_g_post}
    return loss, grad_x, totals, small


def kernel(x, mem, positions, g_pre, w_in, a_ln_g, a_ln_b, a_w_s, a_b_s, q_norm_g, w_uq, kv_norm_g, w_ukv, mem_norm_g, w_mem_kv, w_gate, b_gate, w_branch, w_out, g_post, loss_target, m_g_pre, m_w_in, m_a_ln_g, m_a_ln_b, m_a_w_s, m_a_b_s, m_q_norm_g, m_w_uq, m_kv_norm_g, m_w_ukv, m_mem_norm_g, m_w_mem_kv, m_w_gate, m_b_gate, m_w_branch, m_w_out, m_g_post, v_g_pre, v_w_in, v_a_ln_g, v_a_ln_b, v_a_w_s, v_a_b_s, v_q_norm_g, v_w_uq, v_kv_norm_g, v_w_ukv, v_mem_norm_g, v_w_mem_kv, v_w_gate, v_b_gate, v_w_branch, v_w_out, v_g_post):
    w = dict(g_pre=g_pre, w_in=w_in, a_ln_g=a_ln_g, a_ln_b=a_ln_b, a_w_s=a_w_s, a_b_s=a_b_s, q_norm_g=q_norm_g, w_uq=w_uq,
             kv_norm_g=kv_norm_g, w_ukv=w_ukv, mem_norm_g=mem_norm_g, w_mem_kv=w_mem_kv, w_gate=w_gate, b_gate=b_gate,
             w_branch=w_branch, w_out=w_out, g_post=g_post)
    m = dict(g_pre=m_g_pre, w_in=m_w_in, a_ln_g=m_a_ln_g, a_ln_b=m_a_ln_b, a_w_s=m_a_w_s, a_b_s=m_a_b_s, q_norm_g=m_q_norm_g,
             w_uq=m_w_uq, kv_norm_g=m_kv_norm_g, w_ukv=m_w_ukv, mem_norm_g=m_mem_norm_g, w_mem_kv=m_w_mem_kv, w_gate=m_w_gate,
             b_gate=m_b_gate, w_branch=m_w_branch, w_out=m_w_out, g_post=m_g_post)
    v = dict(g_pre=v_g_pre, w_in=v_w_in, a_ln_g=v_a_ln_g, a_ln_b=v_a_ln_b, a_w_s=v_a_w_s, a_b_s=v_a_b_s, q_norm_g=v_q_norm_g,
             w_uq=v_w_uq, kv_norm_g=v_kv_norm_g, w_ukv=v_w_ukv, mem_norm_g=v_mem_norm_g, w_mem_kv=v_w_mem_kv, w_gate=v_w_gate,
             b_gate=v_b_gate, w_branch=v_w_branch, w_out=v_w_out, g_post=v_g_post)

    def two_d(t, n):
        return t[n].reshape(SHARD_2D[n]) if n in SHARD_2D else t[n].reshape(t[n].shape[1:] if t[n].ndim > 2 else t[n].shape)

    shards = [two_d(w, n).astype(BF16) for n in BIG]
    w_in_full = _w_in_layout(_allgather_chips(shards[:1])[0])
    P = {n: two_d(w, n) for n in SMALL}

    S = x.shape[1]
    loss_row, grad_x, totals, small = _local_step(x[0], mem[0], positions.reshape(S, 1), loss_target[0], w_in_full, shards[1:], P)

    from_sibling = _halves_exchange(totals)
    cidx = lax.axis_index("c").astype(jnp.int32).reshape(1)
    res = {}
    for n, own, other in zip(BIG, totals, from_sibling):
        upd = _adamw(two_d(w, n), own, other, two_d(m, n), two_d(v, n), cidx, name=f"adamw_{n}")
        for key, t in zip(("grad", "delta", "new_m", "new_v"), upd):
            res[key, n] = t.reshape(w[n].shape)

    small_out, loss_sum = _small_step(small, loss_row, P, {n: two_d(m, n) for n in SMALL}, {n: two_d(v, n) for n in SMALL})
    for n in SMALL:
        for key, t in zip(("grad", "delta", "new_m", "new_v"), small_out[n]):
            res[key, n] = t.reshape(w[n].shape)
    loss = loss_sum[0, 0]

    outs = [loss, grad_x[None]]
    for key in ("grad", "delta", "new_m", "new_v"):
        outs += [res[key, n] for n in WEIGHTS]
    return tuple(outs)
```

```python
import math
from typing import NamedTuple

import jax
import jax.numpy as jnp
from jax import lax
from jax.experimental import pallas as pl
from jax.experimental.pallas import tpu as pltpu

F32 = jnp.float32
BF16 = jnp.bfloat16

D = 2048
EPS = 1e-6
CHUNK = 128
A_GROUPS = 16
HEADS = 16
QK_NOPE = 128
QK_ROPE = 64
QK_DIM = QK_NOPE + QK_ROPE
V_DIM = 128
LORA = 512
MEM_HEADS = 4
MEM_HEAD_DIM = 512
ROPE_THETA = 10000.0
QK_PAD = 256
IN_REF = 13376
IN_PAD = 13440
COL_U, COL_V, COL_ZA, COL_ZB, COL_QM, COL_ZM = 0, 1, 2, 3, 4, 5
COL_CQ, COL_CKV = 24, 25
COL_KR = 104

ADAM_LR = 0.001
ADAM_B1 = 0.9
ADAM_B2 = 0.999
ADAM_EPS = 1e-08
ADAM_WD = 0.01
ADAM_STEP = 10

VMEM_LIMIT = 56 * 1024 * 1024
LANES = 128
LOG2E = math.log2(math.e)

BIG = ("w_in", "w_uq", "w_ukv", "w_mem_kv", "w_gate", "w_branch", "w_out")
SMALL = ("g_pre", "a_ln_g", "a_ln_b", "a_w_s", "a_b_s", "q_norm_g", "kv_norm_g", "mem_norm_g", "b_gate", "g_post")
WEIGHTS = ("g_pre", "w_in", "a_ln_g", "a_ln_b", "a_w_s", "a_b_s", "q_norm_g", "w_uq", "kv_norm_g", "w_ukv",
           "mem_norm_g", "w_mem_kv", "w_gate", "b_gate", "w_branch", "w_out", "g_post")
N_CHIPS = 4
N_DEV = 8


def _params(sem=None):
    return pltpu.CompilerParams(dimension_semantics=sem, vmem_limit_bytes=VMEM_LIMIT)


def _sigmoid(z):
    return 1.0 / (1.0 + jnp.exp(-z))


def _gelu_parts(x):
    c = math.sqrt(2.0 / math.pi)
    x2 = x * x
    t = jnp.tanh(c * (x + 0.044715 * x * x2))
    g = 0.5 * x * (1.0 + t)
    dg = 0.5 * (1.0 + t) + 0.5 * x * (1.0 - t * t) * (c * (1.0 + 3.0 * 0.044715 * x2))
    return g, dg


def _silu_parts(z):
    s = _sigmoid(z)
    return z * s, s * (1.0 + z * (1.0 - s))


def _dot(a, b, dims):
    return lax.dot_general(a, b, (dims, ((), ())), preferred_element_type=F32)


NN = ((1,), (0,))
NT = ((1,), (1,))
TN = ((0,), (0,))
TN_TK = 2048


class _Rider(NamedTuple):
    ins: tuple
    out_shapes: tuple
    n_sems: int
    phases: tuple


def _ride(rider, refs_in, refs_out, sems, step, total):
    for frac, fn in rider.phases:
        @pl.when(step == int(frac * (total - 1)))
        def _():
            fn(refs_in, refs_out, sems[0], sems[1])


def _mm(a, b, mode, *, tm, tn, tk, out_dtype, name, add=None, rider=None):
    if mode == "nn":
        (M, K), (_, N) = a.shape, b.shape
    elif mode == "nt":
        (M, K), (N, _) = a.shape, b.shape
    else:
        (K, M), (_, N) = a.shape, b.shape
    tm, tn, tk = min(tm, M), min(tn, N), min(tk, K)
    assert M % tm == 0 and N % tn == 0 and K % tk == 0, (name, M, N, K, tm, tn, tk)
    ni, nj, nk = M // tm, N // tn, K // tk
    dims = {"nn": NN, "nt": NT, "tn": TN}[mode]
    has_add = add is not None
    n_rin = len(rider.ins) if rider else 0
    n_rout = len(rider.out_shapes) if rider else 0

    def body(*refs):
        a_ref, b_ref = refs[0], refs[1]
        pos = 2
        add_ref = refs[pos] if has_add else None
        pos += int(has_add)
        rin = refs[pos:pos + n_rin]
        pos += n_rin
        o_ref = refs[pos]
        rout = refs[pos + 1:pos + 1 + n_rout]
        pos += 1 + n_rout
        acc = refs[pos] if nk > 1 else None
        sems = refs[-2:] if rider else None
        if rider:
            step = (pl.program_id(0) * ni + pl.program_id(1)) * nk + pl.program_id(2)
            _ride(rider._replace(phases=rider.phases[:1]), rin, rout, sems, step, nj * ni * nk)
        part = _dot(a_ref[...].astype(BF16), b_ref[...].astype(BF16), dims)

        def finish(r):
            if has_add:
                r = r + add_ref[...]
            o_ref[...] = r.astype(out_dtype)

        if nk == 1:
            finish(part)
        else:
            k = pl.program_id(2)

            @pl.when(k == 0)
            def _():
                acc[...] = part

            @pl.when(k > 0)
            def _():
                acc[...] += part

            @pl.when(k == nk - 1)
            def _():
                finish(acc[...])

        if rider:
            _ride(rider._replace(phases=rider.phases[1:]), rin, rout, sems, step, nj * ni * nk)

    if mode == "nn":
        a_spec = pl.BlockSpec((tm, tk), lambda j, i, k: (i, k))
        b_spec = pl.BlockSpec((tk, tn), lambda j, i, k: (k, j))
    elif mode == "nt":
        a_spec = pl.BlockSpec((tm, tk), lambda j, i, k: (i, k))
        b_spec = pl.BlockSpec((tn, tk), lambda j, i, k: (j, k))
    else:
        a_spec = pl.BlockSpec((tk, tm), lambda j, i, k: (k, i))
        b_spec = pl.BlockSpec((tk, tn), lambda j, i, k: (k, j))
    o_spec = pl.BlockSpec((tm, tn), lambda j, i, k: (i, j))
    hbm = pl.BlockSpec(memory_space=pl.ANY)
    in_specs = [a_spec, b_spec] + ([o_spec] if has_add else []) + [hbm] * n_rin
    args = (a, b) + ((add,) if has_add else ()) + (tuple(rider.ins) if rider else ())
    scratch = [pltpu.VMEM((tm, tn), F32)] if nk > 1 else []
    if rider:
        scratch += [pltpu.SemaphoreType.DMA((rider.n_sems,)), pltpu.SemaphoreType.DMA((rider.n_sems,))]
    res = pl.pallas_call(
        body, name=name, grid=(nj, ni, nk), in_specs=in_specs, out_specs=[o_spec] + [hbm] * n_rout,
        out_shape=[jax.ShapeDtypeStruct((M, N), out_dtype)] + (list(rider.out_shapes) if rider else []),
        scratch_shapes=scratch,
        compiler_params=_params(("arbitrary",) * 3 if rider else ("parallel", "parallel", "arbitrary")),
    )(*args)
    return (res[0], list(res[1:])) if rider else res[0]


def _rms_fwd(x, g, *, width, col, tm, name):
    rows = x.shape[0]
    tm = min(tm, rows)

    def body(x_ref, g_ref, y_ref):
        xv = x_ref[...].astype(F32)
        r = lax.rsqrt(jnp.mean(xv * xv, axis=-1, keepdims=True) + EPS)
        y_ref[...] = ((xv * r) * g_ref[...]).astype(BF16)

    return pl.pallas_call(
        body, name=name, grid=(rows // tm,),
        in_specs=[pl.BlockSpec((tm, width), lambda i: (i, col)), pl.BlockSpec((1, width), lambda i: (0, 0))],
        out_specs=pl.BlockSpec((tm, width), lambda i: (i, 0)),
        out_shape=jax.ShapeDtypeStruct((rows, width), BF16),
        compiler_params=_params(("parallel",)),
    )(x, g)


def _rms_bwd(d, x, g, *, width, col, tm, out_dtype, name, residual=None):
    rows = d.shape[0]
    tm = min(tm, rows)
    has_res = residual is not None

    def body(*refs):
        d_ref, x_ref, g_ref = refs[:3]
        res_ref = refs[3] if has_res else None
        dx_ref, gg_ref = refs[-2], refs[-1]
        dv = d_ref[...]
        xv = x_ref[...].astype(F32)
        r = lax.rsqrt(jnp.mean(xv * xv, axis=-1, keepdims=True) + EPS)
        n = xv * r

        @pl.when(pl.program_id(0) == 0)
        def _():
            gg_ref[...] = jnp.zeros_like(gg_ref)

        gg_ref[...] += jnp.sum(dv * n, axis=0, keepdims=True)
        gd = dv * g_ref[...]
        dx = r * (gd - n * jnp.mean(gd * n, axis=-1, keepdims=True))
        if has_res:
            dx = dx + res_ref[...]
        dx_ref[...] = dx.astype(out_dtype)

    blk = pl.BlockSpec((tm, width), lambda i: (i, 0))
    in_specs = [blk, pl.BlockSpec((tm, width), lambda i: (i, col)),
                pl.BlockSpec((1, width), lambda i: (0, 0))] + ([blk] if has_res else [])
    args = (d, x, g) + ((residual,) if has_res else ())
    return pl.pallas_call(
        body, name=name, grid=(rows // tm,), in_specs=in_specs,
        out_specs=[blk, pl.BlockSpec((1, width), lambda i: (0, 0))],
        out_shape=[jax.ShapeDtypeStruct((rows, width), out_dtype), jax.ShapeDtypeStruct((1, width), F32)],
        compiler_params=_params(("arbitrary",)),
    )(*args)


def _rope_tables(pos_col, inv_lane, *, tm):
    rows = pos_col.shape[0]
    tm = min(tm, rows)

    def body(p_ref, f_ref, c_ref, s1_ref, s2_ref):
        ang = p_ref[...].astype(F32) * f_ref[...]
        lane = lax.broadcasted_iota(jnp.int32, ang.shape, 1)
        c, s = jnp.cos(ang), jnp.sin(ang)
        half = QK_ROPE // 2
        c_ref[...] = jnp.where(lane < QK_ROPE, c, 0.0)
        s1_ref[...] = jnp.where(lane < half, -s, 0.0)
        s2_ref[...] = jnp.where((lane >= half) & (lane < QK_ROPE), s, 0.0)

    blk = pl.BlockSpec((tm, LANES), lambda i: (i, 0))
    return pl.pallas_call(
        body, name="rope_tables", grid=(rows // tm,),
        in_specs=[pl.BlockSpec((tm, 1), lambda i: (i, 0)), pl.BlockSpec((1, LANES), lambda i: (0, 0))],
        out_specs=[blk, blk, blk], out_shape=[jax.ShapeDtypeStruct((rows, LANES), F32)] * 3,
        compiler_params=_params(("parallel",)),
    )(pos_col, inv_lane)


def _rot(t, c, s1, s2, sign):
    r1 = pltpu.roll(t, LANES - QK_ROPE // 2, 1) * s1
    r2 = pltpu.roll(t, QK_ROPE // 2, 1) * s2
    return t * c + (r1 + r2) if sign > 0 else t * c - (r1 + r2)


def _mla_proj(cqn, ckvn, proj, tabs, wq, wk, wv, *, tm):
    rows = cqn.shape[0]
    tm = min(tm, rows)

    def body(cq_ref, ckv_ref, kr_ref, c_ref, s1_ref, s2_ref, wq_ref, wk_ref, wv_ref, q_ref, k_ref, v_ref):
        c, s1, s2 = c_ref[...], s1_ref[...], s2_ref[...]
        q = _dot(cq_ref[...], wq_ref[...], NN)
        k = _dot(ckv_ref[...], wk_ref[...], NN)
        kpe = _rot(kr_ref[...].astype(F32), c, s1, s2, 1).astype(BF16)
        for h in range(HEADS):
            lo = h * QK_PAD
            q_ref[:, lo:lo + QK_NOPE] = q[:, lo:lo + QK_NOPE].astype(BF16)
            q_ref[:, lo + QK_NOPE:lo + QK_PAD] = _rot(q[:, lo + QK_NOPE:lo + QK_PAD], c, s1, s2, 1).astype(BF16)
            k_ref[:, lo:lo + QK_NOPE] = k[:, lo:lo + QK_NOPE].astype(BF16)
            k_ref[:, lo + QK_NOPE:lo + QK_PAD] = kpe
        v_ref[...] = _dot(ckv_ref[...], wv_ref[...], NN).astype(BF16)

    def row(w):
        return pl.BlockSpec((tm, w), lambda i: (i, 0))

    def whole(w):
        return pl.BlockSpec(w.shape, lambda i: (0, 0))

    return pl.pallas_call(
        body, name="mla_proj", grid=(rows // tm,),
        in_specs=[row(LORA), row(LORA), pl.BlockSpec((tm, LANES), lambda i: (i, COL_KR)), row(LANES), row(LANES), row(LANES),
                  whole(wq), whole(wk), whole(wv)],
        out_specs=[row(HEADS * QK_PAD), row(HEADS * QK_PAD), row(HEADS * V_DIM)],
        out_shape=[jax.ShapeDtypeStruct((rows, HEADS * QK_PAD), BF16), jax.ShapeDtypeStruct((rows, HEADS * QK_PAD), BF16),
                   jax.ShapeDtypeStruct((rows, HEADS * V_DIM), BF16)],
        compiler_params=_params(("parallel",)),
    )(cqn, ckvn, proj, *tabs, wq, wk, wv)


def _mla_fwd(q, k, v, proj, *, t):
    S = q.shape[0]
    t = min(t, S // 2)
    n = S // t
    per = min(4, n)
    scale = QK_DIM ** -0.5

    def body(q_ref, k_ref, v_ref, z_ref, o_ref, y_ref, lse_ref):
        qi = pl.program_id(1)
        qv = q_ref[...]
        c2 = scale * LOG2E

        def block(k0, width, carry, row0):
            m_old, l_old, acc = carry
            ks = pl.ds(pl.multiple_of(k0, t), width)
            s = _dot(qv, k_ref[ks, :], NT)
            if row0 is not None:
                r = lax.broadcasted_iota(jnp.int32, s.shape, 0)
                c = lax.broadcasted_iota(jnp.int32, s.shape, 1)
                s = jnp.where(c <= r + row0, s, -1e30)
            m_new = jnp.maximum(m_old, jnp.max(s, axis=-1, keepdims=True))
            alpha = jnp.exp2((m_old - m_new) * c2)
            p = jnp.exp2((s - m_new) * c2)
            l_new = alpha * l_old + jnp.sum(p, axis=-1, keepdims=True)
            acc = alpha * acc + _dot(p.astype(BF16), v_ref[ks, :], NN)
            return m_new, l_new, acc

        init = (jnp.full((t, 1), -1e30, F32), jnp.zeros((t, 1), F32), jnp.zeros((t, V_DIM), F32))
        carry = lax.fori_loop(0, qi // per, lambda j, cr: block(j * (per * t), per * t, cr, None), init)
        last = [lambda cr, w=w: block((qi - w) * t, (w + 1) * t, cr, w * t) for w in range(per)]
        m_f, l_f, acc = lax.switch(qi % per, last, carry)
        o = acc / l_f
        o_ref[...] = o
        sz, _ = _silu_parts(z_ref[...].astype(F32))
        y_ref[...] = (o * sz).astype(BF16)
        lse2 = (m_f * scale + jnp.log(l_f)) * LOG2E
        lane = lax.broadcasted_iota(jnp.int32, (t, LANES), 1)
        lse_ref[0, 0] = jnp.where(lane == 0, lse2, 0.0).T[0:8, :]

    zcol = COL_ZB * (D // V_DIM)
    return pl.pallas_call(
        body, name="mla_fwd", grid=(HEADS, n),
        in_specs=[pl.BlockSpec((t, QK_PAD), lambda h, i: (i, h)),
                  pl.BlockSpec((S, QK_PAD), lambda h, i: (0, h)),
                  pl.BlockSpec((S, V_DIM), lambda h, i: (0, h)),
                  pl.BlockSpec((t, V_DIM), lambda h, i: (i, zcol + h))],
        out_specs=[pl.BlockSpec((t, V_DIM), lambda h, i: (i, h)), pl.BlockSpec((t, V_DIM), lambda h, i: (i, h)),
                   pl.BlockSpec((1, 1, 8, t), lambda h, i: (h, i, 0, 0))],
        out_shape=[jax.ShapeDtypeStruct((S, HEADS * V_DIM), F32), jax.ShapeDtypeStruct((S, HEADS * V_DIM), BF16),
                   jax.ShapeDtypeStruct((HEADS, n, 8, t), F32)],
        compiler_params=_params(("parallel", "parallel")),
    )(q, k, v, proj)


def _mla_gate_bwd(dy, o, proj, lse, *, tm):
    S = dy.shape[0]
    tm = min(tm, S)

    def body(dy_ref, o_ref, z_ref, lse_ref, do_ref, dz_ref, st_ref):
        sz, dsz = _silu_parts(z_ref[...].astype(F32))
        dyv, ov = dy_ref[...], o_ref[...]
        do = dyv * sz
        do_ref[...] = do.astype(BF16)
        dz_ref[...] = (dyv * ov * dsz).astype(BF16)
        prod = do * ov
        lane = lax.broadcasted_iota(jnp.int32, (tm, LANES), 1)
        for h in range(HEADS):
            delta = jnp.sum(prod[:, h * V_DIM:(h + 1) * V_DIM], axis=-1, keepdims=True)
            st_ref[h, 0] = lse_ref[h, 0] + jnp.where(lane == 1, delta, 0.0).T[0:8, :]

    blk = pl.BlockSpec((tm, D), lambda i: (i, 0))
    return pl.pallas_call(
        body, name="mla_gate_bwd", grid=(S // tm,),
        in_specs=[blk, blk, pl.BlockSpec((tm, D), lambda i: (i, COL_ZB)), pl.BlockSpec((HEADS, 1, 8, tm), lambda i: (0, i, 0, 0))],
        out_specs=[blk, blk, pl.BlockSpec((HEADS, 1, 8, tm), lambda i: (0, i, 0, 0))],
        out_shape=[jax.ShapeDtypeStruct((S, D), BF16), jax.ShapeDtypeStruct((S, D), BF16),
                   jax.ShapeDtypeStruct((HEADS, S // tm, 8, tm), F32)],
        compiler_params=_params(("parallel",)),
    )(dy, o, proj, lse)


def _mla_bwd(q, k, v, do, stats, *, t, rider):
    S = q.shape[0]
    t = min(t, S)
    n = S // t
    per = min(4, n)
    c2 = (QK_DIM ** -0.5) * LOG2E

    n_rin, n_rout = len(rider.ins), len(rider.out_shapes)

    def body(*refs):
        q_ref, k_ref, v_ref, do_ref, st_ref = refs[:5]
        rin = refs[5:5 + n_rin]
        dq_ref, dk_ref, dv_ref = refs[5 + n_rin:8 + n_rin]
        rout = refs[8 + n_rin:8 + n_rin + n_rout]
        ki = pl.program_id(1)
        step = pl.program_id(0) * n + ki
        _ride(rider._replace(phases=rider.phases[:1]), rin, rout, refs[-2:], step, HEADS * n)

        @pl.when(ki == 0)
        def _():
            dq_ref[...] = jnp.zeros_like(dq_ref)

        kv, vv = k_ref[...], v_ref[...]

        def block(i, carry, diag):
            dk, dv = carry
            rows = pl.ds(pl.multiple_of(i * t, t), t)
            qv, dov, st = q_ref[rows, :], do_ref[rows, :], st_ref[0, i]
            s = _dot(kv, qv, NT)
            if diag:
                key = lax.broadcasted_iota(jnp.int32, s.shape, 0)
                qry = lax.broadcasted_iota(jnp.int32, s.shape, 1)
                s = jnp.where(key <= qry, s, -1e30)
            p = jnp.exp2(s * c2 - st[0:1, :])
            p16 = p.astype(BF16)
            dv = dv + _dot(p16, dov, NN)
            dp = _dot(vv, dov, NT)
            ds = (p * (dp - st[1:2, :])).astype(BF16)
            dk = dk + _dot(ds, qv, NN)
            dq_ref[rows, :] += _dot(ds, kv, TN)
            return dk, dv

        carry = block(ki, (jnp.zeros((t, QK_PAD), F32), jnp.zeros((t, V_DIM), F32)), True)
        rest = n - 1 - ki

        def run(start, count):
            def f(cr):
                for u in range(count):
                    cr = block(start + u, cr, False)
                return cr
            return f

        carry = lax.switch(rest % per, [run(ki + 1, w) for w in range(per)], carry)
        first = ki + 1 + rest % per
        dk, dv = lax.fori_loop(0, rest // per, lambda i, cr: run(first + per * i, per)(cr), carry)
        dk_ref[...] = (dk * (QK_DIM ** -0.5)).astype(BF16)
        dv_ref[...] = dv.astype(BF16)
        _ride(rider._replace(phases=rider.phases[1:]), rin, rout, refs[-2:], step, HEADS * n)

    hbm = pl.BlockSpec(memory_space=pl.ANY)
    res = pl.pallas_call(
        body, name="mla_bwd", grid=(HEADS, n),
        in_specs=[pl.BlockSpec((S, QK_PAD), lambda h, j: (0, h)),
                  pl.BlockSpec((t, QK_PAD), lambda h, j: (j, h)),
                  pl.BlockSpec((t, V_DIM), lambda h, j: (j, h)),
                  pl.BlockSpec((S, V_DIM), lambda h, j: (0, h)),
                  pl.BlockSpec((1, n, 8, t), lambda h, j: (h, 0, 0, 0))] + [hbm] * n_rin,
        out_specs=[pl.BlockSpec((S, QK_PAD), lambda h, j: (0, h)),
                   pl.BlockSpec((t, QK_PAD), lambda h, j: (j, h)),
                   pl.BlockSpec((t, V_DIM), lambda h, j: (j, h))] + [hbm] * n_rout,
        out_shape=[jax.ShapeDtypeStruct((S, HEADS * QK_PAD), F32), jax.ShapeDtypeStruct((S, HEADS * QK_PAD), BF16),
                   jax.ShapeDtypeStruct((S, HEADS * V_DIM), BF16)] + list(rider.out_shapes),
        scratch_shapes=[pltpu.SemaphoreType.DMA((rider.n_sems,)), pltpu.SemaphoreType.DMA((rider.n_sems,))],
        compiler_params=_params(("arbitrary", "arbitrary")),
    )(q, k, v, do, stats, *rider.ins)
    return res[0], res[1], res[2], list(res[3:])


def _mla_qk_post(dq, dk, tabs, *, tm):
    S = dq.shape[0]
    tm = min(tm, S)
    scale = QK_DIM ** -0.5

    def body(dq_ref, dk_ref, c_ref, s1_ref, s2_ref, q16_ref, kr_ref):
        c, s1, s2 = c_ref[...], s1_ref[...], s2_ref[...]
        kpe = jnp.zeros((tm, LANES), F32)
        for h in range(HEADS):
            lo = h * QK_PAD
            q16_ref[:, lo:lo + QK_NOPE] = (dq_ref[:, lo:lo + QK_NOPE] * scale).astype(BF16)
            q16_ref[:, lo + QK_NOPE:lo + QK_PAD] = _rot(dq_ref[:, lo + QK_NOPE:lo + QK_PAD] * scale, c, s1, s2, -1).astype(BF16)
            kpe = kpe + dk_ref[:, lo + QK_NOPE:lo + QK_PAD].astype(F32)
        kr_ref[...] = _rot(kpe, c, s1, s2, -1).astype(BF16)

    wide = pl.BlockSpec((tm, HEADS * QK_PAD), lambda i: (i, 0))
    lane = pl.BlockSpec((tm, LANES), lambda i: (i, 0))
    return pl.pallas_call(
        body, name="mla_qk_post", grid=(S // tm,),
        in_specs=[wide, wide, lane, lane, lane], out_specs=[wide, lane],
        out_shape=[jax.ShapeDtypeStruct((S, HEADS * QK_PAD), BF16), jax.ShapeDtypeStruct((S, LANES), BF16)],
        compiler_params=_params(("parallel",)),
    )(dq, dk, *tabs)


def _mem_scores(q16, km_ref, h):
    lo = h * MEM_HEAD_DIM
    s = _dot(q16, km_ref[:, lo:lo + MEM_HEAD_DIM], NT) * (MEM_HEAD_DIM ** -0.5)
    e = jnp.exp(s - jnp.max(s, axis=-1, keepdims=True))
    return e / jnp.sum(e, axis=-1, keepdims=True)


def _mem_fwd(proj, kvm, *, tm):
    S = proj.shape[0]
    tm = min(tm, S)
    M = kvm.shape[0]

    def body(q_ref, z_ref, km_ref, vm_ref, y_ref):
        sz, _ = _silu_parts(z_ref[...].astype(F32))
        for h in range(MEM_HEADS):
            lo = h * MEM_HEAD_DIM
            p = _mem_scores(q_ref[:, lo:lo + MEM_HEAD_DIM].astype(BF16), km_ref, h)
            o = _dot(p.astype(BF16), vm_ref[:, lo:lo + MEM_HEAD_DIM], NN)
            y_ref[:, lo:lo + MEM_HEAD_DIM] = (o * sz[:, lo:lo + MEM_HEAD_DIM]).astype(BF16)

    return pl.pallas_call(
        body, name="mem_fwd", grid=(S // tm,),
        in_specs=[pl.BlockSpec((tm, D), lambda i: (i, COL_QM)), pl.BlockSpec((tm, D), lambda i: (i, COL_ZM)),
                  pl.BlockSpec((M, D), lambda i: (0, 0)), pl.BlockSpec((M, D), lambda i: (0, 1))],
        out_specs=pl.BlockSpec((tm, D), lambda i: (i, 0)),
        out_shape=jax.ShapeDtypeStruct((S, D), BF16),
        compiler_params=_params(("parallel",)),
    )(proj, proj, kvm, kvm)


def _mem_bwd(proj, kvm, dy, *, tm):
    S = proj.shape[0]
    tm = min(tm, S)
    M = kvm.shape[0]
    scale = MEM_HEAD_DIM ** -0.5

    def body(q_ref, z_ref, km_ref, vm_ref, dy_ref, dq_ref, dz_ref, dkv_ref):
        @pl.when(pl.program_id(0) == 0)
        def _():
            dkv_ref[...] = jnp.zeros_like(dkv_ref)

        sz, dsz = _silu_parts(z_ref[...].astype(F32))
        dyv = dy_ref[...]
        for h in range(MEM_HEADS):
            lo = h * MEM_HEAD_DIM
            sl = slice(lo, lo + MEM_HEAD_DIM)
            q16 = q_ref[:, sl].astype(BF16)
            p = _mem_scores(q16, km_ref, h)
            p16 = p.astype(BF16)
            o = _dot(p16, vm_ref[:, sl], NN)
            dy_h = dyv[:, sl]
            dz_ref[:, sl] = (dy_h * o * dsz[:, sl]).astype(BF16)
            do16 = (dy_h * sz[:, sl]).astype(BF16)
            dp = _dot(do16, vm_ref[:, sl], NT)
            ds = (p * (dp - jnp.sum(dp * p, axis=-1, keepdims=True)) * scale).astype(BF16)
            dq_ref[:, sl] = _dot(ds, km_ref[:, sl], NN).astype(BF16)
            dkv_ref[:, sl] += _dot(ds, q16, TN)
            dkv_ref[:, D + lo:D + lo + MEM_HEAD_DIM] += _dot(p16, do16, TN)

    blk = pl.BlockSpec((tm, D), lambda i: (i, 0))
    return pl.pallas_call(
        body, name="mem_bwd", grid=(S // tm,),
        in_specs=[pl.BlockSpec((tm, D), lambda i: (i, COL_QM)), pl.BlockSpec((tm, D), lambda i: (i, COL_ZM)),
                  pl.BlockSpec((M, D), lambda i: (0, 0)), pl.BlockSpec((M, D), lambda i: (0, 1)), blk],
        out_specs=[blk, blk, pl.BlockSpec((M, 2 * D), lambda i: (0, 0))],
        out_shape=[jax.ShapeDtypeStruct((S, D), BF16), jax.ShapeDtypeStruct((S, D), BF16),
                   jax.ShapeDtypeStruct((M, 2 * D), F32)],
        compiler_params=_params(("arbitrary",)),
    )(proj, proj, kvm, kvm, dy)


def _gmlp_common(u_ref, v_ref, lng_ref, lnb_ref):
    u, du = _gelu_parts(u_ref[...].astype(F32))
    vg, dvg = _gelu_parts(v_ref[...].astype(F32))
    mu = jnp.mean(vg, axis=-1, keepdims=True)
    vc = vg - mu
    r = lax.rsqrt(jnp.mean(vc * vc, axis=-1, keepdims=True) + EPS)
    vhat = vc * r
    vn = vhat * lng_ref[...] + lnb_ref[...]
    return u, du, dvg, r, vhat, vn.astype(BF16)


def _gmlp_fwd(proj, ln_g, ln_b, wm, bs_t):
    S = proj.shape[0]

    def body(u_ref, v_ref, z_ref, lng_ref, lnb_ref, wm_ref, bs_ref, y_ref):
        u, _, _, _, _, v16 = _gmlp_common(u_ref, v_ref, lng_ref, lnb_ref)
        sz, _ = _silu_parts(z_ref[...].astype(F32))
        for g in range(A_GROUPS):
            sl = slice(g * CHUNK, (g + 1) * CHUNK)
            sv = _dot(wm_ref[g], v16[:, sl], NN) + bs_ref[:, g:g + 1]
            y_ref[:, sl] = (u[:, sl] * sv * sz[:, sl]).astype(BF16)

    def col(c):
        return pl.BlockSpec((CHUNK, D), lambda i: (i, c))

    vec = pl.BlockSpec((1, D), lambda i: (0, 0))
    return pl.pallas_call(
        body, name="gmlp_fwd", grid=(S // CHUNK,),
        in_specs=[col(COL_U), col(COL_V), col(COL_ZA), vec, vec,
                  pl.BlockSpec((A_GROUPS, CHUNK, CHUNK), lambda i: (0, 0, 0)), pl.BlockSpec((CHUNK, A_GROUPS), lambda i: (0, 0))],
        out_specs=col(0), out_shape=jax.ShapeDtypeStruct((S, D), BF16),
        compiler_params=_params(("parallel",)),
    )(proj, proj, proj, ln_g, ln_b, wm, bs_t)


def _gmlp_bwd(proj, dy, ln_g, ln_b, wm, bs_t):
    S = proj.shape[0]

    def body(u_ref, v_ref, z_ref, dy_ref, lng_ref, lnb_ref, wm_ref, bs_ref,
             du_ref, dv_ref, dz_ref, gws_ref, dsv_ref, glg_ref, glb_ref, dvn_s):
        @pl.when(pl.program_id(0) == 0)
        def _():
            gws_ref[...] = jnp.zeros_like(gws_ref)
            dsv_ref[...] = jnp.zeros_like(dsv_ref)
            glg_ref[...] = jnp.zeros_like(glg_ref)
            glb_ref[...] = jnp.zeros_like(glb_ref)

        u, du, dvg, r, vhat, v16 = _gmlp_common(u_ref, v_ref, lng_ref, lnb_ref)
        sz, dsz = _silu_parts(z_ref[...].astype(F32))
        dyv = dy_ref[...]
        for g in range(A_GROUPS):
            sl = slice(g * CHUNK, (g + 1) * CHUNK)
            sv = _dot(wm_ref[g], v16[:, sl], NN) + bs_ref[:, g:g + 1]
            dy_g, u_g, sz_g = dyv[:, sl], u[:, sl], sz[:, sl]
            dsv = dy_g * u_g * sz_g
            du_ref[:, sl] = (dy_g * sv * sz_g * du[:, sl]).astype(BF16)
            dz_ref[:, sl] = (dy_g * u_g * sv * dsz[:, sl]).astype(BF16)
            dsv16 = dsv.astype(BF16)
            dvn_s[:, sl] = _dot(wm_ref[g], dsv16, TN)
            gws_ref[g] += _dot(dsv16, v16[:, sl], NT)
            dsv_ref[:, sl] += dsv
        dvn = dvn_s[...]
        glb_ref[...] += jnp.sum(dvn, axis=0, keepdims=True)
        glg_ref[...] += jnp.sum(dvn * vhat, axis=0, keepdims=True)
        dvh = dvn * lng_ref[...]
        dvc = r * (dvh - jnp.mean(dvh, axis=-1, keepdims=True) - vhat * jnp.mean(dvh * vhat, axis=-1, keepdims=True))
        dv_ref[...] = (dvc * dvg).astype(BF16)

    def col(c):
        return pl.BlockSpec((CHUNK, D), lambda i: (i, c))

    vec = pl.BlockSpec((1, D), lambda i: (0, 0))
    wsp = pl.BlockSpec((A_GROUPS, CHUNK, CHUNK), lambda i: (0, 0, 0))
    return pl.pallas_call(
        body, name="gmlp_bwd", grid=(S // CHUNK,),
        in_specs=[col(COL_U), col(COL_V), col(COL_ZA), col(0), vec, vec, wsp, pl.BlockSpec((CHUNK, A_GROUPS), lambda i: (0, 0))],
        out_specs=[col(0), col(0), col(0), wsp, pl.BlockSpec((CHUNK, D), lambda i: (0, 0)), vec, vec],
        out_shape=[jax.ShapeDtypeStruct((S, D), BF16)] * 3 + [
            jax.ShapeDtypeStruct((A_GROUPS, CHUNK, CHUNK), F32), jax.ShapeDtypeStruct((CHUNK, D), F32),
            jax.ShapeDtypeStruct((1, D), F32), jax.ShapeDtypeStruct((1, D), F32)],
        scratch_shapes=[pltpu.VMEM((CHUNK, D), F32)],
        compiler_params=_params(("arbitrary",)),
    )(proj, proj, proj, dy, ln_g, ln_b, wm, bs_t)


def _gate_merge(h16, ys, wg, bg, wbs, *, tm, tn):
    S = h16.shape[0]
    tm = min(tm, S)
    nj = D // tn

    def body(h_ref, ya_ref, yb_ref, ym_ref, wg0, wg1, wg2, bg0, bg1, bg2, wb0, wb1, wb2,
             mg_ref, g0_ref, g1_ref, g2_ref, p0_ref, p1_ref, p2_ref):
        hv = h_ref[...]
        acc = None
        for y_ref, wg_ref, bgr, wb_ref, g_ref, p_ref in ((ya_ref, wg0, bg0, wb0, g0_ref, p0_ref),
                                                         (yb_ref, wg1, bg1, wb1, g1_ref, p1_ref),
                                                         (ym_ref, wg2, bg2, wb2, g2_ref, p2_ref)):
            gate = _sigmoid(_dot(hv, wg_ref[...], NN) + bgr[...])
            p = _dot(y_ref[...], wb_ref[...], NN)
            g_ref[...] = gate.astype(BF16)
            p_ref[...] = p.astype(BF16)
            acc = gate * p if acc is None else acc + gate * p
        mg_ref[...] = acc.astype(BF16)

    a_spec = pl.BlockSpec((tm, D), lambda j, i: (i, 0))
    o_spec = pl.BlockSpec((tm, tn), lambda j, i: (i, j))

    def wgs(n):
        return pl.BlockSpec((D, tn), lambda j, i: (0, n * nj + j))

    def bgs(n):
        return pl.BlockSpec((1, tn), lambda j, i: (0, n * nj + j))

    wbsp = pl.BlockSpec((D, tn), lambda j, i: (0, j))
    return pl.pallas_call(
        body, name="gate_merge", grid=(nj, S // tm),
        in_specs=[a_spec] * 4 + [wgs(0), wgs(1), wgs(2), bgs(0), bgs(1), bgs(2), wbsp, wbsp, wbsp],
        out_specs=[o_spec] * 7, out_shape=[jax.ShapeDtypeStruct((S, D), BF16)] * 7,
        compiler_params=_params(("parallel", "parallel")),
    )(h16, *ys, wg, wg, wg, bg, bg, bg, *wbs)


def _gate_bwd(dmerged, gates, ps, *, tm):
    S = dmerged.shape[0]
    tm = min(tm, S)

    def body(dm_ref, g0, g1, g2, p0, p1, p2, dp0, dp1, dp2, dg_ref, gb_ref):
        @pl.when(pl.program_id(0) == 0)
        def _():
            gb_ref[...] = jnp.zeros_like(gb_ref)

        dm = dm_ref[...]
        for n, (g_ref, p_ref, dp_ref) in enumerate(((g0, p0, dp0), (g1, p1, dp1), (g2, p2, dp2))):
            gate = g_ref[...].astype(F32)
            dp_ref[...] = (dm * gate).astype(BF16)
            dg = dm * p_ref[...].astype(F32) * gate * (1.0 - gate)
            dg_ref[:, n * D:(n + 1) * D] = dg.astype(BF16)
            gb_ref[:, n * D:(n + 1) * D] += jnp.sum(dg, axis=0, keepdims=True)

    blk = pl.BlockSpec((tm, D), lambda i: (i, 0))
    return pl.pallas_call(
        body, name="gate_bwd", grid=(S // tm,),
        in_specs=[blk] * 7,
        out_specs=[blk, blk, blk, pl.BlockSpec((tm, 3 * D), lambda i: (i, 0)), pl.BlockSpec((1, 3 * D), lambda i: (0, 0))],
        out_shape=[jax.ShapeDtypeStruct((S, D), BF16)] * 3 + [jax.ShapeDtypeStruct((S, 3 * D), BF16),
                                                              jax.ShapeDtypeStruct((1, 3 * D), F32)],
        compiler_params=_params(("arbitrary",)),
    )(dmerged, *gates, *ps)


def _post_loss(x, out, target, g_post, *, tm):
    S = x.shape[0]
    tm = min(tm, S)

    def body(x_ref, o_ref, t_ref, g_ref, dy_ref, do_ref, gg_ref, ls_ref):
        @pl.when(pl.program_id(0) == 0)
        def _():
            gg_ref[...] = jnp.zeros_like(gg_ref)
            ls_ref[...] = jnp.zeros_like(ls_ref)

        ov = o_ref[...]
        r = lax.rsqrt(jnp.mean(ov * ov, axis=-1, keepdims=True) + EPS)
        n = ov * r
        err = (x_ref[...] + n * g_ref[...]) - t_ref[...]
        ls_ref[...] += 0.5 * jnp.sum(jnp.mean(err * err, axis=-1, keepdims=True))
        dy = err * (1.0 / D)
        dy_ref[...] = dy
        gg_ref[...] += jnp.sum(dy * n, axis=0, keepdims=True)
        gd = dy * g_ref[...]
        do_ref[...] = (r * (gd - n * jnp.mean(gd * n, axis=-1, keepdims=True))).astype(BF16)

    blk = pl.BlockSpec((tm, D), lambda i: (i, 0))
    vec = pl.BlockSpec((1, D), lambda i: (0, 0))
    return pl.pallas_call(
        body, name="post_loss", grid=(S // tm,),
        in_specs=[blk, blk, blk, vec],
        out_specs=[blk, blk, vec, pl.BlockSpec((1, LANES), lambda i: (0, 0))],
        out_shape=[jax.ShapeDtypeStruct((S, D), F32), jax.ShapeDtypeStruct((S, D), BF16),
                   jax.ShapeDtypeStruct((1, D), F32), jax.ShapeDtypeStruct((1, LANES), F32)],
        compiler_params=_params(("arbitrary",)),
    )(x, out, target, g_post)


def _adamw(w, g_own, g_other, m, v, cidx, *, name):
    rows, width = w.shape
    hh = rows // 2
    tr = _row_tile(hh, width, unit=8)
    nb = hh // tr

    def body(c_ref, w_ref, own_ref, oth_ref, m_ref, v_ref, g_ref, d_ref, nm_ref, nv_ref):
        mine = (pl.program_id(0) // nb) == c_ref[0]
        gv = jnp.where(mine, own_ref[...], oth_ref[...])
        delta, nm, nv = _adam_math(w_ref[...], gv, m_ref[...], v_ref[...])
        g_ref[...] = gv
        d_ref[...] = delta
        nm_ref[...] = nm
        nv_ref[...] = nv

    blk = pl.BlockSpec((tr, width), lambda i, c_ref: (i, 0))
    half = pl.BlockSpec((tr, width), lambda i, c_ref: (i % nb, 0))
    return pl.pallas_call(
        body, name=name,
        grid_spec=pltpu.PrefetchScalarGridSpec(num_scalar_prefetch=1, grid=(rows // tr,),
                                               in_specs=[blk, half, half, blk, blk], out_specs=[blk] * 4),
        out_shape=[jax.ShapeDtypeStruct((rows, width), F32)] * 4,
        compiler_params=_params(("parallel",)),
    )(cidx, w, g_own, g_other, m, v)


MESH = pl.DeviceIdType.MESH
ANY = pl.BlockSpec(memory_space=pl.ANY)


def _place():
    return lax.axis_index("x"), lax.axis_index("y"), lax.axis_index("c")


def _other_chips(x, y):
    return [(1 - x, y), (x, 1 - y), (1 - x, 1 - y)]


def _remote(src, dst, send_sem, recv_sem, dev):
    return pltpu.make_async_remote_copy(src_ref=src, dst_ref=dst, send_sem=send_sem, recv_sem=recv_sem,
                                        device_id=dev, device_id_type=MESH)


def _allgather_chips(shards):
    nw = len(shards)

    def body(*refs):
        x_refs, out_refs = refs[:nw], refs[nw:2 * nw]
        send_sems, recv_sems = refs[2 * nw:]
        x, y, c = _place()
        sibling = (x, y, 1 - c)
        chips = _other_chips(x, y)

        def half(w, px, py, hc):
            hh = shards[w].shape[0] // 2
            return out_refs[w].at[2 * px + py, pl.ds(hc * hh, hh), :]

        sent = []
        for w in range(nw):
            hh = shards[w].shape[0] // 2
            for k, (px, py) in enumerate(chips):
                cp = _remote(x_refs[w].at[pl.ds(c * hh, hh), :], half(w, x, y, c), send_sems.at[6 * w + k],
                             recv_sems.at[6 * w + k], (px, py, c))
                cp.start()
                sent.append(cp)
        for w in range(nw):
            for k, (px, py) in enumerate(chips):
                landed = half(w, px, py, c)
                _remote(landed, landed, send_sems.at[6 * w + k], recv_sems.at[6 * w + k], (px, py, c)).wait_recv()
                cp = _remote(landed, landed, send_sems.at[6 * w + 3 + k], recv_sems.at[6 * w + 3 + k], sibling)
                cp.start()
                sent.append(cp)
        for w in range(nw):
            for k, (px, py) in enumerate(chips):
                other = half(w, px, py, 1 - c)
                _remote(other, other, send_sems.at[6 * w + 3 + k], recv_sems.at[6 * w + 3 + k], sibling).wait_recv()
        for cp in sent:
            cp.wait_send()

    outs = pl.pallas_call(
        body, name="allgather_weights", in_specs=[ANY] * nw, out_specs=[ANY] * nw,
        out_shape=[jax.ShapeDtypeStruct((N_CHIPS,) + s.shape, s.dtype) for s in shards],
        scratch_shapes=[pltpu.SemaphoreType.DMA((6 * nw,)), pltpu.SemaphoreType.DMA((6 * nw,))],
    )(*shards)
    own = 2 * lax.axis_index("x") + lax.axis_index("y")
    return [lax.dynamic_update_slice(o, s[None], (own, 0, 0)) for o, s in zip(outs, shards)]


def _row_tile(rows, cols, unit=16, budget=2 * 1024 * 1024):
    best = unit
    for t in range(unit, rows + 1, unit):
        if rows % t == 0 and t * cols * 4 <= budget:
            best = t
    assert rows % best == 0, (rows, cols)
    return best


def _peers(x, y, c):
    out = []
    for k in range(1, N_DEV):
        out.append((k, (1 - x if (k >> 2) & 1 else x, 1 - y if (k >> 1) & 1 else y, 1 - c if k & 1 else c)))
    return out


def _gather_rider(shards):
    nw = len(shards)

    def half(outs, w, px, py, hc):
        hh = shards[w].shape[0] // 2
        return outs[w].at[2 * px + py, pl.ds(hc * hh, hh), :]

    def ici(ins, outs, ss, rs, w, k, px, py, c, x, y):
        hh = shards[w].shape[0] // 2
        return _remote(ins[w].at[pl.ds(c * hh, hh), :], half(outs, w, x, y, c), ss.at[6 * w + k], rs.at[6 * w + k], (px, py, c))

    def passing(outs, ss, rs, w, k, px, py, hc, sibling):
        landed = half(outs, w, px, py, hc)
        return _remote(landed, landed, ss.at[6 * w + 3 + k], rs.at[6 * w + 3 + k], sibling)

    def start(ins, outs, ss, rs):
        x, y, c = _place()
        for w in range(nw):
            for k, (px, py) in enumerate(_other_chips(x, y)):
                ici(ins, outs, ss, rs, w, k, px, py, c, x, y).start()

    def forward(ins, outs, ss, rs):
        x, y, c = _place()
        for w in range(nw):
            for k, (px, py) in enumerate(_other_chips(x, y)):
                landed = half(outs, w, px, py, c)
                _remote(landed, landed, ss.at[6 * w + k], rs.at[6 * w + k], (px, py, c)).wait_recv()
                passing(outs, ss, rs, w, k, px, py, c, (x, y, 1 - c)).start()

    def finish(ins, outs, ss, rs):
        x, y, c = _place()
        for w in range(nw):
            for k, (px, py) in enumerate(_other_chips(x, y)):
                passing(outs, ss, rs, w, k, px, py, 1 - c, (x, y, 1 - c)).wait_recv()
        for w in range(nw):
            for k, (px, py) in enumerate(_other_chips(x, y)):
                ici(ins, outs, ss, rs, w, k, px, py, c, x, y).wait_send()
                passing(outs, ss, rs, w, k, px, py, c, (x, y, 1 - c)).wait_send()

    return _Rider(ins=tuple(shards), out_shapes=tuple(jax.ShapeDtypeStruct((N_CHIPS,) + s.shape, s.dtype) for s in shards),
                  n_sems=6 * nw, phases=((0.0, start), (0.8, forward), (1.0, finish)))


def _own_blocks_in_place(gathered, shards):
    own = 2 * lax.axis_index("x") + lax.axis_index("y")
    return [lax.dynamic_update_slice(o, s[None], (own, 0, 0)) for o, s in zip(gathered, shards)]


def _exchange_rider(blocks):
    nw = len(blocks)

    def copy(ins, outs, ss, rs, w, k, peer):
        hh = blocks[w].shape[1] // 2
        px, py, pc = peer
        return _remote(ins[w].at[2 * px + py, pl.ds(pc * hh, hh), :], outs[w].at[k - 1], ss.at[7 * w + k - 1], rs.at[7 * w + k - 1], peer)

    def start(ins, outs, ss, rs):
        for w in range(nw):
            for k, peer in _peers(*_place()):
                copy(ins, outs, ss, rs, w, k, peer).start()

    def finish(ins, outs, ss, rs):
        for w in range(nw):
            for k, peer in _peers(*_place()):
                copy(ins, outs, ss, rs, w, k, peer).wait()

    return _Rider(ins=tuple(blocks),
                  out_shapes=tuple(jax.ShapeDtypeStruct((N_DEV - 1, b.shape[1] // 2, b.shape[2]), b.dtype) for b in blocks),
                  n_sems=7 * nw, phases=((0.0, start), (1.0, finish)))


def _reduce_add(own, recv, cidx, *, name):
    R, W = own.shape
    hh = R // 2
    tr = _row_tile(hh, W, budget=1024 * 1024)
    nb = hh // tr

    def body(c_ref, o_ref, r_ref, t_ref):
        s = o_ref[...]
        for k in range(N_DEV - 1):
            s = s + r_ref[k].astype(F32)
        t_ref[...] = s

    return pl.pallas_call(
        body, name=name,
        grid_spec=pltpu.PrefetchScalarGridSpec(
            num_scalar_prefetch=1, grid=(nb,),
            in_specs=[pl.BlockSpec((tr, W), lambda i, c_ref: (i + c_ref[0] * nb, 0)),
                      pl.BlockSpec((N_DEV - 1, tr, W), lambda i, c_ref: (0, i, 0))],
            out_specs=pl.BlockSpec((tr, W), lambda i, c_ref: (i, 0))),
        out_shape=jax.ShapeDtypeStruct((hh, W), F32),
        compiler_params=_params(("parallel",)),
    )(cidx, own, recv)


def _halves_exchange(ts):
    nw = len(ts)

    def body(*refs):
        t_refs, out_refs = refs[:nw], refs[nw:2 * nw]
        send_sems, recv_sems = refs[2 * nw:]
        x, y, c = _place()
        cps = []
        for w in range(nw):
            cp = _remote(t_refs[w], out_refs[w], send_sems.at[w], recv_sems.at[w], (x, y, 1 - c))
            cp.start()
            cps.append(cp)
        for cp in cps:
            cp.wait()

    return pl.pallas_call(
        body, name="grad_halves_exchange", in_specs=[ANY] * nw, out_specs=[ANY] * nw,
        out_shape=[jax.ShapeDtypeStruct(t.shape, t.dtype) for t in ts],
        scratch_shapes=[pltpu.SemaphoreType.DMA((nw,)), pltpu.SemaphoreType.DMA((nw,))],
    )(*ts)


def _adam_math(w, g, m, v):
    nm = ADAM_B1 * m + (1.0 - ADAM_B1) * g
    nv = ADAM_B2 * v + (1.0 - ADAM_B2) * (g * g)
    c1 = 1.0 - ADAM_B1 ** ADAM_STEP
    c2 = 1.0 - ADAM_B2 ** ADAM_STEP
    return -ADAM_LR * ((nm / c1) / (jnp.sqrt(nv / c2) + ADAM_EPS) + ADAM_WD * w), nm, nv


STAGE_ROWS = 32
STAGE_VEC = {"g_pre": 0, "a_ln_g": 1, "a_ln_b": 2, "mem_norm_g": 3, "g_post": 4}
STAGE_BGATE = 5
STAGE_MIX = 8
STAGE_ABS = 16


def _small_step(g, loss_row, w, m, v):
    n = len(SMALL)

    def reduce_body(*refs):
        g_r = dict(zip(SMALL, refs[:n]))
        loss_r = refs[n]
        sa_o, sw_o = refs[n + 1], refs[n + 2]
        stage, ga, gw, send_sems, recv_sems = refs[n + 3:]

        stage[...] = jnp.zeros_like(stage)
        for name, row in STAGE_VEC.items():
            stage[row:row + 1, :] = g_r[name][...]
        for t in range(3):
            stage[STAGE_BGATE + t:STAGE_BGATE + t + 1, :] = g_r["b_gate"][:, t * D:(t + 1) * D]
        stage[STAGE_MIX:STAGE_MIX + 1, 0:LORA] = g_r["q_norm_g"][...]
        stage[STAGE_MIX:STAGE_MIX + 1, LORA:2 * LORA] = g_r["kv_norm_g"][...]
        stage[STAGE_MIX:STAGE_MIX + 1, 2 * LORA:2 * LORA + LANES] = loss_r[...]
        stage[STAGE_ABS:STAGE_ABS + A_GROUPS, 0:CHUNK] = g_r["a_b_s"][...]

        x, y, c = _place()
        me = 4 * x + 2 * y + c
        ga[me] = stage[...]
        gw[me] = g_r["a_w_s"][...]
        cps, srcs = [], []
        for k in range(1, N_DEV):
            fx, fy, fc = (k >> 2) & 1, (k >> 1) & 1, k & 1
            peer = (1 - x if fx else x, 1 - y if fy else y, 1 - c if fc else c)
            for j, (src, dst) in enumerate(((stage, ga), (g_r["a_w_s"], gw))):
                cp = _remote(src, dst.at[me], send_sems.at[2 * (k - 1) + j], recv_sems.at[2 * (k - 1) + j], peer)
                cp.start()
                cps.append(cp)
            srcs.append(4 * peer[0] + 2 * peer[1] + peer[2])
        for k, src in enumerate(srcs):
            _remote(stage, ga.at[src], send_sems.at[2 * k], recv_sems.at[2 * k], (x, y, c)).wait_recv()
            _remote(g_r["a_w_s"], gw.at[src], send_sems.at[2 * k + 1], recv_sems.at[2 * k + 1], (x, y, c)).wait_recv()
        for cp in cps:
            cp.wait_send()
        sa, sw = ga[0], gw[0]
        for d in range(1, N_DEV):
            sa = sa + ga[d]
            sw = sw + gw[d]
        sa_o[...] = sa
        sw_o[...] = sw

    def update_body(*refs):
        sa, sw = refs[0][...], refs[1][...]
        w_r = dict(zip(SMALL, refs[2:n + 2]))
        m_r = dict(zip(SMALL, refs[n + 2:2 * n + 2]))
        v_r = dict(zip(SMALL, refs[2 * n + 2:3 * n + 2]))
        outs = refs[3 * n + 2:7 * n + 2]
        o_r = {name: outs[4 * i:4 * i + 4] for i, name in enumerate(SMALL)}
        loss_o = refs[7 * n + 2]

        def update(name, gsum, cols=None):
            sel = (slice(None), cols) if cols is not None else Ellipsis
            delta, nm, nv = _adam_math(w_r[name][sel], gsum, m_r[name][sel], v_r[name][sel])
            for ref, val in zip(o_r[name], (gsum, delta, nm, nv)):
                ref[sel] = val

        for name, row in STAGE_VEC.items():
            update(name, sa[row:row + 1, :])
        for t in range(3):
            update("b_gate", sa[STAGE_BGATE + t:STAGE_BGATE + t + 1, :], slice(t * D, (t + 1) * D))
        update("q_norm_g", sa[STAGE_MIX:STAGE_MIX + 1, 0:LORA])
        update("kv_norm_g", sa[STAGE_MIX:STAGE_MIX + 1, LORA:2 * LORA])
        update("a_b_s", sa[STAGE_ABS:STAGE_ABS + A_GROUPS, 0:CHUNK])
        update("a_w_s", sw)
        loss_o[...] = sa[STAGE_MIX:STAGE_MIX + 1, 2 * LORA:2 * LORA + LANES]

    vm = pl.BlockSpec(memory_space=pltpu.VMEM)
    sa, sw = pl.pallas_call(
        reduce_body, name="small_allreduce", in_specs=[vm] * (n + 1), out_specs=[vm, vm],
        out_shape=[jax.ShapeDtypeStruct((STAGE_ROWS, D), F32), jax.ShapeDtypeStruct((A_GROUPS, CHUNK, CHUNK), F32)],
        scratch_shapes=[pltpu.VMEM((STAGE_ROWS, D), F32), pltpu.VMEM((N_DEV, STAGE_ROWS, D), F32),
                        pltpu.VMEM((N_DEV, A_GROUPS, CHUNK, CHUNK), F32),
                        pltpu.SemaphoreType.DMA((2 * (N_DEV - 1),)), pltpu.SemaphoreType.DMA((2 * (N_DEV - 1),))],
        compiler_params=pltpu.CompilerParams(vmem_limit_bytes=VMEM_LIMIT),
    )(*[g[k] for k in SMALL], loss_row)
    ins = [sa, sw] + [w[k] for k in SMALL] + [m[k] for k in SMALL] + [v[k] for k in SMALL]
    out_shape = [jax.ShapeDtypeStruct(w[k].shape, F32) for k in SMALL for _ in range(4)] + [jax.ShapeDtypeStruct((1, LANES), F32)]
    res = pl.pallas_call(
        update_body, name="small_adamw", in_specs=[vm] * len(ins), out_specs=[vm] * len(out_shape), out_shape=out_shape,
        compiler_params=pltpu.CompilerParams(vmem_limit_bytes=VMEM_LIMIT),
    )(*ins)
    return {k: tuple(res[4 * i:4 * i + 4]) for i, k in enumerate(SMALL)}, res[-1]


SHARD_2D = {"w_in": (D, IN_REF // N_CHIPS), "w_uq": (LORA, HEADS * QK_DIM // N_CHIPS),
            "w_ukv": (LORA, HEADS * (QK_NOPE + V_DIM) // N_CHIPS), "w_mem_kv": (D, 2 * D // N_CHIPS),
            "w_gate": (D, 3 * D // N_CHIPS), "w_branch": (3 * D // N_CHIPS, D), "w_out": (D // N_CHIPS, D)}


def _cols(blocks):
    return jnp.concatenate([blocks[j] for j in range(N_CHIPS)], axis=1)


def _w_in_layout(gathered):
    w = _cols(gathered)
    return jnp.concatenate([w[:, :3 * D], w[:, 3 * D + 2 * LORA + QK_ROPE:], w[:, 3 * D:3 * D + 2 * LORA + QK_ROPE],
                            jnp.zeros((D, IN_PAD - IN_REF), w.dtype)], axis=1)


REST = BIG[1:]


def _rest_layouts(gathered):
    wq = jnp.pad(_cols(gathered["w_uq"]).reshape(LORA, HEADS, QK_DIM), ((0, 0), (0, 0), (0, QK_PAD - QK_DIM))).reshape(LORA, HEADS * QK_PAD)
    kv3 = _cols(gathered["w_ukv"]).reshape(LORA, HEADS, QK_NOPE + V_DIM)
    wk = jnp.pad(kv3[:, :, :QK_NOPE], ((0, 0), (0, 0), (0, QK_PAD - QK_NOPE))).reshape(LORA, HEADS * QK_PAD)
    wv = kv3[:, :, QK_NOPE:].reshape(LORA, HEADS * V_DIM)
    w_branch = gathered["w_branch"].reshape(N_CHIPS, 3, D // N_CHIPS, D).transpose(1, 0, 2, 3).reshape(3, D, D)
    return {"wq": wq, "wk": wk, "wv": wv, "w_mem_kv": _cols(gathered["w_mem_kv"]), "w_gate": _cols(gathered["w_gate"]),
            "w_branch": w_branch, "w_out": gathered["w_out"].reshape(D, D)}


def _grad_reference_layout(name, g):
    if name == "w_in":
        return jnp.concatenate([g[:, :3 * D], g[:, 6 * D:6 * D + 2 * LORA + QK_ROPE], g[:, 3 * D:6 * D]], axis=1)
    if name == "w_uq":
        return g.reshape(LORA, HEADS, QK_PAD)[:, :, :QK_DIM].reshape(LORA, HEADS * QK_DIM)
    if name == "w_ukv":
        gk, gv = g
        return jnp.concatenate([gk.reshape(LORA, HEADS, QK_PAD)[:, :, :QK_NOPE], gv.reshape(LORA, HEADS, V_DIM)],
                               axis=2).reshape(LORA, HEADS * (QK_NOPE + V_DIM))
    return g


def _grad_blocks(name, full):
    own = 2 * lax.axis_index("x") + lax.axis_index("y")
    R, C = SHARD_2D[name]
    if name == "w_branch":
        blocks = full.reshape(3, N_CHIPS, D // N_CHIPS, D).transpose(1, 0, 2, 3).reshape(N_CHIPS, R, C)
        mine = lax.dynamic_slice_in_dim(full, own * (D // N_CHIPS), D // N_CHIPS, axis=1).reshape(R, C)
    elif name == "w_out":
        blocks = full.reshape(N_CHIPS, R, C)
        mine = lax.dynamic_slice_in_dim(full, own * R, R, axis=0)
    else:
        blocks = full.reshape(R, N_CHIPS, C).transpose(1, 0, 2)
        mine = lax.dynamic_slice_in_dim(full, own * C, C, axis=1)
    return blocks.astype(BF16), mine


def _local_step(x, mem, pos_col, target, w_in, rest_shards, P):
    cidx = lax.axis_index("c").astype(jnp.int32).reshape(1)
    h16 = _rms_fwd(x, P["g_pre"], width=D, col=0, tm=256, name="pre_norm")
    memn16 = _rms_fwd(mem, P["mem_norm_g"], width=D, col=0, tm=256, name="mem_norm")
    proj, rest = _mm(h16, w_in, "nn", tm=512, tn=1920, tk=D, out_dtype=BF16, name="in_proj", rider=_gather_rider(rest_shards))
    W = _rest_layouts(dict(zip(REST, _own_blocks_in_place(rest, rest_shards))))

    causal = jnp.tril(jnp.ones((CHUNK, CHUNK), F32))
    wm = (P["a_w_s"] * causal[None]).astype(BF16)
    bs_t = P["a_b_s"].T
    ya = _gmlp_fwd(proj, P["a_ln_g"], P["a_ln_b"], wm, bs_t)

    inv = 1.0 / (ROPE_THETA ** (jnp.arange(0, QK_ROPE, 2, dtype=F32) / QK_ROPE))
    inv_lane = jnp.concatenate([inv, inv, jnp.zeros((LANES - QK_ROPE,), F32)])[None]
    tabs = _rope_tables(pos_col, inv_lane, tm=1024)
    cqn = _rms_fwd(proj, P["q_norm_g"], width=LORA, col=COL_CQ, tm=512, name="q_norm")
    ckvn = _rms_fwd(proj, P["kv_norm_g"], width=LORA, col=COL_CKV, tm=512, name="kv_norm")
    q16, k16, v16 = _mla_proj(cqn, ckvn, proj, tabs, W["wq"], W["wk"], W["wv"], tm=256)
    o_b, yb, lse = _mla_fwd(q16, k16, v16, proj, t=512)

    kvm = _mm(memn16, W["w_mem_kv"], "nn", tm=256, tn=1024, tk=D, out_dtype=BF16, name="mem_kv")
    ym = _mem_fwd(proj, kvm, tm=512)

    wbs = [W["w_branch"][n] for n in range(3)]
    merged, g0, g1, g2, p0, p1, p2 = _gate_merge(h16, (ya, yb, ym), W["w_gate"], P["b_gate"], wbs, tm=512, tn=512)
    out = _mm(merged, W["w_out"], "nn", tm=512, tn=1024, tk=D, out_dtype=F32, name="out_proj")
    dy, dout, g_g_post, loss = _post_loss(x, out, target, P["g_post"], tm=256)

    full = {}
    full["w_out"] = _mm(merged, dout, "tn", tm=1024, tn=1024, tk=TN_TK, out_dtype=F32, name="gw_out")
    dmerged = _mm(dout, W["w_out"], "nt", tm=512, tn=1024, tk=D, out_dtype=F32, name="d_merged")
    dp0, dp1, dp2, dgpre, g_b_gate = _gate_bwd(dmerged, (g0, g1, g2), (p0, p1, p2), tm=256)
    full["w_gate"] = _mm(h16, dgpre, "tn", tm=1024, tn=1024, tk=TN_TK, out_dtype=F32, name="gw_gate")
    dh_gate = _mm(dgpre, W["w_gate"], "nt", tm=1024, tn=1024, tk=3 * D // 2, out_dtype=F32, name="dh_gate")
    full["w_branch"] = jnp.stack([_mm(y, dp, "tn", tm=1024, tn=1024, tk=TN_TK, out_dtype=F32, name=f"gw_branch{n}")
                                  for n, (y, dp) in enumerate(((ya, dp0), (yb, dp1), (ym, dp2)))], axis=0)
    dya, dyb, dym = [_mm(dp, wbs[n], "nt", tm=512, tn=1024, tk=D, out_dtype=F32, name=f"dy_branch{n}")
                     for n, dp in enumerate((dp0, dp1, dp2))]

    dqm, dzm, dkvm = _mem_bwd(proj, kvm, dym, tm=512)
    dkvm16 = dkvm.astype(BF16)
    full["w_mem_kv"] = _mm(memn16, dkvm16, "tn", tm=1024, tn=1024, tk=256, out_dtype=F32, name="gw_mem_kv")
    dmemn = _mm(dkvm16, W["w_mem_kv"], "nt", tm=256, tn=1024, tk=2 * D, out_dtype=F32, name="d_memn")
    _, g_mem_norm = _rms_bwd(dmemn, mem, P["mem_norm_g"], width=D, col=0, tm=256, out_dtype=BF16, name="mem_norm_bwd")

    own, recv = {}, {}
    early = ("w_out", "w_gate", "w_branch", "w_mem_kv")
    early_blocks = []
    for n in early:
        blocks, own[n] = _grad_blocks(n, full[n])
        early_blocks.append(blocks)
    do16, dzb, stats = _mla_gate_bwd(dyb, o_b, proj, lse, tm=512)
    dq, dk16, dv16, landed = _mla_bwd(q16, k16, v16, do16, stats, t=512, rider=_exchange_rider(early_blocks))
    recv.update(zip(early, landed))
    dq16, dkr = _mla_qk_post(dq, dk16, tabs, tm=256)
    g_wq = _mm(cqn, dq16, "tn", tm=512, tn=1024, tk=TN_TK, out_dtype=F32, name="gw_uq")
    g_wk = _mm(ckvn, dk16, "tn", tm=512, tn=1024, tk=TN_TK, out_dtype=F32, name="gw_uk")
    g_wv = _mm(ckvn, dv16, "tn", tm=512, tn=1024, tk=TN_TK, out_dtype=F32, name="gw_uv")
    dcqn = _mm(dq16, W["wq"], "nt", tm=512, tn=LORA, tk=HEADS * QK_PAD, out_dtype=F32, name="d_cqn")
    dckvn_k = _mm(dk16, W["wk"], "nt", tm=512, tn=LORA, tk=HEADS * QK_PAD, out_dtype=F32, name="d_ckvn_k")
    dckvn = _mm(dv16, W["wv"], "nt", tm=512, tn=LORA, tk=HEADS * V_DIM, out_dtype=F32, name="d_ckvn", add=dckvn_k)
    dcq, g_q_norm = _rms_bwd(dcqn, proj, P["q_norm_g"], width=LORA, col=COL_CQ, tm=512, out_dtype=BF16, name="q_norm_bwd")
    dckv, g_kv_norm = _rms_bwd(dckvn, proj, P["kv_norm_g"], width=LORA, col=COL_CKV, tm=512, out_dtype=BF16, name="kv_norm_bwd")

    du, dvr, dza, gws, dsv_sum, g_ln_g, g_ln_b = _gmlp_bwd(proj, dya, P["a_ln_g"], P["a_ln_b"], wm, bs_t)
    g_a_w_s = gws * causal[None]
    g_a_b_s = dsv_sum.reshape(CHUNK, A_GROUPS, CHUNK).sum(axis=-1).T

    mid = ("w_uq", "w_ukv")
    mid_blocks = []
    for n, g in (("w_uq", g_wq), ("w_ukv", (g_wk, g_wv))):
        blocks, own[n] = _grad_blocks(n, _grad_reference_layout(n, g))
        mid_blocks.append(blocks)
    dproj = jnp.concatenate([du, dvr, dza, dzb, dqm, dzm, dcq, dckv, dkr], axis=1)
    g_w_in, landed = _mm(h16, dproj, "tn", tm=1024, tn=896, tk=TN_TK, out_dtype=F32, name="gw_in", rider=_exchange_rider(mid_blocks))
    recv.update(zip(mid, landed))
    in_blocks, own["w_in"] = _grad_blocks("w_in", _grad_reference_layout("w_in", g_w_in))
    dh, landed = _mm(dproj, w_in, "nt", tm=1024, tn=1024, tk=2688, out_dtype=F32, name="d_h", add=dh_gate, rider=_exchange_rider([in_blocks]))
    recv["w_in"] = landed[0]
    grad_x, g_g_pre = _rms_bwd(dh, x, P["g_pre"], width=D, col=0, tm=512, out_dtype=F32, name="pre_norm_bwd", residual=dy)

    totals = [_reduce_add(own[n], recv[n], cidx, name=f"grad_reduce_{n}") for n in BIG]
    small = {"g_pre": g_g_pre, "a_ln_g": g_ln_g, "a_ln_b": g_ln_b, "a_w_s": g_a_w_s, "a_b_s": g_a_b_s,
             "q_norm_g": g_q_norm, "kv_norm_g": g_kv_norm, "mem_norm_g": g_mem_norm, "b_gate": g_b_gate, "g_post": g_g_post}
    return loss, grad_x, totals, small


def kernel(x, mem, positions, g_pre, w_in, a_ln_g, a_ln_b, a_w_s, a_b_s, q_norm_g, w_uq, kv_norm_g, w_ukv, mem_norm_g, w_mem_kv, w_gate, b_gate, w_branch, w_out, g_post, loss_target, m_g_pre, m_w_in, m_a_ln_g, m_a_ln_b, m_a_w_s, m_a_b_s, m_q_norm_g, m_w_uq, m_kv_norm_g, m_w_ukv, m_mem_norm_g, m_w_mem_kv, m_w_gate, m_b_gate, m_w_branch, m_w_out, m_g_post, v_g_pre, v_w_in, v_a_ln_g, v_a_ln_b, v_a_w_s, v_a_b_s, v_q_norm_g, v_w_uq, v_kv_norm_g, v_w_ukv, v_mem_norm_g, v_w_mem_kv, v_w_gate, v_b_gate, v_w_branch, v_w_out, v_g_post):
    w = dict(g_pre=g_pre, w_in=w_in, a_ln_g=a_ln_g, a_ln_b=a_ln_b, a_w_s=a_w_s, a_b_s=a_b_s, q_norm_g=q_norm_g, w_uq=w_uq,
             kv_norm_g=kv_norm_g, w_ukv=w_ukv, mem_norm_g=mem_norm_g, w_mem_kv=w_mem_kv, w_gate=w_gate, b_gate=b_gate,
             w_branch=w_branch, w_out=w_out, g_post=g_post)
    m = dict(g_pre=m_g_pre, w_in=m_w_in, a_ln_g=m_a_ln_g, a_ln_b=m_a_ln_b, a_w_s=m_a_w_s, a_b_s=m_a_b_s, q_norm_g=m_q_norm_g,
             w_uq=m_w_uq, kv_norm_g=m_kv_norm_g, w_ukv=m_w_ukv, mem_norm_g=m_mem_norm_g, w_mem_kv=m_w_mem_kv, w_gate=m_w_gate,
             b_gate=m_b_gate, w_branch=m_w_branch, w_out=m_w_out, g_post=m_g_post)
    v = dict(g_pre=v_g_pre, w_in=v_w_in, a_ln_g=v_a_ln_g, a_ln_b=v_a_ln_b, a_w_s=v_a_w_s, a_b_s=v_a_b_s, q_norm_g=v_q_norm_g,
             w_uq=v_w_uq, kv_norm_g=v_kv_norm_g, w_ukv=v_w_ukv, mem_norm_g=v_mem_norm_g, w_mem_kv=v_w_mem_kv, w_gate=v_w_gate,
             b_gate=v_b_gate, w_branch=v_w_branch, w_out=v_w_out, g_post=v_g_post)

    def two_d(t, n):
        return t[n].reshape(SHARD_2D[n]) if n in SHARD_2D else t[n].reshape(t[n].shape[1:] if t[n].ndim > 2 else t[n].shape)

    shards = [two_d(w, n).astype(BF16) for n in BIG]
    w_in_full = _w_in_layout(_allgather_chips(shards[:1])[0])
    P = {n: two_d(w, n) for n in SMALL}

    S = x.shape[1]
    loss_row, grad_x, totals, small = _local_step(x[0], mem[0], positions.reshape(S, 1), loss_target[0], w_in_full, shards[1:], P)

    from_sibling = _halves_exchange(totals)
    cidx = lax.axis_index("c").astype(jnp.int32).reshape(1)
    res = {}
    for n, own, other in zip(BIG, totals, from_sibling):
        upd = _adamw(two_d(w, n), own, other, two_d(m, n), two_d(v, n), cidx, name=f"adamw_{n}")
        for key, t in zip(("grad", "delta", "new_m", "new_v"), upd):
            res[key, n] = t.reshape(w[n].shape)

    small_out, loss_sum = _small_step(small, loss_row, P, {n: two_d(m, n) for n in SMALL}, {n: two_d(v, n) for n in SMALL})
    for n in SMALL:
        for key, t in zip(("grad", "delta", "new_m", "new_v"), small_out[n]):
            res[key, n] = t.reshape(w[n].shape)
    loss = loss_sum[0, 0]

    outs = [loss, grad_x[None]]
    for key in ("grad", "delta", "new_m", "new_v"):
        outs += [res[key, n] for n in WEIGHTS]
    return tuple(outs)
```

```python
import math
from typing import NamedTuple

import jax
import jax.numpy as jnp
from jax import lax
from jax.experimental import pallas as pl
from jax.experimental.pallas import tpu as pltpu

F32 = jnp.float32
BF16 = jnp.bfloat16

D = 2048
EPS = 1e-6
CHUNK = 128
A_GROUPS = 16
HEADS = 16
QK_NOPE = 128
QK_ROPE = 64
QK_DIM = QK_NOPE + QK_ROPE
V_DIM = 128
LORA = 512
MEM_HEADS = 4
MEM_HEAD_DIM = 512
ROPE_THETA = 10000.0
QK_PAD = 256
IN_REF = 13376
IN_PAD = 13440
COL_U, COL_V, COL_ZA, COL_ZB, COL_QM, COL_ZM = 0, 1, 2, 3, 4, 5
COL_CQ, COL_CKV = 24, 25
COL_KR = 104

ADAM_LR = 0.001
ADAM_B1 = 0.9
ADAM_B2 = 0.999
ADAM_EPS = 1e-08
ADAM_WD = 0.01
ADAM_STEP = 10

VMEM_LIMIT = 56 * 1024 * 1024
LANES = 128
LOG2E = math.log2(math.e)

BIG = ("w_in", "w_uq", "w_ukv", "w_mem_kv", "w_gate", "w_branch", "w_out")
SMALL = ("g_pre", "a_ln_g", "a_ln_b", "a_w_s", "a_b_s", "q_norm_g", "kv_norm_g", "mem_norm_g", "b_gate", "g_post")
WEIGHTS = ("g_pre", "w_in", "a_ln_g", "a_ln_b", "a_w_s", "a_b_s", "q_norm_g", "w_uq", "kv_norm_g", "w_ukv",
           "mem_norm_g", "w_mem_kv", "w_gate", "b_gate", "w_branch", "w_out", "g_post")
N_CHIPS = 4
N_DEV = 8


def _params(sem=None):
    return pltpu.CompilerParams(dimension_semantics=sem, vmem_limit_bytes=VMEM_LIMIT)


def _sigmoid(z):
    return 1.0 / (1.0 + jnp.exp(-z))


def _gelu_parts(x):
    c = math.sqrt(2.0 / math.pi)
    x2 = x * x
    t = jnp.tanh(c * (x + 0.044715 * x * x2))
    g = 0.5 * x * (1.0 + t)
    dg = 0.5 * (1.0 + t) + 0.5 * x * (1.0 - t * t) * (c * (1.0 + 3.0 * 0.044715 * x2))
    return g, dg


def _silu_parts(z):
    s = _sigmoid(z)
    return z * s, s * (1.0 + z * (1.0 - s))


def _dot(a, b, dims):
    return lax.dot_general(a, b, (dims, ((), ())), preferred_element_type=F32)


NN = ((1,), (0,))
NT = ((1,), (1,))
TN = ((0,), (0,))
TN_TK = 2048


class _Rider(NamedTuple):
    ins: tuple
    out_shapes: tuple
    n_sems: int
    phases: tuple


def _ride(rider, refs_in, refs_out, sems, step, total):
    for frac, fn in rider.phases:
        @pl.when(step == int(frac * (total - 1)))
        def _():
            fn(refs_in, refs_out, sems[0], sems[1])


def _mm(a, b, mode, *, tm, tn, tk, out_dtype, name, add=None, rider=None):
    if mode == "nn":
        (M, K), (_, N) = a.shape, b.shape
    elif mode == "nt":
        (M, K), (N, _) = a.shape, b.shape
    else:
        (K, M), (_, N) = a.shape, b.shape
    tm, tn, tk = min(tm, M), min(tn, N), min(tk, K)
    assert M % tm == 0 and N % tn == 0 and K % tk == 0, (name, M, N, K, tm, tn, tk)
    ni, nj, nk = M // tm, N // tn, K // tk
    dims = {"nn": NN, "nt": NT, "tn": TN}[mode]
    has_add = add is not None
    n_rin = len(rider.ins) if rider else 0
    n_rout = len(rider.out_shapes) if rider else 0

    def body(*refs):
        a_ref, b_ref = refs[0], refs[1]
        pos = 2
        add_ref = refs[pos] if has_add else None
        pos += int(has_add)
        rin = refs[pos:pos + n_rin]
        pos += n_rin
        o_ref = refs[pos]
        rout = refs[pos + 1:pos + 1 + n_rout]
        pos += 1 + n_rout
        acc = refs[pos] if nk > 1 else None
        sems = refs[-2:] if rider else None
        if rider:
            step = (pl.program_id(0) * ni + pl.program_id(1)) * nk + pl.program_id(2)
            _ride(rider._replace(phases=rider.phases[:1]), rin, rout, sems, step, nj * ni * nk)
        part = _dot(a_ref[...].astype(BF16), b_ref[...].astype(BF16), dims)

        def finish(r):
            if has_add:
                r = r + add_ref[...]
            o_ref[...] = r.astype(out_dtype)

        if nk == 1:
            finish(part)
        else:
            k = pl.program_id(2)

            @pl.when(k == 0)
            def _():
                acc[...] = part

            @pl.when(k > 0)
            def _():
                acc[...] += part

            @pl.when(k == nk - 1)
            def _():
                finish(acc[...])

        if rider:
            _ride(rider._replace(phases=rider.phases[1:]), rin, rout, sems, step, nj * ni * nk)

    if mode == "nn":
        a_spec = pl.BlockSpec((tm, tk), lambda j, i, k: (i, k))
        b_spec = pl.BlockSpec((tk, tn), lambda j, i, k: (k, j))
    elif mode == "nt":
        a_spec = pl.BlockSpec((tm, tk), lambda j, i, k: (i, k))
        b_spec = pl.BlockSpec((tn, tk), lambda j, i, k: (j, k))
    else:
        a_spec = pl.BlockSpec((tk, tm), lambda j, i, k: (k, i))
        b_spec = pl.BlockSpec((tk, tn), lambda j, i, k: (k, j))
    o_spec = pl.BlockSpec((tm, tn), lambda j, i, k: (i, j))
    hbm = pl.BlockSpec(memory_space=pl.ANY)
    in_specs = [a_spec, b_spec] + ([o_spec] if has_add else []) + [hbm] * n_rin
    args = (a, b) + ((add,) if has_add else ()) + (tuple(rider.ins) if rider else ())
    scratch = [pltpu.VMEM((tm, tn), F32)] if nk > 1 else []
    if rider:
        scratch += [pltpu.SemaphoreType.DMA((rider.n_sems,)), pltpu.SemaphoreType.DMA((rider.n_sems,))]
    res = pl.pallas_call(
        body, name=name, grid=(nj, ni, nk), in_specs=in_specs, out_specs=[o_spec] + [hbm] * n_rout,
        out_shape=[jax.ShapeDtypeStruct((M, N), out_dtype)] + (list(rider.out_shapes) if rider else []),
        scratch_shapes=scratch,
        compiler_params=_params(("arbitrary",) * 3 if rider else ("parallel", "parallel", "arbitrary")),
    )(*args)
    return (res[0], list(res[1:])) if rider else res[0]


def _rms_fwd(x, g, *, width, col, tm, name):
    rows = x.shape[0]
    tm = min(tm, rows)

    def body(x_ref, g_ref, y_ref):
        xv = x_ref[...].astype(F32)
        r = lax.rsqrt(jnp.mean(xv * xv, axis=-1, keepdims=True) + EPS)
        y_ref[...] = ((xv * r) * g_ref[...]).astype(BF16)

    return pl.pallas_call(
        body, name=name, grid=(rows // tm,),
        in_specs=[pl.BlockSpec((tm, width), lambda i: (i, col)), pl.BlockSpec((1, width), lambda i: (0, 0))],
        out_specs=pl.BlockSpec((tm, width), lambda i: (i, 0)),
        out_shape=jax.ShapeDtypeStruct((rows, width), BF16),
        compiler_params=_params(("parallel",)),
    )(x, g)


def _rms_bwd(d, x, g, *, width, col, tm, out_dtype, name, residual=None):
    rows = d.shape[0]
    tm = min(tm, rows)
    has_res = residual is not None

    def body(*refs):
        d_ref, x_ref, g_ref = refs[:3]
        res_ref = refs[3] if has_res else None
        dx_ref, gg_ref = refs[-2], refs[-1]
        dv = d_ref[...]
        xv = x_ref[...].astype(F32)
        r = lax.rsqrt(jnp.mean(xv * xv, axis=-1, keepdims=True) + EPS)
        n = xv * r

        @pl.when(pl.program_id(0) == 0)
        def _():
            gg_ref[...] = jnp.zeros_like(gg_ref)

        gg_ref[...] += jnp.sum(dv * n, axis=0, keepdims=True)
        gd = dv * g_ref[...]
        dx = r * (gd - n * jnp.mean(gd * n, axis=-1, keepdims=True))
        if has_res:
            dx = dx + res_ref[...]
        dx_ref[...] = dx.astype(out_dtype)

    blk = pl.BlockSpec((tm, width), lambda i: (i, 0))
    in_specs = [blk, pl.BlockSpec((tm, width), lambda i: (i, col)),
                pl.BlockSpec((1, width), lambda i: (0, 0))] + ([blk] if has_res else [])
    args = (d, x, g) + ((residual,) if has_res else ())
    return pl.pallas_call(
        body, name=name, grid=(rows // tm,), in_specs=in_specs,
        out_specs=[blk, pl.BlockSpec((1, width), lambda i: (0, 0))],
        out_shape=[jax.ShapeDtypeStruct((rows, width), out_dtype), jax.ShapeDtypeStruct((1, width), F32)],
        compiler_params=_params(("arbitrary",)),
    )(*args)


def _rope_tables(pos_col, inv_lane, *, tm):
    rows = pos_col.shape[0]
    tm = min(tm, rows)

    def body(p_ref, f_ref, c_ref, s1_ref, s2_ref):
        ang = p_ref[...].astype(F32) * f_ref[...]
        lane = lax.broadcasted_iota(jnp.int32, ang.shape, 1)
        c, s = jnp.cos(ang), jnp.sin(ang)
        half = QK_ROPE // 2
        c_ref[...] = jnp.where(lane < QK_ROPE, c, 0.0)
        s1_ref[...] = jnp.where(lane < half, -s, 0.0)
        s2_ref[...] = jnp.where((lane >= half) & (lane < QK_ROPE), s, 0.0)

    blk = pl.BlockSpec((tm, LANES), lambda i: (i, 0))
    return pl.pallas_call(
        body, name="rope_tables", grid=(rows // tm,),
        in_specs=[pl.BlockSpec((tm, 1), lambda i: (i, 0)), pl.BlockSpec((1, LANES), lambda i: (0, 0))],
        out_specs=[blk, blk, blk], out_shape=[jax.ShapeDtypeStruct((rows, LANES), F32)] * 3,
        compiler_params=_params(("parallel",)),
    )(pos_col, inv_lane)


def _rot(t, c, s1, s2, sign):
    r1 = pltpu.roll(t, LANES - QK_ROPE // 2, 1) * s1
    r2 = pltpu.roll(t, QK_ROPE // 2, 1) * s2
    return t * c + (r1 + r2) if sign > 0 else t * c - (r1 + r2)


def _mla_proj(cqn, ckvn, proj, tabs, wq, wk, wv, *, tm):
    rows = cqn.shape[0]
    tm = min(tm, rows)

    def body(cq_ref, ckv_ref, kr_ref, c_ref, s1_ref, s2_ref, wq_ref, wk_ref, wv_ref, q_ref, k_ref, v_ref):
        c, s1, s2 = c_ref[...], s1_ref[...], s2_ref[...]
        q = _dot(cq_ref[...], wq_ref[...], NN)
        k = _dot(ckv_ref[...], wk_ref[...], NN)
        kpe = _rot(kr_ref[...].astype(F32), c, s1, s2, 1).astype(BF16)
        for h in range(HEADS):
            lo = h * QK_PAD
            q_ref[:, lo:lo + QK_NOPE] = q[:, lo:lo + QK_NOPE].astype(BF16)
            q_ref[:, lo + QK_NOPE:lo + QK_PAD] = _rot(q[:, lo + QK_NOPE:lo + QK_PAD], c, s1, s2, 1).astype(BF16)
            k_ref[:, lo:lo + QK_NOPE] = k[:, lo:lo + QK_NOPE].astype(BF16)
            k_ref[:, lo + QK_NOPE:lo + QK_PAD] = kpe
        v_ref[...] = _dot(ckv_ref[...], wv_ref[...], NN).astype(BF16)

    def row(w):
        return pl.BlockSpec((tm, w), lambda i: (i, 0))

    def whole(w):
        return pl.BlockSpec(w.shape, lambda i: (0, 0))

    return pl.pallas_call(
        body, name="mla_proj", grid=(rows // tm,),
        in_specs=[row(LORA), row(LORA), pl.BlockSpec((tm, LANES), lambda i: (i, COL_KR)), row(LANES), row(LANES), row(LANES),
                  whole(wq), whole(wk), whole(wv)],
        out_specs=[row(HEADS * QK_PAD), row(HEADS * QK_PAD), row(HEADS * V_DIM)],
        out_shape=[jax.ShapeDtypeStruct((rows, HEADS * QK_PAD), BF16), jax.ShapeDtypeStruct((rows, HEADS * QK_PAD), BF16),
                   jax.ShapeDtypeStruct((rows, HEADS * V_DIM), BF16)],
        compiler_params=_params(("parallel",)),
    )(cqn, ckvn, proj, *tabs, wq, wk, wv)


def _mla_fwd(q, k, v, proj, *, t):
    S = q.shape[0]
    t = min(t, S // 2)
    n = S // t
    per = min(4, n)
    scale = QK_DIM ** -0.5

    def body(q_ref, k_ref, v_ref, z_ref, o_ref, y_ref, lse_ref):
        qi = pl.program_id(1)
        qv = q_ref[...]
        c2 = scale * LOG2E

        def block(k0, width, carry, row0):
            m_old, l_old, acc = carry
            ks = pl.ds(pl.multiple_of(k0, t), width)
            s = _dot(qv, k_ref[ks, :], NT)
            if row0 is not None:
                r = lax.broadcasted_iota(jnp.int32, s.shape, 0)
                c = lax.broadcasted_iota(jnp.int32, s.shape, 1)
                s = jnp.where(c <= r + row0, s, -1e30)
            m_new = jnp.maximum(m_old, jnp.max(s, axis=-1, keepdims=True))
            alpha = jnp.exp2((m_old - m_new) * c2)
            p = jnp.exp2((s - m_new) * c2)
            l_new = alpha * l_old + jnp.sum(p, axis=-1, keepdims=True)
            acc = alpha * acc + _dot(p.astype(BF16), v_ref[ks, :], NN)
            return m_new, l_new, acc

        init = (jnp.full((t, 1), -1e30, F32), jnp.zeros((t, 1), F32), jnp.zeros((t, V_DIM), F32))
        carry = lax.fori_loop(0, qi // per, lambda j, cr: block(j * (per * t), per * t, cr, None), init)
        last = [lambda cr, w=w: block((qi - w) * t, (w + 1) * t, cr, w * t) for w in range(per)]
        m_f, l_f, acc = lax.switch(qi % per, last, carry)
        o = acc / l_f
        o_ref[...] = o
        sz, _ = _silu_parts(z_ref[...].astype(F32))
        y_ref[...] = (o * sz).astype(BF16)
        lse2 = (m_f * scale + jnp.log(l_f)) * LOG2E
        lane = lax.broadcasted_iota(jnp.int32, (t, LANES), 1)
        lse_ref[0, 0] = jnp.where(lane == 0, lse2, 0.0).T[0:8, :]

    zcol = COL_ZB * (D // V_DIM)
    return pl.pallas_call(
        body, name="mla_fwd", grid=(HEADS, n),
        in_specs=[pl.BlockSpec((t, QK_PAD), lambda h, i: (i, h)),
                  pl.BlockSpec((S, QK_PAD), lambda h, i: (0, h)),
                  pl.BlockSpec((S, V_DIM), lambda h, i: (0, h)),
                  pl.BlockSpec((t, V_DIM), lambda h, i: (i, zcol + h))],
        out_specs=[pl.BlockSpec((t, V_DIM), lambda h, i: (i, h)), pl.BlockSpec((t, V_DIM), lambda h, i: (i, h)),
                   pl.BlockSpec((1, 1, 8, t), lambda h, i: (h, i, 0, 0))],
        out_shape=[jax.ShapeDtypeStruct((S, HEADS * V_DIM), F32), jax.ShapeDtypeStruct((S, HEADS * V_DIM), BF16),
                   jax.ShapeDtypeStruct((HEADS, n, 8, t), F32)],
        compiler_params=_params(("parallel", "parallel")),
    )(q, k, v, proj)


def _mla_gate_bwd(dy, o, proj, lse, *, tm):
    S = dy.shape[0]
    tm = min(tm, S)

    def body(dy_ref, o_ref, z_ref, lse_ref, do_ref, dz_ref, st_ref):
        sz, dsz = _silu_parts(z_ref[...].astype(F32))
        dyv, ov = dy_ref[...].astype(F32), o_ref[...]
        do = dyv * sz
        do_ref[...] = do.astype(BF16)
        dz_ref[...] = (dyv * ov * dsz).astype(BF16)
        prod = do * ov
        lane = lax.broadcasted_iota(jnp.int32, (tm, LANES), 1)
        for h in range(HEADS):
            delta = jnp.sum(prod[:, h * V_DIM:(h + 1) * V_DIM], axis=-1, keepdims=True)
            st_ref[h, 0] = lse_ref[h, 0] + jnp.where(lane == 1, delta, 0.0).T[0:8, :]

    blk = pl.BlockSpec((tm, D), lambda i: (i, 0))
    return pl.pallas_call(
        body, name="mla_gate_bwd", grid=(S // tm,),
        in_specs=[blk, blk, pl.BlockSpec((tm, D), lambda i: (i, COL_ZB)), pl.BlockSpec((HEADS, 1, 8, tm), lambda i: (0, i, 0, 0))],
        out_specs=[blk, blk, pl.BlockSpec((HEADS, 1, 8, tm), lambda i: (0, i, 0, 0))],
        out_shape=[jax.ShapeDtypeStruct((S, D), BF16), jax.ShapeDtypeStruct((S, D), BF16),
                   jax.ShapeDtypeStruct((HEADS, S // tm, 8, tm), F32)],
        compiler_params=_params(("parallel",)),
    )(dy, o, proj, lse)


def _mla_bwd(q, k, v, do, stats, *, t, rider):
    S = q.shape[0]
    t = min(t, S)
    n = S // t
    per = min(4, n)
    c2 = (QK_DIM ** -0.5) * LOG2E

    n_rin, n_rout = len(rider.ins), len(rider.out_shapes)

    def body(*refs):
        q_ref, k_ref, v_ref, do_ref, st_ref = refs[:5]
        rin = refs[5:5 + n_rin]
        dq_ref, dk_ref, dv_ref = refs[5 + n_rin:8 + n_rin]
        rout = refs[8 + n_rin:8 + n_rin + n_rout]
        ki = pl.program_id(1)
        step = pl.program_id(0) * n + ki
        _ride(rider._replace(phases=rider.phases[:1]), rin, rout, refs[-2:], step, HEADS * n)

        @pl.when(ki == 0)
        def _():
            dq_ref[...] = jnp.zeros_like(dq_ref)

        kv, vv = k_ref[...], v_ref[...]

        def block(i, carry, diag):
            dk, dv = carry
            rows = pl.ds(pl.multiple_of(i * t, t), t)
            qv, dov, st = q_ref[rows, :], do_ref[rows, :], st_ref[0, i]
            s = _dot(kv, qv, NT)
            if diag:
                key = lax.broadcasted_iota(jnp.int32, s.shape, 0)
                qry = lax.broadcasted_iota(jnp.int32, s.shape, 1)
                s = jnp.where(key <= qry, s, -1e30)
            p = jnp.exp2(s * c2 - st[0:1, :])
            p16 = p.astype(BF16)
            dv = dv + _dot(p16, dov, NN)
            dp = _dot(vv, dov, NT)
            ds = (p * (dp - st[1:2, :])).astype(BF16)
            dk = dk + _dot(ds, qv, NN)
            dq_ref[rows, :] += _dot(ds, kv, TN)
            return dk, dv

        carry = block(ki, (jnp.zeros((t, QK_PAD), F32), jnp.zeros((t, V_DIM), F32)), True)
        rest = n - 1 - ki

        def run(start, count):
            def f(cr):
                for u in range(count):
                    cr = block(start + u, cr, False)
                return cr
            return f

        carry = lax.switch(rest % per, [run(ki + 1, w) for w in range(per)], carry)
        first = ki + 1 + rest % per
        dk, dv = lax.fori_loop(0, rest // per, lambda i, cr: run(first + per * i, per)(cr), carry)
        dk_ref[...] = (dk * (QK_DIM ** -0.5)).astype(BF16)
        dv_ref[...] = dv.astype(BF16)
        _ride(rider._replace(phases=rider.phases[1:]), rin, rout, refs[-2:], step, HEADS * n)

    hbm = pl.BlockSpec(memory_space=pl.ANY)
    res = pl.pallas_call(
        body, name="mla_bwd", grid=(HEADS, n),
        in_specs=[pl.BlockSpec((S, QK_PAD), lambda h, j: (0, h)),
                  pl.BlockSpec((t, QK_PAD), lambda h, j: (j, h)),
                  pl.BlockSpec((t, V_DIM), lambda h, j: (j, h)),
                  pl.BlockSpec((S, V_DIM), lambda h, j: (0, h)),
                  pl.BlockSpec((1, n, 8, t), lambda h, j: (h, 0, 0, 0))] + [hbm] * n_rin,
        out_specs=[pl.BlockSpec((S, QK_PAD), lambda h, j: (0, h)),
                   pl.BlockSpec((t, QK_PAD), lambda h, j: (j, h)),
                   pl.BlockSpec((t, V_DIM), lambda h, j: (j, h))] + [hbm] * n_rout,
        out_shape=[jax.ShapeDtypeStruct((S, HEADS * QK_PAD), F32), jax.ShapeDtypeStruct((S, HEADS * QK_PAD), BF16),
                   jax.ShapeDtypeStruct((S, HEADS * V_DIM), BF16)] + list(rider.out_shapes),
        scratch_shapes=[pltpu.SemaphoreType.DMA((rider.n_sems,)), pltpu.SemaphoreType.DMA((rider.n_sems,))],
        compiler_params=_params(("arbitrary", "arbitrary")),
    )(q, k, v, do, stats, *rider.ins)
    return res[0], res[1], res[2], list(res[3:])


def _mla_qk_post(dq, dk, tabs, *, tm):
    S = dq.shape[0]
    tm = min(tm, S)
    scale = QK_DIM ** -0.5

    def body(dq_ref, dk_ref, c_ref, s1_ref, s2_ref, q16_ref, kr_ref):
        c, s1, s2 = c_ref[...], s1_ref[...], s2_ref[...]
        kpe = jnp.zeros((tm, LANES), F32)
        for h in range(HEADS):
            lo = h * QK_PAD
            q16_ref[:, lo:lo + QK_NOPE] = (dq_ref[:, lo:lo + QK_NOPE] * scale).astype(BF16)
            q16_ref[:, lo + QK_NOPE:lo + QK_PAD] = _rot(dq_ref[:, lo + QK_NOPE:lo + QK_PAD] * scale, c, s1, s2, -1).astype(BF16)
            kpe = kpe + dk_ref[:, lo + QK_NOPE:lo + QK_PAD].astype(F32)
        kr_ref[...] = _rot(kpe, c, s1, s2, -1).astype(BF16)

    wide = pl.BlockSpec((tm, HEADS * QK_PAD), lambda i: (i, 0))
    lane = pl.BlockSpec((tm, LANES), lambda i: (i, 0))
    return pl.pallas_call(
        body, name="mla_qk_post", grid=(S // tm,),
        in_specs=[wide, wide, lane, lane, lane], out_specs=[wide, lane],
        out_shape=[jax.ShapeDtypeStruct((S, HEADS * QK_PAD), BF16), jax.ShapeDtypeStruct((S, LANES), BF16)],
        compiler_params=_params(("parallel",)),
    )(dq, dk, *tabs)


def _mem_scores(q16, km_ref, h):
    lo = h * MEM_HEAD_DIM
    s = _dot(q16, km_ref[:, lo:lo + MEM_HEAD_DIM], NT) * (MEM_HEAD_DIM ** -0.5)
    e = jnp.exp(s - jnp.max(s, axis=-1, keepdims=True))
    return e / jnp.sum(e, axis=-1, keepdims=True)


def _mem_fwd(proj, kvm, *, tm):
    S = proj.shape[0]
    tm = min(tm, S)
    M = kvm.shape[0]

    def body(q_ref, z_ref, km_ref, vm_ref, y_ref):
        sz, _ = _silu_parts(z_ref[...].astype(F32))
        for h in range(MEM_HEADS):
            lo = h * MEM_HEAD_DIM
            p = _mem_scores(q_ref[:, lo:lo + MEM_HEAD_DIM].astype(BF16), km_ref, h)
            o = _dot(p.astype(BF16), vm_ref[:, lo:lo + MEM_HEAD_DIM], NN)
            y_ref[:, lo:lo + MEM_HEAD_DIM] = (o * sz[:, lo:lo + MEM_HEAD_DIM]).astype(BF16)

    return pl.pallas_call(
        body, name="mem_fwd", grid=(S // tm,),
        in_specs=[pl.BlockSpec((tm, D), lambda i: (i, COL_QM)), pl.BlockSpec((tm, D), lambda i: (i, COL_ZM)),
                  pl.BlockSpec((M, D), lambda i: (0, 0)), pl.BlockSpec((M, D), lambda i: (0, 1))],
        out_specs=pl.BlockSpec((tm, D), lambda i: (i, 0)),
        out_shape=jax.ShapeDtypeStruct((S, D), BF16),
        compiler_params=_params(("parallel",)),
    )(proj, proj, kvm, kvm)


def _mem_bwd(proj, kvm, dy, *, tm):
    S = proj.shape[0]
    tm = min(tm, S)
    M = kvm.shape[0]
    scale = MEM_HEAD_DIM ** -0.5

    def body(q_ref, z_ref, km_ref, vm_ref, dy_ref, dq_ref, dz_ref, dkv_ref):
        @pl.when(pl.program_id(0) == 0)
        def _():
            dkv_ref[...] = jnp.zeros_like(dkv_ref)

        sz, dsz = _silu_parts(z_ref[...].astype(F32))
        dyv = dy_ref[...].astype(F32)
        for h in range(MEM_HEADS):
            lo = h * MEM_HEAD_DIM
            sl = slice(lo, lo + MEM_HEAD_DIM)
            q16 = q_ref[:, sl].astype(BF16)
            p = _mem_scores(q16, km_ref, h)
            p16 = p.astype(BF16)
            o = _dot(p16, vm_ref[:, sl], NN)
            dy_h = dyv[:, sl]
            dz_ref[:, sl] = (dy_h * o * dsz[:, sl]).astype(BF16)
            do16 = (dy_h * sz[:, sl]).astype(BF16)
            dp = _dot(do16, vm_ref[:, sl], NT)
            ds = (p * (dp - jnp.sum(dp * p, axis=-1, keepdims=True)) * scale).astype(BF16)
            dq_ref[:, sl] = _dot(ds, km_ref[:, sl], NN).astype(BF16)
            dkv_ref[:, sl] += _dot(ds, q16, TN)
            dkv_ref[:, D + lo:D + lo + MEM_HEAD_DIM] += _dot(p16, do16, TN)

    blk = pl.BlockSpec((tm, D), lambda i: (i, 0))
    return pl.pallas_call(
        body, name="mem_bwd", grid=(S // tm,),
        in_specs=[pl.BlockSpec((tm, D), lambda i: (i, COL_QM)), pl.BlockSpec((tm, D), lambda i: (i, COL_ZM)),
                  pl.BlockSpec((M, D), lambda i: (0, 0)), pl.BlockSpec((M, D), lambda i: (0, 1)), blk],
        out_specs=[blk, blk, pl.BlockSpec((M, 2 * D), lambda i: (0, 0))],
        out_shape=[jax.ShapeDtypeStruct((S, D), BF16), jax.ShapeDtypeStruct((S, D), BF16),
                   jax.ShapeDtypeStruct((M, 2 * D), F32)],
        compiler_params=_params(("arbitrary",)),
    )(proj, proj, kvm, kvm, dy)


def _gmlp_common(u_ref, v_ref, lng_ref, lnb_ref):
    u, du = _gelu_parts(u_ref[...].astype(F32))
    vg, dvg = _gelu_parts(v_ref[...].astype(F32))
    mu = jnp.mean(vg, axis=-1, keepdims=True)
    vc = vg - mu
    r = lax.rsqrt(jnp.mean(vc * vc, axis=-1, keepdims=True) + EPS)
    vhat = vc * r
    vn = vhat * lng_ref[...] + lnb_ref[...]
    return u, du, dvg, r, vhat, vn.astype(BF16)


def _gmlp_fwd(proj, ln_g, ln_b, wm, bs_t):
    S = proj.shape[0]

    def body(u_ref, v_ref, z_ref, lng_ref, lnb_ref, wm_ref, bs_ref, y_ref):
        u, _, _, _, _, v16 = _gmlp_common(u_ref, v_ref, lng_ref, lnb_ref)
        sz, _ = _silu_parts(z_ref[...].astype(F32))
        for g in range(A_GROUPS):
            sl = slice(g * CHUNK, (g + 1) * CHUNK)
            sv = _dot(wm_ref[g], v16[:, sl], NN) + bs_ref[:, g:g + 1]
            y_ref[:, sl] = (u[:, sl] * sv * sz[:, sl]).astype(BF16)

    def col(c):
        return pl.BlockSpec((CHUNK, D), lambda i: (i, c))

    vec = pl.BlockSpec((1, D), lambda i: (0, 0))
    return pl.pallas_call(
        body, name="gmlp_fwd", grid=(S // CHUNK,),
        in_specs=[col(COL_U), col(COL_V), col(COL_ZA), vec, vec,
                  pl.BlockSpec((A_GROUPS, CHUNK, CHUNK), lambda i: (0, 0, 0)), pl.BlockSpec((CHUNK, A_GROUPS), lambda i: (0, 0))],
        out_specs=col(0), out_shape=jax.ShapeDtypeStruct((S, D), BF16),
        compiler_params=_params(("parallel",)),
    )(proj, proj, proj, ln_g, ln_b, wm, bs_t)


def _gmlp_bwd(proj, dy, ln_g, ln_b, wm, bs_t, others):
    S = proj.shape[0]

    def body(u_ref, v_ref, z_ref, dy_ref, lng_ref, lnb_ref, wm_ref, bs_ref, zb_ref, qm_ref, zm_ref, cq_ref, ckv_ref, kr_ref,
             dp_ref, gws_ref, dsv_ref, glg_ref, glb_ref, dvn_s):
        dp_ref[:, COL_ZB * D:(COL_ZB + 1) * D] = zb_ref[...]
        dp_ref[:, COL_QM * D:(COL_QM + 1) * D] = qm_ref[...]
        dp_ref[:, COL_ZM * D:(COL_ZM + 1) * D] = zm_ref[...]
        dp_ref[:, COL_CQ * LORA:(COL_CQ + 1) * LORA] = cq_ref[...]
        dp_ref[:, COL_CKV * LORA:(COL_CKV + 1) * LORA] = ckv_ref[...]
        dp_ref[:, COL_KR * LANES:(COL_KR + 1) * LANES] = kr_ref[...]

        @pl.when(pl.program_id(0) == 0)
        def _():
            gws_ref[...] = jnp.zeros_like(gws_ref)
            dsv_ref[...] = jnp.zeros_like(dsv_ref)
            glg_ref[...] = jnp.zeros_like(glg_ref)
            glb_ref[...] = jnp.zeros_like(glb_ref)

        u, du, dvg, r, vhat, v16 = _gmlp_common(u_ref, v_ref, lng_ref, lnb_ref)
        sz, dsz = _silu_parts(z_ref[...].astype(F32))
        dyv = dy_ref[...].astype(F32)
        for g in range(A_GROUPS):
            sl = slice(g * CHUNK, (g + 1) * CHUNK)
            sv = _dot(wm_ref[g], v16[:, sl], NN) + bs_ref[:, g:g + 1]
            dy_g, u_g, sz_g = dyv[:, sl], u[:, sl], sz[:, sl]
            dsv = dy_g * u_g * sz_g
            dp_ref[:, g * CHUNK:(g + 1) * CHUNK] = (dy_g * sv * sz_g * du[:, sl]).astype(BF16)
            dp_ref[:, COL_ZA * D + g * CHUNK:COL_ZA * D + (g + 1) * CHUNK] = (dy_g * u_g * sv * dsz[:, sl]).astype(BF16)
            dsv16 = dsv.astype(BF16)
            dvn_s[:, sl] = _dot(wm_ref[g], dsv16, TN)
            gws_ref[g] += _dot(dsv16, v16[:, sl], NT)
            dsv_ref[:, sl] += dsv
        dvn = dvn_s[...]
        glb_ref[...] += jnp.sum(dvn, axis=0, keepdims=True)
        glg_ref[...] += jnp.sum(dvn * vhat, axis=0, keepdims=True)
        dvh = dvn * lng_ref[...]
        dvc = r * (dvh - jnp.mean(dvh, axis=-1, keepdims=True) - vhat * jnp.mean(dvh * vhat, axis=-1, keepdims=True))
        dp_ref[:, COL_V * D:(COL_V + 1) * D] = (dvc * dvg).astype(BF16)

    def col(c):
        return pl.BlockSpec((CHUNK, D), lambda i: (i, c))

    def rows(width):
        return pl.BlockSpec((CHUNK, width), lambda i: (i, 0))

    vec = pl.BlockSpec((1, D), lambda i: (0, 0))
    wsp = pl.BlockSpec((A_GROUPS, CHUNK, CHUNK), lambda i: (0, 0, 0))
    return pl.pallas_call(
        body, name="gmlp_bwd", grid=(S // CHUNK,),
        in_specs=[col(COL_U), col(COL_V), col(COL_ZA), col(0), vec, vec, wsp, pl.BlockSpec((CHUNK, A_GROUPS), lambda i: (0, 0)),
                  rows(D), rows(D), rows(D), rows(LORA), rows(LORA), rows(LANES)],
        out_specs=[rows(IN_PAD), wsp, pl.BlockSpec((CHUNK, D), lambda i: (0, 0)), vec, vec],
        out_shape=[jax.ShapeDtypeStruct((S, IN_PAD), BF16),
                   jax.ShapeDtypeStruct((A_GROUPS, CHUNK, CHUNK), F32), jax.ShapeDtypeStruct((CHUNK, D), F32),
                   jax.ShapeDtypeStruct((1, D), F32), jax.ShapeDtypeStruct((1, D), F32)],
        scratch_shapes=[pltpu.VMEM((CHUNK, D), F32)],
        compiler_params=_params(("arbitrary",)),
    )(proj, proj, proj, dy, ln_g, ln_b, wm, bs_t, *others)


def _gate_merge(h16, ys, wg, bg, wbs, *, tm, tn):
    S = h16.shape[0]
    tm = min(tm, S)
    nj = D // tn

    def body(h_ref, ya_ref, yb_ref, ym_ref, wg0, wg1, wg2, bg0, bg1, bg2, wb0, wb1, wb2,
             mg_ref, g0_ref, g1_ref, g2_ref, p0_ref, p1_ref, p2_ref):
        hv = h_ref[...]
        acc = None
        for y_ref, wg_ref, bgr, wb_ref, g_ref, p_ref in ((ya_ref, wg0, bg0, wb0, g0_ref, p0_ref),
                                                         (yb_ref, wg1, bg1, wb1, g1_ref, p1_ref),
                                                         (ym_ref, wg2, bg2, wb2, g2_ref, p2_ref)):
            gate = _sigmoid(_dot(hv, wg_ref[...], NN) + bgr[...])
            p = _dot(y_ref[...], wb_ref[...], NN)
            g_ref[...] = gate.astype(BF16)
            p_ref[...] = p.astype(BF16)
            acc = gate * p if acc is None else acc + gate * p
        mg_ref[...] = acc.astype(BF16)

    a_spec = pl.BlockSpec((tm, D), lambda j, i: (i, 0))
    o_spec = pl.BlockSpec((tm, tn), lambda j, i: (i, j))

    def wgs(n):
        return pl.BlockSpec((D, tn), lambda j, i: (0, n * nj + j))

    def bgs(n):
        return pl.BlockSpec((1, tn), lambda j, i: (0, n * nj + j))

    wbsp = pl.BlockSpec((D, tn), lambda j, i: (0, j))
    return pl.pallas_call(
        body, name="gate_merge", grid=(nj, S // tm),
        in_specs=[a_spec] * 4 + [wgs(0), wgs(1), wgs(2), bgs(0), bgs(1), bgs(2), wbsp, wbsp, wbsp],
        out_specs=[o_spec] * 7, out_shape=[jax.ShapeDtypeStruct((S, D), BF16)] * 7,
        compiler_params=_params(("parallel", "parallel")),
    )(h16, *ys, wg, wg, wg, bg, bg, bg, *wbs)


def _gate_bwd(dmerged, gates, ps, *, tm):
    S = dmerged.shape[0]
    tm = min(tm, S)

    def body(dm_ref, g0, g1, g2, p0, p1, p2, dp0, dp1, dp2, dg_ref, gb_ref):
        @pl.when(pl.program_id(0) == 0)
        def _():
            gb_ref[...] = jnp.zeros_like(gb_ref)

        dm = dm_ref[...].astype(F32)
        for n, (g_ref, p_ref, dp_ref) in enumerate(((g0, p0, dp0), (g1, p1, dp1), (g2, p2, dp2))):
            gate = g_ref[...].astype(F32)
            dp_ref[...] = (dm * gate).astype(BF16)
            dg = dm * p_ref[...].astype(F32) * gate * (1.0 - gate)
            dg_ref[:, n * D:(n + 1) * D] = dg.astype(BF16)
            gb_ref[:, n * D:(n + 1) * D] += jnp.sum(dg, axis=0, keepdims=True)

    blk = pl.BlockSpec((tm, D), lambda i: (i, 0))
    return pl.pallas_call(
        body, name="gate_bwd", grid=(S // tm,),
        in_specs=[blk] * 7,
        out_specs=[blk, blk, blk, pl.BlockSpec((tm, 3 * D), lambda i: (i, 0)), pl.BlockSpec((1, 3 * D), lambda i: (0, 0))],
        out_shape=[jax.ShapeDtypeStruct((S, D), BF16)] * 3 + [jax.ShapeDtypeStruct((S, 3 * D), BF16),
                                                              jax.ShapeDtypeStruct((1, 3 * D), F32)],
        compiler_params=_params(("arbitrary",)),
    )(dmerged, *gates, *ps)


def _post_loss(x, out, target, g_post, *, tm):
    S = x.shape[0]
    tm = min(tm, S)

    def body(x_ref, o_ref, t_ref, g_ref, dy_ref, do_ref, gg_ref, ls_ref):
        @pl.when(pl.program_id(0) == 0)
        def _():
            gg_ref[...] = jnp.zeros_like(gg_ref)
            ls_ref[...] = jnp.zeros_like(ls_ref)

        ov = o_ref[...]
        r = lax.rsqrt(jnp.mean(ov * ov, axis=-1, keepdims=True) + EPS)
        n = ov * r
        err = (x_ref[...] + n * g_ref[...]) - t_ref[...]
        ls_ref[...] += 0.5 * jnp.sum(jnp.mean(err * err, axis=-1, keepdims=True))
        dy = err * (1.0 / D)
        dy_ref[...] = dy
        gg_ref[...] += jnp.sum(dy * n, axis=0, keepdims=True)
        gd = dy * g_ref[...]
        do_ref[...] = (r * (gd - n * jnp.mean(gd * n, axis=-1, keepdims=True))).astype(BF16)

    blk = pl.BlockSpec((tm, D), lambda i: (i, 0))
    vec = pl.BlockSpec((1, D), lambda i: (0, 0))
    return pl.pallas_call(
        body, name="post_loss", grid=(S // tm,),
        in_specs=[blk, blk, blk, vec],
        out_specs=[blk, blk, vec, pl.BlockSpec((1, LANES), lambda i: (0, 0))],
        out_shape=[jax.ShapeDtypeStruct((S, D), F32), jax.ShapeDtypeStruct((S, D), BF16),
                   jax.ShapeDtypeStruct((1, D), F32), jax.ShapeDtypeStruct((1, LANES), F32)],
        compiler_params=_params(("arbitrary",)),
    )(x, out, target, g_post)


def _adamw(w, g_own, g_other, m, v, cidx, *, name):
    rows, width = w.shape
    hh = rows // 2
    tr = _row_tile(hh, width, unit=8)
    nb = hh // tr

    def body(c_ref, w_ref, own_ref, oth_ref, m_ref, v_ref, g_ref, d_ref, nm_ref, nv_ref):
        mine = (pl.program_id(0) // nb) == c_ref[0]
        gv = jnp.where(mine, own_ref[...], oth_ref[...])
        delta, nm, nv = _adam_math(w_ref[...], gv, m_ref[...], v_ref[...])
        g_ref[...] = gv
        d_ref[...] = delta
        nm_ref[...] = nm
        nv_ref[...] = nv

    blk = pl.BlockSpec((tr, width), lambda i, c_ref: (i, 0))
    half = pl.BlockSpec((tr, width), lambda i, c_ref: (i % nb, 0))
    return pl.pallas_call(
        body, name=name,
        grid_spec=pltpu.PrefetchScalarGridSpec(num_scalar_prefetch=1, grid=(rows // tr,),
                                               in_specs=[blk, half, half, blk, blk], out_specs=[blk] * 4),
        out_shape=[jax.ShapeDtypeStruct((rows, width), F32)] * 4,
        compiler_params=_params(("parallel",)),
    )(cidx, w, g_own, g_other, m, v)


MESH = pl.DeviceIdType.MESH
ANY = pl.BlockSpec(memory_space=pl.ANY)


def _place():
    return lax.axis_index("x"), lax.axis_index("y"), lax.axis_index("c")


def _other_chips(x, y):
    return [(1 - x, y), (x, 1 - y), (1 - x, 1 - y)]


def _remote(src, dst, send_sem, recv_sem, dev):
    return pltpu.make_async_remote_copy(src_ref=src, dst_ref=dst, send_sem=send_sem, recv_sem=recv_sem,
                                        device_id=dev, device_id_type=MESH)


def _allgather_chips(shards):
    nw = len(shards)

    def body(*refs):
        x_refs, out_refs = refs[:nw], refs[nw:2 * nw]
        send_sems, recv_sems = refs[2 * nw:]
        x, y, c = _place()
        sibling = (x, y, 1 - c)
        chips = _other_chips(x, y)

        def half(w, px, py, hc):
            hh = shards[w].shape[0] // 2
            return out_refs[w].at[2 * px + py, pl.ds(hc * hh, hh), :]

        sent = []
        for w in range(nw):
            hh = shards[w].shape[0] // 2
            for k, (px, py) in enumerate(chips):
                cp = _remote(x_refs[w].at[pl.ds(c * hh, hh), :], half(w, x, y, c), send_sems.at[6 * w + k],
                             recv_sems.at[6 * w + k], (px, py, c))
                cp.start()
                sent.append(cp)
        for w in range(nw):
            for k, (px, py) in enumerate(chips):
                landed = half(w, px, py, c)
                _remote(landed, landed, send_sems.at[6 * w + k], recv_sems.at[6 * w + k], (px, py, c)).wait_recv()
                cp = _remote(landed, landed, send_sems.at[6 * w + 3 + k], recv_sems.at[6 * w + 3 + k], sibling)
                cp.start()
                sent.append(cp)
        for w in range(nw):
            for k, (px, py) in enumerate(chips):
                other = half(w, px, py, 1 - c)
                _remote(other, other, send_sems.at[6 * w + 3 + k], recv_sems.at[6 * w + 3 + k], sibling).wait_recv()
        for cp in sent:
            cp.wait_send()

    outs = pl.pallas_call(
        body, name="allgather_weights", in_specs=[ANY] * nw, out_specs=[ANY] * nw,
        out_shape=[jax.ShapeDtypeStruct((N_CHIPS,) + s.shape, s.dtype) for s in shards],
        scratch_shapes=[pltpu.SemaphoreType.DMA((6 * nw,)), pltpu.SemaphoreType.DMA((6 * nw,))],
    )(*shards)
    own = 2 * lax.axis_index("x") + lax.axis_index("y")
    return [lax.dynamic_update_slice(o, s[None], (own, 0, 0)) for o, s in zip(outs, shards)]


def _row_tile(rows, cols, unit=16, budget=2 * 1024 * 1024):
    best = unit
    for t in range(unit, rows + 1, unit):
        if rows % t == 0 and t * cols * 4 <= budget:
            best = t
    assert rows % best == 0, (rows, cols)
    return best


def _peers(x, y, c):
    out = []
    for k in range(1, N_DEV):
        out.append((k, (1 - x if (k >> 2) & 1 else x, 1 - y if (k >> 1) & 1 else y, 1 - c if k & 1 else c)))
    return out


def _gather_rider(shards):
    nw = len(shards)

    def half(outs, w, px, py, hc):
        hh = shards[w].shape[0] // 2
        return outs[w].at[2 * px + py, pl.ds(hc * hh, hh), :]

    def ici(ins, outs, ss, rs, w, k, px, py, c, x, y):
        hh = shards[w].shape[0] // 2
        return _remote(ins[w].at[pl.ds(c * hh, hh), :], half(outs, w, x, y, c), ss.at[6 * w + k], rs.at[6 * w + k], (px, py, c))

    def passing(outs, ss, rs, w, k, px, py, hc, sibling):
        landed = half(outs, w, px, py, hc)
        return _remote(landed, landed, ss.at[6 * w + 3 + k], rs.at[6 * w + 3 + k], sibling)

    def start(ins, outs, ss, rs):
        x, y, c = _place()
        for w in range(nw):
            for k, (px, py) in enumerate(_other_chips(x, y)):
                ici(ins, outs, ss, rs, w, k, px, py, c, x, y).start()

    def forward(ins, outs, ss, rs):
        x, y, c = _place()
        for w in range(nw):
            for k, (px, py) in enumerate(_other_chips(x, y)):
                landed = half(outs, w, px, py, c)
                _remote(landed, landed, ss.at[6 * w + k], rs.at[6 * w + k], (px, py, c)).wait_recv()
                passing(outs, ss, rs, w, k, px, py, c, (x, y, 1 - c)).start()

    def finish(ins, outs, ss, rs):
        x, y, c = _place()
        for w in range(nw):
            for k, (px, py) in enumerate(_other_chips(x, y)):
                passing(outs, ss, rs, w, k, px, py, 1 - c, (x, y, 1 - c)).wait_recv()
        for w in range(nw):
            for k, (px, py) in enumerate(_other_chips(x, y)):
                ici(ins, outs, ss, rs, w, k, px, py, c, x, y).wait_send()
                passing(outs, ss, rs, w, k, px, py, c, (x, y, 1 - c)).wait_send()

    return _Rider(ins=tuple(shards), out_shapes=tuple(jax.ShapeDtypeStruct((N_CHIPS,) + s.shape, s.dtype) for s in shards),
                  n_sems=6 * nw, phases=((0.0, start), (0.8, forward), (1.0, finish)))


def _own_blocks_in_place(gathered, shards):
    own = 2 * lax.axis_index("x") + lax.axis_index("y")
    return [lax.dynamic_update_slice(o, s[None], (own, 0, 0)) for o, s in zip(gathered, shards)]


def _exchange_rider(blocks):
    nw = len(blocks)

    def copy(ins, outs, ss, rs, w, k, peer):
        hh = blocks[w].shape[1] // 2
        px, py, pc = peer
        return _remote(ins[w].at[2 * px + py, pl.ds(pc * hh, hh), :], outs[w].at[k - 1], ss.at[7 * w + k - 1], rs.at[7 * w + k - 1], peer)

    def start(ins, outs, ss, rs):
        for w in range(nw):
            for k, peer in _peers(*_place()):
                copy(ins, outs, ss, rs, w, k, peer).start()

    def finish(ins, outs, ss, rs):
        for w in range(nw):
            for k, peer in _peers(*_place()):
                copy(ins, outs, ss, rs, w, k, peer).wait()

    return _Rider(ins=tuple(blocks),
                  out_shapes=tuple(jax.ShapeDtypeStruct((N_DEV - 1, b.shape[1] // 2, b.shape[2]), b.dtype) for b in blocks),
                  n_sems=7 * nw, phases=((0.0, start), (1.0, finish)))


def _reduce_add(own, recv, cidx, *, name):
    R, W = own.shape
    hh = R // 2
    tr = _row_tile(hh, W, budget=1024 * 1024)
    nb = hh // tr

    def body(c_ref, o_ref, r_ref, t_ref):
        s = o_ref[...]
        for k in range(N_DEV - 1):
            s = s + r_ref[k].astype(F32)
        t_ref[...] = s

    return pl.pallas_call(
        body, name=name,
        grid_spec=pltpu.PrefetchScalarGridSpec(
            num_scalar_prefetch=1, grid=(nb,),
            in_specs=[pl.BlockSpec((tr, W), lambda i, c_ref: (i + c_ref[0] * nb, 0)),
                      pl.BlockSpec((N_DEV - 1, tr, W), lambda i, c_ref: (0, i, 0))],
            out_specs=pl.BlockSpec((tr, W), lambda i, c_ref: (i, 0))),
        out_shape=jax.ShapeDtypeStruct((hh, W), F32),
        compiler_params=_params(("parallel",)),
    )(cidx, own, recv)


def _halves_exchange(ts):
    nw = len(ts)

    def body(*refs):
        t_refs, out_refs = refs[:nw], refs[nw:2 * nw]
        send_sems, recv_sems = refs[2 * nw:]
        x, y, c = _place()
        cps = []
        for w in range(nw):
            cp = _remote(t_refs[w], out_refs[w], send_sems.at[w], recv_sems.at[w], (x, y, 1 - c))
            cp.start()
            cps.append(cp)
        for cp in cps:
            cp.wait()

    return pl.pallas_call(
        body, name="grad_halves_exchange", in_specs=[ANY] * nw, out_specs=[ANY] * nw,
        out_shape=[jax.ShapeDtypeStruct(t.shape, t.dtype) for t in ts],
        scratch_shapes=[pltpu.SemaphoreType.DMA((nw,)), pltpu.SemaphoreType.DMA((nw,))],
    )(*ts)


def _adam_math(w, g, m, v):
    nm = ADAM_B1 * m + (1.0 - ADAM_B1) * g
    nv = ADAM_B2 * v + (1.0 - ADAM_B2) * (g * g)
    c1 = 1.0 - ADAM_B1 ** ADAM_STEP
    c2 = 1.0 - ADAM_B2 ** ADAM_STEP
    return -ADAM_LR * ((nm / c1) / (jnp.sqrt(nv / c2) + ADAM_EPS) + ADAM_WD * w), nm, nv


STAGE_ROWS = 32
STAGE_VEC = {"g_pre": 0, "a_ln_g": 1, "a_ln_b": 2, "mem_norm_g": 3, "g_post": 4}
STAGE_BGATE = 5
STAGE_MIX = 8
STAGE_ABS = 16


def _small_step(g, loss_row, w, m, v):
    n = len(SMALL)

    def reduce_body(*refs):
        g_r = dict(zip(SMALL, refs[:n]))
        loss_r = refs[n]
        sa_o, sw_o = refs[n + 1], refs[n + 2]
        stage, ga, gw, send_sems, recv_sems = refs[n + 3:]

        stage[...] = jnp.zeros_like(stage)
        for name, row in STAGE_VEC.items():
            stage[row:row + 1, :] = g_r[name][...]
        for t in range(3):
            stage[STAGE_BGATE + t:STAGE_BGATE + t + 1, :] = g_r["b_gate"][:, t * D:(t + 1) * D]
        stage[STAGE_MIX:STAGE_MIX + 1, 0:LORA] = g_r["q_norm_g"][...]
        stage[STAGE_MIX:STAGE_MIX + 1, LORA:2 * LORA] = g_r["kv_norm_g"][...]
        stage[STAGE_MIX:STAGE_MIX + 1, 2 * LORA:2 * LORA + LANES] = loss_r[...]
        stage[STAGE_ABS:STAGE_ABS + A_GROUPS, 0:CHUNK] = g_r["a_b_s"][...]

        x, y, c = _place()
        me = 4 * x + 2 * y + c
        ga[me] = stage[...]
        gw[me] = g_r["a_w_s"][...]
        cps, srcs = [], []
        for k in range(1, N_DEV):
            fx, fy, fc = (k >> 2) & 1, (k >> 1) & 1, k & 1
            peer = (1 - x if fx else x, 1 - y if fy else y, 1 - c if fc else c)
            for j, (src, dst) in enumerate(((stage, ga), (g_r["a_w_s"], gw))):
                cp = _remote(src, dst.at[me], send_sems.at[2 * (k - 1) + j], recv_sems.at[2 * (k - 1) + j], peer)
                cp.start()
                cps.append(cp)
            srcs.append(4 * peer[0] + 2 * peer[1] + peer[2])
        for k, src in enumerate(srcs):
            _remote(stage, ga.at[src], send_sems.at[2 * k], recv_sems.at[2 * k], (x, y, c)).wait_recv()
            _remote(g_r["a_w_s"], gw.at[src], send_sems.at[2 * k + 1], recv_sems.at[2 * k + 1], (x, y, c)).wait_recv()
        for cp in cps:
            cp.wait_send()
        sa, sw = ga[0], gw[0]
        for d in range(1, N_DEV):
            sa = sa + ga[d]
            sw = sw + gw[d]
        sa_o[...] = sa
        sw_o[...] = sw

    def update_body(*refs):
        sa, sw = refs[0][...], refs[1][...]
        w_r = dict(zip(SMALL, refs[2:n + 2]))
        m_r = dict(zip(SMALL, refs[n + 2:2 * n + 2]))
        v_r = dict(zip(SMALL, refs[2 * n + 2:3 * n + 2]))
        outs = refs[3 * n + 2:7 * n + 2]
        o_r = {name: outs[4 * i:4 * i + 4] for i, name in enumerate(SMALL)}
        loss_o = refs[7 * n + 2]

        def update(name, gsum, cols=None):
            sel = (slice(None), cols) if cols is not None else Ellipsis
            delta, nm, nv = _adam_math(w_r[name][sel], gsum, m_r[name][sel], v_r[name][sel])
            for ref, val in zip(o_r[name], (gsum, delta, nm, nv)):
                ref[sel] = val

        for name, row in STAGE_VEC.items():
            update(name, sa[row:row + 1, :])
        for t in range(3):
            update("b_gate", sa[STAGE_BGATE + t:STAGE_BGATE + t + 1, :], slice(t * D, (t + 1) * D))
        update("q_norm_g", sa[STAGE_MIX:STAGE_MIX + 1, 0:LORA])
        update("kv_norm_g", sa[STAGE_MIX:STAGE_MIX + 1, LORA:2 * LORA])
        update("a_b_s", sa[STAGE_ABS:STAGE_ABS + A_GROUPS, 0:CHUNK])
        update("a_w_s", sw)
        loss_o[...] = sa[STAGE_MIX:STAGE_MIX + 1, 2 * LORA:2 * LORA + LANES]

    vm = pl.BlockSpec(memory_space=pltpu.VMEM)
    sa, sw = pl.pallas_call(
        reduce_body, name="small_allreduce", in_specs=[vm] * (n + 1), out_specs=[vm, vm],
        out_shape=[jax.ShapeDtypeStruct((STAGE_ROWS, D), F32), jax.ShapeDtypeStruct((A_GROUPS, CHUNK, CHUNK), F32)],
        scratch_shapes=[pltpu.VMEM((STAGE_ROWS, D), F32), pltpu.VMEM((N_DEV, STAGE_ROWS, D), F32),
                        pltpu.VMEM((N_DEV, A_GROUPS, CHUNK, CHUNK), F32),
                        pltpu.SemaphoreType.DMA((2 * (N_DEV - 1),)), pltpu.SemaphoreType.DMA((2 * (N_DEV - 1),))],
        compiler_params=pltpu.CompilerParams(vmem_limit_bytes=VMEM_LIMIT),
    )(*[g[k] for k in SMALL], loss_row)
    ins = [sa, sw] + [w[k] for k in SMALL] + [m[k] for k in SMALL] + [v[k] for k in SMALL]
    out_shape = [jax.ShapeDtypeStruct(w[k].shape, F32) for k in SMALL for _ in range(4)] + [jax.ShapeDtypeStruct((1, LANES), F32)]
    res = pl.pallas_call(
        update_body, name="small_adamw", in_specs=[vm] * len(ins), out_specs=[vm] * len(out_shape), out_shape=out_shape,
        compiler_params=pltpu.CompilerParams(vmem_limit_bytes=VMEM_LIMIT),
    )(*ins)
    return {k: tuple(res[4 * i:4 * i + 4]) for i, k in enumerate(SMALL)}, res[-1]


SHARD_2D = {"w_in": (D, IN_REF // N_CHIPS), "w_uq": (LORA, HEADS * QK_DIM // N_CHIPS),
            "w_ukv": (LORA, HEADS * (QK_NOPE + V_DIM) // N_CHIPS), "w_mem_kv": (D, 2 * D // N_CHIPS),
            "w_gate": (D, 3 * D // N_CHIPS), "w_branch": (3 * D // N_CHIPS, D), "w_out": (D // N_CHIPS, D)}


def _cols(blocks):
    return jnp.concatenate([blocks[j] for j in range(N_CHIPS)], axis=1)


def _w_in_layout(gathered):
    w = _cols(gathered)
    return jnp.concatenate([w[:, :3 * D], w[:, 3 * D + 2 * LORA + QK_ROPE:], w[:, 3 * D:3 * D + 2 * LORA + QK_ROPE],
                            jnp.zeros((D, IN_PAD - IN_REF), w.dtype)], axis=1)


REST = BIG[1:]


def _rest_layouts(gathered):
    wq = jnp.pad(_cols(gathered["w_uq"]).reshape(LORA, HEADS, QK_DIM), ((0, 0), (0, 0), (0, QK_PAD - QK_DIM))).reshape(LORA, HEADS * QK_PAD)
    kv3 = _cols(gathered["w_ukv"]).reshape(LORA, HEADS, QK_NOPE + V_DIM)
    wk = jnp.pad(kv3[:, :, :QK_NOPE], ((0, 0), (0, 0), (0, QK_PAD - QK_NOPE))).reshape(LORA, HEADS * QK_PAD)
    wv = kv3[:, :, QK_NOPE:].reshape(LORA, HEADS * V_DIM)
    w_branch = gathered["w_branch"].reshape(N_CHIPS, 3, D // N_CHIPS, D).transpose(1, 0, 2, 3).reshape(3, D, D)
    return {"wq": wq, "wk": wk, "wv": wv, "w_mem_kv": _cols(gathered["w_mem_kv"]), "w_gate": _cols(gathered["w_gate"]),
            "w_branch": w_branch, "w_out": gathered["w_out"].reshape(D, D)}


def _grad_reference_layout(name, g):
    if name == "w_in":
        return jnp.concatenate([g[:, :3 * D], g[:, 6 * D:6 * D + 2 * LORA + QK_ROPE], g[:, 3 * D:6 * D]], axis=1)
    if name == "w_uq":
        return g.reshape(LORA, HEADS, QK_PAD)[:, :, :QK_DIM].reshape(LORA, HEADS * QK_DIM)
    if name == "w_ukv":
        gk, gv = g
        return jnp.concatenate([gk.reshape(LORA, HEADS, QK_PAD)[:, :, :QK_NOPE], gv.reshape(LORA, HEADS, V_DIM)],
                               axis=2).reshape(LORA, HEADS * (QK_NOPE + V_DIM))
    return g


def _grad_blocks(name, full):
    own = 2 * lax.axis_index("x") + lax.axis_index("y")
    R, C = SHARD_2D[name]
    if name == "w_branch":
        blocks = full.reshape(3, N_CHIPS, D // N_CHIPS, D).transpose(1, 0, 2, 3).reshape(N_CHIPS, R, C)
        mine = lax.dynamic_slice_in_dim(full, own * (D // N_CHIPS), D // N_CHIPS, axis=1).reshape(R, C)
    elif name == "w_out":
        blocks = full.reshape(N_CHIPS, R, C)
        mine = lax.dynamic_slice_in_dim(full, own * R, R, axis=0)
    else:
        blocks = full.reshape(R, N_CHIPS, C).transpose(1, 0, 2)
        mine = lax.dynamic_slice_in_dim(full, own * C, C, axis=1)
    return blocks.astype(BF16), mine


def _local_step(x, mem, pos_col, target, w_in, rest_shards, P):
    cidx = lax.axis_index("c").astype(jnp.int32).reshape(1)
    h16 = _rms_fwd(x, P["g_pre"], width=D, col=0, tm=256, name="pre_norm")
    memn16 = _rms_fwd(mem, P["mem_norm_g"], width=D, col=0, tm=256, name="mem_norm")
    proj, rest = _mm(h16, w_in, "nn", tm=512, tn=1920, tk=D, out_dtype=BF16, name="in_proj", rider=_gather_rider(rest_shards))
    W = _rest_layouts(dict(zip(REST, _own_blocks_in_place(rest, rest_shards))))

    causal = jnp.tril(jnp.ones((CHUNK, CHUNK), F32))
    wm = (P["a_w_s"] * causal[None]).astype(BF16)
    bs_t = P["a_b_s"].T
    ya = _gmlp_fwd(proj, P["a_ln_g"], P["a_ln_b"], wm, bs_t)

    inv = 1.0 / (ROPE_THETA ** (jnp.arange(0, QK_ROPE, 2, dtype=F32) / QK_ROPE))
    inv_lane = jnp.concatenate([inv, inv, jnp.zeros((LANES - QK_ROPE,), F32)])[None]
    tabs = _rope_tables(pos_col, inv_lane, tm=1024)
    cqn = _rms_fwd(proj, P["q_norm_g"], width=LORA, col=COL_CQ, tm=512, name="q_norm")
    ckvn = _rms_fwd(proj, P["kv_norm_g"], width=LORA, col=COL_CKV, tm=512, name="kv_norm")
    q16, k16, v16 = _mla_proj(cqn, ckvn, proj, tabs, W["wq"], W["wk"], W["wv"], tm=256)
    o_b, yb, lse = _mla_fwd(q16, k16, v16, proj, t=512)

    kvm = _mm(memn16, W["w_mem_kv"], "nn", tm=256, tn=1024, tk=D, out_dtype=BF16, name="mem_kv")
    ym = _mem_fwd(proj, kvm, tm=512)

    wbs = [W["w_branch"][n] for n in range(3)]
    merged, g0, g1, g2, p0, p1, p2 = _gate_merge(h16, (ya, yb, ym), W["w_gate"], P["b_gate"], wbs, tm=512, tn=512)
    out = _mm(merged, W["w_out"], "nn", tm=512, tn=1024, tk=D, out_dtype=F32, name="out_proj")
    dy, dout, g_g_post, loss = _post_loss(x, out, target, P["g_post"], tm=256)

    full = {}
    full["w_out"] = _mm(merged, dout, "tn", tm=1024, tn=1024, tk=TN_TK, out_dtype=F32, name="gw_out")
    dmerged = _mm(dout, W["w_out"], "nt", tm=512, tn=1024, tk=D, out_dtype=BF16, name="d_merged")
    dp0, dp1, dp2, dgpre, g_b_gate = _gate_bwd(dmerged, (g0, g1, g2), (p0, p1, p2), tm=256)
    full["w_gate"] = _mm(h16, dgpre, "tn", tm=1024, tn=1024, tk=TN_TK, out_dtype=F32, name="gw_gate")
    dh_gate = _mm(dgpre, W["w_gate"], "nt", tm=1024, tn=1024, tk=3 * D // 2, out_dtype=F32, name="dh_gate")
    full["w_branch"] = jnp.stack([_mm(y, dp, "tn", tm=1024, tn=1024, tk=TN_TK, out_dtype=F32, name=f"gw_branch{n}")
                                  for n, (y, dp) in enumerate(((ya, dp0), (yb, dp1), (ym, dp2)))], axis=0)
    dya, dyb, dym = [_mm(dp, wbs[n], "nt", tm=512, tn=1024, tk=D, out_dtype=BF16, name=f"dy_branch{n}")
                     for n, dp in enumerate((dp0, dp1, dp2))]

    dqm, dzm, dkvm = _mem_bwd(proj, kvm, dym, tm=512)
    dkvm16 = dkvm.astype(BF16)
    full["w_mem_kv"] = _mm(memn16, dkvm16, "tn", tm=1024, tn=1024, tk=256, out_dtype=F32, name="gw_mem_kv")
    dmemn = _mm(dkvm16, W["w_mem_kv"], "nt", tm=256, tn=1024, tk=2 * D, out_dtype=F32, name="d_memn")
    _, g_mem_norm = _rms_bwd(dmemn, mem, P["mem_norm_g"], width=D, col=0, tm=256, out_dtype=BF16, name="mem_norm_bwd")

    own, recv = {}, {}
    early = ("w_out", "w_gate", "w_branch", "w_mem_kv")
    early_blocks = []
    for n in early:
        blocks, own[n] = _grad_blocks(n, full[n])
        early_blocks.append(blocks)
    do16, dzb, stats = _mla_gate_bwd(dyb, o_b, proj, lse, tm=512)
    dq, dk16, dv16, landed = _mla_bwd(q16, k16, v16, do16, stats, t=512, rider=_exchange_rider(early_blocks))
    recv.update(zip(early, landed))
    dq16, dkr = _mla_qk_post(dq, dk16, tabs, tm=256)
    g_wq = _mm(cqn, dq16, "tn", tm=512, tn=1024, tk=TN_TK, out_dtype=F32, name="gw_uq")
    g_wk = _mm(ckvn, dk16, "tn", tm=512, tn=1024, tk=TN_TK, out_dtype=F32, name="gw_uk")
    g_wv = _mm(ckvn, dv16, "tn", tm=512, tn=1024, tk=TN_TK, out_dtype=F32, name="gw_uv")
    dcqn = _mm(dq16, W["wq"], "nt", tm=512, tn=LORA, tk=HEADS * QK_PAD, out_dtype=F32, name="d_cqn")
    dckvn_k = _mm(dk16, W["wk"], "nt", tm=512, tn=LORA, tk=HEADS * QK_PAD, out_dtype=F32, name="d_ckvn_k")
    dckvn = _mm(dv16, W["wv"], "nt", tm=512, tn=LORA, tk=HEADS * V_DIM, out_dtype=F32, name="d_ckvn", add=dckvn_k)
    dcq, g_q_norm = _rms_bwd(dcqn, proj, P["q_norm_g"], width=LORA, col=COL_CQ, tm=512, out_dtype=BF16, name="q_norm_bwd")
    dckv, g_kv_norm = _rms_bwd(dckvn, proj, P["kv_norm_g"], width=LORA, col=COL_CKV, tm=512, out_dtype=BF16, name="kv_norm_bwd")

    dproj, gws, dsv_sum, g_ln_g, g_ln_b = _gmlp_bwd(proj, dya, P["a_ln_g"], P["a_ln_b"], wm, bs_t, (dzb, dqm, dzm, dcq, dckv, dkr))
    g_a_w_s = gws * causal[None]
    g_a_b_s = dsv_sum.reshape(CHUNK, A_GROUPS, CHUNK).sum(axis=-1).T

    mid = ("w_uq", "w_ukv")
    mid_blocks = []
    for n, g in (("w_uq", g_wq), ("w_ukv", (g_wk, g_wv))):
        blocks, own[n] = _grad_blocks(n, _grad_reference_layout(n, g))
        mid_blocks.append(blocks)
    g_w_in, landed = _mm(h16, dproj, "tn", tm=1024, tn=896, tk=TN_TK, out_dtype=F32, name="gw_in", rider=_exchange_rider(mid_blocks))
    recv.update(zip(mid, landed))
    in_blocks, own["w_in"] = _grad_blocks("w_in", _grad_reference_layout("w_in", g_w_in))
    dh, landed = _mm(dproj, w_in, "nt", tm=1024, tn=1024, tk=2688, out_dtype=F32, name="d_h", add=dh_gate, rider=_exchange_rider([in_blocks]))
    recv["w_in"] = landed[0]
    grad_x, g_g_pre = _rms_bwd(dh, x, P["g_pre"], width=D, col=0, tm=512, out_dtype=F32, name="pre_norm_bwd", residual=dy)

    totals = [_reduce_add(own[n], recv[n], cidx, name=f"grad_reduce_{n}") for n in BIG]
    small = {"g_pre": g_g_pre, "a_ln_g": g_ln_g, "a_ln_b": g_ln_b, "a_w_s": g_a_w_s, "a_b_s": g_a_b_s,
             "q_norm_g": g_q_norm, "kv_norm_g": g_kv_norm, "mem_norm_g": g_mem_norm, "b_gate": g_b_gate, "g_post": g_g_post}
    return loss, grad_x, totals, small


def kernel(x, mem, positions, g_pre, w_in, a_ln_g, a_ln_b, a_w_s, a_b_s, q_norm_g, w_uq, kv_norm_g, w_ukv, mem_norm_g, w_mem_kv, w_gate, b_gate, w_branch, w_out, g_post, loss_target, m_g_pre, m_w_in, m_a_ln_g, m_a_ln_b, m_a_w_s, m_a_b_s, m_q_norm_g, m_w_uq, m_kv_norm_g, m_w_ukv, m_mem_norm_g, m_w_mem_kv, m_w_gate, m_b_gate, m_w_branch, m_w_out, m_g_post, v_g_pre, v_w_in, v_a_ln_g, v_a_ln_b, v_a_w_s, v_a_b_s, v_q_norm_g, v_w_uq, v_kv_norm_g, v_w_ukv, v_mem_norm_g, v_w_mem_kv, v_w_gate, v_b_gate, v_w_branch, v_w_out, v_g_post):
    w = dict(g_pre=g_pre, w_in=w_in, a_ln_g=a_ln_g, a_ln_b=a_ln_b, a_w_s=a_w_s, a_b_s=a_b_s, q_norm_g=q_norm_g, w_uq=w_uq,
             kv_norm_g=kv_norm_g, w_ukv=w_ukv, mem_norm_g=mem_norm_g, w_mem_kv=w_mem_kv, w_gate=w_gate, b_gate=b_gate,
             w_branch=w_branch, w_out=w_out, g_post=g_post)
    m = dict(g_pre=m_g_pre, w_in=m_w_in, a_ln_g=m_a_ln_g, a_ln_b=m_a_ln_b, a_w_s=m_a_w_s, a_b_s=m_a_b_s, q_norm_g=m_q_norm_g,
             w_uq=m_w_uq, kv_norm_g=m_kv_norm_g, w_ukv=m_w_ukv, mem_norm_g=m_mem_norm_g, w_mem_kv=m_w_mem_kv, w_gate=m_w_gate,
             b_gate=m_b_gate, w_branch=m_w_branch, w_out=m_w_out, g_post=m_g_post)
    v = dict(g_pre=v_g_pre, w_in=v_w_in, a_ln_g=v_a_ln_g, a_ln_b=v_a_ln_b, a_w_s=v_a_w_s, a_b_s=v_a_b_s, q_norm_g=v_q_norm_g,
             w_uq=v_w_uq, kv_norm_g=v_kv_norm_g, w_ukv=v_w_ukv, mem_norm_g=v_mem_norm_g, w_mem_kv=v_w_mem_kv, w_gate=v_w_gate,
             b_gate=v_b_gate, w_branch=v_w_branch, w_out=v_w_out, g_post=v_g_post)

    def two_d(t, n):
        return t[n].reshape(SHARD_2D[n]) if n in SHARD_2D else t[n].reshape(t[n].shape[1:] if t[n].ndim > 2 else t[n].shape)

    shards = [two_d(w, n).astype(BF16) for n in BIG]
    w_in_full = _w_in_layout(_allgather_chips(shards[:1])[0])
    P = {n: two_d(w, n) for n in SMALL}

    S = x.shape[1]
    loss_row, grad_x, totals, small = _local_step(x[0], mem[0], positions.reshape(S, 1), loss_target[0], w_in_full, shards[1:], P)

    from_sibling = _halves_exchange(totals)
    cidx = lax.axis_index("c").astype(jnp.int32).reshape(1)
    res = {}
    for n, own, other in zip(BIG, totals, from_sibling):
        upd = _adamw(two_d(w, n), own, other, two_d(m, n), two_d(v, n), cidx, name=f"adamw_{n}")
        for key, t in zip(("grad", "delta", "new_m", "new_v"), upd):
            res[key, n] = t.reshape(w[n].shape)

    small_out, loss_sum = _small_step(small, loss_row, P, {n: two_d(m, n) for n in SMALL}, {n: two_d(v, n) for n in SMALL})
    for n in SMALL:
        for key, t in zip(("grad", "delta", "new_m", "new_v"), small_out[n]):
            res[key, n] = t.reshape(w[n].shape)
    loss = loss_sum[0, 0]

    outs = [loss, grad_x[None]]
    for key in ("grad", "delta", "new_m", "new_v"):
        outs += [res[key, n] for n in WEIGHTS]
    return tuple(outs)
```

```python
import math
from typing import NamedTuple

import jax
import jax.numpy as jnp
from jax import lax
from jax.experimental import pallas as pl
from jax.experimental.pallas import tpu as pltpu

F32 = jnp.float32
BF16 = jnp.bfloat16

D = 2048
EPS = 1e-6
CHUNK = 128
A_GROUPS = 16
HEADS = 16
QK_NOPE = 128
QK_ROPE = 64
QK_DIM = QK_NOPE + QK_ROPE
V_DIM = 128
LORA = 512
MEM_HEADS = 4
MEM_HEAD_DIM = 512
ROPE_THETA = 10000.0
QK_PAD = 256
IN_REF = 13376
IN_PAD = 13440
COL_U, COL_V, COL_ZA, COL_ZB, COL_QM, COL_ZM = 0, 1, 2, 3, 4, 5
COL_CQ, COL_CKV = 24, 25
COL_KR = 104

ADAM_LR = 0.001
ADAM_B1 = 0.9
ADAM_B2 = 0.999
ADAM_EPS = 1e-08
ADAM_WD = 0.01
ADAM_STEP = 10

VMEM_LIMIT = 56 * 1024 * 1024
LANES = 128
LOG2E = math.log2(math.e)

BIG = ("w_in", "w_uq", "w_ukv", "w_mem_kv", "w_gate", "w_branch", "w_out")
SMALL = ("g_pre", "a_ln_g", "a_ln_b", "a_w_s", "a_b_s", "q_norm_g", "kv_norm_g", "mem_norm_g", "b_gate", "g_post")
WEIGHTS = ("g_pre", "w_in", "a_ln_g", "a_ln_b", "a_w_s", "a_b_s", "q_norm_g", "w_uq", "kv_norm_g", "w_ukv",
           "mem_norm_g", "w_mem_kv", "w_gate", "b_gate", "w_branch", "w_out", "g_post")
N_CHIPS = 4
N_DEV = 8


def _params(sem=None):
    return pltpu.CompilerParams(dimension_semantics=sem, vmem_limit_bytes=VMEM_LIMIT)


def _sigmoid(z):
    return 1.0 / (1.0 + jnp.exp(-z))


def _gelu_parts(x):
    c = math.sqrt(2.0 / math.pi)
    x2 = x * x
    t = jnp.tanh(c * (x + 0.044715 * x * x2))
    g = 0.5 * x * (1.0 + t)
    dg = 0.5 * (1.0 + t) + 0.5 * x * (1.0 - t * t) * (c * (1.0 + 3.0 * 0.044715 * x2))
    return g, dg


def _silu_parts(z):
    s = _sigmoid(z)
    return z * s, s * (1.0 + z * (1.0 - s))


def _dot(a, b, dims):
    return lax.dot_general(a, b, (dims, ((), ())), preferred_element_type=F32)


NN = ((1,), (0,))
NT = ((1,), (1,))
TN = ((0,), (0,))
TN_TK = 2048


class _Rider(NamedTuple):
    ins: tuple
    out_shapes: tuple
    n_sems: int
    phases: tuple


def _ride(rider, refs_in, refs_out, sems, step, total):
    for frac, fn in rider.phases:
        @pl.when(step == int(frac * (total - 1)))
        def _():
            fn(refs_in, refs_out, sems[0], sems[1])


def _mm(a, b, mode, *, tm, tn, tk, out_dtype, name, add=None, rider=None):
    if mode == "nn":
        (M, K), (_, N) = a.shape, b.shape
    elif mode == "nt":
        (M, K), (N, _) = a.shape, b.shape
    else:
        (K, M), (_, N) = a.shape, b.shape
    tm, tn, tk = min(tm, M), min(tn, N), min(tk, K)
    assert M % tm == 0 and N % tn == 0 and K % tk == 0, (name, M, N, K, tm, tn, tk)
    ni, nj, nk = M // tm, N // tn, K // tk
    dims = {"nn": NN, "nt": NT, "tn": TN}[mode]
    has_add = add is not None
    n_rin = len(rider.ins) if rider else 0
    n_rout = len(rider.out_shapes) if rider else 0

    def body(*refs):
        a_ref, b_ref = refs[0], refs[1]
        pos = 2
        add_ref = refs[pos] if has_add else None
        pos += int(has_add)
        rin = refs[pos:pos + n_rin]
        pos += n_rin
        o_ref = refs[pos]
        rout = refs[pos + 1:pos + 1 + n_rout]
        pos += 1 + n_rout
        acc = refs[pos] if nk > 1 else None
        sems = refs[-2:] if rider else None
        if rider:
            step = (pl.program_id(0) * ni + pl.program_id(1)) * nk + pl.program_id(2)
            _ride(rider._replace(phases=rider.phases[:1]), rin, rout, sems, step, nj * ni * nk)
        part = _dot(a_ref[...].astype(BF16), b_ref[...].astype(BF16), dims)

        def finish(r):
            if has_add:
                r = r + add_ref[...]
            o_ref[...] = r.astype(out_dtype)

        if nk == 1:
            finish(part)
        else:
            k = pl.program_id(2)

            @pl.when(k == 0)
            def _():
                acc[...] = part

            @pl.when(k > 0)
            def _():
                acc[...] += part

            @pl.when(k == nk - 1)
            def _():
                finish(acc[...])

        if rider:
            _ride(rider._replace(phases=rider.phases[1:]), rin, rout, sems, step, nj * ni * nk)

    if mode == "nn":
        a_spec = pl.BlockSpec((tm, tk), lambda j, i, k: (i, k))
        b_spec = pl.BlockSpec((tk, tn), lambda j, i, k: (k, j))
    elif mode == "nt":
        a_spec = pl.BlockSpec((tm, tk), lambda j, i, k: (i, k))
        b_spec = pl.BlockSpec((tn, tk), lambda j, i, k: (j, k))
    else:
        a_spec = pl.BlockSpec((tk, tm), lambda j, i, k: (k, i))
        b_spec = pl.BlockSpec((tk, tn), lambda j, i, k: (k, j))
    o_spec = pl.BlockSpec((tm, tn), lambda j, i, k: (i, j))
    hbm = pl.BlockSpec(memory_space=pl.ANY)
    in_specs = [a_spec, b_spec] + ([o_spec] if has_add else []) + [hbm] * n_rin
    args = (a, b) + ((add,) if has_add else ()) + (tuple(rider.ins) if rider else ())
    scratch = [pltpu.VMEM((tm, tn), F32)] if nk > 1 else []
    if rider:
        scratch += [pltpu.SemaphoreType.DMA((rider.n_sems,)), pltpu.SemaphoreType.DMA((rider.n_sems,))]
    res = pl.pallas_call(
        body, name=name, grid=(nj, ni, nk), in_specs=in_specs, out_specs=[o_spec] + [hbm] * n_rout,
        out_shape=[jax.ShapeDtypeStruct((M, N), out_dtype)] + (list(rider.out_shapes) if rider else []),
        scratch_shapes=scratch,
        compiler_params=_params(("arbitrary",) * 3 if rider else ("parallel", "parallel", "arbitrary")),
    )(*args)
    return (res[0], list(res[1:])) if rider else res[0]


def _rms_fwd(x, g, *, width, col, tm, name):
    rows = x.shape[0]
    tm = min(tm, rows)

    def body(x_ref, g_ref, y_ref):
        xv = x_ref[...].astype(F32)
        r = lax.rsqrt(jnp.mean(xv * xv, axis=-1, keepdims=True) + EPS)
        y_ref[...] = ((xv * r) * g_ref[...]).astype(BF16)

    return pl.pallas_call(
        body, name=name, grid=(rows // tm,),
        in_specs=[pl.BlockSpec((tm, width), lambda i: (i, col)), pl.BlockSpec((1, width), lambda i: (0, 0))],
        out_specs=pl.BlockSpec((tm, width), lambda i: (i, 0)),
        out_shape=jax.ShapeDtypeStruct((rows, width), BF16),
        compiler_params=_params(("parallel",)),
    )(x, g)


def _rms_bwd(d, x, g, *, width, col, tm, out_dtype, name, residual=None):
    rows = d.shape[0]
    tm = min(tm, rows)
    has_res = residual is not None

    def body(*refs):
        d_ref, x_ref, g_ref = refs[:3]
        res_ref = refs[3] if has_res else None
        dx_ref, gg_ref = refs[-2], refs[-1]
        dv = d_ref[...]
        xv = x_ref[...].astype(F32)
        r = lax.rsqrt(jnp.mean(xv * xv, axis=-1, keepdims=True) + EPS)
        n = xv * r

        @pl.when(pl.program_id(0) == 0)
        def _():
            gg_ref[...] = jnp.zeros_like(gg_ref)

        gg_ref[...] += jnp.sum(dv * n, axis=0, keepdims=True)
        gd = dv * g_ref[...]
        dx = r * (gd - n * jnp.mean(gd * n, axis=-1, keepdims=True))
        if has_res:
            dx = dx + res_ref[...]
        dx_ref[...] = dx.astype(out_dtype)

    blk = pl.BlockSpec((tm, width), lambda i: (i, 0))
    in_specs = [blk, pl.BlockSpec((tm, width), lambda i: (i, col)),
                pl.BlockSpec((1, width), lambda i: (0, 0))] + ([blk] if has_res else [])
    args = (d, x, g) + ((residual,) if has_res else ())
    return pl.pallas_call(
        body, name=name, grid=(rows // tm,), in_specs=in_specs,
        out_specs=[blk, pl.BlockSpec((1, width), lambda i: (0, 0))],
        out_shape=[jax.ShapeDtypeStruct((rows, width), out_dtype), jax.ShapeDtypeStruct((1, width), F32)],
        compiler_params=_params(("arbitrary",)),
    )(*args)


def _rope_tables(pos_col, inv_lane, *, tm):
    rows = pos_col.shape[0]
    tm = min(tm, rows)

    def body(p_ref, f_ref, c_ref, s1_ref, s2_ref):
        ang = p_ref[...].astype(F32) * f_ref[...]
        lane = lax.broadcasted_iota(jnp.int32, ang.shape, 1)
        c, s = jnp.cos(ang), jnp.sin(ang)
        half = QK_ROPE // 2
        c_ref[...] = jnp.where(lane < QK_ROPE, c, 0.0)
        s1_ref[...] = jnp.where(lane < half, -s, 0.0)
        s2_ref[...] = jnp.where((lane >= half) & (lane < QK_ROPE), s, 0.0)

    blk = pl.BlockSpec((tm, LANES), lambda i: (i, 0))
    return pl.pallas_call(
        body, name="rope_tables", grid=(rows // tm,),
        in_specs=[pl.BlockSpec((tm, 1), lambda i: (i, 0)), pl.BlockSpec((1, LANES), lambda i: (0, 0))],
        out_specs=[blk, blk, blk], out_shape=[jax.ShapeDtypeStruct((rows, LANES), F32)] * 3,
        compiler_params=_params(("parallel",)),
    )(pos_col, inv_lane)


def _rot(t, c, s1, s2, sign):
    r1 = pltpu.roll(t, LANES - QK_ROPE // 2, 1) * s1
    r2 = pltpu.roll(t, QK_ROPE // 2, 1) * s2
    return t * c + (r1 + r2) if sign > 0 else t * c - (r1 + r2)


def _mla_proj(cqn, ckvn, proj, tabs, wq, wk, wv, *, tm):
    rows = cqn.shape[0]
    tm = min(tm, rows)

    def body(cq_ref, ckv_ref, kr_ref, c_ref, s1_ref, s2_ref, wq_ref, wk_ref, wv_ref, q_ref, k_ref, v_ref):
        c, s1, s2 = c_ref[...], s1_ref[...], s2_ref[...]
        q = _dot(cq_ref[...], wq_ref[...], NN)
        k = _dot(ckv_ref[...], wk_ref[...], NN)
        kpe = _rot(kr_ref[...].astype(F32), c, s1, s2, 1).astype(BF16)
        for h in range(HEADS):
            lo = h * QK_PAD
            q_ref[:, lo:lo + QK_NOPE] = q[:, lo:lo + QK_NOPE].astype(BF16)
            q_ref[:, lo + QK_NOPE:lo + QK_PAD] = _rot(q[:, lo + QK_NOPE:lo + QK_PAD], c, s1, s2, 1).astype(BF16)
            k_ref[:, lo:lo + QK_NOPE] = k[:, lo:lo + QK_NOPE].astype(BF16)
            k_ref[:, lo + QK_NOPE:lo + QK_PAD] = kpe
        v_ref[...] = _dot(ckv_ref[...], wv_ref[...], NN).astype(BF16)

    def row(w):
        return pl.BlockSpec((tm, w), lambda i: (i, 0))

    def whole(w):
        return pl.BlockSpec(w.shape, lambda i: (0, 0))

    return pl.pallas_call(
        body, name="mla_proj", grid=(rows // tm,),
        in_specs=[row(LORA), row(LORA), pl.BlockSpec((tm, LANES), lambda i: (i, COL_KR)), row(LANES), row(LANES), row(LANES),
                  whole(wq), whole(wk), whole(wv)],
        out_specs=[row(HEADS * QK_PAD), row(HEADS * QK_PAD), row(HEADS * V_DIM)],
        out_shape=[jax.ShapeDtypeStruct((rows, HEADS * QK_PAD), BF16), jax.ShapeDtypeStruct((rows, HEADS * QK_PAD), BF16),
                   jax.ShapeDtypeStruct((rows, HEADS * V_DIM), BF16)],
        compiler_params=_params(("parallel",)),
    )(cqn, ckvn, proj, *tabs, wq, wk, wv)


def _mla_fwd(q, k, v, proj, *, t):
    S = q.shape[0]
    t = min(t, S // 2)
    n = S // t
    per = min(4, n)
    scale = QK_DIM ** -0.5

    def body(q_ref, k_ref, v_ref, z_ref, o_ref, y_ref, lse_ref):
        qi = pl.program_id(1)
        qv = q_ref[...]
        c2 = scale * LOG2E

        def block(k0, width, carry, row0):
            m_old, l_old, acc = carry
            ks = pl.ds(pl.multiple_of(k0, t), width)
            s = _dot(qv, k_ref[ks, :], NT)
            if row0 is not None:
                r = lax.broadcasted_iota(jnp.int32, s.shape, 0)
                c = lax.broadcasted_iota(jnp.int32, s.shape, 1)
                s = jnp.where(c <= r + row0, s, -1e30)
            m_new = jnp.maximum(m_old, jnp.max(s, axis=-1, keepdims=True))
            alpha = jnp.exp2((m_old - m_new) * c2)
            p = jnp.exp2((s - m_new) * c2)
            l_new = alpha * l_old + jnp.sum(p, axis=-1, keepdims=True)
            acc = alpha * acc + _dot(p.astype(BF16), v_ref[ks, :], NN)
            return m_new, l_new, acc

        init = (jnp.full((t, 1), -1e30, F32), jnp.zeros((t, 1), F32), jnp.zeros((t, V_DIM), F32))
        carry = lax.fori_loop(0, qi // per, lambda j, cr: block(j * (per * t), per * t, cr, None), init)
        last = [lambda cr, w=w: block((qi - w) * t, (w + 1) * t, cr, w * t) for w in range(per)]
        m_f, l_f, acc = lax.switch(qi % per, last, carry)
        o = acc / l_f
        o_ref[...] = o
        sz, _ = _silu_parts(z_ref[...].astype(F32))
        y_ref[...] = (o * sz).astype(BF16)
        lse2 = (m_f * scale + jnp.log(l_f)) * LOG2E
        lane = lax.broadcasted_iota(jnp.int32, (t, LANES), 1)
        lse_ref[0, 0] = jnp.where(lane == 0, lse2, 0.0).T[0:8, :]

    zcol = COL_ZB * (D // V_DIM)
    return pl.pallas_call(
        body, name="mla_fwd", grid=(HEADS, n),
        in_specs=[pl.BlockSpec((t, QK_PAD), lambda h, i: (i, h)),
                  pl.BlockSpec((S, QK_PAD), lambda h, i: (0, h)),
                  pl.BlockSpec((S, V_DIM), lambda h, i: (0, h)),
                  pl.BlockSpec((t, V_DIM), lambda h, i: (i, zcol + h))],
        out_specs=[pl.BlockSpec((t, V_DIM), lambda h, i: (i, h)), pl.BlockSpec((t, V_DIM), lambda h, i: (i, h)),
                   pl.BlockSpec((1, 1, 8, t), lambda h, i: (h, i, 0, 0))],
        out_shape=[jax.ShapeDtypeStruct((S, HEADS * V_DIM), F32), jax.ShapeDtypeStruct((S, HEADS * V_DIM), BF16),
                   jax.ShapeDtypeStruct((HEADS, n, 8, t), F32)],
        compiler_params=_params(("parallel", "parallel")),
    )(q, k, v, proj)


def _mla_gate_bwd(dy, o, proj, lse, *, tm):
    S = dy.shape[0]
    tm = min(tm, S)

    def body(dy_ref, o_ref, z_ref, lse_ref, do_ref, dz_ref, st_ref):
        sz, dsz = _silu_parts(z_ref[...].astype(F32))
        dyv, ov = dy_ref[...].astype(F32), o_ref[...]
        do = dyv * sz
        do_ref[...] = do.astype(BF16)
        dz_ref[...] = (dyv * ov * dsz).astype(BF16)
        prod = do * ov
        lane = lax.broadcasted_iota(jnp.int32, (tm, LANES), 1)
        for h in range(HEADS):
            delta = jnp.sum(prod[:, h * V_DIM:(h + 1) * V_DIM], axis=-1, keepdims=True)
            st_ref[h, 0] = lse_ref[h, 0] + jnp.where(lane == 1, delta, 0.0).T[0:8, :]

    blk = pl.BlockSpec((tm, D), lambda i: (i, 0))
    return pl.pallas_call(
        body, name="mla_gate_bwd", grid=(S // tm,),
        in_specs=[blk, blk, pl.BlockSpec((tm, D), lambda i: (i, COL_ZB)), pl.BlockSpec((HEADS, 1, 8, tm), lambda i: (0, i, 0, 0))],
        out_specs=[blk, blk, pl.BlockSpec((HEADS, 1, 8, tm), lambda i: (0, i, 0, 0))],
        out_shape=[jax.ShapeDtypeStruct((S, D), BF16), jax.ShapeDtypeStruct((S, D), BF16),
                   jax.ShapeDtypeStruct((HEADS, S // tm, 8, tm), F32)],
        compiler_params=_params(("parallel",)),
    )(dy, o, proj, lse)


def _mla_bwd(q, k, v, do, stats, *, t, rider):
    S = q.shape[0]
    t = min(t, S)
    n = S // t
    per = min(4, n)
    c2 = (QK_DIM ** -0.5) * LOG2E

    n_rin, n_rout = len(rider.ins), len(rider.out_shapes)

    def body(*refs):
        q_ref, k_ref, v_ref, do_ref, st_ref = refs[:5]
        rin = refs[5:5 + n_rin]
        dq_ref, dk_ref, dv_ref = refs[5 + n_rin:8 + n_rin]
        rout = refs[8 + n_rin:8 + n_rin + n_rout]
        ki = pl.program_id(1)
        step = pl.program_id(0) * n + ki
        _ride(rider._replace(phases=rider.phases[:1]), rin, rout, refs[-2:], step, HEADS * n)

        @pl.when(ki == 0)
        def _():
            dq_ref[...] = jnp.zeros_like(dq_ref)

        kv, vv = k_ref[...], v_ref[...]

        def block(i, carry, diag):
            dk, dv = carry
            rows = pl.ds(pl.multiple_of(i * t, t), t)
            qv, dov, st = q_ref[rows, :], do_ref[rows, :], st_ref[0, i]
            s = _dot(kv, qv, NT)
            if diag:
                key = lax.broadcasted_iota(jnp.int32, s.shape, 0)
                qry = lax.broadcasted_iota(jnp.int32, s.shape, 1)
                s = jnp.where(key <= qry, s, -1e30)
            p = jnp.exp2(s * c2 - st[0:1, :])
            p16 = p.astype(BF16)
            dv = dv + _dot(p16, dov, NN)
            dp = _dot(vv, dov, NT)
            ds = (p * (dp - st[1:2, :])).astype(BF16)
            dk = dk + _dot(ds, qv, NN)
            dq_ref[rows, :] += _dot(ds, kv, TN)
            return dk, dv

        carry = block(ki, (jnp.zeros((t, QK_PAD), F32), jnp.zeros((t, V_DIM), F32)), True)
        rest = n - 1 - ki

        def run(start, count):
            def f(cr):
                for u in range(count):
                    cr = block(start + u, cr, False)
                return cr
            return f

        carry = lax.switch(rest % per, [run(ki + 1, w) for w in range(per)], carry)
        first = ki + 1 + rest % per
        dk, dv = lax.fori_loop(0, rest // per, lambda i, cr: run(first + per * i, per)(cr), carry)
        dk_ref[...] = (dk * (QK_DIM ** -0.5)).astype(BF16)
        dv_ref[...] = dv.astype(BF16)
        _ride(rider._replace(phases=rider.phases[1:]), rin, rout, refs[-2:], step, HEADS * n)

    hbm = pl.BlockSpec(memory_space=pl.ANY)
    res = pl.pallas_call(
        body, name="mla_bwd", grid=(HEADS, n),
        in_specs=[pl.BlockSpec((S, QK_PAD), lambda h, j: (0, h)),
                  pl.BlockSpec((t, QK_PAD), lambda h, j: (j, h)),
                  pl.BlockSpec((t, V_DIM), lambda h, j: (j, h)),
                  pl.BlockSpec((S, V_DIM), lambda h, j: (0, h)),
                  pl.BlockSpec((1, n, 8, t), lambda h, j: (h, 0, 0, 0))] + [hbm] * n_rin,
        out_specs=[pl.BlockSpec((S, QK_PAD), lambda h, j: (0, h)),
                   pl.BlockSpec((t, QK_PAD), lambda h, j: (j, h)),
                   pl.BlockSpec((t, V_DIM), lambda h, j: (j, h))] + [hbm] * n_rout,
        out_shape=[jax.ShapeDtypeStruct((S, HEADS * QK_PAD), F32), jax.ShapeDtypeStruct((S, HEADS * QK_PAD), BF16),
                   jax.ShapeDtypeStruct((S, HEADS * V_DIM), BF16)] + list(rider.out_shapes),
        scratch_shapes=[pltpu.SemaphoreType.DMA((rider.n_sems,)), pltpu.SemaphoreType.DMA((rider.n_sems,))],
        compiler_params=_params(("arbitrary", "arbitrary")),
    )(q, k, v, do, stats, *rider.ins)
    return res[0], res[1], res[2], list(res[3:])


def _mla_qk_post(dq, dk, tabs, *, tm):
    S = dq.shape[0]
    tm = min(tm, S)
    scale = QK_DIM ** -0.5

    def body(dq_ref, dk_ref, c_ref, s1_ref, s2_ref, q16_ref, kr_ref):
        c, s1, s2 = c_ref[...], s1_ref[...], s2_ref[...]
        kpe = jnp.zeros((tm, LANES), F32)
        for h in range(HEADS):
            lo = h * QK_PAD
            q16_ref[:, lo:lo + QK_NOPE] = (dq_ref[:, lo:lo + QK_NOPE] * scale).astype(BF16)
            q16_ref[:, lo + QK_NOPE:lo + QK_PAD] = _rot(dq_ref[:, lo + QK_NOPE:lo + QK_PAD] * scale, c, s1, s2, -1).astype(BF16)
            kpe = kpe + dk_ref[:, lo + QK_NOPE:lo + QK_PAD].astype(F32)
        kr_ref[...] = _rot(kpe, c, s1, s2, -1).astype(BF16)

    wide = pl.BlockSpec((tm, HEADS * QK_PAD), lambda i: (i, 0))
    lane = pl.BlockSpec((tm, LANES), lambda i: (i, 0))
    return pl.pallas_call(
        body, name="mla_qk_post", grid=(S // tm,),
        in_specs=[wide, wide, lane, lane, lane], out_specs=[wide, lane],
        out_shape=[jax.ShapeDtypeStruct((S, HEADS * QK_PAD), BF16), jax.ShapeDtypeStruct((S, LANES), BF16)],
        compiler_params=_params(("parallel",)),
    )(dq, dk, *tabs)


def _mem_scores(q16, km_ref, h):
    lo = h * MEM_HEAD_DIM
    s = _dot(q16, km_ref[:, lo:lo + MEM_HEAD_DIM], NT) * (MEM_HEAD_DIM ** -0.5)
    e = jnp.exp(s - jnp.max(s, axis=-1, keepdims=True))
    return e / jnp.sum(e, axis=-1, keepdims=True)


def _mem_fwd(proj, kvm, *, tm):
    S = proj.shape[0]
    tm = min(tm, S)
    M = kvm.shape[0]

    def body(q_ref, z_ref, km_ref, vm_ref, y_ref):
        sz, _ = _silu_parts(z_ref[...].astype(F32))
        for h in range(MEM_HEADS):
            lo = h * MEM_HEAD_DIM
            p = _mem_scores(q_ref[:, lo:lo + MEM_HEAD_DIM].astype(BF16), km_ref, h)
            o = _dot(p.astype(BF16), vm_ref[:, lo:lo + MEM_HEAD_DIM], NN)
            y_ref[:, lo:lo + MEM_HEAD_DIM] = (o * sz[:, lo:lo + MEM_HEAD_DIM]).astype(BF16)

    return pl.pallas_call(
        body, name="mem_fwd", grid=(S // tm,),
        in_specs=[pl.BlockSpec((tm, D), lambda i: (i, COL_QM)), pl.BlockSpec((tm, D), lambda i: (i, COL_ZM)),
                  pl.BlockSpec((M, D), lambda i: (0, 0)), pl.BlockSpec((M, D), lambda i: (0, 1))],
        out_specs=pl.BlockSpec((tm, D), lambda i: (i, 0)),
        out_shape=jax.ShapeDtypeStruct((S, D), BF16),
        compiler_params=_params(("parallel",)),
    )(proj, proj, kvm, kvm)


def _mem_bwd(proj, kvm, dy, *, tm):
    S = proj.shape[0]
    tm = min(tm, S)
    M = kvm.shape[0]
    scale = MEM_HEAD_DIM ** -0.5

    def body(q_ref, z_ref, km_ref, vm_ref, dy_ref, dq_ref, dz_ref, dkv_ref):
        @pl.when(pl.program_id(0) == 0)
        def _():
            dkv_ref[...] = jnp.zeros_like(dkv_ref)

        sz, dsz = _silu_parts(z_ref[...].astype(F32))
        dyv = dy_ref[...].astype(F32)
        for h in range(MEM_HEADS):
            lo = h * MEM_HEAD_DIM
            sl = slice(lo, lo + MEM_HEAD_DIM)
            q16 = q_ref[:, sl].astype(BF16)
            p = _mem_scores(q16, km_ref, h)
            p16 = p.astype(BF16)
            o = _dot(p16, vm_ref[:, sl], NN)
            dy_h = dyv[:, sl]
            dz_ref[:, sl] = (dy_h * o * dsz[:, sl]).astype(BF16)
            do16 = (dy_h * sz[:, sl]).astype(BF16)
            dp = _dot(do16, vm_ref[:, sl], NT)
            ds = (p * (dp - jnp.sum(dp * p, axis=-1, keepdims=True)) * scale).astype(BF16)
            dq_ref[:, sl] = _dot(ds, km_ref[:, sl], NN).astype(BF16)
            dkv_ref[:, sl] += _dot(ds, q16, TN)
            dkv_ref[:, D + lo:D + lo + MEM_HEAD_DIM] += _dot(p16, do16, TN)

    blk = pl.BlockSpec((tm, D), lambda i: (i, 0))
    return pl.pallas_call(
        body, name="mem_bwd", grid=(S // tm,),
        in_specs=[pl.BlockSpec((tm, D), lambda i: (i, COL_QM)), pl.BlockSpec((tm, D), lambda i: (i, COL_ZM)),
                  pl.BlockSpec((M, D), lambda i: (0, 0)), pl.BlockSpec((M, D), lambda i: (0, 1)), blk],
        out_specs=[blk, blk, pl.BlockSpec((M, 2 * D), lambda i: (0, 0))],
        out_shape=[jax.ShapeDtypeStruct((S, D), BF16), jax.ShapeDtypeStruct((S, D), BF16),
                   jax.ShapeDtypeStruct((M, 2 * D), F32)],
        compiler_params=_params(("arbitrary",)),
    )(proj, proj, kvm, kvm, dy)


def _gmlp_common(u_ref, v_ref, lng_ref, lnb_ref):
    u, du = _gelu_parts(u_ref[...].astype(F32))
    vg, dvg = _gelu_parts(v_ref[...].astype(F32))
    mu = jnp.mean(vg, axis=-1, keepdims=True)
    vc = vg - mu
    r = lax.rsqrt(jnp.mean(vc * vc, axis=-1, keepdims=True) + EPS)
    vhat = vc * r
    vn = vhat * lng_ref[...] + lnb_ref[...]
    return u, du, dvg, r, vhat, vn.astype(BF16)


def _gmlp_fwd(proj, ln_g, ln_b, wm, bs_t):
    S = proj.shape[0]

    def body(u_ref, v_ref, z_ref, lng_ref, lnb_ref, wm_ref, bs_ref, y_ref):
        u, _, _, _, _, v16 = _gmlp_common(u_ref, v_ref, lng_ref, lnb_ref)
        sz, _ = _silu_parts(z_ref[...].astype(F32))
        for g in range(A_GROUPS):
            sl = slice(g * CHUNK, (g + 1) * CHUNK)
            sv = _dot(wm_ref[g], v16[:, sl], NN) + bs_ref[:, g:g + 1]
            y_ref[:, sl] = (u[:, sl] * sv * sz[:, sl]).astype(BF16)

    def col(c):
        return pl.BlockSpec((CHUNK, D), lambda i: (i, c))

    vec = pl.BlockSpec((1, D), lambda i: (0, 0))
    return pl.pallas_call(
        body, name="gmlp_fwd", grid=(S // CHUNK,),
        in_specs=[col(COL_U), col(COL_V), col(COL_ZA), vec, vec,
                  pl.BlockSpec((A_GROUPS, CHUNK, CHUNK), lambda i: (0, 0, 0)), pl.BlockSpec((CHUNK, A_GROUPS), lambda i: (0, 0))],
        out_specs=col(0), out_shape=jax.ShapeDtypeStruct((S, D), BF16),
        compiler_params=_params(("parallel",)),
    )(proj, proj, proj, ln_g, ln_b, wm, bs_t)


def _gmlp_bwd(proj, dy, ln_g, ln_b, wm, bs_t, others):
    S = proj.shape[0]

    def body(u_ref, v_ref, z_ref, dy_ref, lng_ref, lnb_ref, wm_ref, bs_ref, zb_ref, qm_ref, zm_ref, cq_ref, ckv_ref, kr_ref,
             dp_ref, gws_ref, dsv_ref, glg_ref, glb_ref, dvn_s):
        dp_ref[:, COL_ZB * D:(COL_ZB + 1) * D] = zb_ref[...]
        dp_ref[:, COL_QM * D:(COL_QM + 1) * D] = qm_ref[...]
        dp_ref[:, COL_ZM * D:(COL_ZM + 1) * D] = zm_ref[...]
        dp_ref[:, COL_CQ * LORA:(COL_CQ + 1) * LORA] = cq_ref[...]
        dp_ref[:, COL_CKV * LORA:(COL_CKV + 1) * LORA] = ckv_ref[...]
        dp_ref[:, COL_KR * LANES:(COL_KR + 1) * LANES] = kr_ref[...]

        @pl.when(pl.program_id(0) == 0)
        def _():
            gws_ref[...] = jnp.zeros_like(gws_ref)
            dsv_ref[...] = jnp.zeros_like(dsv_ref)
            glg_ref[...] = jnp.zeros_like(glg_ref)
            glb_ref[...] = jnp.zeros_like(glb_ref)

        u, du, dvg, r, vhat, v16 = _gmlp_common(u_ref, v_ref, lng_ref, lnb_ref)
        sz, dsz = _silu_parts(z_ref[...].astype(F32))
        dyv = dy_ref[...].astype(F32)
        for g in range(A_GROUPS):
            sl = slice(g * CHUNK, (g + 1) * CHUNK)
            sv = _dot(wm_ref[g], v16[:, sl], NN) + bs_ref[:, g:g + 1]
            dy_g, u_g, sz_g = dyv[:, sl], u[:, sl], sz[:, sl]
            dsv = dy_g * u_g * sz_g
            dp_ref[:, g * CHUNK:(g + 1) * CHUNK] = (dy_g * sv * sz_g * du[:, sl]).astype(BF16)
            dp_ref[:, COL_ZA * D + g * CHUNK:COL_ZA * D + (g + 1) * CHUNK] = (dy_g * u_g * sv * dsz[:, sl]).astype(BF16)
            dsv16 = dsv.astype(BF16)
            dvn_s[:, sl] = _dot(wm_ref[g], dsv16, TN)
            gws_ref[g] += _dot(dsv16, v16[:, sl], NT)
            dsv_ref[:, sl] += dsv
        dvn = dvn_s[...]
        glb_ref[...] += jnp.sum(dvn, axis=0, keepdims=True)
        glg_ref[...] += jnp.sum(dvn * vhat, axis=0, keepdims=True)
        dvh = dvn * lng_ref[...]
        dvc = r * (dvh - jnp.mean(dvh, axis=-1, keepdims=True) - vhat * jnp.mean(dvh * vhat, axis=-1, keepdims=True))
        dp_ref[:, COL_V * D:(COL_V + 1) * D] = (dvc * dvg).astype(BF16)

    def col(c):
        return pl.BlockSpec((CHUNK, D), lambda i: (i, c))

    def rows(width):
        return pl.BlockSpec((CHUNK, width), lambda i: (i, 0))

    vec = pl.BlockSpec((1, D), lambda i: (0, 0))
    wsp = pl.BlockSpec((A_GROUPS, CHUNK, CHUNK), lambda i: (0, 0, 0))
    return pl.pallas_call(
        body, name="gmlp_bwd", grid=(S // CHUNK,),
        in_specs=[col(COL_U), col(COL_V), col(COL_ZA), col(0), vec, vec, wsp, pl.BlockSpec((CHUNK, A_GROUPS), lambda i: (0, 0)),
                  rows(D), rows(D), rows(D), rows(LORA), rows(LORA), rows(LANES)],
        out_specs=[rows(IN_PAD), wsp, pl.BlockSpec((CHUNK, D), lambda i: (0, 0)), vec, vec],
        out_shape=[jax.ShapeDtypeStruct((S, IN_PAD), BF16),
                   jax.ShapeDtypeStruct((A_GROUPS, CHUNK, CHUNK), F32), jax.ShapeDtypeStruct((CHUNK, D), F32),
                   jax.ShapeDtypeStruct((1, D), F32), jax.ShapeDtypeStruct((1, D), F32)],
        scratch_shapes=[pltpu.VMEM((CHUNK, D), F32)],
        compiler_params=_params(("arbitrary",)),
    )(proj, proj, proj, dy, ln_g, ln_b, wm, bs_t, *others)


def _gate_merge(h16, ys, wg, bg, wbs, *, tm, tn):
    S = h16.shape[0]
    tm = min(tm, S)
    nj = D // tn

    def body(h_ref, ya_ref, yb_ref, ym_ref, wg0, wg1, wg2, bg0, bg1, bg2, wb0, wb1, wb2,
             mg_ref, g0_ref, g1_ref, g2_ref, p0_ref, p1_ref, p2_ref):
        hv = h_ref[...]
        acc = None
        for y_ref, wg_ref, bgr, wb_ref, g_ref, p_ref in ((ya_ref, wg0, bg0, wb0, g0_ref, p0_ref),
                                                         (yb_ref, wg1, bg1, wb1, g1_ref, p1_ref),
                                                         (ym_ref, wg2, bg2, wb2, g2_ref, p2_ref)):
            gate = _sigmoid(_dot(hv, wg_ref[...], NN) + bgr[...])
            p = _dot(y_ref[...], wb_ref[...], NN)
            g_ref[...] = gate.astype(BF16)
            p_ref[...] = p.astype(BF16)
            acc = gate * p if acc is None else acc + gate * p
        mg_ref[...] = acc.astype(BF16)

    a_spec = pl.BlockSpec((tm, D), lambda j, i: (i, 0))
    o_spec = pl.BlockSpec((tm, tn), lambda j, i: (i, j))

    def wgs(n):
        return pl.BlockSpec((D, tn), lambda j, i: (0, n * nj + j))

    def bgs(n):
        return pl.BlockSpec((1, tn), lambda j, i: (0, n * nj + j))

    wbsp = pl.BlockSpec((D, tn), lambda j, i: (0, j))
    return pl.pallas_call(
        body, name="gate_merge", grid=(nj, S // tm),
        in_specs=[a_spec] * 4 + [wgs(0), wgs(1), wgs(2), bgs(0), bgs(1), bgs(2), wbsp, wbsp, wbsp],
        out_specs=[o_spec] * 7, out_shape=[jax.ShapeDtypeStruct((S, D), BF16)] * 7,
        compiler_params=_params(("parallel", "parallel")),
    )(h16, *ys, wg, wg, wg, bg, bg, bg, *wbs)


def _gate_bwd(dmerged, gates, ps, *, tm):
    S = dmerged.shape[0]
    tm = min(tm, S)

    def body(dm_ref, g0, g1, g2, p0, p1, p2, dp0, dp1, dp2, dg_ref, gb_ref):
        @pl.when(pl.program_id(0) == 0)
        def _():
            gb_ref[...] = jnp.zeros_like(gb_ref)

        dm = dm_ref[...].astype(F32)
        for n, (g_ref, p_ref, dp_ref) in enumerate(((g0, p0, dp0), (g1, p1, dp1), (g2, p2, dp2))):
            gate = g_ref[...].astype(F32)
            dp_ref[...] = (dm * gate).astype(BF16)
            dg = dm * p_ref[...].astype(F32) * gate * (1.0 - gate)
            dg_ref[:, n * D:(n + 1) * D] = dg.astype(BF16)
            gb_ref[:, n * D:(n + 1) * D] += jnp.sum(dg, axis=0, keepdims=True)

    blk = pl.BlockSpec((tm, D), lambda i: (i, 0))
    return pl.pallas_call(
        body, name="gate_bwd", grid=(S // tm,),
        in_specs=[blk] * 7,
        out_specs=[blk, blk, blk, pl.BlockSpec((tm, 3 * D), lambda i: (i, 0)), pl.BlockSpec((1, 3 * D), lambda i: (0, 0))],
        out_shape=[jax.ShapeDtypeStruct((S, D), BF16)] * 3 + [jax.ShapeDtypeStruct((S, 3 * D), BF16),
                                                              jax.ShapeDtypeStruct((1, 3 * D), F32)],
        compiler_params=_params(("arbitrary",)),
    )(dmerged, *gates, *ps)


def _post_loss(x, out, target, g_post, *, tm):
    S = x.shape[0]
    tm = min(tm, S)

    def body(x_ref, o_ref, t_ref, g_ref, dy_ref, do_ref, gg_ref, ls_ref):
        @pl.when(pl.program_id(0) == 0)
        def _():
            gg_ref[...] = jnp.zeros_like(gg_ref)
            ls_ref[...] = jnp.zeros_like(ls_ref)

        ov = o_ref[...]
        r = lax.rsqrt(jnp.mean(ov * ov, axis=-1, keepdims=True) + EPS)
        n = ov * r
        err = (x_ref[...] + n * g_ref[...]) - t_ref[...]
        ls_ref[...] += 0.5 * jnp.sum(jnp.mean(err * err, axis=-1, keepdims=True))
        dy = err * (1.0 / D)
        dy_ref[...] = dy
        gg_ref[...] += jnp.sum(dy * n, axis=0, keepdims=True)
        gd = dy * g_ref[...]
        do_ref[...] = (r * (gd - n * jnp.mean(gd * n, axis=-1, keepdims=True))).astype(BF16)

    blk = pl.BlockSpec((tm, D), lambda i: (i, 0))
    vec = pl.BlockSpec((1, D), lambda i: (0, 0))
    return pl.pallas_call(
        body, name="post_loss", grid=(S // tm,),
        in_specs=[blk, blk, blk, vec],
        out_specs=[blk, blk, vec, pl.BlockSpec((1, LANES), lambda i: (0, 0))],
        out_shape=[jax.ShapeDtypeStruct((S, D), F32), jax.ShapeDtypeStruct((S, D), BF16),
                   jax.ShapeDtypeStruct((1, D), F32), jax.ShapeDtypeStruct((1, LANES), F32)],
        compiler_params=_params(("arbitrary",)),
    )(x, out, target, g_post)


def _adamw(w, g_own, g_other, m, v, cidx, *, name):
    rows, width = w.shape
    hh = rows // 2
    tr = _row_tile(hh, width, unit=8)
    nb = hh // tr

    def body(c_ref, w_ref, own_ref, oth_ref, m_ref, v_ref, g_ref, d_ref, nm_ref, nv_ref):
        mine = (pl.program_id(0) // nb) == c_ref[0]
        gv = jnp.where(mine, own_ref[...], oth_ref[...])
        delta, nm, nv = _adam_math(w_ref[...], gv, m_ref[...], v_ref[...])
        g_ref[...] = gv
        d_ref[...] = delta
        nm_ref[...] = nm
        nv_ref[...] = nv

    blk = pl.BlockSpec((tr, width), lambda i, c_ref: (i, 0))
    half = pl.BlockSpec((tr, width), lambda i, c_ref: (i % nb, 0))
    return pl.pallas_call(
        body, name=name,
        grid_spec=pltpu.PrefetchScalarGridSpec(num_scalar_prefetch=1, grid=(rows // tr,),
                                               in_specs=[blk, half, half, blk, blk], out_specs=[blk] * 4),
        out_shape=[jax.ShapeDtypeStruct((rows, width), F32)] * 4,
        compiler_params=_params(("parallel",)),
    )(cidx, w, g_own, g_other, m, v)


MESH = pl.DeviceIdType.MESH
ANY = pl.BlockSpec(memory_space=pl.ANY)


def _place():
    return lax.axis_index("x"), lax.axis_index("y"), lax.axis_index("c")


def _other_chips(x, y):
    return [(1 - x, y), (x, 1 - y), (1 - x, 1 - y)]


def _remote(src, dst, send_sem, recv_sem, dev):
    return pltpu.make_async_remote_copy(src_ref=src, dst_ref=dst, send_sem=send_sem, recv_sem=recv_sem,
                                        device_id=dev, device_id_type=MESH)


def _allgather_chips(shards):
    nw = len(shards)

    def body(*refs):
        x_refs, out_refs = refs[:nw], refs[nw:2 * nw]
        send_sems, recv_sems = refs[2 * nw:]
        x, y, c = _place()
        sibling = (x, y, 1 - c)
        chips = _other_chips(x, y)

        def half(w, px, py, hc):
            hh = shards[w].shape[0] // 2
            return out_refs[w].at[2 * px + py, pl.ds(hc * hh, hh), :]

        sent = []
        for w in range(nw):
            hh = shards[w].shape[0] // 2
            for k, (px, py) in enumerate(chips):
                cp = _remote(x_refs[w].at[pl.ds(c * hh, hh), :], half(w, x, y, c), send_sems.at[6 * w + k],
                             recv_sems.at[6 * w + k], (px, py, c))
                cp.start()
                sent.append(cp)
        for w in range(nw):
            for k, (px, py) in enumerate(chips):
                landed = half(w, px, py, c)
                _remote(landed, landed, send_sems.at[6 * w + k], recv_sems.at[6 * w + k], (px, py, c)).wait_recv()
                cp = _remote(landed, landed, send_sems.at[6 * w + 3 + k], recv_sems.at[6 * w + 3 + k], sibling)
                cp.start()
                sent.append(cp)
        for w in range(nw):
            for k, (px, py) in enumerate(chips):
                other = half(w, px, py, 1 - c)
                _remote(other, other, send_sems.at[6 * w + 3 + k], recv_sems.at[6 * w + 3 + k], sibling).wait_recv()
        for cp in sent:
            cp.wait_send()

    outs = pl.pallas_call(
        body, name="allgather_weights", in_specs=[ANY] * nw, out_specs=[ANY] * nw,
        out_shape=[jax.ShapeDtypeStruct((N_CHIPS,) + s.shape, s.dtype) for s in shards],
        scratch_shapes=[pltpu.SemaphoreType.DMA((6 * nw,)), pltpu.SemaphoreType.DMA((6 * nw,))],
    )(*shards)
    own = 2 * lax.axis_index("x") + lax.axis_index("y")
    return [lax.dynamic_update_slice(o, s[None], (own, 0, 0)) for o, s in zip(outs, shards)]


def _row_tile(rows, cols, unit=16, budget=2 * 1024 * 1024):
    best = unit
    for t in range(unit, rows + 1, unit):
        if rows % t == 0 and t * cols * 4 <= budget:
            best = t
    assert rows % best == 0, (rows, cols)
    return best


def _peers(x, y, c):
    out = []
    for k in range(1, N_DEV):
        out.append((k, (1 - x if (k >> 2) & 1 else x, 1 - y if (k >> 1) & 1 else y, 1 - c if k & 1 else c)))
    return out


def _gather_rider(shards):
    nw = len(shards)

    def half(outs, w, px, py, hc):
        hh = shards[w].shape[0] // 2
        return outs[w].at[2 * px + py, pl.ds(hc * hh, hh), :]

    def ici(ins, outs, ss, rs, w, k, px, py, c, x, y):
        hh = shards[w].shape[0] // 2
        return _remote(ins[w].at[pl.ds(c * hh, hh), :], half(outs, w, x, y, c), ss.at[6 * w + k], rs.at[6 * w + k], (px, py, c))

    def passing(outs, ss, rs, w, k, px, py, hc, sibling):
        landed = half(outs, w, px, py, hc)
        return _remote(landed, landed, ss.at[6 * w + 3 + k], rs.at[6 * w + 3 + k], sibling)

    def start(ins, outs, ss, rs):
        x, y, c = _place()
        for w in range(nw):
            for k, (px, py) in enumerate(_other_chips(x, y)):
                ici(ins, outs, ss, rs, w, k, px, py, c, x, y).start()

    def forward(ins, outs, ss, rs):
        x, y, c = _place()
        for w in range(nw):
            for k, (px, py) in enumerate(_other_chips(x, y)):
                landed = half(outs, w, px, py, c)
                _remote(landed, landed, ss.at[6 * w + k], rs.at[6 * w + k], (px, py, c)).wait_recv()
                passing(outs, ss, rs, w, k, px, py, c, (x, y, 1 - c)).start()

    def finish(ins, outs, ss, rs):
        x, y, c = _place()
        for w in range(nw):
            for k, (px, py) in enumerate(_other_chips(x, y)):
                passing(outs, ss, rs, w, k, px, py, 1 - c, (x, y, 1 - c)).wait_recv()
        for w in range(nw):
            for k, (px, py) in enumerate(_other_chips(x, y)):
                ici(ins, outs, ss, rs, w, k, px, py, c, x, y).wait_send()
                passing(outs, ss, rs, w, k, px, py, c, (x, y, 1 - c)).wait_send()

    return _Rider(ins=tuple(shards), out_shapes=tuple(jax.ShapeDtypeStruct((N_CHIPS,) + s.shape, s.dtype) for s in shards),
                  n_sems=6 * nw, phases=((0.0, start), (0.8, forward), (1.0, finish)))


def _own_blocks_in_place(gathered, shards):
    own = 2 * lax.axis_index("x") + lax.axis_index("y")
    return [lax.dynamic_update_slice(o, s[None], (own, 0, 0)) for o, s in zip(gathered, shards)]


def _exchange_rider(blocks):
    nw = len(blocks)

    def copy(ins, outs, ss, rs, w, k, peer):
        hh = blocks[w].shape[1] // 2
        px, py, pc = peer
        return _remote(ins[w].at[2 * px + py, pl.ds(pc * hh, hh), :], outs[w].at[k - 1], ss.at[7 * w + k - 1], rs.at[7 * w + k - 1], peer)

    def start(ins, outs, ss, rs):
        for w in range(nw):
            for k, peer in _peers(*_place()):
                copy(ins, outs, ss, rs, w, k, peer).start()

    def finish(ins, outs, ss, rs):
        for w in range(nw):
            for k, peer in _peers(*_place()):
                copy(ins, outs, ss, rs, w, k, peer).wait()

    return _Rider(ins=tuple(blocks),
                  out_shapes=tuple(jax.ShapeDtypeStruct((N_DEV - 1, b.shape[1] // 2, b.shape[2]), b.dtype) for b in blocks),
                  n_sems=7 * nw, phases=((0.0, start), (1.0, finish)))


def _reduce_add(own, recv, cidx, *, name):
    R, W = own.shape
    hh = R // 2
    tr = _row_tile(hh, W, budget=1024 * 1024)
    nb = hh // tr

    def body(c_ref, o_ref, r_ref, t_ref):
        s = o_ref[...]
        for k in range(N_DEV - 1):
            s = s + r_ref[k].astype(F32)
        t_ref[...] = s

    return pl.pallas_call(
        body, name=name,
        grid_spec=pltpu.PrefetchScalarGridSpec(
            num_scalar_prefetch=1, grid=(nb,),
            in_specs=[pl.BlockSpec((tr, W), lambda i, c_ref: (i + c_ref[0] * nb, 0)),
                      pl.BlockSpec((N_DEV - 1, tr, W), lambda i, c_ref: (0, i, 0))],
            out_specs=pl.BlockSpec((tr, W), lambda i, c_ref: (i, 0))),
        out_shape=jax.ShapeDtypeStruct((hh, W), F32),
        compiler_params=_params(("parallel",)),
    )(cidx, own, recv)


def _halves_exchange(ts):
    nw = len(ts)

    def body(*refs):
        t_refs, out_refs = refs[:nw], refs[nw:2 * nw]
        send_sems, recv_sems = refs[2 * nw:]
        x, y, c = _place()
        cps = []
        for w in range(nw):
            cp = _remote(t_refs[w], out_refs[w], send_sems.at[w], recv_sems.at[w], (x, y, 1 - c))
            cp.start()
            cps.append(cp)
        for cp in cps:
            cp.wait()

    return pl.pallas_call(
        body, name="grad_halves_exchange", in_specs=[ANY] * nw, out_specs=[ANY] * nw,
        out_shape=[jax.ShapeDtypeStruct(t.shape, t.dtype) for t in ts],
        scratch_shapes=[pltpu.SemaphoreType.DMA((nw,)), pltpu.SemaphoreType.DMA((nw,))],
    )(*ts)


def _adam_math(w, g, m, v):
    nm = ADAM_B1 * m + (1.0 - ADAM_B1) * g
    nv = ADAM_B2 * v + (1.0 - ADAM_B2) * (g * g)
    c1 = 1.0 - ADAM_B1 ** ADAM_STEP
    c2 = 1.0 - ADAM_B2 ** ADAM_STEP
    return -ADAM_LR * ((nm / c1) / (jnp.sqrt(nv / c2) + ADAM_EPS) + ADAM_WD * w), nm, nv


STAGE_ROWS = 32
STAGE_VEC = {"g_pre": 0, "a_ln_g": 1, "a_ln_b": 2, "mem_norm_g": 3, "g_post": 4}
STAGE_BGATE = 5
STAGE_MIX = 8
STAGE_ABS = 16


def _small_step(g, loss_row, w, m, v):
    n = len(SMALL)

    def reduce_body(*refs):
        g_r = dict(zip(SMALL, refs[:n]))
        loss_r = refs[n]
        sa_o, sw_o = refs[n + 1], refs[n + 2]
        stage, ga, gw, aws16, send_sems, recv_sems = refs[n + 3:]

        stage[...] = jnp.zeros_like(stage)
        for name, row in STAGE_VEC.items():
            stage[row:row + 1, :] = g_r[name][...]
        for t in range(3):
            stage[STAGE_BGATE + t:STAGE_BGATE + t + 1, :] = g_r["b_gate"][:, t * D:(t + 1) * D]
        stage[STAGE_MIX:STAGE_MIX + 1, 0:LORA] = g_r["q_norm_g"][...]
        stage[STAGE_MIX:STAGE_MIX + 1, LORA:2 * LORA] = g_r["kv_norm_g"][...]
        stage[STAGE_MIX:STAGE_MIX + 1, 2 * LORA:2 * LORA + LANES] = loss_r[...]
        stage[STAGE_ABS:STAGE_ABS + A_GROUPS, 0:CHUNK] = g_r["a_b_s"][...]

        x, y, c = _place()
        me = 4 * x + 2 * y + c
        ga[me] = stage[...]
        aws16[...] = g_r["a_w_s"][...].astype(BF16)
        gw[me] = aws16[...]
        cps, srcs = [], []
        for k in range(1, N_DEV):
            fx, fy, fc = (k >> 2) & 1, (k >> 1) & 1, k & 1
            peer = (1 - x if fx else x, 1 - y if fy else y, 1 - c if fc else c)
            for j, (src, dst) in enumerate(((stage, ga), (aws16, gw))):
                cp = _remote(src, dst.at[me], send_sems.at[2 * (k - 1) + j], recv_sems.at[2 * (k - 1) + j], peer)
                cp.start()
                cps.append(cp)
            srcs.append(4 * peer[0] + 2 * peer[1] + peer[2])
        for k, src in enumerate(srcs):
            _remote(stage, ga.at[src], send_sems.at[2 * k], recv_sems.at[2 * k], (x, y, c)).wait_recv()
            _remote(aws16, gw.at[src], send_sems.at[2 * k + 1], recv_sems.at[2 * k + 1], (x, y, c)).wait_recv()
        for cp in cps:
            cp.wait_send()
        sa, sw = ga[0], gw[0].astype(F32)
        for d in range(1, N_DEV):
            sa = sa + ga[d]
            sw = sw + gw[d].astype(F32)
        sa_o[...] = sa
        sw_o[...] = sw

    def update_body(*refs):
        sa, sw = refs[0][...], refs[1][...]
        w_r = dict(zip(SMALL, refs[2:n + 2]))
        m_r = dict(zip(SMALL, refs[n + 2:2 * n + 2]))
        v_r = dict(zip(SMALL, refs[2 * n + 2:3 * n + 2]))
        outs = refs[3 * n + 2:7 * n + 2]
        o_r = {name: outs[4 * i:4 * i + 4] for i, name in enumerate(SMALL)}
        loss_o = refs[7 * n + 2]

        def update(name, gsum, cols=None):
            sel = (slice(None), cols) if cols is not None else Ellipsis
            delta, nm, nv = _adam_math(w_r[name][sel], gsum, m_r[name][sel], v_r[name][sel])
            for ref, val in zip(o_r[name], (gsum, delta, nm, nv)):
                ref[sel] = val

        for name, row in STAGE_VEC.items():
            update(name, sa[row:row + 1, :])
        for t in range(3):
            update("b_gate", sa[STAGE_BGATE + t:STAGE_BGATE + t + 1, :], slice(t * D, (t + 1) * D))
        update("q_norm_g", sa[STAGE_MIX:STAGE_MIX + 1, 0:LORA])
        update("kv_norm_g", sa[STAGE_MIX:STAGE_MIX + 1, LORA:2 * LORA])
        update("a_b_s", sa[STAGE_ABS:STAGE_ABS + A_GROUPS, 0:CHUNK])
        update("a_w_s", sw)
        loss_o[...] = sa[STAGE_MIX:STAGE_MIX + 1, 2 * LORA:2 * LORA + LANES]

    vm = pl.BlockSpec(memory_space=pltpu.VMEM)
    sa, sw = pl.pallas_call(
        reduce_body, name="small_allreduce", in_specs=[vm] * (n + 1), out_specs=[vm, vm],
        out_shape=[jax.ShapeDtypeStruct((STAGE_ROWS, D), F32), jax.ShapeDtypeStruct((A_GROUPS, CHUNK, CHUNK), F32)],
        scratch_shapes=[pltpu.VMEM((STAGE_ROWS, D), F32), pltpu.VMEM((N_DEV, STAGE_ROWS, D), F32),
                        pltpu.VMEM((N_DEV, A_GROUPS, CHUNK, CHUNK), BF16), pltpu.VMEM((A_GROUPS, CHUNK, CHUNK), BF16),
                        pltpu.SemaphoreType.DMA((2 * (N_DEV - 1),)), pltpu.SemaphoreType.DMA((2 * (N_DEV - 1),))],
        compiler_params=pltpu.CompilerParams(vmem_limit_bytes=VMEM_LIMIT),
    )(*[g[k] for k in SMALL], loss_row)
    ins = [sa, sw] + [w[k] for k in SMALL] + [m[k] for k in SMALL] + [v[k] for k in SMALL]
    out_shape = [jax.ShapeDtypeStruct(w[k].shape, F32) for k in SMALL for _ in range(4)] + [jax.ShapeDtypeStruct((1, LANES), F32)]
    res = pl.pallas_call(
        update_body, name="small_adamw", in_specs=[vm] * len(ins), out_specs=[vm] * len(out_shape), out_shape=out_shape,
        compiler_params=pltpu.CompilerParams(vmem_limit_bytes=VMEM_LIMIT),
    )(*ins)
    return {k: tuple(res[4 * i:4 * i + 4]) for i, k in enumerate(SMALL)}, res[-1]


SHARD_2D = {"w_in": (D, IN_REF // N_CHIPS), "w_uq": (LORA, HEADS * QK_DIM // N_CHIPS),
            "w_ukv": (LORA, HEADS * (QK_NOPE + V_DIM) // N_CHIPS), "w_mem_kv": (D, 2 * D // N_CHIPS),
            "w_gate": (D, 3 * D // N_CHIPS), "w_branch": (3 * D // N_CHIPS, D), "w_out": (D // N_CHIPS, D)}


def _cols(blocks):
    return jnp.concatenate([blocks[j] for j in range(N_CHIPS)], axis=1)


def _w_in_layout(gathered):
    w = _cols(gathered)
    return jnp.concatenate([w[:, :3 * D], w[:, 3 * D + 2 * LORA + QK_ROPE:], w[:, 3 * D:3 * D + 2 * LORA + QK_ROPE],
                            jnp.zeros((D, IN_PAD - IN_REF), w.dtype)], axis=1)


REST = BIG[1:]


def _rest_layouts(gathered):
    wq = jnp.pad(_cols(gathered["w_uq"]).reshape(LORA, HEADS, QK_DIM), ((0, 0), (0, 0), (0, QK_PAD - QK_DIM))).reshape(LORA, HEADS * QK_PAD)
    kv3 = _cols(gathered["w_ukv"]).reshape(LORA, HEADS, QK_NOPE + V_DIM)
    wk = jnp.pad(kv3[:, :, :QK_NOPE], ((0, 0), (0, 0), (0, QK_PAD - QK_NOPE))).reshape(LORA, HEADS * QK_PAD)
    wv = kv3[:, :, QK_NOPE:].reshape(LORA, HEADS * V_DIM)
    w_branch = gathered["w_branch"].reshape(N_CHIPS, 3, D // N_CHIPS, D).transpose(1, 0, 2, 3).reshape(3, D, D)
    return {"wq": wq, "wk": wk, "wv": wv, "w_mem_kv": _cols(gathered["w_mem_kv"]), "w_gate": _cols(gathered["w_gate"]),
            "w_branch": w_branch, "w_out": gathered["w_out"].reshape(D, D)}


def _grad_reference_layout(name, g):
    if name == "w_in":
        return jnp.concatenate([g[:, :3 * D], g[:, 6 * D:6 * D + 2 * LORA + QK_ROPE], g[:, 3 * D:6 * D]], axis=1)
    if name == "w_uq":
        return g.reshape(LORA, HEADS, QK_PAD)[:, :, :QK_DIM].reshape(LORA, HEADS * QK_DIM)
    if name == "w_ukv":
        gk, gv = g
        return jnp.concatenate([gk.reshape(LORA, HEADS, QK_PAD)[:, :, :QK_NOPE], gv.reshape(LORA, HEADS, V_DIM)],
                               axis=2).reshape(LORA, HEADS * (QK_NOPE + V_DIM))
    return g


def _grad_blocks(name, full):
    own = 2 * lax.axis_index("x") + lax.axis_index("y")
    R, C = SHARD_2D[name]
    if name == "w_branch":
        blocks = full.reshape(3, N_CHIPS, D // N_CHIPS, D).transpose(1, 0, 2, 3).reshape(N_CHIPS, R, C)
        mine = lax.dynamic_slice_in_dim(full, own * (D // N_CHIPS), D // N_CHIPS, axis=1).reshape(R, C)
    elif name == "w_out":
        blocks = full.reshape(N_CHIPS, R, C)
        mine = lax.dynamic_slice_in_dim(full, own * R, R, axis=0)
    else:
        blocks = full.reshape(R, N_CHIPS, C).transpose(1, 0, 2)
        mine = lax.dynamic_slice_in_dim(full, own * C, C, axis=1)
    return blocks.astype(BF16), mine


def _local_step(x, mem, pos_col, target, w_in, rest_shards, P):
    cidx = lax.axis_index("c").astype(jnp.int32).reshape(1)
    h16 = _rms_fwd(x, P["g_pre"], width=D, col=0, tm=256, name="pre_norm")
    memn16 = _rms_fwd(mem, P["mem_norm_g"], width=D, col=0, tm=256, name="mem_norm")
    proj, rest = _mm(h16, w_in, "nn", tm=1024, tn=1920, tk=D, out_dtype=BF16, name="in_proj", rider=_gather_rider(rest_shards))
    W = _rest_layouts(dict(zip(REST, _own_blocks_in_place(rest, rest_shards))))

    causal = jnp.tril(jnp.ones((CHUNK, CHUNK), F32))
    wm = (P["a_w_s"] * causal[None]).astype(BF16)
    bs_t = P["a_b_s"].T
    ya = _gmlp_fwd(proj, P["a_ln_g"], P["a_ln_b"], wm, bs_t)

    inv = 1.0 / (ROPE_THETA ** (jnp.arange(0, QK_ROPE, 2, dtype=F32) / QK_ROPE))
    inv_lane = jnp.concatenate([inv, inv, jnp.zeros((LANES - QK_ROPE,), F32)])[None]
    tabs = _rope_tables(pos_col, inv_lane, tm=1024)
    cqn = _rms_fwd(proj, P["q_norm_g"], width=LORA, col=COL_CQ, tm=512, name="q_norm")
    ckvn = _rms_fwd(proj, P["kv_norm_g"], width=LORA, col=COL_CKV, tm=512, name="kv_norm")
    q16, k16, v16 = _mla_proj(cqn, ckvn, proj, tabs, W["wq"], W["wk"], W["wv"], tm=256)
    o_b, yb, lse = _mla_fwd(q16, k16, v16, proj, t=512)

    kvm = _mm(memn16, W["w_mem_kv"], "nn", tm=256, tn=1024, tk=D, out_dtype=BF16, name="mem_kv")
    ym = _mem_fwd(proj, kvm, tm=512)

    wbs = [W["w_branch"][n] for n in range(3)]
    merged, g0, g1, g2, p0, p1, p2 = _gate_merge(h16, (ya, yb, ym), W["w_gate"], P["b_gate"], wbs, tm=512, tn=512)
    out = _mm(merged, W["w_out"], "nn", tm=512, tn=1024, tk=D, out_dtype=F32, name="out_proj")
    dy, dout, g_g_post, loss = _post_loss(x, out, target, P["g_post"], tm=256)

    full = {}
    full["w_out"] = _mm(merged, dout, "tn", tm=1024, tn=1024, tk=TN_TK, out_dtype=F32, name="gw_out")
    dmerged = _mm(dout, W["w_out"], "nt", tm=512, tn=1024, tk=D, out_dtype=BF16, name="d_merged")
    dp0, dp1, dp2, dgpre, g_b_gate = _gate_bwd(dmerged, (g0, g1, g2), (p0, p1, p2), tm=256)
    full["w_gate"] = _mm(h16, dgpre, "tn", tm=1024, tn=1024, tk=TN_TK, out_dtype=F32, name="gw_gate")
    dh_gate = _mm(dgpre, W["w_gate"], "nt", tm=1024, tn=1024, tk=3 * D // 2, out_dtype=F32, name="dh_gate")
    full["w_branch"] = jnp.stack([_mm(y, dp, "tn", tm=1024, tn=1024, tk=TN_TK, out_dtype=F32, name=f"gw_branch{n}")
                                  for n, (y, dp) in enumerate(((ya, dp0), (yb, dp1), (ym, dp2)))], axis=0)
    dya, dyb, dym = [_mm(dp, wbs[n], "nt", tm=512, tn=1024, tk=D, out_dtype=BF16, name=f"dy_branch{n}")
                     for n, dp in enumerate((dp0, dp1, dp2))]

    dqm, dzm, dkvm = _mem_bwd(proj, kvm, dym, tm=512)
    dkvm16 = dkvm.astype(BF16)
    full["w_mem_kv"] = _mm(memn16, dkvm16, "tn", tm=1024, tn=1024, tk=256, out_dtype=F32, name="gw_mem_kv")
    dmemn = _mm(dkvm16, W["w_mem_kv"], "nt", tm=256, tn=1024, tk=2 * D, out_dtype=F32, name="d_memn")
    _, g_mem_norm = _rms_bwd(dmemn, mem, P["mem_norm_g"], width=D, col=0, tm=256, out_dtype=BF16, name="mem_norm_bwd")

    own, recv = {}, {}
    early = ("w_out", "w_gate", "w_branch", "w_mem_kv")
    early_blocks = []
    for n in early:
        blocks, own[n] = _grad_blocks(n, full[n])
        early_blocks.append(blocks)
    do16, dzb, stats = _mla_gate_bwd(dyb, o_b, proj, lse, tm=512)
    dq, dk16, dv16, landed = _mla_bwd(q16, k16, v16, do16, stats, t=512, rider=_exchange_rider(early_blocks))
    recv.update(zip(early, landed))
    dq16, dkr = _mla_qk_post(dq, dk16, tabs, tm=256)
    g_wq = _mm(cqn, dq16, "tn", tm=512, tn=1024, tk=TN_TK, out_dtype=F32, name="gw_uq")
    g_wk = _mm(ckvn, dk16, "tn", tm=512, tn=1024, tk=TN_TK, out_dtype=F32, name="gw_uk")
    g_wv = _mm(ckvn, dv16, "tn", tm=512, tn=1024, tk=TN_TK, out_dtype=F32, name="gw_uv")
    dcqn = _mm(dq16, W["wq"], "nt", tm=512, tn=LORA, tk=HEADS * QK_PAD, out_dtype=F32, name="d_cqn")
    dckvn_k = _mm(dk16, W["wk"], "nt", tm=512, tn=LORA, tk=HEADS * QK_PAD, out_dtype=F32, name="d_ckvn_k")
    dckvn = _mm(dv16, W["wv"], "nt", tm=512, tn=LORA, tk=HEADS * V_DIM, out_dtype=F32, name="d_ckvn", add=dckvn_k)
    dcq, g_q_norm = _rms_bwd(dcqn, proj, P["q_norm_g"], width=LORA, col=COL_CQ, tm=512, out_dtype=BF16, name="q_norm_bwd")
    dckv, g_kv_norm = _rms_bwd(dckvn, proj, P["kv_norm_g"], width=LORA, col=COL_CKV, tm=512, out_dtype=BF16, name="kv_norm_bwd")

    dproj, gws, dsv_sum, g_ln_g, g_ln_b = _gmlp_bwd(proj, dya, P["a_ln_g"], P["a_ln_b"], wm, bs_t, (dzb, dqm, dzm, dcq, dckv, dkr))
    g_a_w_s = gws * causal[None]
    g_a_b_s = dsv_sum.reshape(CHUNK, A_GROUPS, CHUNK).sum(axis=-1).T

    mid = ("w_uq", "w_ukv")
    mid_blocks = []
    for n, g in (("w_uq", g_wq), ("w_ukv", (g_wk, g_wv))):
        blocks, own[n] = _grad_blocks(n, _grad_reference_layout(n, g))
        mid_blocks.append(blocks)
    g_w_in, landed = _mm(h16, dproj, "tn", tm=1024, tn=896, tk=TN_TK, out_dtype=F32, name="gw_in", rider=_exchange_rider(mid_blocks))
    recv.update(zip(mid, landed))
    in_blocks, own["w_in"] = _grad_blocks("w_in", _grad_reference_layout("w_in", g_w_in))
    dh, landed = _mm(dproj, w_in, "nt", tm=1024, tn=1024, tk=2688, out_dtype=F32, name="d_h", add=dh_gate, rider=_exchange_rider([in_blocks]))
    recv["w_in"] = landed[0]
    grad_x, g_g_pre = _rms_bwd(dh, x, P["g_pre"], width=D, col=0, tm=512, out_dtype=F32, name="pre_norm_bwd", residual=dy)

    totals = [_reduce_add(own[n], recv[n], cidx, name=f"grad_reduce_{n}") for n in BIG]
    small = {"g_pre": g_g_pre, "a_ln_g": g_ln_g, "a_ln_b": g_ln_b, "a_w_s": g_a_w_s, "a_b_s": g_a_b_s,
             "q_norm_g": g_q_norm, "kv_norm_g": g_kv_norm, "mem_norm_g": g_mem_norm, "b_gate": g_b_gate, "g_post": g_g_post}
    return loss, grad_x, totals, small


def kernel(x, mem, positions, g_pre, w_in, a_ln_g, a_ln_b, a_w_s, a_b_s, q_norm_g, w_uq, kv_norm_g, w_ukv, mem_norm_g, w_mem_kv, w_gate, b_gate, w_branch, w_out, g_post, loss_target, m_g_pre, m_w_in, m_a_ln_g, m_a_ln_b, m_a_w_s, m_a_b_s, m_q_norm_g, m_w_uq, m_kv_norm_g, m_w_ukv, m_mem_norm_g, m_w_mem_kv, m_w_gate, m_b_gate, m_w_branch, m_w_out, m_g_post, v_g_pre, v_w_in, v_a_ln_g, v_a_ln_b, v_a_w_s, v_a_b_s, v_q_norm_g, v_w_uq, v_kv_norm_g, v_w_ukv, v_mem_norm_g, v_w_mem_kv, v_w_gate, v_b_gate, v_w_branch, v_w_out, v_g_post):
    w = dict(g_pre=g_pre, w_in=w_in, a_ln_g=a_ln_g, a_ln_b=a_ln_b, a_w_s=a_w_s, a_b_s=a_b_s, q_norm_g=q_norm_g, w_uq=w_uq,
             kv_norm_g=kv_norm_g, w_ukv=w_ukv, mem_norm_g=mem_norm_g, w_mem_kv=w_mem_kv, w_gate=w_gate, b_gate=b_gate,
             w_branch=w_branch, w_out=w_out, g_post=g_post)
    m = dict(g_pre=m_g_pre, w_in=m_w_in, a_ln_g=m_a_ln_g, a_ln_b=m_a_ln_b, a_w_s=m_a_w_s, a_b_s=m_a_b_s, q_norm_g=m_q_norm_g,
             w_uq=m_w_uq, kv_norm_g=m_kv_norm_g, w_ukv=m_w_ukv, mem_norm_g=m_mem_norm_g, w_mem_kv=m_w_mem_kv, w_gate=m_w_gate,
             b_gate=m_b_gate, w_branch=m_w_branch, w_out=m_w_out, g_post=m_g_post)
    v = dict(g_pre=v_g_pre, w_in=v_w_in, a_ln_g=v_a_ln_g, a_ln_b=v_a_ln_b, a_w_s=v_a_w_s, a_b_s=v_a_b_s, q_norm_g=v_q_norm_g,
             w_uq=v_w_uq, kv_norm_g=v_kv_norm_g, w_ukv=v_w_ukv, mem_norm_g=v_mem_norm_g, w_mem_kv=v_w_mem_kv, w_gate=v_w_gate,
             b_gate=v_b_gate, w_branch=v_w_branch, w_out=v_w_out, g_post=v_g_post)

    def two_d(t, n):
        return t[n].reshape(SHARD_2D[n]) if n in SHARD_2D else t[n].reshape(t[n].shape[1:] if t[n].ndim > 2 else t[n].shape)

    shards = [two_d(w, n).astype(BF16) for n in BIG]
    w_in_full = _w_in_layout(_allgather_chips(shards[:1])[0])
    P = {n: two_d(w, n) for n in SMALL}

    S = x.shape[1]
    loss_row, grad_x, totals, small = _local_step(x[0], mem[0], positions.reshape(S, 1), loss_target[0], w_in_full, shards[1:], P)

    from_sibling = _halves_exchange(totals)
    cidx = lax.axis_index("c").astype(jnp.int32).reshape(1)
    res = {}
    for n, own, other in zip(BIG, totals, from_sibling):
        upd = _adamw(two_d(w, n), own, other, two_d(m, n), two_d(v, n), cidx, name=f"adamw_{n}")
        for key, t in zip(("grad", "delta", "new_m", "new_v"), upd):
            res[key, n] = t.reshape(w[n].shape)

    small_out, loss_sum = _small_step(small, loss_row, P, {n: two_d(m, n) for n in SMALL}, {n: two_d(v, n) for n in SMALL})
    for n in SMALL:
        for key, t in zip(("grad", "delta", "new_m", "new_v"), small_out[n]):
            res[key, n] = t.reshape(w[n].shape)
    loss = loss_sum[0, 0]

    outs = [loss, grad_x[None]]
    for key in ("grad", "delta", "new_m", "new_v"):
        outs += [res[key, n] for n in WEIGHTS]
    return tuple(outs)
```

```python
import math
from typing import NamedTuple

import jax
import jax.numpy as jnp
from jax import lax
from jax.experimental import pallas as pl
from jax.experimental.pallas import tpu as pltpu

F32 = jnp.float32
BF16 = jnp.bfloat16

D = 2048
EPS = 1e-6
CHUNK = 128
A_GROUPS = 16
HEADS = 16
QK_NOPE = 128
QK_ROPE = 64
QK_DIM = QK_NOPE + QK_ROPE
V_DIM = 128
LORA = 512
MEM_HEADS = 4
MEM_HEAD_DIM = 512
ROPE_THETA = 10000.0
QK_PAD = 256
IN_REF = 13376
IN_PAD = 13440
COL_U, COL_V, COL_ZA, COL_ZB, COL_QM, COL_ZM = 0, 1, 2, 3, 4, 5
COL_CQ, COL_CKV = 24, 25
COL_KR = 104

ADAM_LR = 0.001
ADAM_B1 = 0.9
ADAM_B2 = 0.999
ADAM_EPS = 1e-08
ADAM_WD = 0.01
ADAM_STEP = 10

VMEM_LIMIT = 56 * 1024 * 1024
LANES = 128
LOG2E = math.log2(math.e)

BIG = ("w_in", "w_uq", "w_ukv", "w_mem_kv", "w_gate", "w_branch", "w_out")
SMALL = ("g_pre", "a_ln_g", "a_ln_b", "a_w_s", "a_b_s", "q_norm_g", "kv_norm_g", "mem_norm_g", "b_gate", "g_post")
WEIGHTS = ("g_pre", "w_in", "a_ln_g", "a_ln_b", "a_w_s", "a_b_s", "q_norm_g", "w_uq", "kv_norm_g", "w_ukv",
           "mem_norm_g", "w_mem_kv", "w_gate", "b_gate", "w_branch", "w_out", "g_post")
N_CHIPS = 4
N_DEV = 8


def _params(sem=None):
    return pltpu.CompilerParams(dimension_semantics=sem, vmem_limit_bytes=VMEM_LIMIT)


def _sigmoid(z):
    return 1.0 / (1.0 + jnp.exp(-z))


def _gelu_parts(x):
    c = math.sqrt(2.0 / math.pi)
    x2 = x * x
    t = jnp.tanh(c * (x + 0.044715 * x * x2))
    g = 0.5 * x * (1.0 + t)
    dg = 0.5 * (1.0 + t) + 0.5 * x * (1.0 - t * t) * (c * (1.0 + 3.0 * 0.044715 * x2))
    return g, dg


def _silu_parts(z):
    s = _sigmoid(z)
    return z * s, s * (1.0 + z * (1.0 - s))


def _dot(a, b, dims):
    return lax.dot_general(a, b, (dims, ((), ())), preferred_element_type=F32)


NN = ((1,), (0,))
NT = ((1,), (1,))
TN = ((0,), (0,))
TN_TK = 4096


class _Rider(NamedTuple):
    ins: tuple
    out_shapes: tuple
    n_sems: int
    phases: tuple


def _ride(rider, refs_in, refs_out, sems, step, total):
    for frac, fn in rider.phases:
        @pl.when(step == int(frac * (total - 1)))
        def _():
            fn(refs_in, refs_out, sems[0], sems[1])


def _mm(a, b, mode, *, tm, tn, tk, out_dtype, name, add=None, rider=None):
    if mode == "nn":
        (M, K), (_, N) = a.shape, b.shape
    elif mode == "nt":
        (M, K), (N, _) = a.shape, b.shape
    else:
        (K, M), (_, N) = a.shape, b.shape
    tm, tn, tk = min(tm, M), min(tn, N), min(tk, K)
    assert M % tm == 0 and N % tn == 0 and K % tk == 0, (name, M, N, K, tm, tn, tk)
    ni, nj, nk = M // tm, N // tn, K // tk
    dims = {"nn": NN, "nt": NT, "tn": TN}[mode]
    has_add = add is not None
    n_rin = len(rider.ins) if rider else 0
    n_rout = len(rider.out_shapes) if rider else 0

    def body(*refs):
        a_ref, b_ref = refs[0], refs[1]
        pos = 2
        add_ref = refs[pos] if has_add else None
        pos += int(has_add)
        rin = refs[pos:pos + n_rin]
        pos += n_rin
        o_ref = refs[pos]
        rout = refs[pos + 1:pos + 1 + n_rout]
        pos += 1 + n_rout
        acc = refs[pos] if nk > 1 else None
        sems = refs[-2:] if rider else None
        if rider:
            step = (pl.program_id(0) * ni + pl.program_id(1)) * nk + pl.program_id(2)
            _ride(rider._replace(phases=rider.phases[:1]), rin, rout, sems, step, nj * ni * nk)
        part = _dot(a_ref[...].astype(BF16), b_ref[...].astype(BF16), dims)

        def finish(r):
            if has_add:
                r = r + add_ref[...]
            o_ref[...] = r.astype(out_dtype)

        if nk == 1:
            finish(part)
        else:
            k = pl.program_id(2)

            @pl.when(k == 0)
            def _():
                acc[...] = part

            @pl.when(k > 0)
            def _():
                acc[...] += part

            @pl.when(k == nk - 1)
            def _():
                finish(acc[...])

        if rider:
            _ride(rider._replace(phases=rider.phases[1:]), rin, rout, sems, step, nj * ni * nk)

    if mode == "nn":
        a_spec = pl.BlockSpec((tm, tk), lambda j, i, k: (i, k))
        b_spec = pl.BlockSpec((tk, tn), lambda j, i, k: (k, j))
    elif mode == "nt":
        a_spec = pl.BlockSpec((tm, tk), lambda j, i, k: (i, k))
        b_spec = pl.BlockSpec((tn, tk), lambda j, i, k: (j, k))
    else:
        a_spec = pl.BlockSpec((tk, tm), lambda j, i, k: (k, i))
        b_spec = pl.BlockSpec((tk, tn), lambda j, i, k: (k, j))
    o_spec = pl.BlockSpec((tm, tn), lambda j, i, k: (i, j))
    hbm = pl.BlockSpec(memory_space=pl.ANY)
    in_specs = [a_spec, b_spec] + ([o_spec] if has_add else []) + [hbm] * n_rin
    args = (a, b) + ((add,) if has_add else ()) + (tuple(rider.ins) if rider else ())
    scratch = [pltpu.VMEM((tm, tn), F32)] if nk > 1 else []
    if rider:
        scratch += [pltpu.SemaphoreType.DMA((rider.n_sems,)), pltpu.SemaphoreType.DMA((rider.n_sems,))]
    res = pl.pallas_call(
        body, name=name, grid=(nj, ni, nk), in_specs=in_specs, out_specs=[o_spec] + [hbm] * n_rout,
        out_shape=[jax.ShapeDtypeStruct((M, N), out_dtype)] + (list(rider.out_shapes) if rider else []),
        scratch_shapes=scratch,
        compiler_params=_params(("arbitrary",) * 3 if rider else ("parallel", "parallel", "arbitrary")),
    )(*args)
    return (res[0], list(res[1:])) if rider else res[0]


def _rms_fwd(x, g, *, width, col, tm, name):
    rows = x.shape[0]
    tm = min(tm, rows)

    def body(x_ref, g_ref, y_ref):
        xv = x_ref[...].astype(F32)
        r = lax.rsqrt(jnp.mean(xv * xv, axis=-1, keepdims=True) + EPS)
        y_ref[...] = ((xv * r) * g_ref[...]).astype(BF16)

    return pl.pallas_call(
        body, name=name, grid=(rows // tm,),
        in_specs=[pl.BlockSpec((tm, width), lambda i: (i, col)), pl.BlockSpec((1, width), lambda i: (0, 0))],
        out_specs=pl.BlockSpec((tm, width), lambda i: (i, 0)),
        out_shape=jax.ShapeDtypeStruct((rows, width), BF16),
        compiler_params=_params(("parallel",)),
    )(x, g)


def _rms_bwd(d, x, g, *, width, col, tm, out_dtype, name, residual=None):
    rows = d.shape[0]
    tm = min(tm, rows)
    has_res = residual is not None

    def body(*refs):
        d_ref, x_ref, g_ref = refs[:3]
        res_ref = refs[3] if has_res else None
        dx_ref, gg_ref = refs[-2], refs[-1]
        dv = d_ref[...]
        xv = x_ref[...].astype(F32)
        r = lax.rsqrt(jnp.mean(xv * xv, axis=-1, keepdims=True) + EPS)
        n = xv * r

        @pl.when(pl.program_id(0) == 0)
        def _():
            gg_ref[...] = jnp.zeros_like(gg_ref)

        gg_ref[...] += jnp.sum(dv * n, axis=0, keepdims=True)
        gd = dv * g_ref[...]
        dx = r * (gd - n * jnp.mean(gd * n, axis=-1, keepdims=True))
        if has_res:
            dx = dx + res_ref[...]
        dx_ref[...] = dx.astype(out_dtype)

    blk = pl.BlockSpec((tm, width), lambda i: (i, 0))
    in_specs = [blk, pl.BlockSpec((tm, width), lambda i: (i, col)),
                pl.BlockSpec((1, width), lambda i: (0, 0))] + ([blk] if has_res else [])
    args = (d, x, g) + ((residual,) if has_res else ())
    return pl.pallas_call(
        body, name=name, grid=(rows // tm,), in_specs=in_specs,
        out_specs=[blk, pl.BlockSpec((1, width), lambda i: (0, 0))],
        out_shape=[jax.ShapeDtypeStruct((rows, width), out_dtype), jax.ShapeDtypeStruct((1, width), F32)],
        compiler_params=_params(("arbitrary",)),
    )(*args)


def _rope_tables(pos_col, inv_lane, *, tm):
    rows = pos_col.shape[0]
    tm = min(tm, rows)

    def body(p_ref, f_ref, c_ref, s1_ref, s2_ref):
        ang = p_ref[...].astype(F32) * f_ref[...]
        lane = lax.broadcasted_iota(jnp.int32, ang.shape, 1)
        c, s = jnp.cos(ang), jnp.sin(ang)
        half = QK_ROPE // 2
        c_ref[...] = jnp.where(lane < QK_ROPE, c, 0.0)
        s1_ref[...] = jnp.where(lane < half, -s, 0.0)
        s2_ref[...] = jnp.where((lane >= half) & (lane < QK_ROPE), s, 0.0)

    blk = pl.BlockSpec((tm, LANES), lambda i: (i, 0))
    return pl.pallas_call(
        body, name="rope_tables", grid=(rows // tm,),
        in_specs=[pl.BlockSpec((tm, 1), lambda i: (i, 0)), pl.BlockSpec((1, LANES), lambda i: (0, 0))],
        out_specs=[blk, blk, blk], out_shape=[jax.ShapeDtypeStruct((rows, LANES), F32)] * 3,
        compiler_params=_params(("parallel",)),
    )(pos_col, inv_lane)


def _rot(t, c, s1, s2, sign):
    r1 = pltpu.roll(t, LANES - QK_ROPE // 2, 1) * s1
    r2 = pltpu.roll(t, QK_ROPE // 2, 1) * s2
    return t * c + (r1 + r2) if sign > 0 else t * c - (r1 + r2)


def _mla_proj(cqn, ckvn, proj, tabs, wq, wk, wv, *, tm):
    rows = cqn.shape[0]
    tm = min(tm, rows)

    def body(cq_ref, ckv_ref, kr_ref, c_ref, s1_ref, s2_ref, wq_ref, wk_ref, wv_ref, q_ref, k_ref, v_ref):
        c, s1, s2 = c_ref[...], s1_ref[...], s2_ref[...]
        q = _dot(cq_ref[...], wq_ref[...], NN)
        k = _dot(ckv_ref[...], wk_ref[...], NN)
        kpe = _rot(kr_ref[...].astype(F32), c, s1, s2, 1).astype(BF16)
        for h in range(HEADS):
            lo = h * QK_PAD
            q_ref[:, lo:lo + QK_NOPE] = q[:, lo:lo + QK_NOPE].astype(BF16)
            q_ref[:, lo + QK_NOPE:lo + QK_PAD] = _rot(q[:, lo + QK_NOPE:lo + QK_PAD], c, s1, s2, 1).astype(BF16)
            k_ref[:, lo:lo + QK_NOPE] = k[:, lo:lo + QK_NOPE].astype(BF16)
            k_ref[:, lo + QK_NOPE:lo + QK_PAD] = kpe
        v_ref[...] = _dot(ckv_ref[...], wv_ref[...], NN).astype(BF16)

    def row(w):
        return pl.BlockSpec((tm, w), lambda i: (i, 0))

    def whole(w):
        return pl.BlockSpec(w.shape, lambda i: (0, 0))

    return pl.pallas_call(
        body, name="mla_proj", grid=(rows // tm,),
        in_specs=[row(LORA), row(LORA), pl.BlockSpec((tm, LANES), lambda i: (i, COL_KR)), row(LANES), row(LANES), row(LANES),
                  whole(wq), whole(wk), whole(wv)],
        out_specs=[row(HEADS * QK_PAD), row(HEADS * QK_PAD), row(HEADS * V_DIM)],
        out_shape=[jax.ShapeDtypeStruct((rows, HEADS * QK_PAD), BF16), jax.ShapeDtypeStruct((rows, HEADS * QK_PAD), BF16),
                   jax.ShapeDtypeStruct((rows, HEADS * V_DIM), BF16)],
        compiler_params=_params(("parallel",)),
    )(cqn, ckvn, proj, *tabs, wq, wk, wv)


def _mla_fwd(q, k, v, proj, *, t):
    S = q.shape[0]
    t = min(t, S // 2)
    n = S // t
    per = min(4, n)
    scale = QK_DIM ** -0.5

    def body(q_ref, k_ref, v_ref, z_ref, o_ref, y_ref, lse_ref):
        qi = pl.program_id(1)
        qv = q_ref[...]
        c2 = scale * LOG2E

        def block(k0, width, carry, row0):
            m_old, l_old, acc = carry
            ks = pl.ds(pl.multiple_of(k0, t), width)
            s = _dot(qv, k_ref[ks, :], NT)
            if row0 is not None:
                r = lax.broadcasted_iota(jnp.int32, s.shape, 0)
                c = lax.broadcasted_iota(jnp.int32, s.shape, 1)
                s = jnp.where(c <= r + row0, s, -1e30)
            m_new = jnp.maximum(m_old, jnp.max(s, axis=-1, keepdims=True))
            alpha = jnp.exp2((m_old - m_new) * c2)
            p = jnp.exp2((s - m_new) * c2)
            l_new = alpha * l_old + jnp.sum(p, axis=-1, keepdims=True)
            acc = alpha * acc + _dot(p.astype(BF16), v_ref[ks, :], NN)
            return m_new, l_new, acc

        init = (jnp.full((t, 1), -1e30, F32), jnp.zeros((t, 1), F32), jnp.zeros((t, V_DIM), F32))
        carry = lax.fori_loop(0, qi // per, lambda j, cr: block(j * (per * t), per * t, cr, None), init)
        last = [lambda cr, w=w: block((qi - w) * t, (w + 1) * t, cr, w * t) for w in range(per)]
        m_f, l_f, acc = lax.switch(qi % per, last, carry)
        o = acc / l_f
        o_ref[...] = o
        sz, _ = _silu_parts(z_ref[...].astype(F32))
        y_ref[...] = (o * sz).astype(BF16)
        lse2 = (m_f * scale + jnp.log(l_f)) * LOG2E
        lane = lax.broadcasted_iota(jnp.int32, (t, LANES), 1)
        lse_ref[0, 0] = jnp.where(lane == 0, lse2, 0.0).T[0:8, :]

    zcol = COL_ZB * (D // V_DIM)
    return pl.pallas_call(
        body, name="mla_fwd", grid=(HEADS, n),
        in_specs=[pl.BlockSpec((t, QK_PAD), lambda h, i: (i, h)),
                  pl.BlockSpec((S, QK_PAD), lambda h, i: (0, h)),
                  pl.BlockSpec((S, V_DIM), lambda h, i: (0, h)),
                  pl.BlockSpec((t, V_DIM), lambda h, i: (i, zcol + h))],
        out_specs=[pl.BlockSpec((t, V_DIM), lambda h, i: (i, h)), pl.BlockSpec((t, V_DIM), lambda h, i: (i, h)),
                   pl.BlockSpec((1, 1, 8, t), lambda h, i: (h, i, 0, 0))],
        out_shape=[jax.ShapeDtypeStruct((S, HEADS * V_DIM), F32), jax.ShapeDtypeStruct((S, HEADS * V_DIM), BF16),
                   jax.ShapeDtypeStruct((HEADS, n, 8, t), F32)],
        compiler_params=_params(("parallel", "parallel")),
    )(q, k, v, proj)


def _mla_gate_bwd(dy, o, proj, lse, *, tm):
    S = dy.shape[0]
    tm = min(tm, S)

    def body(dy_ref, o_ref, z_ref, lse_ref, do_ref, dz_ref, st_ref):
        sz, dsz = _silu_parts(z_ref[...].astype(F32))
        dyv, ov = dy_ref[...].astype(F32), o_ref[...]
        do = dyv * sz
        do_ref[...] = do.astype(BF16)
        dz_ref[...] = (dyv * ov * dsz).astype(BF16)
        prod = do * ov
        lane = lax.broadcasted_iota(jnp.int32, (tm, LANES), 1)
        for h in range(HEADS):
            delta = jnp.sum(prod[:, h * V_DIM:(h + 1) * V_DIM], axis=-1, keepdims=True)
            st_ref[h, 0] = lse_ref[h, 0] + jnp.where(lane == 1, delta, 0.0).T[0:8, :]

    blk = pl.BlockSpec((tm, D), lambda i: (i, 0))
    return pl.pallas_call(
        body, name="mla_gate_bwd", grid=(S // tm,),
        in_specs=[blk, blk, pl.BlockSpec((tm, D), lambda i: (i, COL_ZB)), pl.BlockSpec((HEADS, 1, 8, tm), lambda i: (0, i, 0, 0))],
        out_specs=[blk, blk, pl.BlockSpec((HEADS, 1, 8, tm), lambda i: (0, i, 0, 0))],
        out_shape=[jax.ShapeDtypeStruct((S, D), BF16), jax.ShapeDtypeStruct((S, D), BF16),
                   jax.ShapeDtypeStruct((HEADS, S // tm, 8, tm), F32)],
        compiler_params=_params(("parallel",)),
    )(dy, o, proj, lse)


def _mla_bwd(q, k, v, do, stats, *, t, rider):
    S = q.shape[0]
    t = min(t, S)
    n = S // t
    per = min(4, n)
    c2 = (QK_DIM ** -0.5) * LOG2E

    n_rin, n_rout = len(rider.ins), len(rider.out_shapes)

    def body(*refs):
        q_ref, k_ref, v_ref, do_ref, st_ref = refs[:5]
        rin = refs[5:5 + n_rin]
        dq_ref, dk_ref, dv_ref = refs[5 + n_rin:8 + n_rin]
        rout = refs[8 + n_rin:8 + n_rin + n_rout]
        ki = pl.program_id(1)
        step = pl.program_id(0) * n + ki
        _ride(rider._replace(phases=rider.phases[:1]), rin, rout, refs[-2:], step, HEADS * n)

        @pl.when(ki == 0)
        def _():
            dq_ref[...] = jnp.zeros_like(dq_ref)

        kv, vv = k_ref[...], v_ref[...]

        def block(i, carry, diag):
            dk, dv = carry
            rows = pl.ds(pl.multiple_of(i * t, t), t)
            qv, dov, st = q_ref[rows, :], do_ref[rows, :], st_ref[0, i]
            s = _dot(kv, qv, NT)
            if diag:
                key = lax.broadcasted_iota(jnp.int32, s.shape, 0)
                qry = lax.broadcasted_iota(jnp.int32, s.shape, 1)
                s = jnp.where(key <= qry, s, -1e30)
            p = jnp.exp2(s * c2 - st[0:1, :])
            p16 = p.astype(BF16)
            dv = dv + _dot(p16, dov, NN)
            dp = _dot(vv, dov, NT)
            ds = (p * (dp - st[1:2, :])).astype(BF16)
            dk = dk + _dot(ds, qv, NN)
            dq_ref[rows, :] += _dot(ds, kv, TN)
            return dk, dv

        carry = block(ki, (jnp.zeros((t, QK_PAD), F32), jnp.zeros((t, V_DIM), F32)), True)
        rest = n - 1 - ki

        def run(start, count):
            def f(cr):
                for u in range(count):
                    cr = block(start + u, cr, False)
                return cr
            return f

        carry = lax.switch(rest % per, [run(ki + 1, w) for w in range(per)], carry)
        first = ki + 1 + rest % per
        dk, dv = lax.fori_loop(0, rest // per, lambda i, cr: run(first + per * i, per)(cr), carry)
        dk_ref[...] = (dk * (QK_DIM ** -0.5)).astype(BF16)
        dv_ref[...] = dv.astype(BF16)
        _ride(rider._replace(phases=rider.phases[1:]), rin, rout, refs[-2:], step, HEADS * n)

    hbm = pl.BlockSpec(memory_space=pl.ANY)
    res = pl.pallas_call(
        body, name="mla_bwd", grid=(HEADS, n),
        in_specs=[pl.BlockSpec((S, QK_PAD), lambda h, j: (0, h)),
                  pl.BlockSpec((t, QK_PAD), lambda h, j: (j, h)),
                  pl.BlockSpec((t, V_DIM), lambda h, j: (j, h)),
                  pl.BlockSpec((S, V_DIM), lambda h, j: (0, h)),
                  pl.BlockSpec((1, n, 8, t), lambda h, j: (h, 0, 0, 0))] + [hbm] * n_rin,
        out_specs=[pl.BlockSpec((S, QK_PAD), lambda h, j: (0, h)),
                   pl.BlockSpec((t, QK_PAD), lambda h, j: (j, h)),
                   pl.BlockSpec((t, V_DIM), lambda h, j: (j, h))] + [hbm] * n_rout,
        out_shape=[jax.ShapeDtypeStruct((S, HEADS * QK_PAD), F32), jax.ShapeDtypeStruct((S, HEADS * QK_PAD), BF16),
                   jax.ShapeDtypeStruct((S, HEADS * V_DIM), BF16)] + list(rider.out_shapes),
        scratch_shapes=[pltpu.SemaphoreType.DMA((rider.n_sems,)), pltpu.SemaphoreType.DMA((rider.n_sems,))],
        compiler_params=_params(("arbitrary", "arbitrary")),
    )(q, k, v, do, stats, *rider.ins)
    return res[0], res[1], res[2], list(res[3:])


def _mla_qk_post(dq, dk, tabs, *, tm):
    S = dq.shape[0]
    tm = min(tm, S)
    scale = QK_DIM ** -0.5

    def body(dq_ref, dk_ref, c_ref, s1_ref, s2_ref, q16_ref, kr_ref):
        c, s1, s2 = c_ref[...], s1_ref[...], s2_ref[...]
        kpe = jnp.zeros((tm, LANES), F32)
        for h in range(HEADS):
            lo = h * QK_PAD
            q16_ref[:, lo:lo + QK_NOPE] = (dq_ref[:, lo:lo + QK_NOPE] * scale).astype(BF16)
            q16_ref[:, lo + QK_NOPE:lo + QK_PAD] = _rot(dq_ref[:, lo + QK_NOPE:lo + QK_PAD] * scale, c, s1, s2, -1).astype(BF16)
            kpe = kpe + dk_ref[:, lo + QK_NOPE:lo + QK_PAD].astype(F32)
        kr_ref[...] = _rot(kpe, c, s1, s2, -1).astype(BF16)

    wide = pl.BlockSpec((tm, HEADS * QK_PAD), lambda i: (i, 0))
    lane = pl.BlockSpec((tm, LANES), lambda i: (i, 0))
    return pl.pallas_call(
        body, name="mla_qk_post", grid=(S // tm,),
        in_specs=[wide, wide, lane, lane, lane], out_specs=[wide, lane],
        out_shape=[jax.ShapeDtypeStruct((S, HEADS * QK_PAD), BF16), jax.ShapeDtypeStruct((S, LANES), BF16)],
        compiler_params=_params(("parallel",)),
    )(dq, dk, *tabs)


def _mem_scores(q16, km_ref, h):
    lo = h * MEM_HEAD_DIM
    s = _dot(q16, km_ref[:, lo:lo + MEM_HEAD_DIM], NT) * (MEM_HEAD_DIM ** -0.5)
    e = jnp.exp(s - jnp.max(s, axis=-1, keepdims=True))
    return e / jnp.sum(e, axis=-1, keepdims=True)


def _mem_fwd(proj, kvm, *, tm):
    S = proj.shape[0]
    tm = min(tm, S)
    M = kvm.shape[0]

    def body(q_ref, z_ref, km_ref, vm_ref, y_ref):
        sz, _ = _silu_parts(z_ref[...].astype(F32))
        for h in range(MEM_HEADS):
            lo = h * MEM_HEAD_DIM
            p = _mem_scores(q_ref[:, lo:lo + MEM_HEAD_DIM].astype(BF16), km_ref, h)
            o = _dot(p.astype(BF16), vm_ref[:, lo:lo + MEM_HEAD_DIM], NN)
            y_ref[:, lo:lo + MEM_HEAD_DIM] = (o * sz[:, lo:lo + MEM_HEAD_DIM]).astype(BF16)

    return pl.pallas_call(
        body, name="mem_fwd", grid=(S // tm,),
        in_specs=[pl.BlockSpec((tm, D), lambda i: (i, COL_QM)), pl.BlockSpec((tm, D), lambda i: (i, COL_ZM)),
                  pl.BlockSpec((M, D), lambda i: (0, 0)), pl.BlockSpec((M, D), lambda i: (0, 1))],
        out_specs=pl.BlockSpec((tm, D), lambda i: (i, 0)),
        out_shape=jax.ShapeDtypeStruct((S, D), BF16),
        compiler_params=_params(("parallel",)),
    )(proj, proj, kvm, kvm)


def _mem_bwd(proj, kvm, dy, *, tm):
    S = proj.shape[0]
    tm = min(tm, S)
    M = kvm.shape[0]
    scale = MEM_HEAD_DIM ** -0.5

    def body(q_ref, z_ref, km_ref, vm_ref, dy_ref, dq_ref, dz_ref, dkv_ref):
        @pl.when(pl.program_id(0) == 0)
        def _():
            dkv_ref[...] = jnp.zeros_like(dkv_ref)

        sz, dsz = _silu_parts(z_ref[...].astype(F32))
        dyv = dy_ref[...].astype(F32)
        for h in range(MEM_HEADS):
            lo = h * MEM_HEAD_DIM
            sl = slice(lo, lo + MEM_HEAD_DIM)
            q16 = q_ref[:, sl].astype(BF16)
            p = _mem_scores(q16, km_ref, h)
            p16 = p.astype(BF16)
            o = _dot(p16, vm_ref[:, sl], NN)
            dy_h = dyv[:, sl]
            dz_ref[:, sl] = (dy_h * o * dsz[:, sl]).astype(BF16)
            do16 = (dy_h * sz[:, sl]).astype(BF16)
            dp = _dot(do16, vm_ref[:, sl], NT)
            ds = (p * (dp - jnp.sum(dp * p, axis=-1, keepdims=True)) * scale).astype(BF16)
            dq_ref[:, sl] = _dot(ds, km_ref[:, sl], NN).astype(BF16)
            dkv_ref[:, sl] += _dot(ds, q16, TN)
            dkv_ref[:, D + lo:D + lo + MEM_HEAD_DIM] += _dot(p16, do16, TN)

    blk = pl.BlockSpec((tm, D), lambda i: (i, 0))
    return pl.pallas_call(
        body, name="mem_bwd", grid=(S // tm,),
        in_specs=[pl.BlockSpec((tm, D), lambda i: (i, COL_QM)), pl.BlockSpec((tm, D), lambda i: (i, COL_ZM)),
                  pl.BlockSpec((M, D), lambda i: (0, 0)), pl.BlockSpec((M, D), lambda i: (0, 1)), blk],
        out_specs=[blk, blk, pl.BlockSpec((M, 2 * D), lambda i: (0, 0))],
        out_shape=[jax.ShapeDtypeStruct((S, D), BF16), jax.ShapeDtypeStruct((S, D), BF16),
                   jax.ShapeDtypeStruct((M, 2 * D), F32)],
        compiler_params=_params(("arbitrary",)),
    )(proj, proj, kvm, kvm, dy)


def _gmlp_common(u_ref, v_ref, lng_ref, lnb_ref):
    u, du = _gelu_parts(u_ref[...].astype(F32))
    vg, dvg = _gelu_parts(v_ref[...].astype(F32))
    mu = jnp.mean(vg, axis=-1, keepdims=True)
    vc = vg - mu
    r = lax.rsqrt(jnp.mean(vc * vc, axis=-1, keepdims=True) + EPS)
    vhat = vc * r
    vn = vhat * lng_ref[...] + lnb_ref[...]
    return u, du, dvg, r, vhat, vn.astype(BF16)


def _gmlp_fwd(proj, ln_g, ln_b, wm, bs_t):
    S = proj.shape[0]

    def body(u_ref, v_ref, z_ref, lng_ref, lnb_ref, wm_ref, bs_ref, y_ref):
        u, _, _, _, _, v16 = _gmlp_common(u_ref, v_ref, lng_ref, lnb_ref)
        sz, _ = _silu_parts(z_ref[...].astype(F32))
        for g in range(A_GROUPS):
            sl = slice(g * CHUNK, (g + 1) * CHUNK)
            sv = _dot(wm_ref[g], v16[:, sl], NN) + bs_ref[:, g:g + 1]
            y_ref[:, sl] = (u[:, sl] * sv * sz[:, sl]).astype(BF16)

    def col(c):
        return pl.BlockSpec((CHUNK, D), lambda i: (i, c))

    vec = pl.BlockSpec((1, D), lambda i: (0, 0))
    return pl.pallas_call(
        body, name="gmlp_fwd", grid=(S // CHUNK,),
        in_specs=[col(COL_U), col(COL_V), col(COL_ZA), vec, vec,
                  pl.BlockSpec((A_GROUPS, CHUNK, CHUNK), lambda i: (0, 0, 0)), pl.BlockSpec((CHUNK, A_GROUPS), lambda i: (0, 0))],
        out_specs=col(0), out_shape=jax.ShapeDtypeStruct((S, D), BF16),
        compiler_params=_params(("parallel",)),
    )(proj, proj, proj, ln_g, ln_b, wm, bs_t)


def _gmlp_bwd(proj, dy, ln_g, ln_b, wm, bs_t, others):
    S = proj.shape[0]

    def body(u_ref, v_ref, z_ref, dy_ref, lng_ref, lnb_ref, wm_ref, bs_ref, zb_ref, qm_ref, zm_ref, cq_ref, ckv_ref, kr_ref,
             dp_ref, gws_ref, dsv_ref, glg_ref, glb_ref, dvn_s):
        dp_ref[:, COL_ZB * D:(COL_ZB + 1) * D] = zb_ref[...]
        dp_ref[:, COL_QM * D:(COL_QM + 1) * D] = qm_ref[...]
        dp_ref[:, COL_ZM * D:(COL_ZM + 1) * D] = zm_ref[...]
        dp_ref[:, COL_CQ * LORA:(COL_CQ + 1) * LORA] = cq_ref[...]
        dp_ref[:, COL_CKV * LORA:(COL_CKV + 1) * LORA] = ckv_ref[...]
        dp_ref[:, COL_KR * LANES:(COL_KR + 1) * LANES] = kr_ref[...]

        @pl.when(pl.program_id(0) == 0)
        def _():
            gws_ref[...] = jnp.zeros_like(gws_ref)
            dsv_ref[...] = jnp.zeros_like(dsv_ref)
            glg_ref[...] = jnp.zeros_like(glg_ref)
            glb_ref[...] = jnp.zeros_like(glb_ref)

        u, du, dvg, r, vhat, v16 = _gmlp_common(u_ref, v_ref, lng_ref, lnb_ref)
        sz, dsz = _silu_parts(z_ref[...].astype(F32))
        dyv = dy_ref[...].astype(F32)
        for g in range(A_GROUPS):
            sl = slice(g * CHUNK, (g + 1) * CHUNK)
            sv = _dot(wm_ref[g], v16[:, sl], NN) + bs_ref[:, g:g + 1]
            dy_g, u_g, sz_g = dyv[:, sl], u[:, sl], sz[:, sl]
            dsv = dy_g * u_g * sz_g
            dp_ref[:, g * CHUNK:(g + 1) * CHUNK] = (dy_g * sv * sz_g * du[:, sl]).astype(BF16)
            dp_ref[:, COL_ZA * D + g * CHUNK:COL_ZA * D + (g + 1) * CHUNK] = (dy_g * u_g * sv * dsz[:, sl]).astype(BF16)
            dsv16 = dsv.astype(BF16)
            dvn_s[:, sl] = _dot(wm_ref[g], dsv16, TN)
            gws_ref[g] += _dot(dsv16, v16[:, sl], NT)
            dsv_ref[:, sl] += dsv
        dvn = dvn_s[...]
        glb_ref[...] += jnp.sum(dvn, axis=0, keepdims=True)
        glg_ref[...] += jnp.sum(dvn * vhat, axis=0, keepdims=True)
        dvh = dvn * lng_ref[...]
        dvc = r * (dvh - jnp.mean(dvh, axis=-1, keepdims=True) - vhat * jnp.mean(dvh * vhat, axis=-1, keepdims=True))
        dp_ref[:, COL_V * D:(COL_V + 1) * D] = (dvc * dvg).astype(BF16)

    def col(c):
        return pl.BlockSpec((CHUNK, D), lambda i: (i, c))

    def rows(width):
        return pl.BlockSpec((CHUNK, width), lambda i: (i, 0))

    vec = pl.BlockSpec((1, D), lambda i: (0, 0))
    wsp = pl.BlockSpec((A_GROUPS, CHUNK, CHUNK), lambda i: (0, 0, 0))
    return pl.pallas_call(
        body, name="gmlp_bwd", grid=(S // CHUNK,),
        in_specs=[col(COL_U), col(COL_V), col(COL_ZA), col(0), vec, vec, wsp, pl.BlockSpec((CHUNK, A_GROUPS), lambda i: (0, 0)),
                  rows(D), rows(D), rows(D), rows(LORA), rows(LORA), rows(LANES)],
        out_specs=[rows(IN_PAD), wsp, pl.BlockSpec((CHUNK, D), lambda i: (0, 0)), vec, vec],
        out_shape=[jax.ShapeDtypeStruct((S, IN_PAD), BF16),
                   jax.ShapeDtypeStruct((A_GROUPS, CHUNK, CHUNK), F32), jax.ShapeDtypeStruct((CHUNK, D), F32),
                   jax.ShapeDtypeStruct((1, D), F32), jax.ShapeDtypeStruct((1, D), F32)],
        scratch_shapes=[pltpu.VMEM((CHUNK, D), F32)],
        compiler_params=_params(("arbitrary",)),
    )(proj, proj, proj, dy, ln_g, ln_b, wm, bs_t, *others)


def _gate_merge(h16, ys, wg, bg, wbs, *, tm, tn):
    S = h16.shape[0]
    tm = min(tm, S)
    nj = D // tn

    def body(h_ref, ya_ref, yb_ref, ym_ref, wg0, wg1, wg2, bg0, bg1, bg2, wb0, wb1, wb2,
             mg_ref, g0_ref, g1_ref, g2_ref, p0_ref, p1_ref, p2_ref):
        hv = h_ref[...]
        acc = None
        for y_ref, wg_ref, bgr, wb_ref, g_ref, p_ref in ((ya_ref, wg0, bg0, wb0, g0_ref, p0_ref),
                                                         (yb_ref, wg1, bg1, wb1, g1_ref, p1_ref),
                                                         (ym_ref, wg2, bg2, wb2, g2_ref, p2_ref)):
            gate = _sigmoid(_dot(hv, wg_ref[...], NN) + bgr[...])
            p = _dot(y_ref[...], wb_ref[...], NN)
            g_ref[...] = gate.astype(BF16)
            p_ref[...] = p.astype(BF16)
            acc = gate * p if acc is None else acc + gate * p
        mg_ref[...] = acc.astype(BF16)

    a_spec = pl.BlockSpec((tm, D), lambda j, i: (i, 0))
    o_spec = pl.BlockSpec((tm, tn), lambda j, i: (i, j))

    def wgs(n):
        return pl.BlockSpec((D, tn), lambda j, i: (0, n * nj + j))

    def bgs(n):
        return pl.BlockSpec((1, tn), lambda j, i: (0, n * nj + j))

    wbsp = pl.BlockSpec((D, tn), lambda j, i: (0, j))
    return pl.pallas_call(
        body, name="gate_merge", grid=(nj, S // tm),
        in_specs=[a_spec] * 4 + [wgs(0), wgs(1), wgs(2), bgs(0), bgs(1), bgs(2), wbsp, wbsp, wbsp],
        out_specs=[o_spec] * 7, out_shape=[jax.ShapeDtypeStruct((S, D), BF16)] * 7,
        compiler_params=_params(("parallel", "parallel")),
    )(h16, *ys, wg, wg, wg, bg, bg, bg, *wbs)


def _gate_bwd(dmerged, gates, ps, *, tm):
    S = dmerged.shape[0]
    tm = min(tm, S)

    def body(dm_ref, g0, g1, g2, p0, p1, p2, dp0, dp1, dp2, dg_ref, gb_ref):
        @pl.when(pl.program_id(0) == 0)
        def _():
            gb_ref[...] = jnp.zeros_like(gb_ref)

        dm = dm_ref[...].astype(F32)
        for n, (g_ref, p_ref, dp_ref) in enumerate(((g0, p0, dp0), (g1, p1, dp1), (g2, p2, dp2))):
            gate = g_ref[...].astype(F32)
            dp_ref[...] = (dm * gate).astype(BF16)
            dg = dm * p_ref[...].astype(F32) * gate * (1.0 - gate)
            dg_ref[:, n * D:(n + 1) * D] = dg.astype(BF16)
            gb_ref[:, n * D:(n + 1) * D] += jnp.sum(dg, axis=0, keepdims=True)

    blk = pl.BlockSpec((tm, D), lambda i: (i, 0))
    return pl.pallas_call(
        body, name="gate_bwd", grid=(S // tm,),
        in_specs=[blk] * 7,
        out_specs=[blk, blk, blk, pl.BlockSpec((tm, 3 * D), lambda i: (i, 0)), pl.BlockSpec((1, 3 * D), lambda i: (0, 0))],
        out_shape=[jax.ShapeDtypeStruct((S, D), BF16)] * 3 + [jax.ShapeDtypeStruct((S, 3 * D), BF16),
                                                              jax.ShapeDtypeStruct((1, 3 * D), F32)],
        compiler_params=_params(("arbitrary",)),
    )(dmerged, *gates, *ps)


def _post_loss(x, out, target, g_post, *, tm):
    S = x.shape[0]
    tm = min(tm, S)

    def body(x_ref, o_ref, t_ref, g_ref, dy_ref, do_ref, gg_ref, ls_ref):
        @pl.when(pl.program_id(0) == 0)
        def _():
            gg_ref[...] = jnp.zeros_like(gg_ref)
            ls_ref[...] = jnp.zeros_like(ls_ref)

        ov = o_ref[...]
        r = lax.rsqrt(jnp.mean(ov * ov, axis=-1, keepdims=True) + EPS)
        n = ov * r
        err = (x_ref[...] + n * g_ref[...]) - t_ref[...]
        ls_ref[...] += 0.5 * jnp.sum(jnp.mean(err * err, axis=-1, keepdims=True))
        dy = err * (1.0 / D)
        dy_ref[...] = dy
        gg_ref[...] += jnp.sum(dy * n, axis=0, keepdims=True)
        gd = dy * g_ref[...]
        do_ref[...] = (r * (gd - n * jnp.mean(gd * n, axis=-1, keepdims=True))).astype(BF16)

    blk = pl.BlockSpec((tm, D), lambda i: (i, 0))
    vec = pl.BlockSpec((1, D), lambda i: (0, 0))
    return pl.pallas_call(
        body, name="post_loss", grid=(S // tm,),
        in_specs=[blk, blk, blk, vec],
        out_specs=[blk, blk, vec, pl.BlockSpec((1, LANES), lambda i: (0, 0))],
        out_shape=[jax.ShapeDtypeStruct((S, D), F32), jax.ShapeDtypeStruct((S, D), BF16),
                   jax.ShapeDtypeStruct((1, D), F32), jax.ShapeDtypeStruct((1, LANES), F32)],
        compiler_params=_params(("arbitrary",)),
    )(x, out, target, g_post)


def _adamw(w, g_own, g_other, m, v, cidx, *, name):
    rows, width = w.shape
    hh = rows // 2
    tr = _row_tile(hh, width, unit=8)
    nb = hh // tr

    def body(c_ref, w_ref, own_ref, oth_ref, m_ref, v_ref, g_ref, d_ref, nm_ref, nv_ref):
        mine = (pl.program_id(0) // nb) == c_ref[0]
        gv = jnp.where(mine, own_ref[...], oth_ref[...])
        delta, nm, nv = _adam_math(w_ref[...], gv, m_ref[...], v_ref[...])
        g_ref[...] = gv
        d_ref[...] = delta
        nm_ref[...] = nm
        nv_ref[...] = nv

    blk = pl.BlockSpec((tr, width), lambda i, c_ref: (i, 0))
    half = pl.BlockSpec((tr, width), lambda i, c_ref: (i % nb, 0))
    return pl.pallas_call(
        body, name=name,
        grid_spec=pltpu.PrefetchScalarGridSpec(num_scalar_prefetch=1, grid=(rows // tr,),
                                               in_specs=[blk, half, half, blk, blk], out_specs=[blk] * 4),
        out_shape=[jax.ShapeDtypeStruct((rows, width), F32)] * 4,
        compiler_params=_params(("parallel",)),
    )(cidx, w, g_own, g_other, m, v)


MESH = pl.DeviceIdType.MESH
ANY = pl.BlockSpec(memory_space=pl.ANY)


def _place():
    return lax.axis_index("x"), lax.axis_index("y"), lax.axis_index("c")


def _other_chips(x, y):
    return [(1 - x, y), (x, 1 - y), (1 - x, 1 - y)]


def _remote(src, dst, send_sem, recv_sem, dev):
    return pltpu.make_async_remote_copy(src_ref=src, dst_ref=dst, send_sem=send_sem, recv_sem=recv_sem,
                                        device_id=dev, device_id_type=MESH)


def _allgather_chips(shards):
    nw = len(shards)

    def body(*refs):
        x_refs, out_refs = refs[:nw], refs[nw:2 * nw]
        send_sems, recv_sems = refs[2 * nw:]
        x, y, c = _place()
        sibling = (x, y, 1 - c)
        chips = _other_chips(x, y)

        def half(w, px, py, hc):
            hh = shards[w].shape[0] // 2
            return out_refs[w].at[2 * px + py, pl.ds(hc * hh, hh), :]

        sent = []
        for w in range(nw):
            hh = shards[w].shape[0] // 2
            for k, (px, py) in enumerate(chips):
                cp = _remote(x_refs[w].at[pl.ds(c * hh, hh), :], half(w, x, y, c), send_sems.at[6 * w + k],
                             recv_sems.at[6 * w + k], (px, py, c))
                cp.start()
                sent.append(cp)
        for w in range(nw):
            for k, (px, py) in enumerate(chips):
                landed = half(w, px, py, c)
                _remote(landed, landed, send_sems.at[6 * w + k], recv_sems.at[6 * w + k], (px, py, c)).wait_recv()
                cp = _remote(landed, landed, send_sems.at[6 * w + 3 + k], recv_sems.at[6 * w + 3 + k], sibling)
                cp.start()
                sent.append(cp)
        for w in range(nw):
            for k, (px, py) in enumerate(chips):
                other = half(w, px, py, 1 - c)
                _remote(other, other, send_sems.at[6 * w + 3 + k], recv_sems.at[6 * w + 3 + k], sibling).wait_recv()
        for cp in sent:
            cp.wait_send()

    outs = pl.pallas_call(
        body, name="allgather_weights", in_specs=[ANY] * nw, out_specs=[ANY] * nw,
        out_shape=[jax.ShapeDtypeStruct((N_CHIPS,) + s.shape, s.dtype) for s in shards],
        scratch_shapes=[pltpu.SemaphoreType.DMA((6 * nw,)), pltpu.SemaphoreType.DMA((6 * nw,))],
    )(*shards)
    own = 2 * lax.axis_index("x") + lax.axis_index("y")
    return [lax.dynamic_update_slice(o, s[None], (own, 0, 0)) for o, s in zip(outs, shards)]


def _row_tile(rows, cols, unit=16, budget=2 * 1024 * 1024):
    best = unit
    for t in range(unit, rows + 1, unit):
        if rows % t == 0 and t * cols * 4 <= budget:
            best = t
    assert rows % best == 0, (rows, cols)
    return best


def _peers(x, y, c):
    out = []
    for k in range(1, N_DEV):
        out.append((k, (1 - x if (k >> 2) & 1 else x, 1 - y if (k >> 1) & 1 else y, 1 - c if k & 1 else c)))
    return out


def _gather_rider(shards):
    nw = len(shards)

    def half(outs, w, px, py, hc):
        hh = shards[w].shape[0] // 2
        return outs[w].at[2 * px + py, pl.ds(hc * hh, hh), :]

    def ici(ins, outs, ss, rs, w, k, px, py, c, x, y):
        hh = shards[w].shape[0] // 2
        return _remote(ins[w].at[pl.ds(c * hh, hh), :], half(outs, w, x, y, c), ss.at[6 * w + k], rs.at[6 * w + k], (px, py, c))

    def passing(outs, ss, rs, w, k, px, py, hc, sibling):
        landed = half(outs, w, px, py, hc)
        return _remote(landed, landed, ss.at[6 * w + 3 + k], rs.at[6 * w + 3 + k], sibling)

    def start(ins, outs, ss, rs):
        x, y, c = _place()
        for w in range(nw):
            for k, (px, py) in enumerate(_other_chips(x, y)):
                ici(ins, outs, ss, rs, w, k, px, py, c, x, y).start()

    def forward(ins, outs, ss, rs):
        x, y, c = _place()
        for w in range(nw):
            for k, (px, py) in enumerate(_other_chips(x, y)):
                landed = half(outs, w, px, py, c)
                _remote(landed, landed, ss.at[6 * w + k], rs.at[6 * w + k], (px, py, c)).wait_recv()
                passing(outs, ss, rs, w, k, px, py, c, (x, y, 1 - c)).start()

    def finish(ins, outs, ss, rs):
        x, y, c = _place()
        for w in range(nw):
            for k, (px, py) in enumerate(_other_chips(x, y)):
                passing(outs, ss, rs, w, k, px, py, 1 - c, (x, y, 1 - c)).wait_recv()
        for w in range(nw):
            for k, (px, py) in enumerate(_other_chips(x, y)):
                ici(ins, outs, ss, rs, w, k, px, py, c, x, y).wait_send()
                passing(outs, ss, rs, w, k, px, py, c, (x, y, 1 - c)).wait_send()

    return _Rider(ins=tuple(shards), out_shapes=tuple(jax.ShapeDtypeStruct((N_CHIPS,) + s.shape, s.dtype) for s in shards),
                  n_sems=6 * nw, phases=((0.0, start), (0.8, forward), (1.0, finish)))


def _own_blocks_in_place(gathered, shards):
    own = 2 * lax.axis_index("x") + lax.axis_index("y")
    return [lax.dynamic_update_slice(o, s[None], (own, 0, 0)) for o, s in zip(gathered, shards)]


def _exchange_rider(blocks):
    nw = len(blocks)

    def copy(ins, outs, ss, rs, w, k, peer):
        hh = blocks[w].shape[1] // 2
        px, py, pc = peer
        return _remote(ins[w].at[2 * px + py, pl.ds(pc * hh, hh), :], outs[w].at[k - 1], ss.at[7 * w + k - 1], rs.at[7 * w + k - 1], peer)

    def start(ins, outs, ss, rs):
        for w in range(nw):
            for k, peer in _peers(*_place()):
                copy(ins, outs, ss, rs, w, k, peer).start()

    def finish(ins, outs, ss, rs):
        for w in range(nw):
            for k, peer in _peers(*_place()):
                copy(ins, outs, ss, rs, w, k, peer).wait()

    return _Rider(ins=tuple(blocks),
                  out_shapes=tuple(jax.ShapeDtypeStruct((N_DEV - 1, b.shape[1] // 2, b.shape[2]), b.dtype) for b in blocks),
                  n_sems=7 * nw, phases=((0.0, start), (1.0, finish)))


def _reduce_add(own, recv, cidx, *, name):
    R, W = own.shape
    hh = R // 2
    tr = _row_tile(hh, W, budget=1024 * 1024)
    nb = hh // tr

    def body(c_ref, o_ref, r_ref, t_ref):
        s = o_ref[...]
        for k in range(N_DEV - 1):
            s = s + r_ref[k].astype(F32)
        t_ref[...] = s

    return pl.pallas_call(
        body, name=name,
        grid_spec=pltpu.PrefetchScalarGridSpec(
            num_scalar_prefetch=1, grid=(nb,),
            in_specs=[pl.BlockSpec((tr, W), lambda i, c_ref: (i + c_ref[0] * nb, 0)),
                      pl.BlockSpec((N_DEV - 1, tr, W), lambda i, c_ref: (0, i, 0))],
            out_specs=pl.BlockSpec((tr, W), lambda i, c_ref: (i, 0))),
        out_shape=jax.ShapeDtypeStruct((hh, W), F32),
        compiler_params=_params(("parallel",)),
    )(cidx, own, recv)


def _halves_exchange(ts):
    nw = len(ts)

    def body(*refs):
        t_refs, out_refs = refs[:nw], refs[nw:2 * nw]
        send_sems, recv_sems = refs[2 * nw:]
        x, y, c = _place()
        cps = []
        for w in range(nw):
            cp = _remote(t_refs[w], out_refs[w], send_sems.at[w], recv_sems.at[w], (x, y, 1 - c))
            cp.start()
            cps.append(cp)
        for cp in cps:
            cp.wait()

    return pl.pallas_call(
        body, name="grad_halves_exchange", in_specs=[ANY] * nw, out_specs=[ANY] * nw,
        out_shape=[jax.ShapeDtypeStruct(t.shape, t.dtype) for t in ts],
        scratch_shapes=[pltpu.SemaphoreType.DMA((nw,)), pltpu.SemaphoreType.DMA((nw,))],
    )(*ts)


def _adam_math(w, g, m, v):
    nm = ADAM_B1 * m + (1.0 - ADAM_B1) * g
    nv = ADAM_B2 * v + (1.0 - ADAM_B2) * (g * g)
    c1 = 1.0 - ADAM_B1 ** ADAM_STEP
    c2 = 1.0 - ADAM_B2 ** ADAM_STEP
    return -ADAM_LR * ((nm / c1) / (jnp.sqrt(nv / c2) + ADAM_EPS) + ADAM_WD * w), nm, nv


STAGE_ROWS = 32
STAGE_VEC = {"g_pre": 0, "a_ln_g": 1, "a_ln_b": 2, "mem_norm_g": 3, "g_post": 4}
STAGE_BGATE = 5
STAGE_MIX = 8
STAGE_ABS = 16


def _small_step(g, loss_row, w, m, v):
    n = len(SMALL)

    def reduce_body(*refs):
        g_r = dict(zip(SMALL, refs[:n]))
        loss_r = refs[n]
        sa_o, sw_o = refs[n + 1], refs[n + 2]
        stage, ga, gw, aws16, send_sems, recv_sems = refs[n + 3:]

        stage[...] = jnp.zeros_like(stage)
        for name, row in STAGE_VEC.items():
            stage[row:row + 1, :] = g_r[name][...]
        for t in range(3):
            stage[STAGE_BGATE + t:STAGE_BGATE + t + 1, :] = g_r["b_gate"][:, t * D:(t + 1) * D]
        stage[STAGE_MIX:STAGE_MIX + 1, 0:LORA] = g_r["q_norm_g"][...]
        stage[STAGE_MIX:STAGE_MIX + 1, LORA:2 * LORA] = g_r["kv_norm_g"][...]
        stage[STAGE_MIX:STAGE_MIX + 1, 2 * LORA:2 * LORA + LANES] = loss_r[...]
        stage[STAGE_ABS:STAGE_ABS + A_GROUPS, 0:CHUNK] = g_r["a_b_s"][...]

        x, y, c = _place()
        me = 4 * x + 2 * y + c
        ga[me] = stage[...]
        aws16[...] = g_r["a_w_s"][...].astype(BF16)
        gw[me] = aws16[...]
        cps, srcs = [], []
        for k in range(1, N_DEV):
            fx, fy, fc = (k >> 2) & 1, (k >> 1) & 1, k & 1
            peer = (1 - x if fx else x, 1 - y if fy else y, 1 - c if fc else c)
            for j, (src, dst) in enumerate(((stage, ga), (aws16, gw))):
                cp = _remote(src, dst.at[me], send_sems.at[2 * (k - 1) + j], recv_sems.at[2 * (k - 1) + j], peer)
                cp.start()
                cps.append(cp)
            srcs.append(4 * peer[0] + 2 * peer[1] + peer[2])
        for k, src in enumerate(srcs):
            _remote(stage, ga.at[src], send_sems.at[2 * k], recv_sems.at[2 * k], (x, y, c)).wait_recv()
            _remote(aws16, gw.at[src], send_sems.at[2 * k + 1], recv_sems.at[2 * k + 1], (x, y, c)).wait_recv()
        for cp in cps:
            cp.wait_send()
        sa, sw = ga[0], gw[0].astype(F32)
        for d in range(1, N_DEV):
            sa = sa + ga[d]
            sw = sw + gw[d].astype(F32)
        sa_o[...] = sa
        sw_o[...] = sw

    def update_body(*refs):
        sa, sw = refs[0][...], refs[1][...]
        w_r = dict(zip(SMALL, refs[2:n + 2]))
        m_r = dict(zip(SMALL, refs[n + 2:2 * n + 2]))
        v_r = dict(zip(SMALL, refs[2 * n + 2:3 * n + 2]))
        outs = refs[3 * n + 2:7 * n + 2]
        o_r = {name: outs[4 * i:4 * i + 4] for i, name in enumerate(SMALL)}
        loss_o = refs[7 * n + 2]

        def update(name, gsum, cols=None):
            sel = (slice(None), cols) if cols is not None else Ellipsis
            delta, nm, nv = _adam_math(w_r[name][sel], gsum, m_r[name][sel], v_r[name][sel])
            for ref, val in zip(o_r[name], (gsum, delta, nm, nv)):
                ref[sel] = val

        for name, row in STAGE_VEC.items():
            update(name, sa[row:row + 1, :])
        for t in range(3):
            update("b_gate", sa[STAGE_BGATE + t:STAGE_BGATE + t + 1, :], slice(t * D, (t + 1) * D))
        update("q_norm_g", sa[STAGE_MIX:STAGE_MIX + 1, 0:LORA])
        update("kv_norm_g", sa[STAGE_MIX:STAGE_MIX + 1, LORA:2 * LORA])
        update("a_b_s", sa[STAGE_ABS:STAGE_ABS + A_GROUPS, 0:CHUNK])
        update("a_w_s", sw)
        loss_o[...] = sa[STAGE_MIX:STAGE_MIX + 1, 2 * LORA:2 * LORA + LANES]

    vm = pl.BlockSpec(memory_space=pltpu.VMEM)
    sa, sw = pl.pallas_call(
        reduce_body, name="small_allreduce", in_specs=[vm] * (n + 1), out_specs=[vm, vm],
        out_shape=[jax.ShapeDtypeStruct((STAGE_ROWS, D), F32), jax.ShapeDtypeStruct((A_GROUPS, CHUNK, CHUNK), F32)],
        scratch_shapes=[pltpu.VMEM((STAGE_ROWS, D), F32), pltpu.VMEM((N_DEV, STAGE_ROWS, D), F32),
                        pltpu.VMEM((N_DEV, A_GROUPS, CHUNK, CHUNK), BF16), pltpu.VMEM((A_GROUPS, CHUNK, CHUNK), BF16),
                        pltpu.SemaphoreType.DMA((2 * (N_DEV - 1),)), pltpu.SemaphoreType.DMA((2 * (N_DEV - 1),))],
        compiler_params=pltpu.CompilerParams(vmem_limit_bytes=VMEM_LIMIT),
    )(*[g[k] for k in SMALL], loss_row)
    ins = [sa, sw] + [w[k] for k in SMALL] + [m[k] for k in SMALL] + [v[k] for k in SMALL]
    out_shape = [jax.ShapeDtypeStruct(w[k].shape, F32) for k in SMALL for _ in range(4)] + [jax.ShapeDtypeStruct((1, LANES), F32)]
    res = pl.pallas_call(
        update_body, name="small_adamw", in_specs=[vm] * len(ins), out_specs=[vm] * len(out_shape), out_shape=out_shape,
        compiler_params=pltpu.CompilerParams(vmem_limit_bytes=VMEM_LIMIT),
    )(*ins)
    return {k: tuple(res[4 * i:4 * i + 4]) for i, k in enumerate(SMALL)}, res[-1]


SHARD_2D = {"w_in": (D, IN_REF // N_CHIPS), "w_uq": (LORA, HEADS * QK_DIM // N_CHIPS),
            "w_ukv": (LORA, HEADS * (QK_NOPE + V_DIM) // N_CHIPS), "w_mem_kv": (D, 2 * D // N_CHIPS),
            "w_gate": (D, 3 * D // N_CHIPS), "w_branch": (3 * D // N_CHIPS, D), "w_out": (D // N_CHIPS, D)}


def _cols(blocks):
    return jnp.concatenate([blocks[j] for j in range(N_CHIPS)], axis=1)


def _w_in_layout(gathered):
    w = _cols(gathered)
    return jnp.concatenate([w[:, :3 * D], w[:, 3 * D + 2 * LORA + QK_ROPE:], w[:, 3 * D:3 * D + 2 * LORA + QK_ROPE],
                            jnp.zeros((D, IN_PAD - IN_REF), w.dtype)], axis=1)


REST = BIG[1:]


def _rest_layouts(gathered):
    wq = jnp.pad(_cols(gathered["w_uq"]).reshape(LORA, HEADS, QK_DIM), ((0, 0), (0, 0), (0, QK_PAD - QK_DIM))).reshape(LORA, HEADS * QK_PAD)
    kv3 = _cols(gathered["w_ukv"]).reshape(LORA, HEADS, QK_NOPE + V_DIM)
    wk = jnp.pad(kv3[:, :, :QK_NOPE], ((0, 0), (0, 0), (0, QK_PAD - QK_NOPE))).reshape(LORA, HEADS * QK_PAD)
    wv = kv3[:, :, QK_NOPE:].reshape(LORA, HEADS * V_DIM)
    w_branch = gathered["w_branch"].reshape(N_CHIPS, 3, D // N_CHIPS, D).transpose(1, 0, 2, 3).reshape(3, D, D)
    return {"wq": wq, "wk": wk, "wv": wv, "w_mem_kv": _cols(gathered["w_mem_kv"]), "w_gate": _cols(gathered["w_gate"]),
            "w_branch": w_branch, "w_out": gathered["w_out"].reshape(D, D)}


def _grad_reference_layout(name, g):
    if name == "w_in":
        return jnp.concatenate([g[:, :3 * D], g[:, 6 * D:6 * D + 2 * LORA + QK_ROPE], g[:, 3 * D:6 * D]], axis=1)
    if name == "w_uq":
        return g.reshape(LORA, HEADS, QK_PAD)[:, :, :QK_DIM].reshape(LORA, HEADS * QK_DIM)
    if name == "w_ukv":
        gk, gv = g
        return jnp.concatenate([gk.reshape(LORA, HEADS, QK_PAD)[:, :, :QK_NOPE], gv.reshape(LORA, HEADS, V_DIM)],
                               axis=2).reshape(LORA, HEADS * (QK_NOPE + V_DIM))
    return g


def _grad_blocks(name, full):
    own = 2 * lax.axis_index("x") + lax.axis_index("y")
    R, C = SHARD_2D[name]
    if name == "w_branch":
        blocks = full.reshape(3, N_CHIPS, D // N_CHIPS, D).transpose(1, 0, 2, 3).reshape(N_CHIPS, R, C)
        mine = lax.dynamic_slice_in_dim(full, own * (D // N_CHIPS), D // N_CHIPS, axis=1).reshape(R, C)
    elif name == "w_out":
        blocks = full.reshape(N_CHIPS, R, C)
        mine = lax.dynamic_slice_in_dim(full, own * R, R, axis=0)
    else:
        blocks = full.reshape(R, N_CHIPS, C).transpose(1, 0, 2)
        mine = lax.dynamic_slice_in_dim(full, own * C, C, axis=1)
    return blocks.astype(BF16), mine


def _local_step(x, mem, pos_col, target, w_in, rest_shards, P):
    cidx = lax.axis_index("c").astype(jnp.int32).reshape(1)
    h16 = _rms_fwd(x, P["g_pre"], width=D, col=0, tm=256, name="pre_norm")
    memn16 = _rms_fwd(mem, P["mem_norm_g"], width=D, col=0, tm=256, name="mem_norm")
    proj, rest = _mm(h16, w_in, "nn", tm=1024, tn=1920, tk=D, out_dtype=BF16, name="in_proj", rider=_gather_rider(rest_shards))
    W = _rest_layouts(dict(zip(REST, _own_blocks_in_place(rest, rest_shards))))

    causal = jnp.tril(jnp.ones((CHUNK, CHUNK), F32))
    wm = (P["a_w_s"] * causal[None]).astype(BF16)
    bs_t = P["a_b_s"].T
    ya = _gmlp_fwd(proj, P["a_ln_g"], P["a_ln_b"], wm, bs_t)

    inv = 1.0 / (ROPE_THETA ** (jnp.arange(0, QK_ROPE, 2, dtype=F32) / QK_ROPE))
    inv_lane = jnp.concatenate([inv, inv, jnp.zeros((LANES - QK_ROPE,), F32)])[None]
    tabs = _rope_tables(pos_col, inv_lane, tm=1024)
    cqn = _rms_fwd(proj, P["q_norm_g"], width=LORA, col=COL_CQ, tm=512, name="q_norm")
    ckvn = _rms_fwd(proj, P["kv_norm_g"], width=LORA, col=COL_CKV, tm=512, name="kv_norm")
    q16, k16, v16 = _mla_proj(cqn, ckvn, proj, tabs, W["wq"], W["wk"], W["wv"], tm=256)
    o_b, yb, lse = _mla_fwd(q16, k16, v16, proj, t=512)

    kvm = _mm(memn16, W["w_mem_kv"], "nn", tm=256, tn=1024, tk=D, out_dtype=BF16, name="mem_kv")
    ym = _mem_fwd(proj, kvm, tm=512)

    wbs = [W["w_branch"][n] for n in range(3)]
    merged, g0, g1, g2, p0, p1, p2 = _gate_merge(h16, (ya, yb, ym), W["w_gate"], P["b_gate"], wbs, tm=512, tn=512)
    out = _mm(merged, W["w_out"], "nn", tm=512, tn=1024, tk=D, out_dtype=F32, name="out_proj")
    dy, dout, g_g_post, loss = _post_loss(x, out, target, P["g_post"], tm=256)

    full = {}
    full["w_out"] = _mm(merged, dout, "tn", tm=1024, tn=1024, tk=TN_TK, out_dtype=F32, name="gw_out")
    dmerged = _mm(dout, W["w_out"], "nt", tm=512, tn=1024, tk=D, out_dtype=BF16, name="d_merged")
    dp0, dp1, dp2, dgpre, g_b_gate = _gate_bwd(dmerged, (g0, g1, g2), (p0, p1, p2), tm=256)
    full["w_gate"] = _mm(h16, dgpre, "tn", tm=1024, tn=1024, tk=TN_TK, out_dtype=F32, name="gw_gate")
    dh_gate = _mm(dgpre, W["w_gate"], "nt", tm=1024, tn=1024, tk=3 * D // 2, out_dtype=F32, name="dh_gate")
    full["w_branch"] = jnp.stack([_mm(y, dp, "tn", tm=1024, tn=1024, tk=TN_TK, out_dtype=F32, name=f"gw_branch{n}")
                                  for n, (y, dp) in enumerate(((ya, dp0), (yb, dp1), (ym, dp2)))], axis=0)
    dya, dyb, dym = [_mm(dp, wbs[n], "nt", tm=512, tn=1024, tk=D, out_dtype=BF16, name=f"dy_branch{n}")
                     for n, dp in enumerate((dp0, dp1, dp2))]

    dqm, dzm, dkvm = _mem_bwd(proj, kvm, dym, tm=512)
    dkvm16 = dkvm.astype(BF16)
    full["w_mem_kv"] = _mm(memn16, dkvm16, "tn", tm=1024, tn=1024, tk=256, out_dtype=F32, name="gw_mem_kv")
    dmemn = _mm(dkvm16, W["w_mem_kv"], "nt", tm=256, tn=1024, tk=2 * D, out_dtype=F32, name="d_memn")
    _, g_mem_norm = _rms_bwd(dmemn, mem, P["mem_norm_g"], width=D, col=0, tm=256, out_dtype=BF16, name="mem_norm_bwd")

    own, recv = {}, {}
    early = ("w_out", "w_gate", "w_branch", "w_mem_kv")
    early_blocks = []
    for n in early:
        blocks, own[n] = _grad_blocks(n, full[n])
        early_blocks.append(blocks)
    do16, dzb, stats = _mla_gate_bwd(dyb, o_b, proj, lse, tm=512)
    dq, dk16, dv16, landed = _mla_bwd(q16, k16, v16, do16, stats, t=512, rider=_exchange_rider(early_blocks))
    recv.update(zip(early, landed))
    dq16, dkr = _mla_qk_post(dq, dk16, tabs, tm=256)
    g_wq = _mm(cqn, dq16, "tn", tm=512, tn=1024, tk=TN_TK, out_dtype=F32, name="gw_uq")
    g_wk = _mm(ckvn, dk16, "tn", tm=512, tn=1024, tk=TN_TK, out_dtype=F32, name="gw_uk")
    g_wv = _mm(ckvn, dv16, "tn", tm=512, tn=1024, tk=TN_TK, out_dtype=F32, name="gw_uv")
    dcqn = _mm(dq16, W["wq"], "nt", tm=512, tn=LORA, tk=HEADS * QK_PAD, out_dtype=F32, name="d_cqn")
    dckvn_k = _mm(dk16, W["wk"], "nt", tm=512, tn=LORA, tk=HEADS * QK_PAD, out_dtype=F32, name="d_ckvn_k")
    dckvn = _mm(dv16, W["wv"], "nt", tm=512, tn=LORA, tk=HEADS * V_DIM, out_dtype=F32, name="d_ckvn", add=dckvn_k)
    dcq, g_q_norm = _rms_bwd(dcqn, proj, P["q_norm_g"], width=LORA, col=COL_CQ, tm=512, out_dtype=BF16, name="q_norm_bwd")
    dckv, g_kv_norm = _rms_bwd(dckvn, proj, P["kv_norm_g"], width=LORA, col=COL_CKV, tm=512, out_dtype=BF16, name="kv_norm_bwd")

    dproj, gws, dsv_sum, g_ln_g, g_ln_b = _gmlp_bwd(proj, dya, P["a_ln_g"], P["a_ln_b"], wm, bs_t, (dzb, dqm, dzm, dcq, dckv, dkr))
    g_a_w_s = gws * causal[None]
    g_a_b_s = dsv_sum.reshape(CHUNK, A_GROUPS, CHUNK).sum(axis=-1).T

    mid = ("w_uq", "w_ukv")
    mid_blocks = []
    for n, g in (("w_uq", g_wq), ("w_ukv", (g_wk, g_wv))):
        blocks, own[n] = _grad_blocks(n, _grad_reference_layout(n, g))
        mid_blocks.append(blocks)
    g_w_in, landed = _mm(h16, dproj, "tn", tm=1024, tn=896, tk=TN_TK, out_dtype=F32, name="gw_in", rider=_exchange_rider(mid_blocks))
    recv.update(zip(mid, landed))
    in_blocks, own["w_in"] = _grad_blocks("w_in", _grad_reference_layout("w_in", g_w_in))
    dh, landed = _mm(dproj, w_in, "nt", tm=1024, tn=1024, tk=2688, out_dtype=F32, name="d_h", add=dh_gate, rider=_exchange_rider([in_blocks]))
    recv["w_in"] = landed[0]
    grad_x, g_g_pre = _rms_bwd(dh, x, P["g_pre"], width=D, col=0, tm=512, out_dtype=F32, name="pre_norm_bwd", residual=dy)

    totals = [_reduce_add(own[n], recv[n], cidx, name=f"grad_reduce_{n}") for n in BIG]
    small = {"g_pre": g_g_pre, "a_ln_g": g_ln_g, "a_ln_b": g_ln_b, "a_w_s": g_a_w_s, "a_b_s": g_a_b_s,
             "q_norm_g": g_q_norm, "kv_norm_g": g_kv_norm, "mem_norm_g": g_mem_norm, "b_gate": g_b_gate, "g_post": g_g_post}
    return loss, grad_x, totals, small


def kernel(x, mem, positions, g_pre, w_in, a_ln_g, a_ln_b, a_w_s, a_b_s, q_norm_g, w_uq, kv_norm_g, w_ukv, mem_norm_g, w_mem_kv, w_gate, b_gate, w_branch, w_out, g_post, loss_target, m_g_pre, m_w_in, m_a_ln_g, m_a_ln_b, m_a_w_s, m_a_b_s, m_q_norm_g, m_w_uq, m_kv_norm_g, m_w_ukv, m_mem_norm_g, m_w_mem_kv, m_w_gate, m_b_gate, m_w_branch, m_w_out, m_g_post, v_g_pre, v_w_in, v_a_ln_g, v_a_ln_b, v_a_w_s, v_a_b_s, v_q_norm_g, v_w_uq, v_kv_norm_g, v_w_ukv, v_mem_norm_g, v_w_mem_kv, v_w_gate, v_b_gate, v_w_branch, v_w_out, v_g_post):
    w = dict(g_pre=g_pre, w_in=w_in, a_ln_g=a_ln_g, a_ln_b=a_ln_b, a_w_s=a_w_s, a_b_s=a_b_s, q_norm_g=q_norm_g, w_uq=w_uq,
             kv_norm_g=kv_norm_g, w_ukv=w_ukv, mem_norm_g=mem_norm_g, w_mem_kv=w_mem_kv, w_gate=w_gate, b_gate=b_gate,
             w_branch=w_branch, w_out=w_out, g_post=g_post)
    m = dict(g_pre=m_g_pre, w_in=m_w_in, a_ln_g=m_a_ln_g, a_ln_b=m_a_ln_b, a_w_s=m_a_w_s, a_b_s=m_a_b_s, q_norm_g=m_q_norm_g,
             w_uq=m_w_uq, kv_norm_g=m_kv_norm_g, w_ukv=m_w_ukv, mem_norm_g=m_mem_norm_g, w_mem_kv=m_w_mem_kv, w_gate=m_w_gate,
             b_gate=m_b_gate, w_branch=m_w_branch, w_out=m_w_out, g_post=m_g_post)
    v = dict(g_pre=v_g_pre, w_in=v_w_in, a_ln_g=v_a_ln_g, a_ln_b=v_a_ln_b, a_w_s=v_a_w_s, a_b_s=v_a_b_s, q_norm_g=v_q_norm_g,
             w_uq=v_w_uq, kv_norm_g=v_kv_norm_g, w_ukv=v_w_ukv, mem_norm_g=v_mem_norm_g, w_mem_kv=v_w_mem_kv, w_gate=v_w_gate,
             b_gate=v_b_gate, w_branch=v_w_branch, w_out=v_w_out, g_post=v_g_post)

    def two_d(t, n):
        return t[n].reshape(SHARD_2D[n]) if n in SHARD_2D else t[n].reshape(t[n].shape[1:] if t[n].ndim > 2 else t[n].shape)

    shards = [two_d(w, n).astype(BF16) for n in BIG]
    w_in_full = _w_in_layout(_allgather_chips(shards[:1])[0])
    P = {n: two_d(w, n) for n in SMALL}

    S = x.shape[1]
    loss_row, grad_x, totals, small = _local_step(x[0], mem[0], positions.reshape(S, 1), loss_target[0], w_in_full, shards[1:], P)

    from_sibling = _halves_exchange(totals)
    cidx = lax.axis_index("c").astype(jnp.int32).reshape(1)
    res = {}
    for n, own, other in zip(BIG, totals, from_sibling):
        upd = _adamw(two_d(w, n), own, other, two_d(m, n), two_d(v, n), cidx, name=f"adamw_{n}")
        for key, t in zip(("grad", "delta", "new_m", "new_v"), upd):
            res[key, n] = t.reshape(w[n].shape)

    small_out, loss_sum = _small_step(small, loss_row, P, {n: two_d(m, n) for n in SMALL}, {n: two_d(v, n) for n in SMALL})
    for n in SMALL:
        for key, t in zip(("grad", "delta", "new_m", "new_v"), small_out[n]):
            res[key, n] = t.reshape(w[n].shape)
    loss = loss_sum[0, 0]

    outs = [loss, grad_x[None]]
    for key in ("grad", "delta", "new_m", "new_v"):
        outs += [res[key, n] for n in WEIGHTS]
    return tuple(outs)
```

```python
import math
from typing import NamedTuple

import jax
import jax.numpy as jnp
from jax import lax
from jax.experimental import pallas as pl
from jax.experimental.pallas import tpu as pltpu

F32 = jnp.float32
BF16 = jnp.bfloat16

D = 2048
EPS = 1e-6
CHUNK = 128
A_GROUPS = 16
HEADS = 16
QK_NOPE = 128
QK_ROPE = 64
QK_DIM = QK_NOPE + QK_ROPE
V_DIM = 128
LORA = 512
MEM_HEADS = 4
MEM_HEAD_DIM = 512
ROPE_THETA = 10000.0
QK_PAD = 256
IN_REF = 13376
IN_PAD = 13440
COL_U, COL_V, COL_ZA, COL_ZB, COL_QM, COL_ZM = 0, 1, 2, 3, 4, 5
COL_CQ, COL_CKV = 24, 25
COL_KR = 104

ADAM_LR = 0.001
ADAM_B1 = 0.9
ADAM_B2 = 0.999
ADAM_EPS = 1e-08
ADAM_WD = 0.01
ADAM_STEP = 10

VMEM_LIMIT = 56 * 1024 * 1024
LANES = 128
LOG2E = math.log2(math.e)

BIG = ("w_in", "w_uq", "w_ukv", "w_mem_kv", "w_gate", "w_branch", "w_out")
SMALL = ("g_pre", "a_ln_g", "a_ln_b", "a_w_s", "a_b_s", "q_norm_g", "kv_norm_g", "mem_norm_g", "b_gate", "g_post")
WEIGHTS = ("g_pre", "w_in", "a_ln_g", "a_ln_b", "a_w_s", "a_b_s", "q_norm_g", "w_uq", "kv_norm_g", "w_ukv",
           "mem_norm_g", "w_mem_kv", "w_gate", "b_gate", "w_branch", "w_out", "g_post")
N_CHIPS = 4
N_DEV = 8


def _params(sem=None):
    return pltpu.CompilerParams(dimension_semantics=sem, vmem_limit_bytes=VMEM_LIMIT)


def _sigmoid(z):
    return 1.0 / (1.0 + jnp.exp(-z))


def _gelu_parts(x):
    c = math.sqrt(2.0 / math.pi)
    x2 = x * x
    t = jnp.tanh(c * (x + 0.044715 * x * x2))
    g = 0.5 * x * (1.0 + t)
    dg = 0.5 * (1.0 + t) + 0.5 * x * (1.0 - t * t) * (c * (1.0 + 3.0 * 0.044715 * x2))
    return g, dg


def _silu_parts(z):
    s = _sigmoid(z)
    return z * s, s * (1.0 + z * (1.0 - s))


def _dot(a, b, dims):
    return lax.dot_general(a, b, (dims, ((), ())), preferred_element_type=F32)


NN = ((1,), (0,))
NT = ((1,), (1,))
TN = ((0,), (0,))
TN_TK = 4096


class _Rider(NamedTuple):
    ins: tuple
    out_shapes: tuple
    n_sems: int
    phases: tuple


def _ride(rider, refs_in, refs_out, sems, step, total):
    for frac, fn in rider.phases:
        @pl.when(step == int(frac * (total - 1)))
        def _():
            fn(refs_in, refs_out, sems[0], sems[1])


def _mm(a, b, mode, *, tm, tn, tk, out_dtype, name, add=None, rider=None):
    if mode == "nn":
        (M, K), (_, N) = a.shape, b.shape
    elif mode == "nt":
        (M, K), (N, _) = a.shape, b.shape
    else:
        (K, M), (_, N) = a.shape, b.shape
    tm, tn, tk = min(tm, M), min(tn, N), min(tk, K)
    assert M % tm == 0 and N % tn == 0 and K % tk == 0, (name, M, N, K, tm, tn, tk)
    ni, nj, nk = M // tm, N // tn, K // tk
    dims = {"nn": NN, "nt": NT, "tn": TN}[mode]
    has_add = add is not None
    n_rin = len(rider.ins) if rider else 0
    n_rout = len(rider.out_shapes) if rider else 0

    def body(*refs):
        a_ref, b_ref = refs[0], refs[1]
        pos = 2
        add_ref = refs[pos] if has_add else None
        pos += int(has_add)
        rin = refs[pos:pos + n_rin]
        pos += n_rin
        o_ref = refs[pos]
        rout = refs[pos + 1:pos + 1 + n_rout]
        pos += 1 + n_rout
        acc = refs[pos] if nk > 1 else None
        sems = refs[-2:] if rider else None
        if rider:
            step = (pl.program_id(0) * ni + pl.program_id(1)) * nk + pl.program_id(2)
            _ride(rider._replace(phases=rider.phases[:1]), rin, rout, sems, step, nj * ni * nk)
        part = _dot(a_ref[...].astype(BF16), b_ref[...].astype(BF16), dims)

        def finish(r):
            if has_add:
                r = r + add_ref[...]
            o_ref[...] = r.astype(out_dtype)

        if nk == 1:
            finish(part)
        else:
            k = pl.program_id(2)

            @pl.when(k == 0)
            def _():
                acc[...] = part

            @pl.when(k > 0)
            def _():
                acc[...] += part

            @pl.when(k == nk - 1)
            def _():
                finish(acc[...])

        if rider:
            _ride(rider._replace(phases=rider.phases[1:]), rin, rout, sems, step, nj * ni * nk)

    if mode == "nn":
        a_spec = pl.BlockSpec((tm, tk), lambda j, i, k: (i, k))
        b_spec = pl.BlockSpec((tk, tn), lambda j, i, k: (k, j))
    elif mode == "nt":
        a_spec = pl.BlockSpec((tm, tk), lambda j, i, k: (i, k))
        b_spec = pl.BlockSpec((tn, tk), lambda j, i, k: (j, k))
    else:
        a_spec = pl.BlockSpec((tk, tm), lambda j, i, k: (k, i))
        b_spec = pl.BlockSpec((tk, tn), lambda j, i, k: (k, j))
    o_spec = pl.BlockSpec((tm, tn), lambda j, i, k: (i, j))
    hbm = pl.BlockSpec(memory_space=pl.ANY)
    in_specs = [a_spec, b_spec] + ([o_spec] if has_add else []) + [hbm] * n_rin
    args = (a, b) + ((add,) if has_add else ()) + (tuple(rider.ins) if rider else ())
    scratch = [pltpu.VMEM((tm, tn), F32)] if nk > 1 else []
    if rider:
        scratch += [pltpu.SemaphoreType.DMA((rider.n_sems,)), pltpu.SemaphoreType.DMA((rider.n_sems,))]
    res = pl.pallas_call(
        body, name=name, grid=(nj, ni, nk), in_specs=in_specs, out_specs=[o_spec] + [hbm] * n_rout,
        out_shape=[jax.ShapeDtypeStruct((M, N), out_dtype)] + (list(rider.out_shapes) if rider else []),
        scratch_shapes=scratch,
        compiler_params=_params(("arbitrary",) * 3 if rider else ("parallel", "parallel", "arbitrary")),
    )(*args)
    return (res[0], list(res[1:])) if rider else res[0]


def _rms_fwd(x, g, *, width, col, tm, name):
    rows = x.shape[0]
    tm = min(tm, rows)

    def body(x_ref, g_ref, y_ref):
        xv = x_ref[...].astype(F32)
        r = lax.rsqrt(jnp.mean(xv * xv, axis=-1, keepdims=True) + EPS)
        y_ref[...] = ((xv * r) * g_ref[...]).astype(BF16)

    return pl.pallas_call(
        body, name=name, grid=(rows // tm,),
        in_specs=[pl.BlockSpec((tm, width), lambda i: (i, col)), pl.BlockSpec((1, width), lambda i: (0, 0))],
        out_specs=pl.BlockSpec((tm, width), lambda i: (i, 0)),
        out_shape=jax.ShapeDtypeStruct((rows, width), BF16),
        compiler_params=_params(("parallel",)),
    )(x, g)


def _rms_bwd(d, x, g, *, width, col, tm, out_dtype, name, residual=None):
    rows = d.shape[0]
    tm = min(tm, rows)
    has_res = residual is not None

    def body(*refs):
        d_ref, x_ref, g_ref = refs[:3]
        res_ref = refs[3] if has_res else None
        dx_ref, gg_ref = refs[-2], refs[-1]
        dv = d_ref[...]
        xv = x_ref[...].astype(F32)
        r = lax.rsqrt(jnp.mean(xv * xv, axis=-1, keepdims=True) + EPS)
        n = xv * r

        @pl.when(pl.program_id(0) == 0)
        def _():
            gg_ref[...] = jnp.zeros_like(gg_ref)

        gg_ref[...] += jnp.sum(dv * n, axis=0, keepdims=True)
        gd = dv * g_ref[...]
        dx = r * (gd - n * jnp.mean(gd * n, axis=-1, keepdims=True))
        if has_res:
            dx = dx + res_ref[...]
        dx_ref[...] = dx.astype(out_dtype)

    blk = pl.BlockSpec((tm, width), lambda i: (i, 0))
    in_specs = [blk, pl.BlockSpec((tm, width), lambda i: (i, col)),
                pl.BlockSpec((1, width), lambda i: (0, 0))] + ([blk] if has_res else [])
    args = (d, x, g) + ((residual,) if has_res else ())
    return pl.pallas_call(
        body, name=name, grid=(rows // tm,), in_specs=in_specs,
        out_specs=[blk, pl.BlockSpec((1, width), lambda i: (0, 0))],
        out_shape=[jax.ShapeDtypeStruct((rows, width), out_dtype), jax.ShapeDtypeStruct((1, width), F32)],
        compiler_params=_params(("arbitrary",)),
    )(*args)


def _rope_tables(pos_col, inv_lane, *, tm):
    rows = pos_col.shape[0]
    tm = min(tm, rows)

    def body(p_ref, f_ref, c_ref, s1_ref, s2_ref):
        ang = p_ref[...].astype(F32) * f_ref[...]
        lane = lax.broadcasted_iota(jnp.int32, ang.shape, 1)
        c, s = jnp.cos(ang), jnp.sin(ang)
        half = QK_ROPE // 2
        c_ref[...] = jnp.where(lane < QK_ROPE, c, 0.0)
        s1_ref[...] = jnp.where(lane < half, -s, 0.0)
        s2_ref[...] = jnp.where((lane >= half) & (lane < QK_ROPE), s, 0.0)

    blk = pl.BlockSpec((tm, LANES), lambda i: (i, 0))
    return pl.pallas_call(
        body, name="rope_tables", grid=(rows // tm,),
        in_specs=[pl.BlockSpec((tm, 1), lambda i: (i, 0)), pl.BlockSpec((1, LANES), lambda i: (0, 0))],
        out_specs=[blk, blk, blk], out_shape=[jax.ShapeDtypeStruct((rows, LANES), F32)] * 3,
        compiler_params=_params(("parallel",)),
    )(pos_col, inv_lane)


def _rot(t, c, s1, s2, sign):
    r1 = pltpu.roll(t, LANES - QK_ROPE // 2, 1) * s1
    r2 = pltpu.roll(t, QK_ROPE // 2, 1) * s2
    return t * c + (r1 + r2) if sign > 0 else t * c - (r1 + r2)


def _mla_proj(cqn, ckvn, proj, tabs, wq, wk, wv, *, tm):
    rows = cqn.shape[0]
    tm = min(tm, rows)

    def body(cq_ref, ckv_ref, kr_ref, c_ref, s1_ref, s2_ref, wq_ref, wk_ref, wv_ref, q_ref, k_ref, v_ref):
        c, s1, s2 = c_ref[...], s1_ref[...], s2_ref[...]
        q = _dot(cq_ref[...], wq_ref[...], NN)
        k = _dot(ckv_ref[...], wk_ref[...], NN)
        kpe = _rot(kr_ref[...].astype(F32), c, s1, s2, 1).astype(BF16)
        for h in range(HEADS):
            lo = h * QK_PAD
            q_ref[:, lo:lo + QK_NOPE] = q[:, lo:lo + QK_NOPE].astype(BF16)
            q_ref[:, lo + QK_NOPE:lo + QK_PAD] = _rot(q[:, lo + QK_NOPE:lo + QK_PAD], c, s1, s2, 1).astype(BF16)
            k_ref[:, lo:lo + QK_NOPE] = k[:, lo:lo + QK_NOPE].astype(BF16)
            k_ref[:, lo + QK_NOPE:lo + QK_PAD] = kpe
        v_ref[...] = _dot(ckv_ref[...], wv_ref[...], NN).astype(BF16)

    def row(w):
        return pl.BlockSpec((tm, w), lambda i: (i, 0))

    def whole(w):
        return pl.BlockSpec(w.shape, lambda i: (0, 0))

    return pl.pallas_call(
        body, name="mla_proj", grid=(rows // tm,),
        in_specs=[row(LORA), row(LORA), pl.BlockSpec((tm, LANES), lambda i: (i, COL_KR)), row(LANES), row(LANES), row(LANES),
                  whole(wq), whole(wk), whole(wv)],
        out_specs=[row(HEADS * QK_PAD), row(HEADS * QK_PAD), row(HEADS * V_DIM)],
        out_shape=[jax.ShapeDtypeStruct((rows, HEADS * QK_PAD), BF16), jax.ShapeDtypeStruct((rows, HEADS * QK_PAD), BF16),
                   jax.ShapeDtypeStruct((rows, HEADS * V_DIM), BF16)],
        compiler_params=_params(("parallel",)),
    )(cqn, ckvn, proj, *tabs, wq, wk, wv)


def _mla_fwd(q, k, v, proj, *, t):
    S = q.shape[0]
    t = min(t, S // 2)
    n = S // t
    per = min(8, n)
    scale = QK_DIM ** -0.5

    def body(q_ref, k_ref, v_ref, z_ref, o_ref, y_ref, lse_ref):
        qi = pl.program_id(1)
        qv = q_ref[...]
        c2 = scale * LOG2E

        def block(k0, width, carry, row0):
            m_old, l_old, acc = carry
            ks = pl.ds(pl.multiple_of(k0, t), width)
            s = _dot(qv, k_ref[ks, :], NT)
            if row0 is not None:
                r = lax.broadcasted_iota(jnp.int32, s.shape, 0)
                c = lax.broadcasted_iota(jnp.int32, s.shape, 1)
                s = jnp.where(c <= r + row0, s, -1e30)
            m_new = jnp.maximum(m_old, jnp.max(s, axis=-1, keepdims=True))
            alpha = jnp.exp2((m_old - m_new) * c2)
            p = jnp.exp2((s - m_new) * c2)
            l_new = alpha * l_old + jnp.sum(p, axis=-1, keepdims=True)
            acc = alpha * acc + _dot(p.astype(BF16), v_ref[ks, :], NN)
            return m_new, l_new, acc

        init = (jnp.full((t, 1), -1e30, F32), jnp.zeros((t, 1), F32), jnp.zeros((t, V_DIM), F32))
        carry = lax.fori_loop(0, qi // per, lambda j, cr: block(j * (per * t), per * t, cr, None), init)
        last = [lambda cr, w=w: block((qi - w) * t, (w + 1) * t, cr, w * t) for w in range(per)]
        m_f, l_f, acc = lax.switch(qi % per, last, carry)
        o = acc / l_f
        o_ref[...] = o
        sz, _ = _silu_parts(z_ref[...].astype(F32))
        y_ref[...] = (o * sz).astype(BF16)
        lse2 = (m_f * scale + jnp.log(l_f)) * LOG2E
        lane = lax.broadcasted_iota(jnp.int32, (t, LANES), 1)
        lse_ref[0, 0] = jnp.where(lane == 0, lse2, 0.0).T[0:8, :]

    zcol = COL_ZB * (D // V_DIM)
    return pl.pallas_call(
        body, name="mla_fwd", grid=(HEADS, n),
        in_specs=[pl.BlockSpec((t, QK_PAD), lambda h, i: (i, h)),
                  pl.BlockSpec((S, QK_PAD), lambda h, i: (0, h)),
                  pl.BlockSpec((S, V_DIM), lambda h, i: (0, h)),
                  pl.BlockSpec((t, V_DIM), lambda h, i: (i, zcol + h))],
        out_specs=[pl.BlockSpec((t, V_DIM), lambda h, i: (i, h)), pl.BlockSpec((t, V_DIM), lambda h, i: (i, h)),
                   pl.BlockSpec((1, 1, 8, t), lambda h, i: (h, i, 0, 0))],
        out_shape=[jax.ShapeDtypeStruct((S, HEADS * V_DIM), F32), jax.ShapeDtypeStruct((S, HEADS * V_DIM), BF16),
                   jax.ShapeDtypeStruct((HEADS, n, 8, t), F32)],
        compiler_params=_params(("parallel", "parallel")),
    )(q, k, v, proj)


def _mla_gate_bwd(dy, o, proj, lse, *, tm):
    S = dy.shape[0]
    tm = min(tm, S)

    def body(dy_ref, o_ref, z_ref, lse_ref, do_ref, dz_ref, st_ref):
        sz, dsz = _silu_parts(z_ref[...].astype(F32))
        dyv, ov = dy_ref[...].astype(F32), o_ref[...]
        do = dyv * sz
        do_ref[...] = do.astype(BF16)
        dz_ref[...] = (dyv * ov * dsz).astype(BF16)
        prod = do * ov
        lane = lax.broadcasted_iota(jnp.int32, (tm, LANES), 1)
        for h in range(HEADS):
            delta = jnp.sum(prod[:, h * V_DIM:(h + 1) * V_DIM], axis=-1, keepdims=True)
            st_ref[h, 0] = lse_ref[h, 0] + jnp.where(lane == 1, delta, 0.0).T[0:8, :]

    blk = pl.BlockSpec((tm, D), lambda i: (i, 0))
    return pl.pallas_call(
        body, name="mla_gate_bwd", grid=(S // tm,),
        in_specs=[blk, blk, pl.BlockSpec((tm, D), lambda i: (i, COL_ZB)), pl.BlockSpec((HEADS, 1, 8, tm), lambda i: (0, i, 0, 0))],
        out_specs=[blk, blk, pl.BlockSpec((HEADS, 1, 8, tm), lambda i: (0, i, 0, 0))],
        out_shape=[jax.ShapeDtypeStruct((S, D), BF16), jax.ShapeDtypeStruct((S, D), BF16),
                   jax.ShapeDtypeStruct((HEADS, S // tm, 8, tm), F32)],
        compiler_params=_params(("parallel",)),
    )(dy, o, proj, lse)


def _mla_bwd(q, k, v, do, stats, *, t, rider):
    S = q.shape[0]
    t = min(t, S)
    n = S // t
    per = min(4, n)
    c2 = (QK_DIM ** -0.5) * LOG2E

    n_rin, n_rout = len(rider.ins), len(rider.out_shapes)

    def body(*refs):
        q_ref, k_ref, v_ref, do_ref, st_ref = refs[:5]
        rin = refs[5:5 + n_rin]
        dq_ref, dk_ref, dv_ref = refs[5 + n_rin:8 + n_rin]
        rout = refs[8 + n_rin:8 + n_rin + n_rout]
        ki = pl.program_id(1)
        step = pl.program_id(0) * n + ki
        _ride(rider._replace(phases=rider.phases[:1]), rin, rout, refs[-2:], step, HEADS * n)

        @pl.when(ki == 0)
        def _():
            dq_ref[...] = jnp.zeros_like(dq_ref)

        kv, vv = k_ref[...], v_ref[...]

        def block(i, carry, diag):
            dk, dv = carry
            rows = pl.ds(pl.multiple_of(i * t, t), t)
            qv, dov, st = q_ref[rows, :], do_ref[rows, :], st_ref[0, i]
            s = _dot(kv, qv, NT)
            if diag:
                key = lax.broadcasted_iota(jnp.int32, s.shape, 0)
                qry = lax.broadcasted_iota(jnp.int32, s.shape, 1)
                s = jnp.where(key <= qry, s, -1e30)
            p = jnp.exp2(s * c2 - st[0:1, :])
            p16 = p.astype(BF16)
            dv = dv + _dot(p16, dov, NN)
            dp = _dot(vv, dov, NT)
            ds = (p * (dp - st[1:2, :])).astype(BF16)
            dk = dk + _dot(ds, qv, NN)
            dq_ref[rows, :] += _dot(ds, kv, TN)
            return dk, dv

        carry = block(ki, (jnp.zeros((t, QK_PAD), F32), jnp.zeros((t, V_DIM), F32)), True)
        rest = n - 1 - ki

        def run(start, count):
            def f(cr):
                for u in range(count):
                    cr = block(start + u, cr, False)
                return cr
            return f

        carry = lax.switch(rest % per, [run(ki + 1, w) for w in range(per)], carry)
        first = ki + 1 + rest % per
        dk, dv = lax.fori_loop(0, rest // per, lambda i, cr: run(first + per * i, per)(cr), carry)
        dk_ref[...] = (dk * (QK_DIM ** -0.5)).astype(BF16)
        dv_ref[...] = dv.astype(BF16)
        _ride(rider._replace(phases=rider.phases[1:]), rin, rout, refs[-2:], step, HEADS * n)

    hbm = pl.BlockSpec(memory_space=pl.ANY)
    res = pl.pallas_call(
        body, name="mla_bwd", grid=(HEADS, n),
        in_specs=[pl.BlockSpec((S, QK_PAD), lambda h, j: (0, h)),
                  pl.BlockSpec((t, QK_PAD), lambda h, j: (j, h)),
                  pl.BlockSpec((t, V_DIM), lambda h, j: (j, h)),
                  pl.BlockSpec((S, V_DIM), lambda h, j: (0, h)),
                  pl.BlockSpec((1, n, 8, t), lambda h, j: (h, 0, 0, 0))] + [hbm] * n_rin,
        out_specs=[pl.BlockSpec((S, QK_PAD), lambda h, j: (0, h)),
                   pl.BlockSpec((t, QK_PAD), lambda h, j: (j, h)),
                   pl.BlockSpec((t, V_DIM), lambda h, j: (j, h))] + [hbm] * n_rout,
        out_shape=[jax.ShapeDtypeStruct((S, HEADS * QK_PAD), F32), jax.ShapeDtypeStruct((S, HEADS * QK_PAD), BF16),
                   jax.ShapeDtypeStruct((S, HEADS * V_DIM), BF16)] + list(rider.out_shapes),
        scratch_shapes=[pltpu.SemaphoreType.DMA((rider.n_sems,)), pltpu.SemaphoreType.DMA((rider.n_sems,))],
        compiler_params=_params(("arbitrary", "arbitrary")),
    )(q, k, v, do, stats, *rider.ins)
    return res[0], res[1], res[2], list(res[3:])


def _mla_qk_post(dq, dk, tabs, *, tm):
    S = dq.shape[0]
    tm = min(tm, S)
    scale = QK_DIM ** -0.5

    def body(dq_ref, dk_ref, c_ref, s1_ref, s2_ref, q16_ref, kr_ref):
        c, s1, s2 = c_ref[...], s1_ref[...], s2_ref[...]
        kpe = jnp.zeros((tm, LANES), F32)
        for h in range(HEADS):
            lo = h * QK_PAD
            q16_ref[:, lo:lo + QK_NOPE] = (dq_ref[:, lo:lo + QK_NOPE] * scale).astype(BF16)
            q16_ref[:, lo + QK_NOPE:lo + QK_PAD] = _rot(dq_ref[:, lo + QK_NOPE:lo + QK_PAD] * scale, c, s1, s2, -1).astype(BF16)
            kpe = kpe + dk_ref[:, lo + QK_NOPE:lo + QK_PAD].astype(F32)
        kr_ref[...] = _rot(kpe, c, s1, s2, -1).astype(BF16)

    wide = pl.BlockSpec((tm, HEADS * QK_PAD), lambda i: (i, 0))
    lane = pl.BlockSpec((tm, LANES), lambda i: (i, 0))
    return pl.pallas_call(
        body, name="mla_qk_post", grid=(S // tm,),
        in_specs=[wide, wide, lane, lane, lane], out_specs=[wide, lane],
        out_shape=[jax.ShapeDtypeStruct((S, HEADS * QK_PAD), BF16), jax.ShapeDtypeStruct((S, LANES), BF16)],
        compiler_params=_params(("parallel",)),
    )(dq, dk, *tabs)


def _mem_scores(q16, km_ref, h):
    lo = h * MEM_HEAD_DIM
    s = _dot(q16, km_ref[:, lo:lo + MEM_HEAD_DIM], NT) * (MEM_HEAD_DIM ** -0.5)
    e = jnp.exp(s - jnp.max(s, axis=-1, keepdims=True))
    return e / jnp.sum(e, axis=-1, keepdims=True)


def _mem_fwd(proj, kvm, *, tm):
    S = proj.shape[0]
    tm = min(tm, S)
    M = kvm.shape[0]

    def body(q_ref, z_ref, km_ref, vm_ref, y_ref):
        sz, _ = _silu_parts(z_ref[...].astype(F32))
        for h in range(MEM_HEADS):
            lo = h * MEM_HEAD_DIM
            p = _mem_scores(q_ref[:, lo:lo + MEM_HEAD_DIM].astype(BF16), km_ref, h)
            o = _dot(p.astype(BF16), vm_ref[:, lo:lo + MEM_HEAD_DIM], NN)
            y_ref[:, lo:lo + MEM_HEAD_DIM] = (o * sz[:, lo:lo + MEM_HEAD_DIM]).astype(BF16)

    return pl.pallas_call(
        body, name="mem_fwd", grid=(S // tm,),
        in_specs=[pl.BlockSpec((tm, D), lambda i: (i, COL_QM)), pl.BlockSpec((tm, D), lambda i: (i, COL_ZM)),
                  pl.BlockSpec((M, D), lambda i: (0, 0)), pl.BlockSpec((M, D), lambda i: (0, 1))],
        out_specs=pl.BlockSpec((tm, D), lambda i: (i, 0)),
        out_shape=jax.ShapeDtypeStruct((S, D), BF16),
        compiler_params=_params(("parallel",)),
    )(proj, proj, kvm, kvm)


def _mem_bwd(proj, kvm, dy, *, tm):
    S = proj.shape[0]
    tm = min(tm, S)
    M = kvm.shape[0]
    scale = MEM_HEAD_DIM ** -0.5

    def body(q_ref, z_ref, km_ref, vm_ref, dy_ref, dq_ref, dz_ref, dkv_ref):
        @pl.when(pl.program_id(0) == 0)
        def _():
            dkv_ref[...] = jnp.zeros_like(dkv_ref)

        sz, dsz = _silu_parts(z_ref[...].astype(F32))
        dyv = dy_ref[...].astype(F32)
        for h in range(MEM_HEADS):
            lo = h * MEM_HEAD_DIM
            sl = slice(lo, lo + MEM_HEAD_DIM)
            q16 = q_ref[:, sl].astype(BF16)
            p = _mem_scores(q16, km_ref, h)
            p16 = p.astype(BF16)
            o = _dot(p16, vm_ref[:, sl], NN)
            dy_h = dyv[:, sl]
            dz_ref[:, sl] = (dy_h * o * dsz[:, sl]).astype(BF16)
            do16 = (dy_h * sz[:, sl]).astype(BF16)
            dp = _dot(do16, vm_ref[:, sl], NT)
            ds = (p * (dp - jnp.sum(dp * p, axis=-1, keepdims=True)) * scale).astype(BF16)
            dq_ref[:, sl] = _dot(ds, km_ref[:, sl], NN).astype(BF16)
            dkv_ref[:, sl] += _dot(ds, q16, TN)
            dkv_ref[:, D + lo:D + lo + MEM_HEAD_DIM] += _dot(p16, do16, TN)

    blk = pl.BlockSpec((tm, D), lambda i: (i, 0))
    return pl.pallas_call(
        body, name="mem_bwd", grid=(S // tm,),
        in_specs=[pl.BlockSpec((tm, D), lambda i: (i, COL_QM)), pl.BlockSpec((tm, D), lambda i: (i, COL_ZM)),
                  pl.BlockSpec((M, D), lambda i: (0, 0)), pl.BlockSpec((M, D), lambda i: (0, 1)), blk],
        out_specs=[blk, blk, pl.BlockSpec((M, 2 * D), lambda i: (0, 0))],
        out_shape=[jax.ShapeDtypeStruct((S, D), BF16), jax.ShapeDtypeStruct((S, D), BF16),
                   jax.ShapeDtypeStruct((M, 2 * D), F32)],
        compiler_params=_params(("arbitrary",)),
    )(proj, proj, kvm, kvm, dy)


def _gmlp_common(u_ref, v_ref, lng_ref, lnb_ref):
    u, du = _gelu_parts(u_ref[...].astype(F32))
    vg, dvg = _gelu_parts(v_ref[...].astype(F32))
    mu = jnp.mean(vg, axis=-1, keepdims=True)
    vc = vg - mu
    r = lax.rsqrt(jnp.mean(vc * vc, axis=-1, keepdims=True) + EPS)
    vhat = vc * r
    vn = vhat * lng_ref[...] + lnb_ref[...]
    return u, du, dvg, r, vhat, vn.astype(BF16)


def _gmlp_fwd(proj, ln_g, ln_b, wm, bs_t):
    S = proj.shape[0]

    def body(u_ref, v_ref, z_ref, lng_ref, lnb_ref, wm_ref, bs_ref, y_ref):
        u, _, _, _, _, v16 = _gmlp_common(u_ref, v_ref, lng_ref, lnb_ref)
        sz, _ = _silu_parts(z_ref[...].astype(F32))
        for g in range(A_GROUPS):
            sl = slice(g * CHUNK, (g + 1) * CHUNK)
            sv = _dot(wm_ref[g], v16[:, sl], NN) + bs_ref[:, g:g + 1]
            y_ref[:, sl] = (u[:, sl] * sv * sz[:, sl]).astype(BF16)

    def col(c):
        return pl.BlockSpec((CHUNK, D), lambda i: (i, c))

    vec = pl.BlockSpec((1, D), lambda i: (0, 0))
    return pl.pallas_call(
        body, name="gmlp_fwd", grid=(S // CHUNK,),
        in_specs=[col(COL_U), col(COL_V), col(COL_ZA), vec, vec,
                  pl.BlockSpec((A_GROUPS, CHUNK, CHUNK), lambda i: (0, 0, 0)), pl.BlockSpec((CHUNK, A_GROUPS), lambda i: (0, 0))],
        out_specs=col(0), out_shape=jax.ShapeDtypeStruct((S, D), BF16),
        compiler_params=_params(("parallel",)),
    )(proj, proj, proj, ln_g, ln_b, wm, bs_t)


def _gmlp_bwd(proj, dy, ln_g, ln_b, wm, bs_t, others):
    S = proj.shape[0]

    def body(u_ref, v_ref, z_ref, dy_ref, lng_ref, lnb_ref, wm_ref, bs_ref, zb_ref, qm_ref, zm_ref, cq_ref, ckv_ref, kr_ref,
             dp_ref, gws_ref, dsv_ref, glg_ref, glb_ref, dvn_s):
        dp_ref[:, COL_ZB * D:(COL_ZB + 1) * D] = zb_ref[...]
        dp_ref[:, COL_QM * D:(COL_QM + 1) * D] = qm_ref[...]
        dp_ref[:, COL_ZM * D:(COL_ZM + 1) * D] = zm_ref[...]
        dp_ref[:, COL_CQ * LORA:(COL_CQ + 1) * LORA] = cq_ref[...]
        dp_ref[:, COL_CKV * LORA:(COL_CKV + 1) * LORA] = ckv_ref[...]
        dp_ref[:, COL_KR * LANES:(COL_KR + 1) * LANES] = kr_ref[...]

        @pl.when(pl.program_id(0) == 0)
        def _():
            gws_ref[...] = jnp.zeros_like(gws_ref)
            dsv_ref[...] = jnp.zeros_like(dsv_ref)
            glg_ref[...] = jnp.zeros_like(glg_ref)
            glb_ref[...] = jnp.zeros_like(glb_ref)

        u, du, dvg, r, vhat, v16 = _gmlp_common(u_ref, v_ref, lng_ref, lnb_ref)
        sz, dsz = _silu_parts(z_ref[...].astype(F32))
        dyv = dy_ref[...].astype(F32)
        for g in range(A_GROUPS):
            sl = slice(g * CHUNK, (g + 1) * CHUNK)
            sv = _dot(wm_ref[g], v16[:, sl], NN) + bs_ref[:, g:g + 1]
            dy_g, u_g, sz_g = dyv[:, sl], u[:, sl], sz[:, sl]
            dsv = dy_g * u_g * sz_g
            dp_ref[:, g * CHUNK:(g + 1) * CHUNK] = (dy_g * sv * sz_g * du[:, sl]).astype(BF16)
            dp_ref[:, COL_ZA * D + g * CHUNK:COL_ZA * D + (g + 1) * CHUNK] = (dy_g * u_g * sv * dsz[:, sl]).astype(BF16)
            dsv16 = dsv.astype(BF16)
            dvn_s[:, sl] = _dot(wm_ref[g], dsv16, TN)
            gws_ref[g] += _dot(dsv16, v16[:, sl], NT)
            dsv_ref[:, sl] += dsv
        dvn = dvn_s[...]
        glb_ref[...] += jnp.sum(dvn, axis=0, keepdims=True)
        glg_ref[...] += jnp.sum(dvn * vhat, axis=0, keepdims=True)
        dvh = dvn * lng_ref[...]
        dvc = r * (dvh - jnp.mean(dvh, axis=-1, keepdims=True) - vhat * jnp.mean(dvh * vhat, axis=-1, keepdims=True))
        dp_ref[:, COL_V * D:(COL_V + 1) * D] = (dvc * dvg).astype(BF16)

    def col(c):
        return pl.BlockSpec((CHUNK, D), lambda i: (i, c))

    def rows(width):
        return pl.BlockSpec((CHUNK, width), lambda i: (i, 0))

    vec = pl.BlockSpec((1, D), lambda i: (0, 0))
    wsp = pl.BlockSpec((A_GROUPS, CHUNK, CHUNK), lambda i: (0, 0, 0))
    return pl.pallas_call(
        body, name="gmlp_bwd", grid=(S // CHUNK,),
        in_specs=[col(COL_U), col(COL_V), col(COL_ZA), col(0), vec, vec, wsp, pl.BlockSpec((CHUNK, A_GROUPS), lambda i: (0, 0)),
                  rows(D), rows(D), rows(D), rows(LORA), rows(LORA), rows(LANES)],
        out_specs=[rows(IN_PAD), wsp, pl.BlockSpec((CHUNK, D), lambda i: (0, 0)), vec, vec],
        out_shape=[jax.ShapeDtypeStruct((S, IN_PAD), BF16),
                   jax.ShapeDtypeStruct((A_GROUPS, CHUNK, CHUNK), F32), jax.ShapeDtypeStruct((CHUNK, D), F32),
                   jax.ShapeDtypeStruct((1, D), F32), jax.ShapeDtypeStruct((1, D), F32)],
        scratch_shapes=[pltpu.VMEM((CHUNK, D), F32)],
        compiler_params=_params(("arbitrary",)),
    )(proj, proj, proj, dy, ln_g, ln_b, wm, bs_t, *others)


def _gate_merge(h16, ys, wg, bg, wbs, *, tm, tn):
    S = h16.shape[0]
    tm = min(tm, S)
    nj = D // tn

    def body(h_ref, ya_ref, yb_ref, ym_ref, wg0, wg1, wg2, bg0, bg1, bg2, wb0, wb1, wb2,
             mg_ref, g0_ref, g1_ref, g2_ref, p0_ref, p1_ref, p2_ref):
        hv = h_ref[...]
        acc = None
        for y_ref, wg_ref, bgr, wb_ref, g_ref, p_ref in ((ya_ref, wg0, bg0, wb0, g0_ref, p0_ref),
                                                         (yb_ref, wg1, bg1, wb1, g1_ref, p1_ref),
                                                         (ym_ref, wg2, bg2, wb2, g2_ref, p2_ref)):
            gate = _sigmoid(_dot(hv, wg_ref[...], NN) + bgr[...])
            p = _dot(y_ref[...], wb_ref[...], NN)
            g_ref[...] = gate.astype(BF16)
            p_ref[...] = p.astype(BF16)
            acc = gate * p if acc is None else acc + gate * p
        mg_ref[...] = acc.astype(BF16)

    a_spec = pl.BlockSpec((tm, D), lambda j, i: (i, 0))
    o_spec = pl.BlockSpec((tm, tn), lambda j, i: (i, j))

    def wgs(n):
        return pl.BlockSpec((D, tn), lambda j, i: (0, n * nj + j))

    def bgs(n):
        return pl.BlockSpec((1, tn), lambda j, i: (0, n * nj + j))

    wbsp = pl.BlockSpec((D, tn), lambda j, i: (0, j))
    return pl.pallas_call(
        body, name="gate_merge", grid=(nj, S // tm),
        in_specs=[a_spec] * 4 + [wgs(0), wgs(1), wgs(2), bgs(0), bgs(1), bgs(2), wbsp, wbsp, wbsp],
        out_specs=[o_spec] * 7, out_shape=[jax.ShapeDtypeStruct((S, D), BF16)] * 7,
        compiler_params=_params(("parallel", "parallel")),
    )(h16, *ys, wg, wg, wg, bg, bg, bg, *wbs)


def _gate_bwd(dmerged, gates, ps, *, tm):
    S = dmerged.shape[0]
    tm = min(tm, S)

    def body(dm_ref, g0, g1, g2, p0, p1, p2, dp0, dp1, dp2, dg_ref, gb_ref):
        @pl.when(pl.program_id(0) == 0)
        def _():
            gb_ref[...] = jnp.zeros_like(gb_ref)

        dm = dm_ref[...].astype(F32)
        for n, (g_ref, p_ref, dp_ref) in enumerate(((g0, p0, dp0), (g1, p1, dp1), (g2, p2, dp2))):
            gate = g_ref[...].astype(F32)
            dp_ref[...] = (dm * gate).astype(BF16)
            dg = dm * p_ref[...].astype(F32) * gate * (1.0 - gate)
            dg_ref[:, n * D:(n + 1) * D] = dg.astype(BF16)
            gb_ref[:, n * D:(n + 1) * D] += jnp.sum(dg, axis=0, keepdims=True)

    blk = pl.BlockSpec((tm, D), lambda i: (i, 0))
    return pl.pallas_call(
        body, name="gate_bwd", grid=(S // tm,),
        in_specs=[blk] * 7,
        out_specs=[blk, blk, blk, pl.BlockSpec((tm, 3 * D), lambda i: (i, 0)), pl.BlockSpec((1, 3 * D), lambda i: (0, 0))],
        out_shape=[jax.ShapeDtypeStruct((S, D), BF16)] * 3 + [jax.ShapeDtypeStruct((S, 3 * D), BF16),
                                                              jax.ShapeDtypeStruct((1, 3 * D), F32)],
        compiler_params=_params(("arbitrary",)),
    )(dmerged, *gates, *ps)


def _post_loss(x, out, target, g_post, *, tm):
    S = x.shape[0]
    tm = min(tm, S)

    def body(x_ref, o_ref, t_ref, g_ref, dy_ref, do_ref, gg_ref, ls_ref):
        @pl.when(pl.program_id(0) == 0)
        def _():
            gg_ref[...] = jnp.zeros_like(gg_ref)
            ls_ref[...] = jnp.zeros_like(ls_ref)

        ov = o_ref[...]
        r = lax.rsqrt(jnp.mean(ov * ov, axis=-1, keepdims=True) + EPS)
        n = ov * r
        err = (x_ref[...] + n * g_ref[...]) - t_ref[...]
        ls_ref[...] += 0.5 * jnp.sum(jnp.mean(err * err, axis=-1, keepdims=True))
        dy = err * (1.0 / D)
        dy_ref[...] = dy
        gg_ref[...] += jnp.sum(dy * n, axis=0, keepdims=True)
        gd = dy * g_ref[...]
        do_ref[...] = (r * (gd - n * jnp.mean(gd * n, axis=-1, keepdims=True))).astype(BF16)

    blk = pl.BlockSpec((tm, D), lambda i: (i, 0))
    vec = pl.BlockSpec((1, D), lambda i: (0, 0))
    return pl.pallas_call(
        body, name="post_loss", grid=(S // tm,),
        in_specs=[blk, blk, blk, vec],
        out_specs=[blk, blk, vec, pl.BlockSpec((1, LANES), lambda i: (0, 0))],
        out_shape=[jax.ShapeDtypeStruct((S, D), F32), jax.ShapeDtypeStruct((S, D), BF16),
                   jax.ShapeDtypeStruct((1, D), F32), jax.ShapeDtypeStruct((1, LANES), F32)],
        compiler_params=_params(("arbitrary",)),
    )(x, out, target, g_post)


def _adamw(w, g_own, g_other, m, v, cidx, *, name):
    rows, width = w.shape
    hh = rows // 2
    tr = _row_tile(hh, width, unit=8)
    nb = hh // tr

    def body(c_ref, w_ref, own_ref, oth_ref, m_ref, v_ref, g_ref, d_ref, nm_ref, nv_ref):
        mine = (pl.program_id(0) // nb) == c_ref[0]
        gv = jnp.where(mine, own_ref[...], oth_ref[...])
        delta, nm, nv = _adam_math(w_ref[...], gv, m_ref[...], v_ref[...])
        g_ref[...] = gv
        d_ref[...] = delta
        nm_ref[...] = nm
        nv_ref[...] = nv

    blk = pl.BlockSpec((tr, width), lambda i, c_ref: (i, 0))
    half = pl.BlockSpec((tr, width), lambda i, c_ref: (i % nb, 0))
    return pl.pallas_call(
        body, name=name,
        grid_spec=pltpu.PrefetchScalarGridSpec(num_scalar_prefetch=1, grid=(rows // tr,),
                                               in_specs=[blk, half, half, blk, blk], out_specs=[blk] * 4),
        out_shape=[jax.ShapeDtypeStruct((rows, width), F32)] * 4,
        compiler_params=_params(("parallel",)),
    )(cidx, w, g_own, g_other, m, v)


MESH = pl.DeviceIdType.MESH
ANY = pl.BlockSpec(memory_space=pl.ANY)


def _place():
    return lax.axis_index("x"), lax.axis_index("y"), lax.axis_index("c")


def _other_chips(x, y):
    return [(1 - x, y), (x, 1 - y), (1 - x, 1 - y)]


def _remote(src, dst, send_sem, recv_sem, dev):
    return pltpu.make_async_remote_copy(src_ref=src, dst_ref=dst, send_sem=send_sem, recv_sem=recv_sem,
                                        device_id=dev, device_id_type=MESH)


def _allgather_chips(shards):
    nw = len(shards)

    def body(*refs):
        x_refs, out_refs = refs[:nw], refs[nw:2 * nw]
        send_sems, recv_sems = refs[2 * nw:]
        x, y, c = _place()
        sibling = (x, y, 1 - c)
        chips = _other_chips(x, y)

        def half(w, px, py, hc):
            hh = shards[w].shape[0] // 2
            return out_refs[w].at[2 * px + py, pl.ds(hc * hh, hh), :]

        sent = []
        for w in range(nw):
            hh = shards[w].shape[0] // 2
            for k, (px, py) in enumerate(chips):
                cp = _remote(x_refs[w].at[pl.ds(c * hh, hh), :], half(w, x, y, c), send_sems.at[6 * w + k],
                             recv_sems.at[6 * w + k], (px, py, c))
                cp.start()
                sent.append(cp)
        for w in range(nw):
            for k, (px, py) in enumerate(chips):
                landed = half(w, px, py, c)
                _remote(landed, landed, send_sems.at[6 * w + k], recv_sems.at[6 * w + k], (px, py, c)).wait_recv()
                cp = _remote(landed, landed, send_sems.at[6 * w + 3 + k], recv_sems.at[6 * w + 3 + k], sibling)
                cp.start()
                sent.append(cp)
        for w in range(nw):
            for k, (px, py) in enumerate(chips):
                other = half(w, px, py, 1 - c)
                _remote(other, other, send_sems.at[6 * w + 3 + k], recv_sems.at[6 * w + 3 + k], sibling).wait_recv()
        for cp in sent:
            cp.wait_send()

    outs = pl.pallas_call(
        body, name="allgather_weights", in_specs=[ANY] * nw, out_specs=[ANY] * nw,
        out_shape=[jax.ShapeDtypeStruct((N_CHIPS,) + s.shape, s.dtype) for s in shards],
        scratch_shapes=[pltpu.SemaphoreType.DMA((6 * nw,)), pltpu.SemaphoreType.DMA((6 * nw,))],
    )(*shards)
    own = 2 * lax.axis_index("x") + lax.axis_index("y")
    return [lax.dynamic_update_slice(o, s[None], (own, 0, 0)) for o, s in zip(outs, shards)]


def _row_tile(rows, cols, unit=16, budget=2 * 1024 * 1024):
    best = unit
    for t in range(unit, rows + 1, unit):
        if rows % t == 0 and t * cols * 4 <= budget:
            best = t
    assert rows % best == 0, (rows, cols)
    return best


def _peers(x, y, c):
    out = []
    for k in range(1, N_DEV):
        out.append((k, (1 - x if (k >> 2) & 1 else x, 1 - y if (k >> 1) & 1 else y, 1 - c if k & 1 else c)))
    return out


def _gather_rider(shards):
    nw = len(shards)

    def half(outs, w, px, py, hc):
        hh = shards[w].shape[0] // 2
        return outs[w].at[2 * px + py, pl.ds(hc * hh, hh), :]

    def ici(ins, outs, ss, rs, w, k, px, py, c, x, y):
        hh = shards[w].shape[0] // 2
        return _remote(ins[w].at[pl.ds(c * hh, hh), :], half(outs, w, x, y, c), ss.at[6 * w + k], rs.at[6 * w + k], (px, py, c))

    def passing(outs, ss, rs, w, k, px, py, hc, sibling):
        landed = half(outs, w, px, py, hc)
        return _remote(landed, landed, ss.at[6 * w + 3 + k], rs.at[6 * w + 3 + k], sibling)

    def start(ins, outs, ss, rs):
        x, y, c = _place()
        for w in range(nw):
            for k, (px, py) in enumerate(_other_chips(x, y)):
                ici(ins, outs, ss, rs, w, k, px, py, c, x, y).start()

    def forward(ins, outs, ss, rs):
        x, y, c = _place()
        for w in range(nw):
            for k, (px, py) in enumerate(_other_chips(x, y)):
                landed = half(outs, w, px, py, c)
                _remote(landed, landed, ss.at[6 * w + k], rs.at[6 * w + k], (px, py, c)).wait_recv()
                passing(outs, ss, rs, w, k, px, py, c, (x, y, 1 - c)).start()

    def finish(ins, outs, ss, rs):
        x, y, c = _place()
        for w in range(nw):
            for k, (px, py) in enumerate(_other_chips(x, y)):
                passing(outs, ss, rs, w, k, px, py, 1 - c, (x, y, 1 - c)).wait_recv()
        for w in range(nw):
            for k, (px, py) in enumerate(_other_chips(x, y)):
                ici(ins, outs, ss, rs, w, k, px, py, c, x, y).wait_send()
                passing(outs, ss, rs, w, k, px, py, c, (x, y, 1 - c)).wait_send()

    return _Rider(ins=tuple(shards), out_shapes=tuple(jax.ShapeDtypeStruct((N_CHIPS,) + s.shape, s.dtype) for s in shards),
                  n_sems=6 * nw, phases=((0.0, start), (0.8, forward), (1.0, finish)))


def _own_blocks_in_place(gathered, shards):
    own = 2 * lax.axis_index("x") + lax.axis_index("y")
    return [lax.dynamic_update_slice(o, s[None], (own, 0, 0)) for o, s in zip(gathered, shards)]


def _exchange_rider(blocks):
    nw = len(blocks)

    def copy(ins, outs, ss, rs, w, k, peer):
        hh = blocks[w].shape[1] // 2
        px, py, pc = peer
        return _remote(ins[w].at[2 * px + py, pl.ds(pc * hh, hh), :], outs[w].at[k - 1], ss.at[7 * w + k - 1], rs.at[7 * w + k - 1], peer)

    def start(ins, outs, ss, rs):
        for w in range(nw):
            for k, peer in _peers(*_place()):
                copy(ins, outs, ss, rs, w, k, peer).start()

    def finish(ins, outs, ss, rs):
        for w in range(nw):
            for k, peer in _peers(*_place()):
                copy(ins, outs, ss, rs, w, k, peer).wait()

    return _Rider(ins=tuple(blocks),
                  out_shapes=tuple(jax.ShapeDtypeStruct((N_DEV - 1, b.shape[1] // 2, b.shape[2]), b.dtype) for b in blocks),
                  n_sems=7 * nw, phases=((0.0, start), (1.0, finish)))


def _reduce_add(own, recv, cidx, *, name):
    R, W = own.shape
    hh = R // 2
    tr = _row_tile(hh, W, budget=1024 * 1024)
    nb = hh // tr

    def body(c_ref, o_ref, r_ref, t_ref):
        s = o_ref[...]
        for k in range(N_DEV - 1):
            s = s + r_ref[k].astype(F32)
        t_ref[...] = s

    return pl.pallas_call(
        body, name=name,
        grid_spec=pltpu.PrefetchScalarGridSpec(
            num_scalar_prefetch=1, grid=(nb,),
            in_specs=[pl.BlockSpec((tr, W), lambda i, c_ref: (i + c_ref[0] * nb, 0)),
                      pl.BlockSpec((N_DEV - 1, tr, W), lambda i, c_ref: (0, i, 0))],
            out_specs=pl.BlockSpec((tr, W), lambda i, c_ref: (i, 0))),
        out_shape=jax.ShapeDtypeStruct((hh, W), F32),
        compiler_params=_params(("parallel",)),
    )(cidx, own, recv)


def _halves_exchange(ts):
    nw = len(ts)

    def body(*refs):
        t_refs, out_refs = refs[:nw], refs[nw:2 * nw]
        send_sems, recv_sems = refs[2 * nw:]
        x, y, c = _place()
        cps = []
        for w in range(nw):
            cp = _remote(t_refs[w], out_refs[w], send_sems.at[w], recv_sems.at[w], (x, y, 1 - c))
            cp.start()
            cps.append(cp)
        for cp in cps:
            cp.wait()

    return pl.pallas_call(
        body, name="grad_halves_exchange", in_specs=[ANY] * nw, out_specs=[ANY] * nw,
        out_shape=[jax.ShapeDtypeStruct(t.shape, t.dtype) for t in ts],
        scratch_shapes=[pltpu.SemaphoreType.DMA((nw,)), pltpu.SemaphoreType.DMA((nw,))],
    )(*ts)


def _adam_math(w, g, m, v):
    nm = ADAM_B1 * m + (1.0 - ADAM_B1) * g
    nv = ADAM_B2 * v + (1.0 - ADAM_B2) * (g * g)
    c1 = 1.0 - ADAM_B1 ** ADAM_STEP
    c2 = 1.0 - ADAM_B2 ** ADAM_STEP
    return -ADAM_LR * ((nm / c1) / (jnp.sqrt(nv / c2) + ADAM_EPS) + ADAM_WD * w), nm, nv


STAGE_ROWS = 32
STAGE_VEC = {"g_pre": 0, "a_ln_g": 1, "a_ln_b": 2, "mem_norm_g": 3, "g_post": 4}
STAGE_BGATE = 5
STAGE_MIX = 8
STAGE_ABS = 16


def _small_step(g, loss_row, w, m, v):
    n = len(SMALL)

    def reduce_body(*refs):
        g_r = dict(zip(SMALL, refs[:n]))
        loss_r = refs[n]
        sa_o, sw_o = refs[n + 1], refs[n + 2]
        stage, ga, gw, aws16, send_sems, recv_sems = refs[n + 3:]

        stage[...] = jnp.zeros_like(stage)
        for name, row in STAGE_VEC.items():
            stage[row:row + 1, :] = g_r[name][...]
        for t in range(3):
            stage[STAGE_BGATE + t:STAGE_BGATE + t + 1, :] = g_r["b_gate"][:, t * D:(t + 1) * D]
        stage[STAGE_MIX:STAGE_MIX + 1, 0:LORA] = g_r["q_norm_g"][...]
        stage[STAGE_MIX:STAGE_MIX + 1, LORA:2 * LORA] = g_r["kv_norm_g"][...]
        stage[STAGE_MIX:STAGE_MIX + 1, 2 * LORA:2 * LORA + LANES] = loss_r[...]
        stage[STAGE_ABS:STAGE_ABS + A_GROUPS, 0:CHUNK] = g_r["a_b_s"][...]

        x, y, c = _place()
        me = 4 * x + 2 * y + c
        ga[me] = stage[...]
        aws16[...] = g_r["a_w_s"][...].astype(BF16)
        gw[me] = aws16[...]
        cps, srcs = [], []
        for k in range(1, N_DEV):
            fx, fy, fc = (k >> 2) & 1, (k >> 1) & 1, k & 1
            peer = (1 - x if fx else x, 1 - y if fy else y, 1 - c if fc else c)
            for j, (src, dst) in enumerate(((stage, ga), (aws16, gw))):
                cp = _remote(src, dst.at[me], send_sems.at[2 * (k - 1) + j], recv_sems.at[2 * (k - 1) + j], peer)
                cp.start()
                cps.append(cp)
            srcs.append(4 * peer[0] + 2 * peer[1] + peer[2])
        for k, src in enumerate(srcs):
            _remote(stage, ga.at[src], send_sems.at[2 * k], recv_sems.at[2 * k], (x, y, c)).wait_recv()
            _remote(aws16, gw.at[src], send_sems.at[2 * k + 1], recv_sems.at[2 * k + 1], (x, y, c)).wait_recv()
        for cp in cps:
            cp.wait_send()
        sa, sw = ga[0], gw[0].astype(F32)
        for d in range(1, N_DEV):
            sa = sa + ga[d]
            sw = sw + gw[d].astype(F32)
        sa_o[...] = sa
        sw_o[...] = sw

    def update_body(*refs):
        sa, sw = refs[0][...], refs[1][...]
        w_r = dict(zip(SMALL, refs[2:n + 2]))
        m_r = dict(zip(SMALL, refs[n + 2:2 * n + 2]))
        v_r = dict(zip(SMALL, refs[2 * n + 2:3 * n + 2]))
        outs = refs[3 * n + 2:7 * n + 2]
        o_r = {name: outs[4 * i:4 * i + 4] for i, name in enumerate(SMALL)}
        loss_o = refs[7 * n + 2]

        def update(name, gsum, cols=None):
            sel = (slice(None), cols) if cols is not None else Ellipsis
            delta, nm, nv = _adam_math(w_r[name][sel], gsum, m_r[name][sel], v_r[name][sel])
            for ref, val in zip(o_r[name], (gsum, delta, nm, nv)):
                ref[sel] = val

        for name, row in STAGE_VEC.items():
            update(name, sa[row:row + 1, :])
        for t in range(3):
            update("b_gate", sa[STAGE_BGATE + t:STAGE_BGATE + t + 1, :], slice(t * D, (t + 1) * D))
        update("q_norm_g", sa[STAGE_MIX:STAGE_MIX + 1, 0:LORA])
        update("kv_norm_g", sa[STAGE_MIX:STAGE_MIX + 1, LORA:2 * LORA])
        update("a_b_s", sa[STAGE_ABS:STAGE_ABS + A_GROUPS, 0:CHUNK])
        update("a_w_s", sw)
        loss_o[...] = sa[STAGE_MIX:STAGE_MIX + 1, 2 * LORA:2 * LORA + LANES]

    vm = pl.BlockSpec(memory_space=pltpu.VMEM)
    sa, sw = pl.pallas_call(
        reduce_body, name="small_allreduce", in_specs=[vm] * (n + 1), out_specs=[vm, vm],
        out_shape=[jax.ShapeDtypeStruct((STAGE_ROWS, D), F32), jax.ShapeDtypeStruct((A_GROUPS, CHUNK, CHUNK), F32)],
        scratch_shapes=[pltpu.VMEM((STAGE_ROWS, D), F32), pltpu.VMEM((N_DEV, STAGE_ROWS, D), F32),
                        pltpu.VMEM((N_DEV, A_GROUPS, CHUNK, CHUNK), BF16), pltpu.VMEM((A_GROUPS, CHUNK, CHUNK), BF16),
                        pltpu.SemaphoreType.DMA((2 * (N_DEV - 1),)), pltpu.SemaphoreType.DMA((2 * (N_DEV - 1),))],
        compiler_params=pltpu.CompilerParams(vmem_limit_bytes=VMEM_LIMIT),
    )(*[g[k] for k in SMALL], loss_row)
    ins = [sa, sw] + [w[k] for k in SMALL] + [m[k] for k in SMALL] + [v[k] for k in SMALL]
    out_shape = [jax.ShapeDtypeStruct(w[k].shape, F32) for k in SMALL for _ in range(4)] + [jax.ShapeDtypeStruct((1, LANES), F32)]
    res = pl.pallas_call(
        update_body, name="small_adamw", in_specs=[vm] * len(ins), out_specs=[vm] * len(out_shape), out_shape=out_shape,
        compiler_params=pltpu.CompilerParams(vmem_limit_bytes=VMEM_LIMIT),
    )(*ins)
    return {k: tuple(res[4 * i:4 * i + 4]) for i, k in enumerate(SMALL)}, res[-1]


SHARD_2D = {"w_in": (D, IN_REF // N_CHIPS), "w_uq": (LORA, HEADS * QK_DIM // N_CHIPS),
            "w_ukv": (LORA, HEADS * (QK_NOPE + V_DIM) // N_CHIPS), "w_mem_kv": (D, 2 * D // N_CHIPS),
            "w_gate": (D, 3 * D // N_CHIPS), "w_branch": (3 * D // N_CHIPS, D), "w_out": (D // N_CHIPS, D)}


def _cols(blocks):
    return jnp.concatenate([blocks[j] for j in range(N_CHIPS)], axis=1)


def _w_in_layout(gathered):
    w = _cols(gathered)
    return jnp.concatenate([w[:, :3 * D], w[:, 3 * D + 2 * LORA + QK_ROPE:], w[:, 3 * D:3 * D + 2 * LORA + QK_ROPE],
                            jnp.zeros((D, IN_PAD - IN_REF), w.dtype)], axis=1)


REST = BIG[1:]


def _rest_layouts(gathered):
    wq = jnp.pad(_cols(gathered["w_uq"]).reshape(LORA, HEADS, QK_DIM), ((0, 0), (0, 0), (0, QK_PAD - QK_DIM))).reshape(LORA, HEADS * QK_PAD)
    kv3 = _cols(gathered["w_ukv"]).reshape(LORA, HEADS, QK_NOPE + V_DIM)
    wk = jnp.pad(kv3[:, :, :QK_NOPE], ((0, 0), (0, 0), (0, QK_PAD - QK_NOPE))).reshape(LORA, HEADS * QK_PAD)
    wv = kv3[:, :, QK_NOPE:].reshape(LORA, HEADS * V_DIM)
    w_branch = gathered["w_branch"].reshape(N_CHIPS, 3, D // N_CHIPS, D).transpose(1, 0, 2, 3).reshape(3, D, D)
    return {"wq": wq, "wk": wk, "wv": wv, "w_mem_kv": _cols(gathered["w_mem_kv"]), "w_gate": _cols(gathered["w_gate"]),
            "w_branch": w_branch, "w_out": gathered["w_out"].reshape(D, D)}


def _grad_reference_layout(name, g):
    if name == "w_in":
        return jnp.concatenate([g[:, :3 * D], g[:, 6 * D:6 * D + 2 * LORA + QK_ROPE], g[:, 3 * D:6 * D]], axis=1)
    if name == "w_uq":
        return g.reshape(LORA, HEADS, QK_PAD)[:, :, :QK_DIM].reshape(LORA, HEADS * QK_DIM)
    if name == "w_ukv":
        gk, gv = g
        return jnp.concatenate([gk.reshape(LORA, HEADS, QK_PAD)[:, :, :QK_NOPE], gv.reshape(LORA, HEADS, V_DIM)],
                               axis=2).reshape(LORA, HEADS * (QK_NOPE + V_DIM))
    return g


def _grad_blocks(name, full):
    own = 2 * lax.axis_index("x") + lax.axis_index("y")
    R, C = SHARD_2D[name]
    if name == "w_branch":
        blocks = full.reshape(3, N_CHIPS, D // N_CHIPS, D).transpose(1, 0, 2, 3).reshape(N_CHIPS, R, C)
        mine = lax.dynamic_slice_in_dim(full, own * (D // N_CHIPS), D // N_CHIPS, axis=1).reshape(R, C)
    elif name == "w_out":
        blocks = full.reshape(N_CHIPS, R, C)
        mine = lax.dynamic_slice_in_dim(full, own * R, R, axis=0)
    else:
        blocks = full.reshape(R, N_CHIPS, C).transpose(1, 0, 2)
        mine = lax.dynamic_slice_in_dim(full, own * C, C, axis=1)
    return blocks.astype(BF16), mine


def _local_step(x, mem, pos_col, target, w_in, rest_shards, P):
    cidx = lax.axis_index("c").astype(jnp.int32).reshape(1)
    h16 = _rms_fwd(x, P["g_pre"], width=D, col=0, tm=256, name="pre_norm")
    memn16 = _rms_fwd(mem, P["mem_norm_g"], width=D, col=0, tm=256, name="mem_norm")
    proj, rest = _mm(h16, w_in, "nn", tm=1024, tn=1920, tk=D, out_dtype=BF16, name="in_proj", rider=_gather_rider(rest_shards))
    W = _rest_layouts(dict(zip(REST, _own_blocks_in_place(rest, rest_shards))))

    causal = jnp.tril(jnp.ones((CHUNK, CHUNK), F32))
    wm = (P["a_w_s"] * causal[None]).astype(BF16)
    bs_t = P["a_b_s"].T
    ya = _gmlp_fwd(proj, P["a_ln_g"], P["a_ln_b"], wm, bs_t)

    inv = 1.0 / (ROPE_THETA ** (jnp.arange(0, QK_ROPE, 2, dtype=F32) / QK_ROPE))
    inv_lane = jnp.concatenate([inv, inv, jnp.zeros((LANES - QK_ROPE,), F32)])[None]
    tabs = _rope_tables(pos_col, inv_lane, tm=1024)
    cqn = _rms_fwd(proj, P["q_norm_g"], width=LORA, col=COL_CQ, tm=512, name="q_norm")
    ckvn = _rms_fwd(proj, P["kv_norm_g"], width=LORA, col=COL_CKV, tm=512, name="kv_norm")
    q16, k16, v16 = _mla_proj(cqn, ckvn, proj, tabs, W["wq"], W["wk"], W["wv"], tm=256)
    o_b, yb, lse = _mla_fwd(q16, k16, v16, proj, t=512)

    kvm = _mm(memn16, W["w_mem_kv"], "nn", tm=256, tn=1024, tk=D, out_dtype=BF16, name="mem_kv")
    ym = _mem_fwd(proj, kvm, tm=512)

    wbs = [W["w_branch"][n] for n in range(3)]
    merged, g0, g1, g2, p0, p1, p2 = _gate_merge(h16, (ya, yb, ym), W["w_gate"], P["b_gate"], wbs, tm=512, tn=512)
    out = _mm(merged, W["w_out"], "nn", tm=512, tn=1024, tk=D, out_dtype=F32, name="out_proj")
    dy, dout, g_g_post, loss = _post_loss(x, out, target, P["g_post"], tm=256)

    full = {}
    full["w_out"] = _mm(merged, dout, "tn", tm=1024, tn=1024, tk=TN_TK, out_dtype=F32, name="gw_out")
    dmerged = _mm(dout, W["w_out"], "nt", tm=512, tn=1024, tk=D, out_dtype=BF16, name="d_merged")
    dp0, dp1, dp2, dgpre, g_b_gate = _gate_bwd(dmerged, (g0, g1, g2), (p0, p1, p2), tm=256)
    full["w_gate"] = _mm(h16, dgpre, "tn", tm=1024, tn=1024, tk=TN_TK, out_dtype=F32, name="gw_gate")
    dh_gate = _mm(dgpre, W["w_gate"], "nt", tm=1024, tn=1024, tk=3 * D // 2, out_dtype=F32, name="dh_gate")
    full["w_branch"] = jnp.stack([_mm(y, dp, "tn", tm=1024, tn=1024, tk=TN_TK, out_dtype=F32, name=f"gw_branch{n}")
                                  for n, (y, dp) in enumerate(((ya, dp0), (yb, dp1), (ym, dp2)))], axis=0)
    dya, dyb, dym = [_mm(dp, wbs[n], "nt", tm=512, tn=1024, tk=D, out_dtype=BF16, name=f"dy_branch{n}")
                     for n, dp in enumerate((dp0, dp1, dp2))]

    dqm, dzm, dkvm = _mem_bwd(proj, kvm, dym, tm=512)
    dkvm16 = dkvm.astype(BF16)
    full["w_mem_kv"] = _mm(memn16, dkvm16, "tn", tm=1024, tn=1024, tk=256, out_dtype=F32, name="gw_mem_kv")
    dmemn = _mm(dkvm16, W["w_mem_kv"], "nt", tm=256, tn=1024, tk=2 * D, out_dtype=F32, name="d_memn")
    _, g_mem_norm = _rms_bwd(dmemn, mem, P["mem_norm_g"], width=D, col=0, tm=256, out_dtype=BF16, name="mem_norm_bwd")

    own, recv = {}, {}
    early = ("w_out", "w_gate", "w_branch", "w_mem_kv")
    early_blocks = []
    for n in early:
        blocks, own[n] = _grad_blocks(n, full[n])
        early_blocks.append(blocks)
    do16, dzb, stats = _mla_gate_bwd(dyb, o_b, proj, lse, tm=512)
    dq, dk16, dv16, landed = _mla_bwd(q16, k16, v16, do16, stats, t=512, rider=_exchange_rider(early_blocks))
    recv.update(zip(early, landed))
    dq16, dkr = _mla_qk_post(dq, dk16, tabs, tm=256)
    g_wq = _mm(cqn, dq16, "tn", tm=512, tn=1024, tk=TN_TK, out_dtype=F32, name="gw_uq")
    g_wk = _mm(ckvn, dk16, "tn", tm=512, tn=1024, tk=TN_TK, out_dtype=F32, name="gw_uk")
    g_wv = _mm(ckvn, dv16, "tn", tm=512, tn=1024, tk=TN_TK, out_dtype=F32, name="gw_uv")
    dcqn = _mm(dq16, W["wq"], "nt", tm=512, tn=LORA, tk=HEADS * QK_PAD, out_dtype=F32, name="d_cqn")
    dckvn_k = _mm(dk16, W["wk"], "nt", tm=512, tn=LORA, tk=HEADS * QK_PAD, out_dtype=F32, name="d_ckvn_k")
    dckvn = _mm(dv16, W["wv"], "nt", tm=512, tn=LORA, tk=HEADS * V_DIM, out_dtype=F32, name="d_ckvn", add=dckvn_k)
    dcq, g_q_norm = _rms_bwd(dcqn, proj, P["q_norm_g"], width=LORA, col=COL_CQ, tm=512, out_dtype=BF16, name="q_norm_bwd")
    dckv, g_kv_norm = _rms_bwd(dckvn, proj, P["kv_norm_g"], width=LORA, col=COL_CKV, tm=512, out_dtype=BF16, name="kv_norm_bwd")

    dproj, gws, dsv_sum, g_ln_g, g_ln_b = _gmlp_bwd(proj, dya, P["a_ln_g"], P["a_ln_b"], wm, bs_t, (dzb, dqm, dzm, dcq, dckv, dkr))
    g_a_w_s = gws * causal[None]
    g_a_b_s = dsv_sum.reshape(CHUNK, A_GROUPS, CHUNK).sum(axis=-1).T

    mid = ("w_uq", "w_ukv")
    mid_blocks = []
    for n, g in (("w_uq", g_wq), ("w_ukv", (g_wk, g_wv))):
        blocks, own[n] = _grad_blocks(n, _grad_reference_layout(n, g))
        mid_blocks.append(blocks)
    g_w_in, landed = _mm(h16, dproj, "tn", tm=1024, tn=896, tk=TN_TK, out_dtype=F32, name="gw_in", rider=_exchange_rider(mid_blocks))
    recv.update(zip(mid, landed))
    in_blocks, own["w_in"] = _grad_blocks("w_in", _grad_reference_layout("w_in", g_w_in))
    dh, landed = _mm(dproj, w_in, "nt", tm=1024, tn=1024, tk=2688, out_dtype=F32, name="d_h", add=dh_gate, rider=_exchange_rider([in_blocks]))
    recv["w_in"] = landed[0]
    grad_x, g_g_pre = _rms_bwd(dh, x, P["g_pre"], width=D, col=0, tm=512, out_dtype=F32, name="pre_norm_bwd", residual=dy)

    totals = [_reduce_add(own[n], recv[n], cidx, name=f"grad_reduce_{n}") for n in BIG]
    small = {"g_pre": g_g_pre, "a_ln_g": g_ln_g, "a_ln_b": g_ln_b, "a_w_s": g_a_w_s, "a_b_s": g_a_b_s,
             "q_norm_g": g_q_norm, "kv_norm_g": g_kv_norm, "mem_norm_g": g_mem_norm, "b_gate": g_b_gate, "g_post": g_g_post}
    return loss, grad_x, totals, small


def kernel(x, mem, positions, g_pre, w_in, a_ln_g, a_ln_b, a_w_s, a_b_s, q_norm_g, w_uq, kv_norm_g, w_ukv, mem_norm_g, w_mem_kv, w_gate, b_gate, w_branch, w_out, g_post, loss_target, m_g_pre, m_w_in, m_a_ln_g, m_a_ln_b, m_a_w_s, m_a_b_s, m_q_norm_g, m_w_uq, m_kv_norm_g, m_w_ukv, m_mem_norm_g, m_w_mem_kv, m_w_gate, m_b_gate, m_w_branch, m_w_out, m_g_post, v_g_pre, v_w_in, v_a_ln_g, v_a_ln_b, v_a_w_s, v_a_b_s, v_q_norm_g, v_w_uq, v_kv_norm_g, v_w_ukv, v_mem_norm_g, v_w_mem_kv, v_w_gate, v_b_gate, v_w_branch, v_w_out, v_g_post):
    w = dict(g_pre=g_pre, w_in=w_in, a_ln_g=a_ln_g, a_ln_b=a_ln_b, a_w_s=a_w_s, a_b_s=a_b_s, q_norm_g=q_norm_g, w_uq=w_uq,
             kv_norm_g=kv_norm_g, w_ukv=w_ukv, mem_norm_g=mem_norm_g, w_mem_kv=w_mem_kv, w_gate=w_gate, b_gate=b_gate,
             w_branch=w_branch, w_out=w_out, g_post=g_post)
    m = dict(g_pre=m_g_pre, w_in=m_w_in, a_ln_g=m_a_ln_g, a_ln_b=m_a_ln_b, a_w_s=m_a_w_s, a_b_s=m_a_b_s, q_norm_g=m_q_norm_g,
             w_uq=m_w_uq, kv_norm_g=m_kv_norm_g, w_ukv=m_w_ukv, mem_norm_g=m_mem_norm_g, w_mem_kv=m_w_mem_kv, w_gate=m_w_gate,
             b_gate=m_b_gate, w_branch=m_w_branch, w_out=m_w_out, g_post=m_g_post)
    v = dict(g_pre=v_g_pre, w_in=v_w_in, a_ln_g=v_a_ln_g, a_ln_b=v_a_ln_b, a_w_s=v_a_w_s, a_b_s=v_a_b_s, q_norm_g=v_q_norm_g,
             w_uq=v_w_uq, kv_norm_g=v_kv_norm_g, w_ukv=v_w_ukv, mem_norm_g=v_mem_norm_g, w_mem_kv=v_w_mem_kv, w_gate=v_w_gate,
             b_gate=v_b_gate, w_branch=v_w_branch, w_out=v_w_out, g_post=v_g_post)

    def two_d(t, n):
        return t[n].reshape(SHARD_2D[n]) if n in SHARD_2D else t[n].reshape(t[n].shape[1:] if t[n].ndim > 2 else t[n].shape)

    shards = [two_d(w, n).astype(BF16) for n in BIG]
    w_in_full = _w_in_layout(_allgather_chips(shards[:1])[0])
    P = {n: two_d(w, n) for n in SMALL}

    S = x.shape[1]
    loss_row, grad_x, totals, small = _local_step(x[0], mem[0], positions.reshape(S, 1), loss_target[0], w_in_full, shards[1:], P)

    from_sibling = _halves_exchange(totals)
    cidx = lax.axis_index("c").astype(jnp.int32).reshape(1)
    res = {}
    for n, own, other in zip(BIG, totals, from_sibling):
        upd = _adamw(two_d(w, n), own, other, two_d(m, n), two_d(v, n), cidx, name=f"adamw_{n}")
        for key, t in zip(("grad", "delta", "new_m", "new_v"), upd):
            res[key, n] = t.reshape(w[n].shape)

    small_out, loss_sum = _small_step(small, loss_row, P, {n: two_d(m, n) for n in SMALL}, {n: two_d(v, n) for n in SMALL})
    for n in SMALL:
        for key, t in zip(("grad", "delta", "new_m", "new_v"), small_out[n]):
            res[key, n] = t.reshape(w[n].shape)
    loss = loss_sum[0, 0]

    outs = [loss, grad_x[None]]
    for key in ("grad", "delta", "new_m", "new_v"):
        outs += [res[key, n] for n in WEIGHTS]
    return tuple(outs)
```
